```python
import math
import jax, jax.numpy as jnp
from jax import lax
import numpy as np

D_MODEL = 1024
BATCH = 8
SEQ = 4096
DEPTH = 4

CHUNK = 64
Q_BLOCK = 128
N_A_LAYERS = DEPTH // 2
N_B_LAYERS = DEPTH - N_A_LAYERS
SSM_GROUP = 16
SSM_GROUPS = D_MODEL // SSM_GROUP
SSM_STATE = 64
DT_MIN = 0.001
DT_MAX = 0.1
N_HEADS = 16
QK_NOPE_DIM = 64
QK_ROPE_DIM = 32
V_HEAD_DIM = 64
Q_LORA_RANK = 256
KV_LORA_RANK = 256
ROPE_THETA = 10000.0
ATTN_SCALE = 1.0 / math.sqrt(QK_NOPE_DIM + QK_ROPE_DIM)
D_FF = ((8 * D_MODEL + 3 * 256 - 1) // (3 * 256)) * 256
EPS = 1e-6

kernel_name = "yoco_s5_mla_adaln_encoder"


def rms_norm(x, g):
    xf = x.astype(jnp.float32)
    y = xf * lax.rsqrt(jnp.mean(xf * xf, axis=-1, keepdims=True) + EPS)
    return (y * g.astype(jnp.float32)).astype(x.dtype)


def modulate(h, shift, scale):
    return h * (1.0 + scale[:, None, :]) + shift[:, None, :]


def rope_cos_sin(positions):
    inv = 1.0 / (ROPE_THETA ** (jnp.arange(0, QK_ROPE_DIM, 2, dtype=jnp.float32) / QK_ROPE_DIM))
    ang = positions.astype(jnp.float32)[..., None] * inv
    return jnp.cos(ang), jnp.sin(ang)


def apply_rope(x, cos, sin):
    shape = cos.shape[:2] + (1,) * (x.ndim - 3) + cos.shape[-1:]
    cos = cos.reshape(shape)
    sin = sin.reshape(shape)
    x1, x2 = jnp.split(x.astype(jnp.float32), 2, axis=-1)
    return jnp.concatenate([x1 * cos - x2 * sin, x1 * sin + x2 * cos], axis=-1).astype(x.dtype)


def _complex_linear_combine(e1, e2):
    a1r, a1i, b1r, b1i = e1
    a2r, a2i, b2r, b2i = e2
    ar = a1r * a2r - a1i * a2i
    ai = a1r * a2i + a1i * a2r
    br = a2r * b1r - a2i * b1i + b2r
    bi = a2r * b1i + a2i * b1r + b2i
    return (ar, ai, br, bi)


def s5_mixer(h, lam_re, lam_im, log_dt, b_re, b_im, c_re, c_im, d_skip, w_glu, b_glu):
    bsz, s_len, d = h.shape
    f32 = jnp.float32
    lr = lam_re.astype(f32)
    li = lam_im.astype(f32)
    dt = jnp.exp(log_dt.astype(f32))[:, None]
    mag = jnp.exp(lr * dt)
    ab_re = mag * jnp.cos(li * dt)
    ab_im = mag * jnp.sin(li * dt)
    den = lr * lr + li * li
    nr = ab_re - 1.0
    ni = ab_im
    f_re = (nr * lr + ni * li) / den
    f_im = (ni * lr - nr * li) / den
    br = b_re.astype(f32)
    bi = b_im.astype(f32)
    bb_re = f_re[..., None] * br - f_im[..., None] * bi
    bb_im = f_re[..., None] * bi + f_im[..., None] * br
    cr = c_re.astype(f32)
    ci = c_im.astype(f32)

    n_chunks = s_len // CHUNK
    u = h.astype(f32).reshape(bsz, n_chunks, CHUNK, SSM_GROUPS, SSM_GROUP).transpose(1, 0, 2, 3, 4)

    def chunk_step(carry, u_c):
        s_re, s_im = carry
        bu_re = jnp.einsum('blgp,gnp->blgn', u_c, bb_re)
        bu_im = jnp.einsum('blgp,gnp->blgn', u_c, bb_im)
        a_re = jnp.broadcast_to(ab_re, bu_re.shape)
        a_im = jnp.broadcast_to(ab_im, bu_im.shape)
        pa_re, pa_im, loc_re, loc_im = lax.associative_scan(
            _complex_linear_combine, (a_re, a_im, bu_re, bu_im), axis=1)
        st_re = loc_re + pa_re * s_re[:, None] - pa_im * s_im[:, None]
        st_im = loc_im + pa_re * s_im[:, None] + pa_im * s_re[:, None]
        y = jnp.einsum('blgn,gpn->blgp', st_re, cr) - jnp.einsum('blgn,gpn->blgp', st_im, ci)
        return (st_re[:, -1], st_im[:, -1]), y

    init = (jnp.zeros((bsz, SSM_GROUPS, SSM_STATE), f32), jnp.zeros((bsz, SSM_GROUPS, SSM_STATE), f32))
    _, y = lax.scan(chunk_step, init, u)
    y = y.transpose(1, 0, 2, 3, 4).reshape(bsz, s_len, d)
    y = (y + d_skip.astype(f32) * h.astype(f32)).astype(h.dtype)
    g = jax.nn.gelu(y)
    return g * jax.nn.sigmoid(g @ w_glu + b_glu)


def shared_mla_kv(hk, cos, sin, w_kv_a, kv_a_norm_g, w_kv_b, k_nope_norm_g, k_rope_norm_g):
    bsz, s_len, _ = hk.shape
    kv_a = hk @ w_kv_a
    c_kv, k_rope = jnp.split(kv_a, [KV_LORA_RANK], axis=-1)
    c_kv = rms_norm(c_kv, kv_a_norm_g)
    kv = (c_kv @ w_kv_b).reshape(bsz, s_len, N_HEADS, QK_NOPE_DIM + V_HEAD_DIM)
    k_nope, v = jnp.split(kv, [QK_NOPE_DIM], axis=-1)
    k_nope = rms_norm(k_nope, k_nope_norm_g)
    k_rope = apply_rope(rms_norm(k_rope, k_rope_norm_g), cos, sin)
    return k_nope, k_rope, v


def mla_attention(h, cos, sin, k_nope, k_rope, v, w_dq, q_norm_g, w_uq, q_nope_norm_g, q_rope_norm_g, w_o):
    bsz, s_len, _ = h.shape
    q = rms_norm(h @ w_dq, q_norm_g) @ w_uq
    q = q.reshape(bsz, s_len, N_HEADS, QK_NOPE_DIM + QK_ROPE_DIM)
    q_nope, q_rope = jnp.split(q, [QK_NOPE_DIM], axis=-1)
    q_nope = rms_norm(q_nope, q_nope_norm_g)
    q_rope = apply_rope(rms_norm(q_rope, q_rope_norm_g), cos, sin)
    n_blocks = s_len // Q_BLOCK
    qn_b = q_nope.reshape(bsz, n_blocks, Q_BLOCK, N_HEADS, QK_NOPE_DIM).transpose(1, 0, 2, 3, 4)
    qr_b = q_rope.reshape(bsz, n_blocks, Q_BLOCK, N_HEADS, QK_ROPE_DIM).transpose(1, 0, 2, 3, 4)
    key_chunk = jnp.arange(s_len) // CHUNK

    def block_attn(args):
        qn, qr, blk = args
        s = (jnp.einsum('bqhd,bkhd->bhqk', qn, k_nope, preferred_element_type=jnp.float32)
             + jnp.einsum('bqhr,bkr->bhqk', qr, k_rope, preferred_element_type=jnp.float32)) * ATTN_SCALE
        q_chunk = (blk * Q_BLOCK + jnp.arange(Q_BLOCK)) // CHUNK
        mask = q_chunk[:, None] >= key_chunk[None, :]
        s = jnp.where(mask[None, None], s, -1e30)
        p = jax.nn.softmax(s, axis=-1)
        return jnp.einsum('bhqk,bkhd->bqhd', p.astype(v.dtype), v)

    o = lax.map(block_attn, (qn_b, qr_b, jnp.arange(n_blocks)))
    o = o.transpose(1, 0, 2, 3, 4).reshape(bsz, s_len, N_HEADS * V_HEAD_DIM)
    return o @ w_o


def swiglu(h, w_gate, w_up, w_down):
    return (jax.nn.silu(h @ w_gate) * (h @ w_up)) @ w_down


def _fwd_setup_inputs(seed: int = 0) -> dict:
    key = jax.random.key(seed)
    ks = iter(jax.random.split(key, 48))
    f32 = jnp.float32
    D, F, G, N, P = D_MODEL, D_FF, SSM_GROUPS, SSM_STATE, SSM_GROUP
    NA, NB = N_A_LAYERS, N_B_LAYERS

    def nrm(shape, scale):
        return jax.random.normal(next(ks), shape, f32) * scale

    def gain(shape):
        return 1.0 + 0.02 * jax.random.normal(next(ks), shape, f32)

    x = jax.random.normal(next(ks), (BATCH, SEQ, D), f32)
    c = jax.random.normal(next(ks), (BATCH, D), f32)
    offsets = jax.random.randint(next(ks), (BATCH, 1), 0, 4096, dtype=jnp.int32)
    positions = offsets + jnp.arange(SEQ, dtype=jnp.int32)[None, :]

    n_idx = jnp.arange(N, dtype=f32)
    inputs = {
        "x": x, "c": c, "positions": positions,
        "ada_w": nrm((DEPTH, D, 6 * D), 0.5 * D ** -0.5),
        "ada_b": nrm((DEPTH, 6 * D), 0.02),
        "norm1_g": gain((DEPTH, D)),
        "norm2_g": gain((DEPTH, D)),
        "ffn_w_gate": nrm((DEPTH, D, F), D ** -0.5),
        "ffn_w_up": nrm((DEPTH, D, F), D ** -0.5),
        "ffn_w_down": nrm((DEPTH, F, D), F ** -0.5),
        "s5_lam_re": -0.5 + 0.01 * jax.random.normal(next(ks), (NA, G, N), f32),
        "s5_lam_im": math.pi * n_idx[None, None, :] + 0.01 * jax.random.normal(next(ks), (NA, G, N), f32),
        "s5_log_dt": jax.random.uniform(next(ks), (NA, G), f32, math.log(DT_MIN), math.log(DT_MAX)),
        "s5_b_re": nrm((NA, G, N, P), P ** -0.5),
        "s5_b_im": nrm((NA, G, N, P), P ** -0.5),
        "s5_c_re": nrm((NA, G, P, N), N ** -0.5),
        "s5_c_im": nrm((NA, G, P, N), N ** -0.5),
        "s5_d": nrm((NA, D), 1.0),
        "s5_w_glu": nrm((NA, D, D), D ** -0.5),
        "s5_b_glu": nrm((NA, D), 0.02),
        "kv_ada_w": nrm((D, 2 * D), 0.5 * D ** -0.5),
        "kv_ada_b": nrm((2 * D,), 0.02),
        "kv_norm_g": gain((D,)),
        "w_kv_a": nrm((D, KV_LORA_RANK + QK_ROPE_DIM), D ** -0.5),
        "kv_a_norm_g": gain((KV_LORA_RANK,)),
        "w_kv_b": nrm((KV_LORA_RANK, N_HEADS * (QK_NOPE_DIM + V_HEAD_DIM)), KV_LORA_RANK ** -0.5),
        "k_nope_norm_g": gain((QK_NOPE_DIM,)),
        "k_rope_norm_g": gain((QK_ROPE_DIM,)),
        "mla_w_dq": nrm((NB, D, Q_LORA_RANK), D ** -0.5),
        "mla_q_norm_g": gain((NB, Q_LORA_RANK)),
        "mla_w_uq": nrm((NB, Q_LORA_RANK, N_HEADS * (QK_NOPE_DIM + QK_ROPE_DIM)), Q_LORA_RANK ** -0.5),
        "mla_q_nope_norm_g": gain((NB, QK_NOPE_DIM)),
        "mla_q_rope_norm_g": gain((NB, QK_ROPE_DIM)),
        "mla_w_o": nrm((NB, N_HEADS * V_HEAD_DIM, D), (N_HEADS * V_HEAD_DIM) ** -0.5),
    }
    return inputs


def _fwd_reference(x, c, positions, ada_w, ada_b, norm1_g, norm2_g, ffn_w_gate, ffn_w_up, ffn_w_down,
              s5_lam_re, s5_lam_im, s5_log_dt, s5_b_re, s5_b_im, s5_c_re, s5_c_im, s5_d, s5_w_glu, s5_b_glu,
              kv_ada_w, kv_ada_b, kv_norm_g, w_kv_a, kv_a_norm_g, w_kv_b, k_nope_norm_g, k_rope_norm_g,
              mla_w_dq, mla_q_norm_g, mla_w_uq, mla_q_nope_norm_g, mla_q_rope_norm_g, mla_w_o):
    cos, sin = rope_cos_sin(positions)
    c_act = jax.nn.silu(c)
    k_nope = k_rope = v = None
    for l in range(DEPTH):
        shift1, scale1, gate1, shift2, scale2, gate2 = jnp.split(c_act @ ada_w[l] + ada_b[l], 6, axis=-1)
        if l == N_A_LAYERS:
            k_shift, k_scale = jnp.split(c_act @ kv_ada_w + kv_ada_b, 2, axis=-1)
            hk = modulate(rms_norm(x, kv_norm_g), k_shift, k_scale)
            k_nope, k_rope, v = shared_mla_kv(hk, cos, sin, w_kv_a, kv_a_norm_g, w_kv_b,
                                              k_nope_norm_g, k_rope_norm_g)
        h = modulate(rms_norm(x, norm1_g[l]), shift1, scale1)
        if l < N_A_LAYERS:
            mix = s5_mixer(h, s5_lam_re[l], s5_lam_im[l], s5_log_dt[l], s5_b_re[l], s5_b_im[l],
                           s5_c_re[l], s5_c_im[l], s5_d[l], s5_w_glu[l], s5_b_glu[l])
        else:
            j = l - N_A_LAYERS
            mix = mla_attention(h, cos, sin, k_nope, k_rope, v, mla_w_dq[j], mla_q_norm_g[j], mla_w_uq[j],
                                mla_q_nope_norm_g[j], mla_q_rope_norm_g[j], mla_w_o[j])
        x = x + gate1[:, None, :] * mix
        h = modulate(rms_norm(x, norm2_g[l]), shift2, scale2)
        x = x + gate2[:, None, :] * swiglu(h, ffn_w_gate[l], ffn_w_up[l], ffn_w_down[l])
    return x


import jax as _jax
import jax.numpy as _jnp

TWIN_FORMAT = 'train_step'
FWD_PARAMS = ['x', 'c', 'positions', 'ada_w', 'ada_b', 'norm1_g', 'norm2_g', 'ffn_w_gate', 'ffn_w_up', 'ffn_w_down', 's5_lam_re', 's5_lam_im', 's5_log_dt', 's5_b_re', 's5_b_im', 's5_c_re', 's5_c_im', 's5_d', 's5_w_glu', 's5_b_glu', 'kv_ada_w', 'kv_ada_b', 'kv_norm_g', 'w_kv_a', 'kv_a_norm_g', 'w_kv_b', 'k_nope_norm_g', 'k_rope_norm_g', 'mla_w_dq', 'mla_q_norm_g', 'mla_w_uq', 'mla_q_nope_norm_g', 'mla_q_rope_norm_g', 'mla_w_o']
TWIN_WEIGHTS = ['ada_w', 'ada_b', 'norm1_g', 'norm2_g', 'ffn_w_gate', 'ffn_w_up', 'ffn_w_down', 's5_lam_re', 's5_lam_im', 's5_log_dt', 's5_b_re', 's5_b_im', 's5_c_re', 's5_c_im', 's5_d', 's5_w_glu', 's5_b_glu', 'kv_ada_w', 'kv_ada_b', 'kv_norm_g', 'w_kv_a', 'kv_a_norm_g', 'w_kv_b', 'k_nope_norm_g', 'k_rope_norm_g', 'mla_w_dq', 'mla_q_norm_g', 'mla_w_uq', 'mla_q_nope_norm_g', 'mla_q_rope_norm_g', 'mla_w_o']
TWIN_DIFF_INPUT = 'x'
TWIN_INPUTS = ['x', 'c', 'positions', 'ada_w', 'ada_b', 'norm1_g', 'norm2_g', 'ffn_w_gate', 'ffn_w_up', 'ffn_w_down', 's5_lam_re', 's5_lam_im', 's5_log_dt', 's5_b_re', 's5_b_im', 's5_c_re', 's5_c_im', 's5_d', 's5_w_glu', 's5_b_glu', 'kv_ada_w', 'kv_ada_b', 'kv_norm_g', 'w_kv_a', 'kv_a_norm_g', 'w_kv_b', 'k_nope_norm_g', 'k_rope_norm_g', 'mla_w_dq', 'mla_q_norm_g', 'mla_w_uq', 'mla_q_nope_norm_g', 'mla_q_rope_norm_g', 'mla_w_o', 'loss_target', 'm_ada_w', 'm_ada_b', 'm_norm1_g', 'm_norm2_g', 'm_ffn_w_gate', 'm_ffn_w_up', 'm_ffn_w_down', 'm_s5_lam_re', 'm_s5_lam_im', 'm_s5_log_dt', 'm_s5_b_re', 'm_s5_b_im', 'm_s5_c_re', 'm_s5_c_im', 'm_s5_d', 'm_s5_w_glu', 'm_s5_b_glu', 'm_kv_ada_w', 'm_kv_ada_b', 'm_kv_norm_g', 'm_w_kv_a', 'm_kv_a_norm_g', 'm_w_kv_b', 'm_k_nope_norm_g', 'm_k_rope_norm_g', 'm_mla_w_dq', 'm_mla_q_norm_g', 'm_mla_w_uq', 'm_mla_q_nope_norm_g', 'm_mla_q_rope_norm_g', 'm_mla_w_o', 'v_ada_w', 'v_ada_b', 'v_norm1_g', 'v_norm2_g', 'v_ffn_w_gate', 'v_ffn_w_up', 'v_ffn_w_down', 'v_s5_lam_re', 'v_s5_lam_im', 'v_s5_log_dt', 'v_s5_b_re', 'v_s5_b_im', 'v_s5_c_re', 'v_s5_c_im', 'v_s5_d', 'v_s5_w_glu', 'v_s5_b_glu', 'v_kv_ada_w', 'v_kv_ada_b', 'v_kv_norm_g', 'v_w_kv_a', 'v_kv_a_norm_g', 'v_w_kv_b', 'v_k_nope_norm_g', 'v_k_rope_norm_g', 'v_mla_w_dq', 'v_mla_q_norm_g', 'v_mla_w_uq', 'v_mla_q_nope_norm_g', 'v_mla_q_rope_norm_g', 'v_mla_w_o']
TWIN_OUTPUTS = ['loss', 'grad_x', 'grad_ada_w', 'grad_ada_b', 'grad_norm1_g', 'grad_norm2_g', 'grad_ffn_w_gate', 'grad_ffn_w_up', 'grad_ffn_w_down', 'grad_s5_lam_re', 'grad_s5_lam_im', 'grad_s5_log_dt', 'grad_s5_b_re', 'grad_s5_b_im', 'grad_s5_c_re', 'grad_s5_c_im', 'grad_s5_d', 'grad_s5_w_glu', 'grad_s5_b_glu', 'grad_kv_ada_w', 'grad_kv_ada_b', 'grad_kv_norm_g', 'grad_w_kv_a', 'grad_kv_a_norm_g', 'grad_w_kv_b', 'grad_k_nope_norm_g', 'grad_k_rope_norm_g', 'grad_mla_w_dq', 'grad_mla_q_norm_g', 'grad_mla_w_uq', 'grad_mla_q_nope_norm_g', 'grad_mla_q_rope_norm_g', 'grad_mla_w_o', 'delta_ada_w', 'delta_ada_b', 'delta_norm1_g', 'delta_norm2_g', 'delta_ffn_w_gate', 'delta_ffn_w_up', 'delta_ffn_w_down', 'delta_s5_lam_re', 'delta_s5_lam_im', 'delta_s5_log_dt', 'delta_s5_b_re', 'delta_s5_b_im', 'delta_s5_c_re', 'delta_s5_c_im', 'delta_s5_d', 'delta_s5_w_glu', 'delta_s5_b_glu', 'delta_kv_ada_w', 'delta_kv_ada_b', 'delta_kv_norm_g', 'delta_w_kv_a', 'delta_kv_a_norm_g', 'delta_w_kv_b', 'delta_k_nope_norm_g', 'delta_k_rope_norm_g', 'delta_mla_w_dq', 'delta_mla_q_norm_g', 'delta_mla_w_uq', 'delta_mla_q_nope_norm_g', 'delta_mla_q_rope_norm_g', 'delta_mla_w_o', 'new_m_ada_w', 'new_m_ada_b', 'new_m_norm1_g', 'new_m_norm2_g', 'new_m_ffn_w_gate', 'new_m_ffn_w_up', 'new_m_ffn_w_down', 'new_m_s5_lam_re', 'new_m_s5_lam_im', 'new_m_s5_log_dt', 'new_m_s5_b_re', 'new_m_s5_b_im', 'new_m_s5_c_re', 'new_m_s5_c_im', 'new_m_s5_d', 'new_m_s5_w_glu', 'new_m_s5_b_glu', 'new_m_kv_ada_w', 'new_m_kv_ada_b', 'new_m_kv_norm_g', 'new_m_w_kv_a', 'new_m_kv_a_norm_g', 'new_m_w_kv_b', 'new_m_k_nope_norm_g', 'new_m_k_rope_norm_g', 'new_m_mla_w_dq', 'new_m_mla_q_norm_g', 'new_m_mla_w_uq', 'new_m_mla_q_nope_norm_g', 'new_m_mla_q_rope_norm_g', 'new_m_mla_w_o', 'new_v_ada_w', 'new_v_ada_b', 'new_v_norm1_g', 'new_v_norm2_g', 'new_v_ffn_w_gate', 'new_v_ffn_w_up', 'new_v_ffn_w_down', 'new_v_s5_lam_re', 'new_v_s5_lam_im', 'new_v_s5_log_dt', 'new_v_s5_b_re', 'new_v_s5_b_im', 'new_v_s5_c_re', 'new_v_s5_c_im', 'new_v_s5_d', 'new_v_s5_w_glu', 'new_v_s5_b_glu', 'new_v_kv_ada_w', 'new_v_kv_ada_b', 'new_v_kv_norm_g', 'new_v_w_kv_a', 'new_v_kv_a_norm_g', 'new_v_w_kv_b', 'new_v_k_nope_norm_g', 'new_v_k_rope_norm_g', 'new_v_mla_w_dq', 'new_v_mla_q_norm_g', 'new_v_mla_w_uq', 'new_v_mla_q_nope_norm_g', 'new_v_mla_q_rope_norm_g', 'new_v_mla_w_o']
TWIN_LEAF_KINDS = {'loss': 'loss', 'grad_x': 'grad_x', 'grad_ada_w': 'grad_w', 'grad_ada_b': 'grad_w', 'grad_norm1_g': 'grad_w', 'grad_norm2_g': 'grad_w', 'grad_ffn_w_gate': 'grad_w', 'grad_ffn_w_up': 'grad_w', 'grad_ffn_w_down': 'grad_w', 'grad_s5_lam_re': 'grad_w', 'grad_s5_lam_im': 'grad_w', 'grad_s5_log_dt': 'grad_w', 'grad_s5_b_re': 'grad_w', 'grad_s5_b_im': 'grad_w', 'grad_s5_c_re': 'grad_w', 'grad_s5_c_im': 'grad_w', 'grad_s5_d': 'grad_w', 'grad_s5_w_glu': 'grad_w', 'grad_s5_b_glu': 'grad_w', 'grad_kv_ada_w': 'grad_w', 'grad_kv_ada_b': 'grad_w', 'grad_kv_norm_g': 'grad_w', 'grad_w_kv_a': 'grad_w', 'grad_kv_a_norm_g': 'grad_w', 'grad_w_kv_b': 'grad_w', 'grad_k_nope_norm_g': 'grad_w', 'grad_k_rope_norm_g': 'grad_w', 'grad_mla_w_dq': 'grad_w', 'grad_mla_q_norm_g': 'grad_w', 'grad_mla_w_uq': 'grad_w', 'grad_mla_q_nope_norm_g': 'grad_w', 'grad_mla_q_rope_norm_g': 'grad_w', 'grad_mla_w_o': 'grad_w', 'delta_ada_w': 'delta_w', 'delta_ada_b': 'delta_w', 'delta_norm1_g': 'delta_w', 'delta_norm2_g': 'delta_w', 'delta_ffn_w_gate': 'delta_w', 'delta_ffn_w_up': 'delta_w', 'delta_ffn_w_down': 'delta_w', 'delta_s5_lam_re': 'delta_w', 'delta_s5_lam_im': 'delta_w', 'delta_s5_log_dt': 'delta_w', 'delta_s5_b_re': 'delta_w', 'delta_s5_b_im': 'delta_w', 'delta_s5_c_re': 'delta_w', 'delta_s5_c_im': 'delta_w', 'delta_s5_d': 'delta_w', 'delta_s5_w_glu': 'delta_w', 'delta_s5_b_glu': 'delta_w', 'delta_kv_ada_w': 'delta_w', 'delta_kv_ada_b': 'delta_w', 'delta_kv_norm_g': 'delta_w', 'delta_w_kv_a': 'delta_w', 'delta_kv_a_norm_g': 'delta_w', 'delta_w_kv_b': 'delta_w', 'delta_k_nope_norm_g': 'delta_w', 'delta_k_rope_norm_g': 'delta_w', 'delta_mla_w_dq': 'delta_w', 'delta_mla_q_norm_g': 'delta_w', 'delta_mla_w_uq': 'delta_w', 'delta_mla_q_nope_norm_g': 'delta_w', 'delta_mla_q_rope_norm_g': 'delta_w', 'delta_mla_w_o': 'delta_w', 'new_m_ada_w': 'new_m', 'new_m_ada_b': 'new_m', 'new_m_norm1_g': 'new_m', 'new_m_norm2_g': 'new_m', 'new_m_ffn_w_gate': 'new_m', 'new_m_ffn_w_up': 'new_m', 'new_m_ffn_w_down': 'new_m', 'new_m_s5_lam_re': 'new_m', 'new_m_s5_lam_im': 'new_m', 'new_m_s5_log_dt': 'new_m', 'new_m_s5_b_re': 'new_m', 'new_m_s5_b_im': 'new_m', 'new_m_s5_c_re': 'new_m', 'new_m_s5_c_im': 'new_m', 'new_m_s5_d': 'new_m', 'new_m_s5_w_glu': 'new_m', 'new_m_s5_b_glu': 'new_m', 'new_m_kv_ada_w': 'new_m', 'new_m_kv_ada_b': 'new_m', 'new_m_kv_norm_g': 'new_m', 'new_m_w_kv_a': 'new_m', 'new_m_kv_a_norm_g': 'new_m', 'new_m_w_kv_b': 'new_m', 'new_m_k_nope_norm_g': 'new_m', 'new_m_k_rope_norm_g': 'new_m', 'new_m_mla_w_dq': 'new_m', 'new_m_mla_q_norm_g': 'new_m', 'new_m_mla_w_uq': 'new_m', 'new_m_mla_q_nope_norm_g': 'new_m', 'new_m_mla_q_rope_norm_g': 'new_m', 'new_m_mla_w_o': 'new_m', 'new_v_ada_w': 'new_v', 'new_v_ada_b': 'new_v', 'new_v_norm1_g': 'new_v', 'new_v_norm2_g': 'new_v', 'new_v_ffn_w_gate': 'new_v', 'new_v_ffn_w_up': 'new_v', 'new_v_ffn_w_down': 'new_v', 'new_v_s5_lam_re': 'new_v', 'new_v_s5_lam_im': 'new_v', 'new_v_s5_log_dt': 'new_v', 'new_v_s5_b_re': 'new_v', 'new_v_s5_b_im': 'new_v', 'new_v_s5_c_re': 'new_v', 'new_v_s5_c_im': 'new_v', 'new_v_s5_d': 'new_v', 'new_v_s5_w_glu': 'new_v', 'new_v_s5_b_glu': 'new_v', 'new_v_kv_ada_w': 'new_v', 'new_v_kv_ada_b': 'new_v', 'new_v_kv_norm_g': 'new_v', 'new_v_w_kv_a': 'new_v', 'new_v_kv_a_norm_g': 'new_v', 'new_v_w_kv_b': 'new_v', 'new_v_k_nope_norm_g': 'new_v', 'new_v_k_rope_norm_g': 'new_v', 'new_v_mla_w_dq': 'new_v', 'new_v_mla_q_norm_g': 'new_v', 'new_v_mla_w_uq': 'new_v', 'new_v_mla_q_nope_norm_g': 'new_v', 'new_v_mla_q_rope_norm_g': 'new_v', 'new_v_mla_w_o': 'new_v'}


def _forward(args):
    return _fwd_reference(*[args[k] for k in FWD_PARAMS])


def _output_shape():
    out = _jax.eval_shape(lambda: _forward(_fwd_setup_inputs(0)))
    return out.shape, out.dtype

N_MICROBATCH = 1
ADAM_LR = 0.001
ADAM_B1 = 0.9
ADAM_B2 = 0.999
ADAM_EPS = 1e-08
ADAM_WD = 0.01
ADAM_STEP = 10
PER_EXAMPLE_BATCH_AXIS = {'x': 0, 'c': 0, 'positions': 0, 'loss_target': 0}
SHARED_INPUTS = []
_WEIGHT_DTYPES = {'ada_w': _jnp.float32, 'ada_b': _jnp.float32, 'norm1_g': _jnp.float32, 'norm2_g': _jnp.float32, 'ffn_w_gate': _jnp.float32, 'ffn_w_up': _jnp.float32, 'ffn_w_down': _jnp.float32, 's5_lam_re': _jnp.float32, 's5_lam_im': _jnp.float32, 's5_log_dt': _jnp.float32, 's5_b_re': _jnp.float32, 's5_b_im': _jnp.float32, 's5_c_re': _jnp.float32, 's5_c_im': _jnp.float32, 's5_d': _jnp.float32, 's5_w_glu': _jnp.float32, 's5_b_glu': _jnp.float32, 'kv_ada_w': _jnp.float32, 'kv_ada_b': _jnp.float32, 'kv_norm_g': _jnp.float32, 'w_kv_a': _jnp.float32, 'kv_a_norm_g': _jnp.float32, 'w_kv_b': _jnp.float32, 'k_nope_norm_g': _jnp.float32, 'k_rope_norm_g': _jnp.float32, 'mla_w_dq': _jnp.float32, 'mla_q_norm_g': _jnp.float32, 'mla_w_uq': _jnp.float32, 'mla_q_nope_norm_g': _jnp.float32, 'mla_q_rope_norm_g': _jnp.float32, 'mla_w_o': _jnp.float32}
MOMENT_SCALE = {'ada_w': 1.379591e+00, 'ada_b': 3.270729e+00, 'norm1_g': 1.472465e+00, 'norm2_g': 3.469978e+00, 'ffn_w_gate': 7.569298e-02, 'ffn_w_up': 6.471723e-02, 'ffn_w_down': 1.020358e-01, 's5_lam_re': 5.172014e-02, 's5_lam_im': 4.443197e-02, 's5_log_dt': 3.277676e+00, 's5_b_re': 2.643541e-02, 's5_b_im': 2.544768e-02, 's5_c_re': 5.223592e-02, 's5_c_im': 5.218685e-02, 's5_d': 1.796980e+00, 's5_w_glu': 3.594792e-01, 's5_b_glu': 8.665701e-01, 'kv_ada_w': 9.861164e-01, 'kv_ada_b': 1.856876e+00, 'kv_norm_g': 4.810481e-01, 'w_kv_a': 1.250653e+00, 'kv_a_norm_g': 2.890517e+00, 'w_kv_b': 3.374582e-01, 'k_nope_norm_g': 2.565305e-01, 'k_rope_norm_g': 1.837384e-01, 'mla_w_dq': 2.664536e-02, 'mla_q_norm_g': 2.802568e-02, 'mla_w_uq': 1.095091e-02, 'mla_q_nope_norm_g': 1.312520e-01, 'mla_q_rope_norm_g': 9.469945e-02, 'mla_w_o': 2.885566e-01}


def _to_microbatches(a, axis):
    t = _jnp.moveaxis(a, axis, 0)
    t = t.reshape((N_MICROBATCH, t.shape[0] // N_MICROBATCH) + t.shape[1:])
    return _jnp.moveaxis(t, 1, axis + 1)


def setup_inputs(seed: int = 0) -> dict:
    inp = _fwd_setup_inputs(seed)
    key = _jax.random.fold_in(_jax.random.key(seed), 7919)
    shape, _ = _output_shape()
    out = dict(inp)
    out["loss_target"] = _jax.random.normal(_jax.random.fold_in(key, 0), shape, _jnp.float32)
    for i, name in enumerate(TWIN_WEIGHTS):
        w = inp[name].astype(_jnp.float32)
        if MOMENT_SCALE is None:
            s = _jnp.sqrt(_jnp.mean(_jnp.square(w)) + 1e-30)
        else:
            s = MOMENT_SCALE[name]
        km, kv = _jax.random.split(_jax.random.fold_in(key, i + 1))
        out[name] = w
        out["m_" + name] = s * _jax.random.normal(km, w.shape, _jnp.float32)
        out["v_" + name] = (s * s) * _jax.random.uniform(kv, w.shape, _jnp.float32, 0.5, 1.5)
    if N_MICROBATCH > 1:
        for name, axis in PER_EXAMPLE_BATCH_AXIS.items():
            out[name] = _to_microbatches(out[name], axis)
    return {'x': out['x'], 'c': out['c'], 'positions': out['positions'], 'ada_w': out['ada_w'], 'ada_b': out['ada_b'], 'norm1_g': out['norm1_g'], 'norm2_g': out['norm2_g'], 'ffn_w_gate': out['ffn_w_gate'], 'ffn_w_up': out['ffn_w_up'], 'ffn_w_down': out['ffn_w_down'], 's5_lam_re': out['s5_lam_re'], 's5_lam_im': out['s5_lam_im'], 's5_log_dt': out['s5_log_dt'], 's5_b_re': out['s5_b_re'], 's5_b_im': out['s5_b_im'], 's5_c_re': out['s5_c_re'], 's5_c_im': out['s5_c_im'], 's5_d': out['s5_d'], 's5_w_glu': out['s5_w_glu'], 's5_b_glu': out['s5_b_glu'], 'kv_ada_w': out['kv_ada_w'], 'kv_ada_b': out['kv_ada_b'], 'kv_norm_g': out['kv_norm_g'], 'w_kv_a': out['w_kv_a'], 'kv_a_norm_g': out['kv_a_norm_g'], 'w_kv_b': out['w_kv_b'], 'k_nope_norm_g': out['k_nope_norm_g'], 'k_rope_norm_g': out['k_rope_norm_g'], 'mla_w_dq': out['mla_w_dq'], 'mla_q_norm_g': out['mla_q_norm_g'], 'mla_w_uq': out['mla_w_uq'], 'mla_q_nope_norm_g': out['mla_q_nope_norm_g'], 'mla_q_rope_norm_g': out['mla_q_rope_norm_g'], 'mla_w_o': out['mla_w_o'], 'loss_target': out['loss_target'], 'm_ada_w': out['m_ada_w'], 'm_ada_b': out['m_ada_b'], 'm_norm1_g': out['m_norm1_g'], 'm_norm2_g': out['m_norm2_g'], 'm_ffn_w_gate': out['m_ffn_w_gate'], 'm_ffn_w_up': out['m_ffn_w_up'], 'm_ffn_w_down': out['m_ffn_w_down'], 'm_s5_lam_re': out['m_s5_lam_re'], 'm_s5_lam_im': out['m_s5_lam_im'], 'm_s5_log_dt': out['m_s5_log_dt'], 'm_s5_b_re': out['m_s5_b_re'], 'm_s5_b_im': out['m_s5_b_im'], 'm_s5_c_re': out['m_s5_c_re'], 'm_s5_c_im': out['m_s5_c_im'], 'm_s5_d': out['m_s5_d'], 'm_s5_w_glu': out['m_s5_w_glu'], 'm_s5_b_glu': out['m_s5_b_glu'], 'm_kv_ada_w': out['m_kv_ada_w'], 'm_kv_ada_b': out['m_kv_ada_b'], 'm_kv_norm_g': out['m_kv_norm_g'], 'm_w_kv_a': out['m_w_kv_a'], 'm_kv_a_norm_g': out['m_kv_a_norm_g'], 'm_w_kv_b': out['m_w_kv_b'], 'm_k_nope_norm_g': out['m_k_nope_norm_g'], 'm_k_rope_norm_g': out['m_k_rope_norm_g'], 'm_mla_w_dq': out['m_mla_w_dq'], 'm_mla_q_norm_g': out['m_mla_q_norm_g'], 'm_mla_w_uq': out['m_mla_w_uq'], 'm_mla_q_nope_norm_g': out['m_mla_q_nope_norm_g'], 'm_mla_q_rope_norm_g': out['m_mla_q_rope_norm_g'], 'm_mla_w_o': out['m_mla_w_o'], 'v_ada_w': out['v_ada_w'], 'v_ada_b': out['v_ada_b'], 'v_norm1_g': out['v_norm1_g'], 'v_norm2_g': out['v_norm2_g'], 'v_ffn_w_gate': out['v_ffn_w_gate'], 'v_ffn_w_up': out['v_ffn_w_up'], 'v_ffn_w_down': out['v_ffn_w_down'], 'v_s5_lam_re': out['v_s5_lam_re'], 'v_s5_lam_im': out['v_s5_lam_im'], 'v_s5_log_dt': out['v_s5_log_dt'], 'v_s5_b_re': out['v_s5_b_re'], 'v_s5_b_im': out['v_s5_b_im'], 'v_s5_c_re': out['v_s5_c_re'], 'v_s5_c_im': out['v_s5_c_im'], 'v_s5_d': out['v_s5_d'], 'v_s5_w_glu': out['v_s5_w_glu'], 'v_s5_b_glu': out['v_s5_b_glu'], 'v_kv_ada_w': out['v_kv_ada_w'], 'v_kv_ada_b': out['v_kv_ada_b'], 'v_kv_norm_g': out['v_kv_norm_g'], 'v_w_kv_a': out['v_w_kv_a'], 'v_kv_a_norm_g': out['v_kv_a_norm_g'], 'v_w_kv_b': out['v_w_kv_b'], 'v_k_nope_norm_g': out['v_k_nope_norm_g'], 'v_k_rope_norm_g': out['v_k_rope_norm_g'], 'v_mla_w_dq': out['v_mla_w_dq'], 'v_mla_q_norm_g': out['v_mla_q_norm_g'], 'v_mla_w_uq': out['v_mla_w_uq'], 'v_mla_q_nope_norm_g': out['v_mla_q_nope_norm_g'], 'v_mla_q_rope_norm_g': out['v_mla_q_rope_norm_g'], 'v_mla_w_o': out['v_mla_w_o']}


def _loss(weights, diff, rest, loss_target):
    with _jax.named_scope("forward"):
        args = {**rest, TWIN_DIFF_INPUT: diff, **{k: w.astype(_WEIGHT_DTYPES[k]) for k, w in weights.items()}}
        y = _forward(args)
    with _jax.named_scope("loss_head"):
        err = _jnp.square(y.astype(_jnp.float32) - loss_target)
        return 0.5 * _jnp.sum(_jnp.mean(err, axis=-1)) if err.ndim else 0.5 * err


def _adamw(w, g, m, v):
    m = ADAM_B1 * m + (1.0 - ADAM_B1) * g
    v = ADAM_B2 * v + (1.0 - ADAM_B2) * _jnp.square(g)
    m_hat = m / (1.0 - ADAM_B1 ** ADAM_STEP)
    v_hat = v / (1.0 - ADAM_B2 ** ADAM_STEP)
    delta = -ADAM_LR * (m_hat / (_jnp.sqrt(v_hat) + ADAM_EPS) + ADAM_WD * w)
    return delta, m, v


def reference(x, c, positions, ada_w, ada_b, norm1_g, norm2_g, ffn_w_gate, ffn_w_up, ffn_w_down, s5_lam_re, s5_lam_im, s5_log_dt, s5_b_re, s5_b_im, s5_c_re, s5_c_im, s5_d, s5_w_glu, s5_b_glu, kv_ada_w, kv_ada_b, kv_norm_g, w_kv_a, kv_a_norm_g, w_kv_b, k_nope_norm_g, k_rope_norm_g, mla_w_dq, mla_q_norm_g, mla_w_uq, mla_q_nope_norm_g, mla_q_rope_norm_g, mla_w_o, loss_target, m_ada_w, m_ada_b, m_norm1_g, m_norm2_g, m_ffn_w_gate, m_ffn_w_up, m_ffn_w_down, m_s5_lam_re, m_s5_lam_im, m_s5_log_dt, m_s5_b_re, m_s5_b_im, m_s5_c_re, m_s5_c_im, m_s5_d, m_s5_w_glu, m_s5_b_glu, m_kv_ada_w, m_kv_ada_b, m_kv_norm_g, m_w_kv_a, m_kv_a_norm_g, m_w_kv_b, m_k_nope_norm_g, m_k_rope_norm_g, m_mla_w_dq, m_mla_q_norm_g, m_mla_w_uq, m_mla_q_nope_norm_g, m_mla_q_rope_norm_g, m_mla_w_o, v_ada_w, v_ada_b, v_norm1_g, v_norm2_g, v_ffn_w_gate, v_ffn_w_up, v_ffn_w_down, v_s5_lam_re, v_s5_lam_im, v_s5_log_dt, v_s5_b_re, v_s5_b_im, v_s5_c_re, v_s5_c_im, v_s5_d, v_s5_w_glu, v_s5_b_glu, v_kv_ada_w, v_kv_ada_b, v_kv_norm_g, v_w_kv_a, v_kv_a_norm_g, v_w_kv_b, v_k_nope_norm_g, v_k_rope_norm_g, v_mla_w_dq, v_mla_q_norm_g, v_mla_w_uq, v_mla_q_nope_norm_g, v_mla_q_rope_norm_g, v_mla_w_o):
    given = dict(x=x, c=c, positions=positions, ada_w=ada_w, ada_b=ada_b, norm1_g=norm1_g, norm2_g=norm2_g, ffn_w_gate=ffn_w_gate, ffn_w_up=ffn_w_up, ffn_w_down=ffn_w_down, s5_lam_re=s5_lam_re, s5_lam_im=s5_lam_im, s5_log_dt=s5_log_dt, s5_b_re=s5_b_re, s5_b_im=s5_b_im, s5_c_re=s5_c_re, s5_c_im=s5_c_im, s5_d=s5_d, s5_w_glu=s5_w_glu, s5_b_glu=s5_b_glu, kv_ada_w=kv_ada_w, kv_ada_b=kv_ada_b, kv_norm_g=kv_norm_g, w_kv_a=w_kv_a, kv_a_norm_g=kv_a_norm_g, w_kv_b=w_kv_b, k_nope_norm_g=k_nope_norm_g, k_rope_norm_g=k_rope_norm_g, mla_w_dq=mla_w_dq, mla_q_norm_g=mla_q_norm_g, mla_w_uq=mla_w_uq, mla_q_nope_norm_g=mla_q_nope_norm_g, mla_q_rope_norm_g=mla_q_rope_norm_g, mla_w_o=mla_w_o, loss_target=loss_target, m_ada_w=m_ada_w, m_ada_b=m_ada_b, m_norm1_g=m_norm1_g, m_norm2_g=m_norm2_g, m_ffn_w_gate=m_ffn_w_gate, m_ffn_w_up=m_ffn_w_up, m_ffn_w_down=m_ffn_w_down, m_s5_lam_re=m_s5_lam_re, m_s5_lam_im=m_s5_lam_im, m_s5_log_dt=m_s5_log_dt, m_s5_b_re=m_s5_b_re, m_s5_b_im=m_s5_b_im, m_s5_c_re=m_s5_c_re, m_s5_c_im=m_s5_c_im, m_s5_d=m_s5_d, m_s5_w_glu=m_s5_w_glu, m_s5_b_glu=m_s5_b_glu, m_kv_ada_w=m_kv_ada_w, m_kv_ada_b=m_kv_ada_b, m_kv_norm_g=m_kv_norm_g, m_w_kv_a=m_w_kv_a, m_kv_a_norm_g=m_kv_a_norm_g, m_w_kv_b=m_w_kv_b, m_k_nope_norm_g=m_k_nope_norm_g, m_k_rope_norm_g=m_k_rope_norm_g, m_mla_w_dq=m_mla_w_dq, m_mla_q_norm_g=m_mla_q_norm_g, m_mla_w_uq=m_mla_w_uq, m_mla_q_nope_norm_g=m_mla_q_nope_norm_g, m_mla_q_rope_norm_g=m_mla_q_rope_norm_g, m_mla_w_o=m_mla_w_o, v_ada_w=v_ada_w, v_ada_b=v_ada_b, v_norm1_g=v_norm1_g, v_norm2_g=v_norm2_g, v_ffn_w_gate=v_ffn_w_gate, v_ffn_w_up=v_ffn_w_up, v_ffn_w_down=v_ffn_w_down, v_s5_lam_re=v_s5_lam_re, v_s5_lam_im=v_s5_lam_im, v_s5_log_dt=v_s5_log_dt, v_s5_b_re=v_s5_b_re, v_s5_b_im=v_s5_b_im, v_s5_c_re=v_s5_c_re, v_s5_c_im=v_s5_c_im, v_s5_d=v_s5_d, v_s5_w_glu=v_s5_w_glu, v_s5_b_glu=v_s5_b_glu, v_kv_ada_w=v_kv_ada_w, v_kv_ada_b=v_kv_ada_b, v_kv_norm_g=v_kv_norm_g, v_w_kv_a=v_w_kv_a, v_kv_a_norm_g=v_kv_a_norm_g, v_w_kv_b=v_w_kv_b, v_k_nope_norm_g=v_k_nope_norm_g, v_k_rope_norm_g=v_k_rope_norm_g, v_mla_w_dq=v_mla_w_dq, v_mla_q_norm_g=v_mla_q_norm_g, v_mla_w_uq=v_mla_w_uq, v_mla_q_nope_norm_g=v_mla_q_nope_norm_g, v_mla_q_rope_norm_g=v_mla_q_rope_norm_g, v_mla_w_o=v_mla_w_o)
    weights = {n: given[n] for n in TWIN_WEIGHTS}
    shared = {n: given[n] for n in SHARED_INPUTS}
    per_example = {n: given[n] for n in ['x', 'c', 'positions']}
    grad_fn = _jax.value_and_grad(_loss, argnums=(0, 1))

    def one_microbatch(ex, loss_target):
        ex = dict(ex)
        diff = ex.pop(TWIN_DIFF_INPUT)
        return grad_fn(weights, diff, {**shared, **ex}, loss_target)

    if N_MICROBATCH == 1:
        loss, (grad_w, grad_x) = one_microbatch(per_example, given["loss_target"])
    else:
        def body(carry, xs):
            loss_sum, grad_sum = carry
            l_k, (gw_k, gx_k) = one_microbatch(xs[0], xs[1])
            with _jax.named_scope("update"):
                return (loss_sum + l_k, _jax.tree.map(_jnp.add, grad_sum, gw_k)), gx_k

        init = (_jnp.zeros((), _jnp.float32), _jax.tree.map(_jnp.zeros_like, weights))
        (loss, grad_w), grad_x = _jax.lax.scan(body, init, (per_example, given["loss_target"]))
    with _jax.named_scope("update"):
        delta_w, new_m, new_v = {}, {}, {}
        for n in TWIN_WEIGHTS:
            delta_w[n], new_m[n], new_v[n] = _adamw(weights[n], grad_w[n], given["m_" + n], given["v_" + n])
    return (loss, grad_x, *[grad_w[n] for n in TWIN_WEIGHTS], *[delta_w[n] for n in TWIN_WEIGHTS],
            *[new_m[n] for n in TWIN_WEIGHTS], *[new_v[n] for n in TWIN_WEIGHTS])
```

```python
import functools
import math

import numpy as np
import jax
import jax.numpy as jnp
from jax import lax
from jax.experimental import pallas as pl
from jax.experimental.pallas import tpu as pltpu

F32 = jnp.float32
_MXU = jnp.bfloat16
HI = lax.Precision.HIGHEST

D = 1024
DEPTH = 4
N_A = 2
FF = 2816
N_DEV = 8
G = 64
P = 16
N = 64
GB = 8
NBLK = G // GB
HALF = GB * N
H = 16
HP = H // 2
DN, DR, DV = 64, 32, 64
HD = 128
QL = 256
KVL = 256
CHUNK = 64
ROPE_THETA = 10000.0
ATTN_SCALE = 1.0 / math.sqrt(DN + DR)
EPS = 1e-6
ADAM_LR, ADAM_B1, ADAM_B2, ADAM_EPS, ADAM_WD, ADAM_STEP = 0.001, 0.9, 0.999, 1e-08, 0.01, 10
VMEM_LIMIT = 56 * 1024 * 1024
MESH = pl.DeviceIdType.MESH

TILE_ROW = 256
TILE_ATT = 256
TILE_SCAN = 256


def _params(n_grid):
    return pltpu.CompilerParams(dimension_semantics=("arbitrary",) * n_grid, vmem_limit_bytes=VMEM_LIMIT)


@jax.custom_vjp
def mm(a, w):
    return jnp.dot(a.astype(_MXU), w, preferred_element_type=F32)


def _mm_fwd(a, w):
    return mm(a, w), w


def _mm_bwd(w, g):
    da = lax.dot_general(g.astype(_MXU), w, (((1,), (1,)), ((), ())), preferred_element_type=F32)
    return da, jnp.zeros_like(w)


mm.defvjp(_mm_fwd, _mm_bwd)


def rms(x, g):
    return x * lax.rsqrt(jnp.mean(x * x, axis=-1, keepdims=True) + EPS) * g


def modulate(h, shift, scale):
    return h * (1.0 + scale) + shift


def _lane(n=HD):
    return lax.broadcasted_iota(jnp.int32, (1, n), 1)


def _rot_matrix():
    r = lax.broadcasted_iota(jnp.int32, (HD, HD), 0)
    c = lax.broadcasted_iota(jnp.int32, (HD, HD), 1)
    first = (c >= DN) & (c < DN + DR // 2) & (r == c + DR // 2)
    second = (c >= DN + DR // 2) & (c < DN + DR) & (r == c - DR // 2)
    return jnp.where(first, -1.0, jnp.where(second, 1.0, 0.0)).astype(F32)


def head_norm_rope(xh, g128, cosf, sinf, rot, with_nope):
    lane = _lane()
    m_n = lane < DN
    m_r = (lane >= DN) & (lane < DN + DR)
    sq = xh * xh
    inv_r = lax.rsqrt(jnp.sum(jnp.where(m_r, sq, 0.0), axis=-1, keepdims=True) / DR + EPS)
    if with_nope:
        inv_n = lax.rsqrt(jnp.sum(jnp.where(m_n, sq, 0.0), axis=-1, keepdims=True) / DN + EPS)
        inv = jnp.where(m_n, inv_n, jnp.where(m_r, inv_r, 0.0))
    else:
        inv = jnp.where(m_r, inv_r, 0.0)
    xg = xh * inv * g128
    return xg * cosf + jnp.dot(xg, rot, precision=HI, preferred_element_type=F32) * sinf


def seg_pre(x, g, sh, sc):
    return (modulate(rms(x, g), sh, sc),), ()


def seg_ffn(x, g, sh, sc, gt, t_g, t_u, t_d, wg, wu, wd):
    h = modulate(rms(x, g), sh, sc)
    gate = mm(h, wg) + t_g
    up = mm(h, wu) + t_u
    a = jax.nn.silu(gate) * up
    y = mm(a, wd) + t_d
    return (x + gt * y,), (h.astype(_MXU), a.astype(_MXU))


def seg_glu(x, y, gt, b, t_z, w):
    g = jax.nn.gelu(y)
    z = mm(g, w) + b + t_z
    return (x + gt * (g * jax.nn.sigmoid(z)),), (g.astype(_MXU),)


def seg_o(x, o, gt, t_o, w):
    return (x + gt * (mm(o, w) + t_o),), (o.astype(_MXU),)


def seg_q(x, g, sh, sc, qg, g128, t_l, t_q, cosf, sinf, wdq, wuq):
    h = modulate(rms(x, g), sh, sc)
    ql = mm(h, wdq) + t_l
    qn = rms(ql, qg)
    q = mm(qn, wuq) + t_q
    rot = _rot_matrix()
    heads = [head_norm_rope(q[:, HD * i:HD * (i + 1)], g128, cosf, sinf, rot, True) for i in range(H)]
    return (jnp.concatenate(heads, axis=1),), (h.astype(_MXU), qn.astype(_MXU))


def seg_kv(x, g, sh, sc, ag, gkn, gkr, t_a, t_k, t_v, cosf, sinf, wa, wkn, wv):
    hk = modulate(rms(x, g), sh, sc)
    kva = mm(hk, wa) + t_a
    ckv = rms(kva[:, :KVL], ag)
    kr = head_norm_rope(kva[:, KVL:KVL + HD], gkr, cosf, sinf, _rot_matrix(), False)
    kn = mm(ckv, wkn) + t_k
    v = mm(ckv, wv) + t_v
    heads = []
    for i in range(H):
        kh = kn[:, HD * i:HD * (i + 1)]
        inv = lax.rsqrt(jnp.sum(kh * kh, axis=-1, keepdims=True) / DN + EPS)
        heads.append(kh * inv * gkn + kr)
    return (jnp.concatenate(heads, axis=1), v), (hk.astype(_MXU), ckv.astype(_MXU))


def _row_call(name, body_fn, rows, fulls, out_rows, out_accs, tile):
    s = rows[0].shape[0]
    n_tiles = s // tile
    n_rows, n_fulls, n_or, n_oa = len(rows), len(fulls), len(out_rows), len(out_accs)

    def kern(*refs):
        i = pl.program_id(0)
        row_v = [r[...] for r in refs[:n_rows]]
        full_v = [r[...] for r in refs[n_rows:n_rows + n_fulls]]
        o_refs = refs[n_rows + n_fulls:]
        ro, ao = body_fn(row_v, full_v)
        for r, v in zip(o_refs[:n_or], ro):
            r[...] = v.astype(r.dtype)
        if n_oa:
            @pl.when(i == 0)
            def _():
                for r in o_refs[n_or:]:
                    r[...] = jnp.zeros(r.shape, r.dtype)
            for r, v in zip(o_refs[n_or:], ao):
                r[...] += v.astype(r.dtype)

    in_specs = [pl.BlockSpec((tile, a.shape[1]), lambda i: (i, 0)) for a in rows]
    for a in fulls:
        big = a.size * a.dtype.itemsize > (1 << 20)
        nd = a.ndim
        in_specs.append(pl.BlockSpec(a.shape, functools.partial(lambda i, nd_: (0,) * nd_, nd_=nd),
                                     **({"pipeline_mode": pl.Buffered(1)} if big else {})))
    out_shape = [jax.ShapeDtypeStruct((s, w), dt) for w, dt in out_rows]
    out_shape += [jax.ShapeDtypeStruct(shp, dt) for shp, dt in out_accs]
    out_specs = [pl.BlockSpec((tile, w), lambda i: (i, 0)) for w, _ in out_rows]
    out_specs += [pl.BlockSpec(shp, functools.partial(lambda i, nd_: (0,) * nd_, nd_=len(shp))) for shp, _ in out_accs]
    res = pl.pallas_call(kern, out_shape=out_shape, grid=(n_tiles,), in_specs=in_specs, out_specs=out_specs,
                         name=name, compiler_params=_params(1))(*rows, *fulls)
    return list(res)


def seg_forward(name, seg, rows, smalls, consts_rows, consts_full, out_widths, tile=TILE_ROW, tap_widths=()):
    n_r, n_s, n_cr = len(rows), len(smalls), len(consts_rows)

    def body(row_v, full_v):
        t = row_v[0].shape[0]
        taps = [jnp.zeros((t, w), F32) for w in tap_widths]
        outs, _ = seg(*row_v[:n_r], *full_v[:n_s], *taps, *row_v[n_r:], *full_v[n_s:])
        return outs, ()

    return _row_call(name, body, list(rows) + list(consts_rows), list(smalls) + list(consts_full),
                     out_widths, [], tile)


def seg_backward(name, seg, rows, smalls, consts_rows, consts_full, cots, tap_widths, aux_widths,
                 dx_add=None, tile=TILE_ROW):
    n_r, n_s, n_cr, n_c = len(rows), len(smalls), len(consts_rows), len(cots)
    has_add = dx_add is not None

    def body(row_v, full_v):
        t = row_v[0].shape[0]
        prim_rows = row_v[:n_r]
        c_rows = row_v[n_r:n_r + n_cr]
        cot_v = row_v[n_r + n_cr:n_r + n_cr + n_c]
        add_v = row_v[n_r + n_cr + n_c] if has_add else None
        small_v = full_v[:n_s]
        c_full = full_v[n_s:]
        taps = [jnp.zeros((t, w), F32) for w in tap_widths]

        def f(*args):
            return seg(*args, *c_rows, *c_full)

        _, vjp_fn, aux = jax.vjp(f, *prim_rows, *small_v, *taps, has_aux=True)
        grads = vjp_fn(tuple(c.astype(F32) for c in cot_v))
        d_rows = list(grads[:n_r])
        if has_add:
            d_rows[0] = d_rows[0] + add_v
        d_small = grads[n_r:n_r + n_s]
        d_taps = grads[n_r + n_s:]
        return d_rows + list(d_taps) + list(aux), [jnp.sum(g, axis=0, keepdims=True) if g.shape[0] != 1 else g
                                                   for g in d_small]

    all_rows = list(rows) + list(consts_rows) + list(cots) + ([dx_add] if has_add else [])
    out_rows = [(a.shape[1], F32) for a in rows] + [(w, _MXU) for w in tap_widths] + [(w, _MXU) for w in aux_widths]
    out_accs = [((1, a.shape[1]), F32) for a in smalls]
    res = _row_call(name, body, all_rows, list(smalls) + list(consts_full), out_rows, out_accs, tile)
    n_t, n_a = len(tap_widths), len(aux_widths)
    return res[:n_r], res[n_r:n_r + n_t], res[n_r + n_t:n_r + n_t + n_a], res[n_r + n_t + n_a:]


def _split(n):
    if n <= 1024:
        return n
    for t in (1408, 1024, 768, 512, 256, 128):
        if n % t == 0:
            return t
    raise ValueError(n)


def matmul_tn(name, a, b, out_dtype):
    s, k1 = a.shape
    _, k2 = b.shape
    tm, tn, ts = _split(k1), _split(k2), 512
    n_s = s // ts

    def kern(a_ref, b_ref, o_ref, acc_ref):
        k = pl.program_id(2)

        @pl.when(k == 0)
        def _():
            acc_ref[...] = jnp.zeros(acc_ref.shape, F32)

        acc_ref[...] += lax.dot_general(a_ref[...], b_ref[...], (((0,), (0,)), ((), ())),
                                        preferred_element_type=F32)

        @pl.when(k == n_s - 1)
        def _():
            o_ref[...] = acc_ref[...].astype(o_ref.dtype)

    return pl.pallas_call(
        kern, out_shape=jax.ShapeDtypeStruct((k1, k2), out_dtype), grid=(k1 // tm, k2 // tn, n_s),
        in_specs=[pl.BlockSpec((ts, tm), lambda i, j, k: (k, i)), pl.BlockSpec((ts, tn), lambda i, j, k: (k, j))],
        out_specs=pl.BlockSpec((tm, tn), lambda i, j, k: (i, j)),
        scratch_shapes=[pltpu.VMEM((tm, tn), F32)], name=name, compiler_params=_params(3))(a, b)


def small_matmul(name, a, w, tn=256):
    m, k = a.shape
    n = w.shape[1]

    def kern(a_ref, w_ref, o_ref):
        o_ref[...] = jnp.dot(a_ref[...].astype(_MXU), w_ref[...].astype(_MXU), preferred_element_type=F32)

    return pl.pallas_call(kern, out_shape=jax.ShapeDtypeStruct((m, n), F32), grid=(n // tn,),
                          in_specs=[pl.BlockSpec((m, k), lambda j: (0, 0)), pl.BlockSpec((k, tn), lambda j: (0, j))],
                          out_specs=pl.BlockSpec((m, tn), lambda j: (0, j)), name=name,
                          compiler_params=_params(1))(a, w)


def small_matmul_tn(name, a, b, tn=256):
    m, k = a.shape
    n = b.shape[1]

    def kern(a_ref, b_ref, o_ref):
        o_ref[...] = lax.dot_general(a_ref[...].astype(_MXU), b_ref[...].astype(_MXU), (((0,), (0,)), ((), ())),
                                     preferred_element_type=F32)

    return pl.pallas_call(kern, out_shape=jax.ShapeDtypeStruct((k, n), F32), grid=(n // tn,),
                          in_specs=[pl.BlockSpec((m, k), lambda j: (0, 0)), pl.BlockSpec((m, tn), lambda j: (0, j))],
                          out_specs=pl.BlockSpec((k, tn), lambda j: (0, j)), name=name,
                          compiler_params=_params(1))(a, b)


def _s5_prep_math(lam_re, lam_im, log_dt, b_re_t, b_im_t, expand):
    dt = jnp.dot(jnp.exp(log_dt), expand, precision=HI, preferred_element_type=F32)
    mag = jnp.exp(lam_re * dt)
    ab_re = mag * jnp.cos(lam_im * dt)
    ab_im = mag * jnp.sin(lam_im * dt)
    den = lam_re * lam_re + lam_im * lam_im
    nr = ab_re - 1.0
    ni = ab_im
    f_re = (nr * lam_re + ni * lam_im) / den
    f_im = (ni * lam_re - nr * lam_im) / den
    bb_re = f_re * b_re_t - f_im * b_im_t
    bb_im = f_re * b_im_t + f_im * b_re_t
    return ab_re, ab_im, bb_re, bb_im


def _whole(kern, name, out_shape, *args):
    return pl.pallas_call(kern, out_shape=out_shape, name=name,
                          compiler_params=pltpu.CompilerParams(vmem_limit_bytes=VMEM_LIMIT))(*args)


def s5_prep_fwd(name, lam_re, lam_im, log_dt, b_re_t, b_im_t, expand):
    def kern(a, b, c, d, e, f, o0, o1, o2, o3):
        r = _s5_prep_math(a[...], b[...], c[...], d[...], e[...], f[...])
        for o, v in zip((o0, o1, o2, o3), r):
            o[...] = v

    gn = lam_re.shape[1]
    shp = [jax.ShapeDtypeStruct((1, gn), F32)] * 2 + [jax.ShapeDtypeStruct((P, gn), F32)] * 2
    return _whole(kern, name, shp, lam_re, lam_im, log_dt, b_re_t, b_im_t, expand)


def s5_prep_bwd(name, lam_re, lam_im, log_dt, b_re_t, b_im_t, expand, cots):
    def kern(a, b, c, d, e, f, c0, c1, c2, c3, o0, o1, o2, o3, o4):
        ex = f[...]
        _, vjp_fn = jax.vjp(lambda *p: _s5_prep_math(*p, ex), a[...], b[...], c[...], d[...], e[...])
        g = vjp_fn((c0[...], c1[...], c2[...], c3[...]))
        for o, v in zip((o0, o1, o2, o3, o4), g):
            o[...] = v

    shp = [jax.ShapeDtypeStruct(a.shape, F32) for a in (lam_re, lam_im, log_dt, b_re_t, b_im_t)]
    return _whole(kern, name, shp, lam_re, lam_im, log_dt, b_re_t, b_im_t, expand, *cots)


def _cpowers(ar, ai):
    pw = [(ar, ai)]
    for _ in range(7):
        pr, pi = pw[-1]
        pw.append((pr * ar - pi * ai, pr * ai + pi * ar))
    return pw


def _row_select(row, values):
    out = jnp.broadcast_to(values[7], (8, values[7].shape[1]))
    for r in range(6, -1, -1):
        out = jnp.where(row == r, values[r], out)
    return out


def _scan_tables(ar, ai, reverse):
    pw = _cpowers(ar, ai)
    row = lax.broadcasted_iota(jnp.int32, (8, ar.shape[1]), 0)
    steps = []
    for d in (1, 2, 4):
        keep = (row <= 7 - d) if reverse else (row >= d)
        steps.append((jnp.where(keep, pw[d - 1][0], 0.0), jnp.where(keep, pw[d - 1][1], 0.0)))
    order = list(range(7, -1, -1)) if reverse else list(range(8))
    carry = (_row_select(row, [pw[i][0] for i in order]), _row_select(row, [pw[i][1] for i in order]))
    return steps, carry


def _tile_scan_fwd(xr, xi, cr, ci, steps, carry_m):
    for d, (mr, mi) in zip((1, 2, 4), steps):
        sr = pltpu.roll(xr, d, 0)
        si = pltpu.roll(xi, d, 0)
        xr, xi = xr + mr * sr - mi * si, xi + mr * si + mi * sr
    pr, pi = carry_m
    return xr + pr * cr - pi * ci, xi + pr * ci + pi * cr


def _tile_scan_rev(xr, xi, cr, ci, steps, carry_m):
    for d, (mr, mi) in zip((1, 2, 4), steps):
        sr = pltpu.roll(xr, 8 - d, 0)
        si = pltpu.roll(xi, 8 - d, 0)
        xr, xi = xr + mr * sr + mi * si, xi + mr * si - mi * sr
    pr, pi = carry_m
    return xr + pr * cr + pi * ci, xi + pr * ci - pi * cr


def _fwd_scan_block(buf, row0, n_tiles8, ar, ai, c0r, c0i):
    steps, carry_m = _scan_tables(ar, ai, False)

    def body(j, carry):
        cr, ci = carry
        r0 = pl.multiple_of(row0 + j * 8, 8)
        xr = buf[pl.ds(r0, 8), 0:HALF]
        xi = buf[pl.ds(r0, 8), HALF:2 * HALF]
        xr, xi = _tile_scan_fwd(xr, xi, cr, ci, steps, carry_m)
        buf[pl.ds(r0, 8), 0:HALF] = xr
        buf[pl.ds(r0, 8), HALF:2 * HALF] = xi
        return xr[7:8], xi[7:8]

    return lax.fori_loop(0, n_tiles8, body, (c0r, c0i))


def s5_scan_fwd(name, h, wb, wc, a_tab, dskip, tile=TILE_SCAN):
    s = h.shape[0]
    n_t = s // tile

    def kern(h_ref, wb_ref, wc_ref, a_ref, d_ref, y_ref, s0_ref, carry_ref, buf):
        i = pl.program_id(0)

        @pl.when(i == 0)
        def _():
            carry_ref[...] = jnp.zeros(carry_ref.shape, F32)

        s0_ref[0] = carry_ref[...]
        for k in range(NBLK):
            cols = slice(GB * P * k, GB * P * (k + 1))
            u = h_ref[:, cols]
            buf[...] = jnp.dot(u, wb_ref[k], precision=HI, preferred_element_type=F32)
            ar = a_ref[k, :, 0:HALF]
            ai = a_ref[k, :, HALF:2 * HALF]
            cr, ci = _fwd_scan_block(buf, 0, tile // 8, ar, ai, carry_ref[k:k + 1, 0:HALF],
                                     carry_ref[k:k + 1, HALF:2 * HALF])
            carry_ref[k:k + 1, 0:HALF] = cr
            carry_ref[k:k + 1, HALF:2 * HALF] = ci
            y_ref[:, cols] = jnp.dot(buf[...], wc_ref[k], precision=HI, preferred_element_type=F32) + d_ref[:, cols] * u

    full = lambda a: pl.BlockSpec(a.shape, functools.partial(lambda i, nd_: (0,) * nd_, nd_=a.ndim))
    return pl.pallas_call(
        kern,
        out_shape=[jax.ShapeDtypeStruct((s, D), F32), jax.ShapeDtypeStruct((n_t, NBLK, 2 * HALF), F32)],
        grid=(n_t,),
        in_specs=[pl.BlockSpec((tile, D), lambda i: (i, 0)), full(wb), full(wc), full(a_tab), full(dskip)],
        out_specs=[pl.BlockSpec((tile, D), lambda i: (i, 0)), pl.BlockSpec((1, NBLK, 2 * HALF), lambda i: (i, 0, 0))],
        scratch_shapes=[pltpu.VMEM((NBLK, 2 * HALF), F32), pltpu.VMEM((tile, 2 * HALF), F32)],
        name=name, compiler_params=_params(1))(h, wb, wc, a_tab, dskip)


def s5_scan_bwd(name, h, dy, s0, wb, wc, a_tab, dskip, tile=TILE_SCAN):
    s = h.shape[0]
    n_t = s // tile
    n8 = tile // 8

    def kern(h_ref, dy_ref, s0_ref, wb_ref, wc_ref, a_ref, d_ref, dh_ref, dwb_ref, dwc_ref, da_ref, dd_ref,
             lam_ref, sbuf, gbuf):
        i = pl.program_id(0)

        @pl.when(i == 0)
        def _():
            lam_ref[...] = jnp.zeros(lam_ref.shape, F32)
            dwb_ref[...] = jnp.zeros(dwb_ref.shape, F32)
            dwc_ref[...] = jnp.zeros(dwc_ref.shape, F32)
            da_ref[...] = jnp.zeros(da_ref.shape, F32)
            dd_ref[...] = jnp.zeros(dd_ref.shape, F32)

        for k in range(NBLK):
            cols = slice(GB * P * k, GB * P * (k + 1))
            u = h_ref[:, cols]
            dyk = dy_ref[:, cols]
            ar = a_ref[k, :, 0:HALF]
            ai = a_ref[k, :, HALF:2 * HALF]
            sbuf[0:8, :] = jnp.broadcast_to(s0_ref[0, k:k + 1, :], (8, 2 * HALF))
            sbuf[8:tile + 8, :] = jnp.dot(u, wb_ref[k], precision=HI, preferred_element_type=F32)
            _fwd_scan_block(sbuf, 8, n8, ar, ai, s0_ref[0, k:k + 1, 0:HALF], s0_ref[0, k:k + 1, HALF:2 * HALF])
            gbuf[...] = lax.dot_general(dyk, wc_ref[k], (((1,), (1,)), ((), ())), precision=HI,
                                        preferred_element_type=F32)
            dwc_ref[k] += lax.dot_general(sbuf[8:tile + 8, :], dyk, (((0,), (0,)), ((), ())), precision=HI,
                                          preferred_element_type=F32)
            steps, carry_m = _scan_tables(ar, ai, True)
            row = lax.broadcasted_iota(jnp.int32, (8, HALF), 0)

            def body(jj, carry):
                cr, ci, dar, dai = carry
                j = n8 - 1 - jj
                r0 = pl.multiple_of(j * 8, 8)
                xr = gbuf[pl.ds(r0, 8), 0:HALF]
                xi = gbuf[pl.ds(r0, 8), HALF:2 * HALF]
                xr, xi = _tile_scan_rev(xr, xi, cr, ci, steps, carry_m)
                gbuf[pl.ds(r0, 8), 0:HALF] = xr
                gbuf[pl.ds(r0, 8), HALF:2 * HALF] = xi
                r1 = pl.multiple_of(j * 8 + 8, 8)
                spr = jnp.where(row == 0, sbuf[pl.ds(r0, 8), 0:HALF][7:8],
                                pltpu.roll(sbuf[pl.ds(r1, 8), 0:HALF], 1, 0))
                spi = jnp.where(row == 0, sbuf[pl.ds(r0, 8), HALF:2 * HALF][7:8],
                                pltpu.roll(sbuf[pl.ds(r1, 8), HALF:2 * HALF], 1, 0))
                dar = dar + xr * spr + xi * spi
                dai = dai + xi * spr - xr * spi
                return xr[0:1], xi[0:1], dar, dai

            z8 = jnp.zeros((8, HALF), F32)
            cr, ci, dar, dai = lax.fori_loop(
                0, n8, body, (lam_ref[k:k + 1, 0:HALF], lam_ref[k:k + 1, HALF:2 * HALF], z8, z8))
            lam_ref[k:k + 1, 0:HALF] = cr
            lam_ref[k:k + 1, HALF:2 * HALF] = ci
            da_ref[k:k + 1, 0:HALF] += jnp.sum(dar, axis=0, keepdims=True)
            da_ref[k:k + 1, HALF:2 * HALF] += jnp.sum(dai, axis=0, keepdims=True)
            lam = gbuf[...]
            dwb_ref[k] += lax.dot_general(u, lam, (((0,), (0,)), ((), ())), precision=HI, preferred_element_type=F32)
            du = lax.dot_general(lam, wb_ref[k], (((1,), (1,)), ((), ())), precision=HI, preferred_element_type=F32)
            dh_ref[:, cols] = du + d_ref[:, cols] * dyk
            dd_ref[:, cols] += jnp.sum(dyk * u, axis=0, keepdims=True)

    full = lambda a: pl.BlockSpec(a.shape, functools.partial(lambda i, nd_: (0,) * nd_, nd_=a.ndim))
    fullo = lambda shp: pl.BlockSpec(shp, functools.partial(lambda i, nd_: (0,) * nd_, nd_=len(shp)))
    rev = lambda i: (n_t - 1 - i, 0)
    return pl.pallas_call(
        kern,
        out_shape=[jax.ShapeDtypeStruct((s, D), F32), jax.ShapeDtypeStruct(wb.shape, F32),
                   jax.ShapeDtypeStruct(wc.shape, F32), jax.ShapeDtypeStruct((NBLK, 2 * HALF), F32),
                   jax.ShapeDtypeStruct((1, D), F32)],
        grid=(n_t,),
        in_specs=[pl.BlockSpec((tile, D), rev), pl.BlockSpec((tile, D), rev),
                  pl.BlockSpec((1, NBLK, 2 * HALF), lambda i: (n_t - 1 - i, 0, 0)),
                  full(wb), full(wc), full(a_tab), full(dskip)],
        out_specs=[pl.BlockSpec((tile, D), rev), fullo(wb.shape), fullo(wc.shape), fullo((NBLK, 2 * HALF)),
                   fullo((1, D))],
        scratch_shapes=[pltpu.VMEM((NBLK, 2 * HALF), F32), pltpu.VMEM((tile + 8, 2 * HALF), F32),
                        pltpu.VMEM((tile, 2 * HALF), F32)],
        name=name, compiler_params=_params(1))(h, dy, s0, wb, wc, a_tab, dskip)


def _chunk_mask(q0, k0, tq, tk):
    r = (q0 + lax.broadcasted_iota(jnp.int32, (tq, tk), 0)) // CHUNK
    c = (k0 + lax.broadcasted_iota(jnp.int32, (tq, tk), 1)) // CHUNK
    return r >= c


def _head_lanes(j):
    lane = _lane(2 * DV)
    return (lane >= DV * j) & (lane < DV * (j + 1))


def _scores(q, kblk, q0, k0, tq, tk):
    s = lax.dot_general(q, kblk, (((1,), (1,)), ((), ())), preferred_element_type=F32) * ATTN_SCALE
    return jnp.where(_chunk_mask(q0, k0, tq, tk), s, -1e30)


def attn_fwd(name, q, k, v, t=TILE_ATT):
    s = q.shape[0]
    n_q = s // t

    def kern(q_ref, k_ref, v_ref, o_ref, lse_ref):
        qi = pl.program_id(1)
        q0 = qi * t
        outs = []
        for j in range(2):
            qh = q_ref[:, HD * j:HD * (j + 1)]

            def body(kb, carry):
                m, l, acc = carry
                k0 = pl.multiple_of(kb * t, t)
                sc = _scores(qh, k_ref[pl.ds(k0, t), HD * j:HD * (j + 1)], q0, k0, t, t)
                m_new = jnp.maximum(m, jnp.max(sc, axis=-1, keepdims=True))
                p = jnp.exp(sc - m_new)
                alpha = jnp.exp(m - m_new)
                l = alpha * l + jnp.sum(p, axis=-1, keepdims=True)
                acc = alpha * acc + jnp.dot(p.astype(_MXU), v_ref[pl.ds(k0, t), :], preferred_element_type=F32)
                return m_new, l, acc

            m, l, acc = lax.fori_loop(0, qi + 1, body, (jnp.full((t, 1), -1e30, F32), jnp.zeros((t, 1), F32),
                                                        jnp.zeros((t, 2 * DV), F32)))
            outs.append(acc / l)
            lse_ref[0, j] = m + jnp.log(l)
        o_ref[...] = jnp.where(_head_lanes(0), outs[0], outs[1])

    return pl.pallas_call(
        kern,
        out_shape=[jax.ShapeDtypeStruct((s, H * DV), F32), jax.ShapeDtypeStruct((HP, 2, s, 1), F32)],
        grid=(HP, n_q),
        in_specs=[pl.BlockSpec((t, 2 * HD), lambda hp, i: (i, hp)), pl.BlockSpec((s, 2 * HD), lambda hp, i: (0, hp)),
                  pl.BlockSpec((s, 2 * DV), lambda hp, i: (0, hp))],
        out_specs=[pl.BlockSpec((t, 2 * DV), lambda hp, i: (i, hp)),
                   pl.BlockSpec((1, 2, t, 1), lambda hp, i: (hp, 0, i, 0))],
        name=name, compiler_params=_params(2))(q, k, v)


def attn_bwd_q(name, q, k, v, o, do, lse, t=TILE_ATT):
    s = q.shape[0]
    n_q = s // t

    def kern(q_ref, k_ref, v_ref, o_ref, do_ref, lse_ref, dq_ref, dl_ref):
        qi = pl.program_id(1)
        q0 = qi * t
        for j in range(2):
            qh = q_ref[:, HD * j:HD * (j + 1)]
            dom = jnp.where(_head_lanes(j), do_ref[...], 0.0)
            delta = jnp.sum(dom * o_ref[...], axis=-1, keepdims=True)
            dl_ref[0, j] = delta
            lse_j = lse_ref[0, j]
            dom_b = dom.astype(_MXU)

            def body(kb, dq):
                k0 = pl.multiple_of(kb * t, t)
                kblk = k_ref[pl.ds(k0, t), HD * j:HD * (j + 1)]
                p = jnp.exp(_scores(qh, kblk, q0, k0, t, t) - lse_j)
                dp = lax.dot_general(dom_b, v_ref[pl.ds(k0, t), :], (((1,), (1,)), ((), ())),
                                     preferred_element_type=F32)
                ds = p * (dp - delta) * ATTN_SCALE
                return dq + jnp.dot(ds.astype(_MXU), kblk, preferred_element_type=F32)

            dq_ref[:, HD * j:HD * (j + 1)] = lax.fori_loop(0, qi + 1, body, jnp.zeros((t, HD), F32))

    return pl.pallas_call(
        kern,
        out_shape=[jax.ShapeDtypeStruct((s, H * HD), F32), jax.ShapeDtypeStruct((HP, 2, s, 1), F32)],
        grid=(HP, n_q),
        in_specs=[pl.BlockSpec((t, 2 * HD), lambda hp, i: (i, hp)), pl.BlockSpec((s, 2 * HD), lambda hp, i: (0, hp)),
                  pl.BlockSpec((s, 2 * DV), lambda hp, i: (0, hp)), pl.BlockSpec((t, 2 * DV), lambda hp, i: (i, hp)),
                  pl.BlockSpec((t, 2 * DV), lambda hp, i: (i, hp)),
                  pl.BlockSpec((1, 2, t, 1), lambda hp, i: (hp, 0, i, 0))],
        out_specs=[pl.BlockSpec((t, 2 * HD), lambda hp, i: (i, hp)),
                   pl.BlockSpec((1, 2, t, 1), lambda hp, i: (hp, 0, i, 0))],
        name=name, compiler_params=_params(2))(q, k, v, o, do, lse)


def attn_bwd_kv(name, q, k, v, do, lse, delta, t=TILE_ATT):
    s = q.shape[0]
    n_k = s // t

    def kern(q_ref, k_ref, v_ref, do_ref, lse_ref, dl_ref, dk_ref, dv_ref):
        ki = pl.program_id(1)
        k0 = ki * t
        vblk = v_ref[...]
        dv_tot = jnp.zeros((t, 2 * DV), F32)
        for j in range(2):
            kh = k_ref[:, HD * j:HD * (j + 1)]

            def body(qb, carry):
                dk, dv = carry
                q0 = pl.multiple_of(qb * t, t)
                qblk = q_ref[pl.ds(q0, t), HD * j:HD * (j + 1)]
                p = jnp.exp(_scores(qblk, kh, q0, k0, t, t) - lse_ref[0, j, pl.ds(q0, t), :])
                dom = jnp.where(_head_lanes(j), do_ref[pl.ds(q0, t), :], 0.0).astype(_MXU)
                dv = dv + lax.dot_general(p.astype(_MXU), dom, (((0,), (0,)), ((), ())), preferred_element_type=F32)
                dp = lax.dot_general(dom, vblk, (((1,), (1,)), ((), ())), preferred_element_type=F32)
                ds = p * (dp - dl_ref[0, j, pl.ds(q0, t), :]) * ATTN_SCALE
                dk = dk + lax.dot_general(ds.astype(_MXU), qblk, (((0,), (0,)), ((), ())), preferred_element_type=F32)
                return dk, dv

            dk, dv_tot = lax.fori_loop(ki, n_k, body, (jnp.zeros((t, HD), F32), dv_tot))
            dk_ref[:, HD * j:HD * (j + 1)] = dk
        dv_ref[...] = dv_tot

    return pl.pallas_call(
        kern,
        out_shape=[jax.ShapeDtypeStruct((s, H * HD), F32), jax.ShapeDtypeStruct((s, H * DV), F32)],
        grid=(HP, n_k),
        in_specs=[pl.BlockSpec((s, 2 * HD), lambda hp, i: (0, hp)), pl.BlockSpec((t, 2 * HD), lambda hp, i: (i, hp)),
                  pl.BlockSpec((t, 2 * DV), lambda hp, i: (i, hp)), pl.BlockSpec((s, 2 * DV), lambda hp, i: (0, hp)),
                  pl.BlockSpec((1, 2, s, 1), lambda hp, i: (hp, 0, 0, 0)),
                  pl.BlockSpec((1, 2, s, 1), lambda hp, i: (hp, 0, 0, 0))],
        out_specs=[pl.BlockSpec((t, 2 * HD), lambda hp, i: (i, hp)), pl.BlockSpec((t, 2 * DV), lambda hp, i: (i, hp))],
        name=name, compiler_params=_params(2))(q, k, v, do, lse, delta)


def rope_tables(name, pos_col, inv128):
    s = pos_col.shape[0]

    def kern(p_ref, inv_ref, c_ref, s_ref):
        ang = p_ref[...].astype(F32) * inv_ref[...]
        lane = _lane()
        m_r = (lane >= DN) & (lane < DN + DR)
        c_ref[...] = jnp.where(lane < DN, 1.0, jnp.where(m_r, jnp.cos(ang), 0.0))
        s_ref[...] = jnp.where(m_r, jnp.sin(ang), 0.0)

    return _whole(kern, name, [jax.ShapeDtypeStruct((s, HD), F32)] * 2, pos_col, inv128)


def loss_kernel(name, y, tgt, tile=TILE_ROW):
    def body(row_v, _):
        err = row_v[0] - row_v[1]
        part = 0.5 * jnp.sum(jnp.mean(err * err, axis=-1, keepdims=True), axis=0, keepdims=True)
        return [err * (1.0 / D)], [jnp.broadcast_to(part, (1, 128))]

    return _row_call(name, body, [y, tgt], [], [(D, F32)], [((1, 128), F32)], tile)


def _row_tile(r, c):
    cap = max(8, (1 << 18) // max(c, 1))
    for t in (2048, 1024, 512, 256, 128, 64, 32, 16, 8):
        if t <= cap and r % t == 0:
            return t
    return r


def sum_parts(name, parts):
    n, r, c = parts.shape
    t = _row_tile(r, c)

    def kern(p_ref, o_ref):
        acc = p_ref[0].astype(F32)
        for i in range(1, n):
            acc = acc + p_ref[i].astype(F32)
        o_ref[...] = acc

    return pl.pallas_call(kern, out_shape=jax.ShapeDtypeStruct((r, c), F32), grid=(r // t,),
                          in_specs=[pl.BlockSpec((n, t, c), lambda i: (0, i, 0))],
                          out_specs=pl.BlockSpec((t, c), lambda i: (i, 0)), name=name, compiler_params=_params(1))(parts)


def adamw(name, parts, w, m, v):
    n, r, c = parts.shape
    t = _row_tile(r, c)
    c1 = 1.0 / (1.0 - ADAM_B1 ** ADAM_STEP)
    c2 = 1.0 / (1.0 - ADAM_B2 ** ADAM_STEP)

    def kern(p_ref, w_ref, m_ref, v_ref, g_ref, d_ref, nm_ref, nv_ref):
        g = p_ref[0].astype(F32)
        for i in range(1, n):
            g = g + p_ref[i].astype(F32)
        nm = ADAM_B1 * m_ref[...] + (1.0 - ADAM_B1) * g
        nv = ADAM_B2 * v_ref[...] + (1.0 - ADAM_B2) * (g * g)
        g_ref[...] = g
        nm_ref[...] = nm
        nv_ref[...] = nv
        d_ref[...] = -ADAM_LR * ((nm * c1) / (jnp.sqrt(nv * c2) + ADAM_EPS) + ADAM_WD * w_ref[...])

    spec = pl.BlockSpec((t, c), lambda i: (i, 0))
    return pl.pallas_call(kern, out_shape=[jax.ShapeDtypeStruct((r, c), F32)] * 4, grid=(r // t,),
                          in_specs=[pl.BlockSpec((n, t, c), lambda i: (0, i, 0)), spec, spec, spec],
                          out_specs=[spec] * 4, name=name, compiler_params=_params(1))(parts, w, m, v)


def _me():
    return lax.axis_index("x"), lax.axis_index("y"), lax.axis_index("c")


def _flip(x, y, c, mask):
    return (jnp.where((mask >> 2) & 1, 1 - x, x), jnp.where((mask >> 1) & 1, 1 - y, y), jnp.where(mask & 1, 1 - c, c))


def _index(x, y, c):
    return 4 * x + 2 * y + c


def _exchange(name, arr, gather):
    out_shape = (N_DEV,) + arr.shape if gather else arr.shape

    def kern(in_ref, out_ref, send_sems, recv_sems, local_sem):
        x, y, c = _me()
        me = _index(x, y, c)
        mine = pltpu.make_async_copy(in_ref if gather else in_ref.at[me], out_ref.at[me], local_sem)
        mine.start()
        copies = []
        for mask in range(1, N_DEV):
            px, py, pc = _flip(x, y, c, mask)
            peer = _index(px, py, pc)
            cp = pltpu.make_async_remote_copy(
                src_ref=in_ref if gather else in_ref.at[peer], dst_ref=out_ref.at[me],
                send_sem=send_sems.at[mask - 1], recv_sem=recv_sems.at[mask - 1],
                device_id=(px, py, pc), device_id_type=MESH)
            cp.start()
            copies.append((cp, peer))
        for mask, (cp, peer) in enumerate(copies, start=1):
            pltpu.make_async_remote_copy(
                src_ref=in_ref if gather else in_ref.at[peer], dst_ref=out_ref.at[peer],
                send_sem=send_sems.at[mask - 1], recv_sem=recv_sems.at[mask - 1],
                device_id=_flip(x, y, c, mask), device_id_type=MESH).wait_recv()
        for cp, _ in copies:
            cp.wait_send()
        mine.wait()

    any_spec = pl.BlockSpec(memory_space=pl.ANY)
    return pl.pallas_call(
        kern, out_shape=jax.ShapeDtypeStruct(out_shape, arr.dtype), in_specs=[any_spec], out_specs=any_spec,
        scratch_shapes=[pltpu.SemaphoreType.DMA((N_DEV - 1,)), pltpu.SemaphoreType.DMA((N_DEV - 1,)),
                        pltpu.SemaphoreType.DMA],
        name=name, compiler_params=pltpu.CompilerParams(has_side_effects=True))(arr)


def all_gather(name, arr):
    return _exchange(name, arr, True)


def all_to_all(name, arr):
    return _exchange(name, arr, False)


def _unshard_cols(g):
    return jnp.concatenate([g[d] for d in range(N_DEV)], axis=-1)


def _unshard_rows(g):
    return jnp.concatenate([g[d] for d in range(N_DEV)], axis=-2)


def _shard_cols(full):
    r, c8 = full.shape
    return full.reshape(r, N_DEV, c8 // N_DEV).transpose(1, 0, 2)


def _shard_rows(full):
    r8, c = full.shape
    return full.reshape(N_DEV, r8 // N_DEV, c)


def _pad_heads(w, real, padded):
    k = w.shape[0]
    w3 = w.reshape(k, H, real)
    return jnp.pad(w3, ((0, 0), (0, 0), (0, padded - real))).reshape(k, H * padded)


def _unpad_heads(w, real, padded):
    k = w.shape[0]
    return w.reshape(k, H, padded)[:, :, :real].reshape(k, H * real)


def _s5_place(ab_re, ab_im, bb_re_t, bb_im_t, c_re, c_im):
    eye = jnp.eye(GB, dtype=F32)

    def wb_part(bt):
        x4 = bt.reshape(P, NBLK, GB, N).transpose(1, 2, 0, 3)
        return jnp.einsum('kgpn,gh->kgphn', x4, eye).reshape(NBLK, GB * P, HALF)

    def wc_part(cc):
        x4 = cc.reshape(NBLK, GB, P, N)
        return jnp.einsum('kgpn,gh->kgnhp', x4, eye).reshape(NBLK, HALF, GB * P)

    wb = jnp.concatenate([wb_part(bb_re_t), wb_part(bb_im_t)], axis=-1)
    wc = jnp.concatenate([wc_part(c_re), -wc_part(c_im)], axis=1)
    a_tab = jnp.concatenate([ab_re.reshape(NBLK, 1, HALF), ab_im.reshape(NBLK, 1, HALF)], axis=-1)
    return wb, wc, a_tab


def _s5_unplace(dwb, dwc, da):
    eye = jnp.eye(GB, dtype=F32)

    def wb_part(dpart):
        x5 = dpart.reshape(NBLK, GB, P, GB, N)
        return jnp.einsum('kgphn,gh->kgpn', x5, eye).transpose(2, 0, 1, 3).reshape(P, G * N)

    def wc_part(dpart):
        x5 = dpart.reshape(NBLK, GB, N, GB, P)
        return jnp.einsum('kgnhp,gh->kgpn', x5, eye).reshape(G, P, N)

    dbb_re_t, dbb_im_t = wb_part(dwb[..., :HALF]), wb_part(dwb[..., HALF:])
    dc_re, dc_im = wc_part(dwc[:, :HALF]), -wc_part(dwc[:, HALF:])
    dab_re, dab_im = da[:, :HALF].reshape(1, G * N), da[:, HALF:].reshape(1, G * N)
    return dab_re, dab_im, dbb_re_t, dbb_im_t, dc_re, dc_im


def _row(v):
    return v.reshape(1, -1)


def kernel(x, c, positions, ada_w, ada_b, norm1_g, norm2_g, ffn_w_gate, ffn_w_up, ffn_w_down, s5_lam_re, s5_lam_im, s5_log_dt, s5_b_re, s5_b_im, s5_c_re, s5_c_im, s5_d, s5_w_glu, s5_b_glu, kv_ada_w, kv_ada_b, kv_norm_g, w_kv_a, kv_a_norm_g, w_kv_b, k_nope_norm_g, k_rope_norm_g, mla_w_dq, mla_q_norm_g, mla_w_uq, mla_q_nope_norm_g, mla_q_rope_norm_g, mla_w_o, loss_target, m_ada_w, m_ada_b, m_norm1_g, m_norm2_g, m_ffn_w_gate, m_ffn_w_up, m_ffn_w_down, m_s5_lam_re, m_s5_lam_im, m_s5_log_dt, m_s5_b_re, m_s5_b_im, m_s5_c_re, m_s5_c_im, m_s5_d, m_s5_w_glu, m_s5_b_glu, m_kv_ada_w, m_kv_ada_b, m_kv_norm_g, m_w_kv_a, m_kv_a_norm_g, m_w_kv_b, m_k_nope_norm_g, m_k_rope_norm_g, m_mla_w_dq, m_mla_q_norm_g, m_mla_w_uq, m_mla_q_nope_norm_g, m_mla_q_rope_norm_g, m_mla_w_o, v_ada_w, v_ada_b, v_norm1_g, v_norm2_g, v_ffn_w_gate, v_ffn_w_up, v_ffn_w_down, v_s5_lam_re, v_s5_lam_im, v_s5_log_dt, v_s5_b_re, v_s5_b_im, v_s5_c_re, v_s5_c_im, v_s5_d, v_s5_w_glu, v_s5_b_glu, v_kv_ada_w, v_kv_ada_b, v_kv_norm_g, v_w_kv_a, v_kv_a_norm_g, v_w_kv_b, v_k_nope_norm_g, v_k_rope_norm_g, v_mla_w_dq, v_mla_q_norm_g, v_mla_w_uq, v_mla_q_nope_norm_g, v_mla_q_rope_norm_g, v_mla_w_o):
    W = dict(ada_w=ada_w, ada_b=ada_b, norm1_g=norm1_g, norm2_g=norm2_g, ffn_w_gate=ffn_w_gate, ffn_w_up=ffn_w_up, ffn_w_down=ffn_w_down, s5_lam_re=s5_lam_re, s5_lam_im=s5_lam_im, s5_log_dt=s5_log_dt, s5_b_re=s5_b_re, s5_b_im=s5_b_im, s5_c_re=s5_c_re, s5_c_im=s5_c_im, s5_d=s5_d, s5_w_glu=s5_w_glu, s5_b_glu=s5_b_glu, kv_ada_w=kv_ada_w, kv_ada_b=kv_ada_b, kv_norm_g=kv_norm_g, w_kv_a=w_kv_a, kv_a_norm_g=kv_a_norm_g, w_kv_b=w_kv_b, k_nope_norm_g=k_nope_norm_g, k_rope_norm_g=k_rope_norm_g, mla_w_dq=mla_w_dq, mla_q_norm_g=mla_q_norm_g, mla_w_uq=mla_w_uq, mla_q_nope_norm_g=mla_q_nope_norm_g, mla_q_rope_norm_g=mla_q_rope_norm_g, mla_w_o=mla_w_o)
    M = dict(ada_w=m_ada_w, ada_b=m_ada_b, norm1_g=m_norm1_g, norm2_g=m_norm2_g, ffn_w_gate=m_ffn_w_gate, ffn_w_up=m_ffn_w_up, ffn_w_down=m_ffn_w_down, s5_lam_re=m_s5_lam_re, s5_lam_im=m_s5_lam_im, s5_log_dt=m_s5_log_dt, s5_b_re=m_s5_b_re, s5_b_im=m_s5_b_im, s5_c_re=m_s5_c_re, s5_c_im=m_s5_c_im, s5_d=m_s5_d, s5_w_glu=m_s5_w_glu, s5_b_glu=m_s5_b_glu, kv_ada_w=m_kv_ada_w, kv_ada_b=m_kv_ada_b, kv_norm_g=m_kv_norm_g, w_kv_a=m_w_kv_a, kv_a_norm_g=m_kv_a_norm_g, w_kv_b=m_w_kv_b, k_nope_norm_g=m_k_nope_norm_g, k_rope_norm_g=m_k_rope_norm_g, mla_w_dq=m_mla_w_dq, mla_q_norm_g=m_mla_q_norm_g, mla_w_uq=m_mla_w_uq, mla_q_nope_norm_g=m_mla_q_nope_norm_g, mla_q_rope_norm_g=m_mla_q_rope_norm_g, mla_w_o=m_mla_w_o)
    V = dict(ada_w=v_ada_w, ada_b=v_ada_b, norm1_g=v_norm1_g, norm2_g=v_norm2_g, ffn_w_gate=v_ffn_w_gate, ffn_w_up=v_ffn_w_up, ffn_w_down=v_ffn_w_down, s5_lam_re=v_s5_lam_re, s5_lam_im=v_s5_lam_im, s5_log_dt=v_s5_log_dt, s5_b_re=v_s5_b_re, s5_b_im=v_s5_b_im, s5_c_re=v_s5_c_re, s5_c_im=v_s5_c_im, s5_d=v_s5_d, s5_w_glu=v_s5_w_glu, s5_b_glu=v_s5_b_glu, kv_ada_w=v_kv_ada_w, kv_ada_b=v_kv_ada_b, kv_norm_g=v_kv_norm_g, w_kv_a=v_w_kv_a, kv_a_norm_g=v_kv_a_norm_g, w_kv_b=v_w_kv_b, k_nope_norm_g=v_k_nope_norm_g, k_rope_norm_g=v_k_rope_norm_g, mla_w_dq=v_mla_w_dq, mla_q_norm_g=v_mla_q_norm_g, mla_w_uq=v_mla_w_uq, mla_q_nope_norm_g=v_mla_q_nope_norm_g, mla_q_rope_norm_g=v_mla_q_rope_norm_g, mla_w_o=v_mla_w_o)
    return _step(x[0], c, positions, loss_target[0], W, M, V)


WEIGHT_NAMES = ['ada_w', 'ada_b', 'norm1_g', 'norm2_g', 'ffn_w_gate', 'ffn_w_up', 'ffn_w_down', 's5_lam_re', 's5_lam_im', 's5_log_dt', 's5_b_re', 's5_b_im', 's5_c_re', 's5_c_im', 's5_d', 's5_w_glu', 's5_b_glu', 'kv_ada_w', 'kv_ada_b', 'kv_norm_g', 'w_kv_a', 'kv_a_norm_g', 'w_kv_b', 'k_nope_norm_g', 'k_rope_norm_g', 'mla_w_dq', 'mla_q_norm_g', 'mla_w_uq', 'mla_q_nope_norm_g', 'mla_q_rope_norm_g', 'mla_w_o']
REPLICATED = ['ada_b', 'norm1_g', 'norm2_g', 's5_lam_re', 's5_lam_im', 's5_log_dt', 's5_b_re', 's5_b_im', 's5_c_re', 's5_c_im', 'kv_ada_b', 'kv_norm_g', 'kv_a_norm_g', 'k_nope_norm_g', 'k_rope_norm_g', 'mla_q_norm_g', 'mla_q_nope_norm_g', 'mla_q_rope_norm_g']
SHARDED_VEC = ['s5_d', 's5_b_glu']


def _step(x, c, positions, target, W, M, V):
    s = x.shape[0]
    me = _index(*_me())
    mxu = lambda a: a.astype(_MXU)

    wpack = [mxu(W['ffn_w_gate']).reshape(-1), mxu(W['ffn_w_up']).reshape(-1), mxu(W['ffn_w_down']).reshape(-1),
             mxu(W['s5_w_glu']).reshape(-1), mxu(W['w_kv_a']).reshape(-1), mxu(W['w_kv_b']).reshape(-1),
             mxu(W['mla_w_dq']).reshape(-1), mxu(W['mla_w_uq']).reshape(-1), mxu(W['mla_w_o']).reshape(-1)]
    sizes = [int(a.shape[0]) for a in wpack]
    total = sum(sizes)
    padded = -(-total // (16 * 1024)) * (16 * 1024)
    flat = jnp.concatenate(wpack + [jnp.zeros((padded - total,), _MXU)]).reshape(padded // 1024, 1024)
    gw = all_gather("gather_weights", flat).reshape(N_DEV, padded)
    offs = np.cumsum([0] + sizes)

    def piece(i, shape):
        return gw[:, int(offs[i]):int(offs[i + 1])].reshape((N_DEV,) + shape)

    wg_full = _unshard_cols(piece(0, (DEPTH, D, FF // N_DEV)))
    wu_full = _unshard_cols(piece(1, (DEPTH, D, FF // N_DEV)))
    wd_full = _unshard_rows(piece(2, (DEPTH, FF // N_DEV, D)))
    wglu_full = _unshard_rows(piece(3, (N_A, D // N_DEV, D)))
    wkva_full = _unshard_rows(piece(4, (D // N_DEV, KVL + DR)))
    wkvb_full = _unshard_cols(piece(5, (KVL, 2 * D // N_DEV)))
    wdq_full = _unshard_rows(piece(6, (2, D // N_DEV, QL)))
    wuq_full = _unshard_cols(piece(7, (2, QL, H * (DN + DR) // N_DEV)))
    wo_full = _unshard_rows(piece(8, (2, D // N_DEV, D)))

    vec = jnp.concatenate([c.reshape(-1), W['s5_d'].reshape(-1), W['s5_b_glu'].reshape(-1)]).reshape(1, -1)
    vec = jnp.pad(vec, ((0, 7), (0, 0)))
    gv = all_gather("gather_vectors", vec)[:, 0, :]
    c_all = gv[:, :D]
    d_full = jnp.concatenate([gv[d, D:D + 2 * 128].reshape(N_A, 128) for d in range(N_DEV)], axis=1)
    bglu_full = jnp.concatenate([gv[d, D + 256:D + 512].reshape(N_A, 128) for d in range(N_DEV)], axis=1)

    ca_all = jax.nn.silu(c_all)
    w_mod = jnp.concatenate([W['ada_w'][l] for l in range(DEPTH)] + [W['kv_ada_w']], axis=1)
    n_mod = w_mod.shape[1]
    mod_cols = small_matmul("mod_matmul", ca_all, w_mod)
    gm = all_gather("gather_mod", mod_cols)
    mine = lax.dynamic_index_in_dim(gm, me, axis=1, keepdims=False)
    per_l = D * 6 // N_DEV
    mods = []
    for l in range(DEPTH):
        full = jnp.concatenate([mine[d, per_l * l:per_l * (l + 1)] for d in range(N_DEV)]) + W['ada_b'][l]
        mods.append([_row(full[D * i:D * (i + 1)]) for i in range(6)])
    kfull = jnp.concatenate([mine[d, per_l * DEPTH:] for d in range(N_DEV)]) + W['kv_ada_b']
    k_shift, k_scale = _row(kfull[:D]), _row(kfull[D:])

    inv = 1.0 / (ROPE_THETA ** (np.arange(0, DR, 2, dtype=np.float32) / DR))
    inv128 = np.zeros((1, HD), np.float32)
    inv128[0, DN:DN + DR // 2] = inv
    inv128[0, DN + DR // 2:DN + DR] = inv
    cosf, sinf = rope_tables("rope_tables", positions.reshape(s, 1), jnp.asarray(inv128))
    zpad = lambda n: jnp.zeros((n,), F32)
    gkn128 = _row(jnp.concatenate([W['k_nope_norm_g'], zpad(HD - DN)]))
    gkr128 = _row(jnp.concatenate([zpad(DN), W['k_rope_norm_g'], zpad(HD - DN - DR)]))
    gq128 = [_row(jnp.concatenate([W['mla_q_nope_norm_g'][j], W['mla_q_rope_norm_g'][j], zpad(HD - DN - DR)]))
             for j in range(2)]
    wa_pad = jnp.concatenate([wkva_full[:, :KVL], jnp.zeros((D, DN), _MXU), wkva_full[:, KVL:],
                              jnp.zeros((D, HD - DN - DR), _MXU)], axis=1)
    wkvb3 = wkvb_full.reshape(KVL, H, DN + DV)
    wkn_pad = jnp.pad(wkvb3[:, :, :DN], ((0, 0), (0, 0), (0, HD - DN))).reshape(KVL, H * HD)
    wv_mat = wkvb3[:, :, DN:].reshape(KVL, H * DV)
    wuq_pad = [_pad_heads(wuq_full[j], DN + DR, HD) for j in range(2)]

    expand = jnp.asarray(np.kron(np.eye(G, dtype=np.float32), np.ones((1, N), np.float32)))
    s5_raw, s5_mats = [], []
    for l in range(N_A):
        raw = (_row(W['s5_lam_re'][l]), _row(W['s5_lam_im'][l]), _row(W['s5_log_dt'][l]),
               W['s5_b_re'][l].transpose(2, 0, 1).reshape(P, G * N), W['s5_b_im'][l].transpose(2, 0, 1).reshape(P, G * N))
        ab_re, ab_im, bb_re_t, bb_im_t = s5_prep_fwd(f"s5_prep_fwd", *raw, expand)
        s5_raw.append(raw)
        s5_mats.append(_s5_place(ab_re, ab_im, bb_re_t, bb_im_t, W['s5_c_re'][l], W['s5_c_im'][l]))

    g1 = [_row(W['norm1_g'][l]) for l in range(DEPTH)]
    g2 = [_row(W['norm2_g'][l]) for l in range(DEPTH)]
    saved = []
    xs = x
    kv = None
    for l in range(DEPTH):
        sh1, sc1, gt1, sh2, sc2, gt2 = mods[l]
        rec = {'x_in': xs}
        if l == N_A:
            kv_smalls = [_row(W['kv_norm_g']), k_shift, k_scale, _row(W['kv_a_norm_g']), gkn128, gkr128]
            k_mat, v_mat = seg_forward("kv_fwd", seg_kv, [xs], kv_smalls, [cosf, sinf], [wa_pad, wkn_pad, wv_mat],
                                       [(H * HD, _MXU), (H * DV, _MXU)], tap_widths=(KVL + HD, H * HD, H * DV))
            kv = {'x_in': xs, 'smalls': kv_smalls, 'k': k_mat, 'v': v_mat}
        if l < N_A:
            (h,) = seg_forward("pre_fwd", seg_pre, [xs], [g1[l], sh1, sc1], [], [], [(D, F32)])
            wb, wc, a_tab = s5_mats[l]
            y, s0 = s5_scan_fwd("s5_scan_fwd", h, wb, wc, a_tab, _row(d_full[l]))
            (x_mid,) = seg_forward("glu_fwd", seg_glu, [xs, y], [gt1, _row(bglu_full[l])], [], [wglu_full[l]],
                                   [(D, F32)], tap_widths=(D,))
            rec.update(h=h, y=y, s0=s0)
        else:
            j = l - N_A
            q_smalls = [g1[l], sh1, sc1, _row(W['mla_q_norm_g'][j]), gq128[j]]
            (q_mat,) = seg_forward("q_fwd", seg_q, [xs], q_smalls, [cosf, sinf], [wdq_full[j], wuq_pad[j]],
                                   [(H * HD, _MXU)], tap_widths=(QL, H * HD))
            o_mat, lse = attn_fwd("attn_fwd", q_mat, kv['k'], kv['v'])
            (x_mid,) = seg_forward("o_fwd", seg_o, [xs, o_mat], [gt1], [], [wo_full[j]], [(D, F32)], tap_widths=(D,))
            rec.update(q=q_mat, o=o_mat, lse=lse, q_smalls=q_smalls)
        rec['x_mid'] = x_mid
        (xs,) = seg_forward("ffn_fwd", seg_ffn, [x_mid], [g2[l], sh2, sc2, gt2], [], [wg_full[l], wu_full[l], wd_full[l]],
                            [(D, F32)], tap_widths=(FF, FF, D))
        saved.append(rec)

    dy, loss_part = loss_kernel("loss", xs, target)
    loss = lax.psum(loss_part[0, 0], ("x", "y", "c"))

    gfull = {}
    gsmall = {}
    dmod = [None] * DEPTH
    dk_tot = []
    dv_tot = []
    dx = dy
    gfull_ffn = {'g': [None] * DEPTH, 'u': [None] * DEPTH, 'd': [None] * DEPTH}
    g_n1 = [None] * DEPTH
    g_n2 = [None] * DEPTH
    g_glu = [None] * N_A
    g_bglu = [None] * N_A
    g_dskip = [None] * N_A
    g_s5 = [None] * N_A
    g_dq, g_uq, g_wo, g_qn, g_q128 = [None] * 2, [None] * 2, [None] * 2, [None] * 2, [None] * 2
    for l in range(DEPTH - 1, -1, -1):
        rec = saved[l]
        sh1, sc1, gt1, sh2, sc2, gt2 = mods[l]
        (dx,), (dgate, dup, dyd), (h_b, a_b), (dg2, dsh2, dsc2, dgt2) = seg_backward(
            "ffn_bwd", seg_ffn, [rec['x_mid']], [g2[l], sh2, sc2, gt2], [], [wg_full[l], wu_full[l], wd_full[l]],
            [dx], (FF, FF, D), (D, FF))
        gfull_ffn['g'][l] = matmul_tn("tn_ffn_in", h_b, dgate, _MXU)
        gfull_ffn['u'][l] = matmul_tn("tn_ffn_in", h_b, dup, _MXU)
        gfull_ffn['d'][l] = matmul_tn("tn_ffn_out", a_b, dyd, _MXU)
        g_n2[l] = dg2
        if l < N_A:
            (dx, dyy), (dz,), (g_b,), (dgt1, dbg) = seg_backward(
                "glu_bwd", seg_glu, [rec['x_in'], rec['y']], [gt1, _row(bglu_full[l])], [], [wglu_full[l]],
                [dx], (D,), (D,))
            g_glu[l] = matmul_tn("tn_sq", g_b, dz, _MXU)
            g_bglu[l] = dbg
            wb, wc, a_tab = s5_mats[l]
            dh, dwb, dwc, da, dd = s5_scan_bwd("s5_scan_bwd", rec['h'], dyy, rec['s0'], wb, wc, a_tab, _row(d_full[l]))
            g_dskip[l] = dd
            dab_re, dab_im, dbb_re_t, dbb_im_t, dc_re, dc_im = _s5_unplace(dwb, dwc, da)
            dlr, dli, dldt, dbr_t, dbi_t = s5_prep_bwd("s5_prep_bwd", *s5_raw[l], expand,
                                                       (dab_re, dab_im, dbb_re_t, dbb_im_t))
            g_s5[l] = (dlr.reshape(G, N), dli.reshape(G, N), dldt.reshape(G),
                       dbr_t.reshape(P, G, N).transpose(1, 2, 0), dbi_t.reshape(P, G, N).transpose(1, 2, 0), dc_re, dc_im)
            (dx,), _, _, (dg1, dsh1, dsc1) = seg_backward(
                "pre_bwd", seg_pre, [rec['x_in']], [g1[l], sh1, sc1], [], [], [dh], (), (), dx_add=dx)
        else:
            j = l - N_A
            (dx, do), (dzo,), (o_b,), (dgt1,) = seg_backward(
                "o_bwd", seg_o, [rec['x_in'], rec['o']], [gt1], [], [wo_full[j]], [dx], (D,), (D,))
            g_wo[j] = matmul_tn("tn_sq", o_b, dzo, _MXU)
            dq, delta = attn_bwd_q("attn_bwd_q", rec['q'], kv['k'], kv['v'], rec['o'], do, rec['lse'])
            dk, dv = attn_bwd_kv("attn_bwd_kv", rec['q'], kv['k'], kv['v'], do, rec['lse'], delta)
            dk_tot.append(dk)
            dv_tot.append(dv)
            (dx,), (dql, dqq), (hq_b, qn_b), (dg1, dsh1, dsc1, dqg, dq128) = seg_backward(
                "q_bwd", seg_q, [rec['x_in']], rec['q_smalls'], [cosf, sinf], [wdq_full[j], wuq_pad[j]],
                [dq], (QL, H * HD), (D, QL), dx_add=dx)
            g_dq[j] = matmul_tn("tn_dq", hq_b, dql, _MXU)
            g_uq[j] = _unpad_heads(matmul_tn("tn_uq", qn_b, dqq, _MXU), DN + DR, HD)
            g_qn[j], g_q128[j] = dqg, dq128
        g_n1[l] = dg1
        dmod[l] = jnp.concatenate([dsh1, dsc1, dgt1, dsh2, dsc2, dgt2], axis=1)
        if l == N_A:
            dkk = sum_parts("sum_dk", jnp.stack(dk_tot))
            dvv = sum_parts("sum_dv", jnp.stack(dv_tot))
            (dx,), (dta, dtk, dtv), (hk_b, ckv_b), (dkg, dksh, dksc, dag, dgkn, dgkr) = seg_backward(
                "kv_bwd", seg_kv, [kv['x_in']], kv['smalls'], [cosf, sinf], [wa_pad, wkn_pad, wv_mat],
                [dkk, dvv], (KVL + HD, H * HD, H * DV), (D, KVL), dx_add=dx)
            g_wa = matmul_tn("tn_kva", hk_b, dta, _MXU)
            g_wa = jnp.concatenate([g_wa[:, :KVL], g_wa[:, KVL + DN:KVL + DN + DR]], axis=1)
            g_kn = matmul_tn("tn_kn", ckv_b, dtk, _MXU).reshape(KVL, H, HD)[:, :, :DN]
            g_v = matmul_tn("tn_v", ckv_b, dtv, _MXU).reshape(KVL, H, DV)
            g_wkvb = jnp.concatenate([g_kn, g_v], axis=2).reshape(KVL, H * (DN + DV))
            dkmod = jnp.concatenate([dksh, dksc], axis=1)
    grad_x = dx

    dm = jnp.concatenate(dmod + [dkmod], axis=1)[0]
    per_dev = []
    for d in range(N_DEV):
        cols = [dm[6 * D * l + per_l * d:6 * D * l + per_l * (d + 1)] for l in range(DEPTH)]
        cols.append(dm[6 * D * DEPTH + (2 * D // N_DEV) * d:6 * D * DEPTH + (2 * D // N_DEV) * (d + 1)])
        per_dev.append(jnp.concatenate(cols))
    dm_dev = jnp.stack(per_dev)
    gdm = all_gather("gather_dmod", dm_dev)
    dm_mine = lax.dynamic_index_in_dim(gdm, me, axis=1, keepdims=False)
    g_wmod = small_matmul_tn("dmod_matmul", ca_all, dm_mine)
    g_ada_w = jnp.stack([g_wmod[:, per_l * l:per_l * (l + 1)] for l in range(DEPTH)])
    g_kv_ada_w = g_wmod[:, per_l * DEPTH:]
    dm_sum = sum_parts("sum_dmod", gdm.reshape(N_DEV, N_DEV, n_mod))
    g_ada_b = jnp.stack([jnp.concatenate([dm_sum[d, per_l * l:per_l * (l + 1)] for d in range(N_DEV)])
                         for l in range(DEPTH)])
    g_kv_ada_b = jnp.concatenate([dm_sum[d, per_l * DEPTH:] for d in range(N_DEV)])

    small = {
        'norm1_g': jnp.concatenate(g_n1, axis=0), 'norm2_g': jnp.concatenate(g_n2, axis=0),
        's5_lam_re': jnp.stack([g_s5[l][0] for l in range(N_A)]), 's5_lam_im': jnp.stack([g_s5[l][1] for l in range(N_A)]),
        's5_log_dt': jnp.stack([g_s5[l][2] for l in range(N_A)]),
        's5_b_re': jnp.stack([g_s5[l][3] for l in range(N_A)]), 's5_b_im': jnp.stack([g_s5[l][4] for l in range(N_A)]),
        's5_c_re': jnp.stack([g_s5[l][5] for l in range(N_A)]), 's5_c_im': jnp.stack([g_s5[l][6] for l in range(N_A)]),
        'kv_norm_g': dkg, 'kv_a_norm_g': dag, 'k_nope_norm_g': dgkn[:, :DN], 'k_rope_norm_g': dgkr[:, DN:DN + DR],
        'mla_q_norm_g': jnp.concatenate(g_qn, axis=0),
        'mla_q_nope_norm_g': jnp.concatenate([g[:, :DN] for g in g_q128], axis=0),
        'mla_q_rope_norm_g': jnp.concatenate([g[:, DN:DN + DR] for g in g_q128], axis=0),
        's5_d': jnp.concatenate(g_dskip, axis=0), 's5_b_glu': jnp.concatenate(g_bglu, axis=0),
    }
    small_names = [n for n in REPLICATED if n not in ('ada_b', 'kv_ada_b')] + SHARDED_VEC
    flat_small = jnp.concatenate([small[n].reshape(-1) for n in small_names])
    n_small = int(flat_small.shape[0])
    pad_small = -(-n_small // 65536) * 65536
    flat_small = jnp.pad(flat_small, (0, pad_small - n_small)).reshape(pad_small // 128, 128)
    g_small_sum = sum_parts("sum_small", all_gather("gather_small", flat_small)).reshape(-1)
    grads = {}
    off = 0
    for n in small_names:
        size = int(np.prod(small[n].shape))
        full = g_small_sum[off:off + size]
        off += size
        if n in SHARDED_VEC:
            full = lax.dynamic_slice_in_dim(full.reshape(N_A, D), me * (D // N_DEV), D // N_DEV, axis=1)
        grads[n] = full.reshape(W[n].shape)
    grads['ada_b'] = g_ada_b
    grads['kv_ada_b'] = g_kv_ada_b

    packed_names = REPLICATED + SHARDED_VEC

    def pack(dct):
        flat_ = jnp.concatenate([dct[n].reshape(-1) for n in packed_names])
        n_ = int(flat_.shape[0])
        p_ = -(-n_ // 65536) * 65536
        return jnp.pad(flat_, (0, p_ - n_)).reshape(p_ // 128, 128)

    _, d_p, m_p, v_p = adamw("adamw_small", pack(grads)[None], pack(W), pack(M), pack(V))
    out_delta, out_m, out_v = {}, {}, {}
    off = 0
    d_p, m_p, v_p = d_p.reshape(-1), m_p.reshape(-1), v_p.reshape(-1)
    for n in packed_names:
        size = int(np.prod(W[n].shape))
        out_delta[n] = d_p[off:off + size].reshape(W[n].shape)
        out_m[n] = m_p[off:off + size].reshape(W[n].shape)
        out_v[n] = v_p[off:off + size].reshape(W[n].shape)
        off += size

    def update(name, parts):
        shp = W[name].shape
        r = int(np.prod(shp[:-1]))
        g_, d_, m_, v_ = adamw("adamw_" + name, parts.reshape(parts.shape[0], r, shp[-1]), W[name].reshape(r, shp[-1]),
                               M[name].reshape(r, shp[-1]), V[name].reshape(r, shp[-1]))
        grads[name], out_delta[name], out_m[name], out_v[name] = (a.reshape(shp) for a in (g_, d_, m_, v_))

    update('ada_w', g_ada_w[None])
    update('kv_ada_w', g_kv_ada_w[None])
    stack_cols = lambda lst: jnp.stack([_shard_cols(a) for a in lst], axis=1)
    stack_rows = lambda lst: jnp.stack([_shard_rows(a) for a in lst], axis=1)
    update('ffn_w_gate', all_to_all("a2a_ffn_gate", stack_cols(gfull_ffn['g'])))
    update('ffn_w_up', all_to_all("a2a_ffn_up", stack_cols(gfull_ffn['u'])))
    update('ffn_w_down', all_to_all("a2a_ffn_down", stack_rows(gfull_ffn['d'])))
    update('s5_w_glu', all_to_all("a2a_glu", stack_rows(g_glu)))
    update('w_kv_a', all_to_all("a2a_kva", _shard_rows(g_wa)))
    update('w_kv_b', all_to_all("a2a_kvb", _shard_cols(g_wkvb)))
    update('mla_w_dq', all_to_all("a2a_dq", stack_rows(g_dq)))
    update('mla_w_uq', all_to_all("a2a_uq", stack_cols(g_uq)))
    update('mla_w_o', all_to_all("a2a_wo", stack_rows(g_wo)))

    return (loss, grad_x[None], *[grads[n] for n in WEIGHT_NAMES], *[out_delta[n] for n in WEIGHT_NAMES],
            *[out_m[n] for n in WEIGHT_NAMES], *[out_v[n] for n in WEIGHT_NAMES])
```

```python
import functools
import math

import numpy as np
import jax
import jax.numpy as jnp
from jax import lax
from jax.experimental import pallas as pl
from jax.experimental.pallas import tpu as pltpu

F32 = jnp.float32
_MXU = jnp.bfloat16
HI = lax.Precision.HIGHEST

D = 1024
DEPTH = 4
N_A = 2
FF = 2816
N_DEV = 8
G = 64
P = 16
N = 64
GB = 8
NBLK = G // GB
HALF = GB * N
H = 16
HP = H // 2
DN, DR, DV = 64, 32, 64
HD = 128
QL = 256
KVL = 256
CHUNK = 64
ROPE_THETA = 10000.0
ATTN_SCALE = 1.0 / math.sqrt(DN + DR)
LOG2E = 1.4426950408889634
EXP2_SCALE = ATTN_SCALE * LOG2E
EPS = 1e-6
ADAM_LR, ADAM_B1, ADAM_B2, ADAM_EPS, ADAM_WD, ADAM_STEP = 0.001, 0.9, 0.999, 1e-08, 0.01, 10
VMEM_LIMIT = 56 * 1024 * 1024
MESH = pl.DeviceIdType.MESH

TILE_ROW = 256
TILE_ATT = 256
TILE_SCAN = 256


def _params(n_grid):
    return pltpu.CompilerParams(dimension_semantics=("arbitrary",) * n_grid, vmem_limit_bytes=VMEM_LIMIT)


@jax.custom_vjp
def mm(a, w):
    return jnp.dot(a.astype(_MXU), w, preferred_element_type=F32)


def _mm_fwd(a, w):
    return mm(a, w), w


def _mm_bwd(w, g):
    da = lax.dot_general(g.astype(_MXU), w, (((1,), (1,)), ((), ())), preferred_element_type=F32)
    return da, jnp.zeros_like(w)


mm.defvjp(_mm_fwd, _mm_bwd)


def rms(x, g):
    return x * lax.rsqrt(jnp.mean(x * x, axis=-1, keepdims=True) + EPS) * g


def modulate(h, shift, scale):
    return h * (1.0 + scale) + shift


def _lane(n=HD):
    return lax.broadcasted_iota(jnp.int32, (1, n), 1)


def _rot_matrix():
    r = lax.broadcasted_iota(jnp.int32, (HD, HD), 0)
    c = lax.broadcasted_iota(jnp.int32, (HD, HD), 1)
    first = (c >= DN) & (c < DN + DR // 2) & (r == c + DR // 2)
    second = (c >= DN + DR // 2) & (c < DN + DR) & (r == c - DR // 2)
    return jnp.where(first, -1.0, jnp.where(second, 1.0, 0.0)).astype(F32)


def head_norm_rope(xh, g128, cosf, sinf, rot, with_nope):
    lane = _lane()
    m_n = lane < DN
    m_r = (lane >= DN) & (lane < DN + DR)
    sq = xh * xh
    inv_r = lax.rsqrt(jnp.sum(jnp.where(m_r, sq, 0.0), axis=-1, keepdims=True) / DR + EPS)
    if with_nope:
        inv_n = lax.rsqrt(jnp.sum(jnp.where(m_n, sq, 0.0), axis=-1, keepdims=True) / DN + EPS)
        inv = jnp.where(m_n, inv_n, jnp.where(m_r, inv_r, 0.0))
    else:
        inv = jnp.where(m_r, inv_r, 0.0)
    xg = xh * inv * g128
    return xg * cosf + jnp.dot(xg, rot, precision=HI, preferred_element_type=F32) * sinf


def seg_pre(x, g, sh, sc):
    return (modulate(rms(x, g), sh, sc),), ()


def seg_ffn(x, g, sh, sc, gt, t_g, t_u, t_d, wg, wu, wd):
    h = modulate(rms(x, g), sh, sc)
    gate = mm(h, wg) + t_g
    up = mm(h, wu) + t_u
    a = jax.nn.silu(gate) * up
    y = mm(a, wd) + t_d
    return (x + gt * y,), (h.astype(_MXU), a.astype(_MXU))


def seg_glu(x, y, gt, b, t_z, w):
    g = jax.nn.gelu(y)
    z = mm(g, w) + b + t_z
    return (x + gt * (g * jax.nn.sigmoid(z)),), (g.astype(_MXU),)


def seg_o(x, o, gt, t_o, w):
    return (x + gt * (mm(o, w) + t_o),), (o.astype(_MXU),)


def seg_q(x, g, sh, sc, qg, g128, t_l, t_q, cosf, sinf, wdq, wuq):
    h = modulate(rms(x, g), sh, sc)
    ql = mm(h, wdq) + t_l
    qn = rms(ql, qg)
    q = mm(qn, wuq) + t_q
    rot = _rot_matrix()
    heads = [head_norm_rope(q[:, HD * i:HD * (i + 1)], g128, cosf, sinf, rot, True) for i in range(H)]
    return (jnp.concatenate(heads, axis=1),), (h.astype(_MXU), qn.astype(_MXU))


def seg_kv(x, g, sh, sc, ag, gkn, gkr, t_a, t_k, t_v, cosf, sinf, wa, wkn, wv):
    hk = modulate(rms(x, g), sh, sc)
    kva = mm(hk, wa) + t_a
    ckv = rms(kva[:, :KVL], ag)
    kr = head_norm_rope(kva[:, KVL:KVL + HD], gkr, cosf, sinf, _rot_matrix(), False)
    kn = mm(ckv, wkn) + t_k
    v = mm(ckv, wv) + t_v
    heads = []
    for i in range(H):
        kh = kn[:, HD * i:HD * (i + 1)]
        inv = lax.rsqrt(jnp.sum(kh * kh, axis=-1, keepdims=True) / DN + EPS)
        heads.append(kh * inv * gkn + kr)
    return (jnp.concatenate(heads, axis=1), v), (hk.astype(_MXU), ckv.astype(_MXU))


def _row_call(name, body_fn, rows, fulls, out_rows, out_accs, tile):
    s = rows[0].shape[0]
    n_tiles = s // tile
    n_rows, n_fulls, n_or, n_oa = len(rows), len(fulls), len(out_rows), len(out_accs)

    def kern(*refs):
        i = pl.program_id(0)
        row_v = [r[...] for r in refs[:n_rows]]
        full_v = [r[...] for r in refs[n_rows:n_rows + n_fulls]]
        o_refs = refs[n_rows + n_fulls:]
        ro, ao = body_fn(row_v, full_v)
        for r, v in zip(o_refs[:n_or], ro):
            r[...] = v.astype(r.dtype)
        if n_oa:
            @pl.when(i == 0)
            def _():
                for r in o_refs[n_or:]:
                    r[...] = jnp.zeros(r.shape, r.dtype)
            for r, v in zip(o_refs[n_or:], ao):
                r[...] += v.astype(r.dtype)

    in_specs = [pl.BlockSpec((tile, a.shape[1]), lambda i: (i, 0)) for a in rows]
    for a in fulls:
        big = a.size * a.dtype.itemsize > (1 << 20)
        nd = a.ndim
        in_specs.append(pl.BlockSpec(a.shape, functools.partial(lambda i, nd_: (0,) * nd_, nd_=nd),
                                     **({"pipeline_mode": pl.Buffered(1)} if big else {})))
    out_shape = [jax.ShapeDtypeStruct((s, w), dt) for w, dt in out_rows]
    out_shape += [jax.ShapeDtypeStruct(shp, dt) for shp, dt in out_accs]
    out_specs = [pl.BlockSpec((tile, w), lambda i: (i, 0)) for w, _ in out_rows]
    out_specs += [pl.BlockSpec(shp, functools.partial(lambda i, nd_: (0,) * nd_, nd_=len(shp))) for shp, _ in out_accs]
    res = pl.pallas_call(kern, out_shape=out_shape, grid=(n_tiles,), in_specs=in_specs, out_specs=out_specs,
                         name=name, compiler_params=_params(1))(*rows, *fulls)
    return list(res)


def seg_forward(name, seg, rows, smalls, consts_rows, consts_full, out_widths, tile=TILE_ROW, tap_widths=()):
    n_r, n_s, n_cr = len(rows), len(smalls), len(consts_rows)

    def body(row_v, full_v):
        t = row_v[0].shape[0]
        taps = [jnp.zeros((t, w), F32) for w in tap_widths]
        outs, _ = seg(*row_v[:n_r], *full_v[:n_s], *taps, *row_v[n_r:], *full_v[n_s:])
        return outs, ()

    return _row_call(name, body, list(rows) + list(consts_rows), list(smalls) + list(consts_full),
                     out_widths, [], tile)


def seg_backward(name, seg, rows, smalls, consts_rows, consts_full, cots, tap_widths, aux_widths,
                 dx_add=None, tile=TILE_ROW):
    n_r, n_s, n_cr, n_c = len(rows), len(smalls), len(consts_rows), len(cots)
    has_add = dx_add is not None

    def body(row_v, full_v):
        t = row_v[0].shape[0]
        prim_rows = row_v[:n_r]
        c_rows = row_v[n_r:n_r + n_cr]
        cot_v = row_v[n_r + n_cr:n_r + n_cr + n_c]
        add_v = row_v[n_r + n_cr + n_c] if has_add else None
        small_v = full_v[:n_s]
        c_full = full_v[n_s:]
        taps = [jnp.zeros((t, w), F32) for w in tap_widths]

        def f(*args):
            return seg(*args, *c_rows, *c_full)

        _, vjp_fn, aux = jax.vjp(f, *prim_rows, *small_v, *taps, has_aux=True)
        grads = vjp_fn(tuple(c.astype(F32) for c in cot_v))
        d_rows = list(grads[:n_r])
        if has_add:
            d_rows[0] = d_rows[0] + add_v
        d_small = grads[n_r:n_r + n_s]
        d_taps = grads[n_r + n_s:]
        return d_rows + list(d_taps) + list(aux), [jnp.sum(g, axis=0, keepdims=True) if g.shape[0] != 1 else g
                                                   for g in d_small]

    all_rows = list(rows) + list(consts_rows) + list(cots) + ([dx_add] if has_add else [])
    out_rows = [(a.shape[1], F32) for a in rows] + [(w, _MXU) for w in tap_widths] + [(w, _MXU) for w in aux_widths]
    out_accs = [((1, a.shape[1]), F32) for a in smalls]
    res = _row_call(name, body, all_rows, list(smalls) + list(consts_full), out_rows, out_accs, tile)
    n_t, n_a = len(tap_widths), len(aux_widths)
    return res[:n_r], res[n_r:n_r + n_t], res[n_r + n_t:n_r + n_t + n_a], res[n_r + n_t + n_a:]


def _split(n):
    if n <= 1024:
        return n
    for t in (1408, 1024, 768, 512, 256, 128):
        if n % t == 0:
            return t
    raise ValueError(n)


def matmul_tn(name, a, b, out_dtype):
    s, k1 = a.shape
    _, k2 = b.shape
    tm, tn, ts = _split(k1), _split(k2), 512
    n_s = s // ts

    def kern(a_ref, b_ref, o_ref, acc_ref):
        k = pl.program_id(2)

        @pl.when(k == 0)
        def _():
            acc_ref[...] = jnp.zeros(acc_ref.shape, F32)

        acc_ref[...] += lax.dot_general(a_ref[...], b_ref[...], (((0,), (0,)), ((), ())),
                                        preferred_element_type=F32)

        @pl.when(k == n_s - 1)
        def _():
            o_ref[...] = acc_ref[...].astype(o_ref.dtype)

    return pl.pallas_call(
        kern, out_shape=jax.ShapeDtypeStruct((k1, k2), out_dtype), grid=(k1 // tm, k2 // tn, n_s),
        in_specs=[pl.BlockSpec((ts, tm), lambda i, j, k: (k, i)), pl.BlockSpec((ts, tn), lambda i, j, k: (k, j))],
        out_specs=pl.BlockSpec((tm, tn), lambda i, j, k: (i, j)),
        scratch_shapes=[pltpu.VMEM((tm, tn), F32)], name=name, compiler_params=_params(3))(a, b)


def small_matmul(name, a, w, tn=256):
    m, k = a.shape
    n = w.shape[1]

    def kern(a_ref, w_ref, o_ref):
        o_ref[...] = jnp.dot(a_ref[...].astype(_MXU), w_ref[...].astype(_MXU), preferred_element_type=F32)

    return pl.pallas_call(kern, out_shape=jax.ShapeDtypeStruct((m, n), F32), grid=(n // tn,),
                          in_specs=[pl.BlockSpec((m, k), lambda j: (0, 0)), pl.BlockSpec((k, tn), lambda j: (0, j))],
                          out_specs=pl.BlockSpec((m, tn), lambda j: (0, j)), name=name,
                          compiler_params=_params(1))(a, w)


def small_matmul_tn(name, a, b, tn=256):
    m, k = a.shape
    n = b.shape[1]

    def kern(a_ref, b_ref, o_ref):
        o_ref[...] = lax.dot_general(a_ref[...].astype(_MXU), b_ref[...].astype(_MXU), (((0,), (0,)), ((), ())),
                                     preferred_element_type=F32)

    return pl.pallas_call(kern, out_shape=jax.ShapeDtypeStruct((k, n), F32), grid=(n // tn,),
                          in_specs=[pl.BlockSpec((m, k), lambda j: (0, 0)), pl.BlockSpec((m, tn), lambda j: (0, j))],
                          out_specs=pl.BlockSpec((k, tn), lambda j: (0, j)), name=name,
                          compiler_params=_params(1))(a, b)


def _s5_prep_math(lam_re, lam_im, log_dt, b_re_t, b_im_t, expand):
    dt = jnp.dot(jnp.exp(log_dt), expand, precision=HI, preferred_element_type=F32)
    mag = jnp.exp(lam_re * dt)
    ab_re = mag * jnp.cos(lam_im * dt)
    ab_im = mag * jnp.sin(lam_im * dt)
    den = lam_re * lam_re + lam_im * lam_im
    nr = ab_re - 1.0
    ni = ab_im
    f_re = (nr * lam_re + ni * lam_im) / den
    f_im = (ni * lam_re - nr * lam_im) / den
    bb_re = f_re * b_re_t - f_im * b_im_t
    bb_im = f_re * b_im_t + f_im * b_re_t
    return ab_re, ab_im, bb_re, bb_im


def _whole(kern, name, out_shape, *args):
    return pl.pallas_call(kern, out_shape=out_shape, name=name,
                          compiler_params=pltpu.CompilerParams(vmem_limit_bytes=VMEM_LIMIT))(*args)


def s5_prep_fwd(name, lam_re, lam_im, log_dt, b_re_t, b_im_t, expand):
    def kern(a, b, c, d, e, f, o0, o1, o2, o3):
        r = _s5_prep_math(a[...], b[...], c[...], d[...], e[...], f[...])
        for o, v in zip((o0, o1, o2, o3), r):
            o[...] = v

    gn = lam_re.shape[1]
    shp = [jax.ShapeDtypeStruct((1, gn), F32)] * 2 + [jax.ShapeDtypeStruct((P, gn), F32)] * 2
    return _whole(kern, name, shp, lam_re, lam_im, log_dt, b_re_t, b_im_t, expand)


def s5_prep_bwd(name, lam_re, lam_im, log_dt, b_re_t, b_im_t, expand, cots):
    def kern(a, b, c, d, e, f, c0, c1, c2, c3, o0, o1, o2, o3, o4):
        ex = f[...]
        _, vjp_fn = jax.vjp(lambda *p: _s5_prep_math(*p, ex), a[...], b[...], c[...], d[...], e[...])
        g = vjp_fn((c0[...], c1[...], c2[...], c3[...]))
        for o, v in zip((o0, o1, o2, o3, o4), g):
            o[...] = v

    shp = [jax.ShapeDtypeStruct(a.shape, F32) for a in (lam_re, lam_im, log_dt, b_re_t, b_im_t)]
    return _whole(kern, name, shp, lam_re, lam_im, log_dt, b_re_t, b_im_t, expand, *cots)


def _cpowers(ar, ai):
    pw = [(ar, ai)]
    for _ in range(7):
        pr, pi = pw[-1]
        pw.append((pr * ar - pi * ai, pr * ai + pi * ar))
    return pw


def _row_select(row, values):
    out = jnp.broadcast_to(values[7], (8, values[7].shape[1]))
    for r in range(6, -1, -1):
        out = jnp.where(row == r, values[r], out)
    return out


def _scan_tables(ar, ai, reverse):
    pw = _cpowers(ar, ai)
    row = lax.broadcasted_iota(jnp.int32, (8, ar.shape[1]), 0)
    steps = []
    for d in (1, 2, 4):
        keep = (row <= 7 - d) if reverse else (row >= d)
        steps.append((jnp.where(keep, pw[d - 1][0], 0.0), jnp.where(keep, pw[d - 1][1], 0.0)))
    order = list(range(7, -1, -1)) if reverse else list(range(8))
    carry = (_row_select(row, [pw[i][0] for i in order]), _row_select(row, [pw[i][1] for i in order]))
    return steps, carry


def _tile_scan_fwd(xr, xi, cr, ci, steps, carry_m):
    for d, (mr, mi) in zip((1, 2, 4), steps):
        sr = pltpu.roll(xr, d, 0)
        si = pltpu.roll(xi, d, 0)
        xr, xi = xr + mr * sr - mi * si, xi + mr * si + mi * sr
    pr, pi = carry_m
    return xr + pr * cr - pi * ci, xi + pr * ci + pi * cr


def _tile_scan_rev(xr, xi, cr, ci, steps, carry_m):
    for d, (mr, mi) in zip((1, 2, 4), steps):
        sr = pltpu.roll(xr, 8 - d, 0)
        si = pltpu.roll(xi, 8 - d, 0)
        xr, xi = xr + mr * sr + mi * si, xi + mr * si - mi * sr
    pr, pi = carry_m
    return xr + pr * cr + pi * ci, xi + pr * ci - pi * cr


def _fwd_scan_block(buf, row0, n_tiles8, ar, ai, c0r, c0i):
    steps, carry_m = _scan_tables(ar, ai, False)

    def body(j, carry):
        cr, ci = carry
        r0 = pl.multiple_of(row0 + j * 8, 8)
        xr = buf[pl.ds(r0, 8), 0:HALF]
        xi = buf[pl.ds(r0, 8), HALF:2 * HALF]
        xr, xi = _tile_scan_fwd(xr, xi, cr, ci, steps, carry_m)
        buf[pl.ds(r0, 8), 0:HALF] = xr
        buf[pl.ds(r0, 8), HALF:2 * HALF] = xi
        return xr[7:8], xi[7:8]

    return lax.fori_loop(0, n_tiles8, body, (c0r, c0i))


def s5_scan_fwd(name, h, wb, wc, a_tab, dskip, tile=TILE_SCAN):
    s = h.shape[0]
    n_t = s // tile

    def kern(h_ref, wb_ref, wc_ref, a_ref, d_ref, y_ref, s0_ref, carry_ref, buf):
        i = pl.program_id(0)

        @pl.when(i == 0)
        def _():
            carry_ref[...] = jnp.zeros(carry_ref.shape, F32)

        s0_ref[0] = carry_ref[...]
        for k in range(NBLK):
            cols = slice(GB * P * k, GB * P * (k + 1))
            u = h_ref[:, cols]
            buf[...] = jnp.dot(u, wb_ref[k], precision=HI, preferred_element_type=F32)
            ar = a_ref[k, :, 0:HALF]
            ai = a_ref[k, :, HALF:2 * HALF]
            cr, ci = _fwd_scan_block(buf, 0, tile // 8, ar, ai, carry_ref[k:k + 1, 0:HALF],
                                     carry_ref[k:k + 1, HALF:2 * HALF])
            carry_ref[k:k + 1, 0:HALF] = cr
            carry_ref[k:k + 1, HALF:2 * HALF] = ci
            y_ref[:, cols] = jnp.dot(buf[...], wc_ref[k], precision=HI, preferred_element_type=F32) + d_ref[:, cols] * u

    full = lambda a: pl.BlockSpec(a.shape, functools.partial(lambda i, nd_: (0,) * nd_, nd_=a.ndim))
    return pl.pallas_call(
        kern,
        out_shape=[jax.ShapeDtypeStruct((s, D), F32), jax.ShapeDtypeStruct((n_t, NBLK, 2 * HALF), F32)],
        grid=(n_t,),
        in_specs=[pl.BlockSpec((tile, D), lambda i: (i, 0)), full(wb), full(wc), full(a_tab), full(dskip)],
        out_specs=[pl.BlockSpec((tile, D), lambda i: (i, 0)), pl.BlockSpec((1, NBLK, 2 * HALF), lambda i: (i, 0, 0))],
        scratch_shapes=[pltpu.VMEM((NBLK, 2 * HALF), F32), pltpu.VMEM((tile, 2 * HALF), F32)],
        name=name, compiler_params=_params(1))(h, wb, wc, a_tab, dskip)


def s5_scan_bwd(name, h, dy, s0, wb, wc, a_tab, dskip, tile=TILE_SCAN):
    s = h.shape[0]
    n_t = s // tile
    n8 = tile // 8

    def kern(h_ref, dy_ref, s0_ref, wb_ref, wc_ref, a_ref, d_ref, dh_ref, dwb_ref, dwc_ref, da_ref, dd_ref,
             lam_ref, sbuf, gbuf):
        i = pl.program_id(0)

        @pl.when(i == 0)
        def _():
            lam_ref[...] = jnp.zeros(lam_ref.shape, F32)
            dwb_ref[...] = jnp.zeros(dwb_ref.shape, F32)
            dwc_ref[...] = jnp.zeros(dwc_ref.shape, F32)
            da_ref[...] = jnp.zeros(da_ref.shape, F32)
            dd_ref[...] = jnp.zeros(dd_ref.shape, F32)

        for k in range(NBLK):
            cols = slice(GB * P * k, GB * P * (k + 1))
            u = h_ref[:, cols]
            dyk = dy_ref[:, cols]
            ar = a_ref[k, :, 0:HALF]
            ai = a_ref[k, :, HALF:2 * HALF]
            sbuf[0:8, :] = jnp.broadcast_to(s0_ref[0, k:k + 1, :], (8, 2 * HALF))
            sbuf[8:tile + 8, :] = jnp.dot(u, wb_ref[k], precision=HI, preferred_element_type=F32)
            _fwd_scan_block(sbuf, 8, n8, ar, ai, s0_ref[0, k:k + 1, 0:HALF], s0_ref[0, k:k + 1, HALF:2 * HALF])
            gbuf[...] = lax.dot_general(dyk, wc_ref[k], (((1,), (1,)), ((), ())), precision=HI,
                                        preferred_element_type=F32)
            dwc_ref[k] += lax.dot_general(sbuf[8:tile + 8, :], dyk, (((0,), (0,)), ((), ())), precision=HI,
                                          preferred_element_type=F32)
            steps, carry_m = _scan_tables(ar, ai, True)
            row = lax.broadcasted_iota(jnp.int32, (8, HALF), 0)

            def body(jj, carry):
                cr, ci, dar, dai = carry
                j = n8 - 1 - jj
                r0 = pl.multiple_of(j * 8, 8)
                xr = gbuf[pl.ds(r0, 8), 0:HALF]
                xi = gbuf[pl.ds(r0, 8), HALF:2 * HALF]
                xr, xi = _tile_scan_rev(xr, xi, cr, ci, steps, carry_m)
                gbuf[pl.ds(r0, 8), 0:HALF] = xr
                gbuf[pl.ds(r0, 8), HALF:2 * HALF] = xi
                r1 = pl.multiple_of(j * 8 + 8, 8)
                spr = jnp.where(row == 0, sbuf[pl.ds(r0, 8), 0:HALF][7:8],
                                pltpu.roll(sbuf[pl.ds(r1, 8), 0:HALF], 1, 0))
                spi = jnp.where(row == 0, sbuf[pl.ds(r0, 8), HALF:2 * HALF][7:8],
                                pltpu.roll(sbuf[pl.ds(r1, 8), HALF:2 * HALF], 1, 0))
                dar = dar + xr * spr + xi * spi
                dai = dai + xi * spr - xr * spi
                return xr[0:1], xi[0:1], dar, dai

            z8 = jnp.zeros((8, HALF), F32)
            cr, ci, dar, dai = lax.fori_loop(
                0, n8, body, (lam_ref[k:k + 1, 0:HALF], lam_ref[k:k + 1, HALF:2 * HALF], z8, z8))
            lam_ref[k:k + 1, 0:HALF] = cr
            lam_ref[k:k + 1, HALF:2 * HALF] = ci
            da_ref[k:k + 1, 0:HALF] += jnp.sum(dar, axis=0, keepdims=True)
            da_ref[k:k + 1, HALF:2 * HALF] += jnp.sum(dai, axis=0, keepdims=True)
            lam = gbuf[...]
            dwb_ref[k] += lax.dot_general(u, lam, (((0,), (0,)), ((), ())), precision=HI, preferred_element_type=F32)
            du = lax.dot_general(lam, wb_ref[k], (((1,), (1,)), ((), ())), precision=HI, preferred_element_type=F32)
            dh_ref[:, cols] = du + d_ref[:, cols] * dyk
            dd_ref[:, cols] += jnp.sum(dyk * u, axis=0, keepdims=True)

    full = lambda a: pl.BlockSpec(a.shape, functools.partial(lambda i, nd_: (0,) * nd_, nd_=a.ndim))
    fullo = lambda shp: pl.BlockSpec(shp, functools.partial(lambda i, nd_: (0,) * nd_, nd_=len(shp)))
    rev = lambda i: (n_t - 1 - i, 0)
    return pl.pallas_call(
        kern,
        out_shape=[jax.ShapeDtypeStruct((s, D), F32), jax.ShapeDtypeStruct(wb.shape, F32),
                   jax.ShapeDtypeStruct(wc.shape, F32), jax.ShapeDtypeStruct((NBLK, 2 * HALF), F32),
                   jax.ShapeDtypeStruct((1, D), F32)],
        grid=(n_t,),
        in_specs=[pl.BlockSpec((tile, D), rev), pl.BlockSpec((tile, D), rev),
                  pl.BlockSpec((1, NBLK, 2 * HALF), lambda i: (n_t - 1 - i, 0, 0)),
                  full(wb), full(wc), full(a_tab), full(dskip)],
        out_specs=[pl.BlockSpec((tile, D), rev), fullo(wb.shape), fullo(wc.shape), fullo((NBLK, 2 * HALF)),
                   fullo((1, D))],
        scratch_shapes=[pltpu.VMEM((NBLK, 2 * HALF), F32), pltpu.VMEM((tile + 8, 2 * HALF), F32),
                        pltpu.VMEM((tile, 2 * HALF), F32)],
        name=name, compiler_params=_params(1))(h, dy, s0, wb, wc, a_tab, dskip)


def _chunk_mask(q0, k0, tq, tk):
    r = (q0 + lax.broadcasted_iota(jnp.int32, (tq, tk), 0)) // CHUNK
    c = (k0 + lax.broadcasted_iota(jnp.int32, (tq, tk), 1)) // CHUNK
    return r >= c


def _head_lanes(j):
    lane = _lane(2 * DV)
    return (lane >= DV * j) & (lane < DV * (j + 1))


def _raw_scores(q, kblk, masked, t):
    s = lax.dot_general(q, kblk, (((1,), (1,)), ((), ())), preferred_element_type=F32)
    return jnp.where(_chunk_mask(0, 0, t, t), s, -1e30) if masked else s


def attn_fwd(name, q, k, v, t=TILE_ATT):
    s = q.shape[0]
    n_q = s // t

    def kern(q_ref, k_ref, v_ref, o_ref, lse_ref):
        qi = pl.program_id(1)
        qs = [q_ref[:, HD * j:HD * (j + 1)] for j in range(2)]

        def block(k0, carry, masked):
            vblk = v_ref[pl.ds(k0, t), :]
            new = []
            for j in range(2):
                m, l, acc = carry[j]
                sc = _raw_scores(qs[j], k_ref[pl.ds(k0, t), HD * j:HD * (j + 1)], masked, t)
                m_new = jnp.maximum(m, jnp.max(sc, axis=-1, keepdims=True))
                p = jnp.exp2((sc - m_new) * EXP2_SCALE)
                alpha = jnp.exp2((m - m_new) * EXP2_SCALE)
                l = alpha * l + jnp.sum(p, axis=-1, keepdims=True)
                acc = alpha * acc + jnp.dot(p.astype(_MXU), vblk, preferred_element_type=F32)
                new.append((m_new, l, acc))
            return tuple(new)

        init = tuple((jnp.full((t, 1), -1e30, F32), jnp.zeros((t, 1), F32), jnp.zeros((t, 2 * DV), F32))
                     for _ in range(2))
        carry = lax.fori_loop(0, qi, lambda kb, c: block(pl.multiple_of(kb * t, t), c, False), init)
        carry = block(pl.multiple_of(qi * t, t), carry, True)
        outs = []
        for j in range(2):
            m, l, acc = carry[j]
            outs.append(acc / l)
            lse_ref[0, j] = m * ATTN_SCALE + jnp.log(l)
        o_ref[...] = jnp.where(_head_lanes(0), outs[0], outs[1])

    return pl.pallas_call(
        kern,
        out_shape=[jax.ShapeDtypeStruct((s, H * DV), F32), jax.ShapeDtypeStruct((HP, 2, s, 1), F32)],
        grid=(HP, n_q),
        in_specs=[pl.BlockSpec((t, 2 * HD), lambda hp, i: (i, hp)), pl.BlockSpec((s, 2 * HD), lambda hp, i: (0, hp)),
                  pl.BlockSpec((s, 2 * DV), lambda hp, i: (0, hp))],
        out_specs=[pl.BlockSpec((t, 2 * DV), lambda hp, i: (i, hp)),
                   pl.BlockSpec((1, 2, t, 1), lambda hp, i: (hp, 0, i, 0))],
        name=name, compiler_params=_params(2))(q, k, v)


def attn_bwd(name, q, k, v, o, do, lse, t=TILE_ATT):
    s = q.shape[0]
    n_q = s // t

    def kern(q_ref, k_ref, v_ref, o_ref, do_ref, lse_ref, dq_ref, dk_ref, dv_ref):
        qi = pl.program_id(1)

        @pl.when(qi == 0)
        def _():
            dk_ref[...] = jnp.zeros(dk_ref.shape, F32)
            dv_ref[...] = jnp.zeros(dv_ref.shape, F32)

        qs, doms, deltas, lse2 = [], [], [], []
        for j in range(2):
            qs.append(q_ref[:, HD * j:HD * (j + 1)])
            dom = jnp.where(_head_lanes(j), do_ref[...], 0.0)
            deltas.append(jnp.sum(dom * o_ref[...], axis=-1, keepdims=True))
            doms.append(dom.astype(_MXU))
            lse2.append(lse_ref[0, j] * LOG2E)

        def block(k0, dqs, masked):
            vblk = v_ref[pl.ds(k0, t), :]
            dv_acc = None
            new = []
            for j in range(2):
                kblk = k_ref[pl.ds(k0, t), HD * j:HD * (j + 1)]
                p = jnp.exp2(_raw_scores(qs[j], kblk, masked, t) * EXP2_SCALE - lse2[j])
                dp = lax.dot_general(doms[j], vblk, (((1,), (1,)), ((), ())), preferred_element_type=F32)
                ds = (p * (dp - deltas[j])).astype(_MXU)
                new.append(dqs[j] + jnp.dot(ds, kblk, preferred_element_type=F32))
                dk_ref[pl.ds(k0, t), HD * j:HD * (j + 1)] += lax.dot_general(
                    ds, qs[j], (((0,), (0,)), ((), ())), preferred_element_type=F32)
                dvj = lax.dot_general(p.astype(_MXU), doms[j], (((0,), (0,)), ((), ())), preferred_element_type=F32)
                dv_acc = dvj if dv_acc is None else dv_acc + dvj
            dv_ref[pl.ds(k0, t), :] += dv_acc
            return tuple(new)

        init = (jnp.zeros((t, HD), F32), jnp.zeros((t, HD), F32))
        dqs = lax.fori_loop(0, qi, lambda kb, c: block(pl.multiple_of(kb * t, t), c, False), init)
        dqs = block(pl.multiple_of(qi * t, t), dqs, True)
        for j in range(2):
            dq_ref[:, HD * j:HD * (j + 1)] = dqs[j] * ATTN_SCALE

        @pl.when(qi == n_q - 1)
        def _():
            dk_ref[...] = dk_ref[...] * ATTN_SCALE

    return pl.pallas_call(
        kern,
        out_shape=[jax.ShapeDtypeStruct((s, H * HD), F32), jax.ShapeDtypeStruct((s, H * HD), F32),
                   jax.ShapeDtypeStruct((s, H * DV), F32)],
        grid=(HP, n_q),
        in_specs=[pl.BlockSpec((t, 2 * HD), lambda hp, i: (i, hp)), pl.BlockSpec((s, 2 * HD), lambda hp, i: (0, hp)),
                  pl.BlockSpec((s, 2 * DV), lambda hp, i: (0, hp)), pl.BlockSpec((t, 2 * DV), lambda hp, i: (i, hp)),
                  pl.BlockSpec((t, 2 * DV), lambda hp, i: (i, hp)),
                  pl.BlockSpec((1, 2, t, 1), lambda hp, i: (hp, 0, i, 0))],
        out_specs=[pl.BlockSpec((t, 2 * HD), lambda hp, i: (i, hp)), pl.BlockSpec((s, 2 * HD), lambda hp, i: (0, hp)),
                   pl.BlockSpec((s, 2 * DV), lambda hp, i: (0, hp))],
        name=name, compiler_params=_params(2))(q, k, v, o, do, lse)


def rope_tables(name, pos_col, inv128):
    s = pos_col.shape[0]

    def kern(p_ref, inv_ref, c_ref, s_ref):
        ang = p_ref[...].astype(F32) * inv_ref[...]
        lane = _lane()
        m_r = (lane >= DN) & (lane < DN + DR)
        c_ref[...] = jnp.where(lane < DN, 1.0, jnp.where(m_r, jnp.cos(ang), 0.0))
        s_ref[...] = jnp.where(m_r, jnp.sin(ang), 0.0)

    return _whole(kern, name, [jax.ShapeDtypeStruct((s, HD), F32)] * 2, pos_col, inv128)


def loss_kernel(name, y, tgt, tile=TILE_ROW):
    def body(row_v, _):
        err = row_v[0] - row_v[1]
        part = 0.5 * jnp.sum(jnp.mean(err * err, axis=-1, keepdims=True), axis=0, keepdims=True)
        return [err * (1.0 / D)], [jnp.broadcast_to(part, (1, 128))]

    return _row_call(name, body, [y, tgt], [], [(D, F32)], [((1, 128), F32)], tile)


def _row_tile(r, c):
    cap = max(8, (1 << 18) // max(c, 1))
    for t in (2048, 1024, 512, 256, 128, 64, 32, 16, 8):
        if t <= cap and r % t == 0:
            return t
    return r


def sum_parts(name, parts):
    n, r, c = parts.shape
    t = _row_tile(r, c)

    def kern(p_ref, o_ref):
        acc = p_ref[0].astype(F32)
        for i in range(1, n):
            acc = acc + p_ref[i].astype(F32)
        o_ref[...] = acc

    return pl.pallas_call(kern, out_shape=jax.ShapeDtypeStruct((r, c), F32), grid=(r // t,),
                          in_specs=[pl.BlockSpec((n, t, c), lambda i: (0, i, 0))],
                          out_specs=pl.BlockSpec((t, c), lambda i: (i, 0)), name=name, compiler_params=_params(1))(parts)


def adamw(name, parts, w, m, v):
    n, r, c = parts.shape
    t = _row_tile(r, c)
    c1 = 1.0 / (1.0 - ADAM_B1 ** ADAM_STEP)
    c2 = 1.0 / (1.0 - ADAM_B2 ** ADAM_STEP)

    def kern(p_ref, w_ref, m_ref, v_ref, g_ref, d_ref, nm_ref, nv_ref):
        g = p_ref[0].astype(F32)
        for i in range(1, n):
            g = g + p_ref[i].astype(F32)
        nm = ADAM_B1 * m_ref[...] + (1.0 - ADAM_B1) * g
        nv = ADAM_B2 * v_ref[...] + (1.0 - ADAM_B2) * (g * g)
        g_ref[...] = g
        nm_ref[...] = nm
        nv_ref[...] = nv
        d_ref[...] = -ADAM_LR * ((nm * c1) / (jnp.sqrt(nv * c2) + ADAM_EPS) + ADAM_WD * w_ref[...])

    spec = pl.BlockSpec((t, c), lambda i: (i, 0))
    return pl.pallas_call(kern, out_shape=[jax.ShapeDtypeStruct((r, c), F32)] * 4, grid=(r // t,),
                          in_specs=[pl.BlockSpec((n, t, c), lambda i: (0, i, 0)), spec, spec, spec],
                          out_specs=[spec] * 4, name=name, compiler_params=_params(1))(parts, w, m, v)


def _me():
    return lax.axis_index("x"), lax.axis_index("y"), lax.axis_index("c")


def _flip(x, y, c, mask):
    return (jnp.where((mask >> 2) & 1, 1 - x, x), jnp.where((mask >> 1) & 1, 1 - y, y), jnp.where(mask & 1, 1 - c, c))


def _index(x, y, c):
    return 4 * x + 2 * y + c


def _exchange(name, arr, gather):
    out_shape = (N_DEV,) + arr.shape if gather else arr.shape

    def kern(in_ref, out_ref, send_sems, recv_sems, local_sem):
        x, y, c = _me()
        me = _index(x, y, c)
        mine = pltpu.make_async_copy(in_ref if gather else in_ref.at[me], out_ref.at[me], local_sem)
        mine.start()
        copies = []
        for mask in range(1, N_DEV):
            px, py, pc = _flip(x, y, c, mask)
            peer = _index(px, py, pc)
            cp = pltpu.make_async_remote_copy(
                src_ref=in_ref if gather else in_ref.at[peer], dst_ref=out_ref.at[me],
                send_sem=send_sems.at[mask - 1], recv_sem=recv_sems.at[mask - 1],
                device_id=(px, py, pc), device_id_type=MESH)
            cp.start()
            copies.append((cp, peer))
        for mask, (cp, peer) in enumerate(copies, start=1):
            pltpu.make_async_remote_copy(
                src_ref=in_ref if gather else in_ref.at[peer], dst_ref=out_ref.at[peer],
                send_sem=send_sems.at[mask - 1], recv_sem=recv_sems.at[mask - 1],
                device_id=_flip(x, y, c, mask), device_id_type=MESH).wait_recv()
        for cp, _ in copies:
            cp.wait_send()
        mine.wait()

    any_spec = pl.BlockSpec(memory_space=pl.ANY)
    return pl.pallas_call(
        kern, out_shape=jax.ShapeDtypeStruct(out_shape, arr.dtype), in_specs=[any_spec], out_specs=any_spec,
        scratch_shapes=[pltpu.SemaphoreType.DMA((N_DEV - 1,)), pltpu.SemaphoreType.DMA((N_DEV - 1,)),
                        pltpu.SemaphoreType.DMA],
        name=name, compiler_params=pltpu.CompilerParams(has_side_effects=True))(arr)


def all_gather(name, arr):
    return _exchange(name, arr, True)


def all_to_all(name, arr):
    return _exchange(name, arr, False)


def _unshard_cols(g):
    return jnp.concatenate([g[d] for d in range(N_DEV)], axis=-1)


def _unshard_rows(g):
    return jnp.concatenate([g[d] for d in range(N_DEV)], axis=-2)


def _shard_cols(full):
    r, c8 = full.shape
    return full.reshape(r, N_DEV, c8 // N_DEV).transpose(1, 0, 2)


def _shard_rows(full):
    r8, c = full.shape
    return full.reshape(N_DEV, r8 // N_DEV, c)


def _pad_heads(w, real, padded):
    k = w.shape[0]
    w3 = w.reshape(k, H, real)
    return jnp.pad(w3, ((0, 0), (0, 0), (0, padded - real))).reshape(k, H * padded)


def _unpad_heads(w, real, padded):
    k = w.shape[0]
    return w.reshape(k, H, padded)[:, :, :real].reshape(k, H * real)


def _s5_place(ab_re, ab_im, bb_re_t, bb_im_t, c_re, c_im):
    eye = jnp.eye(GB, dtype=F32)

    def wb_part(bt):
        x4 = bt.reshape(P, NBLK, GB, N).transpose(1, 2, 0, 3)
        return jnp.einsum('kgpn,gh->kgphn', x4, eye).reshape(NBLK, GB * P, HALF)

    def wc_part(cc):
        x4 = cc.reshape(NBLK, GB, P, N)
        return jnp.einsum('kgpn,gh->kgnhp', x4, eye).reshape(NBLK, HALF, GB * P)

    wb = jnp.concatenate([wb_part(bb_re_t), wb_part(bb_im_t)], axis=-1)
    wc = jnp.concatenate([wc_part(c_re), -wc_part(c_im)], axis=1)
    a_tab = jnp.concatenate([ab_re.reshape(NBLK, 1, HALF), ab_im.reshape(NBLK, 1, HALF)], axis=-1)
    return wb, wc, a_tab


def _s5_unplace(dwb, dwc, da):
    eye = jnp.eye(GB, dtype=F32)

    def wb_part(dpart):
        x5 = dpart.reshape(NBLK, GB, P, GB, N)
        return jnp.einsum('kgphn,gh->kgpn', x5, eye).transpose(2, 0, 1, 3).reshape(P, G * N)

    def wc_part(dpart):
        x5 = dpart.reshape(NBLK, GB, N, GB, P)
        return jnp.einsum('kgnhp,gh->kgpn', x5, eye).reshape(G, P, N)

    dbb_re_t, dbb_im_t = wb_part(dwb[..., :HALF]), wb_part(dwb[..., HALF:])
    dc_re, dc_im = wc_part(dwc[:, :HALF]), -wc_part(dwc[:, HALF:])
    dab_re, dab_im = da[:, :HALF].reshape(1, G * N), da[:, HALF:].reshape(1, G * N)
    return dab_re, dab_im, dbb_re_t, dbb_im_t, dc_re, dc_im


def _row(v):
    return v.reshape(1, -1)


def kernel(x, c, positions, ada_w, ada_b, norm1_g, norm2_g, ffn_w_gate, ffn_w_up, ffn_w_down, s5_lam_re, s5_lam_im, s5_log_dt, s5_b_re, s5_b_im, s5_c_re, s5_c_im, s5_d, s5_w_glu, s5_b_glu, kv_ada_w, kv_ada_b, kv_norm_g, w_kv_a, kv_a_norm_g, w_kv_b, k_nope_norm_g, k_rope_norm_g, mla_w_dq, mla_q_norm_g, mla_w_uq, mla_q_nope_norm_g, mla_q_rope_norm_g, mla_w_o, loss_target, m_ada_w, m_ada_b, m_norm1_g, m_norm2_g, m_ffn_w_gate, m_ffn_w_up, m_ffn_w_down, m_s5_lam_re, m_s5_lam_im, m_s5_log_dt, m_s5_b_re, m_s5_b_im, m_s5_c_re, m_s5_c_im, m_s5_d, m_s5_w_glu, m_s5_b_glu, m_kv_ada_w, m_kv_ada_b, m_kv_norm_g, m_w_kv_a, m_kv_a_norm_g, m_w_kv_b, m_k_nope_norm_g, m_k_rope_norm_g, m_mla_w_dq, m_mla_q_norm_g, m_mla_w_uq, m_mla_q_nope_norm_g, m_mla_q_rope_norm_g, m_mla_w_o, v_ada_w, v_ada_b, v_norm1_g, v_norm2_g, v_ffn_w_gate, v_ffn_w_up, v_ffn_w_down, v_s5_lam_re, v_s5_lam_im, v_s5_log_dt, v_s5_b_re, v_s5_b_im, v_s5_c_re, v_s5_c_im, v_s5_d, v_s5_w_glu, v_s5_b_glu, v_kv_ada_w, v_kv_ada_b, v_kv_norm_g, v_w_kv_a, v_kv_a_norm_g, v_w_kv_b, v_k_nope_norm_g, v_k_rope_norm_g, v_mla_w_dq, v_mla_q_norm_g, v_mla_w_uq, v_mla_q_nope_norm_g, v_mla_q_rope_norm_g, v_mla_w_o):
    W = dict(ada_w=ada_w, ada_b=ada_b, norm1_g=norm1_g, norm2_g=norm2_g, ffn_w_gate=ffn_w_gate, ffn_w_up=ffn_w_up, ffn_w_down=ffn_w_down, s5_lam_re=s5_lam_re, s5_lam_im=s5_lam_im, s5_log_dt=s5_log_dt, s5_b_re=s5_b_re, s5_b_im=s5_b_im, s5_c_re=s5_c_re, s5_c_im=s5_c_im, s5_d=s5_d, s5_w_glu=s5_w_glu, s5_b_glu=s5_b_glu, kv_ada_w=kv_ada_w, kv_ada_b=kv_ada_b, kv_norm_g=kv_norm_g, w_kv_a=w_kv_a, kv_a_norm_g=kv_a_norm_g, w_kv_b=w_kv_b, k_nope_norm_g=k_nope_norm_g, k_rope_norm_g=k_rope_norm_g, mla_w_dq=mla_w_dq, mla_q_norm_g=mla_q_norm_g, mla_w_uq=mla_w_uq, mla_q_nope_norm_g=mla_q_nope_norm_g, mla_q_rope_norm_g=mla_q_rope_norm_g, mla_w_o=mla_w_o)
    M = dict(ada_w=m_ada_w, ada_b=m_ada_b, norm1_g=m_norm1_g, norm2_g=m_norm2_g, ffn_w_gate=m_ffn_w_gate, ffn_w_up=m_ffn_w_up, ffn_w_down=m_ffn_w_down, s5_lam_re=m_s5_lam_re, s5_lam_im=m_s5_lam_im, s5_log_dt=m_s5_log_dt, s5_b_re=m_s5_b_re, s5_b_im=m_s5_b_im, s5_c_re=m_s5_c_re, s5_c_im=m_s5_c_im, s5_d=m_s5_d, s5_w_glu=m_s5_w_glu, s5_b_glu=m_s5_b_glu, kv_ada_w=m_kv_ada_w, kv_ada_b=m_kv_ada_b, kv_norm_g=m_kv_norm_g, w_kv_a=m_w_kv_a, kv_a_norm_g=m_kv_a_norm_g, w_kv_b=m_w_kv_b, k_nope_norm_g=m_k_nope_norm_g, k_rope_norm_g=m_k_rope_norm_g, mla_w_dq=m_mla_w_dq, mla_q_norm_g=m_mla_q_norm_g, mla_w_uq=m_mla_w_uq, mla_q_nope_norm_g=m_mla_q_nope_norm_g, mla_q_rope_norm_g=m_mla_q_rope_norm_g, mla_w_o=m_mla_w_o)
    V = dict(ada_w=v_ada_w, ada_b=v_ada_b, norm1_g=v_norm1_g, norm2_g=v_norm2_g, ffn_w_gate=v_ffn_w_gate, ffn_w_up=v_ffn_w_up, ffn_w_down=v_ffn_w_down, s5_lam_re=v_s5_lam_re, s5_lam_im=v_s5_lam_im, s5_log_dt=v_s5_log_dt, s5_b_re=v_s5_b_re, s5_b_im=v_s5_b_im, s5_c_re=v_s5_c_re, s5_c_im=v_s5_c_im, s5_d=v_s5_d, s5_w_glu=v_s5_w_glu, s5_b_glu=v_s5_b_glu, kv_ada_w=v_kv_ada_w, kv_ada_b=v_kv_ada_b, kv_norm_g=v_kv_norm_g, w_kv_a=v_w_kv_a, kv_a_norm_g=v_kv_a_norm_g, w_kv_b=v_w_kv_b, k_nope_norm_g=v_k_nope_norm_g, k_rope_norm_g=v_k_rope_norm_g, mla_w_dq=v_mla_w_dq, mla_q_norm_g=v_mla_q_norm_g, mla_w_uq=v_mla_w_uq, mla_q_nope_norm_g=v_mla_q_nope_norm_g, mla_q_rope_norm_g=v_mla_q_rope_norm_g, mla_w_o=v_mla_w_o)
    return _step(x[0], c, positions, loss_target[0], W, M, V)


WEIGHT_NAMES = ['ada_w', 'ada_b', 'norm1_g', 'norm2_g', 'ffn_w_gate', 'ffn_w_up', 'ffn_w_down', 's5_lam_re', 's5_lam_im', 's5_log_dt', 's5_b_re', 's5_b_im', 's5_c_re', 's5_c_im', 's5_d', 's5_w_glu', 's5_b_glu', 'kv_ada_w', 'kv_ada_b', 'kv_norm_g', 'w_kv_a', 'kv_a_norm_g', 'w_kv_b', 'k_nope_norm_g', 'k_rope_norm_g', 'mla_w_dq', 'mla_q_norm_g', 'mla_w_uq', 'mla_q_nope_norm_g', 'mla_q_rope_norm_g', 'mla_w_o']
REPLICATED = ['ada_b', 'norm1_g', 'norm2_g', 's5_lam_re', 's5_lam_im', 's5_log_dt', 's5_b_re', 's5_b_im', 's5_c_re', 's5_c_im', 'kv_ada_b', 'kv_norm_g', 'kv_a_norm_g', 'k_nope_norm_g', 'k_rope_norm_g', 'mla_q_norm_g', 'mla_q_nope_norm_g', 'mla_q_rope_norm_g']
SHARDED_VEC = ['s5_d', 's5_b_glu']


def _step(x, c, positions, target, W, M, V):
    s = x.shape[0]
    me = _index(*_me())
    mxu = lambda a: a.astype(_MXU)

    wpack = [mxu(W['ffn_w_gate']).reshape(-1), mxu(W['ffn_w_up']).reshape(-1), mxu(W['ffn_w_down']).reshape(-1),
             mxu(W['s5_w_glu']).reshape(-1), mxu(W['w_kv_a']).reshape(-1), mxu(W['w_kv_b']).reshape(-1),
             mxu(W['mla_w_dq']).reshape(-1), mxu(W['mla_w_uq']).reshape(-1), mxu(W['mla_w_o']).reshape(-1)]
    sizes = [int(a.shape[0]) for a in wpack]
    total = sum(sizes)
    padded = -(-total // (16 * 1024)) * (16 * 1024)
    flat = jnp.concatenate(wpack + [jnp.zeros((padded - total,), _MXU)]).reshape(padded // 1024, 1024)
    gw = all_gather("gather_weights", flat).reshape(N_DEV, padded)
    offs = np.cumsum([0] + sizes)

    def piece(i, shape):
        return gw[:, int(offs[i]):int(offs[i + 1])].reshape((N_DEV,) + shape)

    wg_full = _unshard_cols(piece(0, (DEPTH, D, FF // N_DEV)))
    wu_full = _unshard_cols(piece(1, (DEPTH, D, FF // N_DEV)))
    wd_full = _unshard_rows(piece(2, (DEPTH, FF // N_DEV, D)))
    wglu_full = _unshard_rows(piece(3, (N_A, D // N_DEV, D)))
    wkva_full = _unshard_rows(piece(4, (D // N_DEV, KVL + DR)))
    wkvb_full = _unshard_cols(piece(5, (KVL, 2 * D // N_DEV)))
    wdq_full = _unshard_rows(piece(6, (2, D // N_DEV, QL)))
    wuq_full = _unshard_cols(piece(7, (2, QL, H * (DN + DR) // N_DEV)))
    wo_full = _unshard_rows(piece(8, (2, D // N_DEV, D)))

    vec = jnp.concatenate([c.reshape(-1), W['s5_d'].reshape(-1), W['s5_b_glu'].reshape(-1)]).reshape(1, -1)
    vec = jnp.pad(vec, ((0, 7), (0, 0)))
    gv = all_gather("gather_vectors", vec)[:, 0, :]
    c_all = gv[:, :D]
    d_full = jnp.concatenate([gv[d, D:D + 2 * 128].reshape(N_A, 128) for d in range(N_DEV)], axis=1)
    bglu_full = jnp.concatenate([gv[d, D + 256:D + 512].reshape(N_A, 128) for d in range(N_DEV)], axis=1)

    ca_all = jax.nn.silu(c_all)
    w_mod = jnp.concatenate([W['ada_w'][l] for l in range(DEPTH)] + [W['kv_ada_w']], axis=1)
    n_mod = w_mod.shape[1]
    mod_cols = small_matmul("mod_matmul", ca_all, w_mod)
    gm = all_gather("gather_mod", mod_cols)
    mine = lax.dynamic_index_in_dim(gm, me, axis=1, keepdims=False)
    per_l = D * 6 // N_DEV
    mods = []
    for l in range(DEPTH):
        full = jnp.concatenate([mine[d, per_l * l:per_l * (l + 1)] for d in range(N_DEV)]) + W['ada_b'][l]
        mods.append([_row(full[D * i:D * (i + 1)]) for i in range(6)])
    kfull = jnp.concatenate([mine[d, per_l * DEPTH:] for d in range(N_DEV)]) + W['kv_ada_b']
    k_shift, k_scale = _row(kfull[:D]), _row(kfull[D:])

    inv = 1.0 / (ROPE_THETA ** (np.arange(0, DR, 2, dtype=np.float32) / DR))
    inv128 = np.zeros((1, HD), np.float32)
    inv128[0, DN:DN + DR // 2] = inv
    inv128[0, DN + DR // 2:DN + DR] = inv
    cosf, sinf = rope_tables("rope_tables", positions.reshape(s, 1), jnp.asarray(inv128))
    zpad = lambda n: jnp.zeros((n,), F32)
    gkn128 = _row(jnp.concatenate([W['k_nope_norm_g'], zpad(HD - DN)]))
    gkr128 = _row(jnp.concatenate([zpad(DN), W['k_rope_norm_g'], zpad(HD - DN - DR)]))
    gq128 = [_row(jnp.concatenate([W['mla_q_nope_norm_g'][j], W['mla_q_rope_norm_g'][j], zpad(HD - DN - DR)]))
             for j in range(2)]
    wa_pad = jnp.concatenate([wkva_full[:, :KVL], jnp.zeros((D, DN), _MXU), wkva_full[:, KVL:],
                              jnp.zeros((D, HD - DN - DR), _MXU)], axis=1)
    wkvb3 = wkvb_full.reshape(KVL, H, DN + DV)
    wkn_pad = jnp.pad(wkvb3[:, :, :DN], ((0, 0), (0, 0), (0, HD - DN))).reshape(KVL, H * HD)
    wv_mat = wkvb3[:, :, DN:].reshape(KVL, H * DV)
    wuq_pad = [_pad_heads(wuq_full[j], DN + DR, HD) for j in range(2)]

    expand = jnp.asarray(np.kron(np.eye(G, dtype=np.float32), np.ones((1, N), np.float32)))
    s5_raw, s5_mats = [], []
    for l in range(N_A):
        raw = (_row(W['s5_lam_re'][l]), _row(W['s5_lam_im'][l]), _row(W['s5_log_dt'][l]),
               W['s5_b_re'][l].transpose(2, 0, 1).reshape(P, G * N), W['s5_b_im'][l].transpose(2, 0, 1).reshape(P, G * N))
        ab_re, ab_im, bb_re_t, bb_im_t = s5_prep_fwd(f"s5_prep_fwd", *raw, expand)
        s5_raw.append(raw)
        s5_mats.append(_s5_place(ab_re, ab_im, bb_re_t, bb_im_t, W['s5_c_re'][l], W['s5_c_im'][l]))

    g1 = [_row(W['norm1_g'][l]) for l in range(DEPTH)]
    g2 = [_row(W['norm2_g'][l]) for l in range(DEPTH)]
    saved = []
    xs = x
    kv = None
    for l in range(DEPTH):
        sh1, sc1, gt1, sh2, sc2, gt2 = mods[l]
        rec = {'x_in': xs}
        if l == N_A:
            kv_smalls = [_row(W['kv_norm_g']), k_shift, k_scale, _row(W['kv_a_norm_g']), gkn128, gkr128]
            k_mat, v_mat = seg_forward("kv_fwd", seg_kv, [xs], kv_smalls, [cosf, sinf], [wa_pad, wkn_pad, wv_mat],
                                       [(H * HD, _MXU), (H * DV, _MXU)], tap_widths=(KVL + HD, H * HD, H * DV))
            kv = {'x_in': xs, 'smalls': kv_smalls, 'k': k_mat, 'v': v_mat}
        if l < N_A:
            (h,) = seg_forward("pre_fwd", seg_pre, [xs], [g1[l], sh1, sc1], [], [], [(D, F32)])
            wb, wc, a_tab = s5_mats[l]
            y, s0 = s5_scan_fwd("s5_scan_fwd", h, wb, wc, a_tab, _row(d_full[l]))
            (x_mid,) = seg_forward("glu_fwd", seg_glu, [xs, y], [gt1, _row(bglu_full[l])], [], [wglu_full[l]],
                                   [(D, F32)], tap_widths=(D,))
            rec.update(h=h, y=y, s0=s0)
        else:
            j = l - N_A
            q_smalls = [g1[l], sh1, sc1, _row(W['mla_q_norm_g'][j]), gq128[j]]
            (q_mat,) = seg_forward("q_fwd", seg_q, [xs], q_smalls, [cosf, sinf], [wdq_full[j], wuq_pad[j]],
                                   [(H * HD, _MXU)], tap_widths=(QL, H * HD))
            o_mat, lse = attn_fwd("attn_fwd", q_mat, kv['k'], kv['v'])
            (x_mid,) = seg_forward("o_fwd", seg_o, [xs, o_mat], [gt1], [], [wo_full[j]], [(D, F32)], tap_widths=(D,))
            rec.update(q=q_mat, o=o_mat, lse=lse, q_smalls=q_smalls)
        rec['x_mid'] = x_mid
        (xs,) = seg_forward("ffn_fwd", seg_ffn, [x_mid], [g2[l], sh2, sc2, gt2], [], [wg_full[l], wu_full[l], wd_full[l]],
                            [(D, F32)], tap_widths=(FF, FF, D))
        saved.append(rec)

    dy, loss_part = loss_kernel("loss", xs, target)
    loss = lax.psum(loss_part[0, 0], ("x", "y", "c"))

    gfull = {}
    gsmall = {}
    dmod = [None] * DEPTH
    dk_tot = []
    dv_tot = []
    dx = dy
    gfull_ffn = {'g': [None] * DEPTH, 'u': [None] * DEPTH, 'd': [None] * DEPTH}
    g_n1 = [None] * DEPTH
    g_n2 = [None] * DEPTH
    g_glu = [None] * N_A
    g_bglu = [None] * N_A
    g_dskip = [None] * N_A
    g_s5 = [None] * N_A
    g_dq, g_uq, g_wo, g_qn, g_q128 = [None] * 2, [None] * 2, [None] * 2, [None] * 2, [None] * 2
    for l in range(DEPTH - 1, -1, -1):
        rec = saved[l]
        sh1, sc1, gt1, sh2, sc2, gt2 = mods[l]
        (dx,), (dgate, dup, dyd), (h_b, a_b), (dg2, dsh2, dsc2, dgt2) = seg_backward(
            "ffn_bwd", seg_ffn, [rec['x_mid']], [g2[l], sh2, sc2, gt2], [], [wg_full[l], wu_full[l], wd_full[l]],
            [dx], (FF, FF, D), (D, FF))
        gfull_ffn['g'][l] = matmul_tn("tn_ffn_in", h_b, dgate, _MXU)
        gfull_ffn['u'][l] = matmul_tn("tn_ffn_in", h_b, dup, _MXU)
        gfull_ffn['d'][l] = matmul_tn("tn_ffn_out", a_b, dyd, _MXU)
        g_n2[l] = dg2
        if l < N_A:
            (dx, dyy), (dz,), (g_b,), (dgt1, dbg) = seg_backward(
                "glu_bwd", seg_glu, [rec['x_in'], rec['y']], [gt1, _row(bglu_full[l])], [], [wglu_full[l]],
                [dx], (D,), (D,))
            g_glu[l] = matmul_tn("tn_sq", g_b, dz, _MXU)
            g_bglu[l] = dbg
            wb, wc, a_tab = s5_mats[l]
            dh, dwb, dwc, da, dd = s5_scan_bwd("s5_scan_bwd", rec['h'], dyy, rec['s0'], wb, wc, a_tab, _row(d_full[l]))
            g_dskip[l] = dd
            dab_re, dab_im, dbb_re_t, dbb_im_t, dc_re, dc_im = _s5_unplace(dwb, dwc, da)
            dlr, dli, dldt, dbr_t, dbi_t = s5_prep_bwd("s5_prep_bwd", *s5_raw[l], expand,
                                                       (dab_re, dab_im, dbb_re_t, dbb_im_t))
            g_s5[l] = (dlr.reshape(G, N), dli.reshape(G, N), dldt.reshape(G),
                       dbr_t.reshape(P, G, N).transpose(1, 2, 0), dbi_t.reshape(P, G, N).transpose(1, 2, 0), dc_re, dc_im)
            (dx,), _, _, (dg1, dsh1, dsc1) = seg_backward(
                "pre_bwd", seg_pre, [rec['x_in']], [g1[l], sh1, sc1], [], [], [dh], (), (), dx_add=dx)
        else:
            j = l - N_A
            (dx, do), (dzo,), (o_b,), (dgt1,) = seg_backward(
                "o_bwd", seg_o, [rec['x_in'], rec['o']], [gt1], [], [wo_full[j]], [dx], (D,), (D,))
            g_wo[j] = matmul_tn("tn_sq", o_b, dzo, _MXU)
            dq, dk, dv = attn_bwd("attn_bwd", rec['q'], kv['k'], kv['v'], rec['o'], do, rec['lse'])
            dk_tot.append(dk)
            dv_tot.append(dv)
            (dx,), (dql, dqq), (hq_b, qn_b), (dg1, dsh1, dsc1, dqg, dq128) = seg_backward(
                "q_bwd", seg_q, [rec['x_in']], rec['q_smalls'], [cosf, sinf], [wdq_full[j], wuq_pad[j]],
                [dq], (QL, H * HD), (D, QL), dx_add=dx)
            g_dq[j] = matmul_tn("tn_dq", hq_b, dql, _MXU)
            g_uq[j] = _unpad_heads(matmul_tn("tn_uq", qn_b, dqq, _MXU), DN + DR, HD)
            g_qn[j], g_q128[j] = dqg, dq128
        g_n1[l] = dg1
        dmod[l] = jnp.concatenate([dsh1, dsc1, dgt1, dsh2, dsc2, dgt2], axis=1)
        if l == N_A:
            dkk = sum_parts("sum_dk", jnp.stack(dk_tot))
            dvv = sum_parts("sum_dv", jnp.stack(dv_tot))
            (dx,), (dta, dtk, dtv), (hk_b, ckv_b), (dkg, dksh, dksc, dag, dgkn, dgkr) = seg_backward(
                "kv_bwd", seg_kv, [kv['x_in']], kv['smalls'], [cosf, sinf], [wa_pad, wkn_pad, wv_mat],
                [dkk, dvv], (KVL + HD, H * HD, H * DV), (D, KVL), dx_add=dx)
            g_wa = matmul_tn("tn_kva", hk_b, dta, _MXU)
            g_wa = jnp.concatenate([g_wa[:, :KVL], g_wa[:, KVL + DN:KVL + DN + DR]], axis=1)
            g_kn = matmul_tn("tn_kn", ckv_b, dtk, _MXU).reshape(KVL, H, HD)[:, :, :DN]
            g_v = matmul_tn("tn_v", ckv_b, dtv, _MXU).reshape(KVL, H, DV)
            g_wkvb = jnp.concatenate([g_kn, g_v], axis=2).reshape(KVL, H * (DN + DV))
            dkmod = jnp.concatenate([dksh, dksc], axis=1)
    grad_x = dx

    dm = jnp.concatenate(dmod + [dkmod], axis=1)[0]
    per_dev = []
    for d in range(N_DEV):
        cols = [dm[6 * D * l + per_l * d:6 * D * l + per_l * (d + 1)] for l in range(DEPTH)]
        cols.append(dm[6 * D * DEPTH + (2 * D // N_DEV) * d:6 * D * DEPTH + (2 * D // N_DEV) * (d + 1)])
        per_dev.append(jnp.concatenate(cols))
    dm_dev = jnp.stack(per_dev)
    gdm = all_gather("gather_dmod", dm_dev)
    dm_mine = lax.dynamic_index_in_dim(gdm, me, axis=1, keepdims=False)
    g_wmod = small_matmul_tn("dmod_matmul", ca_all, dm_mine)
    g_ada_w = jnp.stack([g_wmod[:, per_l * l:per_l * (l + 1)] for l in range(DEPTH)])
    g_kv_ada_w = g_wmod[:, per_l * DEPTH:]
    dm_sum = sum_parts("sum_dmod", gdm.reshape(N_DEV, N_DEV, n_mod))
    g_ada_b = jnp.stack([jnp.concatenate([dm_sum[d, per_l * l:per_l * (l + 1)] for d in range(N_DEV)])
                         for l in range(DEPTH)])
    g_kv_ada_b = jnp.concatenate([dm_sum[d, per_l * DEPTH:] for d in range(N_DEV)])

    small = {
        'norm1_g': jnp.concatenate(g_n1, axis=0), 'norm2_g': jnp.concatenate(g_n2, axis=0),
        's5_lam_re': jnp.stack([g_s5[l][0] for l in range(N_A)]), 's5_lam_im': jnp.stack([g_s5[l][1] for l in range(N_A)]),
        's5_log_dt': jnp.stack([g_s5[l][2] for l in range(N_A)]),
        's5_b_re': jnp.stack([g_s5[l][3] for l in range(N_A)]), 's5_b_im': jnp.stack([g_s5[l][4] for l in range(N_A)]),
        's5_c_re': jnp.stack([g_s5[l][5] for l in range(N_A)]), 's5_c_im': jnp.stack([g_s5[l][6] for l in range(N_A)]),
        'kv_norm_g': dkg, 'kv_a_norm_g': dag, 'k_nope_norm_g': dgkn[:, :DN], 'k_rope_norm_g': dgkr[:, DN:DN + DR],
        'mla_q_norm_g': jnp.concatenate(g_qn, axis=0),
        'mla_q_nope_norm_g': jnp.concatenate([g[:, :DN] for g in g_q128], axis=0),
        'mla_q_rope_norm_g': jnp.concatenate([g[:, DN:DN + DR] for g in g_q128], axis=0),
        's5_d': jnp.concatenate(g_dskip, axis=0), 's5_b_glu': jnp.concatenate(g_bglu, axis=0),
    }
    small_names = [n for n in REPLICATED if n not in ('ada_b', 'kv_ada_b')] + SHARDED_VEC
    flat_small = jnp.concatenate([small[n].reshape(-1) for n in small_names])
    n_small = int(flat_small.shape[0])
    pad_small = -(-n_small // 65536) * 65536
    flat_small = jnp.pad(flat_small, (0, pad_small - n_small)).reshape(pad_small // 128, 128)
    g_small_sum = sum_parts("sum_small", all_gather("gather_small", flat_small)).reshape(-1)
    grads = {}
    off = 0
    for n in small_names:
        size = int(np.prod(small[n].shape))
        full = g_small_sum[off:off + size]
        off += size
        if n in SHARDED_VEC:
            full = lax.dynamic_slice_in_dim(full.reshape(N_A, D), me * (D // N_DEV), D // N_DEV, axis=1)
        grads[n] = full.reshape(W[n].shape)
    grads['ada_b'] = g_ada_b
    grads['kv_ada_b'] = g_kv_ada_b

    packed_names = REPLICATED + SHARDED_VEC

    def pack(dct):
        flat_ = jnp.concatenate([dct[n].reshape(-1) for n in packed_names])
        n_ = int(flat_.shape[0])
        p_ = -(-n_ // 65536) * 65536
        return jnp.pad(flat_, (0, p_ - n_)).reshape(p_ // 128, 128)

    _, d_p, m_p, v_p = adamw("adamw_small", pack(grads)[None], pack(W), pack(M), pack(V))
    out_delta, out_m, out_v = {}, {}, {}
    off = 0
    d_p, m_p, v_p = d_p.reshape(-1), m_p.reshape(-1), v_p.reshape(-1)
    for n in packed_names:
        size = int(np.prod(W[n].shape))
        out_delta[n] = d_p[off:off + size].reshape(W[n].shape)
        out_m[n] = m_p[off:off + size].reshape(W[n].shape)
        out_v[n] = v_p[off:off + size].reshape(W[n].shape)
        off += size

    def update(name, parts):
        shp = W[name].shape
        r = int(np.prod(shp[:-1]))
        g_, d_, m_, v_ = adamw("adamw_" + name, parts.reshape(parts.shape[0], r, shp[-1]), W[name].reshape(r, shp[-1]),
                               M[name].reshape(r, shp[-1]), V[name].reshape(r, shp[-1]))
        grads[name], out_delta[name], out_m[name], out_v[name] = (a.reshape(shp) for a in (g_, d_, m_, v_))

    update('ada_w', g_ada_w[None])
    update('kv_ada_w', g_kv_ada_w[None])
    stack_cols = lambda lst: jnp.stack([_shard_cols(a) for a in lst], axis=1)
    stack_rows = lambda lst: jnp.stack([_shard_rows(a) for a in lst], axis=1)
    update('ffn_w_gate', all_to_all("a2a_ffn_gate", stack_cols(gfull_ffn['g'])))
    update('ffn_w_up', all_to_all("a2a_ffn_up", stack_cols(gfull_ffn['u'])))
    update('ffn_w_down', all_to_all("a2a_ffn_down", stack_rows(gfull_ffn['d'])))
    update('s5_w_glu', all_to_all("a2a_glu", stack_rows(g_glu)))
    update('w_kv_a', all_to_all("a2a_kva", _shard_rows(g_wa)))
    update('w_kv_b', all_to_all("a2a_kvb", _shard_cols(g_wkvb)))
    update('mla_w_dq', all_to_all("a2a_dq", stack_rows(g_dq)))
    update('mla_w_uq', all_to_all("a2a_uq", stack_cols(g_uq)))
    update('mla_w_o', all_to_all("a2a_wo", stack_rows(g_wo)))

    return (loss, grad_x[None], *[grads[n] for n in WEIGHT_NAMES], *[out_delta[n] for n in WEIGHT_NAMES],
            *[out_m[n] for n in WEIGHT_NAMES], *[out_v[n] for n in WEIGHT_NAMES])
```

```python
import functools
import math

import numpy as np
import jax
import jax.numpy as jnp
from jax import lax
from jax.experimental import pallas as pl
from jax.experimental.pallas import tpu as pltpu

F32 = jnp.float32
_MXU = jnp.bfloat16
HI = lax.Precision.HIGHEST

D = 1024
DEPTH = 4
N_A = 2
FF = 2816
FFB = 384
FFP = 8 * FFB
N_DEV = 8
G = 64
P = 16
N = 64
GB = 8
NBLK = G // GB
HALF = GB * N
H = 16
HP = H // 2
DN, DR, DV = 64, 32, 64
HD = 128
QL = 256
KVL = 256
CHUNK = 64
ROPE_THETA = 10000.0
ATTN_SCALE = 1.0 / math.sqrt(DN + DR)
LOG2E = 1.4426950408889634
EXP2_SCALE = ATTN_SCALE * LOG2E
EPS = 1e-6
ADAM_LR, ADAM_B1, ADAM_B2, ADAM_EPS, ADAM_WD, ADAM_STEP = 0.001, 0.9, 0.999, 1e-08, 0.01, 10
VMEM_LIMIT = 56 * 1024 * 1024
MESH = pl.DeviceIdType.MESH

TILE_ROW = 256
TILE_ATT = 256
TILE_SCAN = 256


def _params(n_grid):
    return pltpu.CompilerParams(dimension_semantics=("arbitrary",) * n_grid, vmem_limit_bytes=VMEM_LIMIT)


@jax.custom_vjp
def mm(a, w):
    return jnp.dot(a.astype(_MXU), w, preferred_element_type=F32)


def _mm_fwd(a, w):
    return mm(a, w), w


def _mm_bwd(w, g):
    da = lax.dot_general(g.astype(_MXU), w, (((1,), (1,)), ((), ())), preferred_element_type=F32)
    return da, jnp.zeros_like(w)


mm.defvjp(_mm_fwd, _mm_bwd)


def rms(x, g):
    return x * lax.rsqrt(jnp.mean(x * x, axis=-1, keepdims=True) + EPS) * g


def modulate(h, shift, scale):
    return h * (1.0 + scale) + shift


def _lane(n=HD):
    return lax.broadcasted_iota(jnp.int32, (1, n), 1)


def _rot_matrix():
    r = lax.broadcasted_iota(jnp.int32, (HD, HD), 0)
    c = lax.broadcasted_iota(jnp.int32, (HD, HD), 1)
    first = (c >= DN) & (c < DN + DR // 2) & (r == c + DR // 2)
    second = (c >= DN + DR // 2) & (c < DN + DR) & (r == c - DR // 2)
    return jnp.where(first, -1.0, jnp.where(second, 1.0, 0.0)).astype(F32)


def head_norm_rope(xh, g128, cosf, sinf, rot, with_nope):
    lane = _lane()
    m_n = lane < DN
    m_r = (lane >= DN) & (lane < DN + DR)
    sq = xh * xh
    inv_r = lax.rsqrt(jnp.sum(jnp.where(m_r, sq, 0.0), axis=-1, keepdims=True) / DR + EPS)
    if with_nope:
        inv_n = lax.rsqrt(jnp.sum(jnp.where(m_n, sq, 0.0), axis=-1, keepdims=True) / DN + EPS)
        inv = jnp.where(m_n, inv_n, jnp.where(m_r, inv_r, 0.0))
    else:
        inv = jnp.where(m_r, inv_r, 0.0)
    xg = xh * inv * g128
    return xg * cosf + jnp.dot(xg, rot, precision=HI, preferred_element_type=F32) * sinf


def seg_pre(x, g, sh, sc):
    return (modulate(rms(x, g), sh, sc),), ()


def seg_ffn(x, g, sh, sc, gt, t_g, t_u, t_d, wg, wu, wd):
    h = modulate(rms(x, g), sh, sc)
    gate = mm(h, wg) + t_g
    up = mm(h, wu) + t_u
    a = jax.nn.silu(gate) * up
    y = mm(a, wd) + t_d
    return (x + gt * y,), (h.astype(_MXU), a.astype(_MXU))


def seg_glu(x, y, gt, b, t_z, w):
    g = jax.nn.gelu(y)
    z = mm(g, w) + b + t_z
    return (x + gt * (g * jax.nn.sigmoid(z)),), (g.astype(_MXU),)


def seg_o(x, o, gt, t_o, w):
    return (x + gt * (mm(o, w) + t_o),), (o.astype(_MXU),)


def seg_q(x, g, sh, sc, qg, g128, t_l, t_q, cosf, sinf, wdq, wuq):
    h = modulate(rms(x, g), sh, sc)
    ql = mm(h, wdq) + t_l
    qn = rms(ql, qg)
    q = mm(qn, wuq) + t_q
    rot = _rot_matrix()
    heads = [head_norm_rope(q[:, HD * i:HD * (i + 1)], g128, cosf, sinf, rot, True) for i in range(H)]
    return (jnp.concatenate(heads, axis=1),), (h.astype(_MXU), qn.astype(_MXU))


def seg_kv(x, g, sh, sc, ag, gkn, gkr, t_a, t_k, t_v, cosf, sinf, wa, wkn, wv):
    hk = modulate(rms(x, g), sh, sc)
    kva = mm(hk, wa) + t_a
    ckv = rms(kva[:, :KVL], ag)
    kr = head_norm_rope(kva[:, KVL:KVL + HD], gkr, cosf, sinf, _rot_matrix(), False)
    kn = mm(ckv, wkn) + t_k
    v = mm(ckv, wv) + t_v
    heads = []
    for i in range(H):
        kh = kn[:, HD * i:HD * (i + 1)]
        inv = lax.rsqrt(jnp.sum(kh * kh, axis=-1, keepdims=True) / DN + EPS)
        heads.append(kh * inv * gkn + kr)
    return (jnp.concatenate(heads, axis=1), v), (hk.astype(_MXU), ckv.astype(_MXU))


def _row_call(name, body_fn, rows, fulls, out_rows, out_accs, tile):
    s = rows[0].shape[0]
    n_tiles = s // tile
    n_rows, n_fulls, n_or, n_oa = len(rows), len(fulls), len(out_rows), len(out_accs)

    def kern(*refs):
        i = pl.program_id(0)
        row_v = [r[...] for r in refs[:n_rows]]
        full_v = [r[...] for r in refs[n_rows:n_rows + n_fulls]]
        o_refs = refs[n_rows + n_fulls:]
        ro, ao = body_fn(row_v, full_v)
        for r, v in zip(o_refs[:n_or], ro):
            r[...] = v.astype(r.dtype)
        if n_oa:
            @pl.when(i == 0)
            def _():
                for r in o_refs[n_or:]:
                    r[...] = jnp.zeros(r.shape, r.dtype)
            for r, v in zip(o_refs[n_or:], ao):
                r[...] += v.astype(r.dtype)

    in_specs = [pl.BlockSpec((tile, a.shape[1]), lambda i: (i, 0)) for a in rows]
    for a in fulls:
        big = a.size * a.dtype.itemsize > (1 << 20)
        nd = a.ndim
        in_specs.append(pl.BlockSpec(a.shape, functools.partial(lambda i, nd_: (0,) * nd_, nd_=nd),
                                     **({"pipeline_mode": pl.Buffered(1)} if big else {})))
    out_shape = [jax.ShapeDtypeStruct((s, w), dt) for w, dt in out_rows]
    out_shape += [jax.ShapeDtypeStruct(shp, dt) for shp, dt in out_accs]
    out_specs = [pl.BlockSpec((tile, w), lambda i: (i, 0)) for w, _ in out_rows]
    out_specs += [pl.BlockSpec(shp, functools.partial(lambda i, nd_: (0,) * nd_, nd_=len(shp))) for shp, _ in out_accs]
    res = pl.pallas_call(kern, out_shape=out_shape, grid=(n_tiles,), in_specs=in_specs, out_specs=out_specs,
                         name=name, compiler_params=_params(1))(*rows, *fulls)
    return list(res)


def seg_forward(name, seg, rows, smalls, consts_rows, consts_full, out_widths, tile=TILE_ROW, tap_widths=()):
    n_r, n_s, n_cr = len(rows), len(smalls), len(consts_rows)

    def body(row_v, full_v):
        t = row_v[0].shape[0]
        taps = [jnp.zeros((t, w), F32) for w in tap_widths]
        outs, _ = seg(*row_v[:n_r], *full_v[:n_s], *taps, *row_v[n_r:], *full_v[n_s:])
        return outs, ()

    return _row_call(name, body, list(rows) + list(consts_rows), list(smalls) + list(consts_full),
                     out_widths, [], tile)


def seg_backward(name, seg, rows, smalls, consts_rows, consts_full, cots, tap_widths, aux_widths,
                 dx_add=None, tile=TILE_ROW):
    n_r, n_s, n_cr, n_c = len(rows), len(smalls), len(consts_rows), len(cots)
    has_add = dx_add is not None

    def body(row_v, full_v):
        t = row_v[0].shape[0]
        prim_rows = row_v[:n_r]
        c_rows = row_v[n_r:n_r + n_cr]
        cot_v = row_v[n_r + n_cr:n_r + n_cr + n_c]
        add_v = row_v[n_r + n_cr + n_c] if has_add else None
        small_v = full_v[:n_s]
        c_full = full_v[n_s:]
        taps = [jnp.zeros((t, w), F32) for w in tap_widths]

        def f(*args):
            return seg(*args, *c_rows, *c_full)

        _, vjp_fn, aux = jax.vjp(f, *prim_rows, *small_v, *taps, has_aux=True)
        grads = vjp_fn(tuple(c.astype(F32) for c in cot_v))
        d_rows = list(grads[:n_r])
        if has_add:
            d_rows[0] = d_rows[0] + add_v
        d_small = grads[n_r:n_r + n_s]
        d_taps = grads[n_r + n_s:]
        return d_rows + list(d_taps) + list(aux), [jnp.sum(g, axis=0, keepdims=True) if g.shape[0] != 1 else g
                                                   for g in d_small]

    all_rows = list(rows) + list(consts_rows) + list(cots) + ([dx_add] if has_add else [])
    out_rows = [(a.shape[1], F32) for a in rows] + [(w, _MXU) for w in tap_widths] + [(w, _MXU) for w in aux_widths]
    out_accs = [((1, a.shape[1]), F32) for a in smalls]
    res = _row_call(name, body, all_rows, list(smalls) + list(consts_full), out_rows, out_accs, tile)
    n_t, n_a = len(tap_widths), len(aux_widths)
    return res[:n_r], res[n_r:n_r + n_t], res[n_r + n_t:n_r + n_t + n_a], res[n_r + n_t + n_a:]


def _split(n):
    if n <= 1024:
        return n
    for t in (1408, 1024, 768, 512, 256, 128):
        if n % t == 0:
            return t
    raise ValueError(n)


def matmul_tn(name, a, b, out_dtype, col_blocks=None):
    s, k1 = a.shape
    _, k2 = b.shape
    tm, ts = _split(k1), 512
    tn = _split(k2) if col_blocks is None else k2 // col_blocks
    n_s = s // ts

    def kern(a_ref, b_ref, o_ref, acc_ref):
        k = pl.program_id(2)

        @pl.when(k == 0)
        def _():
            acc_ref[...] = jnp.zeros(acc_ref.shape, F32)

        acc_ref[...] += lax.dot_general(a_ref[...], b_ref[...], (((0,), (0,)), ((), ())),
                                        preferred_element_type=F32)

        @pl.when(k == n_s - 1)
        def _():
            o_ref[...] = acc_ref[...].astype(o_ref.dtype)

    if col_blocks is None:
        out_shape = jax.ShapeDtypeStruct((k1, k2), out_dtype)
        out_spec = pl.BlockSpec((tm, tn), lambda i, j, k: (i, j))
    else:
        out_shape = jax.ShapeDtypeStruct((col_blocks, k1, tn), out_dtype)
        out_spec = pl.BlockSpec((None, tm, tn), lambda i, j, k: (j, i, 0))
    return pl.pallas_call(
        kern, out_shape=out_shape, grid=(k1 // tm, k2 // tn, n_s),
        in_specs=[pl.BlockSpec((ts, tm), lambda i, j, k: (k, i)), pl.BlockSpec((ts, tn), lambda i, j, k: (k, j))],
        out_specs=out_spec,
        scratch_shapes=[pltpu.VMEM((tm, tn), F32)], name=name, compiler_params=_params(3))(a, b)


def small_matmul(name, a, w, tn=256):
    m, k = a.shape
    n = w.shape[1]

    def kern(a_ref, w_ref, o_ref):
        o_ref[...] = jnp.dot(a_ref[...].astype(_MXU), w_ref[...].astype(_MXU), preferred_element_type=F32)

    return pl.pallas_call(kern, out_shape=jax.ShapeDtypeStruct((m, n), F32), grid=(n // tn,),
                          in_specs=[pl.BlockSpec((m, k), lambda j: (0, 0)), pl.BlockSpec((k, tn), lambda j: (0, j))],
                          out_specs=pl.BlockSpec((m, tn), lambda j: (0, j)), name=name,
                          compiler_params=_params(1))(a, w)


def small_matmul_tn(name, a, b, tn=256):
    m, k = a.shape
    n = b.shape[1]

    def kern(a_ref, b_ref, o_ref):
        o_ref[...] = lax.dot_general(a_ref[...].astype(_MXU), b_ref[...].astype(_MXU), (((0,), (0,)), ((), ())),
                                     preferred_element_type=F32)

    return pl.pallas_call(kern, out_shape=jax.ShapeDtypeStruct((k, n), F32), grid=(n // tn,),
                          in_specs=[pl.BlockSpec((m, k), lambda j: (0, 0)), pl.BlockSpec((m, tn), lambda j: (0, j))],
                          out_specs=pl.BlockSpec((k, tn), lambda j: (0, j)), name=name,
                          compiler_params=_params(1))(a, b)


def _s5_prep_math(lam_re, lam_im, log_dt, b_re_t, b_im_t, expand):
    dt = jnp.dot(jnp.exp(log_dt), expand, precision=HI, preferred_element_type=F32)
    mag = jnp.exp(lam_re * dt)
    ab_re = mag * jnp.cos(lam_im * dt)
    ab_im = mag * jnp.sin(lam_im * dt)
    den = lam_re * lam_re + lam_im * lam_im
    nr = ab_re - 1.0
    ni = ab_im
    f_re = (nr * lam_re + ni * lam_im) / den
    f_im = (ni * lam_re - nr * lam_im) / den
    bb_re = f_re * b_re_t - f_im * b_im_t
    bb_im = f_re * b_im_t + f_im * b_re_t
    return ab_re, ab_im, bb_re, bb_im


def _whole(kern, name, out_shape, *args):
    return pl.pallas_call(kern, out_shape=out_shape, name=name,
                          compiler_params=pltpu.CompilerParams(vmem_limit_bytes=VMEM_LIMIT))(*args)


def s5_prep_fwd(name, lam_re, lam_im, log_dt, b_re_t, b_im_t, expand):
    def kern(a, b, c, d, e, f, o0, o1, o2, o3):
        r = _s5_prep_math(a[...], b[...], c[...], d[...], e[...], f[...])
        for o, v in zip((o0, o1, o2, o3), r):
            o[...] = v

    gn = lam_re.shape[1]
    shp = [jax.ShapeDtypeStruct((1, gn), F32)] * 2 + [jax.ShapeDtypeStruct((P, gn), F32)] * 2
    return _whole(kern, name, shp, lam_re, lam_im, log_dt, b_re_t, b_im_t, expand)


def s5_prep_bwd(name, lam_re, lam_im, log_dt, b_re_t, b_im_t, expand, cots):
    def kern(a, b, c, d, e, f, c0, c1, c2, c3, o0, o1, o2, o3, o4):
        ex = f[...]
        _, vjp_fn = jax.vjp(lambda *p: _s5_prep_math(*p, ex), a[...], b[...], c[...], d[...], e[...])
        g = vjp_fn((c0[...], c1[...], c2[...], c3[...]))
        for o, v in zip((o0, o1, o2, o3, o4), g):
            o[...] = v

    shp = [jax.ShapeDtypeStruct(a.shape, F32) for a in (lam_re, lam_im, log_dt, b_re_t, b_im_t)]
    return _whole(kern, name, shp, lam_re, lam_im, log_dt, b_re_t, b_im_t, expand, *cots)


def _cpowers(ar, ai):
    pw = [(ar, ai)]
    for _ in range(7):
        pr, pi = pw[-1]
        pw.append((pr * ar - pi * ai, pr * ai + pi * ar))
    return pw


def _row_select(row, values):
    out = jnp.broadcast_to(values[7], (8, values[7].shape[1]))
    for r in range(6, -1, -1):
        out = jnp.where(row == r, values[r], out)
    return out


def _scan_tables(ar, ai, reverse):
    pw = _cpowers(ar, ai)
    row = lax.broadcasted_iota(jnp.int32, (8, ar.shape[1]), 0)
    steps = []
    for d in (1, 2, 4):
        keep = (row <= 7 - d) if reverse else (row >= d)
        steps.append((jnp.where(keep, pw[d - 1][0], 0.0), jnp.where(keep, pw[d - 1][1], 0.0)))
    order = list(range(7, -1, -1)) if reverse else list(range(8))
    carry = (_row_select(row, [pw[i][0] for i in order]), _row_select(row, [pw[i][1] for i in order]))
    return steps, carry


def _tile_scan_fwd(xr, xi, cr, ci, steps, carry_m):
    for d, (mr, mi) in zip((1, 2, 4), steps):
        sr = pltpu.roll(xr, d, 0)
        si = pltpu.roll(xi, d, 0)
        xr, xi = xr + mr * sr - mi * si, xi + mr * si + mi * sr
    pr, pi = carry_m
    return xr + pr * cr - pi * ci, xi + pr * ci + pi * cr


def _tile_scan_rev(xr, xi, cr, ci, steps, carry_m):
    for d, (mr, mi) in zip((1, 2, 4), steps):
        sr = pltpu.roll(xr, 8 - d, 0)
        si = pltpu.roll(xi, 8 - d, 0)
        xr, xi = xr + mr * sr + mi * si, xi + mr * si - mi * sr
    pr, pi = carry_m
    return xr + pr * cr + pi * ci, xi + pr * ci - pi * cr


def _fwd_scan_block(buf, row0, n_tiles8, ar, ai, c0r, c0i):
    steps, carry_m = _scan_tables(ar, ai, False)

    def body(j, carry):
        cr, ci = carry
        r0 = pl.multiple_of(row0 + j * 8, 8)
        xr = buf[pl.ds(r0, 8), 0:HALF]
        xi = buf[pl.ds(r0, 8), HALF:2 * HALF]
        xr, xi = _tile_scan_fwd(xr, xi, cr, ci, steps, carry_m)
        buf[pl.ds(r0, 8), 0:HALF] = xr
        buf[pl.ds(r0, 8), HALF:2 * HALF] = xi
        return xr[7:8], xi[7:8]

    return lax.fori_loop(0, n_tiles8, body, (c0r, c0i))


def s5_scan_fwd(name, h, wb, wc, a_tab, dskip, tile=TILE_SCAN):
    s = h.shape[0]
    n_t = s // tile

    def kern(h_ref, wb_ref, wc_ref, a_ref, d_ref, y_ref, s0_ref, carry_ref, buf):
        i = pl.program_id(0)

        @pl.when(i == 0)
        def _():
            carry_ref[...] = jnp.zeros(carry_ref.shape, F32)

        s0_ref[0] = carry_ref[...]
        for k in range(NBLK):
            cols = slice(GB * P * k, GB * P * (k + 1))
            u = h_ref[:, cols]
            buf[...] = jnp.dot(u.astype(_MXU), wb_ref[k], preferred_element_type=F32)
            ar = a_ref[k, :, 0:HALF]
            ai = a_ref[k, :, HALF:2 * HALF]
            cr, ci = _fwd_scan_block(buf, 0, tile // 8, ar, ai, carry_ref[k:k + 1, 0:HALF],
                                     carry_ref[k:k + 1, HALF:2 * HALF])
            carry_ref[k:k + 1, 0:HALF] = cr
            carry_ref[k:k + 1, HALF:2 * HALF] = ci
            y_ref[:, cols] = jnp.dot(buf[...].astype(_MXU), wc_ref[k], preferred_element_type=F32) + d_ref[:, cols] * u

    full = lambda a: pl.BlockSpec(a.shape, functools.partial(lambda i, nd_: (0,) * nd_, nd_=a.ndim))
    return pl.pallas_call(
        kern,
        out_shape=[jax.ShapeDtypeStruct((s, D), F32), jax.ShapeDtypeStruct((n_t, NBLK, 2 * HALF), F32)],
        grid=(n_t,),
        in_specs=[pl.BlockSpec((tile, D), lambda i: (i, 0)), full(wb), full(wc), full(a_tab), full(dskip)],
        out_specs=[pl.BlockSpec((tile, D), lambda i: (i, 0)), pl.BlockSpec((1, NBLK, 2 * HALF), lambda i: (i, 0, 0))],
        scratch_shapes=[pltpu.VMEM((NBLK, 2 * HALF), F32), pltpu.VMEM((tile, 2 * HALF), F32)],
        name=name, compiler_params=_params(1))(h, wb, wc, a_tab, dskip)


def s5_scan_bwd(name, h, dy, s0, wb, wc, a_tab, dskip, tile=TILE_SCAN):
    s = h.shape[0]
    n_t = s // tile
    n8 = tile // 8

    def kern(h_ref, dy_ref, s0_ref, wb_ref, wc_ref, a_ref, d_ref, dh_ref, dwb_ref, dwc_ref, da_ref, dd_ref,
             lam_ref, sbuf, gbuf):
        i = pl.program_id(0)

        @pl.when(i == 0)
        def _():
            lam_ref[...] = jnp.zeros(lam_ref.shape, F32)
            dwb_ref[...] = jnp.zeros(dwb_ref.shape, F32)
            dwc_ref[...] = jnp.zeros(dwc_ref.shape, F32)
            da_ref[...] = jnp.zeros(da_ref.shape, F32)
            dd_ref[...] = jnp.zeros(dd_ref.shape, F32)

        for k in range(NBLK):
            cols = slice(GB * P * k, GB * P * (k + 1))
            u = h_ref[:, cols]
            dyk = dy_ref[:, cols]
            ar = a_ref[k, :, 0:HALF]
            ai = a_ref[k, :, HALF:2 * HALF]
            sbuf[0:8, :] = jnp.broadcast_to(s0_ref[0, k:k + 1, :], (8, 2 * HALF))
            sbuf[8:tile + 8, :] = jnp.dot(u.astype(_MXU), wb_ref[k], preferred_element_type=F32)
            _fwd_scan_block(sbuf, 8, n8, ar, ai, s0_ref[0, k:k + 1, 0:HALF], s0_ref[0, k:k + 1, HALF:2 * HALF])
            dyb = dyk.astype(_MXU)
            gbuf[...] = lax.dot_general(dyb, wc_ref[k], (((1,), (1,)), ((), ())), preferred_element_type=F32)
            dwc_ref[k] += lax.dot_general(sbuf[8:tile + 8, :].astype(_MXU), dyb, (((0,), (0,)), ((), ())),
                                          preferred_element_type=F32)
            steps, carry_m = _scan_tables(ar, ai, True)
            row = lax.broadcasted_iota(jnp.int32, (8, HALF), 0)

            def body(jj, carry):
                cr, ci, dar, dai = carry
                j = n8 - 1 - jj
                r0 = pl.multiple_of(j * 8, 8)
                xr = gbuf[pl.ds(r0, 8), 0:HALF]
                xi = gbuf[pl.ds(r0, 8), HALF:2 * HALF]
                xr, xi = _tile_scan_rev(xr, xi, cr, ci, steps, carry_m)
                gbuf[pl.ds(r0, 8), 0:HALF] = xr
                gbuf[pl.ds(r0, 8), HALF:2 * HALF] = xi
                r1 = pl.multiple_of(j * 8 + 8, 8)
                spr = jnp.where(row == 0, sbuf[pl.ds(r0, 8), 0:HALF][7:8],
                                pltpu.roll(sbuf[pl.ds(r1, 8), 0:HALF], 1, 0))
                spi = jnp.where(row == 0, sbuf[pl.ds(r0, 8), HALF:2 * HALF][7:8],
                                pltpu.roll(sbuf[pl.ds(r1, 8), HALF:2 * HALF], 1, 0))
                dar = dar + xr * spr + xi * spi
                dai = dai + xi * spr - xr * spi
                return xr[0:1], xi[0:1], dar, dai

            z8 = jnp.zeros((8, HALF), F32)
            cr, ci, dar, dai = lax.fori_loop(
                0, n8, body, (lam_ref[k:k + 1, 0:HALF], lam_ref[k:k + 1, HALF:2 * HALF], z8, z8))
            lam_ref[k:k + 1, 0:HALF] = cr
            lam_ref[k:k + 1, HALF:2 * HALF] = ci
            da_ref[k:k + 1, 0:HALF] += jnp.sum(dar, axis=0, keepdims=True)
            da_ref[k:k + 1, HALF:2 * HALF] += jnp.sum(dai, axis=0, keepdims=True)
            lam = gbuf[...].astype(_MXU)
            dwb_ref[k] += lax.dot_general(u.astype(_MXU), lam, (((0,), (0,)), ((), ())), preferred_element_type=F32)
            du = lax.dot_general(lam, wb_ref[k], (((1,), (1,)), ((), ())), preferred_element_type=F32)
            dh_ref[:, cols] = du + d_ref[:, cols] * dyk
            dd_ref[:, cols] += jnp.sum(dyk * u, axis=0, keepdims=True)

    full = lambda a: pl.BlockSpec(a.shape, functools.partial(lambda i, nd_: (0,) * nd_, nd_=a.ndim))
    fullo = lambda shp: pl.BlockSpec(shp, functools.partial(lambda i, nd_: (0,) * nd_, nd_=len(shp)))
    rev = lambda i: (n_t - 1 - i, 0)
    return pl.pallas_call(
        kern,
        out_shape=[jax.ShapeDtypeStruct((s, D), F32), jax.ShapeDtypeStruct(wb.shape, F32),
                   jax.ShapeDtypeStruct(wc.shape, F32), jax.ShapeDtypeStruct((NBLK, 2 * HALF), F32),
                   jax.ShapeDtypeStruct((1, D), F32)],
        grid=(n_t,),
        in_specs=[pl.BlockSpec((tile, D), rev), pl.BlockSpec((tile, D), rev),
                  pl.BlockSpec((1, NBLK, 2 * HALF), lambda i: (n_t - 1 - i, 0, 0)),
                  full(wb), full(wc), full(a_tab), full(dskip)],
        out_specs=[pl.BlockSpec((tile, D), rev), fullo(wb.shape), fullo(wc.shape), fullo((NBLK, 2 * HALF)),
                   fullo((1, D))],
        scratch_shapes=[pltpu.VMEM((NBLK, 2 * HALF), F32), pltpu.VMEM((tile + 8, 2 * HALF), F32),
                        pltpu.VMEM((tile, 2 * HALF), F32)],
        name=name, compiler_params=_params(1))(h, dy, s0, wb, wc, a_tab, dskip)


def _chunk_mask(q0, k0, tq, tk):
    r = (q0 + lax.broadcasted_iota(jnp.int32, (tq, tk), 0)) // CHUNK
    c = (k0 + lax.broadcasted_iota(jnp.int32, (tq, tk), 1)) // CHUNK
    return r >= c


def _head_lanes(j):
    lane = _lane(2 * DV)
    return (lane >= DV * j) & (lane < DV * (j + 1))


def _raw_scores(q, kblk, masked, t):
    s = lax.dot_general(q, kblk, (((1,), (1,)), ((), ())), preferred_element_type=F32)
    return jnp.where(_chunk_mask(0, 0, t, t), s, -1e30) if masked else s


def attn_fwd(name, q, k, v, t=TILE_ATT):
    s = q.shape[0]
    n_q = s // t

    def kern(q_ref, k_ref, v_ref, o_ref, lse_ref):
        qi = pl.program_id(1)
        qs = [q_ref[:, HD * j:HD * (j + 1)] for j in range(2)]

        def scores(k0):
            return tuple(_raw_scores(qs[j], k_ref[pl.ds(k0, t), HD * j:HD * (j + 1)], False, t) for j in range(2))

        def absorb(k0, scs, carry):
            vblk = v_ref[pl.ds(k0, t), :]
            m_new = [jnp.maximum(carry[j][0], jnp.max(scs[j], axis=-1, keepdims=True)) for j in range(2)]
            ps = [jnp.exp2((scs[j] - m_new[j]) * EXP2_SCALE) for j in range(2)]
            alphas = [jnp.exp2((carry[j][0] - m_new[j]) * EXP2_SCALE) for j in range(2)]
            pvs = [jnp.dot(ps[j].astype(_MXU), vblk, preferred_element_type=F32) for j in range(2)]
            return tuple((m_new[j], alphas[j] * carry[j][1] + jnp.sum(ps[j], axis=-1, keepdims=True),
                          alphas[j] * carry[j][2] + pvs[j]) for j in range(2))

        def step(kb, state):
            scs, carry = state
            nxt = scores(pl.multiple_of((kb + 1) * t, t))
            return nxt, absorb(pl.multiple_of(kb * t, t), scs, carry)

        init = tuple((jnp.full((t, 1), -1e30, F32), jnp.zeros((t, 1), F32), jnp.zeros((t, 2 * DV), F32))
                     for _ in range(2))
        scs, carry = lax.fori_loop(0, qi, step, (scores(0), init))
        mask = _chunk_mask(0, 0, t, t)
        carry = absorb(pl.multiple_of(qi * t, t), tuple(jnp.where(mask, sc, -1e30) for sc in scs), carry)
        outs = []
        for j in range(2):
            m, l, acc = carry[j]
            outs.append(acc / l)
            lse_ref[0, j] = m * ATTN_SCALE + jnp.log(l)
        o_ref[...] = jnp.where(_head_lanes(0), outs[0], outs[1])

    return pl.pallas_call(
        kern,
        out_shape=[jax.ShapeDtypeStruct((s, H * DV), F32), jax.ShapeDtypeStruct((HP, 2, s, 1), F32)],
        grid=(HP, n_q),
        in_specs=[pl.BlockSpec((t, 2 * HD), lambda hp, i: (i, hp)), pl.BlockSpec((s, 2 * HD), lambda hp, i: (0, hp)),
                  pl.BlockSpec((s, 2 * DV), lambda hp, i: (0, hp))],
        out_specs=[pl.BlockSpec((t, 2 * DV), lambda hp, i: (i, hp)),
                   pl.BlockSpec((1, 2, t, 1), lambda hp, i: (hp, 0, i, 0))],
        name=name, compiler_params=_params(2))(q, k, v)


def attn_bwd(name, q, k, v, o, do, lse, t=TILE_ATT):
    s = q.shape[0]
    n_q = s // t

    def kern(q_ref, k_ref, v_ref, o_ref, do_ref, lse_ref, dq_ref, dk_ref, dv_ref):
        qi = pl.program_id(1)

        @pl.when(qi == 0)
        def _():
            dk_ref[...] = jnp.zeros(dk_ref.shape, F32)
            dv_ref[...] = jnp.zeros(dv_ref.shape, F32)

        qs, doms, deltas, lse2 = [], [], [], []
        for j in range(2):
            qs.append(q_ref[:, HD * j:HD * (j + 1)])
            dom = jnp.where(_head_lanes(j), do_ref[...], 0.0)
            deltas.append(jnp.sum(dom * o_ref[...], axis=-1, keepdims=True))
            doms.append(dom.astype(_MXU))
            lse2.append(lse_ref[0, j] * LOG2E)

        def block(k0, dqs, masked):
            vblk = v_ref[pl.ds(k0, t), :]
            kblks = [k_ref[pl.ds(k0, t), HD * j:HD * (j + 1)] for j in range(2)]
            scs = [_raw_scores(qs[j], kblks[j], masked, t) for j in range(2)]
            dps = [lax.dot_general(doms[j], vblk, (((1,), (1,)), ((), ())), preferred_element_type=F32)
                   for j in range(2)]
            ps = [jnp.exp2(scs[j] * EXP2_SCALE - lse2[j]) for j in range(2)]
            dss = [(ps[j] * (dps[j] - deltas[j])).astype(_MXU) for j in range(2)]
            pbs = [ps[j].astype(_MXU) for j in range(2)]
            new = tuple(dqs[j] + jnp.dot(dss[j], kblks[j], preferred_element_type=F32) for j in range(2))
            for j in range(2):
                dk_ref[pl.ds(k0, t), HD * j:HD * (j + 1)] += lax.dot_general(
                    dss[j], qs[j], (((0,), (0,)), ((), ())), preferred_element_type=F32)
            dvs = [lax.dot_general(pbs[j], doms[j], (((0,), (0,)), ((), ())), preferred_element_type=F32)
                   for j in range(2)]
            dv_ref[pl.ds(k0, t), :] += dvs[0] + dvs[1]
            return new

        init = (jnp.zeros((t, HD), F32), jnp.zeros((t, HD), F32))
        dqs = lax.fori_loop(0, qi, lambda kb, c: block(pl.multiple_of(kb * t, t), c, False), init)
        dqs = block(pl.multiple_of(qi * t, t), dqs, True)
        for j in range(2):
            dq_ref[:, HD * j:HD * (j + 1)] = dqs[j] * ATTN_SCALE

        @pl.when(qi == n_q - 1)
        def _():
            dk_ref[...] = dk_ref[...] * ATTN_SCALE

    return pl.pallas_call(
        kern,
        out_shape=[jax.ShapeDtypeStruct((s, H * HD), F32), jax.ShapeDtypeStruct((s, H * HD), F32),
                   jax.ShapeDtypeStruct((s, H * DV), F32)],
        grid=(HP, n_q),
        in_specs=[pl.BlockSpec((t, 2 * HD), lambda hp, i: (i, hp)), pl.BlockSpec((s, 2 * HD), lambda hp, i: (0, hp)),
                  pl.BlockSpec((s, 2 * DV), lambda hp, i: (0, hp)), pl.BlockSpec((t, 2 * DV), lambda hp, i: (i, hp)),
                  pl.BlockSpec((t, 2 * DV), lambda hp, i: (i, hp)),
                  pl.BlockSpec((1, 2, t, 1), lambda hp, i: (hp, 0, i, 0))],
        out_specs=[pl.BlockSpec((t, 2 * HD), lambda hp, i: (i, hp)), pl.BlockSpec((s, 2 * HD), lambda hp, i: (0, hp)),
                   pl.BlockSpec((s, 2 * DV), lambda hp, i: (0, hp))],
        name=name, compiler_params=_params(2))(q, k, v, o, do, lse)


def rope_tables(name, pos_col, inv128):
    s = pos_col.shape[0]

    def kern(p_ref, inv_ref, c_ref, s_ref):
        ang = p_ref[...].astype(F32) * inv_ref[...]
        lane = _lane()
        m_r = (lane >= DN) & (lane < DN + DR)
        c_ref[...] = jnp.where(lane < DN, 1.0, jnp.where(m_r, jnp.cos(ang), 0.0))
        s_ref[...] = jnp.where(m_r, jnp.sin(ang), 0.0)

    return _whole(kern, name, [jax.ShapeDtypeStruct((s, HD), F32)] * 2, pos_col, inv128)


def loss_kernel(name, y, tgt, tile=TILE_ROW):
    def body(row_v, _):
        err = row_v[0] - row_v[1]
        part = 0.5 * jnp.sum(jnp.mean(err * err, axis=-1, keepdims=True), axis=0, keepdims=True)
        return [err * (1.0 / D)], [jnp.broadcast_to(part, (1, 128))]

    return _row_call(name, body, [y, tgt], [], [(D, F32)], [((1, 128), F32)], tile)


def _row_tile(r, c):
    cap = max(8, (1 << 18) // max(c, 1))
    for t in (2048, 1024, 512, 256, 128, 64, 32, 16, 8):
        if t <= cap and r % t == 0:
            return t
    return r


def sum_parts(name, parts):
    n, r, c = parts.shape
    t = _row_tile(r, c)

    def kern(p_ref, o_ref):
        acc = p_ref[0].astype(F32)
        for i in range(1, n):
            acc = acc + p_ref[i].astype(F32)
        o_ref[...] = acc

    return pl.pallas_call(kern, out_shape=jax.ShapeDtypeStruct((r, c), F32), grid=(r // t,),
                          in_specs=[pl.BlockSpec((n, t, c), lambda i: (0, i, 0))],
                          out_specs=pl.BlockSpec((t, c), lambda i: (i, 0)), name=name, compiler_params=_params(1))(parts)


def adamw(name, parts, w, m, v, base=0, stride=0):
    n, _, cp = parts.shape
    nl, r, c = w.shape
    t = _row_tile(math.gcd(math.gcd(r, base), stride), max(c, cp))
    c1 = 1.0 / (1.0 - ADAM_B1 ** ADAM_STEP)
    c2 = 1.0 / (1.0 - ADAM_B2 ** ADAM_STEP)

    def kern(p_ref, w_ref, m_ref, v_ref, g_ref, d_ref, nm_ref, nv_ref):
        g = p_ref[0].astype(F32)
        for i in range(1, n):
            g = g + p_ref[i].astype(F32)
        g = g[:, :c]
        nm = ADAM_B1 * m_ref[...] + (1.0 - ADAM_B1) * g
        nv = ADAM_B2 * v_ref[...] + (1.0 - ADAM_B2) * (g * g)
        g_ref[...] = g
        nm_ref[...] = nm
        nv_ref[...] = nv
        d_ref[...] = -ADAM_LR * ((nm * c1) / (jnp.sqrt(nv * c2) + ADAM_EPS) + ADAM_WD * w_ref[...])

    spec = pl.BlockSpec((None, t, c), lambda l, i: (l, i, 0))
    pspec = pl.BlockSpec((n, t, cp), lambda l, i: (0, (base + l * stride) // t + i, 0))
    return pl.pallas_call(kern, out_shape=[jax.ShapeDtypeStruct((nl, r, c), F32)] * 4, grid=(nl, r // t),
                          in_specs=[pspec, spec, spec, spec], out_specs=[spec] * 4, name=name,
                          compiler_params=_params(2))(parts, w, m, v)


def _me():
    return lax.axis_index("x"), lax.axis_index("y"), lax.axis_index("c")


def _flip(x, y, c, mask):
    return (jnp.where((mask >> 2) & 1, 1 - x, x), jnp.where((mask >> 1) & 1, 1 - y, y), jnp.where(mask & 1, 1 - c, c))


def _index(x, y, c):
    return 4 * x + 2 * y + c


def _exchange(name, arr, gather):
    out_shape = (N_DEV,) + arr.shape if gather else arr.shape

    def kern(in_ref, out_ref, send_sems, recv_sems, local_sem):
        x, y, c = _me()
        me = _index(x, y, c)
        mine = pltpu.make_async_copy(in_ref if gather else in_ref.at[me], out_ref.at[me], local_sem)
        mine.start()
        copies = []
        for mask in range(1, N_DEV):
            px, py, pc = _flip(x, y, c, mask)
            peer = _index(px, py, pc)
            cp = pltpu.make_async_remote_copy(
                src_ref=in_ref if gather else in_ref.at[peer], dst_ref=out_ref.at[me],
                send_sem=send_sems.at[mask - 1], recv_sem=recv_sems.at[mask - 1],
                device_id=(px, py, pc), device_id_type=MESH)
            cp.start()
            copies.append((cp, peer))
        for mask, (cp, peer) in enumerate(copies, start=1):
            pltpu.make_async_remote_copy(
                src_ref=in_ref if gather else in_ref.at[peer], dst_ref=out_ref.at[peer],
                send_sem=send_sems.at[mask - 1], recv_sem=recv_sems.at[mask - 1],
                device_id=_flip(x, y, c, mask), device_id_type=MESH).wait_recv()
        for cp, _ in copies:
            cp.wait_send()
        mine.wait()

    any_spec = pl.BlockSpec(memory_space=pl.ANY)
    return pl.pallas_call(
        kern, out_shape=jax.ShapeDtypeStruct(out_shape, arr.dtype), in_specs=[any_spec], out_specs=any_spec,
        scratch_shapes=[pltpu.SemaphoreType.DMA((N_DEV - 1,)), pltpu.SemaphoreType.DMA((N_DEV - 1,)),
                        pltpu.SemaphoreType.DMA],
        name=name, compiler_params=pltpu.CompilerParams(has_side_effects=True))(arr)


def all_gather(name, arr):
    return _exchange(name, arr, True)


def all_to_all(name, arr):
    return _exchange(name, arr, False)


def _pad_heads(w, real, padded):
    k = w.shape[0]
    w3 = w.reshape(k, H, real)
    return jnp.pad(w3, ((0, 0), (0, 0), (0, padded - real))).reshape(k, H * padded)


def _unpad_heads(w, real, padded):
    k = w.shape[0]
    return w.reshape(k, H, padded)[:, :, :real].reshape(k, H * real)


def _s5_place(ab_re, ab_im, bb_re_t, bb_im_t, c_re, c_im):
    eye = jnp.eye(GB, dtype=F32)

    def wb_part(bt):
        x4 = bt.reshape(P, NBLK, GB, N).transpose(1, 2, 0, 3)
        return jnp.einsum('kgpn,gh->kgphn', x4, eye).reshape(NBLK, GB * P, HALF)

    def wc_part(cc):
        x4 = cc.reshape(NBLK, GB, P, N)
        return jnp.einsum('kgpn,gh->kgnhp', x4, eye).reshape(NBLK, HALF, GB * P)

    wb = jnp.concatenate([wb_part(bb_re_t), wb_part(bb_im_t)], axis=-1)
    wc = jnp.concatenate([wc_part(c_re), -wc_part(c_im)], axis=1)
    a_tab = jnp.concatenate([ab_re.reshape(NBLK, 1, HALF), ab_im.reshape(NBLK, 1, HALF)], axis=-1)
    return wb.astype(_MXU), wc.astype(_MXU), a_tab


def _s5_unplace(dwb, dwc, da):
    eye = jnp.eye(GB, dtype=F32)

    def wb_part(dpart):
        x5 = dpart.reshape(NBLK, GB, P, GB, N)
        return jnp.einsum('kgphn,gh->kgpn', x5, eye).transpose(2, 0, 1, 3).reshape(P, G * N)

    def wc_part(dpart):
        x5 = dpart.reshape(NBLK, GB, N, GB, P)
        return jnp.einsum('kgnhp,gh->kgpn', x5, eye).reshape(G, P, N)

    dbb_re_t, dbb_im_t = wb_part(dwb[..., :HALF]), wb_part(dwb[..., HALF:])
    dc_re, dc_im = wc_part(dwc[:, :HALF]), -wc_part(dwc[:, HALF:])
    dab_re, dab_im = da[:, :HALF].reshape(1, G * N), da[:, HALF:].reshape(1, G * N)
    return dab_re, dab_im, dbb_re_t, dbb_im_t, dc_re, dc_im


def _row(v):
    return v.reshape(1, -1)


def kernel(x, c, positions, ada_w, ada_b, norm1_g, norm2_g, ffn_w_gate, ffn_w_up, ffn_w_down, s5_lam_re, s5_lam_im, s5_log_dt, s5_b_re, s5_b_im, s5_c_re, s5_c_im, s5_d, s5_w_glu, s5_b_glu, kv_ada_w, kv_ada_b, kv_norm_g, w_kv_a, kv_a_norm_g, w_kv_b, k_nope_norm_g, k_rope_norm_g, mla_w_dq, mla_q_norm_g, mla_w_uq, mla_q_nope_norm_g, mla_q_rope_norm_g, mla_w_o, loss_target, m_ada_w, m_ada_b, m_norm1_g, m_norm2_g, m_ffn_w_gate, m_ffn_w_up, m_ffn_w_down, m_s5_lam_re, m_s5_lam_im, m_s5_log_dt, m_s5_b_re, m_s5_b_im, m_s5_c_re, m_s5_c_im, m_s5_d, m_s5_w_glu, m_s5_b_glu, m_kv_ada_w, m_kv_ada_b, m_kv_norm_g, m_w_kv_a, m_kv_a_norm_g, m_w_kv_b, m_k_nope_norm_g, m_k_rope_norm_g, m_mla_w_dq, m_mla_q_norm_g, m_mla_w_uq, m_mla_q_nope_norm_g, m_mla_q_rope_norm_g, m_mla_w_o, v_ada_w, v_ada_b, v_norm1_g, v_norm2_g, v_ffn_w_gate, v_ffn_w_up, v_ffn_w_down, v_s5_lam_re, v_s5_lam_im, v_s5_log_dt, v_s5_b_re, v_s5_b_im, v_s5_c_re, v_s5_c_im, v_s5_d, v_s5_w_glu, v_s5_b_glu, v_kv_ada_w, v_kv_ada_b, v_kv_norm_g, v_w_kv_a, v_kv_a_norm_g, v_w_kv_b, v_k_nope_norm_g, v_k_rope_norm_g, v_mla_w_dq, v_mla_q_norm_g, v_mla_w_uq, v_mla_q_nope_norm_g, v_mla_q_rope_norm_g, v_mla_w_o):
    W = dict(ada_w=ada_w, ada_b=ada_b, norm1_g=norm1_g, norm2_g=norm2_g, ffn_w_gate=ffn_w_gate, ffn_w_up=ffn_w_up, ffn_w_down=ffn_w_down, s5_lam_re=s5_lam_re, s5_lam_im=s5_lam_im, s5_log_dt=s5_log_dt, s5_b_re=s5_b_re, s5_b_im=s5_b_im, s5_c_re=s5_c_re, s5_c_im=s5_c_im, s5_d=s5_d, s5_w_glu=s5_w_glu, s5_b_glu=s5_b_glu, kv_ada_w=kv_ada_w, kv_ada_b=kv_ada_b, kv_norm_g=kv_norm_g, w_kv_a=w_kv_a, kv_a_norm_g=kv_a_norm_g, w_kv_b=w_kv_b, k_nope_norm_g=k_nope_norm_g, k_rope_norm_g=k_rope_norm_g, mla_w_dq=mla_w_dq, mla_q_norm_g=mla_q_norm_g, mla_w_uq=mla_w_uq, mla_q_nope_norm_g=mla_q_nope_norm_g, mla_q_rope_norm_g=mla_q_rope_norm_g, mla_w_o=mla_w_o)
    M = dict(ada_w=m_ada_w, ada_b=m_ada_b, norm1_g=m_norm1_g, norm2_g=m_norm2_g, ffn_w_gate=m_ffn_w_gate, ffn_w_up=m_ffn_w_up, ffn_w_down=m_ffn_w_down, s5_lam_re=m_s5_lam_re, s5_lam_im=m_s5_lam_im, s5_log_dt=m_s5_log_dt, s5_b_re=m_s5_b_re, s5_b_im=m_s5_b_im, s5_c_re=m_s5_c_re, s5_c_im=m_s5_c_im, s5_d=m_s5_d, s5_w_glu=m_s5_w_glu, s5_b_glu=m_s5_b_glu, kv_ada_w=m_kv_ada_w, kv_ada_b=m_kv_ada_b, kv_norm_g=m_kv_norm_g, w_kv_a=m_w_kv_a, kv_a_norm_g=m_kv_a_norm_g, w_kv_b=m_w_kv_b, k_nope_norm_g=m_k_nope_norm_g, k_rope_norm_g=m_k_rope_norm_g, mla_w_dq=m_mla_w_dq, mla_q_norm_g=m_mla_q_norm_g, mla_w_uq=m_mla_w_uq, mla_q_nope_norm_g=m_mla_q_nope_norm_g, mla_q_rope_norm_g=m_mla_q_rope_norm_g, mla_w_o=m_mla_w_o)
    V = dict(ada_w=v_ada_w, ada_b=v_ada_b, norm1_g=v_norm1_g, norm2_g=v_norm2_g, ffn_w_gate=v_ffn_w_gate, ffn_w_up=v_ffn_w_up, ffn_w_down=v_ffn_w_down, s5_lam_re=v_s5_lam_re, s5_lam_im=v_s5_lam_im, s5_log_dt=v_s5_log_dt, s5_b_re=v_s5_b_re, s5_b_im=v_s5_b_im, s5_c_re=v_s5_c_re, s5_c_im=v_s5_c_im, s5_d=v_s5_d, s5_w_glu=v_s5_w_glu, s5_b_glu=v_s5_b_glu, kv_ada_w=v_kv_ada_w, kv_ada_b=v_kv_ada_b, kv_norm_g=v_kv_norm_g, w_kv_a=v_w_kv_a, kv_a_norm_g=v_kv_a_norm_g, w_kv_b=v_w_kv_b, k_nope_norm_g=v_k_nope_norm_g, k_rope_norm_g=v_k_rope_norm_g, mla_w_dq=v_mla_w_dq, mla_q_norm_g=v_mla_q_norm_g, mla_w_uq=v_mla_w_uq, mla_q_nope_norm_g=v_mla_q_nope_norm_g, mla_q_rope_norm_g=v_mla_q_rope_norm_g, mla_w_o=v_mla_w_o)
    return _step(x[0], c, positions, loss_target[0], W, M, V)


WEIGHT_NAMES = ['ada_w', 'ada_b', 'norm1_g', 'norm2_g', 'ffn_w_gate', 'ffn_w_up', 'ffn_w_down', 's5_lam_re', 's5_lam_im', 's5_log_dt', 's5_b_re', 's5_b_im', 's5_c_re', 's5_c_im', 's5_d', 's5_w_glu', 's5_b_glu', 'kv_ada_w', 'kv_ada_b', 'kv_norm_g', 'w_kv_a', 'kv_a_norm_g', 'w_kv_b', 'k_nope_norm_g', 'k_rope_norm_g', 'mla_w_dq', 'mla_q_norm_g', 'mla_w_uq', 'mla_q_nope_norm_g', 'mla_q_rope_norm_g', 'mla_w_o']
REPLICATED = ['ada_b', 'norm1_g', 'norm2_g', 's5_lam_re', 's5_lam_im', 's5_log_dt', 's5_b_re', 's5_b_im', 's5_c_re', 's5_c_im', 'kv_ada_b', 'kv_norm_g', 'kv_a_norm_g', 'k_nope_norm_g', 'k_rope_norm_g', 'mla_q_norm_g', 'mla_q_nope_norm_g', 'mla_q_rope_norm_g']
SHARDED_VEC = ['s5_d', 's5_b_glu']


def _step(x, c, positions, target, W, M, V):
    s = x.shape[0]
    me = _index(*_me())
    mxu = lambda a: a.astype(_MXU)

    pad_c = lambda a: jnp.pad(a, ((0, 0), (0, 0), (0, FFB - FF // N_DEV)))
    loc1 = jnp.concatenate([mxu(pad_c(W['ffn_w_gate'])).reshape(DEPTH * D, FFB),
                            mxu(pad_c(W['ffn_w_up'])).reshape(DEPTH * D, FFB)], axis=0)
    loc2 = jnp.concatenate([mxu(jnp.pad(W['ffn_w_down'], ((0, 0), (0, FFB - FF // N_DEV), (0, 0)))).reshape(DEPTH * FFB, D),
                            mxu(W['s5_w_glu']).reshape(N_A * D // N_DEV, D),
                            mxu(W['mla_w_o']).reshape(2 * D // N_DEV, D)], axis=0)
    loc3 = jnp.concatenate([mxu(W['w_kv_b']), mxu(W['mla_w_dq']).reshape(2 * D // N_DEV, QL)], axis=0)
    g1 = all_gather("gather_w384", loc1)
    g2 = all_gather("gather_w1024", loc2)
    g3 = all_gather("gather_w256", loc3)
    g4 = all_gather("gather_kva", mxu(W['w_kv_a']))
    g5 = all_gather("gather_uq", mxu(W['mla_w_uq']).reshape(2 * QL, H * (DN + DR) // N_DEV))
    cols = lambda g: g.transpose(1, 0, 2).reshape(g.shape[1], N_DEV * g.shape[2])
    rows = lambda g: g.reshape(N_DEV * g.shape[1], g.shape[2])
    wg_full = [cols(g1[:, D * l:D * (l + 1)]) for l in range(DEPTH)]
    wu_full = [cols(g1[:, D * (DEPTH + l):D * (DEPTH + l + 1)]) for l in range(DEPTH)]
    wd_full = [rows(g2[:, FFB * l:FFB * (l + 1)]) for l in range(DEPTH)]
    o2 = DEPTH * FFB
    wglu_full = [rows(g2[:, o2 + 128 * l:o2 + 128 * (l + 1)]) for l in range(N_A)]
    wo_full = [rows(g2[:, o2 + 256 + 128 * j:o2 + 256 + 128 * (j + 1)]) for j in range(2)]
    wkvb_full = cols(g3[:, :KVL])
    wdq_full = [rows(g3[:, KVL + 128 * j:KVL + 128 * (j + 1)]) for j in range(2)]
    wkva_full = rows(g4)
    wuq_full = [cols(g5[:, QL * j:QL * (j + 1)]) for j in range(2)]

    vec = jnp.concatenate([c.reshape(-1), W['s5_d'].reshape(-1), W['s5_b_glu'].reshape(-1)]).reshape(1, -1)
    vec = jnp.pad(vec, ((0, 7), (0, 0)))
    gv = all_gather("gather_vectors", vec)[:, 0, :]
    c_all = gv[:, :D]
    d_full = jnp.concatenate([gv[d, D:D + 2 * 128].reshape(N_A, 128) for d in range(N_DEV)], axis=1)
    bglu_full = jnp.concatenate([gv[d, D + 256:D + 512].reshape(N_A, 128) for d in range(N_DEV)], axis=1)

    ca_all = jax.nn.silu(c_all)
    w_mod = jnp.concatenate([W['ada_w'][l] for l in range(DEPTH)] + [W['kv_ada_w']], axis=1)
    n_mod = w_mod.shape[1]
    mod_cols = small_matmul("mod_matmul", ca_all, w_mod)
    gm = all_gather("gather_mod", mod_cols)
    mine = lax.dynamic_index_in_dim(gm, me, axis=1, keepdims=False)
    per_l = D * 6 // N_DEV
    mods = []
    for l in range(DEPTH):
        full = jnp.concatenate([mine[d, per_l * l:per_l * (l + 1)] for d in range(N_DEV)]) + W['ada_b'][l]
        mods.append([_row(full[D * i:D * (i + 1)]) for i in range(6)])
    kfull = jnp.concatenate([mine[d, per_l * DEPTH:] for d in range(N_DEV)]) + W['kv_ada_b']
    k_shift, k_scale = _row(kfull[:D]), _row(kfull[D:])

    inv = 1.0 / (ROPE_THETA ** (np.arange(0, DR, 2, dtype=np.float32) / DR))
    inv128 = np.zeros((1, HD), np.float32)
    inv128[0, DN:DN + DR // 2] = inv
    inv128[0, DN + DR // 2:DN + DR] = inv
    cosf, sinf = rope_tables("rope_tables", positions.reshape(s, 1), jnp.asarray(inv128))
    zpad = lambda n: jnp.zeros((n,), F32)
    gkn128 = _row(jnp.concatenate([W['k_nope_norm_g'], zpad(HD - DN)]))
    gkr128 = _row(jnp.concatenate([zpad(DN), W['k_rope_norm_g'], zpad(HD - DN - DR)]))
    gq128 = [_row(jnp.concatenate([W['mla_q_nope_norm_g'][j], W['mla_q_rope_norm_g'][j], zpad(HD - DN - DR)]))
             for j in range(2)]
    wa_pad = jnp.concatenate([wkva_full[:, :KVL], jnp.zeros((D, DN), _MXU), wkva_full[:, KVL:],
                              jnp.zeros((D, HD - DN - DR), _MXU)], axis=1)
    wkvb3 = wkvb_full.reshape(KVL, H, DN + DV)
    wkn_pad = jnp.pad(wkvb3[:, :, :DN], ((0, 0), (0, 0), (0, HD - DN))).reshape(KVL, H * HD)
    wv_mat = wkvb3[:, :, DN:].reshape(KVL, H * DV)
    wuq_pad = [_pad_heads(wuq_full[j], DN + DR, HD) for j in range(2)]

    expand = jnp.asarray(np.kron(np.eye(G, dtype=np.float32), np.ones((1, N), np.float32)))
    s5_raw, s5_mats = [], []
    for l in range(N_A):
        raw = (_row(W['s5_lam_re'][l]), _row(W['s5_lam_im'][l]), _row(W['s5_log_dt'][l]),
               W['s5_b_re'][l].transpose(2, 0, 1).reshape(P, G * N), W['s5_b_im'][l].transpose(2, 0, 1).reshape(P, G * N))
        ab_re, ab_im, bb_re_t, bb_im_t = s5_prep_fwd(f"s5_prep_fwd", *raw, expand)
        s5_raw.append(raw)
        s5_mats.append(_s5_place(ab_re, ab_im, bb_re_t, bb_im_t, W['s5_c_re'][l], W['s5_c_im'][l]))

    g1 = [_row(W['norm1_g'][l]) for l in range(DEPTH)]
    g2 = [_row(W['norm2_g'][l]) for l in range(DEPTH)]
    saved = []
    xs = x
    kv = None
    for l in range(DEPTH):
        sh1, sc1, gt1, sh2, sc2, gt2 = mods[l]
        rec = {'x_in': xs}
        if l == N_A:
            kv_smalls = [_row(W['kv_norm_g']), k_shift, k_scale, _row(W['kv_a_norm_g']), gkn128, gkr128]
            k_mat, v_mat = seg_forward("kv_fwd", seg_kv, [xs], kv_smalls, [cosf, sinf], [wa_pad, wkn_pad, wv_mat],
                                       [(H * HD, _MXU), (H * DV, _MXU)], tap_widths=(KVL + HD, H * HD, H * DV))
            kv = {'x_in': xs, 'smalls': kv_smalls, 'k': k_mat, 'v': v_mat}
        if l < N_A:
            (h,) = seg_forward("pre_fwd", seg_pre, [xs], [g1[l], sh1, sc1], [], [], [(D, F32)])
            wb, wc, a_tab = s5_mats[l]
            y, s0 = s5_scan_fwd("s5_scan_fwd", h, wb, wc, a_tab, _row(d_full[l]))
            (x_mid,) = seg_forward("glu_fwd", seg_glu, [xs, y], [gt1, _row(bglu_full[l])], [], [wglu_full[l]],
                                   [(D, F32)], tap_widths=(D,))
            rec.update(h=h, y=y, s0=s0)
        else:
            j = l - N_A
            q_smalls = [g1[l], sh1, sc1, _row(W['mla_q_norm_g'][j]), gq128[j]]
            (q_mat,) = seg_forward("q_fwd", seg_q, [xs], q_smalls, [cosf, sinf], [wdq_full[j], wuq_pad[j]],
                                   [(H * HD, _MXU)], tap_widths=(QL, H * HD))
            o_mat, lse = attn_fwd("attn_fwd", q_mat, kv['k'], kv['v'])
            (x_mid,) = seg_forward("o_fwd", seg_o, [xs, o_mat], [gt1], [], [wo_full[j]], [(D, F32)], tap_widths=(D,))
            rec.update(q=q_mat, o=o_mat, lse=lse, q_smalls=q_smalls)
        rec['x_mid'] = x_mid
        (xs,) = seg_forward("ffn_fwd", seg_ffn, [x_mid], [g2[l], sh2, sc2, gt2], [], [wg_full[l], wu_full[l], wd_full[l]],
                            [(D, F32)], tap_widths=(FFP, FFP, D))
        saved.append(rec)

    dy, loss_part = loss_kernel("loss", xs, target)
    loss = lax.psum(loss_part[0, 0], ("x", "y", "c"))

    gfull = {}
    gsmall = {}
    dmod = [None] * DEPTH
    dk_tot = []
    dv_tot = []
    dx = dy
    gfull_ffn = {'g': [None] * DEPTH, 'u': [None] * DEPTH, 'd': [None] * DEPTH}
    g_n1 = [None] * DEPTH
    g_n2 = [None] * DEPTH
    g_glu = [None] * N_A
    g_bglu = [None] * N_A
    g_dskip = [None] * N_A
    g_s5 = [None] * N_A
    g_dq, g_uq, g_wo, g_qn, g_q128 = [None] * 2, [None] * 2, [None] * 2, [None] * 2, [None] * 2
    for l in range(DEPTH - 1, -1, -1):
        rec = saved[l]
        sh1, sc1, gt1, sh2, sc2, gt2 = mods[l]
        (dx,), (dgate, dup, dyd), (h_b, a_b), (dg2, dsh2, dsc2, dgt2) = seg_backward(
            "ffn_bwd", seg_ffn, [rec['x_mid']], [g2[l], sh2, sc2, gt2], [], [wg_full[l], wu_full[l], wd_full[l]],
            [dx], (FFP, FFP, D), (D, FFP), tile=TILE_ROW // 2)
        gfull_ffn['g'][l] = matmul_tn("tn_ffn_in", h_b, dgate, _MXU, col_blocks=N_DEV)
        gfull_ffn['u'][l] = matmul_tn("tn_ffn_in", h_b, dup, _MXU, col_blocks=N_DEV)
        gfull_ffn['d'][l] = matmul_tn("tn_ffn_out", a_b, dyd, _MXU).reshape(N_DEV, FFB, D)
        g_n2[l] = dg2
        if l < N_A:
            (dx, dyy), (dz,), (g_b,), (dgt1, dbg) = seg_backward(
                "glu_bwd", seg_glu, [rec['x_in'], rec['y']], [gt1, _row(bglu_full[l])], [], [wglu_full[l]],
                [dx], (D,), (D,))
            g_glu[l] = matmul_tn("tn_sq", g_b, dz, _MXU)
            g_bglu[l] = dbg
            wb, wc, a_tab = s5_mats[l]
            dh, dwb, dwc, da, dd = s5_scan_bwd("s5_scan_bwd", rec['h'], dyy, rec['s0'], wb, wc, a_tab, _row(d_full[l]))
            g_dskip[l] = dd
            dab_re, dab_im, dbb_re_t, dbb_im_t, dc_re, dc_im = _s5_unplace(dwb, dwc, da)
            dlr, dli, dldt, dbr_t, dbi_t = s5_prep_bwd("s5_prep_bwd", *s5_raw[l], expand,
                                                       (dab_re, dab_im, dbb_re_t, dbb_im_t))
            g_s5[l] = (dlr.reshape(G, N), dli.reshape(G, N), dldt.reshape(G),
                       dbr_t.reshape(P, G, N).transpose(1, 2, 0), dbi_t.reshape(P, G, N).transpose(1, 2, 0), dc_re, dc_im)
            (dx,), _, _, (dg1, dsh1, dsc1) = seg_backward(
                "pre_bwd", seg_pre, [rec['x_in']], [g1[l], sh1, sc1], [], [], [dh], (), (), dx_add=dx)
        else:
            j = l - N_A
            (dx, do), (dzo,), (o_b,), (dgt1,) = seg_backward(
                "o_bwd", seg_o, [rec['x_in'], rec['o']], [gt1], [], [wo_full[j]], [dx], (D,), (D,))
            g_wo[j] = matmul_tn("tn_sq", o_b, dzo, _MXU)
            dq, dk, dv = attn_bwd("attn_bwd", rec['q'], kv['k'], kv['v'], rec['o'], do, rec['lse'])
            dk_tot.append(dk)
            dv_tot.append(dv)
            (dx,), (dql, dqq), (hq_b, qn_b), (dg1, dsh1, dsc1, dqg, dq128) = seg_backward(
                "q_bwd", seg_q, [rec['x_in']], rec['q_smalls'], [cosf, sinf], [wdq_full[j], wuq_pad[j]],
                [dq], (QL, H * HD), (D, QL), dx_add=dx)
            g_dq[j] = matmul_tn("tn_dq", hq_b, dql, _MXU)
            g_uq[j] = _unpad_heads(matmul_tn("tn_uq", qn_b, dqq, _MXU), DN + DR, HD)
            g_qn[j], g_q128[j] = dqg, dq128
        g_n1[l] = dg1
        dmod[l] = jnp.concatenate([dsh1, dsc1, dgt1, dsh2, dsc2, dgt2], axis=1)
        if l == N_A:
            dkk = sum_parts("sum_dk", jnp.stack(dk_tot))
            dvv = sum_parts("sum_dv", jnp.stack(dv_tot))
            (dx,), (dta, dtk, dtv), (hk_b, ckv_b), (dkg, dksh, dksc, dag, dgkn, dgkr) = seg_backward(
                "kv_bwd", seg_kv, [kv['x_in']], kv['smalls'], [cosf, sinf], [wa_pad, wkn_pad, wv_mat],
                [dkk, dvv], (KVL + HD, H * HD, H * DV), (D, KVL), dx_add=dx)
            g_wa = matmul_tn("tn_kva", hk_b, dta, _MXU)
            g_wa = jnp.concatenate([g_wa[:, :KVL], g_wa[:, KVL + DN:KVL + DN + DR]], axis=1)
            g_kn = matmul_tn("tn_kn", ckv_b, dtk, _MXU).reshape(KVL, H, HD)[:, :, :DN]
            g_v = matmul_tn("tn_v", ckv_b, dtv, _MXU).reshape(KVL, H, DV)
            g_wkvb = jnp.concatenate([g_kn, g_v], axis=2).reshape(KVL, H * (DN + DV))
            dkmod = jnp.concatenate([dksh, dksc], axis=1)
    grad_x = dx

    dm = jnp.concatenate(dmod + [dkmod], axis=1)[0]
    per_dev = []
    for d in range(N_DEV):
        cols = [dm[6 * D * l + per_l * d:6 * D * l + per_l * (d + 1)] for l in range(DEPTH)]
        cols.append(dm[6 * D * DEPTH + (2 * D // N_DEV) * d:6 * D * DEPTH + (2 * D // N_DEV) * (d + 1)])
        per_dev.append(jnp.concatenate(cols))
    dm_dev = jnp.stack(per_dev)
    gdm = all_gather("gather_dmod", dm_dev)
    dm_mine = lax.dynamic_index_in_dim(gdm, me, axis=1, keepdims=False)
    g_wmod = small_matmul_tn("dmod_matmul", ca_all, dm_mine)
    g_ada_w = jnp.stack([g_wmod[:, per_l * l:per_l * (l + 1)] for l in range(DEPTH)])
    g_kv_ada_w = g_wmod[:, per_l * DEPTH:]
    dm_sum = sum_parts("sum_dmod", gdm.reshape(N_DEV, N_DEV, n_mod))
    g_ada_b = jnp.stack([jnp.concatenate([dm_sum[d, per_l * l:per_l * (l + 1)] for d in range(N_DEV)])
                         for l in range(DEPTH)])
    g_kv_ada_b = jnp.concatenate([dm_sum[d, per_l * DEPTH:] for d in range(N_DEV)])

    small = {
        'norm1_g': jnp.concatenate(g_n1, axis=0), 'norm2_g': jnp.concatenate(g_n2, axis=0),
        's5_lam_re': jnp.stack([g_s5[l][0] for l in range(N_A)]), 's5_lam_im': jnp.stack([g_s5[l][1] for l in range(N_A)]),
        's5_log_dt': jnp.stack([g_s5[l][2] for l in range(N_A)]),
        's5_b_re': jnp.stack([g_s5[l][3] for l in range(N_A)]), 's5_b_im': jnp.stack([g_s5[l][4] for l in range(N_A)]),
        's5_c_re': jnp.stack([g_s5[l][5] for l in range(N_A)]), 's5_c_im': jnp.stack([g_s5[l][6] for l in range(N_A)]),
        'kv_norm_g': dkg, 'kv_a_norm_g': dag, 'k_nope_norm_g': dgkn[:, :DN], 'k_rope_norm_g': dgkr[:, DN:DN + DR],
        'mla_q_norm_g': jnp.concatenate(g_qn, axis=0),
        'mla_q_nope_norm_g': jnp.concatenate([g[:, :DN] for g in g_q128], axis=0),
        'mla_q_rope_norm_g': jnp.concatenate([g[:, DN:DN + DR] for g in g_q128], axis=0),
        's5_d': jnp.concatenate(g_dskip, axis=0), 's5_b_glu': jnp.concatenate(g_bglu, axis=0),
    }
    small_names = [n for n in REPLICATED if n not in ('ada_b', 'kv_ada_b')] + SHARDED_VEC
    flat_small = jnp.concatenate([small[n].reshape(-1) for n in small_names])
    n_small = int(flat_small.shape[0])
    pad_small = -(-n_small // 65536) * 65536
    flat_small = jnp.pad(flat_small, (0, pad_small - n_small)).reshape(pad_small // 128, 128)
    g_small_sum = sum_parts("sum_small", all_gather("gather_small", flat_small)).reshape(-1)
    grads = {}
    off = 0
    for n in small_names:
        size = int(np.prod(small[n].shape))
        full = g_small_sum[off:off + size]
        off += size
        if n in SHARDED_VEC:
            full = lax.dynamic_slice_in_dim(full.reshape(N_A, D), me * (D // N_DEV), D // N_DEV, axis=1)
        grads[n] = full.reshape(W[n].shape)
    grads['ada_b'] = g_ada_b
    grads['kv_ada_b'] = g_kv_ada_b

    packed_names = REPLICATED + SHARDED_VEC

    def pack(dct):
        flat_ = jnp.concatenate([dct[n].reshape(-1) for n in packed_names])
        n_ = int(flat_.shape[0])
        p_ = -(-n_ // 65536) * 65536
        return jnp.pad(flat_, (0, p_ - n_)).reshape(p_ // 128, 128)

    _, d_p, m_p, v_p = adamw("adamw_small", pack(grads)[None], pack(W)[None], pack(M)[None], pack(V)[None])
    out_delta, out_m, out_v = {}, {}, {}
    off = 0
    d_p, m_p, v_p = d_p.reshape(-1), m_p.reshape(-1), v_p.reshape(-1)
    for n in packed_names:
        size = int(np.prod(W[n].shape))
        out_delta[n] = d_p[off:off + size].reshape(W[n].shape)
        out_m[n] = m_p[off:off + size].reshape(W[n].shape)
        out_v[n] = v_p[off:off + size].reshape(W[n].shape)
        off += size

    def update(name, parts, base=0, stride=0):
        shp = W[name].shape
        shp3 = shp if len(shp) == 3 else (1,) + shp
        res = adamw("adamw_" + name, parts, W[name].reshape(shp3), M[name].reshape(shp3), V[name].reshape(shp3),
                    base, stride)
        grads[name], out_delta[name], out_m[name], out_v[name] = (a.reshape(shp) for a in res)

    update('ada_w', g_ada_w.reshape(1, DEPTH * D, per_l), 0, D)
    update('kv_ada_w', g_kv_ada_w[None])
    rblk = lambda a: a.reshape(N_DEV, a.shape[0] // N_DEV, a.shape[1])
    cblk = lambda a: a.reshape(a.shape[0], N_DEV, a.shape[1] // N_DEV).transpose(1, 0, 2)
    r1 = all_to_all("a2a_w384", jnp.concatenate(gfull_ffn['g'] + gfull_ffn['u'], axis=1))
    update('ffn_w_gate', r1, 0, D)
    update('ffn_w_up', r1, DEPTH * D, D)
    r2 = all_to_all("a2a_w1024", jnp.concatenate(gfull_ffn['d'] + [rblk(a) for a in g_glu + g_wo], axis=1))
    update('ffn_w_down', r2, 0, FFB)
    update('s5_w_glu', r2, o2, 128)
    update('mla_w_o', r2, o2 + 256, 128)
    r3 = all_to_all("a2a_w256", jnp.concatenate([cblk(g_wkvb)] + [rblk(a) for a in g_dq], axis=1))
    update('w_kv_b', r3)
    update('mla_w_dq', r3, KVL, 128)
    update('w_kv_a', all_to_all("a2a_kva", rblk(g_wa)))
    update('mla_w_uq', all_to_all("a2a_uq", jnp.concatenate([cblk(a) for a in g_uq], axis=1)), 0, QL)

    return (loss, grad_x[None], *[grads[n] for n in WEIGHT_NAMES], *[out_delta[n] for n in WEIGHT_NAMES],
            *[out_m[n] for n in WEIGHT_NAMES], *[out_v[n] for n in WEIGHT_NAMES])
```

```python
import functools
import math

import numpy as np
import jax
import jax.numpy as jnp
from jax import lax
from jax.experimental import pallas as pl
from jax.experimental.pallas import tpu as pltpu

F32 = jnp.float32
_MXU = jnp.bfloat16
HI = lax.Precision.HIGHEST

D = 1024
DEPTH = 4
N_A = 2
FF = 2816
FFB = 384
FFP = 8 * FFB
N_DEV = 8
G = 64
P = 16
N = 64
GB = 8
NBLK = G // GB
HALF = GB * N
H = 16
HP = H // 2
DN, DR, DV = 64, 32, 64
HD = 128
QL = 256
KVL = 256
CHUNK = 64
ROPE_THETA = 10000.0
ATTN_SCALE = 1.0 / math.sqrt(DN + DR)
LOG2E = 1.4426950408889634
EXP2_SCALE = ATTN_SCALE * LOG2E
EPS = 1e-6
ADAM_LR, ADAM_B1, ADAM_B2, ADAM_EPS, ADAM_WD, ADAM_STEP = 0.001, 0.9, 0.999, 1e-08, 0.01, 10
VMEM_LIMIT = 56 * 1024 * 1024
MESH = pl.DeviceIdType.MESH

TILE_ROW = 256
TILE_ATT = 256
TILE_SCAN = 256


def _params(n_grid):
    return pltpu.CompilerParams(dimension_semantics=("arbitrary",) * n_grid, vmem_limit_bytes=VMEM_LIMIT)


@jax.custom_vjp
def mm(a, w):
    return jnp.dot(a.astype(_MXU), w, preferred_element_type=F32)


def _mm_fwd(a, w):
    return mm(a, w), w


def _mm_bwd(w, g):
    da = lax.dot_general(g.astype(_MXU), w, (((1,), (1,)), ((), ())), preferred_element_type=F32)
    return da, jnp.zeros_like(w)


mm.defvjp(_mm_fwd, _mm_bwd)


def rms(x, g):
    return x * lax.rsqrt(jnp.mean(x * x, axis=-1, keepdims=True) + EPS) * g


def modulate(h, shift, scale):
    return h * (1.0 + scale) + shift


def _lane(n=HD):
    return lax.broadcasted_iota(jnp.int32, (1, n), 1)


def _rot_matrix():
    r = lax.broadcasted_iota(jnp.int32, (HD, HD), 0)
    c = lax.broadcasted_iota(jnp.int32, (HD, HD), 1)
    first = (c >= DN) & (c < DN + DR // 2) & (r == c + DR // 2)
    second = (c >= DN + DR // 2) & (c < DN + DR) & (r == c - DR // 2)
    return jnp.where(first, -1.0, jnp.where(second, 1.0, 0.0)).astype(F32)


def head_norm_rope(xh, g128, cosf, sinf, rot, with_nope):
    lane = _lane()
    m_n = lane < DN
    m_r = (lane >= DN) & (lane < DN + DR)
    sq = xh * xh
    inv_r = lax.rsqrt(jnp.sum(jnp.where(m_r, sq, 0.0), axis=-1, keepdims=True) / DR + EPS)
    if with_nope:
        inv_n = lax.rsqrt(jnp.sum(jnp.where(m_n, sq, 0.0), axis=-1, keepdims=True) / DN + EPS)
        inv = jnp.where(m_n, inv_n, jnp.where(m_r, inv_r, 0.0))
    else:
        inv = jnp.where(m_r, inv_r, 0.0)
    xg = xh * inv * g128
    return xg * cosf + jnp.dot(xg, rot, precision=HI, preferred_element_type=F32) * sinf


def seg_pre(x, g, sh, sc):
    return (modulate(rms(x, g), sh, sc),), ()


def seg_ffn(x, g, sh, sc, gt, t_g, t_u, t_d, wg, wu, wd):
    h = modulate(rms(x, g), sh, sc)
    gate = mm(h, wg) + t_g
    up = mm(h, wu) + t_u
    a = jax.nn.silu(gate) * up
    y = mm(a, wd) + t_d
    return (x + gt * y,), (h.astype(_MXU), a.astype(_MXU))


def seg_glu(x, y, gt, b, t_z, w):
    g = jax.nn.gelu(y)
    z = mm(g, w) + b + t_z
    return (x + gt * (g * jax.nn.sigmoid(z)),), (g.astype(_MXU),)


def seg_o(x, o, gt, t_o, w):
    return (x + gt * (mm(o, w) + t_o),), (o.astype(_MXU),)


def seg_q(x, g, sh, sc, qg, g128, t_l, t_q, cosf, sinf, wdq, wuq):
    h = modulate(rms(x, g), sh, sc)
    ql = mm(h, wdq) + t_l
    qn = rms(ql, qg)
    q = mm(qn, wuq) + t_q
    rot = _rot_matrix()
    heads = [head_norm_rope(q[:, HD * i:HD * (i + 1)], g128, cosf, sinf, rot, True) for i in range(H)]
    return (jnp.concatenate(heads, axis=1),), (h.astype(_MXU), qn.astype(_MXU))


def seg_kv(x, g, sh, sc, ag, gkn, gkr, t_a, t_k, t_v, cosf, sinf, wa, wkn, wv):
    hk = modulate(rms(x, g), sh, sc)
    kva = mm(hk, wa) + t_a
    ckv = rms(kva[:, :KVL], ag)
    kr = head_norm_rope(kva[:, KVL:KVL + HD], gkr, cosf, sinf, _rot_matrix(), False)
    kn = mm(ckv, wkn) + t_k
    v = mm(ckv, wv) + t_v
    heads = []
    for i in range(H):
        kh = kn[:, HD * i:HD * (i + 1)]
        inv = lax.rsqrt(jnp.sum(kh * kh, axis=-1, keepdims=True) / DN + EPS)
        heads.append(kh * inv * gkn + kr)
    return (jnp.concatenate(heads, axis=1), v), (hk.astype(_MXU), ckv.astype(_MXU))


def _row_call(name, body_fn, rows, fulls, out_rows, out_accs, tile):
    s = rows[0].shape[0]
    n_tiles = s // tile
    n_rows, n_fulls, n_or, n_oa = len(rows), len(fulls), len(out_rows), len(out_accs)

    def kern(*refs):
        i = pl.program_id(0)
        row_v = [r[...] for r in refs[:n_rows]]
        full_v = [r[...] for r in refs[n_rows:n_rows + n_fulls]]
        o_refs = refs[n_rows + n_fulls:]
        ro, ao = body_fn(row_v, full_v)
        for r, v in zip(o_refs[:n_or], ro):
            r[...] = v.astype(r.dtype)
        if n_oa:
            @pl.when(i == 0)
            def _():
                for r in o_refs[n_or:]:
                    r[...] = jnp.zeros(r.shape, r.dtype)
            for r, v in zip(o_refs[n_or:], ao):
                r[...] += v.astype(r.dtype)

    in_specs = [pl.BlockSpec((tile, a.shape[1]), lambda i: (i, 0)) for a in rows]
    for a in fulls:
        big = a.size * a.dtype.itemsize > (1 << 20)
        nd = a.ndim
        in_specs.append(pl.BlockSpec(a.shape, functools.partial(lambda i, nd_: (0,) * nd_, nd_=nd),
                                     **({"pipeline_mode": pl.Buffered(1)} if big else {})))
    out_shape = [jax.ShapeDtypeStruct((s, w), dt) for w, dt in out_rows]
    out_shape += [jax.ShapeDtypeStruct(shp, dt) for shp, dt in out_accs]
    out_specs = [pl.BlockSpec((tile, w), lambda i: (i, 0)) for w, _ in out_rows]
    out_specs += [pl.BlockSpec(shp, functools.partial(lambda i, nd_: (0,) * nd_, nd_=len(shp))) for shp, _ in out_accs]
    res = pl.pallas_call(kern, out_shape=out_shape, grid=(n_tiles,), in_specs=in_specs, out_specs=out_specs,
                         name=name, compiler_params=_params(1))(*rows, *fulls)
    return list(res)


def seg_forward(name, seg, rows, smalls, consts_rows, consts_full, out_widths, tile=TILE_ROW, tap_widths=()):
    n_r, n_s, n_cr = len(rows), len(smalls), len(consts_rows)

    def body(row_v, full_v):
        t = row_v[0].shape[0]
        taps = [jnp.zeros((t, w), F32) for w in tap_widths]
        outs, _ = seg(*row_v[:n_r], *full_v[:n_s], *taps, *row_v[n_r:], *full_v[n_s:])
        return outs, ()

    return _row_call(name, body, list(rows) + list(consts_rows), list(smalls) + list(consts_full),
                     out_widths, [], tile)


def seg_backward(name, seg, rows, smalls, consts_rows, consts_full, cots, tap_widths, aux_widths,
                 dx_add=None, tile=TILE_ROW):
    n_r, n_s, n_cr, n_c = len(rows), len(smalls), len(consts_rows), len(cots)
    has_add = dx_add is not None

    def body(row_v, full_v):
        t = row_v[0].shape[0]
        prim_rows = row_v[:n_r]
        c_rows = row_v[n_r:n_r + n_cr]
        cot_v = row_v[n_r + n_cr:n_r + n_cr + n_c]
        add_v = row_v[n_r + n_cr + n_c] if has_add else None
        small_v = full_v[:n_s]
        c_full = full_v[n_s:]
        taps = [jnp.zeros((t, w), F32) for w in tap_widths]

        def f(*args):
            return seg(*args, *c_rows, *c_full)

        _, vjp_fn, aux = jax.vjp(f, *prim_rows, *small_v, *taps, has_aux=True)
        grads = vjp_fn(tuple(c.astype(F32) for c in cot_v))
        d_rows = list(grads[:n_r])
        if has_add:
            d_rows[0] = d_rows[0] + add_v
        d_small = grads[n_r:n_r + n_s]
        d_taps = grads[n_r + n_s:]
        return d_rows + list(d_taps) + list(aux), [jnp.sum(g, axis=0, keepdims=True) if g.shape[0] != 1 else g
                                                   for g in d_small]

    all_rows = list(rows) + list(consts_rows) + list(cots) + ([dx_add] if has_add else [])
    out_rows = [(a.shape[1], F32) for a in rows] + [(w, _MXU) for w in tap_widths] + [(w, _MXU) for w in aux_widths]
    out_accs = [((1, a.shape[1]), F32) for a in smalls]
    res = _row_call(name, body, all_rows, list(smalls) + list(consts_full), out_rows, out_accs, tile)
    n_t, n_a = len(tap_widths), len(aux_widths)
    return res[:n_r], res[n_r:n_r + n_t], res[n_r + n_t:n_r + n_t + n_a], res[n_r + n_t + n_a:]


def _split(n):
    if n <= 1024:
        return n
    for t in (1408, 1024, 768, 512, 256, 128):
        if n % t == 0:
            return t
    raise ValueError(n)


def matmul_tn(name, a, b, out_dtype, col_blocks=None):
    s, k1 = a.shape
    _, k2 = b.shape
    tm, ts = _split(k1), 512
    if col_blocks is None:
        tn, per_step, wblk = _split(k2), 1, None
    else:
        wblk = k2 // col_blocks
        per_step = max(1, min(col_blocks, 1536 // wblk))
        tn = per_step * wblk
    n_s = s // ts

    def kern(a_ref, b_ref, o_ref, acc_ref):
        k = pl.program_id(2)

        @pl.when(k == 0)
        def _():
            acc_ref[...] = jnp.zeros(acc_ref.shape, F32)

        acc_ref[...] += lax.dot_general(a_ref[...], b_ref[...], (((0,), (0,)), ((), ())),
                                        preferred_element_type=F32)

        @pl.when(k == n_s - 1)
        def _():
            if col_blocks is None:
                o_ref[...] = acc_ref[...].astype(o_ref.dtype)
            else:
                for cb in range(per_step):
                    o_ref[cb] = acc_ref[:, wblk * cb:wblk * (cb + 1)].astype(o_ref.dtype)

    if col_blocks is None:
        out_shape = jax.ShapeDtypeStruct((k1, k2), out_dtype)
        out_spec = pl.BlockSpec((tm, tn), lambda i, j, k: (i, j))
    else:
        out_shape = jax.ShapeDtypeStruct((col_blocks, k1, wblk), out_dtype)
        out_spec = pl.BlockSpec((per_step, tm, wblk), lambda i, j, k: (j, i, 0))
    return pl.pallas_call(
        kern, out_shape=out_shape, grid=(k1 // tm, k2 // tn, n_s),
        in_specs=[pl.BlockSpec((ts, tm), lambda i, j, k: (k, i)), pl.BlockSpec((ts, tn), lambda i, j, k: (k, j))],
        out_specs=out_spec,
        scratch_shapes=[pltpu.VMEM((tm, tn), F32)], name=name, compiler_params=_params(3))(a, b)


def ffn_backward(name, x, dxo, g, sh, sc, gt, wg, wu, wd, tile=TILE_ROW):
    s = x.shape[0]
    n_blk = wg.shape[1] // FFB

    def kern(x_ref, dxo_ref, g_ref, sh_ref, sc_ref, gt_ref, wg_ref, wu_ref, wd_ref,
             dx_ref, dg_ref, du_ref, dy_ref, h_ref, a_ref, dgn_ref, dsh_ref, dsc_ref, dgt_ref):
        i = pl.program_id(0)

        @pl.when(i == 0)
        def _():
            for r in (dgn_ref, dsh_ref, dsc_ref, dgt_ref):
                r[...] = jnp.zeros(r.shape, F32)

        dxo = dxo_ref[...]
        h, pre_vjp = jax.vjp(lambda *p: modulate(rms(p[0], p[1]), p[2], p[3]), x_ref[...], g_ref[...], sh_ref[...],
                             sc_ref[...])
        hb = h.astype(_MXU)
        h_ref[...] = hb
        dyb = (gt_ref[...] * dxo).astype(_MXU)
        dy_ref[...] = dyb
        y = jnp.zeros((tile, D), F32)
        dh = jnp.zeros((tile, D), F32)
        tr = (((1,), (1,)), ((), ()))
        for c in range(n_blk):
            cs = slice(FFB * c, FFB * (c + 1))
            gate = jnp.dot(hb, wg_ref[:, cs], preferred_element_type=F32)
            up = jnp.dot(hb, wu_ref[:, cs], preferred_element_type=F32)
            sig = jax.nn.sigmoid(gate)
            sl = gate * sig
            ab = (sl * up).astype(_MXU)
            a_ref[:, cs] = ab
            y = y + jnp.dot(ab, wd_ref[cs, :], preferred_element_type=F32)
            da = lax.dot_general(dyb, wd_ref[cs, :], tr, preferred_element_type=F32)
            dgb = (da * up * (sig * (1.0 + gate * (1.0 - sig)))).astype(_MXU)
            dub = (da * sl).astype(_MXU)
            dg_ref[:, cs] = dgb
            du_ref[:, cs] = dub
            dh = dh + lax.dot_general(dgb, wg_ref[:, cs], tr, preferred_element_type=F32) \
                + lax.dot_general(dub, wu_ref[:, cs], tr, preferred_element_type=F32)
        dgt_ref[...] += jnp.sum(dxo * y, axis=0, keepdims=True)
        dx_pre, dgn, dsh, dsc = pre_vjp(dh)
        dx_ref[...] = dxo + dx_pre
        dgn_ref[...] += dgn
        dsh_ref[...] += dsh
        dsc_ref[...] += dsc

    row = lambda w: pl.BlockSpec((tile, w), lambda i: (i, 0))
    vec = pl.BlockSpec((1, D), lambda i: (0, 0))
    wspec = lambda a: pl.BlockSpec(a.shape, lambda i: (0, 0), pipeline_mode=pl.Buffered(1))
    rows_out = [(D, F32), (wg.shape[1], _MXU), (wg.shape[1], _MXU), (D, _MXU), (D, _MXU), (wg.shape[1], _MXU)]
    res = pl.pallas_call(
        kern,
        out_shape=[jax.ShapeDtypeStruct((s, w), dt) for w, dt in rows_out] + [jax.ShapeDtypeStruct((1, D), F32)] * 4,
        grid=(s // tile,),
        in_specs=[row(D), row(D), vec, vec, vec, vec, wspec(wg), wspec(wu), wspec(wd)],
        out_specs=[row(w) for w, _ in rows_out] + [vec] * 4,
        name=name, compiler_params=_params(1))(x, dxo, g, sh, sc, gt, wg, wu, wd)
    return res


def small_matmul(name, a, w, tn=256):
    m, k = a.shape
    n = w.shape[1]

    def kern(a_ref, w_ref, o_ref):
        o_ref[...] = jnp.dot(a_ref[...].astype(_MXU), w_ref[...].astype(_MXU), preferred_element_type=F32)

    return pl.pallas_call(kern, out_shape=jax.ShapeDtypeStruct((m, n), F32), grid=(n // tn,),
                          in_specs=[pl.BlockSpec((m, k), lambda j: (0, 0)), pl.BlockSpec((k, tn), lambda j: (0, j))],
                          out_specs=pl.BlockSpec((m, tn), lambda j: (0, j)), name=name,
                          compiler_params=_params(1))(a, w)


def small_matmul_tn(name, a, b, tn=256):
    m, k = a.shape
    n = b.shape[1]

    def kern(a_ref, b_ref, o_ref):
        o_ref[...] = lax.dot_general(a_ref[...].astype(_MXU), b_ref[...].astype(_MXU), (((0,), (0,)), ((), ())),
                                     preferred_element_type=F32)

    return pl.pallas_call(kern, out_shape=jax.ShapeDtypeStruct((k, n), F32), grid=(n // tn,),
                          in_specs=[pl.BlockSpec((m, k), lambda j: (0, 0)), pl.BlockSpec((m, tn), lambda j: (0, j))],
                          out_specs=pl.BlockSpec((k, tn), lambda j: (0, j)), name=name,
                          compiler_params=_params(1))(a, b)


def _s5_prep_math(lam_re, lam_im, log_dt, b_re_t, b_im_t, expand):
    dt = jnp.dot(jnp.exp(log_dt), expand, precision=HI, preferred_element_type=F32)
    mag = jnp.exp(lam_re * dt)
    ab_re = mag * jnp.cos(lam_im * dt)
    ab_im = mag * jnp.sin(lam_im * dt)
    den = lam_re * lam_re + lam_im * lam_im
    nr = ab_re - 1.0
    ni = ab_im
    f_re = (nr * lam_re + ni * lam_im) / den
    f_im = (ni * lam_re - nr * lam_im) / den
    bb_re = f_re * b_re_t - f_im * b_im_t
    bb_im = f_re * b_im_t + f_im * b_re_t
    return ab_re, ab_im, bb_re, bb_im


def _whole(kern, name, out_shape, *args):
    return pl.pallas_call(kern, out_shape=out_shape, name=name,
                          compiler_params=pltpu.CompilerParams(vmem_limit_bytes=VMEM_LIMIT))(*args)


def s5_prep_fwd(name, lam_re, lam_im, log_dt, b_re_t, b_im_t, expand):
    def kern(a, b, c, d, e, f, o0, o1, o2, o3):
        r = _s5_prep_math(a[...], b[...], c[...], d[...], e[...], f[...])
        for o, v in zip((o0, o1, o2, o3), r):
            o[...] = v

    gn = lam_re.shape[1]
    shp = [jax.ShapeDtypeStruct((1, gn), F32)] * 2 + [jax.ShapeDtypeStruct((P, gn), F32)] * 2
    return _whole(kern, name, shp, lam_re, lam_im, log_dt, b_re_t, b_im_t, expand)


def s5_prep_bwd(name, lam_re, lam_im, log_dt, b_re_t, b_im_t, expand, cots):
    def kern(a, b, c, d, e, f, c0, c1, c2, c3, o0, o1, o2, o3, o4):
        ex = f[...]
        _, vjp_fn = jax.vjp(lambda *p: _s5_prep_math(*p, ex), a[...], b[...], c[...], d[...], e[...])
        g = vjp_fn((c0[...], c1[...], c2[...], c3[...]))
        for o, v in zip((o0, o1, o2, o3, o4), g):
            o[...] = v

    shp = [jax.ShapeDtypeStruct(a.shape, F32) for a in (lam_re, lam_im, log_dt, b_re_t, b_im_t)]
    return _whole(kern, name, shp, lam_re, lam_im, log_dt, b_re_t, b_im_t, expand, *cots)


def _cpowers(ar, ai):
    pw = [(ar, ai)]
    for _ in range(7):
        pr, pi = pw[-1]
        pw.append((pr * ar - pi * ai, pr * ai + pi * ar))
    return pw


def _row_select(row, values):
    out = jnp.broadcast_to(values[7], (8, values[7].shape[1]))
    for r in range(6, -1, -1):
        out = jnp.where(row == r, values[r], out)
    return out


def _scan_tables(ar, ai, reverse):
    pw = _cpowers(ar, ai)
    row = lax.broadcasted_iota(jnp.int32, (8, ar.shape[1]), 0)
    steps = []
    for d in (1, 2, 4):
        keep = (row <= 7 - d) if reverse else (row >= d)
        steps.append((jnp.where(keep, pw[d - 1][0], 0.0), jnp.where(keep, pw[d - 1][1], 0.0)))
    order = list(range(7, -1, -1)) if reverse else list(range(8))
    carry = (_row_select(row, [pw[i][0] for i in order]), _row_select(row, [pw[i][1] for i in order]))
    return steps, carry


def _tile_scan_fwd(xr, xi, cr, ci, steps, carry_m):
    for d, (mr, mi) in zip((1, 2, 4), steps):
        sr = pltpu.roll(xr, d, 0)
        si = pltpu.roll(xi, d, 0)
        xr, xi = xr + mr * sr - mi * si, xi + mr * si + mi * sr
    pr, pi = carry_m
    return xr + pr * cr - pi * ci, xi + pr * ci + pi * cr


def _tile_scan_rev(xr, xi, cr, ci, steps, carry_m):
    for d, (mr, mi) in zip((1, 2, 4), steps):
        sr = pltpu.roll(xr, 8 - d, 0)
        si = pltpu.roll(xi, 8 - d, 0)
        xr, xi = xr + mr * sr + mi * si, xi + mr * si - mi * sr
    pr, pi = carry_m
    return xr + pr * cr + pi * ci, xi + pr * ci - pi * cr


def _fwd_scan_block(buf, row0, n_tiles8, ar, ai, c0r, c0i):
    steps, carry_m = _scan_tables(ar, ai, False)

    def body(j, carry):
        cr, ci = carry
        r0 = pl.multiple_of(row0 + j * 8, 8)
        xr = buf[pl.ds(r0, 8), 0:HALF]
        xi = buf[pl.ds(r0, 8), HALF:2 * HALF]
        xr, xi = _tile_scan_fwd(xr, xi, cr, ci, steps, carry_m)
        buf[pl.ds(r0, 8), 0:HALF] = xr
        buf[pl.ds(r0, 8), HALF:2 * HALF] = xi
        return xr[7:8], xi[7:8]

    return lax.fori_loop(0, n_tiles8, body, (c0r, c0i))


def s5_scan_fwd(name, h, wb, wc, a_tab, dskip, tile=TILE_SCAN):
    s = h.shape[0]
    n_t = s // tile

    def kern(h_ref, wb_ref, wc_ref, a_ref, d_ref, y_ref, s0_ref, carry_ref, buf):
        i = pl.program_id(0)

        @pl.when(i == 0)
        def _():
            carry_ref[...] = jnp.zeros(carry_ref.shape, F32)

        s0_ref[0] = carry_ref[...]
        for k in range(NBLK):
            cols = slice(GB * P * k, GB * P * (k + 1))
            u = h_ref[:, cols]
            buf[...] = jnp.dot(u.astype(_MXU), wb_ref[k], preferred_element_type=F32)
            ar = a_ref[k, :, 0:HALF]
            ai = a_ref[k, :, HALF:2 * HALF]
            cr, ci = _fwd_scan_block(buf, 0, tile // 8, ar, ai, carry_ref[k:k + 1, 0:HALF],
                                     carry_ref[k:k + 1, HALF:2 * HALF])
            carry_ref[k:k + 1, 0:HALF] = cr
            carry_ref[k:k + 1, HALF:2 * HALF] = ci
            y_ref[:, cols] = jnp.dot(buf[...].astype(_MXU), wc_ref[k], preferred_element_type=F32) + d_ref[:, cols] * u

    full = lambda a: pl.BlockSpec(a.shape, functools.partial(lambda i, nd_: (0,) * nd_, nd_=a.ndim))
    return pl.pallas_call(
        kern,
        out_shape=[jax.ShapeDtypeStruct((s, D), F32), jax.ShapeDtypeStruct((n_t, NBLK, 2 * HALF), F32)],
        grid=(n_t,),
        in_specs=[pl.BlockSpec((tile, D), lambda i: (i, 0)), full(wb), full(wc), full(a_tab), full(dskip)],
        out_specs=[pl.BlockSpec((tile, D), lambda i: (i, 0)), pl.BlockSpec((1, NBLK, 2 * HALF), lambda i: (i, 0, 0))],
        scratch_shapes=[pltpu.VMEM((NBLK, 2 * HALF), F32), pltpu.VMEM((tile, 2 * HALF), F32)],
        name=name, compiler_params=_params(1))(h, wb, wc, a_tab, dskip)


def s5_scan_bwd(name, h, dy, s0, wb, wc, a_tab, dskip, tile=TILE_SCAN):
    s = h.shape[0]
    n_t = s // tile
    n8 = tile // 8

    def kern(h_ref, dy_ref, s0_ref, wb_ref, wc_ref, a_ref, d_ref, dh_ref, dwb_ref, dwc_ref, da_ref, dd_ref,
             lam_ref, sbuf, gbuf):
        i = pl.program_id(0)

        @pl.when(i == 0)
        def _():
            lam_ref[...] = jnp.zeros(lam_ref.shape, F32)
            dwb_ref[...] = jnp.zeros(dwb_ref.shape, F32)
            dwc_ref[...] = jnp.zeros(dwc_ref.shape, F32)
            da_ref[...] = jnp.zeros(da_ref.shape, F32)
            dd_ref[...] = jnp.zeros(dd_ref.shape, F32)

        for k in range(NBLK):
            cols = slice(GB * P * k, GB * P * (k + 1))
            u = h_ref[:, cols]
            dyk = dy_ref[:, cols]
            ar = a_ref[k, :, 0:HALF]
            ai = a_ref[k, :, HALF:2 * HALF]
            sbuf[0:8, :] = jnp.broadcast_to(s0_ref[0, k:k + 1, :], (8, 2 * HALF))
            sbuf[8:tile + 8, :] = jnp.dot(u.astype(_MXU), wb_ref[k], preferred_element_type=F32)
            _fwd_scan_block(sbuf, 8, n8, ar, ai, s0_ref[0, k:k + 1, 0:HALF], s0_ref[0, k:k + 1, HALF:2 * HALF])
            dyb = dyk.astype(_MXU)
            gbuf[...] = lax.dot_general(dyb, wc_ref[k], (((1,), (1,)), ((), ())), preferred_element_type=F32)
            dwc_ref[k] += lax.dot_general(sbuf[8:tile + 8, :].astype(_MXU), dyb, (((0,), (0,)), ((), ())),
                                          preferred_element_type=F32)
            steps, carry_m = _scan_tables(ar, ai, True)
            row = lax.broadcasted_iota(jnp.int32, (8, HALF), 0)

            def body(jj, carry):
                cr, ci, dar, dai = carry
                j = n8 - 1 - jj
                r0 = pl.multiple_of(j * 8, 8)
                xr = gbuf[pl.ds(r0, 8), 0:HALF]
                xi = gbuf[pl.ds(r0, 8), HALF:2 * HALF]
                xr, xi = _tile_scan_rev(xr, xi, cr, ci, steps, carry_m)
                gbuf[pl.ds(r0, 8), 0:HALF] = xr
                gbuf[pl.ds(r0, 8), HALF:2 * HALF] = xi
                r1 = pl.multiple_of(j * 8 + 8, 8)
                spr = jnp.where(row == 0, sbuf[pl.ds(r0, 8), 0:HALF][7:8],
                                pltpu.roll(sbuf[pl.ds(r1, 8), 0:HALF], 1, 0))
                spi = jnp.where(row == 0, sbuf[pl.ds(r0, 8), HALF:2 * HALF][7:8],
                                pltpu.roll(sbuf[pl.ds(r1, 8), HALF:2 * HALF], 1, 0))
                dar = dar + xr * spr + xi * spi
                dai = dai + xi * spr - xr * spi
                return xr[0:1], xi[0:1], dar, dai

            z8 = jnp.zeros((8, HALF), F32)
            cr, ci, dar, dai = lax.fori_loop(
                0, n8, body, (lam_ref[k:k + 1, 0:HALF], lam_ref[k:k + 1, HALF:2 * HALF], z8, z8))
            lam_ref[k:k + 1, 0:HALF] = cr
            lam_ref[k:k + 1, HALF:2 * HALF] = ci
            da_ref[k:k + 1, 0:HALF] += jnp.sum(dar, axis=0, keepdims=True)
            da_ref[k:k + 1, HALF:2 * HALF] += jnp.sum(dai, axis=0, keepdims=True)
            lam = gbuf[...].astype(_MXU)
            dwb_ref[k] += lax.dot_general(u.astype(_MXU), lam, (((0,), (0,)), ((), ())), preferred_element_type=F32)
            du = lax.dot_general(lam, wb_ref[k], (((1,), (1,)), ((), ())), preferred_element_type=F32)
            dh_ref[:, cols] = du + d_ref[:, cols] * dyk
            dd_ref[:, cols] += jnp.sum(dyk * u, axis=0, keepdims=True)

    full = lambda a: pl.BlockSpec(a.shape, functools.partial(lambda i, nd_: (0,) * nd_, nd_=a.ndim))
    fullo = lambda shp: pl.BlockSpec(shp, functools.partial(lambda i, nd_: (0,) * nd_, nd_=len(shp)))
    rev = lambda i: (n_t - 1 - i, 0)
    return pl.pallas_call(
        kern,
        out_shape=[jax.ShapeDtypeStruct((s, D), F32), jax.ShapeDtypeStruct(wb.shape, F32),
                   jax.ShapeDtypeStruct(wc.shape, F32), jax.ShapeDtypeStruct((NBLK, 2 * HALF), F32),
                   jax.ShapeDtypeStruct((1, D), F32)],
        grid=(n_t,),
        in_specs=[pl.BlockSpec((tile, D), rev), pl.BlockSpec((tile, D), rev),
                  pl.BlockSpec((1, NBLK, 2 * HALF), lambda i: (n_t - 1 - i, 0, 0)),
                  full(wb), full(wc), full(a_tab), full(dskip)],
        out_specs=[pl.BlockSpec((tile, D), rev), fullo(wb.shape), fullo(wc.shape), fullo((NBLK, 2 * HALF)),
                   fullo((1, D))],
        scratch_shapes=[pltpu.VMEM((NBLK, 2 * HALF), F32), pltpu.VMEM((tile + 8, 2 * HALF), F32),
                        pltpu.VMEM((tile, 2 * HALF), F32)],
        name=name, compiler_params=_params(1))(h, dy, s0, wb, wc, a_tab, dskip)


def _chunk_mask(q0, k0, tq, tk):
    r = (q0 + lax.broadcasted_iota(jnp.int32, (tq, tk), 0)) // CHUNK
    c = (k0 + lax.broadcasted_iota(jnp.int32, (tq, tk), 1)) // CHUNK
    return r >= c


def _head_lanes(j):
    lane = _lane(2 * DV)
    return (lane >= DV * j) & (lane < DV * (j + 1))


def _raw_scores(q, kblk, masked, t):
    s = lax.dot_general(q, kblk, (((1,), (1,)), ((), ())), preferred_element_type=F32)
    return jnp.where(_chunk_mask(0, 0, t, t), s, -1e30) if masked else s


def attn_fwd(name, q, k, v, t=TILE_ATT):
    s = q.shape[0]
    n_q = s // t

    def kern(q_ref, k_ref, v_ref, o_ref, lse_ref):
        qi = pl.program_id(1)
        qs = [q_ref[:, HD * j:HD * (j + 1)] for j in range(2)]

        def scores(k0):
            return tuple(_raw_scores(qs[j], k_ref[pl.ds(k0, t), HD * j:HD * (j + 1)], False, t) for j in range(2))

        def absorb(k0, scs, carry):
            vblk = v_ref[pl.ds(k0, t), :]
            m_new = [jnp.maximum(carry[j][0], jnp.max(scs[j], axis=-1, keepdims=True)) for j in range(2)]
            ps = [jnp.exp2((scs[j] - m_new[j]) * EXP2_SCALE) for j in range(2)]
            alphas = [jnp.exp2((carry[j][0] - m_new[j]) * EXP2_SCALE) for j in range(2)]
            pvs = [jnp.dot(ps[j].astype(_MXU), vblk, preferred_element_type=F32) for j in range(2)]
            return tuple((m_new[j], alphas[j] * carry[j][1] + jnp.sum(ps[j], axis=-1, keepdims=True),
                          alphas[j] * carry[j][2] + pvs[j]) for j in range(2))

        def step(kb, state):
            scs, carry = state
            nxt = scores(pl.multiple_of((kb + 1) * t, t))
            return nxt, absorb(pl.multiple_of(kb * t, t), scs, carry)

        init = tuple((jnp.full((t, 1), -1e30, F32), jnp.zeros((t, 1), F32), jnp.zeros((t, 2 * DV), F32))
                     for _ in range(2))
        scs, carry = lax.fori_loop(0, qi, step, (scores(0), init))
        mask = _chunk_mask(0, 0, t, t)
        carry = absorb(pl.multiple_of(qi * t, t), tuple(jnp.where(mask, sc, -1e30) for sc in scs), carry)
        outs = []
        for j in range(2):
            m, l, acc = carry[j]
            outs.append(acc / l)
            lse_ref[0, j] = m * ATTN_SCALE + jnp.log(l)
        o_ref[...] = jnp.where(_head_lanes(0), outs[0], outs[1])

    return pl.pallas_call(
        kern,
        out_shape=[jax.ShapeDtypeStruct((s, H * DV), F32), jax.ShapeDtypeStruct((HP, 2, s, 1), F32)],
        grid=(HP, n_q),
        in_specs=[pl.BlockSpec((t, 2 * HD), lambda hp, i: (i, hp)), pl.BlockSpec((s, 2 * HD), lambda hp, i: (0, hp)),
                  pl.BlockSpec((s, 2 * DV), lambda hp, i: (0, hp))],
        out_specs=[pl.BlockSpec((t, 2 * DV), lambda hp, i: (i, hp)),
                   pl.BlockSpec((1, 2, t, 1), lambda hp, i: (hp, 0, i, 0))],
        name=name, compiler_params=_params(2))(q, k, v)


def attn_bwd(name, q, k, v, o, do, lse, t=TILE_ATT):
    s = q.shape[0]
    n_q = s // t

    def kern(q_ref, k_ref, v_ref, o_ref, do_ref, lse_ref, dq_ref, dk_ref, dv_ref):
        qi = pl.program_id(1)

        @pl.when(qi == 0)
        def _():
            dk_ref[...] = jnp.zeros(dk_ref.shape, F32)
            dv_ref[...] = jnp.zeros(dv_ref.shape, F32)

        qs, doms, deltas, lse2 = [], [], [], []
        for j in range(2):
            qs.append(q_ref[:, HD * j:HD * (j + 1)])
            dom = jnp.where(_head_lanes(j), do_ref[...], 0.0)
            deltas.append(jnp.sum(dom * o_ref[...], axis=-1, keepdims=True))
            doms.append(dom.astype(_MXU))
            lse2.append(lse_ref[0, j] * LOG2E)

        def block(k0, dqs, masked):
            vblk = v_ref[pl.ds(k0, t), :]
            kblks = [k_ref[pl.ds(k0, t), HD * j:HD * (j + 1)] for j in range(2)]
            scs = [_raw_scores(qs[j], kblks[j], masked, t) for j in range(2)]
            dps = [lax.dot_general(doms[j], vblk, (((1,), (1,)), ((), ())), preferred_element_type=F32)
                   for j in range(2)]
            ps = [jnp.exp2(scs[j] * EXP2_SCALE - lse2[j]) for j in range(2)]
            dss = [(ps[j] * (dps[j] - deltas[j])).astype(_MXU) for j in range(2)]
            pbs = [ps[j].astype(_MXU) for j in range(2)]
            new = tuple(dqs[j] + jnp.dot(dss[j], kblks[j], preferred_element_type=F32) for j in range(2))
            for j in range(2):
                dk_ref[pl.ds(k0, t), HD * j:HD * (j + 1)] += lax.dot_general(
                    dss[j], qs[j], (((0,), (0,)), ((), ())), preferred_element_type=F32)
            dvs = [lax.dot_general(pbs[j], doms[j], (((0,), (0,)), ((), ())), preferred_element_type=F32)
                   for j in range(2)]
            dv_ref[pl.ds(k0, t), :] += dvs[0] + dvs[1]
            return new

        init = (jnp.zeros((t, HD), F32), jnp.zeros((t, HD), F32))
        dqs = lax.fori_loop(0, qi, lambda kb, c: block(pl.multiple_of(kb * t, t), c, False), init)
        dqs = block(pl.multiple_of(qi * t, t), dqs, True)
        for j in range(2):
            dq_ref[:, HD * j:HD * (j + 1)] = dqs[j] * ATTN_SCALE

        @pl.when(qi == n_q - 1)
        def _():
            dk_ref[...] = dk_ref[...] * ATTN_SCALE

    return pl.pallas_call(
        kern,
        out_shape=[jax.ShapeDtypeStruct((s, H * HD), F32), jax.ShapeDtypeStruct((s, H * HD), F32),
                   jax.ShapeDtypeStruct((s, H * DV), F32)],
        grid=(HP, n_q),
        in_specs=[pl.BlockSpec((t, 2 * HD), lambda hp, i: (i, hp)), pl.BlockSpec((s, 2 * HD), lambda hp, i: (0, hp)),
                  pl.BlockSpec((s, 2 * DV), lambda hp, i: (0, hp)), pl.BlockSpec((t, 2 * DV), lambda hp, i: (i, hp)),
                  pl.BlockSpec((t, 2 * DV), lambda hp, i: (i, hp)),
                  pl.BlockSpec((1, 2, t, 1), lambda hp, i: (hp, 0, i, 0))],
        out_specs=[pl.BlockSpec((t, 2 * HD), lambda hp, i: (i, hp)), pl.BlockSpec((s, 2 * HD), lambda hp, i: (0, hp)),
                   pl.BlockSpec((s, 2 * DV), lambda hp, i: (0, hp))],
        name=name, compiler_params=_params(2))(q, k, v, o, do, lse)


def rope_tables(name, pos_col, inv128):
    s = pos_col.shape[0]

    def kern(p_ref, inv_ref, c_ref, s_ref):
        ang = p_ref[...].astype(F32) * inv_ref[...]
        lane = _lane()
        m_r = (lane >= DN) & (lane < DN + DR)
        c_ref[...] = jnp.where(lane < DN, 1.0, jnp.where(m_r, jnp.cos(ang), 0.0))
        s_ref[...] = jnp.where(m_r, jnp.sin(ang), 0.0)

    return _whole(kern, name, [jax.ShapeDtypeStruct((s, HD), F32)] * 2, pos_col, inv128)


def loss_kernel(name, y, tgt, tile=TILE_ROW):
    def body(row_v, _):
        err = row_v[0] - row_v[1]
        part = 0.5 * jnp.sum(jnp.mean(err * err, axis=-1, keepdims=True), axis=0, keepdims=True)
        return [err * (1.0 / D)], [jnp.broadcast_to(part, (1, 128))]

    return _row_call(name, body, [y, tgt], [], [(D, F32)], [((1, 128), F32)], tile)


def _row_tile(r, c):
    cap = max(8, (1 << 18) // max(c, 1))
    for t in (2048, 1024, 512, 256, 128, 64, 32, 16, 8):
        if t <= cap and r % t == 0:
            return t
    return r


def sum_parts(name, parts):
    n, r, c = parts.shape
    t = _row_tile(r, c)

    def kern(p_ref, o_ref):
        acc = p_ref[0].astype(F32)
        for i in range(1, n):
            acc = acc + p_ref[i].astype(F32)
        o_ref[...] = acc

    return pl.pallas_call(kern, out_shape=jax.ShapeDtypeStruct((r, c), F32), grid=(r // t,),
                          in_specs=[pl.BlockSpec((n, t, c), lambda i: (0, i, 0))],
                          out_specs=pl.BlockSpec((t, c), lambda i: (i, 0)), name=name, compiler_params=_params(1))(parts)


def adamw(name, parts, w, m, v, base=0, stride=0):
    n, _, cp = parts.shape
    nl, r, c = w.shape
    t = _row_tile(math.gcd(math.gcd(r, base), stride), max(c, cp))
    c1 = 1.0 / (1.0 - ADAM_B1 ** ADAM_STEP)
    c2 = 1.0 / (1.0 - ADAM_B2 ** ADAM_STEP)

    def kern(p_ref, w_ref, m_ref, v_ref, g_ref, d_ref, nm_ref, nv_ref):
        g = p_ref[0].astype(F32)
        for i in range(1, n):
            g = g + p_ref[i].astype(F32)
        g = g[:, :c]
        nm = ADAM_B1 * m_ref[...] + (1.0 - ADAM_B1) * g
        nv = ADAM_B2 * v_ref[...] + (1.0 - ADAM_B2) * (g * g)
        g_ref[...] = g
        nm_ref[...] = nm
        nv_ref[...] = nv
        d_ref[...] = -ADAM_LR * ((nm * c1) / (jnp.sqrt(nv * c2) + ADAM_EPS) + ADAM_WD * w_ref[...])

    spec = pl.BlockSpec((None, t, c), lambda l, i: (l, i, 0))
    pspec = pl.BlockSpec((n, t, cp), lambda l, i: (0, (base + l * stride) // t + i, 0))
    return pl.pallas_call(kern, out_shape=[jax.ShapeDtypeStruct((nl, r, c), F32)] * 4, grid=(nl, r // t),
                          in_specs=[pspec, spec, spec, spec], out_specs=[spec] * 4, name=name,
                          compiler_params=_params(2))(parts, w, m, v)


def adamw_layer(name, parts, w, m, v, layer, prev, base=0):
    n, _, cp = parts.shape
    nl, r, c = w.shape
    t = _row_tile(math.gcd(r, base), max(c, cp))
    c1 = 1.0 / (1.0 - ADAM_B1 ** ADAM_STEP)
    c2 = 1.0 / (1.0 - ADAM_B2 ** ADAM_STEP)
    chained = nl > 1

    def kern(p_ref, w_ref, m_ref, v_ref, *rest):
        g_ref, d_ref, nm_ref, nv_ref = rest[-4:]
        g = p_ref[0].astype(F32)
        for i in range(1, n):
            g = g + p_ref[i].astype(F32)
        g = g[:, :c]
        nm = ADAM_B1 * m_ref[...] + (1.0 - ADAM_B1) * g
        nv = ADAM_B2 * v_ref[...] + (1.0 - ADAM_B2) * (g * g)
        g_ref[...] = g
        nm_ref[...] = nm
        nv_ref[...] = nv
        d_ref[...] = -ADAM_LR * ((nm * c1) / (jnp.sqrt(nv * c2) + ADAM_EPS) + ADAM_WD * w_ref[...])

    spec = pl.BlockSpec((None, t, c), lambda i: (layer, i, 0))
    pspec = pl.BlockSpec((n, t, cp), lambda i: (0, base // t + i, 0))
    in_specs = [pspec, spec, spec, spec]
    args = [parts, w, m, v]
    aliases = {}
    if chained:
        if prev is None:
            prev = [lax.empty((nl, r, c), F32) for _ in range(4)]
        in_specs += [pl.BlockSpec(memory_space=pl.ANY)] * 4
        args += list(prev)
        aliases = {4 + i: i for i in range(4)}
    return pl.pallas_call(kern, out_shape=[jax.ShapeDtypeStruct((nl, r, c), F32)] * 4, grid=(r // t,),
                          in_specs=in_specs, out_specs=[spec] * 4, input_output_aliases=aliases, name=name,
                          compiler_params=_params(1))(*args)


def _me():
    return lax.axis_index("x"), lax.axis_index("y"), lax.axis_index("c")


def _flip(x, y, c, mask):
    return (jnp.where((mask >> 2) & 1, 1 - x, x), jnp.where((mask >> 1) & 1, 1 - y, y), jnp.where(mask & 1, 1 - c, c))


def _index(x, y, c):
    return 4 * x + 2 * y + c


def _exchange(name, arr, gather):
    out_shape = (N_DEV,) + arr.shape if gather else arr.shape

    def kern(in_ref, out_ref, send_sems, recv_sems, local_sem):
        x, y, c = _me()
        me = _index(x, y, c)
        mine = pltpu.make_async_copy(in_ref if gather else in_ref.at[me], out_ref.at[me], local_sem)
        mine.start()
        copies = []
        for mask in range(1, N_DEV):
            px, py, pc = _flip(x, y, c, mask)
            peer = _index(px, py, pc)
            cp = pltpu.make_async_remote_copy(
                src_ref=in_ref if gather else in_ref.at[peer], dst_ref=out_ref.at[me],
                send_sem=send_sems.at[mask - 1], recv_sem=recv_sems.at[mask - 1],
                device_id=(px, py, pc), device_id_type=MESH)
            cp.start()
            copies.append((cp, peer))
        for mask, (cp, peer) in enumerate(copies, start=1):
            pltpu.make_async_remote_copy(
                src_ref=in_ref if gather else in_ref.at[peer], dst_ref=out_ref.at[peer],
                send_sem=send_sems.at[mask - 1], recv_sem=recv_sems.at[mask - 1],
                device_id=_flip(x, y, c, mask), device_id_type=MESH).wait_recv()
        for cp, _ in copies:
            cp.wait_send()
        mine.wait()

    any_spec = pl.BlockSpec(memory_space=pl.ANY)
    return pl.pallas_call(
        kern, out_shape=jax.ShapeDtypeStruct(out_shape, arr.dtype), in_specs=[any_spec], out_specs=any_spec,
        scratch_shapes=[pltpu.SemaphoreType.DMA((N_DEV - 1,)), pltpu.SemaphoreType.DMA((N_DEV - 1,)),
                        pltpu.SemaphoreType.DMA],
        name=name, compiler_params=pltpu.CompilerParams(has_side_effects=True))(arr)


def all_gather(name, arr):
    return _exchange(name, arr, True)


def all_to_all(name, arr):
    return _exchange(name, arr, False)


_HBM = pl.BlockSpec(memory_space=pltpu.HBM)
_SEM = pl.BlockSpec(memory_space=pltpu.SEMAPHORE)
_EFFECT = pltpu.SideEffectType.DATAFLOW_SIDE_EFFECTING


def _split_copies(srcs, lands, send_sems, recv_sems, gather):
    x, y, c = _me()
    me = _index(x, y, c)
    out = []
    for a, (src, land) in enumerate(zip(srcs, lands)):
        for mask in range(1, N_DEV):
            px, py, pc = _flip(x, y, c, mask)
            peer = _index(px, py, pc)
            sem = (N_DEV - 1) * a + mask - 1
            mk = lambda dst_slot: pltpu.make_async_remote_copy(
                src_ref=src if gather else src.at[peer], dst_ref=land.at[dst_slot],
                send_sem=send_sems.at[sem], recv_sem=recv_sems.at[sem], device_id=(px, py, pc), device_id_type=MESH)
            out.append((mk(me), mk(peer)))
    return out


def exchange_start(name, arrs, gather):
    k = len(arrs)
    land_shapes = [((N_DEV,) + a.shape if gather else a.shape) for a in arrs]

    def body(*refs):
        srcs, lands = refs[:k], refs[k:2 * k]
        send_sems, recv_sems = refs[2 * k], refs[2 * k + 1]
        token = refs[-1]
        for mine, _ in _split_copies(srcs, lands, send_sems, recv_sems, gather):
            mine.start()
        token[...] = jnp.zeros(token.shape, token.dtype)

    n_sem = (N_DEV - 1) * k
    res = pl.pallas_call(
        body, name=name,
        out_shape=(pltpu.SemaphoreType.DMA((n_sem,)), pltpu.SemaphoreType.DMA((n_sem,)),
                   *[pltpu.HBM(a.shape, a.dtype) for a in arrs],
                   *[pltpu.HBM(shp, a.dtype) for shp, a in zip(land_shapes, arrs)],
                   jax.ShapeDtypeStruct((8, 128), F32)),
        in_specs=[_HBM] * (2 * k), out_specs=(_SEM, _SEM, *[_HBM] * (2 * k), pl.BlockSpec(memory_space=pltpu.VMEM)),
        input_output_aliases={i: 2 + i for i in range(2 * k)},
        compiler_params=pltpu.CompilerParams(has_side_effects=_EFFECT),
    )(*[pltpu.with_memory_space_constraint(a, pltpu.HBM) for a in arrs],
      *[pltpu.with_memory_space_constraint(lax.empty(shp, a.dtype), pltpu.HBM) for shp, a in zip(land_shapes, arrs)])
    return res[0], res[1], list(res[2:2 + k]), list(res[2 + k:2 + 2 * k]), res[-1]


def exchange_wait(name, started, after, gather):
    send_sems, recv_sems, thrus, lands, _ = started
    k = len(thrus)

    def body(*refs):
        srcs, lnds = refs[:k], refs[k:2 * k]
        s_sems, r_sems = refs[2 * k], refs[2 * k + 1]
        for mine, theirs in _split_copies(srcs, lnds, s_sems, r_sems, gather):
            mine.wait_send()
            theirs.wait_recv()

    res = pl.pallas_call(
        body, name=name,
        out_shape=tuple(pltpu.HBM(a.shape, a.dtype) for a in thrus + lands),
        in_specs=[_HBM] * (2 * k) + [_SEM, _SEM, pl.BlockSpec(memory_space=pl.ANY)], out_specs=tuple([_HBM] * (2 * k)),
        input_output_aliases={i: i for i in range(2 * k)},
        compiler_params=pltpu.CompilerParams(has_side_effects=_EFFECT),
    )(*thrus, *lands, send_sems, recv_sems, after)
    return list(res[k:])


def _pad_heads(w, real, padded):
    k = w.shape[0]
    w3 = w.reshape(k, H, real)
    return jnp.pad(w3, ((0, 0), (0, 0), (0, padded - real))).reshape(k, H * padded)


def _unpad_heads(w, real, padded):
    k = w.shape[0]
    return w.reshape(k, H, padded)[:, :, :real].reshape(k, H * real)


def _s5_place(ab_re, ab_im, bb_re_t, bb_im_t, c_re, c_im):
    eye = jnp.eye(GB, dtype=F32)

    def wb_part(bt):
        x4 = bt.reshape(P, NBLK, GB, N).transpose(1, 2, 0, 3)
        return jnp.einsum('kgpn,gh->kgphn', x4, eye).reshape(NBLK, GB * P, HALF)

    def wc_part(cc):
        x4 = cc.reshape(NBLK, GB, P, N)
        return jnp.einsum('kgpn,gh->kgnhp', x4, eye).reshape(NBLK, HALF, GB * P)

    wb = jnp.concatenate([wb_part(bb_re_t), wb_part(bb_im_t)], axis=-1)
    wc = jnp.concatenate([wc_part(c_re), -wc_part(c_im)], axis=1)
    a_tab = jnp.concatenate([ab_re.reshape(NBLK, 1, HALF), ab_im.reshape(NBLK, 1, HALF)], axis=-1)
    return wb.astype(_MXU), wc.astype(_MXU), a_tab


def _s5_unplace(dwb, dwc, da):
    eye = jnp.eye(GB, dtype=F32)

    def wb_part(dpart):
        x5 = dpart.reshape(NBLK, GB, P, GB, N)
        return jnp.einsum('kgphn,gh->kgpn', x5, eye).transpose(2, 0, 1, 3).reshape(P, G * N)

    def wc_part(dpart):
        x5 = dpart.reshape(NBLK, GB, N, GB, P)
        return jnp.einsum('kgnhp,gh->kgpn', x5, eye).reshape(G, P, N)

    dbb_re_t, dbb_im_t = wb_part(dwb[..., :HALF]), wb_part(dwb[..., HALF:])
    dc_re, dc_im = wc_part(dwc[:, :HALF]), -wc_part(dwc[:, HALF:])
    dab_re, dab_im = da[:, :HALF].reshape(1, G * N), da[:, HALF:].reshape(1, G * N)
    return dab_re, dab_im, dbb_re_t, dbb_im_t, dc_re, dc_im


def _row(v):
    return v.reshape(1, -1)


def kernel(x, c, positions, ada_w, ada_b, norm1_g, norm2_g, ffn_w_gate, ffn_w_up, ffn_w_down, s5_lam_re, s5_lam_im, s5_log_dt, s5_b_re, s5_b_im, s5_c_re, s5_c_im, s5_d, s5_w_glu, s5_b_glu, kv_ada_w, kv_ada_b, kv_norm_g, w_kv_a, kv_a_norm_g, w_kv_b, k_nope_norm_g, k_rope_norm_g, mla_w_dq, mla_q_norm_g, mla_w_uq, mla_q_nope_norm_g, mla_q_rope_norm_g, mla_w_o, loss_target, m_ada_w, m_ada_b, m_norm1_g, m_norm2_g, m_ffn_w_gate, m_ffn_w_up, m_ffn_w_down, m_s5_lam_re, m_s5_lam_im, m_s5_log_dt, m_s5_b_re, m_s5_b_im, m_s5_c_re, m_s5_c_im, m_s5_d, m_s5_w_glu, m_s5_b_glu, m_kv_ada_w, m_kv_ada_b, m_kv_norm_g, m_w_kv_a, m_kv_a_norm_g, m_w_kv_b, m_k_nope_norm_g, m_k_rope_norm_g, m_mla_w_dq, m_mla_q_norm_g, m_mla_w_uq, m_mla_q_nope_norm_g, m_mla_q_rope_norm_g, m_mla_w_o, v_ada_w, v_ada_b, v_norm1_g, v_norm2_g, v_ffn_w_gate, v_ffn_w_up, v_ffn_w_down, v_s5_lam_re, v_s5_lam_im, v_s5_log_dt, v_s5_b_re, v_s5_b_im, v_s5_c_re, v_s5_c_im, v_s5_d, v_s5_w_glu, v_s5_b_glu, v_kv_ada_w, v_kv_ada_b, v_kv_norm_g, v_w_kv_a, v_kv_a_norm_g, v_w_kv_b, v_k_nope_norm_g, v_k_rope_norm_g, v_mla_w_dq, v_mla_q_norm_g, v_mla_w_uq, v_mla_q_nope_norm_g, v_mla_q_rope_norm_g, v_mla_w_o):
    W = dict(ada_w=ada_w, ada_b=ada_b, norm1_g=norm1_g, norm2_g=norm2_g, ffn_w_gate=ffn_w_gate, ffn_w_up=ffn_w_up, ffn_w_down=ffn_w_down, s5_lam_re=s5_lam_re, s5_lam_im=s5_lam_im, s5_log_dt=s5_log_dt, s5_b_re=s5_b_re, s5_b_im=s5_b_im, s5_c_re=s5_c_re, s5_c_im=s5_c_im, s5_d=s5_d, s5_w_glu=s5_w_glu, s5_b_glu=s5_b_glu, kv_ada_w=kv_ada_w, kv_ada_b=kv_ada_b, kv_norm_g=kv_norm_g, w_kv_a=w_kv_a, kv_a_norm_g=kv_a_norm_g, w_kv_b=w_kv_b, k_nope_norm_g=k_nope_norm_g, k_rope_norm_g=k_rope_norm_g, mla_w_dq=mla_w_dq, mla_q_norm_g=mla_q_norm_g, mla_w_uq=mla_w_uq, mla_q_nope_norm_g=mla_q_nope_norm_g, mla_q_rope_norm_g=mla_q_rope_norm_g, mla_w_o=mla_w_o)
    M = dict(ada_w=m_ada_w, ada_b=m_ada_b, norm1_g=m_norm1_g, norm2_g=m_norm2_g, ffn_w_gate=m_ffn_w_gate, ffn_w_up=m_ffn_w_up, ffn_w_down=m_ffn_w_down, s5_lam_re=m_s5_lam_re, s5_lam_im=m_s5_lam_im, s5_log_dt=m_s5_log_dt, s5_b_re=m_s5_b_re, s5_b_im=m_s5_b_im, s5_c_re=m_s5_c_re, s5_c_im=m_s5_c_im, s5_d=m_s5_d, s5_w_glu=m_s5_w_glu, s5_b_glu=m_s5_b_glu, kv_ada_w=m_kv_ada_w, kv_ada_b=m_kv_ada_b, kv_norm_g=m_kv_norm_g, w_kv_a=m_w_kv_a, kv_a_norm_g=m_kv_a_norm_g, w_kv_b=m_w_kv_b, k_nope_norm_g=m_k_nope_norm_g, k_rope_norm_g=m_k_rope_norm_g, mla_w_dq=m_mla_w_dq, mla_q_norm_g=m_mla_q_norm_g, mla_w_uq=m_mla_w_uq, mla_q_nope_norm_g=m_mla_q_nope_norm_g, mla_q_rope_norm_g=m_mla_q_rope_norm_g, mla_w_o=m_mla_w_o)
    V = dict(ada_w=v_ada_w, ada_b=v_ada_b, norm1_g=v_norm1_g, norm2_g=v_norm2_g, ffn_w_gate=v_ffn_w_gate, ffn_w_up=v_ffn_w_up, ffn_w_down=v_ffn_w_down, s5_lam_re=v_s5_lam_re, s5_lam_im=v_s5_lam_im, s5_log_dt=v_s5_log_dt, s5_b_re=v_s5_b_re, s5_b_im=v_s5_b_im, s5_c_re=v_s5_c_re, s5_c_im=v_s5_c_im, s5_d=v_s5_d, s5_w_glu=v_s5_w_glu, s5_b_glu=v_s5_b_glu, kv_ada_w=v_kv_ada_w, kv_ada_b=v_kv_ada_b, kv_norm_g=v_kv_norm_g, w_kv_a=v_w_kv_a, kv_a_norm_g=v_kv_a_norm_g, w_kv_b=v_w_kv_b, k_nope_norm_g=v_k_nope_norm_g, k_rope_norm_g=v_k_rope_norm_g, mla_w_dq=v_mla_w_dq, mla_q_norm_g=v_mla_q_norm_g, mla_w_uq=v_mla_w_uq, mla_q_nope_norm_g=v_mla_q_nope_norm_g, mla_q_rope_norm_g=v_mla_q_rope_norm_g, mla_w_o=v_mla_w_o)
    return _step(x[0], c, positions, loss_target[0], W, M, V)


WEIGHT_NAMES = ['ada_w', 'ada_b', 'norm1_g', 'norm2_g', 'ffn_w_gate', 'ffn_w_up', 'ffn_w_down', 's5_lam_re', 's5_lam_im', 's5_log_dt', 's5_b_re', 's5_b_im', 's5_c_re', 's5_c_im', 's5_d', 's5_w_glu', 's5_b_glu', 'kv_ada_w', 'kv_ada_b', 'kv_norm_g', 'w_kv_a', 'kv_a_norm_g', 'w_kv_b', 'k_nope_norm_g', 'k_rope_norm_g', 'mla_w_dq', 'mla_q_norm_g', 'mla_w_uq', 'mla_q_nope_norm_g', 'mla_q_rope_norm_g', 'mla_w_o']
REPLICATED = ['ada_b', 'norm1_g', 'norm2_g', 's5_lam_re', 's5_lam_im', 's5_log_dt', 's5_b_re', 's5_b_im', 's5_c_re', 's5_c_im', 'kv_ada_b', 'kv_norm_g', 'kv_a_norm_g', 'k_nope_norm_g', 'k_rope_norm_g', 'mla_q_norm_g', 'mla_q_nope_norm_g', 'mla_q_rope_norm_g']
SHARDED_VEC = ['s5_d', 's5_b_glu']


def _step(x, c, positions, target, W, M, V):
    s = x.shape[0]
    me = _index(*_me())
    mxu = lambda a: a.astype(_MXU)

    pad_c = lambda a: jnp.pad(a, ((0, 0), (0, FFB - FF // N_DEV)))
    pad_r = lambda a: jnp.pad(a, ((0, FFB - FF // N_DEV), (0, 0)))
    cols = lambda g: g.transpose(1, 0, 2).reshape(g.shape[1], N_DEV * g.shape[2])
    rows = lambda g: g.reshape(N_DEV * g.shape[1], g.shape[2])

    def local_pack(l):
        second = W['s5_w_glu'][l] if l < N_A else W['mla_w_o'][l - N_A]
        arrs = [jnp.concatenate([mxu(pad_c(W['ffn_w_gate'][l])), mxu(pad_c(W['ffn_w_up'][l]))], axis=0),
                jnp.concatenate([mxu(pad_r(W['ffn_w_down'][l])), mxu(second)], axis=0)]
        if l == N_A:
            arrs += [jnp.concatenate([mxu(W['w_kv_b']), mxu(W['mla_w_dq'][0])], axis=0), mxu(W['w_kv_a'])]
        if l > N_A:
            arrs += [mxu(W['mla_w_dq'][l - N_A])]
        if l >= N_A:
            arrs += [mxu(W['mla_w_uq'][l - N_A])]
        return arrs

    gathers = [exchange_start(f"gather_start_{l}", local_pack(l), True) for l in range(DEPTH)]
    tokens = sum(g[4][0, 0] for g in gathers)

    def layer_weights(l, after):
        lands = exchange_wait(f"gather_wait_{l}", gathers[l], after, True)
        full = [lax.dynamic_update_slice(ld, src[None], (me,) + (0,) * src.ndim) for ld, src in zip(lands, gathers[l][2])]
        w = {'wg': cols(full[0][:, :D]), 'wu': cols(full[0][:, D:]), 'wd': rows(full[1][:, :FFB]),
             'second': rows(full[1][:, FFB:])}
        if l >= N_A:
            if l == N_A:
                wkvb3 = cols(full[2][:, :KVL]).reshape(KVL, H, DN + DV)
                wkva = rows(full[3])
                w['wa_pad'] = jnp.concatenate([wkva[:, :KVL], jnp.zeros((D, DN), _MXU), wkva[:, KVL:],
                                               jnp.zeros((D, HD - DN - DR), _MXU)], axis=1)
                w['wkn_pad'] = jnp.pad(wkvb3[:, :, :DN], ((0, 0), (0, 0), (0, HD - DN))).reshape(KVL, H * HD)
                w['wv'] = wkvb3[:, :, DN:].reshape(KVL, H * DV)
                w['wdq'] = rows(full[2][:, KVL:])
            else:
                w['wdq'] = rows(full[2])
            w['wuq_pad'] = _pad_heads(cols(full[-1]), DN + DR, HD)
        return w

    vec = jnp.concatenate([c.reshape(-1), W['s5_d'].reshape(-1), W['s5_b_glu'].reshape(-1)]).reshape(1, -1)
    vec = jnp.pad(vec + tokens, ((0, 7), (0, 0)))
    gv = all_gather("gather_vectors", vec)[:, 0, :]
    c_all = gv[:, :D]
    d_full = jnp.concatenate([gv[d, D:D + 2 * 128].reshape(N_A, 128) for d in range(N_DEV)], axis=1)
    bglu_full = jnp.concatenate([gv[d, D + 256:D + 512].reshape(N_A, 128) for d in range(N_DEV)], axis=1)

    ca_all = jax.nn.silu(c_all)
    w_mod = jnp.concatenate([W['ada_w'][l] for l in range(DEPTH)] + [W['kv_ada_w']], axis=1)
    n_mod = w_mod.shape[1]
    mod_cols = small_matmul("mod_matmul", ca_all, w_mod)
    gm = all_gather("gather_mod", mod_cols)
    mine = lax.dynamic_index_in_dim(gm, me, axis=1, keepdims=False)
    per_l = D * 6 // N_DEV
    mods = []
    for l in range(DEPTH):
        full = jnp.concatenate([mine[d, per_l * l:per_l * (l + 1)] for d in range(N_DEV)]) + W['ada_b'][l]
        mods.append([_row(full[D * i:D * (i + 1)]) for i in range(6)])
    kfull = jnp.concatenate([mine[d, per_l * DEPTH:] for d in range(N_DEV)]) + W['kv_ada_b']
    k_shift, k_scale = _row(kfull[:D]), _row(kfull[D:])

    inv = 1.0 / (ROPE_THETA ** (np.arange(0, DR, 2, dtype=np.float32) / DR))
    inv128 = np.zeros((1, HD), np.float32)
    inv128[0, DN:DN + DR // 2] = inv
    inv128[0, DN + DR // 2:DN + DR] = inv
    cosf, sinf = rope_tables("rope_tables", positions.reshape(s, 1), jnp.asarray(inv128))
    zpad = lambda n: jnp.zeros((n,), F32)
    gkn128 = _row(jnp.concatenate([W['k_nope_norm_g'], zpad(HD - DN)]))
    gkr128 = _row(jnp.concatenate([zpad(DN), W['k_rope_norm_g'], zpad(HD - DN - DR)]))
    gq128 = [_row(jnp.concatenate([W['mla_q_nope_norm_g'][j], W['mla_q_rope_norm_g'][j], zpad(HD - DN - DR)]))
             for j in range(2)]

    expand = jnp.asarray(np.kron(np.eye(G, dtype=np.float32), np.ones((1, N), np.float32)))
    s5_raw, s5_mats = [], []
    for l in range(N_A):
        raw = (_row(W['s5_lam_re'][l]), _row(W['s5_lam_im'][l]), _row(W['s5_log_dt'][l]),
               W['s5_b_re'][l].transpose(2, 0, 1).reshape(P, G * N), W['s5_b_im'][l].transpose(2, 0, 1).reshape(P, G * N))
        ab_re, ab_im, bb_re_t, bb_im_t = s5_prep_fwd(f"s5_prep_fwd", *raw, expand)
        s5_raw.append(raw)
        s5_mats.append(_s5_place(ab_re, ab_im, bb_re_t, bb_im_t, W['s5_c_re'][l], W['s5_c_im'][l]))

    g1 = [_row(W['norm1_g'][l]) for l in range(DEPTH)]
    g2 = [_row(W['norm2_g'][l]) for l in range(DEPTH)]
    saved = []
    xs = x
    kv = None
    lw = [None] * DEPTH
    for l in range(DEPTH):
        sh1, sc1, gt1, sh2, sc2, gt2 = mods[l]
        rec = {'x_in': xs}
        if l >= N_A:
            lw[l] = layer_weights(l, xs)
        if l == N_A:
            kv_smalls = [_row(W['kv_norm_g']), k_shift, k_scale, _row(W['kv_a_norm_g']), gkn128, gkr128]
            kv_w = [lw[l]['wa_pad'], lw[l]['wkn_pad'], lw[l]['wv']]
            k_mat, v_mat = seg_forward("kv_fwd", seg_kv, [xs], kv_smalls, [cosf, sinf], kv_w,
                                       [(H * HD, _MXU), (H * DV, _MXU)], tap_widths=(KVL + HD, H * HD, H * DV))
            kv = {'x_in': xs, 'smalls': kv_smalls, 'k': k_mat, 'v': v_mat, 'w': kv_w}
        if l < N_A:
            (h,) = seg_forward("pre_fwd", seg_pre, [xs], [g1[l], sh1, sc1], [], [], [(D, F32)])
            wb, wc, a_tab = s5_mats[l]
            y, s0 = s5_scan_fwd("s5_scan_fwd", h, wb, wc, a_tab, _row(d_full[l]))
            lw[l] = layer_weights(l, y)
            (x_mid,) = seg_forward("glu_fwd", seg_glu, [xs, y], [gt1, _row(bglu_full[l])], [], [lw[l]['second']],
                                   [(D, F32)], tap_widths=(D,))
            rec.update(h=h, y=y, s0=s0)
        else:
            j = l - N_A
            q_smalls = [g1[l], sh1, sc1, _row(W['mla_q_norm_g'][j]), gq128[j]]
            (q_mat,) = seg_forward("q_fwd", seg_q, [xs], q_smalls, [cosf, sinf], [lw[l]['wdq'], lw[l]['wuq_pad']],
                                   [(H * HD, _MXU)], tap_widths=(QL, H * HD))
            o_mat, lse = attn_fwd("attn_fwd", q_mat, kv['k'], kv['v'])
            (x_mid,) = seg_forward("o_fwd", seg_o, [xs, o_mat], [gt1], [], [lw[l]['second']], [(D, F32)],
                                   tap_widths=(D,))
            rec.update(q=q_mat, o=o_mat, lse=lse, q_smalls=q_smalls)
        rec['x_mid'] = x_mid
        (xs,) = seg_forward("ffn_fwd", seg_ffn, [x_mid], [g2[l], sh2, sc2, gt2], [],
                            [lw[l]['wg'], lw[l]['wu'], lw[l]['wd']], [(D, F32)], tap_widths=(FFP, FFP, D))
        saved.append(rec)

    dy, loss_part = loss_kernel("loss", xs, target)
    loss = lax.psum(loss_part[0, 0], ("x", "y", "c"))

    rblk = lambda a: a.reshape(N_DEV, a.shape[0] // N_DEV, a.shape[1])
    cblk = lambda a: a.reshape(a.shape[0], N_DEV, a.shape[1] // N_DEV).transpose(1, 0, 2)
    dmod = [None] * DEPTH
    dk_tot = []
    dv_tot = []
    dx = dy
    sends = [None] * DEPTH
    send_token = jnp.zeros((1, 1), F32)
    g_n1 = [None] * DEPTH
    g_n2 = [None] * DEPTH
    g_bglu = [None] * N_A
    g_dskip = [None] * N_A
    g_s5 = [None] * N_A
    g_qn, g_q128 = [None] * 2, [None] * 2
    for l in range(DEPTH - 1, -1, -1):
        rec = saved[l]
        sh1, sc1, gt1, sh2, sc2, gt2 = mods[l]
        dx, dgate, dup, dyd, h_b, a_b, dg2, dsh2, dsc2, dgt2 = ffn_backward(
            "ffn_bwd", rec['x_mid'], dx, g2[l], sh2, sc2, gt2 + send_token, lw[l]['wg'], lw[l]['wu'], lw[l]['wd'])
        out_l = [matmul_tn("tn_ffn_in", h_b, dgate, _MXU, col_blocks=N_DEV),
                 matmul_tn("tn_ffn_in", h_b, dup, _MXU, col_blocks=N_DEV),
                 matmul_tn("tn_ffn_out", a_b, dyd, _MXU).reshape(N_DEV, FFB, D)]
        g_n2[l] = dg2
        if l < N_A:
            (dx, dyy), (dz,), (g_b,), (dgt1, dbg) = seg_backward(
                "glu_bwd", seg_glu, [rec['x_in'], rec['y']], [gt1, _row(bglu_full[l])], [], [lw[l]['second']],
                [dx], (D,), (D,))
            out_l.append(rblk(matmul_tn("tn_sq", g_b, dz, _MXU)))
            g_bglu[l] = dbg
            wb, wc, a_tab = s5_mats[l]
            dh, dwb, dwc, da, dd = s5_scan_bwd("s5_scan_bwd", rec['h'], dyy, rec['s0'], wb, wc, a_tab, _row(d_full[l]))
            g_dskip[l] = dd
            dab_re, dab_im, dbb_re_t, dbb_im_t, dc_re, dc_im = _s5_unplace(dwb, dwc, da)
            dlr, dli, dldt, dbr_t, dbi_t = s5_prep_bwd("s5_prep_bwd", *s5_raw[l], expand,
                                                       (dab_re, dab_im, dbb_re_t, dbb_im_t))
            g_s5[l] = (dlr.reshape(G, N), dli.reshape(G, N), dldt.reshape(G),
                       dbr_t.reshape(P, G, N).transpose(1, 2, 0), dbi_t.reshape(P, G, N).transpose(1, 2, 0), dc_re, dc_im)
            (dx,), _, _, (dg1, dsh1, dsc1) = seg_backward(
                "pre_bwd", seg_pre, [rec['x_in']], [g1[l], sh1, sc1], [], [], [dh], (), (), dx_add=dx)
        else:
            j = l - N_A
            (dx, do), (dzo,), (o_b,), (dgt1,) = seg_backward(
                "o_bwd", seg_o, [rec['x_in'], rec['o']], [gt1], [], [lw[l]['second']], [dx], (D,), (D,))
            out_l.append(rblk(matmul_tn("tn_sq", o_b, dzo, _MXU)))
            dq, dk, dv = attn_bwd("attn_bwd", rec['q'], kv['k'], kv['v'], rec['o'], do, rec['lse'])
            dk_tot.append(dk)
            dv_tot.append(dv)
            (dx,), (dql, dqq), (hq_b, qn_b), (dg1, dsh1, dsc1, dqg, dq128) = seg_backward(
                "q_bwd", seg_q, [rec['x_in']], rec['q_smalls'], [cosf, sinf], [lw[l]['wdq'], lw[l]['wuq_pad']],
                [dq], (QL, H * HD), (D, QL), dx_add=dx)
            g_dq = rblk(matmul_tn("tn_dq", hq_b, dql, _MXU))
            g_uq = cblk(_unpad_heads(matmul_tn("tn_uq", qn_b, dqq, _MXU), DN + DR, HD))
            g_qn[j], g_q128[j] = dqg, dq128
        g_n1[l] = dg1
        dmod[l] = jnp.concatenate([dsh1, dsc1, dgt1, dsh2, dsc2, dgt2], axis=1)
        if l == N_A:
            dkk = sum_parts("sum_dk", jnp.stack(dk_tot))
            dvv = sum_parts("sum_dv", jnp.stack(dv_tot))
            (dx,), (dta, dtk, dtv), (hk_b, ckv_b), (dkg, dksh, dksc, dag, dgkn, dgkr) = seg_backward(
                "kv_bwd", seg_kv, [kv['x_in']], kv['smalls'], [cosf, sinf], kv['w'],
                [dkk, dvv], (KVL + HD, H * HD, H * DV), (D, KVL), dx_add=dx)
            g_wa = matmul_tn("tn_kva", hk_b, dta, _MXU)
            g_wa = jnp.concatenate([g_wa[:, :KVL], g_wa[:, KVL + DN:KVL + DN + DR]], axis=1)
            g_kn = matmul_tn("tn_kn", ckv_b, dtk, _MXU).reshape(KVL, H, HD)[:, :, :DN]
            g_v = matmul_tn("tn_v", ckv_b, dtv, _MXU).reshape(KVL, H, DV)
            g_wkvb = jnp.concatenate([g_kn, g_v], axis=2).reshape(KVL, H * (DN + DV))
            dkmod = jnp.concatenate([dksh, dksc], axis=1)
            out_l += [jnp.concatenate([cblk(g_wkvb), g_dq], axis=1), rblk(g_wa)]
        if l > N_A:
            out_l.append(g_dq)
        if l >= N_A:
            out_l.append(g_uq)
        sends[l] = exchange_start(f"a2a_start_{l}", out_l, False)
        send_token = sends[l][4][0:1, 0:1]
    grad_x = dx

    dm = (jnp.concatenate(dmod + [dkmod], axis=1) + send_token)[0]
    per_dev = []
    for d in range(N_DEV):
        cols = [dm[6 * D * l + per_l * d:6 * D * l + per_l * (d + 1)] for l in range(DEPTH)]
        cols.append(dm[6 * D * DEPTH + (2 * D // N_DEV) * d:6 * D * DEPTH + (2 * D // N_DEV) * (d + 1)])
        per_dev.append(jnp.concatenate(cols))
    dm_dev = jnp.stack(per_dev)
    gdm = all_gather("gather_dmod", dm_dev)
    dm_mine = lax.dynamic_index_in_dim(gdm, me, axis=1, keepdims=False)
    g_wmod = small_matmul_tn("dmod_matmul", ca_all, dm_mine)
    g_ada_w = jnp.stack([g_wmod[:, per_l * l:per_l * (l + 1)] for l in range(DEPTH)])
    g_kv_ada_w = g_wmod[:, per_l * DEPTH:]
    dm_sum = sum_parts("sum_dmod", gdm.reshape(N_DEV, N_DEV, n_mod))
    g_ada_b = jnp.stack([jnp.concatenate([dm_sum[d, per_l * l:per_l * (l + 1)] for d in range(N_DEV)])
                         for l in range(DEPTH)])
    g_kv_ada_b = jnp.concatenate([dm_sum[d, per_l * DEPTH:] for d in range(N_DEV)])

    small = {
        'norm1_g': jnp.concatenate(g_n1, axis=0), 'norm2_g': jnp.concatenate(g_n2, axis=0),
        's5_lam_re': jnp.stack([g_s5[l][0] for l in range(N_A)]), 's5_lam_im': jnp.stack([g_s5[l][1] for l in range(N_A)]),
        's5_log_dt': jnp.stack([g_s5[l][2] for l in range(N_A)]),
        's5_b_re': jnp.stack([g_s5[l][3] for l in range(N_A)]), 's5_b_im': jnp.stack([g_s5[l][4] for l in range(N_A)]),
        's5_c_re': jnp.stack([g_s5[l][5] for l in range(N_A)]), 's5_c_im': jnp.stack([g_s5[l][6] for l in range(N_A)]),
        'kv_norm_g': dkg, 'kv_a_norm_g': dag, 'k_nope_norm_g': dgkn[:, :DN], 'k_rope_norm_g': dgkr[:, DN:DN + DR],
        'mla_q_norm_g': jnp.concatenate(g_qn, axis=0),
        'mla_q_nope_norm_g': jnp.concatenate([g[:, :DN] for g in g_q128], axis=0),
        'mla_q_rope_norm_g': jnp.concatenate([g[:, DN:DN + DR] for g in g_q128], axis=0),
        's5_d': jnp.concatenate(g_dskip, axis=0), 's5_b_glu': jnp.concatenate(g_bglu, axis=0),
    }
    small_names = [n for n in REPLICATED if n not in ('ada_b', 'kv_ada_b')] + SHARDED_VEC
    flat_small = jnp.concatenate([small[n].reshape(-1) for n in small_names])
    n_small = int(flat_small.shape[0])
    pad_small = -(-n_small // 65536) * 65536
    flat_small = jnp.pad(flat_small, (0, pad_small - n_small)).reshape(pad_small // 128, 128)
    g_small_sum = sum_parts("sum_small", all_gather("gather_small", flat_small)).reshape(-1)
    grads = {}
    off = 0
    for n in small_names:
        size = int(np.prod(small[n].shape))
        full = g_small_sum[off:off + size]
        off += size
        if n in SHARDED_VEC:
            full = lax.dynamic_slice_in_dim(full.reshape(N_A, D), me * (D // N_DEV), D // N_DEV, axis=1)
        grads[n] = full.reshape(W[n].shape)
    grads['ada_b'] = g_ada_b
    grads['kv_ada_b'] = g_kv_ada_b

    packed_names = REPLICATED + SHARDED_VEC

    def pack(dct):
        flat_ = jnp.concatenate([dct[n].reshape(-1) for n in packed_names])
        n_ = int(flat_.shape[0])
        p_ = -(-n_ // 65536) * 65536
        return jnp.pad(flat_, (0, p_ - n_)).reshape(p_ // 128, 128)

    _, d_p, m_p, v_p = adamw("adamw_small", pack(grads)[None], pack(W)[None], pack(M)[None], pack(V)[None])
    out_delta, out_m, out_v = {}, {}, {}
    off = 0
    d_p, m_p, v_p = d_p.reshape(-1), m_p.reshape(-1), v_p.reshape(-1)
    for n in packed_names:
        size = int(np.prod(W[n].shape))
        out_delta[n] = d_p[off:off + size].reshape(W[n].shape)
        out_m[n] = m_p[off:off + size].reshape(W[n].shape)
        out_v[n] = v_p[off:off + size].reshape(W[n].shape)
        off += size

    def update(name, parts, base=0, stride=0):
        shp = W[name].shape
        shp3 = shp if len(shp) == 3 else (1,) + shp
        res = adamw("adamw_" + name, parts, W[name].reshape(shp3), M[name].reshape(shp3), V[name].reshape(shp3),
                    base, stride)
        grads[name], out_delta[name], out_m[name], out_v[name] = (a.reshape(shp) for a in res)

    update('ada_w', g_ada_w.reshape(1, DEPTH * D, per_l), 0, D)
    update('kv_ada_w', g_kv_ada_w[None])

    chains = {}

    def update_layer(name, parts, layer, base=0):
        shp = W[name].shape
        shp3 = shp if len(shp) == 3 else (1,) + shp
        chains[name] = adamw_layer(f"adamw_{name}_{layer}", parts, W[name].reshape(shp3), M[name].reshape(shp3),
                                   V[name].reshape(shp3), layer, chains.get(name), base)
        grads[name], out_delta[name], out_m[name], out_v[name] = (a.reshape(shp) for a in chains[name])

    for l in range(DEPTH):
        lands = exchange_wait(f"a2a_wait_{l}", sends[l], d_p, False)
        recv = [lax.dynamic_update_slice(ld, lax.dynamic_index_in_dim(src, me, 0, keepdims=True), (me,) + (0,) * (src.ndim - 1))
                for ld, src in zip(lands, sends[l][2])]
        update_layer('ffn_w_gate', recv[0], l)
        update_layer('ffn_w_up', recv[1], l)
        update_layer('ffn_w_down', recv[2], l)
        if l < N_A:
            update_layer('s5_w_glu', recv[3], l)
        else:
            update_layer('mla_w_o', recv[3], l - N_A)
            if l == N_A:
                update_layer('w_kv_b', recv[4], 0)
                update_layer('mla_w_dq', recv[4], 0, KVL)
                update_layer('w_kv_a', recv[5], 0)
            else:
                update_layer('mla_w_dq', recv[4], l - N_A)
            update_layer('mla_w_uq', recv[-1], l - N_A)

    return (loss, grad_x[None], *[grads[n] for n in WEIGHT_NAMES], *[out_delta[n] for n in WEIGHT_NAMES],
            *[out_m[n] for n in WEIGHT_NAMES], *[out_v[n] for n in WEIGHT_NAMES])
```

```python
import functools
import math

import numpy as np
import jax
import jax.numpy as jnp
from jax import lax
from jax.experimental import pallas as pl
from jax.experimental.pallas import tpu as pltpu

F32 = jnp.float32
_MXU = jnp.bfloat16
HI = lax.Precision.HIGHEST

D = 1024
DEPTH = 4
N_A = 2
FF = 2816
FFB = 384
FFP = 8 * FFB
N_DEV = 8
G = 64
P = 16
N = 64
GB = 8
NBLK = G // GB
HALF = GB * N
H = 16
HP = H // 2
DN, DR, DV = 64, 32, 64
HD = 128
QL = 256
KVL = 256
CHUNK = 64
ROPE_THETA = 10000.0
ATTN_SCALE = 1.0 / math.sqrt(DN + DR)
LOG2E = 1.4426950408889634
EXP2_SCALE = ATTN_SCALE * LOG2E
EPS = 1e-6
ADAM_LR, ADAM_B1, ADAM_B2, ADAM_EPS, ADAM_WD, ADAM_STEP = 0.001, 0.9, 0.999, 1e-08, 0.01, 10
VMEM_LIMIT = 56 * 1024 * 1024
MESH = pl.DeviceIdType.MESH

TILE_ROW = 256
TILE_ATT = 256
TILE_SCAN = 256


def _params(n_grid):
    return pltpu.CompilerParams(dimension_semantics=("arbitrary",) * n_grid, vmem_limit_bytes=VMEM_LIMIT)


@jax.custom_vjp
def mm(a, w):
    return jnp.dot(a.astype(_MXU), w, preferred_element_type=F32)


def _mm_fwd(a, w):
    return mm(a, w), w


def _mm_bwd(w, g):
    da = lax.dot_general(g.astype(_MXU), w, (((1,), (1,)), ((), ())), preferred_element_type=F32)
    return da, jnp.zeros_like(w)


mm.defvjp(_mm_fwd, _mm_bwd)


def rms(x, g):
    return x * lax.rsqrt(jnp.mean(x * x, axis=-1, keepdims=True) + EPS) * g


def modulate(h, shift, scale):
    return h * (1.0 + scale) + shift


def _lane(n=HD):
    return lax.broadcasted_iota(jnp.int32, (1, n), 1)


def _rot_matrix():
    r = lax.broadcasted_iota(jnp.int32, (HD, HD), 0)
    c = lax.broadcasted_iota(jnp.int32, (HD, HD), 1)
    first = (c >= DN) & (c < DN + DR // 2) & (r == c + DR // 2)
    second = (c >= DN + DR // 2) & (c < DN + DR) & (r == c - DR // 2)
    return jnp.where(first, -1.0, jnp.where(second, 1.0, 0.0)).astype(F32)


def head_norm_rope(xh, g128, cosf, sinf, rot, with_nope):
    lane = _lane()
    m_n = lane < DN
    m_r = (lane >= DN) & (lane < DN + DR)
    sq = xh * xh
    inv_r = lax.rsqrt(jnp.sum(jnp.where(m_r, sq, 0.0), axis=-1, keepdims=True) / DR + EPS)
    if with_nope:
        inv_n = lax.rsqrt(jnp.sum(jnp.where(m_n, sq, 0.0), axis=-1, keepdims=True) / DN + EPS)
        inv = jnp.where(m_n, inv_n, jnp.where(m_r, inv_r, 0.0))
    else:
        inv = jnp.where(m_r, inv_r, 0.0)
    xg = xh * inv * g128
    return xg * cosf + jnp.dot(xg, rot, precision=HI, preferred_element_type=F32) * sinf


def seg_pre(x, g, sh, sc):
    return (modulate(rms(x, g), sh, sc),), ()


def seg_ffn(x, g, sh, sc, gt, t_g, t_u, t_d, wg, wu, wd):
    h = modulate(rms(x, g), sh, sc)
    gate = mm(h, wg) + t_g
    up = mm(h, wu) + t_u
    a = jax.nn.silu(gate) * up
    y = mm(a, wd) + t_d
    return (x + gt * y,), (h.astype(_MXU), a.astype(_MXU))


def seg_glu(x, y, gt, b, t_z, w):
    g = jax.nn.gelu(y)
    z = mm(g, w) + b + t_z
    return (x + gt * (g * jax.nn.sigmoid(z)),), (g.astype(_MXU),)


def seg_o(x, o, gt, t_o, w):
    return (x + gt * (mm(o, w) + t_o),), (o.astype(_MXU),)


def seg_q(x, g, sh, sc, qg, g128, t_l, t_q, cosf, sinf, wdq, wuq):
    h = modulate(rms(x, g), sh, sc)
    ql = mm(h, wdq) + t_l
    qn = rms(ql, qg)
    q = mm(qn, wuq) + t_q
    rot = _rot_matrix()
    heads = [head_norm_rope(q[:, HD * i:HD * (i + 1)], g128, cosf, sinf, rot, True) for i in range(H)]
    return (jnp.concatenate(heads, axis=1),), (h.astype(_MXU), qn.astype(_MXU))


def seg_kv(x, g, sh, sc, ag, gkn, gkr, t_a, t_k, t_v, cosf, sinf, wa, wkn, wv):
    hk = modulate(rms(x, g), sh, sc)
    kva = mm(hk, wa) + t_a
    ckv = rms(kva[:, :KVL], ag)
    kr = head_norm_rope(kva[:, KVL:KVL + HD], gkr, cosf, sinf, _rot_matrix(), False)
    kn = mm(ckv, wkn) + t_k
    v = mm(ckv, wv) + t_v
    heads = []
    for i in range(H):
        kh = kn[:, HD * i:HD * (i + 1)]
        inv = lax.rsqrt(jnp.sum(kh * kh, axis=-1, keepdims=True) / DN + EPS)
        heads.append(kh * inv * gkn + kr)
    return (jnp.concatenate(heads, axis=1), v), (hk.astype(_MXU), ckv.astype(_MXU))


def _row_call(name, body_fn, rows, fulls, out_rows, out_accs, tile):
    s = rows[0].shape[0]
    n_tiles = s // tile
    n_rows, n_fulls, n_or, n_oa = len(rows), len(fulls), len(out_rows), len(out_accs)

    def kern(*refs):
        i = pl.program_id(0)
        row_v = [r[...] for r in refs[:n_rows]]
        full_v = [r[...] for r in refs[n_rows:n_rows + n_fulls]]
        o_refs = refs[n_rows + n_fulls:]
        ro, ao = body_fn(row_v, full_v)
        for r, v in zip(o_refs[:n_or], ro):
            r[...] = v.astype(r.dtype)
        if n_oa:
            @pl.when(i == 0)
            def _():
                for r in o_refs[n_or:]:
                    r[...] = jnp.zeros(r.shape, r.dtype)
            for r, v in zip(o_refs[n_or:], ao):
                r[...] += v.astype(r.dtype)

    in_specs = [pl.BlockSpec((tile, a.shape[1]), lambda i: (i, 0)) for a in rows]
    for a in fulls:
        big = a.size * a.dtype.itemsize > (1 << 20)
        nd = a.ndim
        in_specs.append(pl.BlockSpec(a.shape, functools.partial(lambda i, nd_: (0,) * nd_, nd_=nd),
                                     **({"pipeline_mode": pl.Buffered(1)} if big else {})))
    out_shape = [jax.ShapeDtypeStruct((s, w), dt) for w, dt in out_rows]
    out_shape += [jax.ShapeDtypeStruct(shp, dt) for shp, dt in out_accs]
    out_specs = [pl.BlockSpec((tile, w), lambda i: (i, 0)) for w, _ in out_rows]
    out_specs += [pl.BlockSpec(shp, functools.partial(lambda i, nd_: (0,) * nd_, nd_=len(shp))) for shp, _ in out_accs]
    res = pl.pallas_call(kern, out_shape=out_shape, grid=(n_tiles,), in_specs=in_specs, out_specs=out_specs,
                         name=name, compiler_params=_params(1))(*rows, *fulls)
    return list(res)


def seg_forward(name, seg, rows, smalls, consts_rows, consts_full, out_widths, tile=TILE_ROW, tap_widths=()):
    n_r, n_s, n_cr = len(rows), len(smalls), len(consts_rows)

    def body(row_v, full_v):
        t = row_v[0].shape[0]
        taps = [jnp.zeros((t, w), F32) for w in tap_widths]
        outs, _ = seg(*row_v[:n_r], *full_v[:n_s], *taps, *row_v[n_r:], *full_v[n_s:])
        return outs, ()

    return _row_call(name, body, list(rows) + list(consts_rows), list(smalls) + list(consts_full),
                     out_widths, [], tile)


def seg_backward(name, seg, rows, smalls, consts_rows, consts_full, cots, tap_widths, aux_widths,
                 dx_add=None, tile=TILE_ROW):
    n_r, n_s, n_cr, n_c = len(rows), len(smalls), len(consts_rows), len(cots)
    has_add = dx_add is not None

    def body(row_v, full_v):
        t = row_v[0].shape[0]
        prim_rows = row_v[:n_r]
        c_rows = row_v[n_r:n_r + n_cr]
        cot_v = row_v[n_r + n_cr:n_r + n_cr + n_c]
        add_v = row_v[n_r + n_cr + n_c] if has_add else None
        small_v = full_v[:n_s]
        c_full = full_v[n_s:]
        taps = [jnp.zeros((t, w), F32) for w in tap_widths]

        def f(*args):
            return seg(*args, *c_rows, *c_full)

        _, vjp_fn, aux = jax.vjp(f, *prim_rows, *small_v, *taps, has_aux=True)
        grads = vjp_fn(tuple(c.astype(F32) for c in cot_v))
        d_rows = list(grads[:n_r])
        if has_add:
            d_rows[0] = d_rows[0] + add_v
        d_small = grads[n_r:n_r + n_s]
        d_taps = grads[n_r + n_s:]
        return d_rows + list(d_taps) + list(aux), [jnp.sum(g, axis=0, keepdims=True) if g.shape[0] != 1 else g
                                                   for g in d_small]

    all_rows = list(rows) + list(consts_rows) + list(cots) + ([dx_add] if has_add else [])
    out_rows = [(a.shape[1], F32) for a in rows] + [(w, _MXU) for w in tap_widths] + [(w, _MXU) for w in aux_widths]
    out_accs = [((1, a.shape[1]), F32) for a in smalls]
    res = _row_call(name, body, all_rows, list(smalls) + list(consts_full), out_rows, out_accs, tile)
    n_t, n_a = len(tap_widths), len(aux_widths)
    return res[:n_r], res[n_r:n_r + n_t], res[n_r + n_t:n_r + n_t + n_a], res[n_r + n_t + n_a:]


def _split(n):
    if n <= 1024:
        return n
    for t in (1408, 1024, 768, 512, 256, 128):
        if n % t == 0:
            return t
    raise ValueError(n)


def matmul_tn(name, a, b, out_dtype, col_blocks=None):
    s, k1 = a.shape
    _, k2 = b.shape
    tm, ts = _split(k1), 512
    if col_blocks is None:
        tn, per_step, wblk = _split(k2), 1, None
    else:
        wblk = k2 // col_blocks
        per_step = max(1, min(col_blocks, 1536 // wblk))
        tn = per_step * wblk
    n_s = s // ts

    def kern(a_ref, b_ref, o_ref, acc_ref):
        k = pl.program_id(2)

        @pl.when(k == 0)
        def _():
            acc_ref[...] = jnp.zeros(acc_ref.shape, F32)

        acc_ref[...] += lax.dot_general(a_ref[...], b_ref[...], (((0,), (0,)), ((), ())),
                                        preferred_element_type=F32)

        @pl.when(k == n_s - 1)
        def _():
            if col_blocks is None:
                o_ref[...] = acc_ref[...].astype(o_ref.dtype)
            else:
                for cb in range(per_step):
                    o_ref[cb] = acc_ref[:, wblk * cb:wblk * (cb + 1)].astype(o_ref.dtype)

    if col_blocks is None:
        out_shape = jax.ShapeDtypeStruct((k1, k2), out_dtype)
        out_spec = pl.BlockSpec((tm, tn), lambda i, j, k: (i, j))
    else:
        out_shape = jax.ShapeDtypeStruct((col_blocks, k1, wblk), out_dtype)
        out_spec = pl.BlockSpec((per_step, tm, wblk), lambda i, j, k: (j, i, 0))
    return pl.pallas_call(
        kern, out_shape=out_shape, grid=(k1 // tm, k2 // tn, n_s),
        in_specs=[pl.BlockSpec((ts, tm), lambda i, j, k: (k, i)), pl.BlockSpec((ts, tn), lambda i, j, k: (k, j))],
        out_specs=out_spec,
        scratch_shapes=[pltpu.VMEM((tm, tn), F32)], name=name, compiler_params=_params(3))(a, b)


def ffn_backward(name, x, dxo, g, sh, sc, gt, wg, wu, wd, tile=TILE_ROW):
    s = x.shape[0]
    blk = 2 * FFB
    n_blk = wg.shape[1] // blk

    def kern(x_ref, dxo_ref, g_ref, sh_ref, sc_ref, gt_ref, wg_ref, wu_ref, wd_ref,
             dx_ref, dg_ref, du_ref, dy_ref, h_ref, a_ref, dgn_ref, dsh_ref, dsc_ref, dgt_ref):
        i = pl.program_id(0)

        @pl.when(i == 0)
        def _():
            for r in (dgn_ref, dsh_ref, dsc_ref, dgt_ref):
                r[...] = jnp.zeros(r.shape, F32)

        dxo = dxo_ref[...]
        h, pre_vjp = jax.vjp(lambda *p: modulate(rms(p[0], p[1]), p[2], p[3]), x_ref[...], g_ref[...], sh_ref[...],
                             sc_ref[...])
        hb = h.astype(_MXU)
        h_ref[...] = hb
        dyb = (gt_ref[...] * dxo).astype(_MXU)
        dy_ref[...] = dyb
        y = jnp.zeros((tile, D), F32)
        dh = jnp.zeros((tile, D), F32)
        tr = (((1,), (1,)), ((), ()))
        for c in range(n_blk):
            cs = slice(blk * c, blk * (c + 1))
            gate = jnp.dot(hb, wg_ref[:, cs], preferred_element_type=F32)
            up = jnp.dot(hb, wu_ref[:, cs], preferred_element_type=F32)
            sig = jax.nn.sigmoid(gate)
            sl = gate * sig
            ab = (sl * up).astype(_MXU)
            a_ref[:, cs] = ab
            y = y + jnp.dot(ab, wd_ref[cs, :], preferred_element_type=F32)
            da = lax.dot_general(dyb, wd_ref[cs, :], tr, preferred_element_type=F32)
            dgb = (da * up * (sig * (1.0 + gate * (1.0 - sig)))).astype(_MXU)
            dub = (da * sl).astype(_MXU)
            dg_ref[:, cs] = dgb
            du_ref[:, cs] = dub
            dh = dh + lax.dot_general(dgb, wg_ref[:, cs], tr, preferred_element_type=F32) \
                + lax.dot_general(dub, wu_ref[:, cs], tr, preferred_element_type=F32)
        dgt_ref[...] += jnp.sum(dxo * y, axis=0, keepdims=True)
        dx_pre, dgn, dsh, dsc = pre_vjp(dh)
        dx_ref[...] = dxo + dx_pre
        dgn_ref[...] += dgn
        dsh_ref[...] += dsh
        dsc_ref[...] += dsc

    row = lambda w: pl.BlockSpec((tile, w), lambda i: (i, 0))
    vec = pl.BlockSpec((1, D), lambda i: (0, 0))
    wspec = lambda a: pl.BlockSpec(a.shape, lambda i: (0, 0), pipeline_mode=pl.Buffered(1))
    rows_out = [(D, F32), (wg.shape[1], _MXU), (wg.shape[1], _MXU), (D, _MXU), (D, _MXU), (wg.shape[1], _MXU)]
    res = pl.pallas_call(
        kern,
        out_shape=[jax.ShapeDtypeStruct((s, w), dt) for w, dt in rows_out] + [jax.ShapeDtypeStruct((1, D), F32)] * 4,
        grid=(s // tile,),
        in_specs=[row(D), row(D), vec, vec, vec, vec, wspec(wg), wspec(wu), wspec(wd)],
        out_specs=[row(w) for w, _ in rows_out] + [vec] * 4,
        name=name, compiler_params=_params(1))(x, dxo, g, sh, sc, gt, wg, wu, wd)
    return res


def small_matmul(name, a, w, tn=256):
    m, k = a.shape
    n = w.shape[1]

    def kern(a_ref, w_ref, o_ref):
        o_ref[...] = jnp.dot(a_ref[...].astype(_MXU), w_ref[...].astype(_MXU), preferred_element_type=F32)

    return pl.pallas_call(kern, out_shape=jax.ShapeDtypeStruct((m, n), F32), grid=(n // tn,),
                          in_specs=[pl.BlockSpec((m, k), lambda j: (0, 0)), pl.BlockSpec((k, tn), lambda j: (0, j))],
                          out_specs=pl.BlockSpec((m, tn), lambda j: (0, j)), name=name,
                          compiler_params=_params(1))(a, w)


def small_matmul_tn(name, a, b, tn=256):
    m, k = a.shape
    n = b.shape[1]

    def kern(a_ref, b_ref, o_ref):
        o_ref[...] = lax.dot_general(a_ref[...].astype(_MXU), b_ref[...].astype(_MXU), (((0,), (0,)), ((), ())),
                                     preferred_element_type=F32)

    return pl.pallas_call(kern, out_shape=jax.ShapeDtypeStruct((k, n), F32), grid=(n // tn,),
                          in_specs=[pl.BlockSpec((m, k), lambda j: (0, 0)), pl.BlockSpec((m, tn), lambda j: (0, j))],
                          out_specs=pl.BlockSpec((k, tn), lambda j: (0, j)), name=name,
                          compiler_params=_params(1))(a, b)


def _s5_prep_math(lam_re, lam_im, log_dt, b_re_t, b_im_t, expand):
    dt = jnp.dot(jnp.exp(log_dt), expand, precision=HI, preferred_element_type=F32)
    mag = jnp.exp(lam_re * dt)
    ab_re = mag * jnp.cos(lam_im * dt)
    ab_im = mag * jnp.sin(lam_im * dt)
    den = lam_re * lam_re + lam_im * lam_im
    nr = ab_re - 1.0
    ni = ab_im
    f_re = (nr * lam_re + ni * lam_im) / den
    f_im = (ni * lam_re - nr * lam_im) / den
    bb_re = f_re * b_re_t - f_im * b_im_t
    bb_im = f_re * b_im_t + f_im * b_re_t
    return ab_re, ab_im, bb_re, bb_im


def _whole(kern, name, out_shape, *args):
    return pl.pallas_call(kern, out_shape=out_shape, name=name,
                          compiler_params=pltpu.CompilerParams(vmem_limit_bytes=VMEM_LIMIT))(*args)


def s5_prep_fwd(name, lam_re, lam_im, log_dt, b_re_t, b_im_t, expand):
    def kern(a, b, c, d, e, f, o0, o1, o2, o3):
        r = _s5_prep_math(a[...], b[...], c[...], d[...], e[...], f[...])
        for o, v in zip((o0, o1, o2, o3), r):
            o[...] = v

    gn = lam_re.shape[1]
    shp = [jax.ShapeDtypeStruct((1, gn), F32)] * 2 + [jax.ShapeDtypeStruct((P, gn), F32)] * 2
    return _whole(kern, name, shp, lam_re, lam_im, log_dt, b_re_t, b_im_t, expand)


def s5_prep_bwd(name, lam_re, lam_im, log_dt, b_re_t, b_im_t, expand, cots):
    def kern(a, b, c, d, e, f, c0, c1, c2, c3, o0, o1, o2, o3, o4):
        ex = f[...]
        _, vjp_fn = jax.vjp(lambda *p: _s5_prep_math(*p, ex), a[...], b[...], c[...], d[...], e[...])
        g = vjp_fn((c0[...], c1[...], c2[...], c3[...]))
        for o, v in zip((o0, o1, o2, o3, o4), g):
            o[...] = v

    shp = [jax.ShapeDtypeStruct(a.shape, F32) for a in (lam_re, lam_im, log_dt, b_re_t, b_im_t)]
    return _whole(kern, name, shp, lam_re, lam_im, log_dt, b_re_t, b_im_t, expand, *cots)


def _cpowers(ar, ai):
    pw = [(ar, ai)]
    for _ in range(7):
        pr, pi = pw[-1]
        pw.append((pr * ar - pi * ai, pr * ai + pi * ar))
    return pw


def _row_select(row, values):
    out = jnp.broadcast_to(values[7], (8, values[7].shape[1]))
    for r in range(6, -1, -1):
        out = jnp.where(row == r, values[r], out)
    return out


def _scan_tables(ar, ai, reverse):
    pw = _cpowers(ar, ai)
    row = lax.broadcasted_iota(jnp.int32, (8, ar.shape[1]), 0)
    steps = []
    for d in (1, 2, 4):
        keep = (row <= 7 - d) if reverse else (row >= d)
        steps.append((jnp.where(keep, pw[d - 1][0], 0.0), jnp.where(keep, pw[d - 1][1], 0.0)))
    order = list(range(7, -1, -1)) if reverse else list(range(8))
    carry = (_row_select(row, [pw[i][0] for i in order]), _row_select(row, [pw[i][1] for i in order]))
    return steps, carry


def _tile_scan_fwd(xr, xi, cr, ci, steps, carry_m):
    for d, (mr, mi) in zip((1, 2, 4), steps):
        sr = pltpu.roll(xr, d, 0)
        si = pltpu.roll(xi, d, 0)
        xr, xi = xr + mr * sr - mi * si, xi + mr * si + mi * sr
    pr, pi = carry_m
    return xr + pr * cr - pi * ci, xi + pr * ci + pi * cr


def _tile_scan_rev(xr, xi, cr, ci, steps, carry_m):
    for d, (mr, mi) in zip((1, 2, 4), steps):
        sr = pltpu.roll(xr, 8 - d, 0)
        si = pltpu.roll(xi, 8 - d, 0)
        xr, xi = xr + mr * sr + mi * si, xi + mr * si - mi * sr
    pr, pi = carry_m
    return xr + pr * cr + pi * ci, xi + pr * ci - pi * cr


def _fwd_scan_block(buf, row0, n_tiles8, ar, ai, c0r, c0i):
    steps, carry_m = _scan_tables(ar, ai, False)

    def body(j, carry):
        cr, ci = carry
        r0 = pl.multiple_of(row0 + j * 8, 8)
        xr = buf[pl.ds(r0, 8), 0:HALF]
        xi = buf[pl.ds(r0, 8), HALF:2 * HALF]
        xr, xi = _tile_scan_fwd(xr, xi, cr, ci, steps, carry_m)
        buf[pl.ds(r0, 8), 0:HALF] = xr
        buf[pl.ds(r0, 8), HALF:2 * HALF] = xi
        return xr[7:8], xi[7:8]

    return lax.fori_loop(0, n_tiles8, body, (c0r, c0i))


def s5_scan_fwd(name, h, wb, wc, a_tab, dskip, tile=TILE_SCAN):
    s = h.shape[0]
    n_t = s // tile

    def kern(h_ref, wb_ref, wc_ref, a_ref, d_ref, y_ref, s0_ref, carry_ref, buf):
        i = pl.program_id(0)

        @pl.when(i == 0)
        def _():
            carry_ref[...] = jnp.zeros(carry_ref.shape, F32)

        s0_ref[0] = carry_ref[...]
        for k in range(NBLK):
            cols = slice(GB * P * k, GB * P * (k + 1))
            u = h_ref[:, cols]
            buf[...] = jnp.dot(u.astype(_MXU), wb_ref[k], preferred_element_type=F32)
            ar = a_ref[k, :, 0:HALF]
            ai = a_ref[k, :, HALF:2 * HALF]
            cr, ci = _fwd_scan_block(buf, 0, tile // 8, ar, ai, carry_ref[k:k + 1, 0:HALF],
                                     carry_ref[k:k + 1, HALF:2 * HALF])
            carry_ref[k:k + 1, 0:HALF] = cr
            carry_ref[k:k + 1, HALF:2 * HALF] = ci
            y_ref[:, cols] = jnp.dot(buf[...].astype(_MXU), wc_ref[k], preferred_element_type=F32) + d_ref[:, cols] * u

    full = lambda a: pl.BlockSpec(a.shape, functools.partial(lambda i, nd_: (0,) * nd_, nd_=a.ndim))
    return pl.pallas_call(
        kern,
        out_shape=[jax.ShapeDtypeStruct((s, D), F32), jax.ShapeDtypeStruct((n_t, NBLK, 2 * HALF), F32)],
        grid=(n_t,),
        in_specs=[pl.BlockSpec((tile, D), lambda i: (i, 0)), full(wb), full(wc), full(a_tab), full(dskip)],
        out_specs=[pl.BlockSpec((tile, D), lambda i: (i, 0)), pl.BlockSpec((1, NBLK, 2 * HALF), lambda i: (i, 0, 0))],
        scratch_shapes=[pltpu.VMEM((NBLK, 2 * HALF), F32), pltpu.VMEM((tile, 2 * HALF), F32)],
        name=name, compiler_params=_params(1))(h, wb, wc, a_tab, dskip)


def s5_scan_bwd(name, h, dy, s0, wb, wc, a_tab, dskip, tile=TILE_SCAN):
    s = h.shape[0]
    n_t = s // tile
    n8 = tile // 8

    def kern(h_ref, dy_ref, s0_ref, wb_ref, wc_ref, a_ref, d_ref, dh_ref, dwb_ref, dwc_ref, da_ref, dd_ref,
             lam_ref, sbuf, gbuf):
        i = pl.program_id(0)

        @pl.when(i == 0)
        def _():
            lam_ref[...] = jnp.zeros(lam_ref.shape, F32)
            dwb_ref[...] = jnp.zeros(dwb_ref.shape, F32)
            dwc_ref[...] = jnp.zeros(dwc_ref.shape, F32)
            da_ref[...] = jnp.zeros(da_ref.shape, F32)
            dd_ref[...] = jnp.zeros(dd_ref.shape, F32)

        for k in range(NBLK):
            cols = slice(GB * P * k, GB * P * (k + 1))
            u = h_ref[:, cols]
            dyk = dy_ref[:, cols]
            ar = a_ref[k, :, 0:HALF]
            ai = a_ref[k, :, HALF:2 * HALF]
            sbuf[0:8, :] = jnp.broadcast_to(s0_ref[0, k:k + 1, :], (8, 2 * HALF))
            sbuf[8:tile + 8, :] = jnp.dot(u.astype(_MXU), wb_ref[k], preferred_element_type=F32)
            _fwd_scan_block(sbuf, 8, n8, ar, ai, s0_ref[0, k:k + 1, 0:HALF], s0_ref[0, k:k + 1, HALF:2 * HALF])
            dyb = dyk.astype(_MXU)
            gbuf[...] = lax.dot_general(dyb, wc_ref[k], (((1,), (1,)), ((), ())), preferred_element_type=F32)
            dwc_ref[k] += lax.dot_general(sbuf[8:tile + 8, :].astype(_MXU), dyb, (((0,), (0,)), ((), ())),
                                          preferred_element_type=F32)
            steps, carry_m = _scan_tables(ar, ai, True)
            row = lax.broadcasted_iota(jnp.int32, (8, HALF), 0)

            def body(jj, carry):
                cr, ci, dar, dai = carry
                j = n8 - 1 - jj
                r0 = pl.multiple_of(j * 8, 8)
                xr = gbuf[pl.ds(r0, 8), 0:HALF]
                xi = gbuf[pl.ds(r0, 8), HALF:2 * HALF]
                xr, xi = _tile_scan_rev(xr, xi, cr, ci, steps, carry_m)
                gbuf[pl.ds(r0, 8), 0:HALF] = xr
                gbuf[pl.ds(r0, 8), HALF:2 * HALF] = xi
                r1 = pl.multiple_of(j * 8 + 8, 8)
                spr = jnp.where(row == 0, sbuf[pl.ds(r0, 8), 0:HALF][7:8],
                                pltpu.roll(sbuf[pl.ds(r1, 8), 0:HALF], 1, 0))
                spi = jnp.where(row == 0, sbuf[pl.ds(r0, 8), HALF:2 * HALF][7:8],
                                pltpu.roll(sbuf[pl.ds(r1, 8), HALF:2 * HALF], 1, 0))
                dar = dar + xr * spr + xi * spi
                dai = dai + xi * spr - xr * spi
                return xr[0:1], xi[0:1], dar, dai

            z8 = jnp.zeros((8, HALF), F32)
            cr, ci, dar, dai = lax.fori_loop(
                0, n8, body, (lam_ref[k:k + 1, 0:HALF], lam_ref[k:k + 1, HALF:2 * HALF], z8, z8))
            lam_ref[k:k + 1, 0:HALF] = cr
            lam_ref[k:k + 1, HALF:2 * HALF] = ci
            da_ref[k:k + 1, 0:HALF] += jnp.sum(dar, axis=0, keepdims=True)
            da_ref[k:k + 1, HALF:2 * HALF] += jnp.sum(dai, axis=0, keepdims=True)
            lam = gbuf[...].astype(_MXU)
            dwb_ref[k] += lax.dot_general(u.astype(_MXU), lam, (((0,), (0,)), ((), ())), preferred_element_type=F32)
            du = lax.dot_general(lam, wb_ref[k], (((1,), (1,)), ((), ())), preferred_element_type=F32)
            dh_ref[:, cols] = du + d_ref[:, cols] * dyk
            dd_ref[:, cols] += jnp.sum(dyk * u, axis=0, keepdims=True)

    full = lambda a: pl.BlockSpec(a.shape, functools.partial(lambda i, nd_: (0,) * nd_, nd_=a.ndim))
    fullo = lambda shp: pl.BlockSpec(shp, functools.partial(lambda i, nd_: (0,) * nd_, nd_=len(shp)))
    rev = lambda i: (n_t - 1 - i, 0)
    return pl.pallas_call(
        kern,
        out_shape=[jax.ShapeDtypeStruct((s, D), F32), jax.ShapeDtypeStruct(wb.shape, F32),
                   jax.ShapeDtypeStruct(wc.shape, F32), jax.ShapeDtypeStruct((NBLK, 2 * HALF), F32),
                   jax.ShapeDtypeStruct((1, D), F32)],
        grid=(n_t,),
        in_specs=[pl.BlockSpec((tile, D), rev), pl.BlockSpec((tile, D), rev),
                  pl.BlockSpec((1, NBLK, 2 * HALF), lambda i: (n_t - 1 - i, 0, 0)),
                  full(wb), full(wc), full(a_tab), full(dskip)],
        out_specs=[pl.BlockSpec((tile, D), rev), fullo(wb.shape), fullo(wc.shape), fullo((NBLK, 2 * HALF)),
                   fullo((1, D))],
        scratch_shapes=[pltpu.VMEM((NBLK, 2 * HALF), F32), pltpu.VMEM((tile + 8, 2 * HALF), F32),
                        pltpu.VMEM((tile, 2 * HALF), F32)],
        name=name, compiler_params=_params(1))(h, dy, s0, wb, wc, a_tab, dskip)


def _chunk_mask(q0, k0, tq, tk):
    r = (q0 + lax.broadcasted_iota(jnp.int32, (tq, tk), 0)) // CHUNK
    c = (k0 + lax.broadcasted_iota(jnp.int32, (tq, tk), 1)) // CHUNK
    return r >= c


def _head_lanes(j):
    lane = _lane(2 * DV)
    return (lane >= DV * j) & (lane < DV * (j + 1))


def _raw_scores(q, kblk, masked, t):
    s = lax.dot_general(q, kblk, (((1,), (1,)), ((), ())), preferred_element_type=F32)
    return jnp.where(_chunk_mask(0, 0, t, t), s, -1e30) if masked else s


def attn_fwd(name, q, k, v, t=TILE_ATT):
    s = q.shape[0]
    n_q = s // t

    def kern(q_ref, k_ref, v_ref, o_ref, lse_ref):
        qi = pl.program_id(1)
        qs = [q_ref[:, HD * j:HD * (j + 1)] for j in range(2)]

        def scores(k0):
            return tuple(_raw_scores(qs[j], k_ref[pl.ds(k0, t), HD * j:HD * (j + 1)], False, t) for j in range(2))

        def absorb(k0, scs, carry):
            vblk = v_ref[pl.ds(k0, t), :]
            m_new = [jnp.maximum(carry[j][0], jnp.max(scs[j], axis=-1, keepdims=True)) for j in range(2)]
            ps = [jnp.exp2((scs[j] - m_new[j]) * EXP2_SCALE) for j in range(2)]
            alphas = [jnp.exp2((carry[j][0] - m_new[j]) * EXP2_SCALE) for j in range(2)]
            pvs = [jnp.dot(ps[j].astype(_MXU), vblk, preferred_element_type=F32) for j in range(2)]
            return tuple((m_new[j], alphas[j] * carry[j][1] + jnp.sum(ps[j], axis=-1, keepdims=True),
                          alphas[j] * carry[j][2] + pvs[j]) for j in range(2))

        def step(kb, state):
            scs, carry = state
            nxt = scores(pl.multiple_of((kb + 1) * t, t))
            return nxt, absorb(pl.multiple_of(kb * t, t), scs, carry)

        init = tuple((jnp.full((t, 1), -1e30, F32), jnp.zeros((t, 1), F32), jnp.zeros((t, 2 * DV), F32))
                     for _ in range(2))
        scs, carry = lax.fori_loop(0, qi, step, (scores(0), init))
        mask = _chunk_mask(0, 0, t, t)
        carry = absorb(pl.multiple_of(qi * t, t), tuple(jnp.where(mask, sc, -1e30) for sc in scs), carry)
        outs = []
        for j in range(2):
            m, l, acc = carry[j]
            outs.append(acc / l)
            lse_ref[0, j] = m * ATTN_SCALE + jnp.log(l)
        o_ref[...] = jnp.where(_head_lanes(0), outs[0], outs[1])

    return pl.pallas_call(
        kern,
        out_shape=[jax.ShapeDtypeStruct((s, H * DV), F32), jax.ShapeDtypeStruct((HP, 2, s, 1), F32)],
        grid=(HP, n_q),
        in_specs=[pl.BlockSpec((t, 2 * HD), lambda hp, i: (i, hp)), pl.BlockSpec((s, 2 * HD), lambda hp, i: (0, hp)),
                  pl.BlockSpec((s, 2 * DV), lambda hp, i: (0, hp))],
        out_specs=[pl.BlockSpec((t, 2 * DV), lambda hp, i: (i, hp)),
                   pl.BlockSpec((1, 2, t, 1), lambda hp, i: (hp, 0, i, 0))],
        name=name, compiler_params=_params(2))(q, k, v)


def attn_bwd(name, q, k, v, o, do, lse, t=TILE_ATT):
    s = q.shape[0]
    n_q = s // t

    def kern(q_ref, k_ref, v_ref, o_ref, do_ref, lse_ref, dq_ref, dk_ref, dv_ref):
        qi = pl.program_id(1)

        @pl.when(qi == 0)
        def _():
            dk_ref[...] = jnp.zeros(dk_ref.shape, F32)
            dv_ref[...] = jnp.zeros(dv_ref.shape, F32)

        qs, doms, deltas, lse2 = [], [], [], []
        for j in range(2):
            qs.append(q_ref[:, HD * j:HD * (j + 1)])
            dom = jnp.where(_head_lanes(j), do_ref[...], 0.0)
            deltas.append(jnp.sum(dom * o_ref[...], axis=-1, keepdims=True))
            doms.append(dom.astype(_MXU))
            lse2.append(lse_ref[0, j] * LOG2E)

        def block(k0, dqs, masked):
            vblk = v_ref[pl.ds(k0, t), :]
            kblks = [k_ref[pl.ds(k0, t), HD * j:HD * (j + 1)] for j in range(2)]
            scs = [_raw_scores(qs[j], kblks[j], masked, t) for j in range(2)]
            dps = [lax.dot_general(doms[j], vblk, (((1,), (1,)), ((), ())), preferred_element_type=F32)
                   for j in range(2)]
            ps = [jnp.exp2(scs[j] * EXP2_SCALE - lse2[j]) for j in range(2)]
            dss = [(ps[j] * (dps[j] - deltas[j])).astype(_MXU) for j in range(2)]
            pbs = [ps[j].astype(_MXU) for j in range(2)]
            new = tuple(dqs[j] + jnp.dot(dss[j], kblks[j], preferred_element_type=F32) for j in range(2))
            for j in range(2):
                dk_ref[pl.ds(k0, t), HD * j:HD * (j + 1)] += lax.dot_general(
                    dss[j], qs[j], (((0,), (0,)), ((), ())), preferred_element_type=F32)
            dvs = [lax.dot_general(pbs[j], doms[j], (((0,), (0,)), ((), ())), preferred_element_type=F32)
                   for j in range(2)]
            dv_ref[pl.ds(k0, t), :] += dvs[0] + dvs[1]
            return new

        init = (jnp.zeros((t, HD), F32), jnp.zeros((t, HD), F32))
        dqs = lax.fori_loop(0, qi, lambda kb, c: block(pl.multiple_of(kb * t, t), c, False), init)
        dqs = block(pl.multiple_of(qi * t, t), dqs, True)
        for j in range(2):
            dq_ref[:, HD * j:HD * (j + 1)] = dqs[j] * ATTN_SCALE

        @pl.when(qi == n_q - 1)
        def _():
            dk_ref[...] = dk_ref[...] * ATTN_SCALE

    return pl.pallas_call(
        kern,
        out_shape=[jax.ShapeDtypeStruct((s, H * HD), F32), jax.ShapeDtypeStruct((s, H * HD), F32),
                   jax.ShapeDtypeStruct((s, H * DV), F32)],
        grid=(HP, n_q),
        in_specs=[pl.BlockSpec((t, 2 * HD), lambda hp, i: (i, hp)), pl.BlockSpec((s, 2 * HD), lambda hp, i: (0, hp)),
                  pl.BlockSpec((s, 2 * DV), lambda hp, i: (0, hp)), pl.BlockSpec((t, 2 * DV), lambda hp, i: (i, hp)),
                  pl.BlockSpec((t, 2 * DV), lambda hp, i: (i, hp)),
                  pl.BlockSpec((1, 2, t, 1), lambda hp, i: (hp, 0, i, 0))],
        out_specs=[pl.BlockSpec((t, 2 * HD), lambda hp, i: (i, hp)), pl.BlockSpec((s, 2 * HD), lambda hp, i: (0, hp)),
                   pl.BlockSpec((s, 2 * DV), lambda hp, i: (0, hp))],
        name=name, compiler_params=_params(2))(q, k, v, o, do, lse)


def rope_tables(name, pos_col, inv128):
    s = pos_col.shape[0]

    def kern(p_ref, inv_ref, c_ref, s_ref):
        ang = p_ref[...].astype(F32) * inv_ref[...]
        lane = _lane()
        m_r = (lane >= DN) & (lane < DN + DR)
        c_ref[...] = jnp.where(lane < DN, 1.0, jnp.where(m_r, jnp.cos(ang), 0.0))
        s_ref[...] = jnp.where(m_r, jnp.sin(ang), 0.0)

    return _whole(kern, name, [jax.ShapeDtypeStruct((s, HD), F32)] * 2, pos_col, inv128)


def loss_kernel(name, y, tgt, tile=TILE_ROW):
    def body(row_v, _):
        err = row_v[0] - row_v[1]
        part = 0.5 * jnp.sum(jnp.mean(err * err, axis=-1, keepdims=True), axis=0, keepdims=True)
        return [err * (1.0 / D)], [jnp.broadcast_to(part, (1, 128))]

    return _row_call(name, body, [y, tgt], [], [(D, F32)], [((1, 128), F32)], tile)


def _row_tile(r, c):
    cap = max(8, (1 << 18) // max(c, 1))
    for t in (2048, 1024, 512, 256, 128, 64, 32, 16, 8):
        if t <= cap and r % t == 0:
            return t
    return r


def sum_parts(name, parts):
    n, r, c = parts.shape
    t = _row_tile(r, c)

    def kern(p_ref, o_ref):
        acc = p_ref[0].astype(F32)
        for i in range(1, n):
            acc = acc + p_ref[i].astype(F32)
        o_ref[...] = acc

    return pl.pallas_call(kern, out_shape=jax.ShapeDtypeStruct((r, c), F32), grid=(r // t,),
                          in_specs=[pl.BlockSpec((n, t, c), lambda i: (0, i, 0))],
                          out_specs=pl.BlockSpec((t, c), lambda i: (i, 0)), name=name, compiler_params=_params(1))(parts)


def adamw(name, parts, w, m, v, base=0, stride=0):
    n, _, cp = parts.shape
    nl, r, c = w.shape
    t = _row_tile(math.gcd(math.gcd(r, base), stride), max(c, cp))
    c1 = 1.0 / (1.0 - ADAM_B1 ** ADAM_STEP)
    c2 = 1.0 / (1.0 - ADAM_B2 ** ADAM_STEP)

    def kern(p_ref, w_ref, m_ref, v_ref, g_ref, d_ref, nm_ref, nv_ref):
        g = p_ref[0].astype(F32)
        for i in range(1, n):
            g = g + p_ref[i].astype(F32)
        g = g[:, :c]
        nm = ADAM_B1 * m_ref[...] + (1.0 - ADAM_B1) * g
        nv = ADAM_B2 * v_ref[...] + (1.0 - ADAM_B2) * (g * g)
        g_ref[...] = g
        nm_ref[...] = nm
        nv_ref[...] = nv
        d_ref[...] = -ADAM_LR * ((nm * c1) / (jnp.sqrt(nv * c2) + ADAM_EPS) + ADAM_WD * w_ref[...])

    spec = pl.BlockSpec((None, t, c), lambda l, i: (l, i, 0))
    pspec = pl.BlockSpec((n, t, cp), lambda l, i: (0, (base + l * stride) // t + i, 0))
    return pl.pallas_call(kern, out_shape=[jax.ShapeDtypeStruct((nl, r, c), F32)] * 4, grid=(nl, r // t),
                          in_specs=[pspec, spec, spec, spec], out_specs=[spec] * 4, name=name,
                          compiler_params=_params(2))(parts, w, m, v)


def adamw_layer(name, parts, w, m, v, layer, prev, base=0):
    n, _, cp = parts.shape
    nl, r, c = w.shape
    t = _row_tile(math.gcd(r, base), max(c, cp))
    c1 = 1.0 / (1.0 - ADAM_B1 ** ADAM_STEP)
    c2 = 1.0 / (1.0 - ADAM_B2 ** ADAM_STEP)
    chained = nl > 1

    def kern(p_ref, w_ref, m_ref, v_ref, *rest):
        g_ref, d_ref, nm_ref, nv_ref = rest[-4:]
        g = p_ref[0].astype(F32)
        for i in range(1, n):
            g = g + p_ref[i].astype(F32)
        g = g[:, :c]
        nm = ADAM_B1 * m_ref[...] + (1.0 - ADAM_B1) * g
        nv = ADAM_B2 * v_ref[...] + (1.0 - ADAM_B2) * (g * g)
        g_ref[...] = g
        nm_ref[...] = nm
        nv_ref[...] = nv
        d_ref[...] = -ADAM_LR * ((nm * c1) / (jnp.sqrt(nv * c2) + ADAM_EPS) + ADAM_WD * w_ref[...])

    spec = pl.BlockSpec((None, t, c), lambda i: (layer, i, 0))
    pspec = pl.BlockSpec((n, t, cp), lambda i: (0, base // t + i, 0))
    in_specs = [pspec, spec, spec, spec]
    args = [parts, w, m, v]
    aliases = {}
    if chained:
        if prev is None:
            prev = [lax.empty((nl, r, c), F32) for _ in range(4)]
        in_specs += [pl.BlockSpec(memory_space=pl.ANY)] * 4
        args += list(prev)
        aliases = {4 + i: i for i in range(4)}
    return pl.pallas_call(kern, out_shape=[jax.ShapeDtypeStruct((nl, r, c), F32)] * 4, grid=(r // t,),
                          in_specs=in_specs, out_specs=[spec] * 4, input_output_aliases=aliases, name=name,
                          compiler_params=_params(1))(*args)


def _me():
    return lax.axis_index("x"), lax.axis_index("y"), lax.axis_index("c")


def _flip(x, y, c, mask):
    return (jnp.where((mask >> 2) & 1, 1 - x, x), jnp.where((mask >> 1) & 1, 1 - y, y), jnp.where(mask & 1, 1 - c, c))


def _index(x, y, c):
    return 4 * x + 2 * y + c


def _exchange(name, arr, gather):
    out_shape = (N_DEV,) + arr.shape if gather else arr.shape

    def kern(in_ref, out_ref, send_sems, recv_sems, local_sem):
        x, y, c = _me()
        me = _index(x, y, c)
        mine = pltpu.make_async_copy(in_ref if gather else in_ref.at[me], out_ref.at[me], local_sem)
        mine.start()
        copies = []
        for mask in range(1, N_DEV):
            px, py, pc = _flip(x, y, c, mask)
            peer = _index(px, py, pc)
            cp = pltpu.make_async_remote_copy(
                src_ref=in_ref if gather else in_ref.at[peer], dst_ref=out_ref.at[me],
                send_sem=send_sems.at[mask - 1], recv_sem=recv_sems.at[mask - 1],
                device_id=(px, py, pc), device_id_type=MESH)
            cp.start()
            copies.append((cp, peer))
        for mask, (cp, peer) in enumerate(copies, start=1):
            pltpu.make_async_remote_copy(
                src_ref=in_ref if gather else in_ref.at[peer], dst_ref=out_ref.at[peer],
                send_sem=send_sems.at[mask - 1], recv_sem=recv_sems.at[mask - 1],
                device_id=_flip(x, y, c, mask), device_id_type=MESH).wait_recv()
        for cp, _ in copies:
            cp.wait_send()
        mine.wait()

    any_spec = pl.BlockSpec(memory_space=pl.ANY)
    return pl.pallas_call(
        kern, out_shape=jax.ShapeDtypeStruct(out_shape, arr.dtype), in_specs=[any_spec], out_specs=any_spec,
        scratch_shapes=[pltpu.SemaphoreType.DMA((N_DEV - 1,)), pltpu.SemaphoreType.DMA((N_DEV - 1,)),
                        pltpu.SemaphoreType.DMA],
        name=name, compiler_params=pltpu.CompilerParams(has_side_effects=True))(arr)


def all_gather(name, arr):
    return _exchange(name, arr, True)


def all_to_all(name, arr):
    return _exchange(name, arr, False)


_HBM = pl.BlockSpec(memory_space=pltpu.HBM)
_SEM = pl.BlockSpec(memory_space=pltpu.SEMAPHORE)
_EFFECT = pltpu.SideEffectType.DATAFLOW_SIDE_EFFECTING


def _split_copies(srcs, lands, send_sems, recv_sems, gather):
    x, y, c = _me()
    me = _index(x, y, c)
    out = []
    for a, (src, land) in enumerate(zip(srcs, lands)):
        for mask in range(1, N_DEV):
            px, py, pc = _flip(x, y, c, mask)
            peer = _index(px, py, pc)
            sem = (N_DEV - 1) * a + mask - 1
            mk = lambda dst_slot: pltpu.make_async_remote_copy(
                src_ref=src if gather else src.at[peer], dst_ref=land.at[dst_slot],
                send_sem=send_sems.at[sem], recv_sem=recv_sems.at[sem], device_id=(px, py, pc), device_id_type=MESH)
            out.append((mk(me), mk(peer)))
    return out


def exchange_start(name, arrs, gather, after):
    k = len(arrs)
    land_shapes = [((N_DEV,) + a.shape if gather else a.shape) for a in arrs]

    def body(*refs):
        srcs, lands = refs[:k], refs[k:2 * k]
        send_sems, recv_sems = refs[2 * k + 1], refs[2 * k + 2]
        token = refs[-1]
        for mine, _ in _split_copies(srcs, lands, send_sems, recv_sems, gather):
            mine.start()
        token[...] = jnp.zeros(token.shape, token.dtype)

    n_sem = (N_DEV - 1) * k
    res = pl.pallas_call(
        body, name=name,
        out_shape=(pltpu.SemaphoreType.DMA((n_sem,)), pltpu.SemaphoreType.DMA((n_sem,)),
                   *[pltpu.HBM(a.shape, a.dtype) for a in arrs],
                   *[pltpu.HBM(shp, a.dtype) for shp, a in zip(land_shapes, arrs)],
                   jax.ShapeDtypeStruct((8, 128), F32)),
        in_specs=[_HBM] * (2 * k) + [pl.BlockSpec(memory_space=pl.ANY)],
        out_specs=(_SEM, _SEM, *[_HBM] * (2 * k), pl.BlockSpec(memory_space=pltpu.VMEM)),
        input_output_aliases={i: 2 + i for i in range(2 * k)},
        compiler_params=pltpu.CompilerParams(has_side_effects=_EFFECT),
    )(*[pltpu.with_memory_space_constraint(a, pltpu.HBM) for a in arrs],
      *[pltpu.with_memory_space_constraint(lax.empty(shp, a.dtype), pltpu.HBM) for shp, a in zip(land_shapes, arrs)],
      after)
    return res[0], res[1], list(res[2:2 + k]), list(res[2 + k:2 + 2 * k]), res[-1]


def exchange_wait(name, started, after, gather):
    send_sems, recv_sems, thrus, lands, _ = started
    k = len(thrus)

    def body(*refs):
        srcs, lnds = refs[:k], refs[k:2 * k]
        s_sems, r_sems = refs[2 * k], refs[2 * k + 1]
        for mine, theirs in _split_copies(srcs, lnds, s_sems, r_sems, gather):
            mine.wait_send()
            theirs.wait_recv()

    res = pl.pallas_call(
        body, name=name,
        out_shape=tuple(pltpu.HBM(a.shape, a.dtype) for a in thrus + lands),
        in_specs=[_HBM] * (2 * k) + [_SEM, _SEM, pl.BlockSpec(memory_space=pl.ANY)], out_specs=tuple([_HBM] * (2 * k)),
        input_output_aliases={i: i for i in range(2 * k)},
        compiler_params=pltpu.CompilerParams(has_side_effects=_EFFECT),
    )(*thrus, *lands, send_sems, recv_sems, after)
    return list(res[k:])


def _pad_heads(w, real, padded):
    k = w.shape[0]
    w3 = w.reshape(k, H, real)
    return jnp.pad(w3, ((0, 0), (0, 0), (0, padded - real))).reshape(k, H * padded)


def _unpad_heads(w, real, padded):
    k = w.shape[0]
    return w.reshape(k, H, padded)[:, :, :real].reshape(k, H * real)


def _s5_place(ab_re, ab_im, bb_re_t, bb_im_t, c_re, c_im):
    eye = jnp.eye(GB, dtype=F32)

    def wb_part(bt):
        x4 = bt.reshape(P, NBLK, GB, N).transpose(1, 2, 0, 3)
        return jnp.einsum('kgpn,gh->kgphn', x4, eye).reshape(NBLK, GB * P, HALF)

    def wc_part(cc):
        x4 = cc.reshape(NBLK, GB, P, N)
        return jnp.einsum('kgpn,gh->kgnhp', x4, eye).reshape(NBLK, HALF, GB * P)

    wb = jnp.concatenate([wb_part(bb_re_t), wb_part(bb_im_t)], axis=-1)
    wc = jnp.concatenate([wc_part(c_re), -wc_part(c_im)], axis=1)
    a_tab = jnp.concatenate([ab_re.reshape(NBLK, 1, HALF), ab_im.reshape(NBLK, 1, HALF)], axis=-1)
    return wb.astype(_MXU), wc.astype(_MXU), a_tab


def _s5_unplace(dwb, dwc, da):
    eye = jnp.eye(GB, dtype=F32)

    def wb_part(dpart):
        x5 = dpart.reshape(NBLK, GB, P, GB, N)
        return jnp.einsum('kgphn,gh->kgpn', x5, eye).transpose(2, 0, 1, 3).reshape(P, G * N)

    def wc_part(dpart):
        x5 = dpart.reshape(NBLK, GB, N, GB, P)
        return jnp.einsum('kgnhp,gh->kgpn', x5, eye).reshape(G, P, N)

    dbb_re_t, dbb_im_t = wb_part(dwb[..., :HALF]), wb_part(dwb[..., HALF:])
    dc_re, dc_im = wc_part(dwc[:, :HALF]), -wc_part(dwc[:, HALF:])
    dab_re, dab_im = da[:, :HALF].reshape(1, G * N), da[:, HALF:].reshape(1, G * N)
    return dab_re, dab_im, dbb_re_t, dbb_im_t, dc_re, dc_im


def _row(v):
    return v.reshape(1, -1)


def kernel(x, c, positions, ada_w, ada_b, norm1_g, norm2_g, ffn_w_gate, ffn_w_up, ffn_w_down, s5_lam_re, s5_lam_im, s5_log_dt, s5_b_re, s5_b_im, s5_c_re, s5_c_im, s5_d, s5_w_glu, s5_b_glu, kv_ada_w, kv_ada_b, kv_norm_g, w_kv_a, kv_a_norm_g, w_kv_b, k_nope_norm_g, k_rope_norm_g, mla_w_dq, mla_q_norm_g, mla_w_uq, mla_q_nope_norm_g, mla_q_rope_norm_g, mla_w_o, loss_target, m_ada_w, m_ada_b, m_norm1_g, m_norm2_g, m_ffn_w_gate, m_ffn_w_up, m_ffn_w_down, m_s5_lam_re, m_s5_lam_im, m_s5_log_dt, m_s5_b_re, m_s5_b_im, m_s5_c_re, m_s5_c_im, m_s5_d, m_s5_w_glu, m_s5_b_glu, m_kv_ada_w, m_kv_ada_b, m_kv_norm_g, m_w_kv_a, m_kv_a_norm_g, m_w_kv_b, m_k_nope_norm_g, m_k_rope_norm_g, m_mla_w_dq, m_mla_q_norm_g, m_mla_w_uq, m_mla_q_nope_norm_g, m_mla_q_rope_norm_g, m_mla_w_o, v_ada_w, v_ada_b, v_norm1_g, v_norm2_g, v_ffn_w_gate, v_ffn_w_up, v_ffn_w_down, v_s5_lam_re, v_s5_lam_im, v_s5_log_dt, v_s5_b_re, v_s5_b_im, v_s5_c_re, v_s5_c_im, v_s5_d, v_s5_w_glu, v_s5_b_glu, v_kv_ada_w, v_kv_ada_b, v_kv_norm_g, v_w_kv_a, v_kv_a_norm_g, v_w_kv_b, v_k_nope_norm_g, v_k_rope_norm_g, v_mla_w_dq, v_mla_q_norm_g, v_mla_w_uq, v_mla_q_nope_norm_g, v_mla_q_rope_norm_g, v_mla_w_o):
    W = dict(ada_w=ada_w, ada_b=ada_b, norm1_g=norm1_g, norm2_g=norm2_g, ffn_w_gate=ffn_w_gate, ffn_w_up=ffn_w_up, ffn_w_down=ffn_w_down, s5_lam_re=s5_lam_re, s5_lam_im=s5_lam_im, s5_log_dt=s5_log_dt, s5_b_re=s5_b_re, s5_b_im=s5_b_im, s5_c_re=s5_c_re, s5_c_im=s5_c_im, s5_d=s5_d, s5_w_glu=s5_w_glu, s5_b_glu=s5_b_glu, kv_ada_w=kv_ada_w, kv_ada_b=kv_ada_b, kv_norm_g=kv_norm_g, w_kv_a=w_kv_a, kv_a_norm_g=kv_a_norm_g, w_kv_b=w_kv_b, k_nope_norm_g=k_nope_norm_g, k_rope_norm_g=k_rope_norm_g, mla_w_dq=mla_w_dq, mla_q_norm_g=mla_q_norm_g, mla_w_uq=mla_w_uq, mla_q_nope_norm_g=mla_q_nope_norm_g, mla_q_rope_norm_g=mla_q_rope_norm_g, mla_w_o=mla_w_o)
    M = dict(ada_w=m_ada_w, ada_b=m_ada_b, norm1_g=m_norm1_g, norm2_g=m_norm2_g, ffn_w_gate=m_ffn_w_gate, ffn_w_up=m_ffn_w_up, ffn_w_down=m_ffn_w_down, s5_lam_re=m_s5_lam_re, s5_lam_im=m_s5_lam_im, s5_log_dt=m_s5_log_dt, s5_b_re=m_s5_b_re, s5_b_im=m_s5_b_im, s5_c_re=m_s5_c_re, s5_c_im=m_s5_c_im, s5_d=m_s5_d, s5_w_glu=m_s5_w_glu, s5_b_glu=m_s5_b_glu, kv_ada_w=m_kv_ada_w, kv_ada_b=m_kv_ada_b, kv_norm_g=m_kv_norm_g, w_kv_a=m_w_kv_a, kv_a_norm_g=m_kv_a_norm_g, w_kv_b=m_w_kv_b, k_nope_norm_g=m_k_nope_norm_g, k_rope_norm_g=m_k_rope_norm_g, mla_w_dq=m_mla_w_dq, mla_q_norm_g=m_mla_q_norm_g, mla_w_uq=m_mla_w_uq, mla_q_nope_norm_g=m_mla_q_nope_norm_g, mla_q_rope_norm_g=m_mla_q_rope_norm_g, mla_w_o=m_mla_w_o)
    V = dict(ada_w=v_ada_w, ada_b=v_ada_b, norm1_g=v_norm1_g, norm2_g=v_norm2_g, ffn_w_gate=v_ffn_w_gate, ffn_w_up=v_ffn_w_up, ffn_w_down=v_ffn_w_down, s5_lam_re=v_s5_lam_re, s5_lam_im=v_s5_lam_im, s5_log_dt=v_s5_log_dt, s5_b_re=v_s5_b_re, s5_b_im=v_s5_b_im, s5_c_re=v_s5_c_re, s5_c_im=v_s5_c_im, s5_d=v_s5_d, s5_w_glu=v_s5_w_glu, s5_b_glu=v_s5_b_glu, kv_ada_w=v_kv_ada_w, kv_ada_b=v_kv_ada_b, kv_norm_g=v_kv_norm_g, w_kv_a=v_w_kv_a, kv_a_norm_g=v_kv_a_norm_g, w_kv_b=v_w_kv_b, k_nope_norm_g=v_k_nope_norm_g, k_rope_norm_g=v_k_rope_norm_g, mla_w_dq=v_mla_w_dq, mla_q_norm_g=v_mla_q_norm_g, mla_w_uq=v_mla_w_uq, mla_q_nope_norm_g=v_mla_q_nope_norm_g, mla_q_rope_norm_g=v_mla_q_rope_norm_g, mla_w_o=v_mla_w_o)
    return _step(x[0], c, positions, loss_target[0], W, M, V)


WEIGHT_NAMES = ['ada_w', 'ada_b', 'norm1_g', 'norm2_g', 'ffn_w_gate', 'ffn_w_up', 'ffn_w_down', 's5_lam_re', 's5_lam_im', 's5_log_dt', 's5_b_re', 's5_b_im', 's5_c_re', 's5_c_im', 's5_d', 's5_w_glu', 's5_b_glu', 'kv_ada_w', 'kv_ada_b', 'kv_norm_g', 'w_kv_a', 'kv_a_norm_g', 'w_kv_b', 'k_nope_norm_g', 'k_rope_norm_g', 'mla_w_dq', 'mla_q_norm_g', 'mla_w_uq', 'mla_q_nope_norm_g', 'mla_q_rope_norm_g', 'mla_w_o']
REPLICATED = ['ada_b', 'norm1_g', 'norm2_g', 's5_lam_re', 's5_lam_im', 's5_log_dt', 's5_b_re', 's5_b_im', 's5_c_re', 's5_c_im', 'kv_ada_b', 'kv_norm_g', 'kv_a_norm_g', 'k_nope_norm_g', 'k_rope_norm_g', 'mla_q_norm_g', 'mla_q_nope_norm_g', 'mla_q_rope_norm_g']
SHARDED_VEC = ['s5_d', 's5_b_glu']


def _step(x, c, positions, target, W, M, V):
    s = x.shape[0]
    me = _index(*_me())
    mxu = lambda a: a.astype(_MXU)

    pad_c = lambda a: jnp.pad(a, ((0, 0), (0, FFB - FF // N_DEV)))
    pad_r = lambda a: jnp.pad(a, ((0, FFB - FF // N_DEV), (0, 0)))
    cols = lambda g: g.transpose(1, 0, 2).reshape(g.shape[1], N_DEV * g.shape[2])
    rows = lambda g: g.reshape(N_DEV * g.shape[1], g.shape[2])

    def local_pack(l):
        second = W['s5_w_glu'][l] if l < N_A else W['mla_w_o'][l - N_A]
        arrs = [jnp.concatenate([mxu(pad_c(W['ffn_w_gate'][l])), mxu(pad_c(W['ffn_w_up'][l]))], axis=0),
                jnp.concatenate([mxu(pad_r(W['ffn_w_down'][l])), mxu(second)], axis=0)]
        if l == N_A:
            arrs += [jnp.concatenate([mxu(W['w_kv_b']), mxu(W['mla_w_dq'][0])], axis=0), mxu(W['w_kv_a'])]
        if l > N_A:
            arrs += [mxu(W['mla_w_dq'][l - N_A])]
        if l >= N_A:
            arrs += [mxu(W['mla_w_uq'][l - N_A])]
        return arrs


    def layer_weights(l, after):
        lands = exchange_wait(f"gather_wait_{l}", gathers[l], after, True)
        full = [lax.dynamic_update_slice(ld, src[None], (me,) + (0,) * src.ndim) for ld, src in zip(lands, gathers[l][2])]
        w = {'wg': cols(full[0][:, :D]), 'wu': cols(full[0][:, D:]), 'wd': rows(full[1][:, :FFB]),
             'second': rows(full[1][:, FFB:])}
        if l >= N_A:
            if l == N_A:
                wkvb3 = cols(full[2][:, :KVL]).reshape(KVL, H, DN + DV)
                wkva = rows(full[3])
                w['wa_pad'] = jnp.concatenate([wkva[:, :KVL], jnp.zeros((D, DN), _MXU), wkva[:, KVL:],
                                               jnp.zeros((D, HD - DN - DR), _MXU)], axis=1)
                w['wkn_pad'] = jnp.pad(wkvb3[:, :, :DN], ((0, 0), (0, 0), (0, HD - DN))).reshape(KVL, H * HD)
                w['wv'] = wkvb3[:, :, DN:].reshape(KVL, H * DV)
                w['wdq'] = rows(full[2][:, KVL:])
            else:
                w['wdq'] = rows(full[2])
            w['wuq_pad'] = _pad_heads(cols(full[-1]), DN + DR, HD)
        return w

    vec = jnp.concatenate([c.reshape(-1), W['s5_d'].reshape(-1), W['s5_b_glu'].reshape(-1)]).reshape(1, -1)
    vec = jnp.pad(vec, ((0, 7), (0, 0)))
    gv = all_gather("gather_vectors", vec)[:, 0, :]
    c_all = gv[:, :D]
    d_full = jnp.concatenate([gv[d, D:D + 2 * 128].reshape(N_A, 128) for d in range(N_DEV)], axis=1)
    bglu_full = jnp.concatenate([gv[d, D + 256:D + 512].reshape(N_A, 128) for d in range(N_DEV)], axis=1)

    ca_all = jax.nn.silu(c_all)
    w_mod = jnp.concatenate([W['ada_w'][l] for l in range(DEPTH)] + [W['kv_ada_w']], axis=1)
    n_mod = w_mod.shape[1]
    mod_cols = small_matmul("mod_matmul", ca_all, w_mod)
    gm = all_gather("gather_mod", mod_cols)
    gathers = [exchange_start(f"gather_start_{l}", local_pack(l), True, gm) for l in range(DEPTH)]
    tokens = sum(g[4][0, 0] for g in gathers)
    mine = lax.dynamic_index_in_dim(gm, me, axis=1, keepdims=False) + tokens
    per_l = D * 6 // N_DEV
    mods = []
    for l in range(DEPTH):
        full = jnp.concatenate([mine[d, per_l * l:per_l * (l + 1)] for d in range(N_DEV)]) + W['ada_b'][l]
        mods.append([_row(full[D * i:D * (i + 1)]) for i in range(6)])
    kfull = jnp.concatenate([mine[d, per_l * DEPTH:] for d in range(N_DEV)]) + W['kv_ada_b']
    k_shift, k_scale = _row(kfull[:D]), _row(kfull[D:])

    inv = 1.0 / (ROPE_THETA ** (np.arange(0, DR, 2, dtype=np.float32) / DR))
    inv128 = np.zeros((1, HD), np.float32)
    inv128[0, DN:DN + DR // 2] = inv
    inv128[0, DN + DR // 2:DN + DR] = inv
    cosf, sinf = rope_tables("rope_tables", positions.reshape(s, 1), jnp.asarray(inv128))
    zpad = lambda n: jnp.zeros((n,), F32)
    gkn128 = _row(jnp.concatenate([W['k_nope_norm_g'], zpad(HD - DN)]))
    gkr128 = _row(jnp.concatenate([zpad(DN), W['k_rope_norm_g'], zpad(HD - DN - DR)]))
    gq128 = [_row(jnp.concatenate([W['mla_q_nope_norm_g'][j], W['mla_q_rope_norm_g'][j], zpad(HD - DN - DR)]))
             for j in range(2)]

    expand = jnp.asarray(np.kron(np.eye(G, dtype=np.float32), np.ones((1, N), np.float32)))
    s5_raw, s5_mats = [], []
    for l in range(N_A):
        raw = (_row(W['s5_lam_re'][l]), _row(W['s5_lam_im'][l]), _row(W['s5_log_dt'][l]),
               W['s5_b_re'][l].transpose(2, 0, 1).reshape(P, G * N), W['s5_b_im'][l].transpose(2, 0, 1).reshape(P, G * N))
        ab_re, ab_im, bb_re_t, bb_im_t = s5_prep_fwd(f"s5_prep_fwd", *raw, expand)
        s5_raw.append(raw)
        s5_mats.append(_s5_place(ab_re, ab_im, bb_re_t, bb_im_t, W['s5_c_re'][l], W['s5_c_im'][l]))

    g1 = [_row(W['norm1_g'][l]) for l in range(DEPTH)]
    g2 = [_row(W['norm2_g'][l]) for l in range(DEPTH)]
    saved = []
    xs = x
    kv = None
    lw = [None] * DEPTH
    for l in range(DEPTH):
        sh1, sc1, gt1, sh2, sc2, gt2 = mods[l]
        rec = {'x_in': xs}
        if l >= N_A:
            lw[l] = layer_weights(l, xs)
        if l == N_A:
            kv_smalls = [_row(W['kv_norm_g']), k_shift, k_scale, _row(W['kv_a_norm_g']), gkn128, gkr128]
            kv_w = [lw[l]['wa_pad'], lw[l]['wkn_pad'], lw[l]['wv']]
            k_mat, v_mat = seg_forward("kv_fwd", seg_kv, [xs], kv_smalls, [cosf, sinf], kv_w,
                                       [(H * HD, _MXU), (H * DV, _MXU)], tap_widths=(KVL + HD, H * HD, H * DV))
            kv = {'x_in': xs, 'smalls': kv_smalls, 'k': k_mat, 'v': v_mat, 'w': kv_w}
        if l < N_A:
            (h,) = seg_forward("pre_fwd", seg_pre, [xs], [g1[l], sh1, sc1], [], [], [(D, F32)])
            wb, wc, a_tab = s5_mats[l]
            y, s0 = s5_scan_fwd("s5_scan_fwd", h, wb, wc, a_tab, _row(d_full[l]))
            lw[l] = layer_weights(l, y)
            (x_mid,) = seg_forward("glu_fwd", seg_glu, [xs, y], [gt1, _row(bglu_full[l])], [], [lw[l]['second']],
                                   [(D, F32)], tap_widths=(D,))
            rec.update(h=h, y=y, s0=s0)
        else:
            j = l - N_A
            q_smalls = [g1[l], sh1, sc1, _row(W['mla_q_norm_g'][j]), gq128[j]]
            (q_mat,) = seg_forward("q_fwd", seg_q, [xs], q_smalls, [cosf, sinf], [lw[l]['wdq'], lw[l]['wuq_pad']],
                                   [(H * HD, _MXU)], tap_widths=(QL, H * HD))
            o_mat, lse = attn_fwd("attn_fwd", q_mat, kv['k'], kv['v'])
            (x_mid,) = seg_forward("o_fwd", seg_o, [xs, o_mat], [gt1], [], [lw[l]['second']], [(D, F32)],
                                   tap_widths=(D,))
            rec.update(q=q_mat, o=o_mat, lse=lse, q_smalls=q_smalls)
        rec['x_mid'] = x_mid
        (xs,) = seg_forward("ffn_fwd", seg_ffn, [x_mid], [g2[l], sh2, sc2, gt2], [],
                            [lw[l]['wg'], lw[l]['wu'], lw[l]['wd']], [(D, F32)], tap_widths=(FFP, FFP, D))
        saved.append(rec)

    dy, loss_part = loss_kernel("loss", xs, target)
    loss = lax.psum(loss_part[0, 0], ("x", "y", "c"))

    rblk = lambda a: a.reshape(N_DEV, a.shape[0] // N_DEV, a.shape[1])
    cblk = lambda a: a.reshape(a.shape[0], N_DEV, a.shape[1] // N_DEV).transpose(1, 0, 2)
    dmod = [None] * DEPTH
    dk_tot = []
    dv_tot = []
    dx = dy
    sends = [None] * DEPTH
    send_token = jnp.zeros((1, 1), F32)
    g_n1 = [None] * DEPTH
    g_n2 = [None] * DEPTH
    g_bglu = [None] * N_A
    g_dskip = [None] * N_A
    g_s5 = [None] * N_A
    g_qn, g_q128 = [None] * 2, [None] * 2
    for l in range(DEPTH - 1, -1, -1):
        rec = saved[l]
        sh1, sc1, gt1, sh2, sc2, gt2 = mods[l]
        dx, dgate, dup, dyd, h_b, a_b, dg2, dsh2, dsc2, dgt2 = ffn_backward(
            "ffn_bwd", rec['x_mid'], dx, g2[l], sh2, sc2, gt2 + send_token, lw[l]['wg'], lw[l]['wu'], lw[l]['wd'])
        out_l = [matmul_tn("tn_ffn_in", h_b, dgate, _MXU, col_blocks=N_DEV),
                 matmul_tn("tn_ffn_in", h_b, dup, _MXU, col_blocks=N_DEV),
                 matmul_tn("tn_ffn_out", a_b, dyd, _MXU).reshape(N_DEV, FFB, D)]
        g_n2[l] = dg2
        if l < N_A:
            (dx, dyy), (dz,), (g_b,), (dgt1, dbg) = seg_backward(
                "glu_bwd", seg_glu, [rec['x_in'], rec['y']], [gt1, _row(bglu_full[l])], [], [lw[l]['second']],
                [dx], (D,), (D,))
            out_l.append(rblk(matmul_tn("tn_sq", g_b, dz, _MXU)))
            g_bglu[l] = dbg
            wb, wc, a_tab = s5_mats[l]
            dh, dwb, dwc, da, dd = s5_scan_bwd("s5_scan_bwd", rec['h'], dyy, rec['s0'], wb, wc, a_tab, _row(d_full[l]))
            g_dskip[l] = dd
            dab_re, dab_im, dbb_re_t, dbb_im_t, dc_re, dc_im = _s5_unplace(dwb, dwc, da)
            dlr, dli, dldt, dbr_t, dbi_t = s5_prep_bwd("s5_prep_bwd", *s5_raw[l], expand,
                                                       (dab_re, dab_im, dbb_re_t, dbb_im_t))
            g_s5[l] = (dlr.reshape(G, N), dli.reshape(G, N), dldt.reshape(G),
                       dbr_t.reshape(P, G, N).transpose(1, 2, 0), dbi_t.reshape(P, G, N).transpose(1, 2, 0), dc_re, dc_im)
            (dx,), _, _, (dg1, dsh1, dsc1) = seg_backward(
                "pre_bwd", seg_pre, [rec['x_in']], [g1[l], sh1, sc1], [], [], [dh], (), (), dx_add=dx)
        else:
            j = l - N_A
            (dx, do), (dzo,), (o_b,), (dgt1,) = seg_backward(
                "o_bwd", seg_o, [rec['x_in'], rec['o']], [gt1], [], [lw[l]['second']], [dx], (D,), (D,))
            out_l.append(rblk(matmul_tn("tn_sq", o_b, dzo, _MXU)))
            dq, dk, dv = attn_bwd("attn_bwd", rec['q'], kv['k'], kv['v'], rec['o'], do, rec['lse'])
            dk_tot.append(dk)
            dv_tot.append(dv)
            (dx,), (dql, dqq), (hq_b, qn_b), (dg1, dsh1, dsc1, dqg, dq128) = seg_backward(
                "q_bwd", seg_q, [rec['x_in']], rec['q_smalls'], [cosf, sinf], [lw[l]['wdq'], lw[l]['wuq_pad']],
                [dq], (QL, H * HD), (D, QL), dx_add=dx)
            g_dq = rblk(matmul_tn("tn_dq", hq_b, dql, _MXU))
            g_uq = cblk(_unpad_heads(matmul_tn("tn_uq", qn_b, dqq, _MXU), DN + DR, HD))
            g_qn[j], g_q128[j] = dqg, dq128
        g_n1[l] = dg1
        dmod[l] = jnp.concatenate([dsh1, dsc1, dgt1, dsh2, dsc2, dgt2], axis=1)
        if l == N_A:
            dkk = sum_parts("sum_dk", jnp.stack(dk_tot))
            dvv = sum_parts("sum_dv", jnp.stack(dv_tot))
            (dx,), (dta, dtk, dtv), (hk_b, ckv_b), (dkg, dksh, dksc, dag, dgkn, dgkr) = seg_backward(
                "kv_bwd", seg_kv, [kv['x_in']], kv['smalls'], [cosf, sinf], kv['w'],
                [dkk, dvv], (KVL + HD, H * HD, H * DV), (D, KVL), dx_add=dx)
            g_wa = matmul_tn("tn_kva", hk_b, dta, _MXU)
            g_wa = jnp.concatenate([g_wa[:, :KVL], g_wa[:, KVL + DN:KVL + DN + DR]], axis=1)
            g_kn = matmul_tn("tn_kn", ckv_b, dtk, _MXU).reshape(KVL, H, HD)[:, :, :DN]
            g_v = matmul_tn("tn_v", ckv_b, dtv, _MXU).reshape(KVL, H, DV)
            g_wkvb = jnp.concatenate([g_kn, g_v], axis=2).reshape(KVL, H * (DN + DV))
            dkmod = jnp.concatenate([dksh, dksc], axis=1)
            out_l += [jnp.concatenate([cblk(g_wkvb), g_dq], axis=1), rblk(g_wa)]
        if l > N_A:
            out_l.append(g_dq)
        if l >= N_A:
            out_l.append(g_uq)
        if l > 0:
            sends[l] = exchange_start(f"a2a_start_{l}", out_l, False, dx)
            send_token = sends[l][4][0:1, 0:1]
    grad_x = dx

    dm = jnp.concatenate(dmod + [dkmod], axis=1)[0]
    per_dev = []
    for d in range(N_DEV):
        cols = [dm[6 * D * l + per_l * d:6 * D * l + per_l * (d + 1)] for l in range(DEPTH)]
        cols.append(dm[6 * D * DEPTH + (2 * D // N_DEV) * d:6 * D * DEPTH + (2 * D // N_DEV) * (d + 1)])
        per_dev.append(jnp.concatenate(cols))
    dm_dev = jnp.stack(per_dev)
    gdm = all_gather("gather_dmod", dm_dev)
    dm_mine = lax.dynamic_index_in_dim(gdm, me, axis=1, keepdims=False)
    g_wmod = small_matmul_tn("dmod_matmul", ca_all, dm_mine)
    g_ada_w = jnp.stack([g_wmod[:, per_l * l:per_l * (l + 1)] for l in range(DEPTH)])
    g_kv_ada_w = g_wmod[:, per_l * DEPTH:]
    dm_sum = sum_parts("sum_dmod", gdm.reshape(N_DEV, N_DEV, n_mod))
    g_ada_b = jnp.stack([jnp.concatenate([dm_sum[d, per_l * l:per_l * (l + 1)] for d in range(N_DEV)])
                         for l in range(DEPTH)])
    g_kv_ada_b = jnp.concatenate([dm_sum[d, per_l * DEPTH:] for d in range(N_DEV)])

    small = {
        'norm1_g': jnp.concatenate(g_n1, axis=0), 'norm2_g': jnp.concatenate(g_n2, axis=0),
        's5_lam_re': jnp.stack([g_s5[l][0] for l in range(N_A)]), 's5_lam_im': jnp.stack([g_s5[l][1] for l in range(N_A)]),
        's5_log_dt': jnp.stack([g_s5[l][2] for l in range(N_A)]),
        's5_b_re': jnp.stack([g_s5[l][3] for l in range(N_A)]), 's5_b_im': jnp.stack([g_s5[l][4] for l in range(N_A)]),
        's5_c_re': jnp.stack([g_s5[l][5] for l in range(N_A)]), 's5_c_im': jnp.stack([g_s5[l][6] for l in range(N_A)]),
        'kv_norm_g': dkg, 'kv_a_norm_g': dag, 'k_nope_norm_g': dgkn[:, :DN], 'k_rope_norm_g': dgkr[:, DN:DN + DR],
        'mla_q_norm_g': jnp.concatenate(g_qn, axis=0),
        'mla_q_nope_norm_g': jnp.concatenate([g[:, :DN] for g in g_q128], axis=0),
        'mla_q_rope_norm_g': jnp.concatenate([g[:, DN:DN + DR] for g in g_q128], axis=0),
        's5_d': jnp.concatenate(g_dskip, axis=0), 's5_b_glu': jnp.concatenate(g_bglu, axis=0),
    }
    small_names = [n for n in REPLICATED if n not in ('ada_b', 'kv_ada_b')] + SHARDED_VEC
    flat_small = jnp.concatenate([small[n].reshape(-1) for n in small_names])
    n_small = int(flat_small.shape[0])
    pad_small = -(-n_small // 65536) * 65536
    flat_small = jnp.pad(flat_small, (0, pad_small - n_small)).reshape(pad_small // 128, 128)
    g_small_sum = sum_parts("sum_small", all_gather("gather_small", flat_small)).reshape(-1)
    sends[0] = exchange_start("a2a_start_0", out_l, False, g_small_sum)
    g_small_sum = g_small_sum + sends[0][4][0, 0]
    grads = {}
    off = 0
    for n in small_names:
        size = int(np.prod(small[n].shape))
        full = g_small_sum[off:off + size]
        off += size
        if n in SHARDED_VEC:
            full = lax.dynamic_slice_in_dim(full.reshape(N_A, D), me * (D // N_DEV), D // N_DEV, axis=1)
        grads[n] = full.reshape(W[n].shape)
    grads['ada_b'] = g_ada_b
    grads['kv_ada_b'] = g_kv_ada_b

    packed_names = REPLICATED + SHARDED_VEC

    def pack(dct):
        flat_ = jnp.concatenate([dct[n].reshape(-1) for n in packed_names])
        n_ = int(flat_.shape[0])
        p_ = -(-n_ // 65536) * 65536
        return jnp.pad(flat_, (0, p_ - n_)).reshape(p_ // 128, 128)

    _, d_p, m_p, v_p = adamw("adamw_small", pack(grads)[None], pack(W)[None], pack(M)[None], pack(V)[None])
    out_delta, out_m, out_v = {}, {}, {}
    off = 0
    d_p, m_p, v_p = d_p.reshape(-1), m_p.reshape(-1), v_p.reshape(-1)
    for n in packed_names:
        size = int(np.prod(W[n].shape))
        out_delta[n] = d_p[off:off + size].reshape(W[n].shape)
        out_m[n] = m_p[off:off + size].reshape(W[n].shape)
        out_v[n] = v_p[off:off + size].reshape(W[n].shape)
        off += size

    def update(name, parts, base=0, stride=0):
        shp = W[name].shape
        shp3 = shp if len(shp) == 3 else (1,) + shp
        res = adamw("adamw_" + name, parts, W[name].reshape(shp3), M[name].reshape(shp3), V[name].reshape(shp3),
                    base, stride)
        grads[name], out_delta[name], out_m[name], out_v[name] = (a.reshape(shp) for a in res)

    update('ada_w', g_ada_w.reshape(1, DEPTH * D, per_l), 0, D)
    update('kv_ada_w', g_kv_ada_w[None])

    chains = {}

    def update_layer(name, parts, layer, base=0):
        shp = W[name].shape
        shp3 = shp if len(shp) == 3 else (1,) + shp
        chains[name] = adamw_layer(f"adamw_{name}_{layer}", parts, W[name].reshape(shp3), M[name].reshape(shp3),
                                   V[name].reshape(shp3), layer, chains.get(name), base)
        grads[name], out_delta[name], out_m[name], out_v[name] = (a.reshape(shp) for a in chains[name])

    for l in range(DEPTH):
        lands = exchange_wait(f"a2a_wait_{l}", sends[l], d_p, False)
        recv = [lax.dynamic_update_slice(ld, lax.dynamic_index_in_dim(src, me, 0, keepdims=True), (me,) + (0,) * (src.ndim - 1))
                for ld, src in zip(lands, sends[l][2])]
        update_layer('ffn_w_gate', recv[0], l)
        update_layer('ffn_w_up', recv[1], l)
        update_layer('ffn_w_down', recv[2], l)
        if l < N_A:
            update_layer('s5_w_glu', recv[3], l)
        else:
            update_layer('mla_w_o', recv[3], l - N_A)
            if l == N_A:
                update_layer('w_kv_b', recv[4], 0)
                update_layer('mla_w_dq', recv[4], 0, KVL)
                update_layer('w_kv_a', recv[5], 0)
            else:
                update_layer('mla_w_dq', recv[4], l - N_A)
            update_layer('mla_w_uq', recv[-1], l - N_A)

    return (loss, grad_x[None], *[grads[n] for n in WEIGHT_NAMES], *[out_delta[n] for n in WEIGHT_NAMES],
            *[out_m[n] for n in WEIGHT_NAMES], *[out_v[n] for n in WEIGHT_NAMES])
```

```python
import functools
import math

import numpy as np
import jax
import jax.numpy as jnp
from jax import lax
from jax.experimental import pallas as pl
from jax.experimental.pallas import tpu as pltpu

F32 = jnp.float32
_MXU = jnp.bfloat16
HI = lax.Precision.HIGHEST

D = 1024
DEPTH = 4
N_A = 2
FF = 2816
FFB = 384
FFP = 8 * FFB
N_DEV = 8
G = 64
P = 16
N = 64
GB = 8
NBLK = G // GB
HALF = GB * N
H = 16
HP = H // 2
DN, DR, DV = 64, 32, 64
HD = 128
QL = 256
KVL = 256
CHUNK = 64
ROPE_THETA = 10000.0
ATTN_SCALE = 1.0 / math.sqrt(DN + DR)
LOG2E = 1.4426950408889634
EXP2_SCALE = ATTN_SCALE * LOG2E
EPS = 1e-6
ADAM_LR, ADAM_B1, ADAM_B2, ADAM_EPS, ADAM_WD, ADAM_STEP = 0.001, 0.9, 0.999, 1e-08, 0.01, 10
VMEM_LIMIT = 56 * 1024 * 1024
MESH = pl.DeviceIdType.MESH

TILE_ROW = 256
TILE_ATT = 512
TILE_SCAN = 256


def _params(n_grid):
    return pltpu.CompilerParams(dimension_semantics=("arbitrary",) * n_grid, vmem_limit_bytes=VMEM_LIMIT)


@jax.custom_vjp
def mm(a, w):
    return jnp.dot(a.astype(_MXU), w, preferred_element_type=F32)


def _mm_fwd(a, w):
    return mm(a, w), w


def _mm_bwd(w, g):
    da = lax.dot_general(g.astype(_MXU), w, (((1,), (1,)), ((), ())), preferred_element_type=F32)
    return da, jnp.zeros_like(w)


mm.defvjp(_mm_fwd, _mm_bwd)


def rms(x, g):
    return x * lax.rsqrt(jnp.mean(x * x, axis=-1, keepdims=True) + EPS) * g


def modulate(h, shift, scale):
    return h * (1.0 + scale) + shift


def _lane(n=HD):
    return lax.broadcasted_iota(jnp.int32, (1, n), 1)


def _rot_matrix():
    r = lax.broadcasted_iota(jnp.int32, (HD, HD), 0)
    c = lax.broadcasted_iota(jnp.int32, (HD, HD), 1)
    first = (c >= DN) & (c < DN + DR // 2) & (r == c + DR // 2)
    second = (c >= DN + DR // 2) & (c < DN + DR) & (r == c - DR // 2)
    return jnp.where(first, -1.0, jnp.where(second, 1.0, 0.0)).astype(F32)


def head_norm_rope(xh, g128, cosf, sinf, rot, with_nope):
    lane = _lane()
    m_n = lane < DN
    m_r = (lane >= DN) & (lane < DN + DR)
    sq = xh * xh
    inv_r = lax.rsqrt(jnp.sum(jnp.where(m_r, sq, 0.0), axis=-1, keepdims=True) / DR + EPS)
    if with_nope:
        inv_n = lax.rsqrt(jnp.sum(jnp.where(m_n, sq, 0.0), axis=-1, keepdims=True) / DN + EPS)
        inv = jnp.where(m_n, inv_n, jnp.where(m_r, inv_r, 0.0))
    else:
        inv = jnp.where(m_r, inv_r, 0.0)
    xg = xh * inv * g128
    return xg * cosf + jnp.dot(xg, rot, precision=HI, preferred_element_type=F32) * sinf


def seg_pre(x, g, sh, sc):
    return (modulate(rms(x, g), sh, sc),), ()


def seg_ffn(x, g, sh, sc, gt, t_g, t_u, t_d, wg, wu, wd):
    h = modulate(rms(x, g), sh, sc)
    gate = mm(h, wg) + t_g
    up = mm(h, wu) + t_u
    a = jax.nn.silu(gate) * up
    y = mm(a, wd) + t_d
    return (x + gt * y,), (h.astype(_MXU), a.astype(_MXU))


def seg_glu(x, y, gt, b, t_z, w):
    g = jax.nn.gelu(y)
    z = mm(g, w) + b + t_z
    return (x + gt * (g * jax.nn.sigmoid(z)),), (g.astype(_MXU),)


def seg_o(x, o, gt, t_o, w):
    return (x + gt * (mm(o, w) + t_o),), (o.astype(_MXU),)


def seg_q(x, g, sh, sc, qg, g128, t_l, t_q, cosf, sinf, wdq, wuq):
    h = modulate(rms(x, g), sh, sc)
    ql = mm(h, wdq) + t_l
    qn = rms(ql, qg)
    q = mm(qn, wuq) + t_q
    rot = _rot_matrix()
    heads = [head_norm_rope(q[:, HD * i:HD * (i + 1)], g128, cosf, sinf, rot, True) for i in range(H)]
    return (jnp.concatenate(heads, axis=1),), (h.astype(_MXU), qn.astype(_MXU))


def seg_kv(x, g, sh, sc, ag, gkn, gkr, t_a, t_k, t_v, cosf, sinf, wa, wkn, wv):
    hk = modulate(rms(x, g), sh, sc)
    kva = mm(hk, wa) + t_a
    ckv = rms(kva[:, :KVL], ag)
    kr = head_norm_rope(kva[:, KVL:KVL + HD], gkr, cosf, sinf, _rot_matrix(), False)
    kn = mm(ckv, wkn) + t_k
    v = mm(ckv, wv) + t_v
    heads = []
    for i in range(H):
        kh = kn[:, HD * i:HD * (i + 1)]
        inv = lax.rsqrt(jnp.sum(kh * kh, axis=-1, keepdims=True) / DN + EPS)
        heads.append(kh * inv * gkn + kr)
    return (jnp.concatenate(heads, axis=1), v), (hk.astype(_MXU), ckv.astype(_MXU))


def _row_call(name, body_fn, rows, fulls, out_rows, out_accs, tile):
    s = rows[0].shape[0]
    n_tiles = s // tile
    n_rows, n_fulls, n_or, n_oa = len(rows), len(fulls), len(out_rows), len(out_accs)

    def kern(*refs):
        i = pl.program_id(0)
        row_v = [r[...] for r in refs[:n_rows]]
        full_v = [r[...] for r in refs[n_rows:n_rows + n_fulls]]
        o_refs = refs[n_rows + n_fulls:]
        ro, ao = body_fn(row_v, full_v)
        for r, v in zip(o_refs[:n_or], ro):
            r[...] = v.astype(r.dtype)
        if n_oa:
            @pl.when(i == 0)
            def _():
                for r in o_refs[n_or:]:
                    r[...] = jnp.zeros(r.shape, r.dtype)
            for r, v in zip(o_refs[n_or:], ao):
                r[...] += v.astype(r.dtype)

    in_specs = [pl.BlockSpec((tile, a.shape[1]), lambda i: (i, 0)) for a in rows]
    for a in fulls:
        big = a.size * a.dtype.itemsize > (1 << 20)
        nd = a.ndim
        in_specs.append(pl.BlockSpec(a.shape, functools.partial(lambda i, nd_: (0,) * nd_, nd_=nd),
                                     **({"pipeline_mode": pl.Buffered(1)} if big else {})))
    out_shape = [jax.ShapeDtypeStruct((s, w), dt) for w, dt in out_rows]
    out_shape += [jax.ShapeDtypeStruct(shp, dt) for shp, dt in out_accs]
    out_specs = [pl.BlockSpec((tile, w), lambda i: (i, 0)) for w, _ in out_rows]
    out_specs += [pl.BlockSpec(shp, functools.partial(lambda i, nd_: (0,) * nd_, nd_=len(shp))) for shp, _ in out_accs]
    res = pl.pallas_call(kern, out_shape=out_shape, grid=(n_tiles,), in_specs=in_specs, out_specs=out_specs,
                         name=name, compiler_params=_params(1))(*rows, *fulls)
    return list(res)


def seg_forward(name, seg, rows, smalls, consts_rows, consts_full, out_widths, tile=TILE_ROW, tap_widths=()):
    n_r, n_s, n_cr = len(rows), len(smalls), len(consts_rows)

    def body(row_v, full_v):
        t = row_v[0].shape[0]
        taps = [jnp.zeros((t, w), F32) for w in tap_widths]
        outs, _ = seg(*row_v[:n_r], *full_v[:n_s], *taps, *row_v[n_r:], *full_v[n_s:])
        return outs, ()

    return _row_call(name, body, list(rows) + list(consts_rows), list(smalls) + list(consts_full),
                     out_widths, [], tile)


def seg_backward(name, seg, rows, smalls, consts_rows, consts_full, cots, tap_widths, aux_widths,
                 dx_add=None, tile=TILE_ROW):
    n_r, n_s, n_cr, n_c = len(rows), len(smalls), len(consts_rows), len(cots)
    has_add = dx_add is not None

    def body(row_v, full_v):
        t = row_v[0].shape[0]
        prim_rows = row_v[:n_r]
        c_rows = row_v[n_r:n_r + n_cr]
        cot_v = row_v[n_r + n_cr:n_r + n_cr + n_c]
        add_v = row_v[n_r + n_cr + n_c] if has_add else None
        small_v = full_v[:n_s]
        c_full = full_v[n_s:]
        taps = [jnp.zeros((t, w), F32) for w in tap_widths]

        def f(*args):
            return seg(*args, *c_rows, *c_full)

        _, vjp_fn, aux = jax.vjp(f, *prim_rows, *small_v, *taps, has_aux=True)
        grads = vjp_fn(tuple(c.astype(F32) for c in cot_v))
        d_rows = list(grads[:n_r])
        if has_add:
            d_rows[0] = d_rows[0] + add_v
        d_small = grads[n_r:n_r + n_s]
        d_taps = grads[n_r + n_s:]
        return d_rows + list(d_taps) + list(aux), [jnp.sum(g, axis=0, keepdims=True) if g.shape[0] != 1 else g
                                                   for g in d_small]

    all_rows = list(rows) + list(consts_rows) + list(cots) + ([dx_add] if has_add else [])
    out_rows = [(a.shape[1], F32) for a in rows] + [(w, _MXU) for w in tap_widths] + [(w, _MXU) for w in aux_widths]
    out_accs = [((1, a.shape[1]), F32) for a in smalls]
    res = _row_call(name, body, all_rows, list(smalls) + list(consts_full), out_rows, out_accs, tile)
    n_t, n_a = len(tap_widths), len(aux_widths)
    return res[:n_r], res[n_r:n_r + n_t], res[n_r + n_t:n_r + n_t + n_a], res[n_r + n_t + n_a:]


def _split(n):
    if n <= 1024:
        return n
    for t in (1408, 1024, 768, 512, 256, 128):
        if n % t == 0:
            return t
    raise ValueError(n)


def matmul_tn(name, a, b, out_dtype, col_blocks=None):
    s, k1 = a.shape
    _, k2 = b.shape
    tm, ts = _split(k1), 512
    if col_blocks is None:
        tn, per_step, wblk = _split(k2), 1, None
    else:
        wblk = k2 // col_blocks
        per_step = max(1, min(col_blocks, 1536 // wblk))
        tn = per_step * wblk
    n_s = s // ts

    def kern(a_ref, b_ref, o_ref, acc_ref):
        k = pl.program_id(2)

        @pl.when(k == 0)
        def _():
            acc_ref[...] = jnp.zeros(acc_ref.shape, F32)

        acc_ref[...] += lax.dot_general(a_ref[...], b_ref[...], (((0,), (0,)), ((), ())),
                                        preferred_element_type=F32)

        @pl.when(k == n_s - 1)
        def _():
            if col_blocks is None:
                o_ref[...] = acc_ref[...].astype(o_ref.dtype)
            else:
                for cb in range(per_step):
                    o_ref[cb] = acc_ref[:, wblk * cb:wblk * (cb + 1)].astype(o_ref.dtype)

    if col_blocks is None:
        out_shape = jax.ShapeDtypeStruct((k1, k2), out_dtype)
        out_spec = pl.BlockSpec((tm, tn), lambda i, j, k: (i, j))
    else:
        out_shape = jax.ShapeDtypeStruct((col_blocks, k1, wblk), out_dtype)
        out_spec = pl.BlockSpec((per_step, tm, wblk), lambda i, j, k: (j, i, 0))
    return pl.pallas_call(
        kern, out_shape=out_shape, grid=(k1 // tm, k2 // tn, n_s),
        in_specs=[pl.BlockSpec((ts, tm), lambda i, j, k: (k, i)), pl.BlockSpec((ts, tn), lambda i, j, k: (k, j))],
        out_specs=out_spec,
        scratch_shapes=[pltpu.VMEM((tm, tn), F32)], name=name, compiler_params=_params(3))(a, b)


def ffn_backward(name, x, dxo, g, sh, sc, gt, wg, wu, wd, tile=TILE_ROW):
    s = x.shape[0]
    blk = 2 * FFB
    n_blk = wg.shape[1] // blk

    def kern(x_ref, dxo_ref, g_ref, sh_ref, sc_ref, gt_ref, wg_ref, wu_ref, wd_ref,
             dx_ref, dg_ref, du_ref, dy_ref, h_ref, a_ref, dgn_ref, dsh_ref, dsc_ref, dgt_ref):
        i = pl.program_id(0)

        @pl.when(i == 0)
        def _():
            for r in (dgn_ref, dsh_ref, dsc_ref, dgt_ref):
                r[...] = jnp.zeros(r.shape, F32)

        dxo = dxo_ref[...]
        h, pre_vjp = jax.vjp(lambda *p: modulate(rms(p[0], p[1]), p[2], p[3]), x_ref[...], g_ref[...], sh_ref[...],
                             sc_ref[...])
        hb = h.astype(_MXU)
        h_ref[...] = hb
        dyb = (gt_ref[...] * dxo).astype(_MXU)
        dy_ref[...] = dyb
        y = jnp.zeros((tile, D), F32)
        dh = jnp.zeros((tile, D), F32)
        tr = (((1,), (1,)), ((), ()))
        for c in range(n_blk):
            cs = slice(blk * c, blk * (c + 1))
            gate = jnp.dot(hb, wg_ref[:, cs], preferred_element_type=F32)
            up = jnp.dot(hb, wu_ref[:, cs], preferred_element_type=F32)
            sig = jax.nn.sigmoid(gate)
            sl = gate * sig
            ab = (sl * up).astype(_MXU)
            a_ref[:, cs] = ab
            y = y + jnp.dot(ab, wd_ref[cs, :], preferred_element_type=F32)
            da = lax.dot_general(dyb, wd_ref[cs, :], tr, preferred_element_type=F32)
            dgb = (da * up * (sig * (1.0 + gate * (1.0 - sig)))).astype(_MXU)
            dub = (da * sl).astype(_MXU)
            dg_ref[:, cs] = dgb
            du_ref[:, cs] = dub
            dh = dh + lax.dot_general(dgb, wg_ref[:, cs], tr, preferred_element_type=F32) \
                + lax.dot_general(dub, wu_ref[:, cs], tr, preferred_element_type=F32)
        dgt_ref[...] += jnp.sum(dxo * y, axis=0, keepdims=True)
        dx_pre, dgn, dsh, dsc = pre_vjp(dh)
        dx_ref[...] = dxo + dx_pre
        dgn_ref[...] += dgn
        dsh_ref[...] += dsh
        dsc_ref[...] += dsc

    row = lambda w: pl.BlockSpec((tile, w), lambda i: (i, 0))
    vec = pl.BlockSpec((1, D), lambda i: (0, 0))
    wspec = lambda a: pl.BlockSpec(a.shape, lambda i: (0, 0), pipeline_mode=pl.Buffered(1))
    rows_out = [(D, F32), (wg.shape[1], _MXU), (wg.shape[1], _MXU), (D, _MXU), (D, _MXU), (wg.shape[1], _MXU)]
    res = pl.pallas_call(
        kern,
        out_shape=[jax.ShapeDtypeStruct((s, w), dt) for w, dt in rows_out] + [jax.ShapeDtypeStruct((1, D), F32)] * 4,
        grid=(s // tile,),
        in_specs=[row(D), row(D), vec, vec, vec, vec, wspec(wg), wspec(wu), wspec(wd)],
        out_specs=[row(w) for w, _ in rows_out] + [vec] * 4,
        name=name, compiler_params=_params(1))(x, dxo, g, sh, sc, gt, wg, wu, wd)
    return res


def small_matmul(name, a, w, tn=256):
    m, k = a.shape
    n = w.shape[1]

    def kern(a_ref, w_ref, o_ref):
        o_ref[...] = jnp.dot(a_ref[...].astype(_MXU), w_ref[...].astype(_MXU), preferred_element_type=F32)

    return pl.pallas_call(kern, out_shape=jax.ShapeDtypeStruct((m, n), F32), grid=(n // tn,),
                          in_specs=[pl.BlockSpec((m, k), lambda j: (0, 0)), pl.BlockSpec((k, tn), lambda j: (0, j))],
                          out_specs=pl.BlockSpec((m, tn), lambda j: (0, j)), name=name,
                          compiler_params=_params(1))(a, w)


def small_matmul_tn(name, a, b, tn=256):
    m, k = a.shape
    n = b.shape[1]

    def kern(a_ref, b_ref, o_ref):
        o_ref[...] = lax.dot_general(a_ref[...].astype(_MXU), b_ref[...].astype(_MXU), (((0,), (0,)), ((), ())),
                                     preferred_element_type=F32)

    return pl.pallas_call(kern, out_shape=jax.ShapeDtypeStruct((k, n), F32), grid=(n // tn,),
                          in_specs=[pl.BlockSpec((m, k), lambda j: (0, 0)), pl.BlockSpec((m, tn), lambda j: (0, j))],
                          out_specs=pl.BlockSpec((k, tn), lambda j: (0, j)), name=name,
                          compiler_params=_params(1))(a, b)


def _s5_prep_math(lam_re, lam_im, log_dt, b_re_t, b_im_t, expand):
    dt = jnp.dot(jnp.exp(log_dt), expand, precision=HI, preferred_element_type=F32)
    mag = jnp.exp(lam_re * dt)
    ab_re = mag * jnp.cos(lam_im * dt)
    ab_im = mag * jnp.sin(lam_im * dt)
    den = lam_re * lam_re + lam_im * lam_im
    nr = ab_re - 1.0
    ni = ab_im
    f_re = (nr * lam_re + ni * lam_im) / den
    f_im = (ni * lam_re - nr * lam_im) / den
    bb_re = f_re * b_re_t - f_im * b_im_t
    bb_im = f_re * b_im_t + f_im * b_re_t
    return ab_re, ab_im, bb_re, bb_im


def _whole(kern, name, out_shape, *args):
    return pl.pallas_call(kern, out_shape=out_shape, name=name,
                          compiler_params=pltpu.CompilerParams(vmem_limit_bytes=VMEM_LIMIT))(*args)


def s5_prep_fwd(name, lam_re, lam_im, log_dt, b_re_t, b_im_t, expand):
    def kern(a, b, c, d, e, f, o0, o1, o2, o3):
        r = _s5_prep_math(a[...], b[...], c[...], d[...], e[...], f[...])
        for o, v in zip((o0, o1, o2, o3), r):
            o[...] = v

    gn = lam_re.shape[1]
    shp = [jax.ShapeDtypeStruct((1, gn), F32)] * 2 + [jax.ShapeDtypeStruct((P, gn), F32)] * 2
    return _whole(kern, name, shp, lam_re, lam_im, log_dt, b_re_t, b_im_t, expand)


def s5_prep_bwd(name, lam_re, lam_im, log_dt, b_re_t, b_im_t, expand, cots):
    def kern(a, b, c, d, e, f, c0, c1, c2, c3, o0, o1, o2, o3, o4):
        ex = f[...]
        _, vjp_fn = jax.vjp(lambda *p: _s5_prep_math(*p, ex), a[...], b[...], c[...], d[...], e[...])
        g = vjp_fn((c0[...], c1[...], c2[...], c3[...]))
        for o, v in zip((o0, o1, o2, o3, o4), g):
            o[...] = v

    shp = [jax.ShapeDtypeStruct(a.shape, F32) for a in (lam_re, lam_im, log_dt, b_re_t, b_im_t)]
    return _whole(kern, name, shp, lam_re, lam_im, log_dt, b_re_t, b_im_t, expand, *cots)


def _cpowers(ar, ai):
    pw = [(ar, ai)]
    for _ in range(7):
        pr, pi = pw[-1]
        pw.append((pr * ar - pi * ai, pr * ai + pi * ar))
    return pw


def _row_select(row, values):
    out = jnp.broadcast_to(values[7], (8, values[7].shape[1]))
    for r in range(6, -1, -1):
        out = jnp.where(row == r, values[r], out)
    return out


def _scan_tables(ar, ai, reverse):
    pw = _cpowers(ar, ai)
    row = lax.broadcasted_iota(jnp.int32, (8, ar.shape[1]), 0)
    steps = []
    for d in (1, 2, 4):
        keep = (row <= 7 - d) if reverse else (row >= d)
        steps.append((jnp.where(keep, pw[d - 1][0], 0.0), jnp.where(keep, pw[d - 1][1], 0.0)))
    order = list(range(7, -1, -1)) if reverse else list(range(8))
    carry = (_row_select(row, [pw[i][0] for i in order]), _row_select(row, [pw[i][1] for i in order]))
    return steps, carry


def _tile_scan_fwd(xr, xi, cr, ci, steps, carry_m):
    for d, (mr, mi) in zip((1, 2, 4), steps):
        sr = pltpu.roll(xr, d, 0)
        si = pltpu.roll(xi, d, 0)
        xr, xi = xr + mr * sr - mi * si, xi + mr * si + mi * sr
    pr, pi = carry_m
    return xr + pr * cr - pi * ci, xi + pr * ci + pi * cr


def _tile_scan_rev(xr, xi, cr, ci, steps, carry_m):
    for d, (mr, mi) in zip((1, 2, 4), steps):
        sr = pltpu.roll(xr, 8 - d, 0)
        si = pltpu.roll(xi, 8 - d, 0)
        xr, xi = xr + mr * sr + mi * si, xi + mr * si - mi * sr
    pr, pi = carry_m
    return xr + pr * cr + pi * ci, xi + pr * ci - pi * cr


def _fwd_scan_block(buf, row0, n_tiles8, ar, ai, c0r, c0i):
    steps, carry_m = _scan_tables(ar, ai, False)

    def body(j, carry):
        cr, ci = carry
        r0 = pl.multiple_of(row0 + j * 8, 8)
        xr = buf[pl.ds(r0, 8), 0:HALF]
        xi = buf[pl.ds(r0, 8), HALF:2 * HALF]
        xr, xi = _tile_scan_fwd(xr, xi, cr, ci, steps, carry_m)
        buf[pl.ds(r0, 8), 0:HALF] = xr
        buf[pl.ds(r0, 8), HALF:2 * HALF] = xi
        return xr[7:8], xi[7:8]

    return lax.fori_loop(0, n_tiles8, body, (c0r, c0i))


def s5_scan_fwd(name, h, wb, wc, a_tab, dskip, tile=TILE_SCAN):
    s = h.shape[0]
    n_t = s // tile

    def kern(h_ref, wb_ref, wc_ref, a_ref, d_ref, y_ref, s0_ref, carry_ref, buf):
        i = pl.program_id(0)

        @pl.when(i == 0)
        def _():
            carry_ref[...] = jnp.zeros(carry_ref.shape, F32)

        s0_ref[0] = carry_ref[...]
        for k in range(NBLK):
            cols = slice(GB * P * k, GB * P * (k + 1))
            u = h_ref[:, cols]
            buf[...] = jnp.dot(u.astype(_MXU), wb_ref[k], preferred_element_type=F32)
            ar = a_ref[k, :, 0:HALF]
            ai = a_ref[k, :, HALF:2 * HALF]
            cr, ci = _fwd_scan_block(buf, 0, tile // 8, ar, ai, carry_ref[k:k + 1, 0:HALF],
                                     carry_ref[k:k + 1, HALF:2 * HALF])
            carry_ref[k:k + 1, 0:HALF] = cr
            carry_ref[k:k + 1, HALF:2 * HALF] = ci
            y_ref[:, cols] = jnp.dot(buf[...].astype(_MXU), wc_ref[k], preferred_element_type=F32) + d_ref[:, cols] * u

    full = lambda a: pl.BlockSpec(a.shape, functools.partial(lambda i, nd_: (0,) * nd_, nd_=a.ndim))
    return pl.pallas_call(
        kern,
        out_shape=[jax.ShapeDtypeStruct((s, D), F32), jax.ShapeDtypeStruct((n_t, NBLK, 2 * HALF), F32)],
        grid=(n_t,),
        in_specs=[pl.BlockSpec((tile, D), lambda i: (i, 0)), full(wb), full(wc), full(a_tab), full(dskip)],
        out_specs=[pl.BlockSpec((tile, D), lambda i: (i, 0)), pl.BlockSpec((1, NBLK, 2 * HALF), lambda i: (i, 0, 0))],
        scratch_shapes=[pltpu.VMEM((NBLK, 2 * HALF), F32), pltpu.VMEM((tile, 2 * HALF), F32)],
        name=name, compiler_params=_params(1))(h, wb, wc, a_tab, dskip)


def s5_scan_bwd(name, h, dy, s0, wb, wc, a_tab, dskip, tile=TILE_SCAN):
    s = h.shape[0]
    n_t = s // tile
    n8 = tile // 8

    def kern(h_ref, dy_ref, s0_ref, wb_ref, wc_ref, a_ref, d_ref, dh_ref, dwb_ref, dwc_ref, da_ref, dd_ref,
             lam_ref, sbuf, gbuf):
        i = pl.program_id(0)

        @pl.when(i == 0)
        def _():
            lam_ref[...] = jnp.zeros(lam_ref.shape, F32)
            dwb_ref[...] = jnp.zeros(dwb_ref.shape, F32)
            dwc_ref[...] = jnp.zeros(dwc_ref.shape, F32)
            da_ref[...] = jnp.zeros(da_ref.shape, F32)
            dd_ref[...] = jnp.zeros(dd_ref.shape, F32)

        for k in range(NBLK):
            cols = slice(GB * P * k, GB * P * (k + 1))
            u = h_ref[:, cols]
            dyk = dy_ref[:, cols]
            ar = a_ref[k, :, 0:HALF]
            ai = a_ref[k, :, HALF:2 * HALF]
            sbuf[0:8, :] = jnp.broadcast_to(s0_ref[0, k:k + 1, :], (8, 2 * HALF))
            sbuf[8:tile + 8, :] = jnp.dot(u.astype(_MXU), wb_ref[k], preferred_element_type=F32)
            _fwd_scan_block(sbuf, 8, n8, ar, ai, s0_ref[0, k:k + 1, 0:HALF], s0_ref[0, k:k + 1, HALF:2 * HALF])
            dyb = dyk.astype(_MXU)
            gbuf[...] = lax.dot_general(dyb, wc_ref[k], (((1,), (1,)), ((), ())), preferred_element_type=F32)
            dwc_ref[k] += lax.dot_general(sbuf[8:tile + 8, :].astype(_MXU), dyb, (((0,), (0,)), ((), ())),
                                          preferred_element_type=F32)
            steps, carry_m = _scan_tables(ar, ai, True)
            row = lax.broadcasted_iota(jnp.int32, (8, HALF), 0)

            def body(jj, carry):
                cr, ci, dar, dai = carry
                j = n8 - 1 - jj
                r0 = pl.multiple_of(j * 8, 8)
                xr = gbuf[pl.ds(r0, 8), 0:HALF]
                xi = gbuf[pl.ds(r0, 8), HALF:2 * HALF]
                xr, xi = _tile_scan_rev(xr, xi, cr, ci, steps, carry_m)
                gbuf[pl.ds(r0, 8), 0:HALF] = xr
                gbuf[pl.ds(r0, 8), HALF:2 * HALF] = xi
                r1 = pl.multiple_of(j * 8 + 8, 8)
                spr = jnp.where(row == 0, sbuf[pl.ds(r0, 8), 0:HALF][7:8],
                                pltpu.roll(sbuf[pl.ds(r1, 8), 0:HALF], 1, 0))
                spi = jnp.where(row == 0, sbuf[pl.ds(r0, 8), HALF:2 * HALF][7:8],
                                pltpu.roll(sbuf[pl.ds(r1, 8), HALF:2 * HALF], 1, 0))
                dar = dar + xr * spr + xi * spi
                dai = dai + xi * spr - xr * spi
                return xr[0:1], xi[0:1], dar, dai

            z8 = jnp.zeros((8, HALF), F32)
            cr, ci, dar, dai = lax.fori_loop(
                0, n8, body, (lam_ref[k:k + 1, 0:HALF], lam_ref[k:k + 1, HALF:2 * HALF], z8, z8))
            lam_ref[k:k + 1, 0:HALF] = cr
            lam_ref[k:k + 1, HALF:2 * HALF] = ci
            da_ref[k:k + 1, 0:HALF] += jnp.sum(dar, axis=0, keepdims=True)
            da_ref[k:k + 1, HALF:2 * HALF] += jnp.sum(dai, axis=0, keepdims=True)
            lam = gbuf[...].astype(_MXU)
            dwb_ref[k] += lax.dot_general(u.astype(_MXU), lam, (((0,), (0,)), ((), ())), preferred_element_type=F32)
            du = lax.dot_general(lam, wb_ref[k], (((1,), (1,)), ((), ())), preferred_element_type=F32)
            dh_ref[:, cols] = du + d_ref[:, cols] * dyk
            dd_ref[:, cols] += jnp.sum(dyk * u, axis=0, keepdims=True)

    full = lambda a: pl.BlockSpec(a.shape, functools.partial(lambda i, nd_: (0,) * nd_, nd_=a.ndim))
    fullo = lambda shp: pl.BlockSpec(shp, functools.partial(lambda i, nd_: (0,) * nd_, nd_=len(shp)))
    rev = lambda i: (n_t - 1 - i, 0)
    return pl.pallas_call(
        kern,
        out_shape=[jax.ShapeDtypeStruct((s, D), F32), jax.ShapeDtypeStruct(wb.shape, F32),
                   jax.ShapeDtypeStruct(wc.shape, F32), jax.ShapeDtypeStruct((NBLK, 2 * HALF), F32),
                   jax.ShapeDtypeStruct((1, D), F32)],
        grid=(n_t,),
        in_specs=[pl.BlockSpec((tile, D), rev), pl.BlockSpec((tile, D), rev),
                  pl.BlockSpec((1, NBLK, 2 * HALF), lambda i: (n_t - 1 - i, 0, 0)),
                  full(wb), full(wc), full(a_tab), full(dskip)],
        out_specs=[pl.BlockSpec((tile, D), rev), fullo(wb.shape), fullo(wc.shape), fullo((NBLK, 2 * HALF)),
                   fullo((1, D))],
        scratch_shapes=[pltpu.VMEM((NBLK, 2 * HALF), F32), pltpu.VMEM((tile + 8, 2 * HALF), F32),
                        pltpu.VMEM((tile, 2 * HALF), F32)],
        name=name, compiler_params=_params(1))(h, dy, s0, wb, wc, a_tab, dskip)


def _chunk_mask(q0, k0, tq, tk):
    r = (q0 + lax.broadcasted_iota(jnp.int32, (tq, tk), 0)) // CHUNK
    c = (k0 + lax.broadcasted_iota(jnp.int32, (tq, tk), 1)) // CHUNK
    return r >= c


def _head_lanes(j):
    lane = _lane(2 * DV)
    return (lane >= DV * j) & (lane < DV * (j + 1))


def _raw_scores(q, kblk, masked, t):
    s = lax.dot_general(q, kblk, (((1,), (1,)), ((), ())), preferred_element_type=F32)
    return jnp.where(_chunk_mask(0, 0, t, t), s, -1e30) if masked else s


def attn_fwd(name, q, k, v, t=TILE_ATT):
    s = q.shape[0]
    n_q = s // t

    def kern(q_ref, k_ref, v_ref, o_ref, lse_ref):
        qi = pl.program_id(1)
        qs = [q_ref[:, HD * j:HD * (j + 1)] for j in range(2)]

        def scores(k0):
            return tuple(_raw_scores(qs[j], k_ref[pl.ds(k0, t), HD * j:HD * (j + 1)], False, t) for j in range(2))

        def absorb(k0, scs, carry):
            vblk = v_ref[pl.ds(k0, t), :]
            m_new = [jnp.maximum(carry[j][0], jnp.max(scs[j], axis=-1, keepdims=True)) for j in range(2)]
            ps = [jnp.exp2((scs[j] - m_new[j]) * EXP2_SCALE) for j in range(2)]
            alphas = [jnp.exp2((carry[j][0] - m_new[j]) * EXP2_SCALE) for j in range(2)]
            pvs = [jnp.dot(ps[j].astype(_MXU), vblk, preferred_element_type=F32) for j in range(2)]
            return tuple((m_new[j], alphas[j] * carry[j][1] + jnp.sum(ps[j], axis=-1, keepdims=True),
                          alphas[j] * carry[j][2] + pvs[j]) for j in range(2))

        def step(kb, state):
            scs, carry = state
            nxt = scores(pl.multiple_of((kb + 1) * t, t))
            return nxt, absorb(pl.multiple_of(kb * t, t), scs, carry)

        init = tuple((jnp.full((t, 1), -1e30, F32), jnp.zeros((t, 1), F32), jnp.zeros((t, 2 * DV), F32))
                     for _ in range(2))
        scs, carry = lax.fori_loop(0, qi, step, (scores(0), init))
        mask = _chunk_mask(0, 0, t, t)
        carry = absorb(pl.multiple_of(qi * t, t), tuple(jnp.where(mask, sc, -1e30) for sc in scs), carry)
        outs = []
        for j in range(2):
            m, l, acc = carry[j]
            outs.append(acc / l)
            lse_ref[0, j] = m * ATTN_SCALE + jnp.log(l)
        o_ref[...] = jnp.where(_head_lanes(0), outs[0], outs[1])

    return pl.pallas_call(
        kern,
        out_shape=[jax.ShapeDtypeStruct((s, H * DV), F32), jax.ShapeDtypeStruct((HP, 2, s, 1), F32)],
        grid=(HP, n_q),
        in_specs=[pl.BlockSpec((t, 2 * HD), lambda hp, i: (i, hp)), pl.BlockSpec((s, 2 * HD), lambda hp, i: (0, hp)),
                  pl.BlockSpec((s, 2 * DV), lambda hp, i: (0, hp))],
        out_specs=[pl.BlockSpec((t, 2 * DV), lambda hp, i: (i, hp)),
                   pl.BlockSpec((1, 2, t, 1), lambda hp, i: (hp, 0, i, 0))],
        name=name, compiler_params=_params(2))(q, k, v)


def attn_bwd(name, q, k, v, o, do, lse, t=TILE_ATT):
    s = q.shape[0]
    n_q = s // t

    def kern(q_ref, k_ref, v_ref, o_ref, do_ref, lse_ref, dq_ref, dk_ref, dv_ref):
        qi = pl.program_id(1)

        @pl.when(qi == 0)
        def _():
            dk_ref[...] = jnp.zeros(dk_ref.shape, F32)
            dv_ref[...] = jnp.zeros(dv_ref.shape, F32)

        qs, doms, deltas, lse2 = [], [], [], []
        for j in range(2):
            qs.append(q_ref[:, HD * j:HD * (j + 1)])
            dom = jnp.where(_head_lanes(j), do_ref[...], 0.0)
            deltas.append(jnp.sum(dom * o_ref[...], axis=-1, keepdims=True))
            doms.append(dom.astype(_MXU))
            lse2.append(lse_ref[0, j] * LOG2E)

        def block(k0, dqs, masked):
            vblk = v_ref[pl.ds(k0, t), :]
            kblks = [k_ref[pl.ds(k0, t), HD * j:HD * (j + 1)] for j in range(2)]
            scs = [_raw_scores(qs[j], kblks[j], masked, t) for j in range(2)]
            dps = [lax.dot_general(doms[j], vblk, (((1,), (1,)), ((), ())), preferred_element_type=F32)
                   for j in range(2)]
            ps = [jnp.exp2(scs[j] * EXP2_SCALE - lse2[j]) for j in range(2)]
            dss = [(ps[j] * (dps[j] - deltas[j])).astype(_MXU) for j in range(2)]
            pbs = [ps[j].astype(_MXU) for j in range(2)]
            new = tuple(dqs[j] + jnp.dot(dss[j], kblks[j], preferred_element_type=F32) for j in range(2))
            for j in range(2):
                dk_ref[pl.ds(k0, t), HD * j:HD * (j + 1)] += lax.dot_general(
                    dss[j], qs[j], (((0,), (0,)), ((), ())), preferred_element_type=F32)
            dvs = [lax.dot_general(pbs[j], doms[j], (((0,), (0,)), ((), ())), preferred_element_type=F32)
                   for j in range(2)]
            dv_ref[pl.ds(k0, t), :] += dvs[0] + dvs[1]
            return new

        init = (jnp.zeros((t, HD), F32), jnp.zeros((t, HD), F32))
        dqs = lax.fori_loop(0, qi, lambda kb, c: block(pl.multiple_of(kb * t, t), c, False), init)
        dqs = block(pl.multiple_of(qi * t, t), dqs, True)
        for j in range(2):
            dq_ref[:, HD * j:HD * (j + 1)] = dqs[j] * ATTN_SCALE

        @pl.when(qi == n_q - 1)
        def _():
            dk_ref[...] = dk_ref[...] * ATTN_SCALE

    return pl.pallas_call(
        kern,
        out_shape=[jax.ShapeDtypeStruct((s, H * HD), F32), jax.ShapeDtypeStruct((s, H * HD), F32),
                   jax.ShapeDtypeStruct((s, H * DV), F32)],
        grid=(HP, n_q),
        in_specs=[pl.BlockSpec((t, 2 * HD), lambda hp, i: (i, hp)), pl.BlockSpec((s, 2 * HD), lambda hp, i: (0, hp)),
                  pl.BlockSpec((s, 2 * DV), lambda hp, i: (0, hp)), pl.BlockSpec((t, 2 * DV), lambda hp, i: (i, hp)),
                  pl.BlockSpec((t, 2 * DV), lambda hp, i: (i, hp)),
                  pl.BlockSpec((1, 2, t, 1), lambda hp, i: (hp, 0, i, 0))],
        out_specs=[pl.BlockSpec((t, 2 * HD), lambda hp, i: (i, hp)), pl.BlockSpec((s, 2 * HD), lambda hp, i: (0, hp)),
                   pl.BlockSpec((s, 2 * DV), lambda hp, i: (0, hp))],
        name=name, compiler_params=_params(2))(q, k, v, o, do, lse)


def rope_tables(name, pos_col, inv128):
    s = pos_col.shape[0]

    def kern(p_ref, inv_ref, c_ref, s_ref):
        ang = p_ref[...].astype(F32) * inv_ref[...]
        lane = _lane()
        m_r = (lane >= DN) & (lane < DN + DR)
        c_ref[...] = jnp.where(lane < DN, 1.0, jnp.where(m_r, jnp.cos(ang), 0.0))
        s_ref[...] = jnp.where(m_r, jnp.sin(ang), 0.0)

    return _whole(kern, name, [jax.ShapeDtypeStruct((s, HD), F32)] * 2, pos_col, inv128)


def loss_kernel(name, y, tgt, tile=TILE_ROW):
    def body(row_v, _):
        err = row_v[0] - row_v[1]
        part = 0.5 * jnp.sum(jnp.mean(err * err, axis=-1, keepdims=True), axis=0, keepdims=True)
        return [err * (1.0 / D)], [jnp.broadcast_to(part, (1, 128))]

    return _row_call(name, body, [y, tgt], [], [(D, F32)], [((1, 128), F32)], tile)


def _row_tile(r, c):
    cap = max(8, (1 << 18) // max(c, 1))
    for t in (2048, 1024, 512, 256, 128, 64, 32, 16, 8):
        if t <= cap and r % t == 0:
            return t
    return r


def sum_parts(name, parts):
    n, r, c = parts.shape
    t = _row_tile(r, c)

    def kern(p_ref, o_ref):
        acc = p_ref[0].astype(F32)
        for i in range(1, n):
            acc = acc + p_ref[i].astype(F32)
        o_ref[...] = acc

    return pl.pallas_call(kern, out_shape=jax.ShapeDtypeStruct((r, c), F32), grid=(r // t,),
                          in_specs=[pl.BlockSpec((n, t, c), lambda i: (0, i, 0))],
                          out_specs=pl.BlockSpec((t, c), lambda i: (i, 0)), name=name, compiler_params=_params(1))(parts)


def adamw(name, parts, w, m, v, base=0, stride=0):
    n, _, cp = parts.shape
    nl, r, c = w.shape
    t = _row_tile(math.gcd(math.gcd(r, base), stride), max(c, cp))
    c1 = 1.0 / (1.0 - ADAM_B1 ** ADAM_STEP)
    c2 = 1.0 / (1.0 - ADAM_B2 ** ADAM_STEP)

    def kern(p_ref, w_ref, m_ref, v_ref, g_ref, d_ref, nm_ref, nv_ref):
        g = p_ref[0].astype(F32)
        for i in range(1, n):
            g = g + p_ref[i].astype(F32)
        g = g[:, :c]
        nm = ADAM_B1 * m_ref[...] + (1.0 - ADAM_B1) * g
        nv = ADAM_B2 * v_ref[...] + (1.0 - ADAM_B2) * (g * g)
        g_ref[...] = g
        nm_ref[...] = nm
        nv_ref[...] = nv
        d_ref[...] = -ADAM_LR * ((nm * c1) / (jnp.sqrt(nv * c2) + ADAM_EPS) + ADAM_WD * w_ref[...])

    spec = pl.BlockSpec((None, t, c), lambda l, i: (l, i, 0))
    pspec = pl.BlockSpec((n, t, cp), lambda l, i: (0, (base + l * stride) // t + i, 0))
    return pl.pallas_call(kern, out_shape=[jax.ShapeDtypeStruct((nl, r, c), F32)] * 4, grid=(nl, r // t),
                          in_specs=[pspec, spec, spec, spec], out_specs=[spec] * 4, name=name,
                          compiler_params=_params(2))(parts, w, m, v)


def adamw_layer(name, parts, w, m, v, layer, prev, base=0):
    n, _, cp = parts.shape
    nl, r, c = w.shape
    t = _row_tile(math.gcd(r, base), max(c, cp))
    c1 = 1.0 / (1.0 - ADAM_B1 ** ADAM_STEP)
    c2 = 1.0 / (1.0 - ADAM_B2 ** ADAM_STEP)
    chained = nl > 1

    def kern(p_ref, w_ref, m_ref, v_ref, *rest):
        g_ref, d_ref, nm_ref, nv_ref = rest[-4:]
        g = p_ref[0].astype(F32)
        for i in range(1, n):
            g = g + p_ref[i].astype(F32)
        g = g[:, :c]
        nm = ADAM_B1 * m_ref[...] + (1.0 - ADAM_B1) * g
        nv = ADAM_B2 * v_ref[...] + (1.0 - ADAM_B2) * (g * g)
        g_ref[...] = g
        nm_ref[...] = nm
        nv_ref[...] = nv
        d_ref[...] = -ADAM_LR * ((nm * c1) / (jnp.sqrt(nv * c2) + ADAM_EPS) + ADAM_WD * w_ref[...])

    spec = pl.BlockSpec((None, t, c), lambda i: (layer, i, 0))
    pspec = pl.BlockSpec((n, t, cp), lambda i: (0, base // t + i, 0))
    in_specs = [pspec, spec, spec, spec]
    args = [parts, w, m, v]
    aliases = {}
    if chained:
        if prev is None:
            prev = [lax.empty((nl, r, c), F32) for _ in range(4)]
        in_specs += [pl.BlockSpec(memory_space=pl.ANY)] * 4
        args += list(prev)
        aliases = {4 + i: i for i in range(4)}
    return pl.pallas_call(kern, out_shape=[jax.ShapeDtypeStruct((nl, r, c), F32)] * 4, grid=(r // t,),
                          in_specs=in_specs, out_specs=[spec] * 4, input_output_aliases=aliases, name=name,
                          compiler_params=_params(1))(*args)


def _me():
    return lax.axis_index("x"), lax.axis_index("y"), lax.axis_index("c")


def _flip(x, y, c, mask):
    return (jnp.where((mask >> 2) & 1, 1 - x, x), jnp.where((mask >> 1) & 1, 1 - y, y), jnp.where(mask & 1, 1 - c, c))


def _index(x, y, c):
    return 4 * x + 2 * y + c


def _exchange(name, arr, gather):
    out_shape = (N_DEV,) + arr.shape if gather else arr.shape

    def kern(in_ref, out_ref, send_sems, recv_sems, local_sem):
        x, y, c = _me()
        me = _index(x, y, c)
        mine = pltpu.make_async_copy(in_ref if gather else in_ref.at[me], out_ref.at[me], local_sem)
        mine.start()
        copies = []
        for mask in range(1, N_DEV):
            px, py, pc = _flip(x, y, c, mask)
            peer = _index(px, py, pc)
            cp = pltpu.make_async_remote_copy(
                src_ref=in_ref if gather else in_ref.at[peer], dst_ref=out_ref.at[me],
                send_sem=send_sems.at[mask - 1], recv_sem=recv_sems.at[mask - 1],
                device_id=(px, py, pc), device_id_type=MESH)
            cp.start()
            copies.append((cp, peer))
        for mask, (cp, peer) in enumerate(copies, start=1):
            pltpu.make_async_remote_copy(
                src_ref=in_ref if gather else in_ref.at[peer], dst_ref=out_ref.at[peer],
                send_sem=send_sems.at[mask - 1], recv_sem=recv_sems.at[mask - 1],
                device_id=_flip(x, y, c, mask), device_id_type=MESH).wait_recv()
        for cp, _ in copies:
            cp.wait_send()
        mine.wait()

    any_spec = pl.BlockSpec(memory_space=pl.ANY)
    return pl.pallas_call(
        kern, out_shape=jax.ShapeDtypeStruct(out_shape, arr.dtype), in_specs=[any_spec], out_specs=any_spec,
        scratch_shapes=[pltpu.SemaphoreType.DMA((N_DEV - 1,)), pltpu.SemaphoreType.DMA((N_DEV - 1,)),
                        pltpu.SemaphoreType.DMA],
        name=name, compiler_params=pltpu.CompilerParams(has_side_effects=True))(arr)


def all_gather(name, arr):
    return _exchange(name, arr, True)


def all_to_all(name, arr):
    return _exchange(name, arr, False)


_HBM = pl.BlockSpec(memory_space=pltpu.HBM)
_SEM = pl.BlockSpec(memory_space=pltpu.SEMAPHORE)
_EFFECT = pltpu.SideEffectType.DATAFLOW_SIDE_EFFECTING


def _split_copies(srcs, lands, send_sems, recv_sems, gather):
    x, y, c = _me()
    me = _index(x, y, c)
    out = []
    for a, (src, land) in enumerate(zip(srcs, lands)):
        for mask in range(1, N_DEV):
            px, py, pc = _flip(x, y, c, mask)
            peer = _index(px, py, pc)
            sem = (N_DEV - 1) * a + mask - 1
            mk = lambda dst_slot: pltpu.make_async_remote_copy(
                src_ref=src if gather else src.at[peer], dst_ref=land.at[dst_slot],
                send_sem=send_sems.at[sem], recv_sem=recv_sems.at[sem], device_id=(px, py, pc), device_id_type=MESH)
            out.append((mk(me), mk(peer)))
    return out


def exchange_start(name, arrs, gather, after):
    k = len(arrs)
    land_shapes = [((N_DEV,) + a.shape if gather else a.shape) for a in arrs]

    def body(*refs):
        srcs, lands = refs[:k], refs[k:2 * k]
        send_sems, recv_sems = refs[2 * k + 1], refs[2 * k + 2]
        token = refs[-1]
        for mine, _ in _split_copies(srcs, lands, send_sems, recv_sems, gather):
            mine.start()
        token[...] = jnp.zeros(token.shape, token.dtype)

    n_sem = (N_DEV - 1) * k
    res = pl.pallas_call(
        body, name=name,
        out_shape=(pltpu.SemaphoreType.DMA((n_sem,)), pltpu.SemaphoreType.DMA((n_sem,)),
                   *[pltpu.HBM(a.shape, a.dtype) for a in arrs],
                   *[pltpu.HBM(shp, a.dtype) for shp, a in zip(land_shapes, arrs)],
                   jax.ShapeDtypeStruct((8, 128), F32)),
        in_specs=[_HBM] * (2 * k) + [pl.BlockSpec(memory_space=pl.ANY)],
        out_specs=(_SEM, _SEM, *[_HBM] * (2 * k), pl.BlockSpec(memory_space=pltpu.VMEM)),
        input_output_aliases={i: 2 + i for i in range(2 * k)},
        compiler_params=pltpu.CompilerParams(has_side_effects=_EFFECT),
    )(*[pltpu.with_memory_space_constraint(a, pltpu.HBM) for a in arrs],
      *[pltpu.with_memory_space_constraint(lax.empty(shp, a.dtype), pltpu.HBM) for shp, a in zip(land_shapes, arrs)],
      after)
    return res[0], res[1], list(res[2:2 + k]), list(res[2 + k:2 + 2 * k]), res[-1]


def exchange_wait(name, started, after, gather):
    send_sems, recv_sems, thrus, lands, _ = started
    k = len(thrus)

    def body(*refs):
        srcs, lnds = refs[:k], refs[k:2 * k]
        s_sems, r_sems = refs[2 * k], refs[2 * k + 1]
        for mine, theirs in _split_copies(srcs, lnds, s_sems, r_sems, gather):
            mine.wait_send()
            theirs.wait_recv()

    res = pl.pallas_call(
        body, name=name,
        out_shape=tuple(pltpu.HBM(a.shape, a.dtype) for a in thrus + lands),
        in_specs=[_HBM] * (2 * k) + [_SEM, _SEM, pl.BlockSpec(memory_space=pl.ANY)], out_specs=tuple([_HBM] * (2 * k)),
        input_output_aliases={i: i for i in range(2 * k)},
        compiler_params=pltpu.CompilerParams(has_side_effects=_EFFECT),
    )(*thrus, *lands, send_sems, recv_sems, after)
    return list(res[k:])


def _pad_heads(w, real, padded):
    k = w.shape[0]
    w3 = w.reshape(k, H, real)
    return jnp.pad(w3, ((0, 0), (0, 0), (0, padded - real))).reshape(k, H * padded)


def _unpad_heads(w, real, padded):
    k = w.shape[0]
    return w.reshape(k, H, padded)[:, :, :real].reshape(k, H * real)


def _s5_place(ab_re, ab_im, bb_re_t, bb_im_t, c_re, c_im):
    eye = jnp.eye(GB, dtype=F32)

    def wb_part(bt):
        x4 = bt.reshape(P, NBLK, GB, N).transpose(1, 2, 0, 3)
        return jnp.einsum('kgpn,gh->kgphn', x4, eye).reshape(NBLK, GB * P, HALF)

    def wc_part(cc):
        x4 = cc.reshape(NBLK, GB, P, N)
        return jnp.einsum('kgpn,gh->kgnhp', x4, eye).reshape(NBLK, HALF, GB * P)

    wb = jnp.concatenate([wb_part(bb_re_t), wb_part(bb_im_t)], axis=-1)
    wc = jnp.concatenate([wc_part(c_re), -wc_part(c_im)], axis=1)
    a_tab = jnp.concatenate([ab_re.reshape(NBLK, 1, HALF), ab_im.reshape(NBLK, 1, HALF)], axis=-1)
    return wb.astype(_MXU), wc.astype(_MXU), a_tab


def _s5_unplace(dwb, dwc, da):
    eye = jnp.eye(GB, dtype=F32)

    def wb_part(dpart):
        x5 = dpart.reshape(NBLK, GB, P, GB, N)
        return jnp.einsum('kgphn,gh->kgpn', x5, eye).transpose(2, 0, 1, 3).reshape(P, G * N)

    def wc_part(dpart):
        x5 = dpart.reshape(NBLK, GB, N, GB, P)
        return jnp.einsum('kgnhp,gh->kgpn', x5, eye).reshape(G, P, N)

    dbb_re_t, dbb_im_t = wb_part(dwb[..., :HALF]), wb_part(dwb[..., HALF:])
    dc_re, dc_im = wc_part(dwc[:, :HALF]), -wc_part(dwc[:, HALF:])
    dab_re, dab_im = da[:, :HALF].reshape(1, G * N), da[:, HALF:].reshape(1, G * N)
    return dab_re, dab_im, dbb_re_t, dbb_im_t, dc_re, dc_im


def _row(v):
    return v.reshape(1, -1)


def kernel(x, c, positions, ada_w, ada_b, norm1_g, norm2_g, ffn_w_gate, ffn_w_up, ffn_w_down, s5_lam_re, s5_lam_im, s5_log_dt, s5_b_re, s5_b_im, s5_c_re, s5_c_im, s5_d, s5_w_glu, s5_b_glu, kv_ada_w, kv_ada_b, kv_norm_g, w_kv_a, kv_a_norm_g, w_kv_b, k_nope_norm_g, k_rope_norm_g, mla_w_dq, mla_q_norm_g, mla_w_uq, mla_q_nope_norm_g, mla_q_rope_norm_g, mla_w_o, loss_target, m_ada_w, m_ada_b, m_norm1_g, m_norm2_g, m_ffn_w_gate, m_ffn_w_up, m_ffn_w_down, m_s5_lam_re, m_s5_lam_im, m_s5_log_dt, m_s5_b_re, m_s5_b_im, m_s5_c_re, m_s5_c_im, m_s5_d, m_s5_w_glu, m_s5_b_glu, m_kv_ada_w, m_kv_ada_b, m_kv_norm_g, m_w_kv_a, m_kv_a_norm_g, m_w_kv_b, m_k_nope_norm_g, m_k_rope_norm_g, m_mla_w_dq, m_mla_q_norm_g, m_mla_w_uq, m_mla_q_nope_norm_g, m_mla_q_rope_norm_g, m_mla_w_o, v_ada_w, v_ada_b, v_norm1_g, v_norm2_g, v_ffn_w_gate, v_ffn_w_up, v_ffn_w_down, v_s5_lam_re, v_s5_lam_im, v_s5_log_dt, v_s5_b_re, v_s5_b_im, v_s5_c_re, v_s5_c_im, v_s5_d, v_s5_w_glu, v_s5_b_glu, v_kv_ada_w, v_kv_ada_b, v_kv_norm_g, v_w_kv_a, v_kv_a_norm_g, v_w_kv_b, v_k_nope_norm_g, v_k_rope_norm_g, v_mla_w_dq, v_mla_q_norm_g, v_mla_w_uq, v_mla_q_nope_norm_g, v_mla_q_rope_norm_g, v_mla_w_o):
    W = dict(ada_w=ada_w, ada_b=ada_b, norm1_g=norm1_g, norm2_g=norm2_g, ffn_w_gate=ffn_w_gate, ffn_w_up=ffn_w_up, ffn_w_down=ffn_w_down, s5_lam_re=s5_lam_re, s5_lam_im=s5_lam_im, s5_log_dt=s5_log_dt, s5_b_re=s5_b_re, s5_b_im=s5_b_im, s5_c_re=s5_c_re, s5_c_im=s5_c_im, s5_d=s5_d, s5_w_glu=s5_w_glu, s5_b_glu=s5_b_glu, kv_ada_w=kv_ada_w, kv_ada_b=kv_ada_b, kv_norm_g=kv_norm_g, w_kv_a=w_kv_a, kv_a_norm_g=kv_a_norm_g, w_kv_b=w_kv_b, k_nope_norm_g=k_nope_norm_g, k_rope_norm_g=k_rope_norm_g, mla_w_dq=mla_w_dq, mla_q_norm_g=mla_q_norm_g, mla_w_uq=mla_w_uq, mla_q_nope_norm_g=mla_q_nope_norm_g, mla_q_rope_norm_g=mla_q_rope_norm_g, mla_w_o=mla_w_o)
    M = dict(ada_w=m_ada_w, ada_b=m_ada_b, norm1_g=m_norm1_g, norm2_g=m_norm2_g, ffn_w_gate=m_ffn_w_gate, ffn_w_up=m_ffn_w_up, ffn_w_down=m_ffn_w_down, s5_lam_re=m_s5_lam_re, s5_lam_im=m_s5_lam_im, s5_log_dt=m_s5_log_dt, s5_b_re=m_s5_b_re, s5_b_im=m_s5_b_im, s5_c_re=m_s5_c_re, s5_c_im=m_s5_c_im, s5_d=m_s5_d, s5_w_glu=m_s5_w_glu, s5_b_glu=m_s5_b_glu, kv_ada_w=m_kv_ada_w, kv_ada_b=m_kv_ada_b, kv_norm_g=m_kv_norm_g, w_kv_a=m_w_kv_a, kv_a_norm_g=m_kv_a_norm_g, w_kv_b=m_w_kv_b, k_nope_norm_g=m_k_nope_norm_g, k_rope_norm_g=m_k_rope_norm_g, mla_w_dq=m_mla_w_dq, mla_q_norm_g=m_mla_q_norm_g, mla_w_uq=m_mla_w_uq, mla_q_nope_norm_g=m_mla_q_nope_norm_g, mla_q_rope_norm_g=m_mla_q_rope_norm_g, mla_w_o=m_mla_w_o)
    V = dict(ada_w=v_ada_w, ada_b=v_ada_b, norm1_g=v_norm1_g, norm2_g=v_norm2_g, ffn_w_gate=v_ffn_w_gate, ffn_w_up=v_ffn_w_up, ffn_w_down=v_ffn_w_down, s5_lam_re=v_s5_lam_re, s5_lam_im=v_s5_lam_im, s5_log_dt=v_s5_log_dt, s5_b_re=v_s5_b_re, s5_b_im=v_s5_b_im, s5_c_re=v_s5_c_re, s5_c_im=v_s5_c_im, s5_d=v_s5_d, s5_w_glu=v_s5_w_glu, s5_b_glu=v_s5_b_glu, kv_ada_w=v_kv_ada_w, kv_ada_b=v_kv_ada_b, kv_norm_g=v_kv_norm_g, w_kv_a=v_w_kv_a, kv_a_norm_g=v_kv_a_norm_g, w_kv_b=v_w_kv_b, k_nope_norm_g=v_k_nope_norm_g, k_rope_norm_g=v_k_rope_norm_g, mla_w_dq=v_mla_w_dq, mla_q_norm_g=v_mla_q_norm_g, mla_w_uq=v_mla_w_uq, mla_q_nope_norm_g=v_mla_q_nope_norm_g, mla_q_rope_norm_g=v_mla_q_rope_norm_g, mla_w_o=v_mla_w_o)
    return _step(x[0], c, positions, loss_target[0], W, M, V)


WEIGHT_NAMES = ['ada_w', 'ada_b', 'norm1_g', 'norm2_g', 'ffn_w_gate', 'ffn_w_up', 'ffn_w_down', 's5_lam_re', 's5_lam_im', 's5_log_dt', 's5_b_re', 's5_b_im', 's5_c_re', 's5_c_im', 's5_d', 's5_w_glu', 's5_b_glu', 'kv_ada_w', 'kv_ada_b', 'kv_norm_g', 'w_kv_a', 'kv_a_norm_g', 'w_kv_b', 'k_nope_norm_g', 'k_rope_norm_g', 'mla_w_dq', 'mla_q_norm_g', 'mla_w_uq', 'mla_q_nope_norm_g', 'mla_q_rope_norm_g', 'mla_w_o']
REPLICATED = ['ada_b', 'norm1_g', 'norm2_g', 's5_lam_re', 's5_lam_im', 's5_log_dt', 's5_b_re', 's5_b_im', 's5_c_re', 's5_c_im', 'kv_ada_b', 'kv_norm_g', 'kv_a_norm_g', 'k_nope_norm_g', 'k_rope_norm_g', 'mla_q_norm_g', 'mla_q_nope_norm_g', 'mla_q_rope_norm_g']
SHARDED_VEC = ['s5_d', 's5_b_glu']


def _step(x, c, positions, target, W, M, V):
    s = x.shape[0]
    me = _index(*_me())
    mxu = lambda a: a.astype(_MXU)

    pad_c = lambda a: jnp.pad(a, ((0, 0), (0, FFB - FF // N_DEV)))
    pad_r = lambda a: jnp.pad(a, ((0, FFB - FF // N_DEV), (0, 0)))
    cols = lambda g: g.transpose(1, 0, 2).reshape(g.shape[1], N_DEV * g.shape[2])
    rows = lambda g: g.reshape(N_DEV * g.shape[1], g.shape[2])

    def local_pack(l):
        second = W['s5_w_glu'][l] if l < N_A else W['mla_w_o'][l - N_A]
        arrs = [jnp.concatenate([mxu(pad_c(W['ffn_w_gate'][l])), mxu(pad_c(W['ffn_w_up'][l]))], axis=0),
                jnp.concatenate([mxu(pad_r(W['ffn_w_down'][l])), mxu(second)], axis=0)]
        if l == N_A:
            arrs += [jnp.concatenate([mxu(W['w_kv_b']), mxu(W['mla_w_dq'][0])], axis=0), mxu(W['w_kv_a'])]
        if l > N_A:
            arrs += [mxu(W['mla_w_dq'][l - N_A])]
        if l >= N_A:
            arrs += [mxu(W['mla_w_uq'][l - N_A])]
        return arrs


    def layer_weights(l, after):
        lands = exchange_wait(f"gather_wait_{l}", gathers[l], after, True)
        full = [lax.dynamic_update_slice(ld, src[None], (me,) + (0,) * src.ndim) for ld, src in zip(lands, gathers[l][2])]
        w = {'wg': cols(full[0][:, :D]), 'wu': cols(full[0][:, D:]), 'wd': rows(full[1][:, :FFB]),
             'second': rows(full[1][:, FFB:])}
        if l >= N_A:
            if l == N_A:
                wkvb3 = cols(full[2][:, :KVL]).reshape(KVL, H, DN + DV)
                wkva = rows(full[3])
                w['wa_pad'] = jnp.concatenate([wkva[:, :KVL], jnp.zeros((D, DN), _MXU), wkva[:, KVL:],
                                               jnp.zeros((D, HD - DN - DR), _MXU)], axis=1)
                w['wkn_pad'] = jnp.pad(wkvb3[:, :, :DN], ((0, 0), (0, 0), (0, HD - DN))).reshape(KVL, H * HD)
                w['wv'] = wkvb3[:, :, DN:].reshape(KVL, H * DV)
                w['wdq'] = rows(full[2][:, KVL:])
            else:
                w['wdq'] = rows(full[2])
            w['wuq_pad'] = _pad_heads(cols(full[-1]), DN + DR, HD)
        return w

    vec = jnp.concatenate([c.reshape(-1), W['s5_d'].reshape(-1), W['s5_b_glu'].reshape(-1)]).reshape(1, -1)
    vec = jnp.pad(vec, ((0, 7), (0, 0)))
    gv = all_gather("gather_vectors", vec)[:, 0, :]
    c_all = gv[:, :D]
    d_full = jnp.concatenate([gv[d, D:D + 2 * 128].reshape(N_A, 128) for d in range(N_DEV)], axis=1)
    bglu_full = jnp.concatenate([gv[d, D + 256:D + 512].reshape(N_A, 128) for d in range(N_DEV)], axis=1)

    ca_all = jax.nn.silu(c_all)
    w_mod = jnp.concatenate([W['ada_w'][l] for l in range(DEPTH)] + [W['kv_ada_w']], axis=1)
    n_mod = w_mod.shape[1]
    mod_cols = small_matmul("mod_matmul", ca_all, w_mod)
    gm = all_gather("gather_mod", mod_cols)
    gathers = [exchange_start(f"gather_start_{l}", local_pack(l), True, gm) for l in range(DEPTH)]
    tokens = sum(g[4][0, 0] for g in gathers)
    mine = lax.dynamic_index_in_dim(gm, me, axis=1, keepdims=False) + tokens
    per_l = D * 6 // N_DEV
    mods = []
    for l in range(DEPTH):
        full = jnp.concatenate([mine[d, per_l * l:per_l * (l + 1)] for d in range(N_DEV)]) + W['ada_b'][l]
        mods.append([_row(full[D * i:D * (i + 1)]) for i in range(6)])
    kfull = jnp.concatenate([mine[d, per_l * DEPTH:] for d in range(N_DEV)]) + W['kv_ada_b']
    k_shift, k_scale = _row(kfull[:D]), _row(kfull[D:])

    inv = 1.0 / (ROPE_THETA ** (np.arange(0, DR, 2, dtype=np.float32) / DR))
    inv128 = np.zeros((1, HD), np.float32)
    inv128[0, DN:DN + DR // 2] = inv
    inv128[0, DN + DR // 2:DN + DR] = inv
    cosf, sinf = rope_tables("rope_tables", positions.reshape(s, 1), jnp.asarray(inv128))
    zpad = lambda n: jnp.zeros((n,), F32)
    gkn128 = _row(jnp.concatenate([W['k_nope_norm_g'], zpad(HD - DN)]))
    gkr128 = _row(jnp.concatenate([zpad(DN), W['k_rope_norm_g'], zpad(HD - DN - DR)]))
    gq128 = [_row(jnp.concatenate([W['mla_q_nope_norm_g'][j], W['mla_q_rope_norm_g'][j], zpad(HD - DN - DR)]))
             for j in range(2)]

    expand = jnp.asarray(np.kron(np.eye(G, dtype=np.float32), np.ones((1, N), np.float32)))
    s5_raw, s5_mats = [], []
    for l in range(N_A):
        raw = (_row(W['s5_lam_re'][l]), _row(W['s5_lam_im'][l]), _row(W['s5_log_dt'][l]),
               W['s5_b_re'][l].transpose(2, 0, 1).reshape(P, G * N), W['s5_b_im'][l].transpose(2, 0, 1).reshape(P, G * N))
        ab_re, ab_im, bb_re_t, bb_im_t = s5_prep_fwd(f"s5_prep_fwd", *raw, expand)
        s5_raw.append(raw)
        s5_mats.append(_s5_place(ab_re, ab_im, bb_re_t, bb_im_t, W['s5_c_re'][l], W['s5_c_im'][l]))

    g1 = [_row(W['norm1_g'][l]) for l in range(DEPTH)]
    g2 = [_row(W['norm2_g'][l]) for l in range(DEPTH)]
    saved = []
    xs = x
    kv = None
    lw = [None] * DEPTH
    for l in range(DEPTH):
        sh1, sc1, gt1, sh2, sc2, gt2 = mods[l]
        rec = {'x_in': xs}
        if l >= N_A:
            lw[l] = layer_weights(l, xs)
        if l == N_A:
            kv_smalls = [_row(W['kv_norm_g']), k_shift, k_scale, _row(W['kv_a_norm_g']), gkn128, gkr128]
            kv_w = [lw[l]['wa_pad'], lw[l]['wkn_pad'], lw[l]['wv']]
            k_mat, v_mat = seg_forward("kv_fwd", seg_kv, [xs], kv_smalls, [cosf, sinf], kv_w,
                                       [(H * HD, _MXU), (H * DV, _MXU)], tap_widths=(KVL + HD, H * HD, H * DV))
            kv = {'x_in': xs, 'smalls': kv_smalls, 'k': k_mat, 'v': v_mat, 'w': kv_w}
        if l < N_A:
            (h,) = seg_forward("pre_fwd", seg_pre, [xs], [g1[l], sh1, sc1], [], [], [(D, F32)])
            wb, wc, a_tab = s5_mats[l]
            y, s0 = s5_scan_fwd("s5_scan_fwd", h, wb, wc, a_tab, _row(d_full[l]))
            lw[l] = layer_weights(l, y)
            (x_mid,) = seg_forward("glu_fwd", seg_glu, [xs, y], [gt1, _row(bglu_full[l])], [], [lw[l]['second']],
                                   [(D, F32)], tap_widths=(D,))
            rec.update(h=h, y=y, s0=s0)
        else:
            j = l - N_A
            q_smalls = [g1[l], sh1, sc1, _row(W['mla_q_norm_g'][j]), gq128[j]]
            (q_mat,) = seg_forward("q_fwd", seg_q, [xs], q_smalls, [cosf, sinf], [lw[l]['wdq'], lw[l]['wuq_pad']],
                                   [(H * HD, _MXU)], tap_widths=(QL, H * HD))
            o_mat, lse = attn_fwd("attn_fwd", q_mat, kv['k'], kv['v'])
            (x_mid,) = seg_forward("o_fwd", seg_o, [xs, o_mat], [gt1], [], [lw[l]['second']], [(D, F32)],
                                   tap_widths=(D,))
            rec.update(q=q_mat, o=o_mat, lse=lse, q_smalls=q_smalls)
        rec['x_mid'] = x_mid
        (xs,) = seg_forward("ffn_fwd", seg_ffn, [x_mid], [g2[l], sh2, sc2, gt2], [],
                            [lw[l]['wg'], lw[l]['wu'], lw[l]['wd']], [(D, F32)], tap_widths=(FFP, FFP, D))
        saved.append(rec)

    dy, loss_part = loss_kernel("loss", xs, target)
    loss = lax.psum(loss_part[0, 0], ("x", "y", "c"))

    rblk = lambda a: a.reshape(N_DEV, a.shape[0] // N_DEV, a.shape[1])
    cblk = lambda a: a.reshape(a.shape[0], N_DEV, a.shape[1] // N_DEV).transpose(1, 0, 2)
    dmod = [None] * DEPTH
    dk_tot = []
    dv_tot = []
    dx = dy
    sends = [None] * DEPTH
    send_token = jnp.zeros((1, 1), F32)
    g_n1 = [None] * DEPTH
    g_n2 = [None] * DEPTH
    g_bglu = [None] * N_A
    g_dskip = [None] * N_A
    g_s5 = [None] * N_A
    g_qn, g_q128 = [None] * 2, [None] * 2
    for l in range(DEPTH - 1, -1, -1):
        rec = saved[l]
        sh1, sc1, gt1, sh2, sc2, gt2 = mods[l]
        dx, dgate, dup, dyd, h_b, a_b, dg2, dsh2, dsc2, dgt2 = ffn_backward(
            "ffn_bwd", rec['x_mid'], dx, g2[l], sh2, sc2, gt2 + send_token, lw[l]['wg'], lw[l]['wu'], lw[l]['wd'])
        out_l = [matmul_tn("tn_ffn_in", h_b, dgate, _MXU, col_blocks=N_DEV),
                 matmul_tn("tn_ffn_in", h_b, dup, _MXU, col_blocks=N_DEV),
                 matmul_tn("tn_ffn_out", a_b, dyd, _MXU).reshape(N_DEV, FFB, D)]
        g_n2[l] = dg2
        if l < N_A:
            (dx, dyy), (dz,), (g_b,), (dgt1, dbg) = seg_backward(
                "glu_bwd", seg_glu, [rec['x_in'], rec['y']], [gt1, _row(bglu_full[l])], [], [lw[l]['second']],
                [dx], (D,), (D,))
            out_l.append(rblk(matmul_tn("tn_sq", g_b, dz, _MXU)))
            g_bglu[l] = dbg
            wb, wc, a_tab = s5_mats[l]
            dh, dwb, dwc, da, dd = s5_scan_bwd("s5_scan_bwd", rec['h'], dyy, rec['s0'], wb, wc, a_tab, _row(d_full[l]))
            g_dskip[l] = dd
            dab_re, dab_im, dbb_re_t, dbb_im_t, dc_re, dc_im = _s5_unplace(dwb, dwc, da)
            dlr, dli, dldt, dbr_t, dbi_t = s5_prep_bwd("s5_prep_bwd", *s5_raw[l], expand,
                                                       (dab_re, dab_im, dbb_re_t, dbb_im_t))
            g_s5[l] = (dlr.reshape(G, N), dli.reshape(G, N), dldt.reshape(G),
                       dbr_t.reshape(P, G, N).transpose(1, 2, 0), dbi_t.reshape(P, G, N).transpose(1, 2, 0), dc_re, dc_im)
            (dx,), _, _, (dg1, dsh1, dsc1) = seg_backward(
                "pre_bwd", seg_pre, [rec['x_in']], [g1[l], sh1, sc1], [], [], [dh], (), (), dx_add=dx)
        else:
            j = l - N_A
            (dx, do), (dzo,), (o_b,), (dgt1,) = seg_backward(
                "o_bwd", seg_o, [rec['x_in'], rec['o']], [gt1], [], [lw[l]['second']], [dx], (D,), (D,))
            out_l.append(rblk(matmul_tn("tn_sq", o_b, dzo, _MXU)))
            dq, dk, dv = attn_bwd("attn_bwd", rec['q'], kv['k'], kv['v'], rec['o'], do, rec['lse'])
            dk_tot.append(dk)
            dv_tot.append(dv)
            (dx,), (dql, dqq), (hq_b, qn_b), (dg1, dsh1, dsc1, dqg, dq128) = seg_backward(
                "q_bwd", seg_q, [rec['x_in']], rec['q_smalls'], [cosf, sinf], [lw[l]['wdq'], lw[l]['wuq_pad']],
                [dq], (QL, H * HD), (D, QL), dx_add=dx)
            g_dq = rblk(matmul_tn("tn_dq", hq_b, dql, _MXU))
            g_uq = cblk(_unpad_heads(matmul_tn("tn_uq", qn_b, dqq, _MXU), DN + DR, HD))
            g_qn[j], g_q128[j] = dqg, dq128
        g_n1[l] = dg1
        dmod[l] = jnp.concatenate([dsh1, dsc1, dgt1, dsh2, dsc2, dgt2], axis=1)
        if l == N_A:
            dkk = sum_parts("sum_dk", jnp.stack(dk_tot))
            dvv = sum_parts("sum_dv", jnp.stack(dv_tot))
            (dx,), (dta, dtk, dtv), (hk_b, ckv_b), (dkg, dksh, dksc, dag, dgkn, dgkr) = seg_backward(
                "kv_bwd", seg_kv, [kv['x_in']], kv['smalls'], [cosf, sinf], kv['w'],
                [dkk, dvv], (KVL + HD, H * HD, H * DV), (D, KVL), dx_add=dx)
            g_wa = matmul_tn("tn_kva", hk_b, dta, _MXU)
            g_wa = jnp.concatenate([g_wa[:, :KVL], g_wa[:, KVL + DN:KVL + DN + DR]], axis=1)
            g_kn = matmul_tn("tn_kn", ckv_b, dtk, _MXU).reshape(KVL, H, HD)[:, :, :DN]
            g_v = matmul_tn("tn_v", ckv_b, dtv, _MXU).reshape(KVL, H, DV)
            g_wkvb = jnp.concatenate([g_kn, g_v], axis=2).reshape(KVL, H * (DN + DV))
            dkmod = jnp.concatenate([dksh, dksc], axis=1)
            out_l += [jnp.concatenate([cblk(g_wkvb), g_dq], axis=1), rblk(g_wa)]
        if l > N_A:
            out_l.append(g_dq)
        if l >= N_A:
            out_l.append(g_uq)
        if l > 0:
            sends[l] = exchange_start(f"a2a_start_{l}", out_l, False, dx)
            send_token = sends[l][4][0:1, 0:1]
    grad_x = dx

    dm = jnp.concatenate(dmod + [dkmod], axis=1)[0]
    per_dev = []
    for d in range(N_DEV):
        cols = [dm[6 * D * l + per_l * d:6 * D * l + per_l * (d + 1)] for l in range(DEPTH)]
        cols.append(dm[6 * D * DEPTH + (2 * D // N_DEV) * d:6 * D * DEPTH + (2 * D // N_DEV) * (d + 1)])
        per_dev.append(jnp.concatenate(cols))
    dm_dev = jnp.stack(per_dev)
    gdm = all_gather("gather_dmod", dm_dev)
    dm_mine = lax.dynamic_index_in_dim(gdm, me, axis=1, keepdims=False)
    g_wmod = small_matmul_tn("dmod_matmul", ca_all, dm_mine)
    g_ada_w = jnp.stack([g_wmod[:, per_l * l:per_l * (l + 1)] for l in range(DEPTH)])
    g_kv_ada_w = g_wmod[:, per_l * DEPTH:]
    dm_sum = sum_parts("sum_dmod", gdm.reshape(N_DEV, N_DEV, n_mod))
    g_ada_b = jnp.stack([jnp.concatenate([dm_sum[d, per_l * l:per_l * (l + 1)] for d in range(N_DEV)])
                         for l in range(DEPTH)])
    g_kv_ada_b = jnp.concatenate([dm_sum[d, per_l * DEPTH:] for d in range(N_DEV)])

    small = {
        'norm1_g': jnp.concatenate(g_n1, axis=0), 'norm2_g': jnp.concatenate(g_n2, axis=0),
        's5_lam_re': jnp.stack([g_s5[l][0] for l in range(N_A)]), 's5_lam_im': jnp.stack([g_s5[l][1] for l in range(N_A)]),
        's5_log_dt': jnp.stack([g_s5[l][2] for l in range(N_A)]),
        's5_b_re': jnp.stack([g_s5[l][3] for l in range(N_A)]), 's5_b_im': jnp.stack([g_s5[l][4] for l in range(N_A)]),
        's5_c_re': jnp.stack([g_s5[l][5] for l in range(N_A)]), 's5_c_im': jnp.stack([g_s5[l][6] for l in range(N_A)]),
        'kv_norm_g': dkg, 'kv_a_norm_g': dag, 'k_nope_norm_g': dgkn[:, :DN], 'k_rope_norm_g': dgkr[:, DN:DN + DR],
        'mla_q_norm_g': jnp.concatenate(g_qn, axis=0),
        'mla_q_nope_norm_g': jnp.concatenate([g[:, :DN] for g in g_q128], axis=0),
        'mla_q_rope_norm_g': jnp.concatenate([g[:, DN:DN + DR] for g in g_q128], axis=0),
        's5_d': jnp.concatenate(g_dskip, axis=0), 's5_b_glu': jnp.concatenate(g_bglu, axis=0),
    }
    small_names = [n for n in REPLICATED if n not in ('ada_b', 'kv_ada_b')] + SHARDED_VEC
    flat_small = jnp.concatenate([small[n].reshape(-1) for n in small_names])
    n_small = int(flat_small.shape[0])
    pad_small = -(-n_small // 65536) * 65536
    flat_small = jnp.pad(flat_small, (0, pad_small - n_small)).reshape(pad_small // 128, 128)
    g_small_sum = sum_parts("sum_small", all_gather("gather_small", flat_small)).reshape(-1)
    sends[0] = exchange_start("a2a_start_0", out_l, False, g_small_sum)
    g_small_sum = g_small_sum + sends[0][4][0, 0]
    grads = {}
    off = 0
    for n in small_names:
        size = int(np.prod(small[n].shape))
        full = g_small_sum[off:off + size]
        off += size
        if n in SHARDED_VEC:
            full = lax.dynamic_slice_in_dim(full.reshape(N_A, D), me * (D // N_DEV), D // N_DEV, axis=1)
        grads[n] = full.reshape(W[n].shape)
    grads['ada_b'] = g_ada_b
    grads['kv_ada_b'] = g_kv_ada_b

    packed_names = REPLICATED + SHARDED_VEC

    def pack(dct):
        flat_ = jnp.concatenate([dct[n].reshape(-1) for n in packed_names])
        n_ = int(flat_.shape[0])
        p_ = -(-n_ // 65536) * 65536
        return jnp.pad(flat_, (0, p_ - n_)).reshape(p_ // 128, 128)

    _, d_p, m_p, v_p = adamw("adamw_small", pack(grads)[None], pack(W)[None], pack(M)[None], pack(V)[None])
    out_delta, out_m, out_v = {}, {}, {}
    off = 0
    d_p, m_p, v_p = d_p.reshape(-1), m_p.reshape(-1), v_p.reshape(-1)
    for n in packed_names:
        size = int(np.prod(W[n].shape))
        out_delta[n] = d_p[off:off + size].reshape(W[n].shape)
        out_m[n] = m_p[off:off + size].reshape(W[n].shape)
        out_v[n] = v_p[off:off + size].reshape(W[n].shape)
        off += size

    def update(name, parts, base=0, stride=0):
        shp = W[name].shape
        shp3 = shp if len(shp) == 3 else (1,) + shp
        res = adamw("adamw_" + name, parts, W[name].reshape(shp3), M[name].reshape(shp3), V[name].reshape(shp3),
                    base, stride)
        grads[name], out_delta[name], out_m[name], out_v[name] = (a.reshape(shp) for a in res)

    update('ada_w', g_ada_w.reshape(1, DEPTH * D, per_l), 0, D)
    update('kv_ada_w', g_kv_ada_w[None])

    chains = {}

    def update_layer(name, parts, layer, base=0):
        shp = W[name].shape
        shp3 = shp if len(shp) == 3 else (1,) + shp
        chains[name] = adamw_layer(f"adamw_{name}_{layer}", parts, W[name].reshape(shp3), M[name].reshape(shp3),
                                   V[name].reshape(shp3), layer, chains.get(name), base)
        grads[name], out_delta[name], out_m[name], out_v[name] = (a.reshape(shp) for a in chains[name])

    for l in range(DEPTH):
        lands = exchange_wait(f"a2a_wait_{l}", sends[l], d_p, False)
        recv = [lax.dynamic_update_slice(ld, lax.dynamic_index_in_dim(src, me, 0, keepdims=True), (me,) + (0,) * (src.ndim - 1))
                for ld, src in zip(lands, sends[l][2])]
        update_layer('ffn_w_gate', recv[0], l)
        update_layer('ffn_w_up', recv[1], l)
        update_layer('ffn_w_down', recv[2], l)
        if l < N_A:
            update_layer('s5_w_glu', recv[3], l)
        else:
            update_layer('mla_w_o', recv[3], l - N_A)
            if l == N_A:
                update_layer('w_kv_b', recv[4], 0)
                update_layer('mla_w_dq', recv[4], 0, KVL)
                update_layer('w_kv_a', recv[5], 0)
            else:
                update_layer('mla_w_dq', recv[4], l - N_A)
            update_layer('mla_w_uq', recv[-1], l - N_A)

    return (loss, grad_x[None], *[grads[n] for n in WEIGHT_NAMES], *[out_delta[n] for n in WEIGHT_NAMES],
            *[out_m[n] for n in WEIGHT_NAMES], *[out_v[n] for n in WEIGHT_NAMES])
```

```python
import functools
import math

import numpy as np
import jax
import jax.numpy as jnp
from jax import lax
from jax.experimental import pallas as pl
from jax.experimental.pallas import tpu as pltpu

F32 = jnp.float32
_MXU = jnp.bfloat16
HI = lax.Precision.HIGHEST

D = 1024
DEPTH = 4
N_A = 2
FF = 2816
FFB = 384
FFP = 8 * FFB
N_DEV = 8
G = 64
P = 16
N = 64
GB = 8
NBLK = G // GB
HALF = GB * N
H = 16
HP = H // 2
DN, DR, DV = 64, 32, 64
HD = 128
QL = 256
KVL = 256
CHUNK = 64
ROPE_THETA = 10000.0
ATTN_SCALE = 1.0 / math.sqrt(DN + DR)
LOG2E = 1.4426950408889634
EXP2_SCALE = ATTN_SCALE * LOG2E
EPS = 1e-6
ADAM_LR, ADAM_B1, ADAM_B2, ADAM_EPS, ADAM_WD, ADAM_STEP = 0.001, 0.9, 0.999, 1e-08, 0.01, 10
VMEM_LIMIT = 56 * 1024 * 1024
MESH = pl.DeviceIdType.MESH

TILE_ROW = 256
TILE_ATT = 512
TILE_ATT_FWD = 512
TILE_ATT_KEYS = 512
TILE_SCAN = 512


def _params(n_grid):
    return pltpu.CompilerParams(dimension_semantics=("arbitrary",) * n_grid, vmem_limit_bytes=VMEM_LIMIT)


@jax.custom_vjp
def mm(a, w):
    return jnp.dot(a.astype(_MXU), w, preferred_element_type=F32)


def _mm_fwd(a, w):
    return mm(a, w), w


def _mm_bwd(w, g):
    da = lax.dot_general(g.astype(_MXU), w, (((1,), (1,)), ((), ())), preferred_element_type=F32)
    return da, jnp.zeros_like(w)


mm.defvjp(_mm_fwd, _mm_bwd)


def rms(x, g):
    return x * lax.rsqrt(jnp.mean(x * x, axis=-1, keepdims=True) + EPS) * g


def modulate(h, shift, scale):
    return h * (1.0 + scale) + shift


def _lane(n=HD):
    return lax.broadcasted_iota(jnp.int32, (1, n), 1)


@jax.custom_vjp
def rot_half(x):
    lane = _lane()
    first = (lane >= DN) & (lane < DN + DR // 2)
    second = (lane >= DN + DR // 2) & (lane < DN + DR)
    return jnp.where(first, -pltpu.roll(x, HD - DR // 2, 1), jnp.where(second, pltpu.roll(x, DR // 2, 1), 0.0))


rot_half.defvjp(lambda x: (rot_half(x), None), lambda _, g: (-rot_half(g),))


def head_norm_rope(xh, g128, cosf, sinf, with_nope):
    lane = _lane()
    m_n = lane < DN
    m_r = (lane >= DN) & (lane < DN + DR)
    sq = xh * xh
    inv_r = lax.rsqrt(jnp.sum(jnp.where(m_r, sq, 0.0), axis=-1, keepdims=True) / DR + EPS)
    if with_nope:
        inv_n = lax.rsqrt(jnp.sum(jnp.where(m_n, sq, 0.0), axis=-1, keepdims=True) / DN + EPS)
        inv = jnp.where(m_n, inv_n, jnp.where(m_r, inv_r, 0.0))
    else:
        inv = jnp.where(m_r, inv_r, 0.0)
    xg = xh * inv * g128
    return xg * cosf + rot_half(xg) * sinf


def seg_pre(x, g, sh, sc):
    return (modulate(rms(x, g), sh, sc),), ()


def seg_ffn(x, g, sh, sc, gt, t_g, t_u, t_d, wg, wu, wd):
    h = modulate(rms(x, g), sh, sc)
    gate = mm(h, wg) + t_g
    up = mm(h, wu) + t_u
    a = jax.nn.silu(gate) * up
    y = mm(a, wd) + t_d
    return (x + gt * y,), (h.astype(_MXU), a.astype(_MXU))


def seg_glu(x, y, gt, b, t_z, w):
    g = jax.nn.gelu(y)
    z = mm(g, w) + b + t_z
    return (x + gt * (g * jax.nn.sigmoid(z)),), (g.astype(_MXU),)


def seg_o(x, o, gt, t_o, w):
    return (x + gt * (mm(o, w) + t_o),), (o.astype(_MXU),)


def seg_q(x, g, sh, sc, qg, g128, t_l, t_q, cosf, sinf, wdq, wuq):
    h = modulate(rms(x, g), sh, sc)
    ql = mm(h, wdq) + t_l
    qn = rms(ql, qg)
    q = mm(qn, wuq) + t_q
    heads = [head_norm_rope(q[:, HD * i:HD * (i + 1)], g128, cosf, sinf, True) for i in range(H)]
    return (jnp.concatenate(heads, axis=1),), (h.astype(_MXU), qn.astype(_MXU))


def seg_kv(x, g, sh, sc, ag, gkn, gkr, t_a, t_k, t_v, cosf, sinf, wa, wkn, wv):
    hk = modulate(rms(x, g), sh, sc)
    kva = mm(hk, wa) + t_a
    ckv = rms(kva[:, :KVL], ag)
    kr = head_norm_rope(kva[:, KVL:KVL + HD], gkr, cosf, sinf, False)
    kn = mm(ckv, wkn) + t_k
    v = mm(ckv, wv) + t_v
    heads = []
    for i in range(H):
        kh = kn[:, HD * i:HD * (i + 1)]
        inv = lax.rsqrt(jnp.sum(kh * kh, axis=-1, keepdims=True) / DN + EPS)
        heads.append(kh * inv * gkn + kr)
    return (jnp.concatenate(heads, axis=1), v), (hk.astype(_MXU), ckv.astype(_MXU))


def _row_call(name, body_fn, rows, fulls, out_rows, out_accs, tile):
    s = rows[0].shape[0]
    n_tiles = s // tile
    n_rows, n_fulls, n_or, n_oa = len(rows), len(fulls), len(out_rows), len(out_accs)

    def kern(*refs):
        i = pl.program_id(0)
        row_v = [r[...] for r in refs[:n_rows]]
        full_v = [r[...] for r in refs[n_rows:n_rows + n_fulls]]
        o_refs = refs[n_rows + n_fulls:]
        ro, ao = body_fn(row_v, full_v)
        for r, v in zip(o_refs[:n_or], ro):
            r[...] = v.astype(r.dtype)
        if n_oa:
            @pl.when(i == 0)
            def _():
                for r in o_refs[n_or:]:
                    r[...] = jnp.zeros(r.shape, r.dtype)
            for r, v in zip(o_refs[n_or:], ao):
                r[...] += v.astype(r.dtype)

    in_specs = [pl.BlockSpec((tile, a.shape[1]), lambda i: (i, 0)) for a in rows]
    for a in fulls:
        big = a.size * a.dtype.itemsize > (1 << 20)
        nd = a.ndim
        in_specs.append(pl.BlockSpec(a.shape, functools.partial(lambda i, nd_: (0,) * nd_, nd_=nd),
                                     **({"pipeline_mode": pl.Buffered(1)} if big else {})))
    out_shape = [jax.ShapeDtypeStruct((s, w), dt) for w, dt in out_rows]
    out_shape += [jax.ShapeDtypeStruct(shp, dt) for shp, dt in out_accs]
    out_specs = [pl.BlockSpec((tile, w), lambda i: (i, 0)) for w, _ in out_rows]
    out_specs += [pl.BlockSpec(shp, functools.partial(lambda i, nd_: (0,) * nd_, nd_=len(shp))) for shp, _ in out_accs]
    res = pl.pallas_call(kern, out_shape=out_shape, grid=(n_tiles,), in_specs=in_specs, out_specs=out_specs,
                         name=name, compiler_params=_params(1))(*rows, *fulls)
    return list(res)


def seg_forward(name, seg, rows, smalls, consts_rows, consts_full, out_widths, tile=TILE_ROW, tap_widths=()):
    n_r, n_s, n_cr = len(rows), len(smalls), len(consts_rows)

    def body(row_v, full_v):
        t = row_v[0].shape[0]
        taps = [jnp.zeros((t, w), F32) for w in tap_widths]
        outs, _ = seg(*row_v[:n_r], *full_v[:n_s], *taps, *row_v[n_r:], *full_v[n_s:])
        return outs, ()

    return _row_call(name, body, list(rows) + list(consts_rows), list(smalls) + list(consts_full),
                     out_widths, [], tile)


def seg_backward(name, seg, rows, smalls, consts_rows, consts_full, cots, tap_widths, aux_widths,
                 dx_add=None, tile=TILE_ROW):
    n_r, n_s, n_cr, n_c = len(rows), len(smalls), len(consts_rows), len(cots)
    has_add = dx_add is not None

    def body(row_v, full_v):
        t = row_v[0].shape[0]
        prim_rows = row_v[:n_r]
        c_rows = row_v[n_r:n_r + n_cr]
        cot_v = row_v[n_r + n_cr:n_r + n_cr + n_c]
        add_v = row_v[n_r + n_cr + n_c] if has_add else None
        small_v = full_v[:n_s]
        c_full = full_v[n_s:]
        taps = [jnp.zeros((t, w), F32) for w in tap_widths]

        def f(*args):
            return seg(*args, *c_rows, *c_full)

        _, vjp_fn, aux = jax.vjp(f, *prim_rows, *small_v, *taps, has_aux=True)
        grads = vjp_fn(tuple(c.astype(F32) for c in cot_v))
        d_rows = list(grads[:n_r])
        if has_add:
            d_rows[0] = d_rows[0] + add_v
        d_small = grads[n_r:n_r + n_s]
        d_taps = grads[n_r + n_s:]
        return d_rows + list(d_taps) + list(aux), [jnp.sum(g, axis=0, keepdims=True) if g.shape[0] != 1 else g
                                                   for g in d_small]

    all_rows = list(rows) + list(consts_rows) + list(cots) + ([dx_add] if has_add else [])
    out_rows = [(a.shape[1], F32) for a in rows] + [(w, _MXU) for w in tap_widths] + [(w, _MXU) for w in aux_widths]
    out_accs = [((1, a.shape[1]), F32) for a in smalls]
    res = _row_call(name, body, all_rows, list(smalls) + list(consts_full), out_rows, out_accs, tile)
    n_t, n_a = len(tap_widths), len(aux_widths)
    return res[:n_r], res[n_r:n_r + n_t], res[n_r + n_t:n_r + n_t + n_a], res[n_r + n_t + n_a:]


def _split(n):
    if n <= 1024:
        return n
    for t in (1408, 1024, 768, 512, 256, 128):
        if n % t == 0:
            return t
    raise ValueError(n)


def matmul_tn(name, a, b, out_dtype, col_blocks=None):
    s, k1 = a.shape
    _, k2 = b.shape
    tm, ts = _split(k1), 512
    if col_blocks is None:
        tn, per_step, wblk = _split(k2), 1, None
    else:
        wblk = k2 // col_blocks
        per_step = max(1, min(col_blocks, 1536 // wblk))
        tn = per_step * wblk
    n_s = s // ts

    def kern(a_ref, b_ref, o_ref, acc_ref):
        k = pl.program_id(2)

        @pl.when(k == 0)
        def _():
            acc_ref[...] = jnp.zeros(acc_ref.shape, F32)

        acc_ref[...] += lax.dot_general(a_ref[...], b_ref[...], (((0,), (0,)), ((), ())),
                                        preferred_element_type=F32)

        @pl.when(k == n_s - 1)
        def _():
            if col_blocks is None:
                o_ref[...] = acc_ref[...].astype(o_ref.dtype)
            else:
                for cb in range(per_step):
                    o_ref[cb] = acc_ref[:, wblk * cb:wblk * (cb + 1)].astype(o_ref.dtype)

    if col_blocks is None:
        out_shape = jax.ShapeDtypeStruct((k1, k2), out_dtype)
        out_spec = pl.BlockSpec((tm, tn), lambda i, j, k: (i, j))
    else:
        out_shape = jax.ShapeDtypeStruct((col_blocks, k1, wblk), out_dtype)
        out_spec = pl.BlockSpec((per_step, tm, wblk), lambda i, j, k: (j, i, 0))
    return pl.pallas_call(
        kern, out_shape=out_shape, grid=(k1 // tm, k2 // tn, n_s),
        in_specs=[pl.BlockSpec((ts, tm), lambda i, j, k: (k, i)), pl.BlockSpec((ts, tn), lambda i, j, k: (k, j))],
        out_specs=out_spec,
        scratch_shapes=[pltpu.VMEM((tm, tn), F32)], name=name, compiler_params=_params(3))(a, b)


def ffn_backward(name, x, dxo, g, sh, sc, gt, wg, wu, wd, tile=TILE_ROW):
    s = x.shape[0]
    blk = 2 * FFB
    n_blk = wg.shape[1] // blk

    def kern(x_ref, dxo_ref, g_ref, sh_ref, sc_ref, gt_ref, wg_ref, wu_ref, wd_ref,
             dx_ref, dg_ref, du_ref, dy_ref, h_ref, a_ref, dgn_ref, dsh_ref, dsc_ref, dgt_ref):
        i = pl.program_id(0)

        @pl.when(i == 0)
        def _():
            for r in (dgn_ref, dsh_ref, dsc_ref, dgt_ref):
                r[...] = jnp.zeros(r.shape, F32)

        dxo = dxo_ref[...]
        h, pre_vjp = jax.vjp(lambda *p: modulate(rms(p[0], p[1]), p[2], p[3]), x_ref[...], g_ref[...], sh_ref[...],
                             sc_ref[...])
        hb = h.astype(_MXU)
        h_ref[...] = hb
        dyb = (gt_ref[...] * dxo).astype(_MXU)
        dy_ref[...] = dyb
        y = jnp.zeros((tile, D), F32)
        dh = jnp.zeros((tile, D), F32)
        tr = (((1,), (1,)), ((), ()))
        for c in range(n_blk):
            cs = slice(blk * c, blk * (c + 1))
            gate = jnp.dot(hb, wg_ref[:, cs], preferred_element_type=F32)
            up = jnp.dot(hb, wu_ref[:, cs], preferred_element_type=F32)
            sig = jax.nn.sigmoid(gate)
            sl = gate * sig
            ab = (sl * up).astype(_MXU)
            a_ref[:, cs] = ab
            y = y + jnp.dot(ab, wd_ref[cs, :], preferred_element_type=F32)
            da = lax.dot_general(dyb, wd_ref[cs, :], tr, preferred_element_type=F32)
            dgb = (da * up * (sig * (1.0 + gate * (1.0 - sig)))).astype(_MXU)
            dub = (da * sl).astype(_MXU)
            dg_ref[:, cs] = dgb
            du_ref[:, cs] = dub
            dh = dh + lax.dot_general(dgb, wg_ref[:, cs], tr, preferred_element_type=F32) \
                + lax.dot_general(dub, wu_ref[:, cs], tr, preferred_element_type=F32)
        dgt_ref[...] += jnp.sum(dxo * y, axis=0, keepdims=True)
        dx_pre, dgn, dsh, dsc = pre_vjp(dh)
        dx_ref[...] = dxo + dx_pre
        dgn_ref[...] += dgn
        dsh_ref[...] += dsh
        dsc_ref[...] += dsc

    row = lambda w: pl.BlockSpec((tile, w), lambda i: (i, 0))
    vec = pl.BlockSpec((1, D), lambda i: (0, 0))
    wspec = lambda a: pl.BlockSpec(a.shape, lambda i: (0, 0), pipeline_mode=pl.Buffered(1))
    rows_out = [(D, F32), (wg.shape[1], _MXU), (wg.shape[1], _MXU), (D, _MXU), (D, _MXU), (wg.shape[1], _MXU)]
    res = pl.pallas_call(
        kern,
        out_shape=[jax.ShapeDtypeStruct((s, w), dt) for w, dt in rows_out] + [jax.ShapeDtypeStruct((1, D), F32)] * 4,
        grid=(s // tile,),
        in_specs=[row(D), row(D), vec, vec, vec, vec, wspec(wg), wspec(wu), wspec(wd)],
        out_specs=[row(w) for w, _ in rows_out] + [vec] * 4,
        name=name, compiler_params=_params(1))(x, dxo, g, sh, sc, gt, wg, wu, wd)
    return res


def small_matmul(name, a, w, tn=256):
    m, k = a.shape
    n = w.shape[1]

    def kern(a_ref, w_ref, o_ref):
        o_ref[...] = jnp.dot(a_ref[...].astype(_MXU), w_ref[...].astype(_MXU), preferred_element_type=F32)

    return pl.pallas_call(kern, out_shape=jax.ShapeDtypeStruct((m, n), F32), grid=(n // tn,),
                          in_specs=[pl.BlockSpec((m, k), lambda j: (0, 0)), pl.BlockSpec((k, tn), lambda j: (0, j))],
                          out_specs=pl.BlockSpec((m, tn), lambda j: (0, j)), name=name,
                          compiler_params=_params(1))(a, w)


def small_matmul_tn(name, a, b, tn=256):
    m, k = a.shape
    n = b.shape[1]

    def kern(a_ref, b_ref, o_ref):
        o_ref[...] = lax.dot_general(a_ref[...].astype(_MXU), b_ref[...].astype(_MXU), (((0,), (0,)), ((), ())),
                                     preferred_element_type=F32)

    return pl.pallas_call(kern, out_shape=jax.ShapeDtypeStruct((k, n), F32), grid=(n // tn,),
                          in_specs=[pl.BlockSpec((m, k), lambda j: (0, 0)), pl.BlockSpec((m, tn), lambda j: (0, j))],
                          out_specs=pl.BlockSpec((k, tn), lambda j: (0, j)), name=name,
                          compiler_params=_params(1))(a, b)


def _s5_prep_math(lam_re, lam_im, log_dt, b_re_t, b_im_t, expand):
    dt = jnp.dot(jnp.exp(log_dt), expand, precision=HI, preferred_element_type=F32)
    mag = jnp.exp(lam_re * dt)
    ab_re = mag * jnp.cos(lam_im * dt)
    ab_im = mag * jnp.sin(lam_im * dt)
    den = lam_re * lam_re + lam_im * lam_im
    nr = ab_re - 1.0
    ni = ab_im
    f_re = (nr * lam_re + ni * lam_im) / den
    f_im = (ni * lam_re - nr * lam_im) / den
    bb_re = f_re * b_re_t - f_im * b_im_t
    bb_im = f_re * b_im_t + f_im * b_re_t
    return ab_re, ab_im, bb_re, bb_im


def _whole(kern, name, out_shape, *args):
    return pl.pallas_call(kern, out_shape=out_shape, name=name,
                          compiler_params=pltpu.CompilerParams(vmem_limit_bytes=VMEM_LIMIT))(*args)


def s5_prep_fwd(name, lam_re, lam_im, log_dt, b_re_t, b_im_t, expand):
    def kern(a, b, c, d, e, f, o0, o1, o2, o3):
        r = _s5_prep_math(a[...], b[...], c[...], d[...], e[...], f[...])
        for o, v in zip((o0, o1, o2, o3), r):
            o[...] = v

    gn = lam_re.shape[1]
    shp = [jax.ShapeDtypeStruct((1, gn), F32)] * 2 + [jax.ShapeDtypeStruct((P, gn), F32)] * 2
    return _whole(kern, name, shp, lam_re, lam_im, log_dt, b_re_t, b_im_t, expand)


def s5_prep_bwd(name, lam_re, lam_im, log_dt, b_re_t, b_im_t, expand, cots):
    def kern(a, b, c, d, e, f, c0, c1, c2, c3, o0, o1, o2, o3, o4):
        ex = f[...]
        _, vjp_fn = jax.vjp(lambda *p: _s5_prep_math(*p, ex), a[...], b[...], c[...], d[...], e[...])
        g = vjp_fn((c0[...], c1[...], c2[...], c3[...]))
        for o, v in zip((o0, o1, o2, o3, o4), g):
            o[...] = v

    shp = [jax.ShapeDtypeStruct(a.shape, F32) for a in (lam_re, lam_im, log_dt, b_re_t, b_im_t)]
    return _whole(kern, name, shp, lam_re, lam_im, log_dt, b_re_t, b_im_t, expand, *cots)


def _cpowers(ar, ai):
    pw = [(ar, ai)]
    for _ in range(7):
        pr, pi = pw[-1]
        pw.append((pr * ar - pi * ai, pr * ai + pi * ar))
    return pw


def _row_select(row, values):
    out = jnp.broadcast_to(values[7], (8, values[7].shape[1]))
    for r in range(6, -1, -1):
        out = jnp.where(row == r, values[r], out)
    return out


def _scan_tables(ar, ai, reverse):
    pw = _cpowers(ar, ai)
    row = lax.broadcasted_iota(jnp.int32, (8, ar.shape[1]), 0)
    steps = []
    for d in (1, 2, 4):
        keep = (row <= 7 - d) if reverse else (row >= d)
        steps.append((jnp.where(keep, pw[d - 1][0], 0.0), jnp.where(keep, pw[d - 1][1], 0.0)))
    order = list(range(7, -1, -1)) if reverse else list(range(8))
    carry = (_row_select(row, [pw[i][0] for i in order]), _row_select(row, [pw[i][1] for i in order]))
    return steps, carry


def _tile_scan_fwd(xr, xi, cr, ci, steps, carry_m):
    for d, (mr, mi) in zip((1, 2, 4), steps):
        sr = pltpu.roll(xr, d, 0)
        si = pltpu.roll(xi, d, 0)
        xr, xi = xr + mr * sr - mi * si, xi + mr * si + mi * sr
    pr, pi = carry_m
    return xr + pr * cr - pi * ci, xi + pr * ci + pi * cr


def _tile_scan_rev(xr, xi, cr, ci, steps, carry_m):
    for d, (mr, mi) in zip((1, 2, 4), steps):
        sr = pltpu.roll(xr, 8 - d, 0)
        si = pltpu.roll(xi, 8 - d, 0)
        xr, xi = xr + mr * sr + mi * si, xi + mr * si - mi * sr
    pr, pi = carry_m
    return xr + pr * cr + pi * ci, xi + pr * ci - pi * cr


def _fwd_scan_block(buf, row0, n_tiles8, ar, ai, c0r, c0i):
    steps, carry_m = _scan_tables(ar, ai, False)

    def body(j, carry):
        cr, ci = carry
        r0 = pl.multiple_of(row0 + j * 8, 8)
        xr = buf[pl.ds(r0, 8), 0:HALF]
        xi = buf[pl.ds(r0, 8), HALF:2 * HALF]
        xr, xi = _tile_scan_fwd(xr, xi, cr, ci, steps, carry_m)
        buf[pl.ds(r0, 8), 0:HALF] = xr
        buf[pl.ds(r0, 8), HALF:2 * HALF] = xi
        return xr[7:8], xi[7:8]

    return lax.fori_loop(0, n_tiles8, body, (c0r, c0i))


def s5_scan_fwd(name, h, wb, wc, a_tab, dskip, tile=TILE_SCAN):
    s = h.shape[0]
    n_t = s // tile

    def kern(h_ref, wb_ref, wc_ref, a_ref, d_ref, y_ref, s0_ref, carry_ref, buf):
        i = pl.program_id(0)

        @pl.when(i == 0)
        def _():
            carry_ref[...] = jnp.zeros(carry_ref.shape, F32)

        s0_ref[0] = carry_ref[...]
        for k in range(NBLK):
            cols = slice(GB * P * k, GB * P * (k + 1))
            u = h_ref[:, cols]
            buf[...] = jnp.dot(u.astype(_MXU), wb_ref[k], preferred_element_type=F32)
            ar = a_ref[k, :, 0:HALF]
            ai = a_ref[k, :, HALF:2 * HALF]
            cr, ci = _fwd_scan_block(buf, 0, tile // 8, ar, ai, carry_ref[k:k + 1, 0:HALF],
                                     carry_ref[k:k + 1, HALF:2 * HALF])
            carry_ref[k:k + 1, 0:HALF] = cr
            carry_ref[k:k + 1, HALF:2 * HALF] = ci
            y_ref[:, cols] = jnp.dot(buf[...].astype(_MXU), wc_ref[k], preferred_element_type=F32) + d_ref[:, cols] * u

    full = lambda a: pl.BlockSpec(a.shape, functools.partial(lambda i, nd_: (0,) * nd_, nd_=a.ndim))
    return pl.pallas_call(
        kern,
        out_shape=[jax.ShapeDtypeStruct((s, D), F32), jax.ShapeDtypeStruct((n_t, NBLK, 2 * HALF), F32)],
        grid=(n_t,),
        in_specs=[pl.BlockSpec((tile, D), lambda i: (i, 0)), full(wb), full(wc), full(a_tab), full(dskip)],
        out_specs=[pl.BlockSpec((tile, D), lambda i: (i, 0)), pl.BlockSpec((1, NBLK, 2 * HALF), lambda i: (i, 0, 0))],
        scratch_shapes=[pltpu.VMEM((NBLK, 2 * HALF), F32), pltpu.VMEM((tile, 2 * HALF), F32)],
        name=name, compiler_params=_params(1))(h, wb, wc, a_tab, dskip)


def s5_scan_bwd(name, h, dy, s0, wb, wc, a_tab, dskip, tile=TILE_SCAN):
    s = h.shape[0]
    n_t = s // tile
    n8 = tile // 8

    def kern(h_ref, dy_ref, s0_ref, wb_ref, wc_ref, a_ref, d_ref, dh_ref, dwb_ref, dwc_ref, da_ref, dd_ref,
             lam_ref, sbuf, gbuf):
        i = pl.program_id(0)

        @pl.when(i == 0)
        def _():
            lam_ref[...] = jnp.zeros(lam_ref.shape, F32)
            dwb_ref[...] = jnp.zeros(dwb_ref.shape, F32)
            dwc_ref[...] = jnp.zeros(dwc_ref.shape, F32)
            da_ref[...] = jnp.zeros(da_ref.shape, F32)
            dd_ref[...] = jnp.zeros(dd_ref.shape, F32)

        for k in range(NBLK):
            cols = slice(GB * P * k, GB * P * (k + 1))
            u = h_ref[:, cols]
            dyk = dy_ref[:, cols]
            ar = a_ref[k, :, 0:HALF]
            ai = a_ref[k, :, HALF:2 * HALF]
            sbuf[0:8, :] = jnp.broadcast_to(s0_ref[0, k:k + 1, :], (8, 2 * HALF))
            sbuf[8:tile + 8, :] = jnp.dot(u.astype(_MXU), wb_ref[k], preferred_element_type=F32)
            _fwd_scan_block(sbuf, 8, n8, ar, ai, s0_ref[0, k:k + 1, 0:HALF], s0_ref[0, k:k + 1, HALF:2 * HALF])
            dyb = dyk.astype(_MXU)
            gbuf[...] = lax.dot_general(dyb, wc_ref[k], (((1,), (1,)), ((), ())), preferred_element_type=F32)
            dwc_ref[k] += lax.dot_general(sbuf[8:tile + 8, :].astype(_MXU), dyb, (((0,), (0,)), ((), ())),
                                          preferred_element_type=F32)
            steps, carry_m = _scan_tables(ar, ai, True)
            row = lax.broadcasted_iota(jnp.int32, (8, HALF), 0)

            def body(jj, carry):
                cr, ci, dar, dai = carry
                j = n8 - 1 - jj
                r0 = pl.multiple_of(j * 8, 8)
                xr = gbuf[pl.ds(r0, 8), 0:HALF]
                xi = gbuf[pl.ds(r0, 8), HALF:2 * HALF]
                xr, xi = _tile_scan_rev(xr, xi, cr, ci, steps, carry_m)
                gbuf[pl.ds(r0, 8), 0:HALF] = xr
                gbuf[pl.ds(r0, 8), HALF:2 * HALF] = xi
                r1 = pl.multiple_of(j * 8 + 8, 8)
                spr = jnp.where(row == 0, sbuf[pl.ds(r0, 8), 0:HALF][7:8],
                                pltpu.roll(sbuf[pl.ds(r1, 8), 0:HALF], 1, 0))
                spi = jnp.where(row == 0, sbuf[pl.ds(r0, 8), HALF:2 * HALF][7:8],
                                pltpu.roll(sbuf[pl.ds(r1, 8), HALF:2 * HALF], 1, 0))
                dar = dar + xr * spr + xi * spi
                dai = dai + xi * spr - xr * spi
                return xr[0:1], xi[0:1], dar, dai

            z8 = jnp.zeros((8, HALF), F32)
            cr, ci, dar, dai = lax.fori_loop(
                0, n8, body, (lam_ref[k:k + 1, 0:HALF], lam_ref[k:k + 1, HALF:2 * HALF], z8, z8))
            lam_ref[k:k + 1, 0:HALF] = cr
            lam_ref[k:k + 1, HALF:2 * HALF] = ci
            da_ref[k:k + 1, 0:HALF] += jnp.sum(dar, axis=0, keepdims=True)
            da_ref[k:k + 1, HALF:2 * HALF] += jnp.sum(dai, axis=0, keepdims=True)
            lam = gbuf[...].astype(_MXU)
            dwb_ref[k] += lax.dot_general(u.astype(_MXU), lam, (((0,), (0,)), ((), ())), preferred_element_type=F32)
            du = lax.dot_general(lam, wb_ref[k], (((1,), (1,)), ((), ())), preferred_element_type=F32)
            dh_ref[:, cols] = du + d_ref[:, cols] * dyk
            dd_ref[:, cols] += jnp.sum(dyk * u, axis=0, keepdims=True)

    full = lambda a: pl.BlockSpec(a.shape, functools.partial(lambda i, nd_: (0,) * nd_, nd_=a.ndim))
    fullo = lambda shp: pl.BlockSpec(shp, functools.partial(lambda i, nd_: (0,) * nd_, nd_=len(shp)))
    rev = lambda i: (n_t - 1 - i, 0)
    return pl.pallas_call(
        kern,
        out_shape=[jax.ShapeDtypeStruct((s, D), F32), jax.ShapeDtypeStruct(wb.shape, F32),
                   jax.ShapeDtypeStruct(wc.shape, F32), jax.ShapeDtypeStruct((NBLK, 2 * HALF), F32),
                   jax.ShapeDtypeStruct((1, D), F32)],
        grid=(n_t,),
        in_specs=[pl.BlockSpec((tile, D), rev), pl.BlockSpec((tile, D), rev),
                  pl.BlockSpec((1, NBLK, 2 * HALF), lambda i: (n_t - 1 - i, 0, 0)),
                  full(wb), full(wc), full(a_tab), full(dskip)],
        out_specs=[pl.BlockSpec((tile, D), rev), fullo(wb.shape), fullo(wc.shape), fullo((NBLK, 2 * HALF)),
                   fullo((1, D))],
        scratch_shapes=[pltpu.VMEM((NBLK, 2 * HALF), F32), pltpu.VMEM((tile + 8, 2 * HALF), F32),
                        pltpu.VMEM((tile, 2 * HALF), F32)],
        name=name, compiler_params=_params(1))(h, dy, s0, wb, wc, a_tab, dskip)


def _chunk_mask(q0, k0, tq, tk):
    r = (q0 + lax.broadcasted_iota(jnp.int32, (tq, tk), 0)) // CHUNK
    c = (k0 + lax.broadcasted_iota(jnp.int32, (tq, tk), 1)) // CHUNK
    return r >= c


def _head_lanes(j):
    lane = _lane(2 * DV)
    return (lane >= DV * j) & (lane < DV * (j + 1))


def _raw_scores(q, kblk, masked, t):
    s = lax.dot_general(q, kblk, (((1,), (1,)), ((), ())), preferred_element_type=F32)
    return jnp.where(_chunk_mask(0, 0, t, t), s, -1e30) if masked else s


def attn_fwd(name, q, k, v, t=TILE_ATT_FWD, tk=TILE_ATT_KEYS):
    s = q.shape[0]
    n_q = s // t
    r = t // tk

    def kern(q_ref, k_ref, v_ref, o_ref, lse_ref):
        qi = pl.program_id(1)
        qs = [q_ref[:, HD * j:HD * (j + 1)] for j in range(2)]

        def absorb(k0, carry, mask):
            vblk = v_ref[pl.ds(k0, tk), :]
            scs = [lax.dot_general(qs[j], k_ref[pl.ds(k0, tk), HD * j:HD * (j + 1)], (((1,), (1,)), ((), ())),
                                   preferred_element_type=F32) for j in range(2)]
            if mask is not None:
                scs = [jnp.where(mask, sc, -1e30) for sc in scs]
            m_new = [jnp.maximum(carry[j][0], jnp.max(scs[j], axis=-1, keepdims=True)) for j in range(2)]
            ps = [jnp.exp2((scs[j] - m_new[j]) * EXP2_SCALE) for j in range(2)]
            alphas = [jnp.exp2((carry[j][0] - m_new[j]) * EXP2_SCALE) for j in range(2)]
            pvs = [jnp.dot(ps[j].astype(_MXU), vblk, preferred_element_type=F32) for j in range(2)]
            return tuple((m_new[j], alphas[j] * carry[j][1] + jnp.sum(ps[j], axis=-1, keepdims=True),
                          alphas[j] * carry[j][2] + pvs[j]) for j in range(2))

        init = tuple((jnp.full((t, 1), -1e30, F32), jnp.zeros((t, 1), F32), jnp.zeros((t, 2 * DV), F32))
                     for _ in range(2))
        carry = lax.fori_loop(0, qi * r, lambda kb, c: absorb(pl.multiple_of(kb * tk, tk), c, None), init)
        for i in range(r):
            carry = absorb(pl.multiple_of(qi * t + i * tk, tk), carry, _chunk_mask(0, i * tk, t, tk))
        outs = []
        for j in range(2):
            m, l, acc = carry[j]
            outs.append(acc / l)
            lse_ref[0, j] = m * ATTN_SCALE + jnp.log(l)
        o_ref[...] = jnp.where(_head_lanes(0), outs[0], outs[1])

    return pl.pallas_call(
        kern,
        out_shape=[jax.ShapeDtypeStruct((s, H * DV), F32), jax.ShapeDtypeStruct((HP, 2, s, 1), F32)],
        grid=(HP, n_q),
        in_specs=[pl.BlockSpec((t, 2 * HD), lambda hp, i: (i, hp)), pl.BlockSpec((s, 2 * HD), lambda hp, i: (0, hp)),
                  pl.BlockSpec((s, 2 * DV), lambda hp, i: (0, hp))],
        out_specs=[pl.BlockSpec((t, 2 * DV), lambda hp, i: (i, hp)),
                   pl.BlockSpec((1, 2, t, 1), lambda hp, i: (hp, 0, i, 0))],
        name=name, compiler_params=_params(2))(q, k, v)


def attn_bwd(name, q, k, v, o, do, lse, t=TILE_ATT):
    s = q.shape[0]
    n_q = s // t

    def kern(q_ref, k_ref, v_ref, o_ref, do_ref, lse_ref, dq_ref, dk_ref, dv_ref):
        qi = pl.program_id(1)

        @pl.when(qi == 0)
        def _():
            dk_ref[...] = jnp.zeros(dk_ref.shape, F32)
            dv_ref[...] = jnp.zeros(dv_ref.shape, F32)

        qs, doms, deltas, lse2 = [], [], [], []
        for j in range(2):
            qs.append(q_ref[:, HD * j:HD * (j + 1)])
            dom = jnp.where(_head_lanes(j), do_ref[...], 0.0)
            deltas.append(jnp.sum(dom * o_ref[...], axis=-1, keepdims=True))
            doms.append(dom.astype(_MXU))
            lse2.append(lse_ref[0, j] * LOG2E)

        def block(k0, dqs, masked):
            vblk = v_ref[pl.ds(k0, t), :]
            kblks = [k_ref[pl.ds(k0, t), HD * j:HD * (j + 1)] for j in range(2)]
            scs = [_raw_scores(qs[j], kblks[j], masked, t) for j in range(2)]
            dps = [lax.dot_general(doms[j], vblk, (((1,), (1,)), ((), ())), preferred_element_type=F32)
                   for j in range(2)]
            ps = [jnp.exp2(scs[j] * EXP2_SCALE - lse2[j]) for j in range(2)]
            dss = [(ps[j] * (dps[j] - deltas[j])).astype(_MXU) for j in range(2)]
            pbs = [ps[j].astype(_MXU) for j in range(2)]
            new = tuple(dqs[j] + jnp.dot(dss[j], kblks[j], preferred_element_type=F32) for j in range(2))
            for j in range(2):
                dk_ref[pl.ds(k0, t), HD * j:HD * (j + 1)] += lax.dot_general(
                    dss[j], qs[j], (((0,), (0,)), ((), ())), preferred_element_type=F32)
            dvs = [lax.dot_general(pbs[j], doms[j], (((0,), (0,)), ((), ())), preferred_element_type=F32)
                   for j in range(2)]
            dv_ref[pl.ds(k0, t), :] += dvs[0] + dvs[1]
            return new

        init = (jnp.zeros((t, HD), F32), jnp.zeros((t, HD), F32))
        dqs = lax.fori_loop(0, qi, lambda kb, c: block(pl.multiple_of(kb * t, t), c, False), init)
        dqs = block(pl.multiple_of(qi * t, t), dqs, True)
        for j in range(2):
            dq_ref[:, HD * j:HD * (j + 1)] = dqs[j] * ATTN_SCALE

        @pl.when(qi == n_q - 1)
        def _():
            dk_ref[...] = dk_ref[...] * ATTN_SCALE

    return pl.pallas_call(
        kern,
        out_shape=[jax.ShapeDtypeStruct((s, H * HD), F32), jax.ShapeDtypeStruct((s, H * HD), F32),
                   jax.ShapeDtypeStruct((s, H * DV), F32)],
        grid=(HP, n_q),
        in_specs=[pl.BlockSpec((t, 2 * HD), lambda hp, i: (i, hp)), pl.BlockSpec((s, 2 * HD), lambda hp, i: (0, hp)),
                  pl.BlockSpec((s, 2 * DV), lambda hp, i: (0, hp)), pl.BlockSpec((t, 2 * DV), lambda hp, i: (i, hp)),
                  pl.BlockSpec((t, 2 * DV), lambda hp, i: (i, hp)),
                  pl.BlockSpec((1, 2, t, 1), lambda hp, i: (hp, 0, i, 0))],
        out_specs=[pl.BlockSpec((t, 2 * HD), lambda hp, i: (i, hp)), pl.BlockSpec((s, 2 * HD), lambda hp, i: (0, hp)),
                   pl.BlockSpec((s, 2 * DV), lambda hp, i: (0, hp))],
        name=name, compiler_params=_params(2))(q, k, v, o, do, lse)


def rope_tables(name, pos_col, inv128):
    s = pos_col.shape[0]

    def kern(p_ref, inv_ref, c_ref, s_ref):
        ang = p_ref[...].astype(F32) * inv_ref[...]
        lane = _lane()
        m_r = (lane >= DN) & (lane < DN + DR)
        c_ref[...] = jnp.where(lane < DN, 1.0, jnp.where(m_r, jnp.cos(ang), 0.0))
        s_ref[...] = jnp.where(m_r, jnp.sin(ang), 0.0)

    return _whole(kern, name, [jax.ShapeDtypeStruct((s, HD), F32)] * 2, pos_col, inv128)


def loss_kernel(name, y, tgt, tile=TILE_ROW):
    def body(row_v, _):
        err = row_v[0] - row_v[1]
        part = 0.5 * jnp.sum(jnp.mean(err * err, axis=-1, keepdims=True), axis=0, keepdims=True)
        return [err * (1.0 / D)], [jnp.broadcast_to(part, (1, 128))]

    return _row_call(name, body, [y, tgt], [], [(D, F32)], [((1, 128), F32)], tile)


def _row_tile(r, c):
    cap = max(8, (1 << 18) // max(c, 1))
    for t in (2048, 1024, 512, 256, 128, 64, 32, 16, 8):
        if t <= cap and r % t == 0:
            return t
    return r


def sum_parts(name, parts):
    n, r, c = parts.shape
    t = _row_tile(r, c)

    def kern(p_ref, o_ref):
        acc = p_ref[0].astype(F32)
        for i in range(1, n):
            acc = acc + p_ref[i].astype(F32)
        o_ref[...] = acc

    return pl.pallas_call(kern, out_shape=jax.ShapeDtypeStruct((r, c), F32), grid=(r // t,),
                          in_specs=[pl.BlockSpec((n, t, c), lambda i: (0, i, 0))],
                          out_specs=pl.BlockSpec((t, c), lambda i: (i, 0)), name=name, compiler_params=_params(1))(parts)


def adamw(name, parts, w, m, v, base=0, stride=0):
    n, _, cp = parts.shape
    nl, r, c = w.shape
    t = _row_tile(math.gcd(math.gcd(r, base), stride), max(c, cp))
    c1 = 1.0 / (1.0 - ADAM_B1 ** ADAM_STEP)
    c2 = 1.0 / (1.0 - ADAM_B2 ** ADAM_STEP)

    def kern(p_ref, w_ref, m_ref, v_ref, g_ref, d_ref, nm_ref, nv_ref):
        g = p_ref[0].astype(F32)
        for i in range(1, n):
            g = g + p_ref[i].astype(F32)
        g = g[:, :c]
        nm = ADAM_B1 * m_ref[...] + (1.0 - ADAM_B1) * g
        nv = ADAM_B2 * v_ref[...] + (1.0 - ADAM_B2) * (g * g)
        g_ref[...] = g
        nm_ref[...] = nm
        nv_ref[...] = nv
        d_ref[...] = -ADAM_LR * ((nm * c1) / (jnp.sqrt(nv * c2) + ADAM_EPS) + ADAM_WD * w_ref[...])

    spec = pl.BlockSpec((None, t, c), lambda l, i: (l, i, 0))
    pspec = pl.BlockSpec((n, t, cp), lambda l, i: (0, (base + l * stride) // t + i, 0))
    return pl.pallas_call(kern, out_shape=[jax.ShapeDtypeStruct((nl, r, c), F32)] * 4, grid=(nl, r // t),
                          in_specs=[pspec, spec, spec, spec], out_specs=[spec] * 4, name=name,
                          compiler_params=_params(2))(parts, w, m, v)


def adamw_layer(name, parts, w, m, v, layer, prev, base=0):
    n, _, cp = parts.shape
    nl, r, c = w.shape
    t = _row_tile(math.gcd(r, base), max(c, cp))
    c1 = 1.0 / (1.0 - ADAM_B1 ** ADAM_STEP)
    c2 = 1.0 / (1.0 - ADAM_B2 ** ADAM_STEP)
    chained = nl > 1

    def kern(p_ref, w_ref, m_ref, v_ref, *rest):
        g_ref, d_ref, nm_ref, nv_ref = rest[-4:]
        g = p_ref[0].astype(F32)
        for i in range(1, n):
            g = g + p_ref[i].astype(F32)
        g = g[:, :c]
        nm = ADAM_B1 * m_ref[...] + (1.0 - ADAM_B1) * g
        nv = ADAM_B2 * v_ref[...] + (1.0 - ADAM_B2) * (g * g)
        g_ref[...] = g
        nm_ref[...] = nm
        nv_ref[...] = nv
        d_ref[...] = -ADAM_LR * ((nm * c1) / (jnp.sqrt(nv * c2) + ADAM_EPS) + ADAM_WD * w_ref[...])

    spec = pl.BlockSpec((None, t, c), lambda i: (layer, i, 0))
    pspec = pl.BlockSpec((n, t, cp), lambda i: (0, base // t + i, 0))
    in_specs = [pspec, spec, spec, spec]
    args = [parts, w, m, v]
    aliases = {}
    if chained:
        if prev is None:
            prev = [lax.empty((nl, r, c), F32) for _ in range(4)]
        in_specs += [pl.BlockSpec(memory_space=pl.ANY)] * 4
        args += list(prev)
        aliases = {4 + i: i for i in range(4)}
    return pl.pallas_call(kern, out_shape=[jax.ShapeDtypeStruct((nl, r, c), F32)] * 4, grid=(r // t,),
                          in_specs=in_specs, out_specs=[spec] * 4, input_output_aliases=aliases, name=name,
                          compiler_params=_params(1))(*args)


def _me():
    return lax.axis_index("x"), lax.axis_index("y"), lax.axis_index("c")


def _flip(x, y, c, mask):
    return (jnp.where((mask >> 2) & 1, 1 - x, x), jnp.where((mask >> 1) & 1, 1 - y, y), jnp.where(mask & 1, 1 - c, c))


def _index(x, y, c):
    return 4 * x + 2 * y + c


def _exchange(name, arr, gather):
    out_shape = (N_DEV,) + arr.shape if gather else arr.shape

    def kern(in_ref, out_ref, send_sems, recv_sems, local_sem):
        x, y, c = _me()
        me = _index(x, y, c)
        mine = pltpu.make_async_copy(in_ref if gather else in_ref.at[me], out_ref.at[me], local_sem)
        mine.start()
        copies = []
        for mask in range(1, N_DEV):
            px, py, pc = _flip(x, y, c, mask)
            peer = _index(px, py, pc)
            cp = pltpu.make_async_remote_copy(
                src_ref=in_ref if gather else in_ref.at[peer], dst_ref=out_ref.at[me],
                send_sem=send_sems.at[mask - 1], recv_sem=recv_sems.at[mask - 1],
                device_id=(px, py, pc), device_id_type=MESH)
            cp.start()
            copies.append((cp, peer))
        for mask, (cp, peer) in enumerate(copies, start=1):
            pltpu.make_async_remote_copy(
                src_ref=in_ref if gather else in_ref.at[peer], dst_ref=out_ref.at[peer],
                send_sem=send_sems.at[mask - 1], recv_sem=recv_sems.at[mask - 1],
                device_id=_flip(x, y, c, mask), device_id_type=MESH).wait_recv()
        for cp, _ in copies:
            cp.wait_send()
        mine.wait()

    any_spec = pl.BlockSpec(memory_space=pl.ANY)
    return pl.pallas_call(
        kern, out_shape=jax.ShapeDtypeStruct(out_shape, arr.dtype), in_specs=[any_spec], out_specs=any_spec,
        scratch_shapes=[pltpu.SemaphoreType.DMA((N_DEV - 1,)), pltpu.SemaphoreType.DMA((N_DEV - 1,)),
                        pltpu.SemaphoreType.DMA],
        name=name, compiler_params=pltpu.CompilerParams(has_side_effects=True))(arr)


def all_gather(name, arr):
    return _exchange(name, arr, True)


def all_to_all(name, arr):
    return _exchange(name, arr, False)


_HBM = pl.BlockSpec(memory_space=pltpu.HBM)
_SEM = pl.BlockSpec(memory_space=pltpu.SEMAPHORE)
_EFFECT = pltpu.SideEffectType.DATAFLOW_SIDE_EFFECTING


def _split_copies(srcs, lands, send_sems, recv_sems, gather):
    x, y, c = _me()
    me = _index(x, y, c)
    out = []
    for a, (src, land) in enumerate(zip(srcs, lands)):
        for mask in range(1, N_DEV):
            px, py, pc = _flip(x, y, c, mask)
            peer = _index(px, py, pc)
            sem = (N_DEV - 1) * a + mask - 1
            mk = lambda dst_slot: pltpu.make_async_remote_copy(
                src_ref=src if gather else src.at[peer], dst_ref=land.at[dst_slot],
                send_sem=send_sems.at[sem], recv_sem=recv_sems.at[sem], device_id=(px, py, pc), device_id_type=MESH)
            out.append((mk(me), mk(peer)))
    return out


def exchange_start(name, arrs, gather, after):
    k = len(arrs)
    land_shapes = [((N_DEV,) + a.shape if gather else a.shape) for a in arrs]

    def body(*refs):
        srcs, lands = refs[:k], refs[k:2 * k]
        send_sems, recv_sems = refs[2 * k + 1], refs[2 * k + 2]
        token = refs[-1]
        for mine, _ in _split_copies(srcs, lands, send_sems, recv_sems, gather):
            mine.start()
        token[...] = jnp.zeros(token.shape, token.dtype)

    n_sem = (N_DEV - 1) * k
    res = pl.pallas_call(
        body, name=name,
        out_shape=(pltpu.SemaphoreType.DMA((n_sem,)), pltpu.SemaphoreType.DMA((n_sem,)),
                   *[pltpu.HBM(a.shape, a.dtype) for a in arrs],
                   *[pltpu.HBM(shp, a.dtype) for shp, a in zip(land_shapes, arrs)],
                   jax.ShapeDtypeStruct((8, 128), F32)),
        in_specs=[_HBM] * (2 * k) + [pl.BlockSpec(memory_space=pl.ANY)],
        out_specs=(_SEM, _SEM, *[_HBM] * (2 * k), pl.BlockSpec(memory_space=pltpu.VMEM)),
        input_output_aliases={i: 2 + i for i in range(2 * k)},
        compiler_params=pltpu.CompilerParams(has_side_effects=_EFFECT),
    )(*[pltpu.with_memory_space_constraint(a, pltpu.HBM) for a in arrs],
      *[pltpu.with_memory_space_constraint(lax.empty(shp, a.dtype), pltpu.HBM) for shp, a in zip(land_shapes, arrs)],
      after)
    return res[0], res[1], list(res[2:2 + k]), list(res[2 + k:2 + 2 * k]), res[-1]


def exchange_wait(name, started, after, gather):
    send_sems, recv_sems, thrus, lands, _ = started
    k = len(thrus)

    def body(*refs):
        srcs, lnds = refs[:k], refs[k:2 * k]
        s_sems, r_sems = refs[2 * k], refs[2 * k + 1]
        for mine, theirs in _split_copies(srcs, lnds, s_sems, r_sems, gather):
            mine.wait_send()
            theirs.wait_recv()

    res = pl.pallas_call(
        body, name=name,
        out_shape=tuple(pltpu.HBM(a.shape, a.dtype) for a in thrus + lands),
        in_specs=[_HBM] * (2 * k) + [_SEM, _SEM, pl.BlockSpec(memory_space=pl.ANY)], out_specs=tuple([_HBM] * (2 * k)),
        input_output_aliases={i: i for i in range(2 * k)},
        compiler_params=pltpu.CompilerParams(has_side_effects=_EFFECT),
    )(*thrus, *lands, send_sems, recv_sems, after)
    return list(res[k:])


def _pad_heads(w, real, padded):
    k = w.shape[0]
    w3 = w.reshape(k, H, real)
    return jnp.pad(w3, ((0, 0), (0, 0), (0, padded - real))).reshape(k, H * padded)


def _unpad_heads(w, real, padded):
    k = w.shape[0]
    return w.reshape(k, H, padded)[:, :, :real].reshape(k, H * real)


def _s5_place(ab_re, ab_im, bb_re_t, bb_im_t, c_re, c_im):
    eye = jnp.eye(GB, dtype=F32)

    def wb_part(bt):
        x4 = bt.reshape(P, NBLK, GB, N).transpose(1, 2, 0, 3)
        return jnp.einsum('kgpn,gh->kgphn', x4, eye).reshape(NBLK, GB * P, HALF)

    def wc_part(cc):
        x4 = cc.reshape(NBLK, GB, P, N)
        return jnp.einsum('kgpn,gh->kgnhp', x4, eye).reshape(NBLK, HALF, GB * P)

    wb = jnp.concatenate([wb_part(bb_re_t), wb_part(bb_im_t)], axis=-1)
    wc = jnp.concatenate([wc_part(c_re), -wc_part(c_im)], axis=1)
    a_tab = jnp.concatenate([ab_re.reshape(NBLK, 1, HALF), ab_im.reshape(NBLK, 1, HALF)], axis=-1)
    return wb.astype(_MXU), wc.astype(_MXU), a_tab


def _s5_unplace(dwb, dwc, da):
    eye = jnp.eye(GB, dtype=F32)

    def wb_part(dpart):
        x5 = dpart.reshape(NBLK, GB, P, GB, N)
        return jnp.einsum('kgphn,gh->kgpn', x5, eye).transpose(2, 0, 1, 3).reshape(P, G * N)

    def wc_part(dpart):
        x5 = dpart.reshape(NBLK, GB, N, GB, P)
        return jnp.einsum('kgnhp,gh->kgpn', x5, eye).reshape(G, P, N)

    dbb_re_t, dbb_im_t = wb_part(dwb[..., :HALF]), wb_part(dwb[..., HALF:])
    dc_re, dc_im = wc_part(dwc[:, :HALF]), -wc_part(dwc[:, HALF:])
    dab_re, dab_im = da[:, :HALF].reshape(1, G * N), da[:, HALF:].reshape(1, G * N)
    return dab_re, dab_im, dbb_re_t, dbb_im_t, dc_re, dc_im


def _row(v):
    return v.reshape(1, -1)


def kernel(x, c, positions, ada_w, ada_b, norm1_g, norm2_g, ffn_w_gate, ffn_w_up, ffn_w_down, s5_lam_re, s5_lam_im, s5_log_dt, s5_b_re, s5_b_im, s5_c_re, s5_c_im, s5_d, s5_w_glu, s5_b_glu, kv_ada_w, kv_ada_b, kv_norm_g, w_kv_a, kv_a_norm_g, w_kv_b, k_nope_norm_g, k_rope_norm_g, mla_w_dq, mla_q_norm_g, mla_w_uq, mla_q_nope_norm_g, mla_q_rope_norm_g, mla_w_o, loss_target, m_ada_w, m_ada_b, m_norm1_g, m_norm2_g, m_ffn_w_gate, m_ffn_w_up, m_ffn_w_down, m_s5_lam_re, m_s5_lam_im, m_s5_log_dt, m_s5_b_re, m_s5_b_im, m_s5_c_re, m_s5_c_im, m_s5_d, m_s5_w_glu, m_s5_b_glu, m_kv_ada_w, m_kv_ada_b, m_kv_norm_g, m_w_kv_a, m_kv_a_norm_g, m_w_kv_b, m_k_nope_norm_g, m_k_rope_norm_g, m_mla_w_dq, m_mla_q_norm_g, m_mla_w_uq, m_mla_q_nope_norm_g, m_mla_q_rope_norm_g, m_mla_w_o, v_ada_w, v_ada_b, v_norm1_g, v_norm2_g, v_ffn_w_gate, v_ffn_w_up, v_ffn_w_down, v_s5_lam_re, v_s5_lam_im, v_s5_log_dt, v_s5_b_re, v_s5_b_im, v_s5_c_re, v_s5_c_im, v_s5_d, v_s5_w_glu, v_s5_b_glu, v_kv_ada_w, v_kv_ada_b, v_kv_norm_g, v_w_kv_a, v_kv_a_norm_g, v_w_kv_b, v_k_nope_norm_g, v_k_rope_norm_g, v_mla_w_dq, v_mla_q_norm_g, v_mla_w_uq, v_mla_q_nope_norm_g, v_mla_q_rope_norm_g, v_mla_w_o):
    W = dict(ada_w=ada_w, ada_b=ada_b, norm1_g=norm1_g, norm2_g=norm2_g, ffn_w_gate=ffn_w_gate, ffn_w_up=ffn_w_up, ffn_w_down=ffn_w_down, s5_lam_re=s5_lam_re, s5_lam_im=s5_lam_im, s5_log_dt=s5_log_dt, s5_b_re=s5_b_re, s5_b_im=s5_b_im, s5_c_re=s5_c_re, s5_c_im=s5_c_im, s5_d=s5_d, s5_w_glu=s5_w_glu, s5_b_glu=s5_b_glu, kv_ada_w=kv_ada_w, kv_ada_b=kv_ada_b, kv_norm_g=kv_norm_g, w_kv_a=w_kv_a, kv_a_norm_g=kv_a_norm_g, w_kv_b=w_kv_b, k_nope_norm_g=k_nope_norm_g, k_rope_norm_g=k_rope_norm_g, mla_w_dq=mla_w_dq, mla_q_norm_g=mla_q_norm_g, mla_w_uq=mla_w_uq, mla_q_nope_norm_g=mla_q_nope_norm_g, mla_q_rope_norm_g=mla_q_rope_norm_g, mla_w_o=mla_w_o)
    M = dict(ada_w=m_ada_w, ada_b=m_ada_b, norm1_g=m_norm1_g, norm2_g=m_norm2_g, ffn_w_gate=m_ffn_w_gate, ffn_w_up=m_ffn_w_up, ffn_w_down=m_ffn_w_down, s5_lam_re=m_s5_lam_re, s5_lam_im=m_s5_lam_im, s5_log_dt=m_s5_log_dt, s5_b_re=m_s5_b_re, s5_b_im=m_s5_b_im, s5_c_re=m_s5_c_re, s5_c_im=m_s5_c_im, s5_d=m_s5_d, s5_w_glu=m_s5_w_glu, s5_b_glu=m_s5_b_glu, kv_ada_w=m_kv_ada_w, kv_ada_b=m_kv_ada_b, kv_norm_g=m_kv_norm_g, w_kv_a=m_w_kv_a, kv_a_norm_g=m_kv_a_norm_g, w_kv_b=m_w_kv_b, k_nope_norm_g=m_k_nope_norm_g, k_rope_norm_g=m_k_rope_norm_g, mla_w_dq=m_mla_w_dq, mla_q_norm_g=m_mla_q_norm_g, mla_w_uq=m_mla_w_uq, mla_q_nope_norm_g=m_mla_q_nope_norm_g, mla_q_rope_norm_g=m_mla_q_rope_norm_g, mla_w_o=m_mla_w_o)
    V = dict(ada_w=v_ada_w, ada_b=v_ada_b, norm1_g=v_norm1_g, norm2_g=v_norm2_g, ffn_w_gate=v_ffn_w_gate, ffn_w_up=v_ffn_w_up, ffn_w_down=v_ffn_w_down, s5_lam_re=v_s5_lam_re, s5_lam_im=v_s5_lam_im, s5_log_dt=v_s5_log_dt, s5_b_re=v_s5_b_re, s5_b_im=v_s5_b_im, s5_c_re=v_s5_c_re, s5_c_im=v_s5_c_im, s5_d=v_s5_d, s5_w_glu=v_s5_w_glu, s5_b_glu=v_s5_b_glu, kv_ada_w=v_kv_ada_w, kv_ada_b=v_kv_ada_b, kv_norm_g=v_kv_norm_g, w_kv_a=v_w_kv_a, kv_a_norm_g=v_kv_a_norm_g, w_kv_b=v_w_kv_b, k_nope_norm_g=v_k_nope_norm_g, k_rope_norm_g=v_k_rope_norm_g, mla_w_dq=v_mla_w_dq, mla_q_norm_g=v_mla_q_norm_g, mla_w_uq=v_mla_w_uq, mla_q_nope_norm_g=v_mla_q_nope_norm_g, mla_q_rope_norm_g=v_mla_q_rope_norm_g, mla_w_o=v_mla_w_o)
    return _step(x[0], c, positions, loss_target[0], W, M, V)


WEIGHT_NAMES = ['ada_w', 'ada_b', 'norm1_g', 'norm2_g', 'ffn_w_gate', 'ffn_w_up', 'ffn_w_down', 's5_lam_re', 's5_lam_im', 's5_log_dt', 's5_b_re', 's5_b_im', 's5_c_re', 's5_c_im', 's5_d', 's5_w_glu', 's5_b_glu', 'kv_ada_w', 'kv_ada_b', 'kv_norm_g', 'w_kv_a', 'kv_a_norm_g', 'w_kv_b', 'k_nope_norm_g', 'k_rope_norm_g', 'mla_w_dq', 'mla_q_norm_g', 'mla_w_uq', 'mla_q_nope_norm_g', 'mla_q_rope_norm_g', 'mla_w_o']
REPLICATED = ['ada_b', 'norm1_g', 'norm2_g', 's5_lam_re', 's5_lam_im', 's5_log_dt', 's5_b_re', 's5_b_im', 's5_c_re', 's5_c_im', 'kv_ada_b', 'kv_norm_g', 'kv_a_norm_g', 'k_nope_norm_g', 'k_rope_norm_g', 'mla_q_norm_g', 'mla_q_nope_norm_g', 'mla_q_rope_norm_g']
SHARDED_VEC = ['s5_d', 's5_b_glu']


def _step(x, c, positions, target, W, M, V):
    s = x.shape[0]
    me = _index(*_me())
    mxu = lambda a: a.astype(_MXU)

    pad_c = lambda a: jnp.pad(a, ((0, 0), (0, FFB - FF // N_DEV)))
    pad_r = lambda a: jnp.pad(a, ((0, FFB - FF // N_DEV), (0, 0)))
    cols = lambda g: g.transpose(1, 0, 2).reshape(g.shape[1], N_DEV * g.shape[2])
    rows = lambda g: g.reshape(N_DEV * g.shape[1], g.shape[2])

    def local_pack(l):
        second = W['s5_w_glu'][l] if l < N_A else W['mla_w_o'][l - N_A]
        arrs = [jnp.concatenate([mxu(pad_c(W['ffn_w_gate'][l])), mxu(pad_c(W['ffn_w_up'][l]))], axis=0),
                jnp.concatenate([mxu(pad_r(W['ffn_w_down'][l])), mxu(second)], axis=0)]
        if l == N_A:
            arrs += [jnp.concatenate([mxu(W['w_kv_b']), mxu(W['mla_w_dq'][0])], axis=0), mxu(W['w_kv_a'])]
        if l > N_A:
            arrs += [mxu(W['mla_w_dq'][l - N_A])]
        if l >= N_A:
            arrs += [mxu(W['mla_w_uq'][l - N_A])]
        return arrs


    def layer_weights(l, after):
        lands = exchange_wait(f"gather_wait_{l}", gathers[l], after, True)
        full = [lax.dynamic_update_slice(ld, src[None], (me,) + (0,) * src.ndim) for ld, src in zip(lands, gathers[l][2])]
        w = {'wg': cols(full[0][:, :D]), 'wu': cols(full[0][:, D:]), 'wd': rows(full[1][:, :FFB]),
             'second': rows(full[1][:, FFB:])}
        if l >= N_A:
            if l == N_A:
                wkvb3 = cols(full[2][:, :KVL]).reshape(KVL, H, DN + DV)
                wkva = rows(full[3])
                w['wa_pad'] = jnp.concatenate([wkva[:, :KVL], jnp.zeros((D, DN), _MXU), wkva[:, KVL:],
                                               jnp.zeros((D, HD - DN - DR), _MXU)], axis=1)
                w['wkn_pad'] = jnp.pad(wkvb3[:, :, :DN], ((0, 0), (0, 0), (0, HD - DN))).reshape(KVL, H * HD)
                w['wv'] = wkvb3[:, :, DN:].reshape(KVL, H * DV)
                w['wdq'] = rows(full[2][:, KVL:])
            else:
                w['wdq'] = rows(full[2])
            w['wuq_pad'] = _pad_heads(cols(full[-1]), DN + DR, HD)
        return w

    vec = jnp.concatenate([c.reshape(-1), W['s5_d'].reshape(-1), W['s5_b_glu'].reshape(-1)]).reshape(1, -1)
    vec = jnp.pad(vec, ((0, 7), (0, 0)))
    gv = all_gather("gather_vectors", vec)[:, 0, :]
    c_all = gv[:, :D]
    d_full = jnp.concatenate([gv[d, D:D + 2 * 128].reshape(N_A, 128) for d in range(N_DEV)], axis=1)
    bglu_full = jnp.concatenate([gv[d, D + 256:D + 512].reshape(N_A, 128) for d in range(N_DEV)], axis=1)

    ca_all = jax.nn.silu(c_all)
    w_mod = jnp.concatenate([W['ada_w'][l] for l in range(DEPTH)] + [W['kv_ada_w']], axis=1)
    n_mod = w_mod.shape[1]
    mod_cols = small_matmul("mod_matmul", ca_all, w_mod)
    gm = all_gather("gather_mod", mod_cols)
    gathers = [exchange_start(f"gather_start_{l}", local_pack(l), True, gm) for l in range(DEPTH)]
    tokens = sum(g[4][0, 0] for g in gathers)
    mine = lax.dynamic_index_in_dim(gm, me, axis=1, keepdims=False) + tokens
    per_l = D * 6 // N_DEV
    mods = []
    for l in range(DEPTH):
        full = jnp.concatenate([mine[d, per_l * l:per_l * (l + 1)] for d in range(N_DEV)]) + W['ada_b'][l]
        mods.append([_row(full[D * i:D * (i + 1)]) for i in range(6)])
    kfull = jnp.concatenate([mine[d, per_l * DEPTH:] for d in range(N_DEV)]) + W['kv_ada_b']
    k_shift, k_scale = _row(kfull[:D]), _row(kfull[D:])

    inv = 1.0 / (ROPE_THETA ** (np.arange(0, DR, 2, dtype=np.float32) / DR))
    inv128 = np.zeros((1, HD), np.float32)
    inv128[0, DN:DN + DR // 2] = inv
    inv128[0, DN + DR // 2:DN + DR] = inv
    cosf, sinf = rope_tables("rope_tables", positions.reshape(s, 1), jnp.asarray(inv128))
    zpad = lambda n: jnp.zeros((n,), F32)
    gkn128 = _row(jnp.concatenate([W['k_nope_norm_g'], zpad(HD - DN)]))
    gkr128 = _row(jnp.concatenate([zpad(DN), W['k_rope_norm_g'], zpad(HD - DN - DR)]))
    gq128 = [_row(jnp.concatenate([W['mla_q_nope_norm_g'][j], W['mla_q_rope_norm_g'][j], zpad(HD - DN - DR)]))
             for j in range(2)]

    expand = jnp.asarray(np.kron(np.eye(G, dtype=np.float32), np.ones((1, N), np.float32)))
    s5_raw, s5_mats = [], []
    for l in range(N_A):
        raw = (_row(W['s5_lam_re'][l]), _row(W['s5_lam_im'][l]), _row(W['s5_log_dt'][l]),
               W['s5_b_re'][l].transpose(2, 0, 1).reshape(P, G * N), W['s5_b_im'][l].transpose(2, 0, 1).reshape(P, G * N))
        ab_re, ab_im, bb_re_t, bb_im_t = s5_prep_fwd(f"s5_prep_fwd", *raw, expand)
        s5_raw.append(raw)
        s5_mats.append(_s5_place(ab_re, ab_im, bb_re_t, bb_im_t, W['s5_c_re'][l], W['s5_c_im'][l]))

    g1 = [_row(W['norm1_g'][l]) for l in range(DEPTH)]
    g2 = [_row(W['norm2_g'][l]) for l in range(DEPTH)]
    saved = []
    xs = x
    kv = None
    lw = [None] * DEPTH
    for l in range(DEPTH):
        sh1, sc1, gt1, sh2, sc2, gt2 = mods[l]
        rec = {'x_in': xs}
        if l >= N_A:
            lw[l] = layer_weights(l, xs)
        if l == N_A:
            kv_smalls = [_row(W['kv_norm_g']), k_shift, k_scale, _row(W['kv_a_norm_g']), gkn128, gkr128]
            kv_w = [lw[l]['wa_pad'], lw[l]['wkn_pad'], lw[l]['wv']]
            k_mat, v_mat = seg_forward("kv_fwd", seg_kv, [xs], kv_smalls, [cosf, sinf], kv_w,
                                       [(H * HD, _MXU), (H * DV, _MXU)], tap_widths=(KVL + HD, H * HD, H * DV))
            kv = {'x_in': xs, 'smalls': kv_smalls, 'k': k_mat, 'v': v_mat, 'w': kv_w}
        if l < N_A:
            (h,) = seg_forward("pre_fwd", seg_pre, [xs], [g1[l], sh1, sc1], [], [], [(D, F32)])
            wb, wc, a_tab = s5_mats[l]
            y, s0 = s5_scan_fwd("s5_scan_fwd", h, wb, wc, a_tab, _row(d_full[l]))
            lw[l] = layer_weights(l, y)
            (x_mid,) = seg_forward("glu_fwd", seg_glu, [xs, y], [gt1, _row(bglu_full[l])], [], [lw[l]['second']],
                                   [(D, F32)], tap_widths=(D,))
            rec.update(h=h, y=y, s0=s0)
        else:
            j = l - N_A
            q_smalls = [g1[l], sh1, sc1, _row(W['mla_q_norm_g'][j]), gq128[j]]
            (q_mat,) = seg_forward("q_fwd", seg_q, [xs], q_smalls, [cosf, sinf], [lw[l]['wdq'], lw[l]['wuq_pad']],
                                   [(H * HD, _MXU)], tap_widths=(QL, H * HD))
            o_mat, lse = attn_fwd("attn_fwd", q_mat, kv['k'], kv['v'])
            (x_mid,) = seg_forward("o_fwd", seg_o, [xs, o_mat], [gt1], [], [lw[l]['second']], [(D, F32)],
                                   tap_widths=(D,))
            rec.update(q=q_mat, o=o_mat, lse=lse, q_smalls=q_smalls)
        rec['x_mid'] = x_mid
        (xs,) = seg_forward("ffn_fwd", seg_ffn, [x_mid], [g2[l], sh2, sc2, gt2], [],
                            [lw[l]['wg'], lw[l]['wu'], lw[l]['wd']], [(D, F32)], tap_widths=(FFP, FFP, D))
        saved.append(rec)

    dy, loss_part = loss_kernel("loss", xs, target)
    loss = lax.psum(loss_part[0, 0], ("x", "y", "c"))

    rblk = lambda a: a.reshape(N_DEV, a.shape[0] // N_DEV, a.shape[1])
    cblk = lambda a: a.reshape(a.shape[0], N_DEV, a.shape[1] // N_DEV).transpose(1, 0, 2)
    dmod = [None] * DEPTH
    dk_tot = []
    dv_tot = []
    dx = dy
    sends = [None] * DEPTH
    send_token = jnp.zeros((1, 1), F32)
    g_n1 = [None] * DEPTH
    g_n2 = [None] * DEPTH
    g_bglu = [None] * N_A
    g_dskip = [None] * N_A
    g_s5 = [None] * N_A
    g_qn, g_q128 = [None] * 2, [None] * 2
    for l in range(DEPTH - 1, -1, -1):
        rec = saved[l]
        sh1, sc1, gt1, sh2, sc2, gt2 = mods[l]
        dx, dgate, dup, dyd, h_b, a_b, dg2, dsh2, dsc2, dgt2 = ffn_backward(
            "ffn_bwd", rec['x_mid'], dx, g2[l], sh2, sc2, gt2 + send_token, lw[l]['wg'], lw[l]['wu'], lw[l]['wd'])
        out_l = [matmul_tn("tn_ffn_in", h_b, dgate, _MXU, col_blocks=N_DEV),
                 matmul_tn("tn_ffn_in", h_b, dup, _MXU, col_blocks=N_DEV),
                 matmul_tn("tn_ffn_out", a_b, dyd, _MXU).reshape(N_DEV, FFB, D)]
        g_n2[l] = dg2
        if l < N_A:
            (dx, dyy), (dz,), (g_b,), (dgt1, dbg) = seg_backward(
                "glu_bwd", seg_glu, [rec['x_in'], rec['y']], [gt1, _row(bglu_full[l])], [], [lw[l]['second']],
                [dx], (D,), (D,))
            out_l.append(rblk(matmul_tn("tn_sq", g_b, dz, _MXU)))
            g_bglu[l] = dbg
            wb, wc, a_tab = s5_mats[l]
            dh, dwb, dwc, da, dd = s5_scan_bwd("s5_scan_bwd", rec['h'], dyy, rec['s0'], wb, wc, a_tab, _row(d_full[l]))
            g_dskip[l] = dd
            dab_re, dab_im, dbb_re_t, dbb_im_t, dc_re, dc_im = _s5_unplace(dwb, dwc, da)
            dlr, dli, dldt, dbr_t, dbi_t = s5_prep_bwd("s5_prep_bwd", *s5_raw[l], expand,
                                                       (dab_re, dab_im, dbb_re_t, dbb_im_t))
            g_s5[l] = (dlr.reshape(G, N), dli.reshape(G, N), dldt.reshape(G),
                       dbr_t.reshape(P, G, N).transpose(1, 2, 0), dbi_t.reshape(P, G, N).transpose(1, 2, 0), dc_re, dc_im)
            (dx,), _, _, (dg1, dsh1, dsc1) = seg_backward(
                "pre_bwd", seg_pre, [rec['x_in']], [g1[l], sh1, sc1], [], [], [dh], (), (), dx_add=dx)
        else:
            j = l - N_A
            (dx, do), (dzo,), (o_b,), (dgt1,) = seg_backward(
                "o_bwd", seg_o, [rec['x_in'], rec['o']], [gt1], [], [lw[l]['second']], [dx], (D,), (D,))
            out_l.append(rblk(matmul_tn("tn_sq", o_b, dzo, _MXU)))
            dq, dk, dv = attn_bwd("attn_bwd", rec['q'], kv['k'], kv['v'], rec['o'], do, rec['lse'])
            dk_tot.append(dk)
            dv_tot.append(dv)
            (dx,), (dql, dqq), (hq_b, qn_b), (dg1, dsh1, dsc1, dqg, dq128) = seg_backward(
                "q_bwd", seg_q, [rec['x_in']], rec['q_smalls'], [cosf, sinf], [lw[l]['wdq'], lw[l]['wuq_pad']],
                [dq], (QL, H * HD), (D, QL), dx_add=dx)
            g_dq = rblk(matmul_tn("tn_dq", hq_b, dql, _MXU))
            g_uq = cblk(_unpad_heads(matmul_tn("tn_uq", qn_b, dqq, _MXU), DN + DR, HD))
            g_qn[j], g_q128[j] = dqg, dq128
        g_n1[l] = dg1
        dmod[l] = jnp.concatenate([dsh1, dsc1, dgt1, dsh2, dsc2, dgt2], axis=1)
        if l == N_A:
            dkk = sum_parts("sum_dk", jnp.stack(dk_tot))
            dvv = sum_parts("sum_dv", jnp.stack(dv_tot))
            (dx,), (dta, dtk, dtv), (hk_b, ckv_b), (dkg, dksh, dksc, dag, dgkn, dgkr) = seg_backward(
                "kv_bwd", seg_kv, [kv['x_in']], kv['smalls'], [cosf, sinf], kv['w'],
                [dkk, dvv], (KVL + HD, H * HD, H * DV), (D, KVL), dx_add=dx)
            g_wa = matmul_tn("tn_kva", hk_b, dta, _MXU)
            g_wa = jnp.concatenate([g_wa[:, :KVL], g_wa[:, KVL + DN:KVL + DN + DR]], axis=1)
            g_kn = matmul_tn("tn_kn", ckv_b, dtk, _MXU).reshape(KVL, H, HD)[:, :, :DN]
            g_v = matmul_tn("tn_v", ckv_b, dtv, _MXU).reshape(KVL, H, DV)
            g_wkvb = jnp.concatenate([g_kn, g_v], axis=2).reshape(KVL, H * (DN + DV))
            dkmod = jnp.concatenate([dksh, dksc], axis=1)
            out_l += [jnp.concatenate([cblk(g_wkvb), g_dq], axis=1), rblk(g_wa)]
        if l > N_A:
            out_l.append(g_dq)
        if l >= N_A:
            out_l.append(g_uq)
        if l > 0:
            sends[l] = exchange_start(f"a2a_start_{l}", out_l, False, dx)
            send_token = sends[l][4][0:1, 0:1]
    grad_x = dx

    dm = jnp.concatenate(dmod + [dkmod], axis=1)[0]
    per_dev = []
    for d in range(N_DEV):
        cols = [dm[6 * D * l + per_l * d:6 * D * l + per_l * (d + 1)] for l in range(DEPTH)]
        cols.append(dm[6 * D * DEPTH + (2 * D // N_DEV) * d:6 * D * DEPTH + (2 * D // N_DEV) * (d + 1)])
        per_dev.append(jnp.concatenate(cols))
    dm_dev = jnp.stack(per_dev)
    gdm = all_gather("gather_dmod", dm_dev)
    dm_mine = lax.dynamic_index_in_dim(gdm, me, axis=1, keepdims=False)
    g_wmod = small_matmul_tn("dmod_matmul", ca_all, dm_mine)
    g_ada_w = jnp.stack([g_wmod[:, per_l * l:per_l * (l + 1)] for l in range(DEPTH)])
    g_kv_ada_w = g_wmod[:, per_l * DEPTH:]
    dm_sum = sum_parts("sum_dmod", gdm.reshape(N_DEV, N_DEV, n_mod))
    g_ada_b = jnp.stack([jnp.concatenate([dm_sum[d, per_l * l:per_l * (l + 1)] for d in range(N_DEV)])
                         for l in range(DEPTH)])
    g_kv_ada_b = jnp.concatenate([dm_sum[d, per_l * DEPTH:] for d in range(N_DEV)])

    small = {
        'norm1_g': jnp.concatenate(g_n1, axis=0), 'norm2_g': jnp.concatenate(g_n2, axis=0),
        's5_lam_re': jnp.stack([g_s5[l][0] for l in range(N_A)]), 's5_lam_im': jnp.stack([g_s5[l][1] for l in range(N_A)]),
        's5_log_dt': jnp.stack([g_s5[l][2] for l in range(N_A)]),
        's5_b_re': jnp.stack([g_s5[l][3] for l in range(N_A)]), 's5_b_im': jnp.stack([g_s5[l][4] for l in range(N_A)]),
        's5_c_re': jnp.stack([g_s5[l][5] for l in range(N_A)]), 's5_c_im': jnp.stack([g_s5[l][6] for l in range(N_A)]),
        'kv_norm_g': dkg, 'kv_a_norm_g': dag, 'k_nope_norm_g': dgkn[:, :DN], 'k_rope_norm_g': dgkr[:, DN:DN + DR],
        'mla_q_norm_g': jnp.concatenate(g_qn, axis=0),
        'mla_q_nope_norm_g': jnp.concatenate([g[:, :DN] for g in g_q128], axis=0),
        'mla_q_rope_norm_g': jnp.concatenate([g[:, DN:DN + DR] for g in g_q128], axis=0),
        's5_d': jnp.concatenate(g_dskip, axis=0), 's5_b_glu': jnp.concatenate(g_bglu, axis=0),
    }
    small_names = [n for n in REPLICATED if n not in ('ada_b', 'kv_ada_b')] + SHARDED_VEC
    flat_small = jnp.concatenate([small[n].reshape(-1) for n in small_names])
    n_small = int(flat_small.shape[0])
    pad_small = -(-n_small // 65536) * 65536
    flat_small = jnp.pad(flat_small, (0, pad_small - n_small)).reshape(pad_small // 128, 128)
    g_small_sum = sum_parts("sum_small", all_gather("gather_small", flat_small)).reshape(-1)
    sends[0] = exchange_start("a2a_start_0", out_l, False, g_small_sum)
    g_small_sum = g_small_sum + sends[0][4][0, 0]
    grads = {}
    off = 0
    for n in small_names:
        size = int(np.prod(small[n].shape))
        full = g_small_sum[off:off + size]
        off += size
        if n in SHARDED_VEC:
            full = lax.dynamic_slice_in_dim(full.reshape(N_A, D), me * (D // N_DEV), D // N_DEV, axis=1)
        grads[n] = full.reshape(W[n].shape)
    grads['ada_b'] = g_ada_b
    grads['kv_ada_b'] = g_kv_ada_b

    packed_names = REPLICATED + SHARDED_VEC

    def pack(dct):
        flat_ = jnp.concatenate([dct[n].reshape(-1) for n in packed_names])
        n_ = int(flat_.shape[0])
        p_ = -(-n_ // 65536) * 65536
        return jnp.pad(flat_, (0, p_ - n_)).reshape(p_ // 128, 128)

    _, d_p, m_p, v_p = adamw("adamw_small", pack(grads)[None], pack(W)[None], pack(M)[None], pack(V)[None])
    out_delta, out_m, out_v = {}, {}, {}
    off = 0
    d_p, m_p, v_p = d_p.reshape(-1), m_p.reshape(-1), v_p.reshape(-1)
    for n in packed_names:
        size = int(np.prod(W[n].shape))
        out_delta[n] = d_p[off:off + size].reshape(W[n].shape)
        out_m[n] = m_p[off:off + size].reshape(W[n].shape)
        out_v[n] = v_p[off:off + size].reshape(W[n].shape)
        off += size

    def update(name, parts, base=0, stride=0):
        shp = W[name].shape
        shp3 = shp if len(shp) == 3 else (1,) + shp
        res = adamw("adamw_" + name, parts, W[name].reshape(shp3), M[name].reshape(shp3), V[name].reshape(shp3),
                    base, stride)
        grads[name], out_delta[name], out_m[name], out_v[name] = (a.reshape(shp) for a in res)

    update('ada_w', g_ada_w.reshape(1, DEPTH * D, per_l), 0, D)
    update('kv_ada_w', g_kv_ada_w[None])

    chains = {}

    def update_layer(name, parts, layer, base=0):
        shp = W[name].shape
        shp3 = shp if len(shp) == 3 else (1,) + shp
        chains[name] = adamw_layer(f"adamw_{name}_{layer}", parts, W[name].reshape(shp3), M[name].reshape(shp3),
                                   V[name].reshape(shp3), layer, chains.get(name), base)
        grads[name], out_delta[name], out_m[name], out_v[name] = (a.reshape(shp) for a in chains[name])

    ffn_parts = [[], [], []]
    for l in range(DEPTH):
        lands = exchange_wait(f"a2a_wait_{l}", sends[l], d_p, False)
        recv = [lax.dynamic_update_slice(ld, lax.dynamic_index_in_dim(src, me, 0, keepdims=True), (me,) + (0,) * (src.ndim - 1))
                for ld, src in zip(lands, sends[l][2])]
        for i in range(3):
            ffn_parts[i].append(recv[i])
        if l < N_A:
            update_layer('s5_w_glu', recv[3], l)
        else:
            update_layer('mla_w_o', recv[3], l - N_A)
            if l == N_A:
                update_layer('w_kv_b', recv[4], 0)
                update_layer('mla_w_dq', recv[4], 0, KVL)
                update_layer('w_kv_a', recv[5], 0)
            else:
                update_layer('mla_w_dq', recv[4], l - N_A)
            update_layer('mla_w_uq', recv[-1], l - N_A)
    update('ffn_w_gate', jnp.concatenate(ffn_parts[0], axis=1), 0, D)
    update('ffn_w_up', jnp.concatenate(ffn_parts[1], axis=1), 0, D)
    update('ffn_w_down', jnp.concatenate(ffn_parts[2], axis=1), 0, FFB)

    return (loss, grad_x[None], *[grads[n] for n in WEIGHT_NAMES], *[out_delta[n] for n in WEIGHT_NAMES],
            *[out_m[n] for n in WEIGHT_NAMES], *[out_v[n] for n in WEIGHT_NAMES])
```

```python
import functools
import math

import numpy as np
import jax
import jax.numpy as jnp
from jax import lax
from jax.experimental import pallas as pl
from jax.experimental.pallas import tpu as pltpu

F32 = jnp.float32
_MXU = jnp.bfloat16
HI = lax.Precision.HIGHEST

D = 1024
DEPTH = 4
N_A = 2
FF = 2816
FFB = 384
FFP = 8 * FFB
N_DEV = 8
G = 64
P = 16
N = 64
GB = 8
NBLK = G // GB
HALF = GB * N
H = 16
HP = H // 2
DN, DR, DV = 64, 32, 64
HD = 128
QL = 256
KVL = 256
CHUNK = 64
ROPE_THETA = 10000.0
ATTN_SCALE = 1.0 / math.sqrt(DN + DR)
LOG2E = 1.4426950408889634
EXP2_SCALE = ATTN_SCALE * LOG2E
EPS = 1e-6
ADAM_LR, ADAM_B1, ADAM_B2, ADAM_EPS, ADAM_WD, ADAM_STEP = 0.001, 0.9, 0.999, 1e-08, 0.01, 10
VMEM_LIMIT = 56 * 1024 * 1024
MESH = pl.DeviceIdType.MESH

TILE_ROW = 256
TILE_ATT = 512
TILE_ATT_FWD = 512
TILE_ATT_KEYS = 512
TILE_SCAN = 512


def _params(n_grid):
    return pltpu.CompilerParams(dimension_semantics=("arbitrary",) * n_grid, vmem_limit_bytes=VMEM_LIMIT)


@jax.custom_vjp
def mm(a, w):
    return jnp.dot(a.astype(_MXU), w, preferred_element_type=F32)


def _mm_fwd(a, w):
    return mm(a, w), w


def _mm_bwd(w, g):
    da = lax.dot_general(g.astype(_MXU), w, (((1,), (1,)), ((), ())), preferred_element_type=F32)
    return da, jnp.zeros_like(w)


mm.defvjp(_mm_fwd, _mm_bwd)


def rms(x, g):
    return x * lax.rsqrt(jnp.mean(x * x, axis=-1, keepdims=True) + EPS) * g


def modulate(h, shift, scale):
    return h * (1.0 + scale) + shift


def _lane(n=HD):
    return lax.broadcasted_iota(jnp.int32, (1, n), 1)


def _rot_matrix():
    r = lax.broadcasted_iota(jnp.int32, (HD, HD), 0)
    c = lax.broadcasted_iota(jnp.int32, (HD, HD), 1)
    first = (c >= DN) & (c < DN + DR // 2) & (r == c + DR // 2)
    second = (c >= DN + DR // 2) & (c < DN + DR) & (r == c - DR // 2)
    return jnp.where(first, -1.0, jnp.where(second, 1.0, 0.0)).astype(F32)


def head_norm_rope(xh, g128, cosf, sinf, rot, with_nope):
    lane = _lane()
    m_n = lane < DN
    m_r = (lane >= DN) & (lane < DN + DR)
    sq = xh * xh
    inv_r = lax.rsqrt(jnp.sum(jnp.where(m_r, sq, 0.0), axis=-1, keepdims=True) / DR + EPS)
    if with_nope:
        inv_n = lax.rsqrt(jnp.sum(jnp.where(m_n, sq, 0.0), axis=-1, keepdims=True) / DN + EPS)
        inv = jnp.where(m_n, inv_n, jnp.where(m_r, inv_r, 0.0))
    else:
        inv = jnp.where(m_r, inv_r, 0.0)
    xg = xh * inv * g128
    return xg * cosf + jnp.dot(xg, rot, precision=HI, preferred_element_type=F32) * sinf


def seg_pre(x, g, sh, sc):
    return (modulate(rms(x, g), sh, sc),), ()


def seg_ffn(x, g, sh, sc, gt, t_g, t_u, t_d, wg, wu, wd):
    h = modulate(rms(x, g), sh, sc)
    gate = mm(h, wg) + t_g
    up = mm(h, wu) + t_u
    a = jax.nn.silu(gate) * up
    y = mm(a, wd) + t_d
    return (x + gt * y,), (h.astype(_MXU), a.astype(_MXU))


def seg_glu(x, y, gt, b, t_z, w):
    g = jax.nn.gelu(y)
    z = mm(g, w) + b + t_z
    return (x + gt * (g * jax.nn.sigmoid(z)),), (g.astype(_MXU),)


def seg_o(x, o, gt, t_o, w):
    return (x + gt * (mm(o, w) + t_o),), (o.astype(_MXU),)


def seg_q(x, g, sh, sc, qg, g128, t_l, t_q, cosf, sinf, wdq, wuq):
    h = modulate(rms(x, g), sh, sc)
    ql = mm(h, wdq) + t_l
    qn = rms(ql, qg)
    q = mm(qn, wuq) + t_q
    rot = _rot_matrix()
    heads = [head_norm_rope(q[:, HD * i:HD * (i + 1)], g128, cosf, sinf, rot, True) for i in range(H)]
    return (jnp.concatenate(heads, axis=1),), (h.astype(_MXU), qn.astype(_MXU))


def seg_kv(x, g, sh, sc, ag, gkn, gkr, t_a, t_k, t_v, cosf, sinf, wa, wkn, wv):
    hk = modulate(rms(x, g), sh, sc)
    kva = mm(hk, wa) + t_a
    ckv = rms(kva[:, :KVL], ag)
    kr = head_norm_rope(kva[:, KVL:KVL + HD], gkr, cosf, sinf, _rot_matrix(), False)
    kn = mm(ckv, wkn) + t_k
    v = mm(ckv, wv) + t_v
    heads = []
    for i in range(H):
        kh = kn[:, HD * i:HD * (i + 1)]
        inv = lax.rsqrt(jnp.sum(kh * kh, axis=-1, keepdims=True) / DN + EPS)
        heads.append(kh * inv * gkn + kr)
    return (jnp.concatenate(heads, axis=1), v), (hk.astype(_MXU), ckv.astype(_MXU))


def _row_call(name, body_fn, rows, fulls, out_rows, out_accs, tile):
    s = rows[0].shape[0]
    n_tiles = s // tile
    n_rows, n_fulls, n_or, n_oa = len(rows), len(fulls), len(out_rows), len(out_accs)

    def kern(*refs):
        i = pl.program_id(0)
        row_v = [r[...] for r in refs[:n_rows]]
        full_v = [r[...] for r in refs[n_rows:n_rows + n_fulls]]
        o_refs = refs[n_rows + n_fulls:]
        ro, ao = body_fn(row_v, full_v)
        for r, v in zip(o_refs[:n_or], ro):
            r[...] = v.astype(r.dtype)
        if n_oa:
            @pl.when(i == 0)
            def _():
                for r in o_refs[n_or:]:
                    r[...] = jnp.zeros(r.shape, r.dtype)
            for r, v in zip(o_refs[n_or:], ao):
                r[...] += v.astype(r.dtype)

    in_specs = [pl.BlockSpec((tile, a.shape[1]), lambda i: (i, 0)) for a in rows]
    for a in fulls:
        big = a.size * a.dtype.itemsize > (1 << 20)
        nd = a.ndim
        in_specs.append(pl.BlockSpec(a.shape, functools.partial(lambda i, nd_: (0,) * nd_, nd_=nd),
                                     **({"pipeline_mode": pl.Buffered(1)} if big else {})))
    out_shape = [jax.ShapeDtypeStruct((s, w), dt) for w, dt in out_rows]
    out_shape += [jax.ShapeDtypeStruct(shp, dt) for shp, dt in out_accs]
    out_specs = [pl.BlockSpec((tile, w), lambda i: (i, 0)) for w, _ in out_rows]
    out_specs += [pl.BlockSpec(shp, functools.partial(lambda i, nd_: (0,) * nd_, nd_=len(shp))) for shp, _ in out_accs]
    res = pl.pallas_call(kern, out_shape=out_shape, grid=(n_tiles,), in_specs=in_specs, out_specs=out_specs,
                         name=name, compiler_params=_params(1))(*rows, *fulls)
    return list(res)


def seg_forward(name, seg, rows, smalls, consts_rows, consts_full, out_widths, tile=TILE_ROW, tap_widths=()):
    n_r, n_s, n_cr = len(rows), len(smalls), len(consts_rows)

    def body(row_v, full_v):
        t = row_v[0].shape[0]
        taps = [jnp.zeros((t, w), F32) for w in tap_widths]
        outs, _ = seg(*row_v[:n_r], *full_v[:n_s], *taps, *row_v[n_r:], *full_v[n_s:])
        return outs, ()

    return _row_call(name, body, list(rows) + list(consts_rows), list(smalls) + list(consts_full),
                     out_widths, [], tile)


def seg_backward(name, seg, rows, smalls, consts_rows, consts_full, cots, tap_widths, aux_widths,
                 dx_add=None, tile=TILE_ROW):
    n_r, n_s, n_cr, n_c = len(rows), len(smalls), len(consts_rows), len(cots)
    has_add = dx_add is not None

    def body(row_v, full_v):
        t = row_v[0].shape[0]
        prim_rows = row_v[:n_r]
        c_rows = row_v[n_r:n_r + n_cr]
        cot_v = row_v[n_r + n_cr:n_r + n_cr + n_c]
        add_v = row_v[n_r + n_cr + n_c] if has_add else None
        small_v = full_v[:n_s]
        c_full = full_v[n_s:]
        taps = [jnp.zeros((t, w), F32) for w in tap_widths]

        def f(*args):
            return seg(*args, *c_rows, *c_full)

        _, vjp_fn, aux = jax.vjp(f, *prim_rows, *small_v, *taps, has_aux=True)
        grads = vjp_fn(tuple(c.astype(F32) for c in cot_v))
        d_rows = list(grads[:n_r])
        if has_add:
            d_rows[0] = d_rows[0] + add_v
        d_small = grads[n_r:n_r + n_s]
        d_taps = grads[n_r + n_s:]
        return d_rows + list(d_taps) + list(aux), [jnp.sum(g, axis=0, keepdims=True) if g.shape[0] != 1 else g
                                                   for g in d_small]

    all_rows = list(rows) + list(consts_rows) + list(cots) + ([dx_add] if has_add else [])
    out_rows = [(a.shape[1], F32) for a in rows] + [(w, _MXU) for w in tap_widths] + [(w, _MXU) for w in aux_widths]
    out_accs = [((1, a.shape[1]), F32) for a in smalls]
    res = _row_call(name, body, all_rows, list(smalls) + list(consts_full), out_rows, out_accs, tile)
    n_t, n_a = len(tap_widths), len(aux_widths)
    return res[:n_r], res[n_r:n_r + n_t], res[n_r + n_t:n_r + n_t + n_a], res[n_r + n_t + n_a:]


def _split(n):
    if n <= 1024:
        return n
    for t in (1408, 1024, 768, 512, 256, 128):
        if n % t == 0:
            return t
    raise ValueError(n)


def matmul_tn(name, a, b, out_dtype, col_blocks=None):
    s, k1 = a.shape
    _, k2 = b.shape
    tm, ts = _split(k1), 2048
    if col_blocks is None:
        tn, per_step, wblk = _split(k2), 1, None
    else:
        wblk = k2 // col_blocks
        per_step = max(1, min(col_blocks, 1536 // wblk))
        tn = per_step * wblk
    n_s = s // ts

    def kern(a_ref, b_ref, o_ref, acc_ref):
        k = pl.program_id(2)

        @pl.when(k == 0)
        def _():
            acc_ref[...] = jnp.zeros(acc_ref.shape, F32)

        acc_ref[...] += lax.dot_general(a_ref[...], b_ref[...], (((0,), (0,)), ((), ())),
                                        preferred_element_type=F32)

        @pl.when(k == n_s - 1)
        def _():
            if col_blocks is None:
                o_ref[...] = acc_ref[...].astype(o_ref.dtype)
            else:
                for cb in range(per_step):
                    o_ref[cb] = acc_ref[:, wblk * cb:wblk * (cb + 1)].astype(o_ref.dtype)

    if col_blocks is None:
        out_shape = jax.ShapeDtypeStruct((k1, k2), out_dtype)
        out_spec = pl.BlockSpec((tm, tn), lambda i, j, k: (i, j))
    else:
        out_shape = jax.ShapeDtypeStruct((col_blocks, k1, wblk), out_dtype)
        out_spec = pl.BlockSpec((per_step, tm, wblk), lambda i, j, k: (j, i, 0))
    return pl.pallas_call(
        kern, out_shape=out_shape, grid=(k1 // tm, k2 // tn, n_s),
        in_specs=[pl.BlockSpec((ts, tm), lambda i, j, k: (k, i)), pl.BlockSpec((ts, tn), lambda i, j, k: (k, j))],
        out_specs=out_spec,
        scratch_shapes=[pltpu.VMEM((tm, tn), F32)], name=name, compiler_params=_params(3))(a, b)


def ffn_backward(name, x, dxo, g, sh, sc, gt, wg, wu, wd, tile=TILE_ROW):
    s = x.shape[0]
    blk = 2 * FFB
    n_blk = wg.shape[1] // blk

    def kern(x_ref, dxo_ref, g_ref, sh_ref, sc_ref, gt_ref, wg_ref, wu_ref, wd_ref,
             dx_ref, dg_ref, du_ref, dy_ref, h_ref, a_ref, dgn_ref, dsh_ref, dsc_ref, dgt_ref):
        i = pl.program_id(0)

        @pl.when(i == 0)
        def _():
            for r in (dgn_ref, dsh_ref, dsc_ref, dgt_ref):
                r[...] = jnp.zeros(r.shape, F32)

        dxo = dxo_ref[...]
        h, pre_vjp = jax.vjp(lambda *p: modulate(rms(p[0], p[1]), p[2], p[3]), x_ref[...], g_ref[...], sh_ref[...],
                             sc_ref[...])
        hb = h.astype(_MXU)
        h_ref[...] = hb
        dyb = (gt_ref[...] * dxo).astype(_MXU)
        dy_ref[...] = dyb
        y = jnp.zeros((tile, D), F32)
        dh = jnp.zeros((tile, D), F32)
        tr = (((1,), (1,)), ((), ()))
        for c in range(n_blk):
            cs = slice(blk * c, blk * (c + 1))
            gate = jnp.dot(hb, wg_ref[:, cs], preferred_element_type=F32)
            up = jnp.dot(hb, wu_ref[:, cs], preferred_element_type=F32)
            sig = jax.nn.sigmoid(gate)
            sl = gate * sig
            ab = (sl * up).astype(_MXU)
            a_ref[:, cs] = ab
            y = y + jnp.dot(ab, wd_ref[cs, :], preferred_element_type=F32)
            da = lax.dot_general(dyb, wd_ref[cs, :], tr, preferred_element_type=F32)
            dgb = (da * up * (sig * (1.0 + gate * (1.0 - sig)))).astype(_MXU)
            dub = (da * sl).astype(_MXU)
            dg_ref[:, cs] = dgb
            du_ref[:, cs] = dub
            dh = dh + lax.dot_general(dgb, wg_ref[:, cs], tr, preferred_element_type=F32) \
                + lax.dot_general(dub, wu_ref[:, cs], tr, preferred_element_type=F32)
        dgt_ref[...] += jnp.sum(dxo * y, axis=0, keepdims=True)
        dx_pre, dgn, dsh, dsc = pre_vjp(dh)
        dx_ref[...] = dxo + dx_pre
        dgn_ref[...] += dgn
        dsh_ref[...] += dsh
        dsc_ref[...] += dsc

    row = lambda w: pl.BlockSpec((tile, w), lambda i: (i, 0))
    vec = pl.BlockSpec((1, D), lambda i: (0, 0))
    wspec = lambda a: pl.BlockSpec(a.shape, lambda i: (0, 0), pipeline_mode=pl.Buffered(1))
    rows_out = [(D, F32), (wg.shape[1], _MXU), (wg.shape[1], _MXU), (D, _MXU), (D, _MXU), (wg.shape[1], _MXU)]
    res = pl.pallas_call(
        kern,
        out_shape=[jax.ShapeDtypeStruct((s, w), dt) for w, dt in rows_out] + [jax.ShapeDtypeStruct((1, D), F32)] * 4,
        grid=(s // tile,),
        in_specs=[row(D), row(D), vec, vec, vec, vec, wspec(wg), wspec(wu), wspec(wd)],
        out_specs=[row(w) for w, _ in rows_out] + [vec] * 4,
        name=name, compiler_params=_params(1))(x, dxo, g, sh, sc, gt, wg, wu, wd)
    return res


def small_matmul(name, a, w, tn=256):
    m, k = a.shape
    n = w.shape[1]

    def kern(a_ref, w_ref, o_ref):
        o_ref[...] = jnp.dot(a_ref[...].astype(_MXU), w_ref[...].astype(_MXU), preferred_element_type=F32)

    return pl.pallas_call(kern, out_shape=jax.ShapeDtypeStruct((m, n), F32), grid=(n // tn,),
                          in_specs=[pl.BlockSpec((m, k), lambda j: (0, 0)), pl.BlockSpec((k, tn), lambda j: (0, j))],
                          out_specs=pl.BlockSpec((m, tn), lambda j: (0, j)), name=name,
                          compiler_params=_params(1))(a, w)


def small_matmul_tn(name, a, b, tn=256):
    m, k = a.shape
    n = b.shape[1]

    def kern(a_ref, b_ref, o_ref):
        o_ref[...] = lax.dot_general(a_ref[...].astype(_MXU), b_ref[...].astype(_MXU), (((0,), (0,)), ((), ())),
                                     preferred_element_type=F32)

    return pl.pallas_call(kern, out_shape=jax.ShapeDtypeStruct((k, n), F32), grid=(n // tn,),
                          in_specs=[pl.BlockSpec((m, k), lambda j: (0, 0)), pl.BlockSpec((m, tn), lambda j: (0, j))],
                          out_specs=pl.BlockSpec((k, tn), lambda j: (0, j)), name=name,
                          compiler_params=_params(1))(a, b)


def _s5_prep_math(lam_re, lam_im, log_dt, b_re_t, b_im_t, expand):
    dt = jnp.dot(jnp.exp(log_dt), expand, precision=HI, preferred_element_type=F32)
    mag = jnp.exp(lam_re * dt)
    ab_re = mag * jnp.cos(lam_im * dt)
    ab_im = mag * jnp.sin(lam_im * dt)
    den = lam_re * lam_re + lam_im * lam_im
    nr = ab_re - 1.0
    ni = ab_im
    f_re = (nr * lam_re + ni * lam_im) / den
    f_im = (ni * lam_re - nr * lam_im) / den
    bb_re = f_re * b_re_t - f_im * b_im_t
    bb_im = f_re * b_im_t + f_im * b_re_t
    return ab_re, ab_im, bb_re, bb_im


def _whole(kern, name, out_shape, *args):
    return pl.pallas_call(kern, out_shape=out_shape, name=name,
                          compiler_params=pltpu.CompilerParams(vmem_limit_bytes=VMEM_LIMIT))(*args)


def s5_prep_fwd(name, lam_re, lam_im, log_dt, b_re_t, b_im_t, expand):
    def kern(a, b, c, d, e, f, o0, o1, o2, o3):
        r = _s5_prep_math(a[...], b[...], c[...], d[...], e[...], f[...])
        for o, v in zip((o0, o1, o2, o3), r):
            o[...] = v

    gn = lam_re.shape[1]
    shp = [jax.ShapeDtypeStruct((1, gn), F32)] * 2 + [jax.ShapeDtypeStruct((P, gn), F32)] * 2
    return _whole(kern, name, shp, lam_re, lam_im, log_dt, b_re_t, b_im_t, expand)


def s5_prep_bwd(name, lam_re, lam_im, log_dt, b_re_t, b_im_t, expand, cots):
    def kern(a, b, c, d, e, f, c0, c1, c2, c3, o0, o1, o2, o3, o4):
        ex = f[...]
        _, vjp_fn = jax.vjp(lambda *p: _s5_prep_math(*p, ex), a[...], b[...], c[...], d[...], e[...])
        g = vjp_fn((c0[...], c1[...], c2[...], c3[...]))
        for o, v in zip((o0, o1, o2, o3, o4), g):
            o[...] = v

    shp = [jax.ShapeDtypeStruct(a.shape, F32) for a in (lam_re, lam_im, log_dt, b_re_t, b_im_t)]
    return _whole(kern, name, shp, lam_re, lam_im, log_dt, b_re_t, b_im_t, expand, *cots)


def _cpowers(ar, ai):
    pw = [(ar, ai)]
    for _ in range(7):
        pr, pi = pw[-1]
        pw.append((pr * ar - pi * ai, pr * ai + pi * ar))
    return pw


def _row_select(row, values):
    out = jnp.broadcast_to(values[7], (8, values[7].shape[1]))
    for r in range(6, -1, -1):
        out = jnp.where(row == r, values[r], out)
    return out


def _scan_tables(ar, ai, reverse):
    pw = _cpowers(ar, ai)
    row = lax.broadcasted_iota(jnp.int32, (8, ar.shape[1]), 0)
    steps = []
    for d in (1, 2, 4):
        keep = (row <= 7 - d) if reverse else (row >= d)
        steps.append((jnp.where(keep, pw[d - 1][0], 0.0), jnp.where(keep, pw[d - 1][1], 0.0)))
    order = list(range(7, -1, -1)) if reverse else list(range(8))
    carry = (_row_select(row, [pw[i][0] for i in order]), _row_select(row, [pw[i][1] for i in order]))
    return steps, carry


def _tile_scan_fwd(xr, xi, cr, ci, steps, carry_m):
    for d, (mr, mi) in zip((1, 2, 4), steps):
        sr = pltpu.roll(xr, d, 0)
        si = pltpu.roll(xi, d, 0)
        xr, xi = xr + mr * sr - mi * si, xi + mr * si + mi * sr
    pr, pi = carry_m
    return xr + pr * cr - pi * ci, xi + pr * ci + pi * cr


def _tile_scan_rev(xr, xi, cr, ci, steps, carry_m):
    for d, (mr, mi) in zip((1, 2, 4), steps):
        sr = pltpu.roll(xr, 8 - d, 0)
        si = pltpu.roll(xi, 8 - d, 0)
        xr, xi = xr + mr * sr + mi * si, xi + mr * si - mi * sr
    pr, pi = carry_m
    return xr + pr * cr + pi * ci, xi + pr * ci - pi * cr


def _fwd_scan_block(buf, row0, n_tiles8, ar, ai, c0r, c0i):
    steps, carry_m = _scan_tables(ar, ai, False)

    def body(j, carry):
        cr, ci = carry
        r0 = pl.multiple_of(row0 + j * 8, 8)
        xr = buf[pl.ds(r0, 8), 0:HALF]
        xi = buf[pl.ds(r0, 8), HALF:2 * HALF]
        xr, xi = _tile_scan_fwd(xr, xi, cr, ci, steps, carry_m)
        buf[pl.ds(r0, 8), 0:HALF] = xr
        buf[pl.ds(r0, 8), HALF:2 * HALF] = xi
        return xr[7:8], xi[7:8]

    return lax.fori_loop(0, n_tiles8, body, (c0r, c0i))


def s5_scan_fwd(name, h, wb, wc, a_tab, dskip, tile=TILE_SCAN):
    s = h.shape[0]
    n_t = s // tile

    def kern(h_ref, wb_ref, wc_ref, a_ref, d_ref, y_ref, s0_ref, carry_ref, buf):
        i = pl.program_id(0)

        @pl.when(i == 0)
        def _():
            carry_ref[...] = jnp.zeros(carry_ref.shape, F32)

        s0_ref[0] = carry_ref[...]
        for k in range(NBLK):
            cols = slice(GB * P * k, GB * P * (k + 1))
            u = h_ref[:, cols]
            buf[...] = jnp.dot(u.astype(_MXU), wb_ref[k], preferred_element_type=F32)
            ar = a_ref[k, :, 0:HALF]
            ai = a_ref[k, :, HALF:2 * HALF]
            cr, ci = _fwd_scan_block(buf, 0, tile // 8, ar, ai, carry_ref[k:k + 1, 0:HALF],
                                     carry_ref[k:k + 1, HALF:2 * HALF])
            carry_ref[k:k + 1, 0:HALF] = cr
            carry_ref[k:k + 1, HALF:2 * HALF] = ci
            y_ref[:, cols] = jnp.dot(buf[...].astype(_MXU), wc_ref[k], preferred_element_type=F32) + d_ref[:, cols] * u

    full = lambda a: pl.BlockSpec(a.shape, functools.partial(lambda i, nd_: (0,) * nd_, nd_=a.ndim))
    return pl.pallas_call(
        kern,
        out_shape=[jax.ShapeDtypeStruct((s, D), F32), jax.ShapeDtypeStruct((n_t, NBLK, 2 * HALF), F32)],
        grid=(n_t,),
        in_specs=[pl.BlockSpec((tile, D), lambda i: (i, 0)), full(wb), full(wc), full(a_tab), full(dskip)],
        out_specs=[pl.BlockSpec((tile, D), lambda i: (i, 0)), pl.BlockSpec((1, NBLK, 2 * HALF), lambda i: (i, 0, 0))],
        scratch_shapes=[pltpu.VMEM((NBLK, 2 * HALF), F32), pltpu.VMEM((tile, 2 * HALF), F32)],
        name=name, compiler_params=_params(1))(h, wb, wc, a_tab, dskip)


def s5_scan_bwd(name, h, dy, s0, wb, wc, a_tab, dskip, tile=TILE_SCAN):
    s = h.shape[0]
    n_t = s // tile
    n8 = tile // 8

    def kern(h_ref, dy_ref, s0_ref, wb_ref, wc_ref, a_ref, d_ref, dh_ref, dwb_ref, dwc_ref, da_ref, dd_ref,
             lam_ref, sbuf, gbuf):
        i = pl.program_id(0)

        @pl.when(i == 0)
        def _():
            lam_ref[...] = jnp.zeros(lam_ref.shape, F32)
            dwb_ref[...] = jnp.zeros(dwb_ref.shape, F32)
            dwc_ref[...] = jnp.zeros(dwc_ref.shape, F32)
            da_ref[...] = jnp.zeros(da_ref.shape, F32)
            dd_ref[...] = jnp.zeros(dd_ref.shape, F32)

        for k in range(NBLK):
            cols = slice(GB * P * k, GB * P * (k + 1))
            u = h_ref[:, cols]
            dyk = dy_ref[:, cols]
            ar = a_ref[k, :, 0:HALF]
            ai = a_ref[k, :, HALF:2 * HALF]
            sbuf[0:8, :] = jnp.broadcast_to(s0_ref[0, k:k + 1, :], (8, 2 * HALF))
            sbuf[8:tile + 8, :] = jnp.dot(u.astype(_MXU), wb_ref[k], preferred_element_type=F32)
            _fwd_scan_block(sbuf, 8, n8, ar, ai, s0_ref[0, k:k + 1, 0:HALF], s0_ref[0, k:k + 1, HALF:2 * HALF])
            dyb = dyk.astype(_MXU)
            gbuf[...] = lax.dot_general(dyb, wc_ref[k], (((1,), (1,)), ((), ())), preferred_element_type=F32)
            dwc_ref[k] += lax.dot_general(sbuf[8:tile + 8, :].astype(_MXU), dyb, (((0,), (0,)), ((), ())),
                                          preferred_element_type=F32)
            steps, carry_m = _scan_tables(ar, ai, True)
            row = lax.broadcasted_iota(jnp.int32, (8, HALF), 0)

            def body(jj, carry):
                cr, ci, dar, dai = carry
                j = n8 - 1 - jj
                r0 = pl.multiple_of(j * 8, 8)
                xr = gbuf[pl.ds(r0, 8), 0:HALF]
                xi = gbuf[pl.ds(r0, 8), HALF:2 * HALF]
                xr, xi = _tile_scan_rev(xr, xi, cr, ci, steps, carry_m)
                gbuf[pl.ds(r0, 8), 0:HALF] = xr
                gbuf[pl.ds(r0, 8), HALF:2 * HALF] = xi
                r1 = pl.multiple_of(j * 8 + 8, 8)
                spr = jnp.where(row == 0, sbuf[pl.ds(r0, 8), 0:HALF][7:8],
                                pltpu.roll(sbuf[pl.ds(r1, 8), 0:HALF], 1, 0))
                spi = jnp.where(row == 0, sbuf[pl.ds(r0, 8), HALF:2 * HALF][7:8],
                                pltpu.roll(sbuf[pl.ds(r1, 8), HALF:2 * HALF], 1, 0))
                dar = dar + xr * spr + xi * spi
                dai = dai + xi * spr - xr * spi
                return xr[0:1], xi[0:1], dar, dai

            z8 = jnp.zeros((8, HALF), F32)
            cr, ci, dar, dai = lax.fori_loop(
                0, n8, body, (lam_ref[k:k + 1, 0:HALF], lam_ref[k:k + 1, HALF:2 * HALF], z8, z8))
            lam_ref[k:k + 1, 0:HALF] = cr
            lam_ref[k:k + 1, HALF:2 * HALF] = ci
            da_ref[k:k + 1, 0:HALF] += jnp.sum(dar, axis=0, keepdims=True)
            da_ref[k:k + 1, HALF:2 * HALF] += jnp.sum(dai, axis=0, keepdims=True)
            lam = gbuf[...].astype(_MXU)
            dwb_ref[k] += lax.dot_general(u.astype(_MXU), lam, (((0,), (0,)), ((), ())), preferred_element_type=F32)
            du = lax.dot_general(lam, wb_ref[k], (((1,), (1,)), ((), ())), preferred_element_type=F32)
            dh_ref[:, cols] = du + d_ref[:, cols] * dyk
            dd_ref[:, cols] += jnp.sum(dyk * u, axis=0, keepdims=True)

    full = lambda a: pl.BlockSpec(a.shape, functools.partial(lambda i, nd_: (0,) * nd_, nd_=a.ndim))
    fullo = lambda shp: pl.BlockSpec(shp, functools.partial(lambda i, nd_: (0,) * nd_, nd_=len(shp)))
    rev = lambda i: (n_t - 1 - i, 0)
    return pl.pallas_call(
        kern,
        out_shape=[jax.ShapeDtypeStruct((s, D), F32), jax.ShapeDtypeStruct(wb.shape, F32),
                   jax.ShapeDtypeStruct(wc.shape, F32), jax.ShapeDtypeStruct((NBLK, 2 * HALF), F32),
                   jax.ShapeDtypeStruct((1, D), F32)],
        grid=(n_t,),
        in_specs=[pl.BlockSpec((tile, D), rev), pl.BlockSpec((tile, D), rev),
                  pl.BlockSpec((1, NBLK, 2 * HALF), lambda i: (n_t - 1 - i, 0, 0)),
                  full(wb), full(wc), full(a_tab), full(dskip)],
        out_specs=[pl.BlockSpec((tile, D), rev), fullo(wb.shape), fullo(wc.shape), fullo((NBLK, 2 * HALF)),
                   fullo((1, D))],
        scratch_shapes=[pltpu.VMEM((NBLK, 2 * HALF), F32), pltpu.VMEM((tile + 8, 2 * HALF), F32),
                        pltpu.VMEM((tile, 2 * HALF), F32)],
        name=name, compiler_params=_params(1))(h, dy, s0, wb, wc, a_tab, dskip)


def _chunk_mask(q0, k0, tq, tk):
    r = (q0 + lax.broadcasted_iota(jnp.int32, (tq, tk), 0)) // CHUNK
    c = (k0 + lax.broadcasted_iota(jnp.int32, (tq, tk), 1)) // CHUNK
    return r >= c


def _head_lanes(j):
    lane = _lane(2 * DV)
    return (lane >= DV * j) & (lane < DV * (j + 1))


def _raw_scores(q, kblk, masked, t):
    s = lax.dot_general(q, kblk, (((1,), (1,)), ((), ())), preferred_element_type=F32)
    return jnp.where(_chunk_mask(0, 0, t, t), s, -1e30) if masked else s


def attn_fwd(name, q, k, v, t=TILE_ATT_FWD, tk=TILE_ATT_KEYS):
    s = q.shape[0]
    n_q = s // t
    r = t // tk

    def kern(q_ref, k_ref, v_ref, o_ref, lse_ref):
        qi = pl.program_id(1)
        qs = [q_ref[:, HD * j:HD * (j + 1)] for j in range(2)]

        def absorb(k0, carry, mask):
            vblk = v_ref[pl.ds(k0, tk), :]
            scs = [lax.dot_general(qs[j], k_ref[pl.ds(k0, tk), HD * j:HD * (j + 1)], (((1,), (1,)), ((), ())),
                                   preferred_element_type=F32) for j in range(2)]
            if mask is not None:
                scs = [jnp.where(mask, sc, -1e30) for sc in scs]
            m_new = [jnp.maximum(carry[j][0], jnp.max(scs[j], axis=-1, keepdims=True)) for j in range(2)]
            ps = [jnp.exp2((scs[j] - m_new[j]) * EXP2_SCALE) for j in range(2)]
            alphas = [jnp.exp2((carry[j][0] - m_new[j]) * EXP2_SCALE) for j in range(2)]
            pvs = [jnp.dot(ps[j].astype(_MXU), vblk, preferred_element_type=F32) for j in range(2)]
            return tuple((m_new[j], alphas[j] * carry[j][1] + jnp.sum(ps[j], axis=-1, keepdims=True),
                          alphas[j] * carry[j][2] + pvs[j]) for j in range(2))

        init = tuple((jnp.full((t, 1), -1e30, F32), jnp.zeros((t, 1), F32), jnp.zeros((t, 2 * DV), F32))
                     for _ in range(2))
        carry = lax.fori_loop(0, qi * r, lambda kb, c: absorb(pl.multiple_of(kb * tk, tk), c, None), init)
        for i in range(r):
            carry = absorb(pl.multiple_of(qi * t + i * tk, tk), carry, _chunk_mask(0, i * tk, t, tk))
        outs = []
        for j in range(2):
            m, l, acc = carry[j]
            outs.append(acc / l)
            lse_ref[0, j] = m * ATTN_SCALE + jnp.log(l)
        o_ref[...] = jnp.where(_head_lanes(0), outs[0], outs[1])

    return pl.pallas_call(
        kern,
        out_shape=[jax.ShapeDtypeStruct((s, H * DV), F32), jax.ShapeDtypeStruct((HP, 2, s, 1), F32)],
        grid=(HP, n_q),
        in_specs=[pl.BlockSpec((t, 2 * HD), lambda hp, i: (i, hp)), pl.BlockSpec((s, 2 * HD), lambda hp, i: (0, hp)),
                  pl.BlockSpec((s, 2 * DV), lambda hp, i: (0, hp))],
        out_specs=[pl.BlockSpec((t, 2 * DV), lambda hp, i: (i, hp)),
                   pl.BlockSpec((1, 2, t, 1), lambda hp, i: (hp, 0, i, 0))],
        name=name, compiler_params=_params(2))(q, k, v)


def attn_bwd(name, q, k, v, o, do, lse, t=TILE_ATT):
    s = q.shape[0]
    n_q = s // t

    def kern(q_ref, k_ref, v_ref, o_ref, do_ref, lse_ref, dq_ref, dk_ref, dv_ref):
        qi = pl.program_id(1)

        @pl.when(qi == 0)
        def _():
            dk_ref[...] = jnp.zeros(dk_ref.shape, F32)
            dv_ref[...] = jnp.zeros(dv_ref.shape, F32)

        qs, doms, deltas, lse2 = [], [], [], []
        for j in range(2):
            qs.append(q_ref[:, HD * j:HD * (j + 1)])
            dom = jnp.where(_head_lanes(j), do_ref[...], 0.0)
            deltas.append(jnp.sum(dom * o_ref[...], axis=-1, keepdims=True))
            doms.append(dom.astype(_MXU))
            lse2.append(lse_ref[0, j] * LOG2E)

        def block(k0, dqs, masked):
            vblk = v_ref[pl.ds(k0, t), :]
            kblks = [k_ref[pl.ds(k0, t), HD * j:HD * (j + 1)] for j in range(2)]
            scs = [_raw_scores(qs[j], kblks[j], masked, t) for j in range(2)]
            dps = [lax.dot_general(doms[j], vblk, (((1,), (1,)), ((), ())), preferred_element_type=F32)
                   for j in range(2)]
            ps = [jnp.exp2(scs[j] * EXP2_SCALE - lse2[j]) for j in range(2)]
            dss = [(ps[j] * (dps[j] - deltas[j])).astype(_MXU) for j in range(2)]
            pbs = [ps[j].astype(_MXU) for j in range(2)]
            new = tuple(dqs[j] + jnp.dot(dss[j], kblks[j], preferred_element_type=F32) for j in range(2))
            for j in range(2):
                dk_ref[pl.ds(k0, t), HD * j:HD * (j + 1)] += lax.dot_general(
                    dss[j], qs[j], (((0,), (0,)), ((), ())), preferred_element_type=F32)
            dvs = [lax.dot_general(pbs[j], doms[j], (((0,), (0,)), ((), ())), preferred_element_type=F32)
                   for j in range(2)]
            dv_ref[pl.ds(k0, t), :] += dvs[0] + dvs[1]
            return new

        init = (jnp.zeros((t, HD), F32), jnp.zeros((t, HD), F32))
        dqs = lax.fori_loop(0, qi, lambda kb, c: block(pl.multiple_of(kb * t, t), c, False), init)
        dqs = block(pl.multiple_of(qi * t, t), dqs, True)
        for j in range(2):
            dq_ref[:, HD * j:HD * (j + 1)] = dqs[j] * ATTN_SCALE

        @pl.when(qi == n_q - 1)
        def _():
            dk_ref[...] = dk_ref[...] * ATTN_SCALE

    return pl.pallas_call(
        kern,
        out_shape=[jax.ShapeDtypeStruct((s, H * HD), F32), jax.ShapeDtypeStruct((s, H * HD), F32),
                   jax.ShapeDtypeStruct((s, H * DV), F32)],
        grid=(HP, n_q),
        in_specs=[pl.BlockSpec((t, 2 * HD), lambda hp, i: (i, hp)), pl.BlockSpec((s, 2 * HD), lambda hp, i: (0, hp)),
                  pl.BlockSpec((s, 2 * DV), lambda hp, i: (0, hp)), pl.BlockSpec((t, 2 * DV), lambda hp, i: (i, hp)),
                  pl.BlockSpec((t, 2 * DV), lambda hp, i: (i, hp)),
                  pl.BlockSpec((1, 2, t, 1), lambda hp, i: (hp, 0, i, 0))],
        out_specs=[pl.BlockSpec((t, 2 * HD), lambda hp, i: (i, hp)), pl.BlockSpec((s, 2 * HD), lambda hp, i: (0, hp)),
                   pl.BlockSpec((s, 2 * DV), lambda hp, i: (0, hp))],
        name=name, compiler_params=_params(2))(q, k, v, o, do, lse)


def rope_tables(name, pos_col, inv128):
    s = pos_col.shape[0]

    def kern(p_ref, inv_ref, c_ref, s_ref):
        ang = p_ref[...].astype(F32) * inv_ref[...]
        lane = _lane()
        m_r = (lane >= DN) & (lane < DN + DR)
        c_ref[...] = jnp.where(lane < DN, 1.0, jnp.where(m_r, jnp.cos(ang), 0.0))
        s_ref[...] = jnp.where(m_r, jnp.sin(ang), 0.0)

    return _whole(kern, name, [jax.ShapeDtypeStruct((s, HD), F32)] * 2, pos_col, inv128)


def loss_kernel(name, y, tgt, tile=TILE_ROW):
    def body(row_v, _):
        err = row_v[0] - row_v[1]
        part = 0.5 * jnp.sum(jnp.mean(err * err, axis=-1, keepdims=True), axis=0, keepdims=True)
        return [err * (1.0 / D)], [jnp.broadcast_to(part, (1, 128))]

    return _row_call(name, body, [y, tgt], [], [(D, F32)], [((1, 128), F32)], tile)


def _row_tile(r, c):
    cap = max(8, (1 << 18) // max(c, 1))
    for t in (2048, 1024, 512, 256, 128, 64, 32, 16, 8):
        if t <= cap and r % t == 0:
            return t
    return r


def sum_parts(name, parts):
    n, r, c = parts.shape
    t = _row_tile(r, c)

    def kern(p_ref, o_ref):
        acc = p_ref[0].astype(F32)
        for i in range(1, n):
            acc = acc + p_ref[i].astype(F32)
        o_ref[...] = acc

    return pl.pallas_call(kern, out_shape=jax.ShapeDtypeStruct((r, c), F32), grid=(r // t,),
                          in_specs=[pl.BlockSpec((n, t, c), lambda i: (0, i, 0))],
                          out_specs=pl.BlockSpec((t, c), lambda i: (i, 0)), name=name, compiler_params=_params(1))(parts)


def adamw(name, parts, w, m, v, base=0, stride=0):
    n, _, cp = parts.shape
    nl, r, c = w.shape
    t = _row_tile(math.gcd(math.gcd(r, base), stride), max(c, cp))
    c1 = 1.0 / (1.0 - ADAM_B1 ** ADAM_STEP)
    c2 = 1.0 / (1.0 - ADAM_B2 ** ADAM_STEP)

    def kern(p_ref, w_ref, m_ref, v_ref, g_ref, d_ref, nm_ref, nv_ref):
        g = p_ref[0].astype(F32)
        for i in range(1, n):
            g = g + p_ref[i].astype(F32)
        g = g[:, :c]
        nm = ADAM_B1 * m_ref[...] + (1.0 - ADAM_B1) * g
        nv = ADAM_B2 * v_ref[...] + (1.0 - ADAM_B2) * (g * g)
        g_ref[...] = g
        nm_ref[...] = nm
        nv_ref[...] = nv
        d_ref[...] = -ADAM_LR * ((nm * c1) / (jnp.sqrt(nv * c2) + ADAM_EPS) + ADAM_WD * w_ref[...])

    spec = pl.BlockSpec((None, t, c), lambda l, i: (l, i, 0))
    pspec = pl.BlockSpec((n, t, cp), lambda l, i: (0, (base + l * stride) // t + i, 0))
    return pl.pallas_call(kern, out_shape=[jax.ShapeDtypeStruct((nl, r, c), F32)] * 4, grid=(nl, r // t),
                          in_specs=[pspec, spec, spec, spec], out_specs=[spec] * 4, name=name,
                          compiler_params=_params(2))(parts, w, m, v)


def adamw_layer(name, parts, w, m, v, layer, prev, base=0):
    n, _, cp = parts.shape
    nl, r, c = w.shape
    t = _row_tile(math.gcd(r, base), max(c, cp))
    c1 = 1.0 / (1.0 - ADAM_B1 ** ADAM_STEP)
    c2 = 1.0 / (1.0 - ADAM_B2 ** ADAM_STEP)
    chained = nl > 1

    def kern(p_ref, w_ref, m_ref, v_ref, *rest):
        g_ref, d_ref, nm_ref, nv_ref = rest[-4:]
        g = p_ref[0].astype(F32)
        for i in range(1, n):
            g = g + p_ref[i].astype(F32)
        g = g[:, :c]
        nm = ADAM_B1 * m_ref[...] + (1.0 - ADAM_B1) * g
        nv = ADAM_B2 * v_ref[...] + (1.0 - ADAM_B2) * (g * g)
        g_ref[...] = g
        nm_ref[...] = nm
        nv_ref[...] = nv
        d_ref[...] = -ADAM_LR * ((nm * c1) / (jnp.sqrt(nv * c2) + ADAM_EPS) + ADAM_WD * w_ref[...])

    spec = pl.BlockSpec((None, t, c), lambda i: (layer, i, 0))
    pspec = pl.BlockSpec((n, t, cp), lambda i: (0, base // t + i, 0))
    in_specs = [pspec, spec, spec, spec]
    args = [parts, w, m, v]
    aliases = {}
    if chained:
        if prev is None:
            prev = [lax.empty((nl, r, c), F32) for _ in range(4)]
        in_specs += [pl.BlockSpec(memory_space=pl.ANY)] * 4
        args += list(prev)
        aliases = {4 + i: i for i in range(4)}
    return pl.pallas_call(kern, out_shape=[jax.ShapeDtypeStruct((nl, r, c), F32)] * 4, grid=(r // t,),
                          in_specs=in_specs, out_specs=[spec] * 4, input_output_aliases=aliases, name=name,
                          compiler_params=_params(1))(*args)


def _me():
    return lax.axis_index("x"), lax.axis_index("y"), lax.axis_index("c")


def _flip(x, y, c, mask):
    return (jnp.where((mask >> 2) & 1, 1 - x, x), jnp.where((mask >> 1) & 1, 1 - y, y), jnp.where(mask & 1, 1 - c, c))


def _index(x, y, c):
    return 4 * x + 2 * y + c


def _exchange(name, arr, gather):
    out_shape = (N_DEV,) + arr.shape if gather else arr.shape

    def kern(in_ref, out_ref, send_sems, recv_sems, local_sem):
        x, y, c = _me()
        me = _index(x, y, c)
        mine = pltpu.make_async_copy(in_ref if gather else in_ref.at[me], out_ref.at[me], local_sem)
        mine.start()
        copies = []
        for mask in range(1, N_DEV):
            px, py, pc = _flip(x, y, c, mask)
            peer = _index(px, py, pc)
            cp = pltpu.make_async_remote_copy(
                src_ref=in_ref if gather else in_ref.at[peer], dst_ref=out_ref.at[me],
                send_sem=send_sems.at[mask - 1], recv_sem=recv_sems.at[mask - 1],
                device_id=(px, py, pc), device_id_type=MESH)
            cp.start()
            copies.append((cp, peer))
        for mask, (cp, peer) in enumerate(copies, start=1):
            pltpu.make_async_remote_copy(
                src_ref=in_ref if gather else in_ref.at[peer], dst_ref=out_ref.at[peer],
                send_sem=send_sems.at[mask - 1], recv_sem=recv_sems.at[mask - 1],
                device_id=_flip(x, y, c, mask), device_id_type=MESH).wait_recv()
        for cp, _ in copies:
            cp.wait_send()
        mine.wait()

    any_spec = pl.BlockSpec(memory_space=pl.ANY)
    return pl.pallas_call(
        kern, out_shape=jax.ShapeDtypeStruct(out_shape, arr.dtype), in_specs=[any_spec], out_specs=any_spec,
        scratch_shapes=[pltpu.SemaphoreType.DMA((N_DEV - 1,)), pltpu.SemaphoreType.DMA((N_DEV - 1,)),
                        pltpu.SemaphoreType.DMA],
        name=name, compiler_params=pltpu.CompilerParams(has_side_effects=True))(arr)


def all_gather(name, arr):
    return _exchange(name, arr, True)


def all_to_all(name, arr):
    return _exchange(name, arr, False)


_HBM = pl.BlockSpec(memory_space=pltpu.HBM)
_SEM = pl.BlockSpec(memory_space=pltpu.SEMAPHORE)
_EFFECT = pltpu.SideEffectType.DATAFLOW_SIDE_EFFECTING


def _split_copies(srcs, lands, send_sems, recv_sems, gather):
    x, y, c = _me()
    me = _index(x, y, c)
    out = []
    for a, (src, land) in enumerate(zip(srcs, lands)):
        for mask in range(1, N_DEV):
            px, py, pc = _flip(x, y, c, mask)
            peer = _index(px, py, pc)
            sem = (N_DEV - 1) * a + mask - 1
            mk = lambda dst_slot: pltpu.make_async_remote_copy(
                src_ref=src if gather else src.at[peer], dst_ref=land.at[dst_slot],
                send_sem=send_sems.at[sem], recv_sem=recv_sems.at[sem], device_id=(px, py, pc), device_id_type=MESH)
            out.append((mk(me), mk(peer)))
    return out


def exchange_start(name, arrs, gather, after):
    k = len(arrs)
    land_shapes = [((N_DEV,) + a.shape if gather else a.shape) for a in arrs]

    def body(*refs):
        srcs, lands = refs[:k], refs[k:2 * k]
        send_sems, recv_sems = refs[2 * k + 1], refs[2 * k + 2]
        token = refs[-1]
        for mine, _ in _split_copies(srcs, lands, send_sems, recv_sems, gather):
            mine.start()
        token[...] = jnp.zeros(token.shape, token.dtype)

    n_sem = (N_DEV - 1) * k
    res = pl.pallas_call(
        body, name=name,
        out_shape=(pltpu.SemaphoreType.DMA((n_sem,)), pltpu.SemaphoreType.DMA((n_sem,)),
                   *[pltpu.HBM(a.shape, a.dtype) for a in arrs],
                   *[pltpu.HBM(shp, a.dtype) for shp, a in zip(land_shapes, arrs)],
                   jax.ShapeDtypeStruct((8, 128), F32)),
        in_specs=[_HBM] * (2 * k) + [pl.BlockSpec(memory_space=pl.ANY)],
        out_specs=(_SEM, _SEM, *[_HBM] * (2 * k), pl.BlockSpec(memory_space=pltpu.VMEM)),
        input_output_aliases={i: 2 + i for i in range(2 * k)},
        compiler_params=pltpu.CompilerParams(has_side_effects=_EFFECT),
    )(*[pltpu.with_memory_space_constraint(a, pltpu.HBM) for a in arrs],
      *[pltpu.with_memory_space_constraint(lax.empty(shp, a.dtype), pltpu.HBM) for shp, a in zip(land_shapes, arrs)],
      after)
    return res[0], res[1], list(res[2:2 + k]), list(res[2 + k:2 + 2 * k]), res[-1]


def exchange_wait(name, started, after, gather):
    send_sems, recv_sems, thrus, lands, _ = started
    k = len(thrus)

    def body(*refs):
        srcs, lnds = refs[:k], refs[k:2 * k]
        s_sems, r_sems = refs[2 * k], refs[2 * k + 1]
        for mine, theirs in _split_copies(srcs, lnds, s_sems, r_sems, gather):
            mine.wait_send()
            theirs.wait_recv()

    res = pl.pallas_call(
        body, name=name,
        out_shape=tuple(pltpu.HBM(a.shape, a.dtype) for a in thrus + lands),
        in_specs=[_HBM] * (2 * k) + [_SEM, _SEM, pl.BlockSpec(memory_space=pl.ANY)], out_specs=tuple([_HBM] * (2 * k)),
        input_output_aliases={i: i for i in range(2 * k)},
        compiler_params=pltpu.CompilerParams(has_side_effects=_EFFECT),
    )(*thrus, *lands, send_sems, recv_sems, after)
    return list(res[k:])


def _pad_heads(w, real, padded):
    k = w.shape[0]
    w3 = w.reshape(k, H, real)
    return jnp.pad(w3, ((0, 0), (0, 0), (0, padded - real))).reshape(k, H * padded)


def _unpad_heads(w, real, padded):
    k = w.shape[0]
    return w.reshape(k, H, padded)[:, :, :real].reshape(k, H * real)


def _s5_place(ab_re, ab_im, bb_re_t, bb_im_t, c_re, c_im):
    eye = jnp.eye(GB, dtype=F32)

    def wb_part(bt):
        x4 = bt.reshape(P, NBLK, GB, N).transpose(1, 2, 0, 3)
        return jnp.einsum('kgpn,gh->kgphn', x4, eye).reshape(NBLK, GB * P, HALF)

    def wc_part(cc):
        x4 = cc.reshape(NBLK, GB, P, N)
        return jnp.einsum('kgpn,gh->kgnhp', x4, eye).reshape(NBLK, HALF, GB * P)

    wb = jnp.concatenate([wb_part(bb_re_t), wb_part(bb_im_t)], axis=-1)
    wc = jnp.concatenate([wc_part(c_re), -wc_part(c_im)], axis=1)
    a_tab = jnp.concatenate([ab_re.reshape(NBLK, 1, HALF), ab_im.reshape(NBLK, 1, HALF)], axis=-1)
    return wb.astype(_MXU), wc.astype(_MXU), a_tab


def _s5_unplace(dwb, dwc, da):
    eye = jnp.eye(GB, dtype=F32)

    def wb_part(dpart):
        x5 = dpart.reshape(NBLK, GB, P, GB, N)
        return jnp.einsum('kgphn,gh->kgpn', x5, eye).transpose(2, 0, 1, 3).reshape(P, G * N)

    def wc_part(dpart):
        x5 = dpart.reshape(NBLK, GB, N, GB, P)
        return jnp.einsum('kgnhp,gh->kgpn', x5, eye).reshape(G, P, N)

    dbb_re_t, dbb_im_t = wb_part(dwb[..., :HALF]), wb_part(dwb[..., HALF:])
    dc_re, dc_im = wc_part(dwc[:, :HALF]), -wc_part(dwc[:, HALF:])
    dab_re, dab_im = da[:, :HALF].reshape(1, G * N), da[:, HALF:].reshape(1, G * N)
    return dab_re, dab_im, dbb_re_t, dbb_im_t, dc_re, dc_im


def _row(v):
    return v.reshape(1, -1)


def kernel(x, c, positions, ada_w, ada_b, norm1_g, norm2_g, ffn_w_gate, ffn_w_up, ffn_w_down, s5_lam_re, s5_lam_im, s5_log_dt, s5_b_re, s5_b_im, s5_c_re, s5_c_im, s5_d, s5_w_glu, s5_b_glu, kv_ada_w, kv_ada_b, kv_norm_g, w_kv_a, kv_a_norm_g, w_kv_b, k_nope_norm_g, k_rope_norm_g, mla_w_dq, mla_q_norm_g, mla_w_uq, mla_q_nope_norm_g, mla_q_rope_norm_g, mla_w_o, loss_target, m_ada_w, m_ada_b, m_norm1_g, m_norm2_g, m_ffn_w_gate, m_ffn_w_up, m_ffn_w_down, m_s5_lam_re, m_s5_lam_im, m_s5_log_dt, m_s5_b_re, m_s5_b_im, m_s5_c_re, m_s5_c_im, m_s5_d, m_s5_w_glu, m_s5_b_glu, m_kv_ada_w, m_kv_ada_b, m_kv_norm_g, m_w_kv_a, m_kv_a_norm_g, m_w_kv_b, m_k_nope_norm_g, m_k_rope_norm_g, m_mla_w_dq, m_mla_q_norm_g, m_mla_w_uq, m_mla_q_nope_norm_g, m_mla_q_rope_norm_g, m_mla_w_o, v_ada_w, v_ada_b, v_norm1_g, v_norm2_g, v_ffn_w_gate, v_ffn_w_up, v_ffn_w_down, v_s5_lam_re, v_s5_lam_im, v_s5_log_dt, v_s5_b_re, v_s5_b_im, v_s5_c_re, v_s5_c_im, v_s5_d, v_s5_w_glu, v_s5_b_glu, v_kv_ada_w, v_kv_ada_b, v_kv_norm_g, v_w_kv_a, v_kv_a_norm_g, v_w_kv_b, v_k_nope_norm_g, v_k_rope_norm_g, v_mla_w_dq, v_mla_q_norm_g, v_mla_w_uq, v_mla_q_nope_norm_g, v_mla_q_rope_norm_g, v_mla_w_o):
    W = dict(ada_w=ada_w, ada_b=ada_b, norm1_g=norm1_g, norm2_g=norm2_g, ffn_w_gate=ffn_w_gate, ffn_w_up=ffn_w_up, ffn_w_down=ffn_w_down, s5_lam_re=s5_lam_re, s5_lam_im=s5_lam_im, s5_log_dt=s5_log_dt, s5_b_re=s5_b_re, s5_b_im=s5_b_im, s5_c_re=s5_c_re, s5_c_im=s5_c_im, s5_d=s5_d, s5_w_glu=s5_w_glu, s5_b_glu=s5_b_glu, kv_ada_w=kv_ada_w, kv_ada_b=kv_ada_b, kv_norm_g=kv_norm_g, w_kv_a=w_kv_a, kv_a_norm_g=kv_a_norm_g, w_kv_b=w_kv_b, k_nope_norm_g=k_nope_norm_g, k_rope_norm_g=k_rope_norm_g, mla_w_dq=mla_w_dq, mla_q_norm_g=mla_q_norm_g, mla_w_uq=mla_w_uq, mla_q_nope_norm_g=mla_q_nope_norm_g, mla_q_rope_norm_g=mla_q_rope_norm_g, mla_w_o=mla_w_o)
    M = dict(ada_w=m_ada_w, ada_b=m_ada_b, norm1_g=m_norm1_g, norm2_g=m_norm2_g, ffn_w_gate=m_ffn_w_gate, ffn_w_up=m_ffn_w_up, ffn_w_down=m_ffn_w_down, s5_lam_re=m_s5_lam_re, s5_lam_im=m_s5_lam_im, s5_log_dt=m_s5_log_dt, s5_b_re=m_s5_b_re, s5_b_im=m_s5_b_im, s5_c_re=m_s5_c_re, s5_c_im=m_s5_c_im, s5_d=m_s5_d, s5_w_glu=m_s5_w_glu, s5_b_glu=m_s5_b_glu, kv_ada_w=m_kv_ada_w, kv_ada_b=m_kv_ada_b, kv_norm_g=m_kv_norm_g, w_kv_a=m_w_kv_a, kv_a_norm_g=m_kv_a_norm_g, w_kv_b=m_w_kv_b, k_nope_norm_g=m_k_nope_norm_g, k_rope_norm_g=m_k_rope_norm_g, mla_w_dq=m_mla_w_dq, mla_q_norm_g=m_mla_q_norm_g, mla_w_uq=m_mla_w_uq, mla_q_nope_norm_g=m_mla_q_nope_norm_g, mla_q_rope_norm_g=m_mla_q_rope_norm_g, mla_w_o=m_mla_w_o)
    V = dict(ada_w=v_ada_w, ada_b=v_ada_b, norm1_g=v_norm1_g, norm2_g=v_norm2_g, ffn_w_gate=v_ffn_w_gate, ffn_w_up=v_ffn_w_up, ffn_w_down=v_ffn_w_down, s5_lam_re=v_s5_lam_re, s5_lam_im=v_s5_lam_im, s5_log_dt=v_s5_log_dt, s5_b_re=v_s5_b_re, s5_b_im=v_s5_b_im, s5_c_re=v_s5_c_re, s5_c_im=v_s5_c_im, s5_d=v_s5_d, s5_w_glu=v_s5_w_glu, s5_b_glu=v_s5_b_glu, kv_ada_w=v_kv_ada_w, kv_ada_b=v_kv_ada_b, kv_norm_g=v_kv_norm_g, w_kv_a=v_w_kv_a, kv_a_norm_g=v_kv_a_norm_g, w_kv_b=v_w_kv_b, k_nope_norm_g=v_k_nope_norm_g, k_rope_norm_g=v_k_rope_norm_g, mla_w_dq=v_mla_w_dq, mla_q_norm_g=v_mla_q_norm_g, mla_w_uq=v_mla_w_uq, mla_q_nope_norm_g=v_mla_q_nope_norm_g, mla_q_rope_norm_g=v_mla_q_rope_norm_g, mla_w_o=v_mla_w_o)
    return _step(x[0], c, positions, loss_target[0], W, M, V)


WEIGHT_NAMES = ['ada_w', 'ada_b', 'norm1_g', 'norm2_g', 'ffn_w_gate', 'ffn_w_up', 'ffn_w_down', 's5_lam_re', 's5_lam_im', 's5_log_dt', 's5_b_re', 's5_b_im', 's5_c_re', 's5_c_im', 's5_d', 's5_w_glu', 's5_b_glu', 'kv_ada_w', 'kv_ada_b', 'kv_norm_g', 'w_kv_a', 'kv_a_norm_g', 'w_kv_b', 'k_nope_norm_g', 'k_rope_norm_g', 'mla_w_dq', 'mla_q_norm_g', 'mla_w_uq', 'mla_q_nope_norm_g', 'mla_q_rope_norm_g', 'mla_w_o']
REPLICATED = ['ada_b', 'norm1_g', 'norm2_g', 's5_lam_re', 's5_lam_im', 's5_log_dt', 's5_b_re', 's5_b_im', 's5_c_re', 's5_c_im', 'kv_ada_b', 'kv_norm_g', 'kv_a_norm_g', 'k_nope_norm_g', 'k_rope_norm_g', 'mla_q_norm_g', 'mla_q_nope_norm_g', 'mla_q_rope_norm_g']
SHARDED_VEC = ['s5_d', 's5_b_glu']


def _step(x, c, positions, target, W, M, V):
    s = x.shape[0]
    me = _index(*_me())
    mxu = lambda a: a.astype(_MXU)

    pad_c = lambda a: jnp.pad(a, ((0, 0), (0, FFB - FF // N_DEV)))
    pad_r = lambda a: jnp.pad(a, ((0, FFB - FF // N_DEV), (0, 0)))
    cols = lambda g: g.transpose(1, 0, 2).reshape(g.shape[1], N_DEV * g.shape[2])
    rows = lambda g: g.reshape(N_DEV * g.shape[1], g.shape[2])

    def local_pack(l):
        second = W['s5_w_glu'][l] if l < N_A else W['mla_w_o'][l - N_A]
        arrs = [jnp.concatenate([mxu(pad_c(W['ffn_w_gate'][l])), mxu(pad_c(W['ffn_w_up'][l]))], axis=0),
                jnp.concatenate([mxu(pad_r(W['ffn_w_down'][l])), mxu(second)], axis=0)]
        if l == N_A:
            arrs += [jnp.concatenate([mxu(W['w_kv_b']), mxu(W['mla_w_dq'][0])], axis=0), mxu(W['w_kv_a'])]
        if l > N_A:
            arrs += [mxu(W['mla_w_dq'][l - N_A])]
        if l >= N_A:
            arrs += [mxu(W['mla_w_uq'][l - N_A])]
        return arrs


    def layer_weights(l, after):
        lands = exchange_wait(f"gather_wait_{l}", gathers[l], after, True)
        full = [lax.dynamic_update_slice(ld, src[None], (me,) + (0,) * src.ndim) for ld, src in zip(lands, gathers[l][2])]
        w = {'wg': cols(full[0][:, :D]), 'wu': cols(full[0][:, D:]), 'wd': rows(full[1][:, :FFB]),
             'second': rows(full[1][:, FFB:])}
        if l >= N_A:
            if l == N_A:
                wkvb3 = cols(full[2][:, :KVL]).reshape(KVL, H, DN + DV)
                wkva = rows(full[3])
                w['wa_pad'] = jnp.concatenate([wkva[:, :KVL], jnp.zeros((D, DN), _MXU), wkva[:, KVL:],
                                               jnp.zeros((D, HD - DN - DR), _MXU)], axis=1)
                w['wkn_pad'] = jnp.pad(wkvb3[:, :, :DN], ((0, 0), (0, 0), (0, HD - DN))).reshape(KVL, H * HD)
                w['wv'] = wkvb3[:, :, DN:].reshape(KVL, H * DV)
                w['wdq'] = rows(full[2][:, KVL:])
            else:
                w['wdq'] = rows(full[2])
            w['wuq_pad'] = _pad_heads(cols(full[-1]), DN + DR, HD)
        return w

    vec = jnp.concatenate([c.reshape(-1), W['s5_d'].reshape(-1), W['s5_b_glu'].reshape(-1)]).reshape(1, -1)
    vec = jnp.pad(vec, ((0, 7), (0, 0)))
    gv = all_gather("gather_vectors", vec)[:, 0, :]
    c_all = gv[:, :D]
    d_full = jnp.concatenate([gv[d, D:D + 2 * 128].reshape(N_A, 128) for d in range(N_DEV)], axis=1)
    bglu_full = jnp.concatenate([gv[d, D + 256:D + 512].reshape(N_A, 128) for d in range(N_DEV)], axis=1)

    ca_all = jax.nn.silu(c_all)
    w_mod = jnp.concatenate([W['ada_w'][l] for l in range(DEPTH)] + [W['kv_ada_w']], axis=1)
    n_mod = w_mod.shape[1]
    mod_cols = small_matmul("mod_matmul", ca_all, w_mod)
    gm = all_gather("gather_mod", mod_cols)
    gathers = [exchange_start(f"gather_start_{l}", local_pack(l), True, gm) for l in range(DEPTH)]
    tokens = sum(g[4][0, 0] for g in gathers)
    mine = lax.dynamic_index_in_dim(gm, me, axis=1, keepdims=False) + tokens
    per_l = D * 6 // N_DEV
    mods = []
    for l in range(DEPTH):
        full = jnp.concatenate([mine[d, per_l * l:per_l * (l + 1)] for d in range(N_DEV)]) + W['ada_b'][l]
        mods.append([_row(full[D * i:D * (i + 1)]) for i in range(6)])
    kfull = jnp.concatenate([mine[d, per_l * DEPTH:] for d in range(N_DEV)]) + W['kv_ada_b']
    k_shift, k_scale = _row(kfull[:D]), _row(kfull[D:])

    inv = 1.0 / (ROPE_THETA ** (np.arange(0, DR, 2, dtype=np.float32) / DR))
    inv128 = np.zeros((1, HD), np.float32)
    inv128[0, DN:DN + DR // 2] = inv
    inv128[0, DN + DR // 2:DN + DR] = inv
    cosf, sinf = rope_tables("rope_tables", positions.reshape(s, 1), jnp.asarray(inv128))
    zpad = lambda n: jnp.zeros((n,), F32)
    gkn128 = _row(jnp.concatenate([W['k_nope_norm_g'], zpad(HD - DN)]))
    gkr128 = _row(jnp.concatenate([zpad(DN), W['k_rope_norm_g'], zpad(HD - DN - DR)]))
    gq128 = [_row(jnp.concatenate([W['mla_q_nope_norm_g'][j], W['mla_q_rope_norm_g'][j], zpad(HD - DN - DR)]))
             for j in range(2)]

    expand = jnp.asarray(np.kron(np.eye(G, dtype=np.float32), np.ones((1, N), np.float32)))
    s5_raw, s5_mats = [], []
    for l in range(N_A):
        raw = (_row(W['s5_lam_re'][l]), _row(W['s5_lam_im'][l]), _row(W['s5_log_dt'][l]),
               W['s5_b_re'][l].transpose(2, 0, 1).reshape(P, G * N), W['s5_b_im'][l].transpose(2, 0, 1).reshape(P, G * N))
        ab_re, ab_im, bb_re_t, bb_im_t = s5_prep_fwd(f"s5_prep_fwd", *raw, expand)
        s5_raw.append(raw)
        s5_mats.append(_s5_place(ab_re, ab_im, bb_re_t, bb_im_t, W['s5_c_re'][l], W['s5_c_im'][l]))

    g1 = [_row(W['norm1_g'][l]) for l in range(DEPTH)]
    g2 = [_row(W['norm2_g'][l]) for l in range(DEPTH)]
    saved = []
    xs = x
    kv = None
    lw = [None] * DEPTH
    for l in range(DEPTH):
        sh1, sc1, gt1, sh2, sc2, gt2 = mods[l]
        rec = {'x_in': xs}
        if l >= N_A:
            lw[l] = layer_weights(l, xs)
        if l == N_A:
            kv_smalls = [_row(W['kv_norm_g']), k_shift, k_scale, _row(W['kv_a_norm_g']), gkn128, gkr128]
            kv_w = [lw[l]['wa_pad'], lw[l]['wkn_pad'], lw[l]['wv']]
            k_mat, v_mat = seg_forward("kv_fwd", seg_kv, [xs], kv_smalls, [cosf, sinf], kv_w,
                                       [(H * HD, _MXU), (H * DV, _MXU)], tap_widths=(KVL + HD, H * HD, H * DV))
            kv = {'x_in': xs, 'smalls': kv_smalls, 'k': k_mat, 'v': v_mat, 'w': kv_w}
        if l < N_A:
            (h,) = seg_forward("pre_fwd", seg_pre, [xs], [g1[l], sh1, sc1], [], [], [(D, F32)])
            wb, wc, a_tab = s5_mats[l]
            y, s0 = s5_scan_fwd("s5_scan_fwd", h, wb, wc, a_tab, _row(d_full[l]))
            lw[l] = layer_weights(l, y)
            (x_mid,) = seg_forward("glu_fwd", seg_glu, [xs, y], [gt1, _row(bglu_full[l])], [], [lw[l]['second']],
                                   [(D, F32)], tap_widths=(D,))
            rec.update(h=h, y=y, s0=s0)
        else:
            j = l - N_A
            q_smalls = [g1[l], sh1, sc1, _row(W['mla_q_norm_g'][j]), gq128[j]]
            (q_mat,) = seg_forward("q_fwd", seg_q, [xs], q_smalls, [cosf, sinf], [lw[l]['wdq'], lw[l]['wuq_pad']],
                                   [(H * HD, _MXU)], tap_widths=(QL, H * HD))
            o_mat, lse = attn_fwd("attn_fwd", q_mat, kv['k'], kv['v'])
            (x_mid,) = seg_forward("o_fwd", seg_o, [xs, o_mat], [gt1], [], [lw[l]['second']], [(D, F32)],
                                   tap_widths=(D,))
            rec.update(q=q_mat, o=o_mat, lse=lse, q_smalls=q_smalls)
        rec['x_mid'] = x_mid
        (xs,) = seg_forward("ffn_fwd", seg_ffn, [x_mid], [g2[l], sh2, sc2, gt2], [],
                            [lw[l]['wg'], lw[l]['wu'], lw[l]['wd']], [(D, F32)], tap_widths=(FFP, FFP, D))
        saved.append(rec)

    dy, loss_part = loss_kernel("loss", xs, target)
    loss = lax.psum(loss_part[0, 0], ("x", "y", "c"))

    rblk = lambda a: a.reshape(N_DEV, a.shape[0] // N_DEV, a.shape[1])
    cblk = lambda a: a.reshape(a.shape[0], N_DEV, a.shape[1] // N_DEV).transpose(1, 0, 2)
    dmod = [None] * DEPTH
    dk_tot = []
    dv_tot = []
    dx = dy
    sends = [None] * DEPTH
    send_token = jnp.zeros((1, 1), F32)
    g_n1 = [None] * DEPTH
    g_n2 = [None] * DEPTH
    g_bglu = [None] * N_A
    g_dskip = [None] * N_A
    g_s5 = [None] * N_A
    g_qn, g_q128 = [None] * 2, [None] * 2
    for l in range(DEPTH - 1, -1, -1):
        rec = saved[l]
        sh1, sc1, gt1, sh2, sc2, gt2 = mods[l]
        dx, dgate, dup, dyd, h_b, a_b, dg2, dsh2, dsc2, dgt2 = ffn_backward(
            "ffn_bwd", rec['x_mid'], dx, g2[l], sh2, sc2, gt2 + send_token, lw[l]['wg'], lw[l]['wu'], lw[l]['wd'])
        out_l = [matmul_tn("tn_ffn_in", h_b, dgate, _MXU, col_blocks=N_DEV),
                 matmul_tn("tn_ffn_in", h_b, dup, _MXU, col_blocks=N_DEV),
                 matmul_tn("tn_ffn_out", a_b, dyd, _MXU).reshape(N_DEV, FFB, D)]
        g_n2[l] = dg2
        if l < N_A:
            (dx, dyy), (dz,), (g_b,), (dgt1, dbg) = seg_backward(
                "glu_bwd", seg_glu, [rec['x_in'], rec['y']], [gt1, _row(bglu_full[l])], [], [lw[l]['second']],
                [dx], (D,), (D,))
            out_l.append(rblk(matmul_tn("tn_sq", g_b, dz, _MXU)))
            g_bglu[l] = dbg
            wb, wc, a_tab = s5_mats[l]
            dh, dwb, dwc, da, dd = s5_scan_bwd("s5_scan_bwd", rec['h'], dyy, rec['s0'], wb, wc, a_tab, _row(d_full[l]))
            g_dskip[l] = dd
            dab_re, dab_im, dbb_re_t, dbb_im_t, dc_re, dc_im = _s5_unplace(dwb, dwc, da)
            dlr, dli, dldt, dbr_t, dbi_t = s5_prep_bwd("s5_prep_bwd", *s5_raw[l], expand,
                                                       (dab_re, dab_im, dbb_re_t, dbb_im_t))
            g_s5[l] = (dlr.reshape(G, N), dli.reshape(G, N), dldt.reshape(G),
                       dbr_t.reshape(P, G, N).transpose(1, 2, 0), dbi_t.reshape(P, G, N).transpose(1, 2, 0), dc_re, dc_im)
            (dx,), _, _, (dg1, dsh1, dsc1) = seg_backward(
                "pre_bwd", seg_pre, [rec['x_in']], [g1[l], sh1, sc1], [], [], [dh], (), (), dx_add=dx)
        else:
            j = l - N_A
            (dx, do), (dzo,), (o_b,), (dgt1,) = seg_backward(
                "o_bwd", seg_o, [rec['x_in'], rec['o']], [gt1], [], [lw[l]['second']], [dx], (D,), (D,))
            out_l.append(rblk(matmul_tn("tn_sq", o_b, dzo, _MXU)))
            dq, dk, dv = attn_bwd("attn_bwd", rec['q'], kv['k'], kv['v'], rec['o'], do, rec['lse'])
            dk_tot.append(dk)
            dv_tot.append(dv)
            (dx,), (dql, dqq), (hq_b, qn_b), (dg1, dsh1, dsc1, dqg, dq128) = seg_backward(
                "q_bwd", seg_q, [rec['x_in']], rec['q_smalls'], [cosf, sinf], [lw[l]['wdq'], lw[l]['wuq_pad']],
                [dq], (QL, H * HD), (D, QL), dx_add=dx)
            g_dq = rblk(matmul_tn("tn_dq", hq_b, dql, _MXU))
            g_uq = cblk(_unpad_heads(matmul_tn("tn_uq", qn_b, dqq, _MXU), DN + DR, HD))
            g_qn[j], g_q128[j] = dqg, dq128
        g_n1[l] = dg1
        dmod[l] = jnp.concatenate([dsh1, dsc1, dgt1, dsh2, dsc2, dgt2], axis=1)
        if l == N_A:
            dkk = sum_parts("sum_dk", jnp.stack(dk_tot))
            dvv = sum_parts("sum_dv", jnp.stack(dv_tot))
            (dx,), (dta, dtk, dtv), (hk_b, ckv_b), (dkg, dksh, dksc, dag, dgkn, dgkr) = seg_backward(
                "kv_bwd", seg_kv, [kv['x_in']], kv['smalls'], [cosf, sinf], kv['w'],
                [dkk, dvv], (KVL + HD, H * HD, H * DV), (D, KVL), dx_add=dx)
            g_wa = matmul_tn("tn_kva", hk_b, dta, _MXU)
            g_wa = jnp.concatenate([g_wa[:, :KVL], g_wa[:, KVL + DN:KVL + DN + DR]], axis=1)
            g_kn = matmul_tn("tn_kn", ckv_b, dtk, _MXU).reshape(KVL, H, HD)[:, :, :DN]
            g_v = matmul_tn("tn_v", ckv_b, dtv, _MXU).reshape(KVL, H, DV)
            g_wkvb = jnp.concatenate([g_kn, g_v], axis=2).reshape(KVL, H * (DN + DV))
            dkmod = jnp.concatenate([dksh, dksc], axis=1)
            out_l += [jnp.concatenate([cblk(g_wkvb), g_dq], axis=1), rblk(g_wa)]
        if l > N_A:
            out_l.append(g_dq)
        if l >= N_A:
            out_l.append(g_uq)
        if l > 0:
            sends[l] = exchange_start(f"a2a_start_{l}", out_l, False, dx)
            send_token = sends[l][4][0:1, 0:1]
    grad_x = dx

    small = {
        'norm1_g': jnp.concatenate(g_n1, axis=0), 'norm2_g': jnp.concatenate(g_n2, axis=0),
        's5_lam_re': jnp.stack([g_s5[l][0] for l in range(N_A)]), 's5_lam_im': jnp.stack([g_s5[l][1] for l in range(N_A)]),
        's5_log_dt': jnp.stack([g_s5[l][2] for l in range(N_A)]),
        's5_b_re': jnp.stack([g_s5[l][3] for l in range(N_A)]), 's5_b_im': jnp.stack([g_s5[l][4] for l in range(N_A)]),
        's5_c_re': jnp.stack([g_s5[l][5] for l in range(N_A)]), 's5_c_im': jnp.stack([g_s5[l][6] for l in range(N_A)]),
        'kv_norm_g': dkg, 'kv_a_norm_g': dag, 'k_nope_norm_g': dgkn[:, :DN], 'k_rope_norm_g': dgkr[:, DN:DN + DR],
        'mla_q_norm_g': jnp.concatenate(g_qn, axis=0),
        'mla_q_nope_norm_g': jnp.concatenate([g[:, :DN] for g in g_q128], axis=0),
        'mla_q_rope_norm_g': jnp.concatenate([g[:, DN:DN + DR] for g in g_q128], axis=0),
        's5_d': jnp.concatenate(g_dskip, axis=0), 's5_b_glu': jnp.concatenate(g_bglu, axis=0),
    }
    small_names = [n for n in REPLICATED if n not in ('ada_b', 'kv_ada_b')] + SHARDED_VEC
    flat_small = jnp.concatenate([small[n].reshape(-1) for n in small_names])
    n_small = int(flat_small.shape[0])
    pad_small = -(-n_small // 65536) * 65536
    flat_small = jnp.pad(flat_small, (0, pad_small - n_small)).reshape(pad_small // 128, 128)

    dm = jnp.concatenate(dmod + [dkmod], axis=1)[0]
    per_dev = []
    for d in range(N_DEV):
        cols = [dm[6 * D * l + per_l * d:6 * D * l + per_l * (d + 1)] for l in range(DEPTH)]
        cols.append(dm[6 * D * DEPTH + (2 * D // N_DEV) * d:6 * D * DEPTH + (2 * D // N_DEV) * (d + 1)])
        per_dev.append(jnp.concatenate(cols))
    dm_dev = jnp.stack(per_dev)
    gdm = all_gather("gather_dmod", dm_dev)
    small_st = exchange_start("small_start", [flat_small], True, gdm)
    sends[0] = exchange_start("a2a_start_0", out_l, False, small_st[4])
    dm_mine = lax.dynamic_index_in_dim(gdm, me, axis=1, keepdims=False) + sends[0][4][0, 0]
    g_wmod = small_matmul_tn("dmod_matmul", ca_all, dm_mine)
    g_ada_w = jnp.stack([g_wmod[:, per_l * l:per_l * (l + 1)] for l in range(DEPTH)])
    g_kv_ada_w = g_wmod[:, per_l * DEPTH:]
    dm_sum = sum_parts("sum_dmod", gdm.reshape(N_DEV, N_DEV, n_mod))
    g_ada_b = jnp.stack([jnp.concatenate([dm_sum[d, per_l * l:per_l * (l + 1)] for d in range(N_DEV)])
                         for l in range(DEPTH)])
    g_kv_ada_b = jnp.concatenate([dm_sum[d, per_l * DEPTH:] for d in range(N_DEV)])

    grads, out_delta, out_m, out_v = {}, {}, {}, {}

    def update(name, parts, base=0, stride=0):
        shp = W[name].shape
        shp3 = shp if len(shp) == 3 else (1,) + shp
        res = adamw("adamw_" + name, parts, W[name].reshape(shp3), M[name].reshape(shp3), V[name].reshape(shp3),
                    base, stride)
        grads[name], out_delta[name], out_m[name], out_v[name] = (a.reshape(shp) for a in res)

    update('ada_w', g_ada_w.reshape(1, DEPTH * D, per_l), 0, D)
    update('kv_ada_w', g_kv_ada_w[None])

    chains = {}

    def update_layer(name, parts, layer, base=0):
        shp = W[name].shape
        shp3 = shp if len(shp) == 3 else (1,) + shp
        chains[name] = adamw_layer(f"adamw_{name}_{layer}", parts, W[name].reshape(shp3), M[name].reshape(shp3),
                                   V[name].reshape(shp3), layer, chains.get(name), base)
        grads[name], out_delta[name], out_m[name], out_v[name] = (a.reshape(shp) for a in chains[name])

    ffn_parts = [[None] * DEPTH for _ in range(3)]

    def receive(l, after):
        lands = exchange_wait(f"a2a_wait_{l}", sends[l], after, False)
        recv = [lax.dynamic_update_slice(ld, lax.dynamic_index_in_dim(src, me, 0, keepdims=True), (me,) + (0,) * (src.ndim - 1))
                for ld, src in zip(lands, sends[l][2])]
        for i in range(3):
            ffn_parts[i][l] = recv[i]
        if l < N_A:
            update_layer('s5_w_glu', recv[3], l)
        else:
            update_layer('mla_w_o', recv[3], l - N_A)
            if l == N_A:
                update_layer('w_kv_b', recv[4], 0)
                update_layer('mla_w_dq', recv[4], 0, KVL)
                update_layer('w_kv_a', recv[5], 0)
            else:
                update_layer('mla_w_dq', recv[4], l - N_A)
            update_layer('mla_w_uq', recv[-1], l - N_A)

    for l in range(DEPTH - 1, 0, -1):
        receive(l, out_delta['kv_ada_w'])

    (small_land,) = exchange_wait("small_wait", small_st, chains['s5_w_glu'][1], True)
    small_all = lax.dynamic_update_slice(small_land, flat_small[None], (me, 0, 0))
    g_small_sum = sum_parts("sum_small", small_all).reshape(-1)
    off = 0
    for n in small_names:
        size = int(np.prod(small[n].shape))
        full = g_small_sum[off:off + size]
        off += size
        if n in SHARDED_VEC:
            full = lax.dynamic_slice_in_dim(full.reshape(N_A, D), me * (D // N_DEV), D // N_DEV, axis=1)
        grads[n] = full.reshape(W[n].shape)
    grads['ada_b'] = g_ada_b
    grads['kv_ada_b'] = g_kv_ada_b

    packed_names = REPLICATED + SHARDED_VEC

    def pack(dct):
        flat_ = jnp.concatenate([dct[n].reshape(-1) for n in packed_names])
        n_ = int(flat_.shape[0])
        p_ = -(-n_ // 65536) * 65536
        return jnp.pad(flat_, (0, p_ - n_)).reshape(p_ // 128, 128)

    _, d_p, m_p, v_p = adamw("adamw_small", pack(grads)[None], pack(W)[None], pack(M)[None], pack(V)[None])
    off = 0
    d_p, m_p, v_p = d_p.reshape(-1), m_p.reshape(-1), v_p.reshape(-1)
    for n in packed_names:
        size = int(np.prod(W[n].shape))
        out_delta[n] = d_p[off:off + size].reshape(W[n].shape)
        out_m[n] = m_p[off:off + size].reshape(W[n].shape)
        out_v[n] = v_p[off:off + size].reshape(W[n].shape)
        off += size

    receive(0, d_p)
    update('ffn_w_gate', jnp.concatenate(ffn_parts[0], axis=1), 0, D)
    update('ffn_w_up', jnp.concatenate(ffn_parts[1], axis=1), 0, D)
    update('ffn_w_down', jnp.concatenate(ffn_parts[2], axis=1), 0, FFB)

    return (loss, grad_x[None], *[grads[n] for n in WEIGHT_NAMES], *[out_delta[n] for n in WEIGHT_NAMES],
            *[out_m[n] for n in WEIGHT_NAMES], *[out_v[n] for n in WEIGHT_NAMES])
```

```python
import functools
import math

import numpy as np
import jax
import jax.numpy as jnp
from jax import lax
from jax.experimental import pallas as pl
from jax.experimental.pallas import tpu as pltpu

F32 = jnp.float32
_MXU = jnp.bfloat16
HI = lax.Precision.HIGHEST

D = 1024
DEPTH = 4
N_A = 2
FF = 2816
FFB = 384
FFP = 8 * FFB
N_DEV = 8
G = 64
P = 16
N = 64
GB = 8
NBLK = G // GB
HALF = GB * N
H = 16
HP = H // 2
DN, DR, DV = 64, 32, 64
HD = 128
QL = 256
KVL = 256
CHUNK = 64
ROPE_THETA = 10000.0
ATTN_SCALE = 1.0 / math.sqrt(DN + DR)
LOG2E = 1.4426950408889634
EXP2_SCALE = ATTN_SCALE * LOG2E
EPS = 1e-6
ADAM_LR, ADAM_B1, ADAM_B2, ADAM_EPS, ADAM_WD, ADAM_STEP = 0.001, 0.9, 0.999, 1e-08, 0.01, 10
VMEM_LIMIT = 56 * 1024 * 1024
MESH = pl.DeviceIdType.MESH

TILE_ROW = 256
TILE_ATT = 512
TILE_ATT_FWD = 512
TILE_ATT_KEYS = 512
TILE_SCAN = 512


def _params(n_grid):
    return pltpu.CompilerParams(dimension_semantics=("arbitrary",) * n_grid, vmem_limit_bytes=VMEM_LIMIT)


@jax.custom_vjp
def mm(a, w):
    return jnp.dot(a.astype(_MXU), w, preferred_element_type=F32)


def _mm_fwd(a, w):
    return mm(a, w), w


def _mm_bwd(w, g):
    da = lax.dot_general(g.astype(_MXU), w, (((1,), (1,)), ((), ())), preferred_element_type=F32)
    return da, jnp.zeros_like(w)


mm.defvjp(_mm_fwd, _mm_bwd)


def rms(x, g):
    return x * lax.rsqrt(jnp.mean(x * x, axis=-1, keepdims=True) + EPS) * g


def modulate(h, shift, scale):
    return h * (1.0 + scale) + shift


def _lane(n=HD):
    return lax.broadcasted_iota(jnp.int32, (1, n), 1)


def _rot_matrix():
    r = lax.broadcasted_iota(jnp.int32, (HD, HD), 0)
    c = lax.broadcasted_iota(jnp.int32, (HD, HD), 1)
    first = (c >= DN) & (c < DN + DR // 2) & (r == c + DR // 2)
    second = (c >= DN + DR // 2) & (c < DN + DR) & (r == c - DR // 2)
    return jnp.where(first, -1.0, jnp.where(second, 1.0, 0.0)).astype(F32)


def head_norm_rope(xh, g128, cosf, sinf, rot, with_nope):
    lane = _lane()
    m_n = lane < DN
    m_r = (lane >= DN) & (lane < DN + DR)
    sq = xh * xh
    inv_r = lax.rsqrt(jnp.sum(jnp.where(m_r, sq, 0.0), axis=-1, keepdims=True) / DR + EPS)
    if with_nope:
        inv_n = lax.rsqrt(jnp.sum(jnp.where(m_n, sq, 0.0), axis=-1, keepdims=True) / DN + EPS)
        inv = jnp.where(m_n, inv_n, jnp.where(m_r, inv_r, 0.0))
    else:
        inv = jnp.where(m_r, inv_r, 0.0)
    xg = xh * inv * g128
    return xg * cosf + jnp.dot(xg, rot, precision=HI, preferred_element_type=F32) * sinf


def seg_pre(x, g, sh, sc):
    return (modulate(rms(x, g), sh, sc),), ()


def seg_ffn(x, g, sh, sc, gt, t_g, t_u, t_d, wg, wu, wd):
    h = modulate(rms(x, g), sh, sc)
    gate = mm(h, wg) + t_g
    up = mm(h, wu) + t_u
    a = jax.nn.silu(gate) * up
    y = mm(a, wd) + t_d
    return (x + gt * y,), (h.astype(_MXU), a.astype(_MXU))


def seg_glu(x, y, gt, b, t_z, w):
    g = jax.nn.gelu(y)
    z = mm(g, w) + b + t_z
    return (x + gt * (g * jax.nn.sigmoid(z)),), (g.astype(_MXU),)


def seg_o(x, o, gt, t_o, w):
    return (x + gt * (mm(o, w) + t_o),), (o.astype(_MXU),)


def seg_q(x, g, sh, sc, qg, g128, t_l, t_q, cosf, sinf, wdq, wuq):
    h = modulate(rms(x, g), sh, sc)
    ql = mm(h, wdq) + t_l
    qn = rms(ql, qg)
    q = mm(qn, wuq) + t_q
    rot = _rot_matrix()
    heads = [head_norm_rope(q[:, HD * i:HD * (i + 1)], g128, cosf, sinf, rot, True) for i in range(H)]
    return (jnp.concatenate(heads, axis=1),), (h.astype(_MXU), qn.astype(_MXU))


def seg_kv(x, g, sh, sc, ag, gkn, gkr, t_a, t_k, t_v, cosf, sinf, wa, wkn, wv):
    hk = modulate(rms(x, g), sh, sc)
    kva = mm(hk, wa) + t_a
    ckv = rms(kva[:, :KVL], ag)
    kr = head_norm_rope(kva[:, KVL:KVL + HD], gkr, cosf, sinf, _rot_matrix(), False)
    kn = mm(ckv, wkn) + t_k
    v = mm(ckv, wv) + t_v
    heads = []
    for i in range(H):
        kh = kn[:, HD * i:HD * (i + 1)]
        inv = lax.rsqrt(jnp.sum(kh * kh, axis=-1, keepdims=True) / DN + EPS)
        heads.append(kh * inv * gkn + kr)
    return (jnp.concatenate(heads, axis=1), v), (hk.astype(_MXU), ckv.astype(_MXU))


def _row_call(name, body_fn, rows, fulls, out_rows, out_accs, tile):
    s = rows[0].shape[0]
    n_tiles = s // tile
    n_rows, n_fulls, n_or, n_oa = len(rows), len(fulls), len(out_rows), len(out_accs)

    def kern(*refs):
        i = pl.program_id(0)
        row_v = [r[...] for r in refs[:n_rows]]
        full_v = [r[...] for r in refs[n_rows:n_rows + n_fulls]]
        o_refs = refs[n_rows + n_fulls:]
        ro, ao = body_fn(row_v, full_v)
        for r, v in zip(o_refs[:n_or], ro):
            r[...] = v.astype(r.dtype)
        if n_oa:
            @pl.when(i == 0)
            def _():
                for r in o_refs[n_or:]:
                    r[...] = jnp.zeros(r.shape, r.dtype)
            for r, v in zip(o_refs[n_or:], ao):
                r[...] += v.astype(r.dtype)

    in_specs = [pl.BlockSpec((tile, a.shape[1]), lambda i: (i, 0)) for a in rows]
    for a in fulls:
        big = a.size * a.dtype.itemsize > (1 << 20)
        nd = a.ndim
        in_specs.append(pl.BlockSpec(a.shape, functools.partial(lambda i, nd_: (0,) * nd_, nd_=nd),
                                     **({"pipeline_mode": pl.Buffered(1)} if big else {})))
    out_shape = [jax.ShapeDtypeStruct((s, w), dt) for w, dt in out_rows]
    out_shape += [jax.ShapeDtypeStruct(shp, dt) for shp, dt in out_accs]
    out_specs = [pl.BlockSpec((tile, w), lambda i: (i, 0)) for w, _ in out_rows]
    out_specs += [pl.BlockSpec(shp, functools.partial(lambda i, nd_: (0,) * nd_, nd_=len(shp))) for shp, _ in out_accs]
    res = pl.pallas_call(kern, out_shape=out_shape, grid=(n_tiles,), in_specs=in_specs, out_specs=out_specs,
                         name=name, compiler_params=_params(1))(*rows, *fulls)
    return list(res)


def seg_forward(name, seg, rows, smalls, consts_rows, consts_full, out_widths, tile=TILE_ROW, tap_widths=()):
    n_r, n_s, n_cr = len(rows), len(smalls), len(consts_rows)

    def body(row_v, full_v):
        t = row_v[0].shape[0]
        taps = [jnp.zeros((t, w), F32) for w in tap_widths]
        outs, _ = seg(*row_v[:n_r], *full_v[:n_s], *taps, *row_v[n_r:], *full_v[n_s:])
        return outs, ()

    return _row_call(name, body, list(rows) + list(consts_rows), list(smalls) + list(consts_full),
                     out_widths, [], tile)


def seg_backward(name, seg, rows, smalls, consts_rows, consts_full, cots, tap_widths, aux_widths,
                 dx_add=None, tile=TILE_ROW):
    n_r, n_s, n_cr, n_c = len(rows), len(smalls), len(consts_rows), len(cots)
    has_add = dx_add is not None

    def body(row_v, full_v):
        t = row_v[0].shape[0]
        prim_rows = row_v[:n_r]
        c_rows = row_v[n_r:n_r + n_cr]
        cot_v = row_v[n_r + n_cr:n_r + n_cr + n_c]
        add_v = row_v[n_r + n_cr + n_c] if has_add else None
        small_v = full_v[:n_s]
        c_full = full_v[n_s:]
        taps = [jnp.zeros((t, w), F32) for w in tap_widths]

        def f(*args):
            return seg(*args, *c_rows, *c_full)

        _, vjp_fn, aux = jax.vjp(f, *prim_rows, *small_v, *taps, has_aux=True)
        grads = vjp_fn(tuple(c.astype(F32) for c in cot_v))
        d_rows = list(grads[:n_r])
        if has_add:
            d_rows[0] = d_rows[0] + add_v
        d_small = grads[n_r:n_r + n_s]
        d_taps = grads[n_r + n_s:]
        return d_rows + list(d_taps) + list(aux), [jnp.sum(g, axis=0, keepdims=True) if g.shape[0] != 1 else g
                                                   for g in d_small]

    all_rows = list(rows) + list(consts_rows) + list(cots) + ([dx_add] if has_add else [])
    out_rows = [(a.shape[1], F32) for a in rows] + [(w, _MXU) for w in tap_widths] + [(w, _MXU) for w in aux_widths]
    out_accs = [((1, a.shape[1]), F32) for a in smalls]
    res = _row_call(name, body, all_rows, list(smalls) + list(consts_full), out_rows, out_accs, tile)
    n_t, n_a = len(tap_widths), len(aux_widths)
    return res[:n_r], res[n_r:n_r + n_t], res[n_r + n_t:n_r + n_t + n_a], res[n_r + n_t + n_a:]


def _split(n):
    if n <= 1024:
        return n
    for t in (1408, 1024, 768, 512, 256, 128):
        if n % t == 0:
            return t
    raise ValueError(n)


def matmul_tn(name, a, b, out_dtype, col_blocks=None):
    s, k1 = a.shape
    _, k2 = b.shape
    tm, ts = _split(k1), 2048
    if col_blocks is None:
        tn, per_step, wblk = _split(k2), 1, None
    else:
        wblk = k2 // col_blocks
        per_step = max(1, min(col_blocks, 1536 // wblk))
        tn = per_step * wblk
    n_s = s // ts

    def kern(a_ref, b_ref, o_ref, acc_ref):
        k = pl.program_id(2)

        @pl.when(k == 0)
        def _():
            acc_ref[...] = jnp.zeros(acc_ref.shape, F32)

        acc_ref[...] += lax.dot_general(a_ref[...], b_ref[...], (((0,), (0,)), ((), ())),
                                        preferred_element_type=F32)

        @pl.when(k == n_s - 1)
        def _():
            if col_blocks is None:
                o_ref[...] = acc_ref[...].astype(o_ref.dtype)
            else:
                for cb in range(per_step):
                    o_ref[cb] = acc_ref[:, wblk * cb:wblk * (cb + 1)].astype(o_ref.dtype)

    if col_blocks is None:
        out_shape = jax.ShapeDtypeStruct((k1, k2), out_dtype)
        out_spec = pl.BlockSpec((tm, tn), lambda i, j, k: (i, j))
    else:
        out_shape = jax.ShapeDtypeStruct((col_blocks, k1, wblk), out_dtype)
        out_spec = pl.BlockSpec((per_step, tm, wblk), lambda i, j, k: (j, i, 0))
    return pl.pallas_call(
        kern, out_shape=out_shape, grid=(k1 // tm, k2 // tn, n_s),
        in_specs=[pl.BlockSpec((ts, tm), lambda i, j, k: (k, i)), pl.BlockSpec((ts, tn), lambda i, j, k: (k, j))],
        out_specs=out_spec,
        scratch_shapes=[pltpu.VMEM((tm, tn), F32)], name=name, compiler_params=_params(3))(a, b)


def ffn_forward(name, x, g, sh, sc, gt, wg, wu, wd, tile=TILE_ROW):
    s = x.shape[0]
    fp = wg.shape[1]
    blk = 2 * FFB
    n_blk = fp // blk

    def kern(x_ref, g_ref, sh_ref, sc_ref, gt_ref, wg_ref, wu_ref, wd_ref, o_ref, gate_ref, up_ref):
        xv = x_ref[...]
        hb = modulate(rms(xv, g_ref[...]), sh_ref[...], sc_ref[...]).astype(_MXU)
        y = jnp.zeros((tile, D), F32)
        for c in range(n_blk):
            cs = slice(blk * c, blk * (c + 1))
            gate = jnp.dot(hb, wg_ref[:, cs], preferred_element_type=F32)
            up = jnp.dot(hb, wu_ref[:, cs], preferred_element_type=F32)
            gate_ref[:, cs] = gate.astype(_MXU)
            up_ref[:, cs] = up.astype(_MXU)
            y = y + jnp.dot((jax.nn.silu(gate) * up).astype(_MXU), wd_ref[cs, :], preferred_element_type=F32)
        o_ref[...] = xv + gt_ref[...] * y

    row = lambda w: pl.BlockSpec((tile, w), lambda i: (i, 0))
    vec = pl.BlockSpec((1, D), lambda i: (0, 0))
    wspec = lambda a: pl.BlockSpec(a.shape, lambda i: (0, 0), pipeline_mode=pl.Buffered(1))
    return pl.pallas_call(
        kern, out_shape=[jax.ShapeDtypeStruct((s, D), F32), jax.ShapeDtypeStruct((s, fp), _MXU),
                         jax.ShapeDtypeStruct((s, fp), _MXU)],
        grid=(s // tile,), in_specs=[row(D), vec, vec, vec, vec, wspec(wg), wspec(wu), wspec(wd)],
        out_specs=[row(D), row(fp), row(fp)], name=name, compiler_params=_params(1))(x, g, sh, sc, gt, wg, wu, wd)


def ffn_backward(name, x, dxo, gate, up, g, sh, sc, gt, wg, wu, wd, tile=TILE_ROW):
    s = x.shape[0]
    fp = wg.shape[1]
    blk = 2 * FFB
    n_blk = fp // blk

    def kern(x_ref, dxo_ref, gate_ref, up_ref, g_ref, sh_ref, sc_ref, gt_ref, wg_ref, wu_ref, wd_ref,
             dx_ref, dg_ref, du_ref, dy_ref, h_ref, a_ref, dgn_ref, dsh_ref, dsc_ref, dgt_ref):
        i = pl.program_id(0)

        @pl.when(i == 0)
        def _():
            for r in (dgn_ref, dsh_ref, dsc_ref, dgt_ref):
                r[...] = jnp.zeros(r.shape, F32)

        dxo = dxo_ref[...]
        h, pre_vjp = jax.vjp(lambda *p: modulate(rms(p[0], p[1]), p[2], p[3]), x_ref[...], g_ref[...], sh_ref[...],
                             sc_ref[...])
        h_ref[...] = h.astype(_MXU)
        dyb = (gt_ref[...] * dxo).astype(_MXU)
        dy_ref[...] = dyb
        y = jnp.zeros((tile, D), F32)
        dh = jnp.zeros((tile, D), F32)
        tr = (((1,), (1,)), ((), ()))
        for c in range(n_blk):
            cs = slice(blk * c, blk * (c + 1))
            gate = gate_ref[:, cs].astype(F32)
            up = up_ref[:, cs].astype(F32)
            sig = jax.nn.sigmoid(gate)
            sl = gate * sig
            ab = (sl * up).astype(_MXU)
            a_ref[:, cs] = ab
            y = y + jnp.dot(ab, wd_ref[cs, :], preferred_element_type=F32)
            da = lax.dot_general(dyb, wd_ref[cs, :], tr, preferred_element_type=F32)
            dgb = (da * up * (sig * (1.0 + gate * (1.0 - sig)))).astype(_MXU)
            dub = (da * sl).astype(_MXU)
            dg_ref[:, cs] = dgb
            du_ref[:, cs] = dub
            dh = dh + lax.dot_general(dgb, wg_ref[:, cs], tr, preferred_element_type=F32) \
                + lax.dot_general(dub, wu_ref[:, cs], tr, preferred_element_type=F32)
        dgt_ref[...] += jnp.sum(dxo * y, axis=0, keepdims=True)
        dx_pre, dgn, dsh, dsc = pre_vjp(dh)
        dx_ref[...] = dxo + dx_pre
        dgn_ref[...] += dgn
        dsh_ref[...] += dsh
        dsc_ref[...] += dsc

    row = lambda w: pl.BlockSpec((tile, w), lambda i: (i, 0))
    vec = pl.BlockSpec((1, D), lambda i: (0, 0))
    wspec = lambda a: pl.BlockSpec(a.shape, lambda i: (0, 0), pipeline_mode=pl.Buffered(1))
    rows_out = [(D, F32), (fp, _MXU), (fp, _MXU), (D, _MXU), (D, _MXU), (fp, _MXU)]
    return pl.pallas_call(
        kern,
        out_shape=[jax.ShapeDtypeStruct((s, w), dt) for w, dt in rows_out] + [jax.ShapeDtypeStruct((1, D), F32)] * 4,
        grid=(s // tile,),
        in_specs=[row(D), row(D), row(fp), row(fp), vec, vec, vec, vec, wspec(wg), wspec(wu), wspec(wd)],
        out_specs=[row(w) for w, _ in rows_out] + [vec] * 4,
        name=name, compiler_params=_params(1))(x, dxo, gate, up, g, sh, sc, gt, wg, wu, wd)


def small_matmul(name, a, w, tn=256):
    m, k = a.shape
    n = w.shape[1]

    def kern(a_ref, w_ref, o_ref):
        o_ref[...] = jnp.dot(a_ref[...].astype(_MXU), w_ref[...].astype(_MXU), preferred_element_type=F32)

    return pl.pallas_call(kern, out_shape=jax.ShapeDtypeStruct((m, n), F32), grid=(n // tn,),
                          in_specs=[pl.BlockSpec((m, k), lambda j: (0, 0)), pl.BlockSpec((k, tn), lambda j: (0, j))],
                          out_specs=pl.BlockSpec((m, tn), lambda j: (0, j)), name=name,
                          compiler_params=_params(1))(a, w)


def small_matmul_tn(name, a, b, tn=256):
    m, k = a.shape
    n = b.shape[1]

    def kern(a_ref, b_ref, o_ref):
        o_ref[...] = lax.dot_general(a_ref[...].astype(_MXU), b_ref[...].astype(_MXU), (((0,), (0,)), ((), ())),
                                     preferred_element_type=F32)

    return pl.pallas_call(kern, out_shape=jax.ShapeDtypeStruct((k, n), F32), grid=(n // tn,),
                          in_specs=[pl.BlockSpec((m, k), lambda j: (0, 0)), pl.BlockSpec((m, tn), lambda j: (0, j))],
                          out_specs=pl.BlockSpec((k, tn), lambda j: (0, j)), name=name,
                          compiler_params=_params(1))(a, b)


def _s5_prep_math(lam_re, lam_im, log_dt, b_re_t, b_im_t, expand):
    dt = jnp.dot(jnp.exp(log_dt), expand, precision=HI, preferred_element_type=F32)
    mag = jnp.exp(lam_re * dt)
    ab_re = mag * jnp.cos(lam_im * dt)
    ab_im = mag * jnp.sin(lam_im * dt)
    den = lam_re * lam_re + lam_im * lam_im
    nr = ab_re - 1.0
    ni = ab_im
    f_re = (nr * lam_re + ni * lam_im) / den
    f_im = (ni * lam_re - nr * lam_im) / den
    bb_re = f_re * b_re_t - f_im * b_im_t
    bb_im = f_re * b_im_t + f_im * b_re_t
    return ab_re, ab_im, bb_re, bb_im


def _whole(kern, name, out_shape, *args):
    return pl.pallas_call(kern, out_shape=out_shape, name=name,
                          compiler_params=pltpu.CompilerParams(vmem_limit_bytes=VMEM_LIMIT))(*args)


def s5_prep_fwd(name, lam_re, lam_im, log_dt, b_re_t, b_im_t, expand):
    def kern(a, b, c, d, e, f, o0, o1, o2, o3):
        r = _s5_prep_math(a[...], b[...], c[...], d[...], e[...], f[...])
        for o, v in zip((o0, o1, o2, o3), r):
            o[...] = v

    gn = lam_re.shape[1]
    shp = [jax.ShapeDtypeStruct((1, gn), F32)] * 2 + [jax.ShapeDtypeStruct((P, gn), F32)] * 2
    return _whole(kern, name, shp, lam_re, lam_im, log_dt, b_re_t, b_im_t, expand)


def s5_prep_bwd(name, lam_re, lam_im, log_dt, b_re_t, b_im_t, expand, cots):
    def kern(a, b, c, d, e, f, c0, c1, c2, c3, o0, o1, o2, o3, o4):
        ex = f[...]
        _, vjp_fn = jax.vjp(lambda *p: _s5_prep_math(*p, ex), a[...], b[...], c[...], d[...], e[...])
        g = vjp_fn((c0[...], c1[...], c2[...], c3[...]))
        for o, v in zip((o0, o1, o2, o3, o4), g):
            o[...] = v

    shp = [jax.ShapeDtypeStruct(a.shape, F32) for a in (lam_re, lam_im, log_dt, b_re_t, b_im_t)]
    return _whole(kern, name, shp, lam_re, lam_im, log_dt, b_re_t, b_im_t, expand, *cots)


def _cpowers(ar, ai):
    pw = [(ar, ai)]
    for _ in range(7):
        pr, pi = pw[-1]
        pw.append((pr * ar - pi * ai, pr * ai + pi * ar))
    return pw


def _row_select(row, values):
    out = jnp.broadcast_to(values[7], (8, values[7].shape[1]))
    for r in range(6, -1, -1):
        out = jnp.where(row == r, values[r], out)
    return out


def _scan_tables(ar, ai, reverse):
    pw = _cpowers(ar, ai)
    row = lax.broadcasted_iota(jnp.int32, (8, ar.shape[1]), 0)
    steps = []
    for d in (1, 2, 4):
        keep = (row <= 7 - d) if reverse else (row >= d)
        steps.append((jnp.where(keep, pw[d - 1][0], 0.0), jnp.where(keep, pw[d - 1][1], 0.0)))
    order = list(range(7, -1, -1)) if reverse else list(range(8))
    carry = (_row_select(row, [pw[i][0] for i in order]), _row_select(row, [pw[i][1] for i in order]))
    return steps, carry


def _tile_scan_fwd(xr, xi, cr, ci, steps, carry_m):
    for d, (mr, mi) in zip((1, 2, 4), steps):
        sr = pltpu.roll(xr, d, 0)
        si = pltpu.roll(xi, d, 0)
        xr, xi = xr + mr * sr - mi * si, xi + mr * si + mi * sr
    pr, pi = carry_m
    return xr + pr * cr - pi * ci, xi + pr * ci + pi * cr


def _tile_scan_rev(xr, xi, cr, ci, steps, carry_m):
    for d, (mr, mi) in zip((1, 2, 4), steps):
        sr = pltpu.roll(xr, 8 - d, 0)
        si = pltpu.roll(xi, 8 - d, 0)
        xr, xi = xr + mr * sr + mi * si, xi + mr * si - mi * sr
    pr, pi = carry_m
    return xr + pr * cr + pi * ci, xi + pr * ci - pi * cr


def _fwd_scan_block(buf, row0, n_tiles8, ar, ai, c0r, c0i):
    steps, carry_m = _scan_tables(ar, ai, False)

    def body(j, carry):
        cr, ci = carry
        r0 = pl.multiple_of(row0 + j * 8, 8)
        xr = buf[pl.ds(r0, 8), 0:HALF]
        xi = buf[pl.ds(r0, 8), HALF:2 * HALF]
        xr, xi = _tile_scan_fwd(xr, xi, cr, ci, steps, carry_m)
        buf[pl.ds(r0, 8), 0:HALF] = xr
        buf[pl.ds(r0, 8), HALF:2 * HALF] = xi
        return xr[7:8], xi[7:8]

    return lax.fori_loop(0, n_tiles8, body, (c0r, c0i))


def s5_scan_fwd(name, h, wb, wc, a_tab, dskip, tile=TILE_SCAN):
    s = h.shape[0]
    n_t = s // tile

    def kern(h_ref, wb_ref, wc_ref, a_ref, d_ref, y_ref, s0_ref, carry_ref, buf):
        i = pl.program_id(0)

        @pl.when(i == 0)
        def _():
            carry_ref[...] = jnp.zeros(carry_ref.shape, F32)

        s0_ref[0] = carry_ref[...]
        for k in range(NBLK):
            cols = slice(GB * P * k, GB * P * (k + 1))
            u = h_ref[:, cols]
            buf[...] = jnp.dot(u.astype(_MXU), wb_ref[k], preferred_element_type=F32)
            ar = a_ref[k, :, 0:HALF]
            ai = a_ref[k, :, HALF:2 * HALF]
            cr, ci = _fwd_scan_block(buf, 0, tile // 8, ar, ai, carry_ref[k:k + 1, 0:HALF],
                                     carry_ref[k:k + 1, HALF:2 * HALF])
            carry_ref[k:k + 1, 0:HALF] = cr
            carry_ref[k:k + 1, HALF:2 * HALF] = ci
            y_ref[:, cols] = jnp.dot(buf[...].astype(_MXU), wc_ref[k], preferred_element_type=F32) + d_ref[:, cols] * u

    full = lambda a: pl.BlockSpec(a.shape, functools.partial(lambda i, nd_: (0,) * nd_, nd_=a.ndim))
    return pl.pallas_call(
        kern,
        out_shape=[jax.ShapeDtypeStruct((s, D), F32), jax.ShapeDtypeStruct((n_t, NBLK, 2 * HALF), F32)],
        grid=(n_t,),
        in_specs=[pl.BlockSpec((tile, D), lambda i: (i, 0)), full(wb), full(wc), full(a_tab), full(dskip)],
        out_specs=[pl.BlockSpec((tile, D), lambda i: (i, 0)), pl.BlockSpec((1, NBLK, 2 * HALF), lambda i: (i, 0, 0))],
        scratch_shapes=[pltpu.VMEM((NBLK, 2 * HALF), F32), pltpu.VMEM((tile, 2 * HALF), F32)],
        name=name, compiler_params=_params(1))(h, wb, wc, a_tab, dskip)


def s5_scan_bwd(name, h, dy, s0, wb, wc, a_tab, dskip, tile=TILE_SCAN):
    s = h.shape[0]
    n_t = s // tile
    n8 = tile // 8

    def kern(h_ref, dy_ref, s0_ref, wb_ref, wc_ref, a_ref, d_ref, dh_ref, dwb_ref, dwc_ref, da_ref, dd_ref,
             lam_ref, sbuf, gbuf):
        i = pl.program_id(0)

        @pl.when(i == 0)
        def _():
            lam_ref[...] = jnp.zeros(lam_ref.shape, F32)
            dwb_ref[...] = jnp.zeros(dwb_ref.shape, F32)
            dwc_ref[...] = jnp.zeros(dwc_ref.shape, F32)
            da_ref[...] = jnp.zeros(da_ref.shape, F32)
            dd_ref[...] = jnp.zeros(dd_ref.shape, F32)

        for k in range(NBLK):
            cols = slice(GB * P * k, GB * P * (k + 1))
            u = h_ref[:, cols]
            dyk = dy_ref[:, cols]
            ar = a_ref[k, :, 0:HALF]
            ai = a_ref[k, :, HALF:2 * HALF]
            sbuf[0:8, :] = jnp.broadcast_to(s0_ref[0, k:k + 1, :], (8, 2 * HALF))
            sbuf[8:tile + 8, :] = jnp.dot(u.astype(_MXU), wb_ref[k], preferred_element_type=F32)
            _fwd_scan_block(sbuf, 8, n8, ar, ai, s0_ref[0, k:k + 1, 0:HALF], s0_ref[0, k:k + 1, HALF:2 * HALF])
            dyb = dyk.astype(_MXU)
            gbuf[...] = lax.dot_general(dyb, wc_ref[k], (((1,), (1,)), ((), ())), preferred_element_type=F32)
            dwc_ref[k] += lax.dot_general(sbuf[8:tile + 8, :].astype(_MXU), dyb, (((0,), (0,)), ((), ())),
                                          preferred_element_type=F32)
            steps, carry_m = _scan_tables(ar, ai, True)
            row = lax.broadcasted_iota(jnp.int32, (8, HALF), 0)

            def body(jj, carry):
                cr, ci, dar, dai = carry
                j = n8 - 1 - jj
                r0 = pl.multiple_of(j * 8, 8)
                xr = gbuf[pl.ds(r0, 8), 0:HALF]
                xi = gbuf[pl.ds(r0, 8), HALF:2 * HALF]
                xr, xi = _tile_scan_rev(xr, xi, cr, ci, steps, carry_m)
                gbuf[pl.ds(r0, 8), 0:HALF] = xr
                gbuf[pl.ds(r0, 8), HALF:2 * HALF] = xi
                r1 = pl.multiple_of(j * 8 + 8, 8)
                spr = jnp.where(row == 0, sbuf[pl.ds(r0, 8), 0:HALF][7:8],
                                pltpu.roll(sbuf[pl.ds(r1, 8), 0:HALF], 1, 0))
                spi = jnp.where(row == 0, sbuf[pl.ds(r0, 8), HALF:2 * HALF][7:8],
                                pltpu.roll(sbuf[pl.ds(r1, 8), HALF:2 * HALF], 1, 0))
                dar = dar + xr * spr + xi * spi
                dai = dai + xi * spr - xr * spi
                return xr[0:1], xi[0:1], dar, dai

            z8 = jnp.zeros((8, HALF), F32)
            cr, ci, dar, dai = lax.fori_loop(
                0, n8, body, (lam_ref[k:k + 1, 0:HALF], lam_ref[k:k + 1, HALF:2 * HALF], z8, z8))
            lam_ref[k:k + 1, 0:HALF] = cr
            lam_ref[k:k + 1, HALF:2 * HALF] = ci
            da_ref[k:k + 1, 0:HALF] += jnp.sum(dar, axis=0, keepdims=True)
            da_ref[k:k + 1, HALF:2 * HALF] += jnp.sum(dai, axis=0, keepdims=True)
            lam = gbuf[...].astype(_MXU)
            dwb_ref[k] += lax.dot_general(u.astype(_MXU), lam, (((0,), (0,)), ((), ())), preferred_element_type=F32)
            du = lax.dot_general(lam, wb_ref[k], (((1,), (1,)), ((), ())), preferred_element_type=F32)
            dh_ref[:, cols] = du + d_ref[:, cols] * dyk
            dd_ref[:, cols] += jnp.sum(dyk * u, axis=0, keepdims=True)

    full = lambda a: pl.BlockSpec(a.shape, functools.partial(lambda i, nd_: (0,) * nd_, nd_=a.ndim))
    fullo = lambda shp: pl.BlockSpec(shp, functools.partial(lambda i, nd_: (0,) * nd_, nd_=len(shp)))
    rev = lambda i: (n_t - 1 - i, 0)
    return pl.pallas_call(
        kern,
        out_shape=[jax.ShapeDtypeStruct((s, D), F32), jax.ShapeDtypeStruct(wb.shape, F32),
                   jax.ShapeDtypeStruct(wc.shape, F32), jax.ShapeDtypeStruct((NBLK, 2 * HALF), F32),
                   jax.ShapeDtypeStruct((1, D), F32)],
        grid=(n_t,),
        in_specs=[pl.BlockSpec((tile, D), rev), pl.BlockSpec((tile, D), rev),
                  pl.BlockSpec((1, NBLK, 2 * HALF), lambda i: (n_t - 1 - i, 0, 0)),
                  full(wb), full(wc), full(a_tab), full(dskip)],
        out_specs=[pl.BlockSpec((tile, D), rev), fullo(wb.shape), fullo(wc.shape), fullo((NBLK, 2 * HALF)),
                   fullo((1, D))],
        scratch_shapes=[pltpu.VMEM((NBLK, 2 * HALF), F32), pltpu.VMEM((tile + 8, 2 * HALF), F32),
                        pltpu.VMEM((tile, 2 * HALF), F32)],
        name=name, compiler_params=_params(1))(h, dy, s0, wb, wc, a_tab, dskip)


def _chunk_mask(q0, k0, tq, tk):
    r = (q0 + lax.broadcasted_iota(jnp.int32, (tq, tk), 0)) // CHUNK
    c = (k0 + lax.broadcasted_iota(jnp.int32, (tq, tk), 1)) // CHUNK
    return r >= c


def _head_lanes(j):
    lane = _lane(2 * DV)
    return (lane >= DV * j) & (lane < DV * (j + 1))


def _raw_scores(q, kblk, masked, t):
    s = lax.dot_general(q, kblk, (((1,), (1,)), ((), ())), preferred_element_type=F32)
    return jnp.where(_chunk_mask(0, 0, t, t), s, -1e30) if masked else s


def attn_fwd(name, q, k, v, t=TILE_ATT_FWD, tk=TILE_ATT_KEYS):
    s = q.shape[0]
    n_q = s // t
    r = t // tk

    def kern(q_ref, k_ref, v_ref, o_ref, lse_ref):
        qi = pl.program_id(1)
        qs = [q_ref[:, HD * j:HD * (j + 1)] for j in range(2)]

        def absorb(k0, carry, mask):
            vblk = v_ref[pl.ds(k0, tk), :]
            scs = [lax.dot_general(qs[j], k_ref[pl.ds(k0, tk), HD * j:HD * (j + 1)], (((1,), (1,)), ((), ())),
                                   preferred_element_type=F32) for j in range(2)]
            if mask is not None:
                scs = [jnp.where(mask, sc, -1e30) for sc in scs]
            m_new = [jnp.maximum(carry[j][0], jnp.max(scs[j], axis=-1, keepdims=True)) for j in range(2)]
            ps = [jnp.exp2((scs[j] - m_new[j]) * EXP2_SCALE) for j in range(2)]
            alphas = [jnp.exp2((carry[j][0] - m_new[j]) * EXP2_SCALE) for j in range(2)]
            pvs = [jnp.dot(ps[j].astype(_MXU), vblk, preferred_element_type=F32) for j in range(2)]
            return tuple((m_new[j], alphas[j] * carry[j][1] + jnp.sum(ps[j], axis=-1, keepdims=True),
                          alphas[j] * carry[j][2] + pvs[j]) for j in range(2))

        init = tuple((jnp.full((t, 1), -1e30, F32), jnp.zeros((t, 1), F32), jnp.zeros((t, 2 * DV), F32))
                     for _ in range(2))
        carry = lax.fori_loop(0, qi * r, lambda kb, c: absorb(pl.multiple_of(kb * tk, tk), c, None), init)
        for i in range(r):
            carry = absorb(pl.multiple_of(qi * t + i * tk, tk), carry, _chunk_mask(0, i * tk, t, tk))
        outs = []
        for j in range(2):
            m, l, acc = carry[j]
            outs.append(acc / l)
            lse_ref[0, j] = m * ATTN_SCALE + jnp.log(l)
        o_ref[...] = jnp.where(_head_lanes(0), outs[0], outs[1])

    return pl.pallas_call(
        kern,
        out_shape=[jax.ShapeDtypeStruct((s, H * DV), F32), jax.ShapeDtypeStruct((HP, 2, s, 1), F32)],
        grid=(HP, n_q),
        in_specs=[pl.BlockSpec((t, 2 * HD), lambda hp, i: (i, hp)), pl.BlockSpec((s, 2 * HD), lambda hp, i: (0, hp)),
                  pl.BlockSpec((s, 2 * DV), lambda hp, i: (0, hp))],
        out_specs=[pl.BlockSpec((t, 2 * DV), lambda hp, i: (i, hp)),
                   pl.BlockSpec((1, 2, t, 1), lambda hp, i: (hp, 0, i, 0))],
        name=name, compiler_params=_params(2))(q, k, v)


def attn_bwd(name, q, k, v, o, do, lse, t=TILE_ATT):
    s = q.shape[0]
    n_q = s // t

    def kern(q_ref, k_ref, v_ref, o_ref, do_ref, lse_ref, dq_ref, dk_ref, dv_ref):
        qi = pl.program_id(1)

        @pl.when(qi == 0)
        def _():
            dk_ref[...] = jnp.zeros(dk_ref.shape, F32)
            dv_ref[...] = jnp.zeros(dv_ref.shape, F32)

        qs, doms, deltas, lse2 = [], [], [], []
        for j in range(2):
            qs.append(q_ref[:, HD * j:HD * (j + 1)])
            dom = jnp.where(_head_lanes(j), do_ref[...], 0.0)
            deltas.append(jnp.sum(dom * o_ref[...], axis=-1, keepdims=True))
            doms.append(dom.astype(_MXU))
            lse2.append(lse_ref[0, j] * LOG2E)

        def block(k0, dqs, masked):
            vblk = v_ref[pl.ds(k0, t), :]
            kblks = [k_ref[pl.ds(k0, t), HD * j:HD * (j + 1)] for j in range(2)]
            scs = [_raw_scores(qs[j], kblks[j], masked, t) for j in range(2)]
            dps = [lax.dot_general(doms[j], vblk, (((1,), (1,)), ((), ())), preferred_element_type=F32)
                   for j in range(2)]
            ps = [jnp.exp2(scs[j] * EXP2_SCALE - lse2[j]) for j in range(2)]
            dss = [(ps[j] * (dps[j] - deltas[j])).astype(_MXU) for j in range(2)]
            pbs = [ps[j].astype(_MXU) for j in range(2)]
            new = tuple(dqs[j] + jnp.dot(dss[j], kblks[j], preferred_element_type=F32) for j in range(2))
            for j in range(2):
                dk_ref[pl.ds(k0, t), HD * j:HD * (j + 1)] += lax.dot_general(
                    dss[j], qs[j], (((0,), (0,)), ((), ())), preferred_element_type=F32)
            dvs = [lax.dot_general(pbs[j], doms[j], (((0,), (0,)), ((), ())), preferred_element_type=F32)
                   for j in range(2)]
            dv_ref[pl.ds(k0, t), :] += dvs[0] + dvs[1]
            return new

        init = (jnp.zeros((t, HD), F32), jnp.zeros((t, HD), F32))
        dqs = lax.fori_loop(0, qi, lambda kb, c: block(pl.multiple_of(kb * t, t), c, False), init)
        dqs = block(pl.multiple_of(qi * t, t), dqs, True)
        for j in range(2):
            dq_ref[:, HD * j:HD * (j + 1)] = dqs[j] * ATTN_SCALE

        @pl.when(qi == n_q - 1)
        def _():
            dk_ref[...] = dk_ref[...] * ATTN_SCALE

    return pl.pallas_call(
        kern,
        out_shape=[jax.ShapeDtypeStruct((s, H * HD), F32), jax.ShapeDtypeStruct((s, H * HD), F32),
                   jax.ShapeDtypeStruct((s, H * DV), F32)],
        grid=(HP, n_q),
        in_specs=[pl.BlockSpec((t, 2 * HD), lambda hp, i: (i, hp)), pl.BlockSpec((s, 2 * HD), lambda hp, i: (0, hp)),
                  pl.BlockSpec((s, 2 * DV), lambda hp, i: (0, hp)), pl.BlockSpec((t, 2 * DV), lambda hp, i: (i, hp)),
                  pl.BlockSpec((t, 2 * DV), lambda hp, i: (i, hp)),
                  pl.BlockSpec((1, 2, t, 1), lambda hp, i: (hp, 0, i, 0))],
        out_specs=[pl.BlockSpec((t, 2 * HD), lambda hp, i: (i, hp)), pl.BlockSpec((s, 2 * HD), lambda hp, i: (0, hp)),
                   pl.BlockSpec((s, 2 * DV), lambda hp, i: (0, hp))],
        name=name, compiler_params=_params(2))(q, k, v, o, do, lse)


def rope_tables(name, pos_col, inv128):
    s = pos_col.shape[0]

    def kern(p_ref, inv_ref, c_ref, s_ref):
        ang = p_ref[...].astype(F32) * inv_ref[...]
        lane = _lane()
        m_r = (lane >= DN) & (lane < DN + DR)
        c_ref[...] = jnp.where(lane < DN, 1.0, jnp.where(m_r, jnp.cos(ang), 0.0))
        s_ref[...] = jnp.where(m_r, jnp.sin(ang), 0.0)

    return _whole(kern, name, [jax.ShapeDtypeStruct((s, HD), F32)] * 2, pos_col, inv128)


def loss_kernel(name, y, tgt, tile=TILE_ROW):
    def body(row_v, _):
        err = row_v[0] - row_v[1]
        part = 0.5 * jnp.sum(jnp.mean(err * err, axis=-1, keepdims=True), axis=0, keepdims=True)
        return [err * (1.0 / D)], [jnp.broadcast_to(part, (1, 128))]

    return _row_call(name, body, [y, tgt], [], [(D, F32)], [((1, 128), F32)], tile)


def _row_tile(r, c):
    cap = max(8, (1 << 18) // max(c, 1))
    for t in (2048, 1024, 512, 256, 128, 64, 32, 16, 8):
        if t <= cap and r % t == 0:
            return t
    return r


def sum_parts(name, parts):
    n, r, c = parts.shape
    t = _row_tile(r, c)

    def kern(p_ref, o_ref):
        acc = p_ref[0].astype(F32)
        for i in range(1, n):
            acc = acc + p_ref[i].astype(F32)
        o_ref[...] = acc

    return pl.pallas_call(kern, out_shape=jax.ShapeDtypeStruct((r, c), F32), grid=(r // t,),
                          in_specs=[pl.BlockSpec((n, t, c), lambda i: (0, i, 0))],
                          out_specs=pl.BlockSpec((t, c), lambda i: (i, 0)), name=name, compiler_params=_params(1))(parts)


def adamw(name, parts, w, m, v, base=0, stride=0):
    n, _, cp = parts.shape
    nl, r, c = w.shape
    t = _row_tile(math.gcd(math.gcd(r, base), stride), max(c, cp))
    c1 = 1.0 / (1.0 - ADAM_B1 ** ADAM_STEP)
    c2 = 1.0 / (1.0 - ADAM_B2 ** ADAM_STEP)

    def kern(p_ref, w_ref, m_ref, v_ref, g_ref, d_ref, nm_ref, nv_ref):
        g = p_ref[0].astype(F32)
        for i in range(1, n):
            g = g + p_ref[i].astype(F32)
        g = g[:, :c]
        nm = ADAM_B1 * m_ref[...] + (1.0 - ADAM_B1) * g
        nv = ADAM_B2 * v_ref[...] + (1.0 - ADAM_B2) * (g * g)
        g_ref[...] = g
        nm_ref[...] = nm
        nv_ref[...] = nv
        d_ref[...] = -ADAM_LR * ((nm * c1) / (jnp.sqrt(nv * c2) + ADAM_EPS) + ADAM_WD * w_ref[...])

    spec = pl.BlockSpec((None, t, c), lambda l, i: (l, i, 0))
    pspec = pl.BlockSpec((n, t, cp), lambda l, i: (0, (base + l * stride) // t + i, 0))
    return pl.pallas_call(kern, out_shape=[jax.ShapeDtypeStruct((nl, r, c), F32)] * 4, grid=(nl, r // t),
                          in_specs=[pspec, spec, spec, spec], out_specs=[spec] * 4, name=name,
                          compiler_params=_params(2))(parts, w, m, v)


def adamw_multi(name, parts_list, w, m, v):
    nl, r, c = w.shape
    n, _, cp = parts_list[0].shape
    t = _row_tile(r, max(c, cp))
    c1 = 1.0 / (1.0 - ADAM_B1 ** ADAM_STEP)
    c2 = 1.0 / (1.0 - ADAM_B2 ** ADAM_STEP)

    def kern(*refs):
        p_refs = refs[:nl]
        w_ref, m_ref, v_ref, g_ref, d_ref, nm_ref, nv_ref = refs[nl:]
        layer = pl.program_id(0)
        for ll in range(nl):
            @pl.when(layer == ll)
            def _(ll=ll):
                g = p_refs[ll][0].astype(F32)
                for i in range(1, n):
                    g = g + p_refs[ll][i].astype(F32)
                g = g[:, :c]
                nm = ADAM_B1 * m_ref[...] + (1.0 - ADAM_B1) * g
                nv = ADAM_B2 * v_ref[...] + (1.0 - ADAM_B2) * (g * g)
                g_ref[...] = g
                nm_ref[...] = nm
                nv_ref[...] = nv
                d_ref[...] = -ADAM_LR * ((nm * c1) / (jnp.sqrt(nv * c2) + ADAM_EPS) + ADAM_WD * w_ref[...])

    spec = pl.BlockSpec((None, t, c), lambda l, i: (l, i, 0))
    pspecs = [pl.BlockSpec((n, t, cp), functools.partial(lambda l, i, ll_: (0, jnp.where(l == ll_, i, 0), 0), ll_=ll))
              for ll in range(nl)]
    return pl.pallas_call(kern, out_shape=[jax.ShapeDtypeStruct((nl, r, c), F32)] * 4, grid=(nl, r // t),
                          in_specs=pspecs + [spec, spec, spec], out_specs=[spec] * 4, name=name,
                          compiler_params=_params(2))(*parts_list, w, m, v)


def adamw_layer(name, parts, w, m, v, layer, prev, base=0):
    n, _, cp = parts.shape
    nl, r, c = w.shape
    t = _row_tile(math.gcd(r, base), max(c, cp))
    c1 = 1.0 / (1.0 - ADAM_B1 ** ADAM_STEP)
    c2 = 1.0 / (1.0 - ADAM_B2 ** ADAM_STEP)
    chained = nl > 1

    def kern(p_ref, w_ref, m_ref, v_ref, *rest):
        g_ref, d_ref, nm_ref, nv_ref = rest[-4:]
        g = p_ref[0].astype(F32)
        for i in range(1, n):
            g = g + p_ref[i].astype(F32)
        g = g[:, :c]
        nm = ADAM_B1 * m_ref[...] + (1.0 - ADAM_B1) * g
        nv = ADAM_B2 * v_ref[...] + (1.0 - ADAM_B2) * (g * g)
        g_ref[...] = g
        nm_ref[...] = nm
        nv_ref[...] = nv
        d_ref[...] = -ADAM_LR * ((nm * c1) / (jnp.sqrt(nv * c2) + ADAM_EPS) + ADAM_WD * w_ref[...])

    spec = pl.BlockSpec((None, t, c), lambda i: (layer, i, 0))
    pspec = pl.BlockSpec((n, t, cp), lambda i: (0, base // t + i, 0))
    in_specs = [pspec, spec, spec, spec]
    args = [parts, w, m, v]
    aliases = {}
    if chained:
        if prev is None:
            prev = [lax.empty((nl, r, c), F32) for _ in range(4)]
        in_specs += [pl.BlockSpec(memory_space=pl.ANY)] * 4
        args += list(prev)
        aliases = {4 + i: i for i in range(4)}
    return pl.pallas_call(kern, out_shape=[jax.ShapeDtypeStruct((nl, r, c), F32)] * 4, grid=(r // t,),
                          in_specs=in_specs, out_specs=[spec] * 4, input_output_aliases=aliases, name=name,
                          compiler_params=_params(1))(*args)


def _me():
    return lax.axis_index("x"), lax.axis_index("y"), lax.axis_index("c")


def _flip(x, y, c, mask):
    return (jnp.where((mask >> 2) & 1, 1 - x, x), jnp.where((mask >> 1) & 1, 1 - y, y), jnp.where(mask & 1, 1 - c, c))


def _index(x, y, c):
    return 4 * x + 2 * y + c


def _exchange(name, arr, gather):
    out_shape = (N_DEV,) + arr.shape if gather else arr.shape

    def kern(in_ref, out_ref, send_sems, recv_sems, local_sem):
        x, y, c = _me()
        me = _index(x, y, c)
        mine = pltpu.make_async_copy(in_ref if gather else in_ref.at[me], out_ref.at[me], local_sem)
        mine.start()
        copies = []
        for mask in range(1, N_DEV):
            px, py, pc = _flip(x, y, c, mask)
            peer = _index(px, py, pc)
            cp = pltpu.make_async_remote_copy(
                src_ref=in_ref if gather else in_ref.at[peer], dst_ref=out_ref.at[me],
                send_sem=send_sems.at[mask - 1], recv_sem=recv_sems.at[mask - 1],
                device_id=(px, py, pc), device_id_type=MESH)
            cp.start()
            copies.append((cp, peer))
        for mask, (cp, peer) in enumerate(copies, start=1):
            pltpu.make_async_remote_copy(
                src_ref=in_ref if gather else in_ref.at[peer], dst_ref=out_ref.at[peer],
                send_sem=send_sems.at[mask - 1], recv_sem=recv_sems.at[mask - 1],
                device_id=_flip(x, y, c, mask), device_id_type=MESH).wait_recv()
        for cp, _ in copies:
            cp.wait_send()
        mine.wait()

    any_spec = pl.BlockSpec(memory_space=pl.ANY)
    return pl.pallas_call(
        kern, out_shape=jax.ShapeDtypeStruct(out_shape, arr.dtype), in_specs=[any_spec], out_specs=any_spec,
        scratch_shapes=[pltpu.SemaphoreType.DMA((N_DEV - 1,)), pltpu.SemaphoreType.DMA((N_DEV - 1,)),
                        pltpu.SemaphoreType.DMA],
        name=name, compiler_params=pltpu.CompilerParams(has_side_effects=True))(arr)


def all_gather(name, arr):
    return _exchange(name, arr, True)


def all_to_all(name, arr):
    return _exchange(name, arr, False)


_HBM = pl.BlockSpec(memory_space=pltpu.HBM)
_SEM = pl.BlockSpec(memory_space=pltpu.SEMAPHORE)
_EFFECT = pltpu.SideEffectType.DATAFLOW_SIDE_EFFECTING


def _split_copies(srcs, lands, send_sems, recv_sems, gather):
    x, y, c = _me()
    me = _index(x, y, c)
    out = []
    for a, (src, land) in enumerate(zip(srcs, lands)):
        for mask in range(1, N_DEV):
            px, py, pc = _flip(x, y, c, mask)
            peer = _index(px, py, pc)
            sem = (N_DEV - 1) * a + mask - 1
            mk = lambda dst_slot: pltpu.make_async_remote_copy(
                src_ref=src if gather else src.at[peer], dst_ref=land.at[dst_slot],
                send_sem=send_sems.at[sem], recv_sem=recv_sems.at[sem], device_id=(px, py, pc), device_id_type=MESH)
            out.append((mk(me), mk(peer)))
    return out


def exchange_start(name, arrs, gather, after):
    k = len(arrs)
    land_shapes = [((N_DEV,) + a.shape if gather else a.shape) for a in arrs]

    def body(*refs):
        srcs, lands = refs[:k], refs[k:2 * k]
        send_sems, recv_sems = refs[2 * k + 1], refs[2 * k + 2]
        token = refs[-1]
        for mine, _ in _split_copies(srcs, lands, send_sems, recv_sems, gather):
            mine.start()
        token[...] = jnp.zeros(token.shape, token.dtype)

    n_sem = (N_DEV - 1) * k
    res = pl.pallas_call(
        body, name=name,
        out_shape=(pltpu.SemaphoreType.DMA((n_sem,)), pltpu.SemaphoreType.DMA((n_sem,)),
                   *[pltpu.HBM(a.shape, a.dtype) for a in arrs],
                   *[pltpu.HBM(shp, a.dtype) for shp, a in zip(land_shapes, arrs)],
                   jax.ShapeDtypeStruct((8, 128), F32)),
        in_specs=[_HBM] * (2 * k) + [pl.BlockSpec(memory_space=pl.ANY)],
        out_specs=(_SEM, _SEM, *[_HBM] * (2 * k), pl.BlockSpec(memory_space=pltpu.VMEM)),
        input_output_aliases={i: 2 + i for i in range(2 * k)},
        compiler_params=pltpu.CompilerParams(has_side_effects=_EFFECT),
    )(*[pltpu.with_memory_space_constraint(a, pltpu.HBM) for a in arrs],
      *[pltpu.with_memory_space_constraint(lax.empty(shp, a.dtype), pltpu.HBM) for shp, a in zip(land_shapes, arrs)],
      after)
    return res[0], res[1], list(res[2:2 + k]), list(res[2 + k:2 + 2 * k]), res[-1]


def exchange_wait(name, started, after, gather):
    send_sems, recv_sems, thrus, lands, _ = started
    k = len(thrus)

    def body(*refs):
        srcs, lnds = refs[:k], refs[k:2 * k]
        s_sems, r_sems = refs[2 * k], refs[2 * k + 1]
        for mine, theirs in _split_copies(srcs, lnds, s_sems, r_sems, gather):
            mine.wait_send()
            theirs.wait_recv()

    res = pl.pallas_call(
        body, name=name,
        out_shape=tuple(pltpu.HBM(a.shape, a.dtype) for a in thrus + lands),
        in_specs=[_HBM] * (2 * k) + [_SEM, _SEM, pl.BlockSpec(memory_space=pl.ANY)], out_specs=tuple([_HBM] * (2 * k)),
        input_output_aliases={i: i for i in range(2 * k)},
        compiler_params=pltpu.CompilerParams(has_side_effects=_EFFECT),
    )(*thrus, *lands, send_sems, recv_sems, after)
    return list(res[k:])


def _pad_heads(w, real, padded):
    k = w.shape[0]
    w3 = w.reshape(k, H, real)
    return jnp.pad(w3, ((0, 0), (0, 0), (0, padded - real))).reshape(k, H * padded)


def _unpad_heads(w, real, padded):
    k = w.shape[0]
    return w.reshape(k, H, padded)[:, :, :real].reshape(k, H * real)


def _s5_place(ab_re, ab_im, bb_re_t, bb_im_t, c_re, c_im):
    eye = jnp.eye(GB, dtype=F32)

    def wb_part(bt):
        x4 = bt.reshape(P, NBLK, GB, N).transpose(1, 2, 0, 3)
        return jnp.einsum('kgpn,gh->kgphn', x4, eye).reshape(NBLK, GB * P, HALF)

    def wc_part(cc):
        x4 = cc.reshape(NBLK, GB, P, N)
        return jnp.einsum('kgpn,gh->kgnhp', x4, eye).reshape(NBLK, HALF, GB * P)

    wb = jnp.concatenate([wb_part(bb_re_t), wb_part(bb_im_t)], axis=-1)
    wc = jnp.concatenate([wc_part(c_re), -wc_part(c_im)], axis=1)
    a_tab = jnp.concatenate([ab_re.reshape(NBLK, 1, HALF), ab_im.reshape(NBLK, 1, HALF)], axis=-1)
    return wb.astype(_MXU), wc.astype(_MXU), a_tab


def _s5_unplace(dwb, dwc, da):
    eye = jnp.eye(GB, dtype=F32)

    def wb_part(dpart):
        x5 = dpart.reshape(NBLK, GB, P, GB, N)
        return jnp.einsum('kgphn,gh->kgpn', x5, eye).transpose(2, 0, 1, 3).reshape(P, G * N)

    def wc_part(dpart):
        x5 = dpart.reshape(NBLK, GB, N, GB, P)
        return jnp.einsum('kgnhp,gh->kgpn', x5, eye).reshape(G, P, N)

    dbb_re_t, dbb_im_t = wb_part(dwb[..., :HALF]), wb_part(dwb[..., HALF:])
    dc_re, dc_im = wc_part(dwc[:, :HALF]), -wc_part(dwc[:, HALF:])
    dab_re, dab_im = da[:, :HALF].reshape(1, G * N), da[:, HALF:].reshape(1, G * N)
    return dab_re, dab_im, dbb_re_t, dbb_im_t, dc_re, dc_im


def _row(v):
    return v.reshape(1, -1)


def kernel(x, c, positions, ada_w, ada_b, norm1_g, norm2_g, ffn_w_gate, ffn_w_up, ffn_w_down, s5_lam_re, s5_lam_im, s5_log_dt, s5_b_re, s5_b_im, s5_c_re, s5_c_im, s5_d, s5_w_glu, s5_b_glu, kv_ada_w, kv_ada_b, kv_norm_g, w_kv_a, kv_a_norm_g, w_kv_b, k_nope_norm_g, k_rope_norm_g, mla_w_dq, mla_q_norm_g, mla_w_uq, mla_q_nope_norm_g, mla_q_rope_norm_g, mla_w_o, loss_target, m_ada_w, m_ada_b, m_norm1_g, m_norm2_g, m_ffn_w_gate, m_ffn_w_up, m_ffn_w_down, m_s5_lam_re, m_s5_lam_im, m_s5_log_dt, m_s5_b_re, m_s5_b_im, m_s5_c_re, m_s5_c_im, m_s5_d, m_s5_w_glu, m_s5_b_glu, m_kv_ada_w, m_kv_ada_b, m_kv_norm_g, m_w_kv_a, m_kv_a_norm_g, m_w_kv_b, m_k_nope_norm_g, m_k_rope_norm_g, m_mla_w_dq, m_mla_q_norm_g, m_mla_w_uq, m_mla_q_nope_norm_g, m_mla_q_rope_norm_g, m_mla_w_o, v_ada_w, v_ada_b, v_norm1_g, v_norm2_g, v_ffn_w_gate, v_ffn_w_up, v_ffn_w_down, v_s5_lam_re, v_s5_lam_im, v_s5_log_dt, v_s5_b_re, v_s5_b_im, v_s5_c_re, v_s5_c_im, v_s5_d, v_s5_w_glu, v_s5_b_glu, v_kv_ada_w, v_kv_ada_b, v_kv_norm_g, v_w_kv_a, v_kv_a_norm_g, v_w_kv_b, v_k_nope_norm_g, v_k_rope_norm_g, v_mla_w_dq, v_mla_q_norm_g, v_mla_w_uq, v_mla_q_nope_norm_g, v_mla_q_rope_norm_g, v_mla_w_o):
    W = dict(ada_w=ada_w, ada_b=ada_b, norm1_g=norm1_g, norm2_g=norm2_g, ffn_w_gate=ffn_w_gate, ffn_w_up=ffn_w_up, ffn_w_down=ffn_w_down, s5_lam_re=s5_lam_re, s5_lam_im=s5_lam_im, s5_log_dt=s5_log_dt, s5_b_re=s5_b_re, s5_b_im=s5_b_im, s5_c_re=s5_c_re, s5_c_im=s5_c_im, s5_d=s5_d, s5_w_glu=s5_w_glu, s5_b_glu=s5_b_glu, kv_ada_w=kv_ada_w, kv_ada_b=kv_ada_b, kv_norm_g=kv_norm_g, w_kv_a=w_kv_a, kv_a_norm_g=kv_a_norm_g, w_kv_b=w_kv_b, k_nope_norm_g=k_nope_norm_g, k_rope_norm_g=k_rope_norm_g, mla_w_dq=mla_w_dq, mla_q_norm_g=mla_q_norm_g, mla_w_uq=mla_w_uq, mla_q_nope_norm_g=mla_q_nope_norm_g, mla_q_rope_norm_g=mla_q_rope_norm_g, mla_w_o=mla_w_o)
    M = dict(ada_w=m_ada_w, ada_b=m_ada_b, norm1_g=m_norm1_g, norm2_g=m_norm2_g, ffn_w_gate=m_ffn_w_gate, ffn_w_up=m_ffn_w_up, ffn_w_down=m_ffn_w_down, s5_lam_re=m_s5_lam_re, s5_lam_im=m_s5_lam_im, s5_log_dt=m_s5_log_dt, s5_b_re=m_s5_b_re, s5_b_im=m_s5_b_im, s5_c_re=m_s5_c_re, s5_c_im=m_s5_c_im, s5_d=m_s5_d, s5_w_glu=m_s5_w_glu, s5_b_glu=m_s5_b_glu, kv_ada_w=m_kv_ada_w, kv_ada_b=m_kv_ada_b, kv_norm_g=m_kv_norm_g, w_kv_a=m_w_kv_a, kv_a_norm_g=m_kv_a_norm_g, w_kv_b=m_w_kv_b, k_nope_norm_g=m_k_nope_norm_g, k_rope_norm_g=m_k_rope_norm_g, mla_w_dq=m_mla_w_dq, mla_q_norm_g=m_mla_q_norm_g, mla_w_uq=m_mla_w_uq, mla_q_nope_norm_g=m_mla_q_nope_norm_g, mla_q_rope_norm_g=m_mla_q_rope_norm_g, mla_w_o=m_mla_w_o)
    V = dict(ada_w=v_ada_w, ada_b=v_ada_b, norm1_g=v_norm1_g, norm2_g=v_norm2_g, ffn_w_gate=v_ffn_w_gate, ffn_w_up=v_ffn_w_up, ffn_w_down=v_ffn_w_down, s5_lam_re=v_s5_lam_re, s5_lam_im=v_s5_lam_im, s5_log_dt=v_s5_log_dt, s5_b_re=v_s5_b_re, s5_b_im=v_s5_b_im, s5_c_re=v_s5_c_re, s5_c_im=v_s5_c_im, s5_d=v_s5_d, s5_w_glu=v_s5_w_glu, s5_b_glu=v_s5_b_glu, kv_ada_w=v_kv_ada_w, kv_ada_b=v_kv_ada_b, kv_norm_g=v_kv_norm_g, w_kv_a=v_w_kv_a, kv_a_norm_g=v_kv_a_norm_g, w_kv_b=v_w_kv_b, k_nope_norm_g=v_k_nope_norm_g, k_rope_norm_g=v_k_rope_norm_g, mla_w_dq=v_mla_w_dq, mla_q_norm_g=v_mla_q_norm_g, mla_w_uq=v_mla_w_uq, mla_q_nope_norm_g=v_mla_q_nope_norm_g, mla_q_rope_norm_g=v_mla_q_rope_norm_g, mla_w_o=v_mla_w_o)
    return _step(x[0], c, positions, loss_target[0], W, M, V)


WEIGHT_NAMES = ['ada_w', 'ada_b', 'norm1_g', 'norm2_g', 'ffn_w_gate', 'ffn_w_up', 'ffn_w_down', 's5_lam_re', 's5_lam_im', 's5_log_dt', 's5_b_re', 's5_b_im', 's5_c_re', 's5_c_im', 's5_d', 's5_w_glu', 's5_b_glu', 'kv_ada_w', 'kv_ada_b', 'kv_norm_g', 'w_kv_a', 'kv_a_norm_g', 'w_kv_b', 'k_nope_norm_g', 'k_rope_norm_g', 'mla_w_dq', 'mla_q_norm_g', 'mla_w_uq', 'mla_q_nope_norm_g', 'mla_q_rope_norm_g', 'mla_w_o']
REPLICATED = ['ada_b', 'norm1_g', 'norm2_g', 's5_lam_re', 's5_lam_im', 's5_log_dt', 's5_b_re', 's5_b_im', 's5_c_re', 's5_c_im', 'kv_ada_b', 'kv_norm_g', 'kv_a_norm_g', 'k_nope_norm_g', 'k_rope_norm_g', 'mla_q_norm_g', 'mla_q_nope_norm_g', 'mla_q_rope_norm_g']
SHARDED_VEC = ['s5_d', 's5_b_glu']


def _step(x, c, positions, target, W, M, V):
    s = x.shape[0]
    me = _index(*_me())
    mxu = lambda a: a.astype(_MXU)

    pad_c = lambda a: jnp.pad(a, ((0, 0), (0, FFB - FF // N_DEV)))
    pad_r = lambda a: jnp.pad(a, ((0, FFB - FF // N_DEV), (0, 0)))
    cols = lambda g: g.transpose(1, 0, 2).reshape(g.shape[1], N_DEV * g.shape[2])
    rows = lambda g: g.reshape(N_DEV * g.shape[1], g.shape[2])

    def local_pack(l):
        second = W['s5_w_glu'][l] if l < N_A else W['mla_w_o'][l - N_A]
        arrs = [jnp.concatenate([mxu(pad_c(W['ffn_w_gate'][l])), mxu(pad_c(W['ffn_w_up'][l]))], axis=0),
                jnp.concatenate([mxu(pad_r(W['ffn_w_down'][l])), mxu(second)], axis=0)]
        if l == N_A:
            arrs += [jnp.concatenate([mxu(W['w_kv_b']), mxu(W['mla_w_dq'][0])], axis=0), mxu(W['w_kv_a'])]
        if l > N_A:
            arrs += [mxu(W['mla_w_dq'][l - N_A])]
        if l >= N_A:
            arrs += [mxu(W['mla_w_uq'][l - N_A])]
        return arrs


    def layer_weights(l, after):
        lands = exchange_wait(f"gather_wait_{l}", gathers[l], after, True)
        full = [lax.dynamic_update_slice(ld, src[None], (me,) + (0,) * src.ndim) for ld, src in zip(lands, gathers[l][2])]
        w = {'wg': cols(full[0][:, :D]), 'wu': cols(full[0][:, D:]), 'wd': rows(full[1][:, :FFB]),
             'second': rows(full[1][:, FFB:])}
        if l >= N_A:
            if l == N_A:
                wkvb3 = cols(full[2][:, :KVL]).reshape(KVL, H, DN + DV)
                wkva = rows(full[3])
                w['wa_pad'] = jnp.concatenate([wkva[:, :KVL], jnp.zeros((D, DN), _MXU), wkva[:, KVL:],
                                               jnp.zeros((D, HD - DN - DR), _MXU)], axis=1)
                w['wkn_pad'] = jnp.pad(wkvb3[:, :, :DN], ((0, 0), (0, 0), (0, HD - DN))).reshape(KVL, H * HD)
                w['wv'] = wkvb3[:, :, DN:].reshape(KVL, H * DV)
                w['wdq'] = rows(full[2][:, KVL:])
            else:
                w['wdq'] = rows(full[2])
            w['wuq_pad'] = _pad_heads(cols(full[-1]), DN + DR, HD)
        return w

    vec = jnp.concatenate([c.reshape(-1), W['s5_d'].reshape(-1), W['s5_b_glu'].reshape(-1)]).reshape(1, -1)
    vec = jnp.pad(vec, ((0, 7), (0, 0)))
    gv = all_gather("gather_vectors", vec)[:, 0, :]
    c_all = gv[:, :D]
    d_full = jnp.concatenate([gv[d, D:D + 2 * 128].reshape(N_A, 128) for d in range(N_DEV)], axis=1)
    bglu_full = jnp.concatenate([gv[d, D + 256:D + 512].reshape(N_A, 128) for d in range(N_DEV)], axis=1)

    ca_all = jax.nn.silu(c_all)
    w_mod = jnp.concatenate([W['ada_w'][l] for l in range(DEPTH)] + [W['kv_ada_w']], axis=1)
    n_mod = w_mod.shape[1]
    mod_cols = small_matmul("mod_matmul", ca_all, w_mod)
    gm = all_gather("gather_mod", mod_cols)
    gathers = [exchange_start(f"gather_start_{l}", local_pack(l), True, gm) for l in range(DEPTH)]
    tokens = sum(g[4][0, 0] for g in gathers)
    mine = lax.dynamic_index_in_dim(gm, me, axis=1, keepdims=False) + tokens
    per_l = D * 6 // N_DEV
    mods = []
    for l in range(DEPTH):
        full = jnp.concatenate([mine[d, per_l * l:per_l * (l + 1)] for d in range(N_DEV)]) + W['ada_b'][l]
        mods.append([_row(full[D * i:D * (i + 1)]) for i in range(6)])
    kfull = jnp.concatenate([mine[d, per_l * DEPTH:] for d in range(N_DEV)]) + W['kv_ada_b']
    k_shift, k_scale = _row(kfull[:D]), _row(kfull[D:])

    inv = 1.0 / (ROPE_THETA ** (np.arange(0, DR, 2, dtype=np.float32) / DR))
    inv128 = np.zeros((1, HD), np.float32)
    inv128[0, DN:DN + DR // 2] = inv
    inv128[0, DN + DR // 2:DN + DR] = inv
    cosf, sinf = rope_tables("rope_tables", positions.reshape(s, 1), jnp.asarray(inv128))
    zpad = lambda n: jnp.zeros((n,), F32)
    gkn128 = _row(jnp.concatenate([W['k_nope_norm_g'], zpad(HD - DN)]))
    gkr128 = _row(jnp.concatenate([zpad(DN), W['k_rope_norm_g'], zpad(HD - DN - DR)]))
    gq128 = [_row(jnp.concatenate([W['mla_q_nope_norm_g'][j], W['mla_q_rope_norm_g'][j], zpad(HD - DN - DR)]))
             for j in range(2)]

    expand = jnp.asarray(np.kron(np.eye(G, dtype=np.float32), np.ones((1, N), np.float32)))
    s5_raw, s5_mats = [], []
    for l in range(N_A):
        raw = (_row(W['s5_lam_re'][l]), _row(W['s5_lam_im'][l]), _row(W['s5_log_dt'][l]),
               W['s5_b_re'][l].transpose(2, 0, 1).reshape(P, G * N), W['s5_b_im'][l].transpose(2, 0, 1).reshape(P, G * N))
        ab_re, ab_im, bb_re_t, bb_im_t = s5_prep_fwd(f"s5_prep_fwd", *raw, expand)
        s5_raw.append(raw)
        s5_mats.append(_s5_place(ab_re, ab_im, bb_re_t, bb_im_t, W['s5_c_re'][l], W['s5_c_im'][l]))

    g1 = [_row(W['norm1_g'][l]) for l in range(DEPTH)]
    g2 = [_row(W['norm2_g'][l]) for l in range(DEPTH)]
    saved = []
    xs = x
    kv = None
    lw = [None] * DEPTH
    for l in range(DEPTH):
        sh1, sc1, gt1, sh2, sc2, gt2 = mods[l]
        rec = {'x_in': xs}
        if l >= N_A:
            lw[l] = layer_weights(l, xs)
        if l == N_A:
            kv_smalls = [_row(W['kv_norm_g']), k_shift, k_scale, _row(W['kv_a_norm_g']), gkn128, gkr128]
            kv_w = [lw[l]['wa_pad'], lw[l]['wkn_pad'], lw[l]['wv']]
            k_mat, v_mat = seg_forward("kv_fwd", seg_kv, [xs], kv_smalls, [cosf, sinf], kv_w,
                                       [(H * HD, _MXU), (H * DV, _MXU)], tap_widths=(KVL + HD, H * HD, H * DV))
            kv = {'x_in': xs, 'smalls': kv_smalls, 'k': k_mat, 'v': v_mat, 'w': kv_w}
        if l < N_A:
            (h,) = seg_forward("pre_fwd", seg_pre, [xs], [g1[l], sh1, sc1], [], [], [(D, F32)])
            wb, wc, a_tab = s5_mats[l]
            y, s0 = s5_scan_fwd("s5_scan_fwd", h, wb, wc, a_tab, _row(d_full[l]))
            lw[l] = layer_weights(l, y)
            (x_mid,) = seg_forward("glu_fwd", seg_glu, [xs, y], [gt1, _row(bglu_full[l])], [], [lw[l]['second']],
                                   [(D, F32)], tap_widths=(D,))
            rec.update(h=h, y=y, s0=s0)
        else:
            j = l - N_A
            q_smalls = [g1[l], sh1, sc1, _row(W['mla_q_norm_g'][j]), gq128[j]]
            (q_mat,) = seg_forward("q_fwd", seg_q, [xs], q_smalls, [cosf, sinf], [lw[l]['wdq'], lw[l]['wuq_pad']],
                                   [(H * HD, _MXU)], tap_widths=(QL, H * HD))
            o_mat, lse = attn_fwd("attn_fwd", q_mat, kv['k'], kv['v'])
            (x_mid,) = seg_forward("o_fwd", seg_o, [xs, o_mat], [gt1], [], [lw[l]['second']], [(D, F32)],
                                   tap_widths=(D,))
            rec.update(q=q_mat, o=o_mat, lse=lse, q_smalls=q_smalls)
        rec['x_mid'] = x_mid
        xs, rec['gate'], rec['up'] = ffn_forward("ffn_fwd", x_mid, g2[l], sh2, sc2, gt2,
                                                 lw[l]['wg'], lw[l]['wu'], lw[l]['wd'])
        saved.append(rec)

    dy, loss_part = loss_kernel("loss", xs, target)
    loss = lax.psum(loss_part[0, 0], ("x", "y", "c"))

    rblk = lambda a: a.reshape(N_DEV, a.shape[0] // N_DEV, a.shape[1])
    cblk = lambda a: a.reshape(a.shape[0], N_DEV, a.shape[1] // N_DEV).transpose(1, 0, 2)
    dmod = [None] * DEPTH
    dk_tot = []
    dv_tot = []
    dx = dy
    sends = [None] * DEPTH
    send_token = jnp.zeros((1, 1), F32)
    g_n1 = [None] * DEPTH
    g_n2 = [None] * DEPTH
    g_bglu = [None] * N_A
    g_dskip = [None] * N_A
    g_s5 = [None] * N_A
    g_qn, g_q128 = [None] * 2, [None] * 2
    for l in range(DEPTH - 1, -1, -1):
        rec = saved[l]
        sh1, sc1, gt1, sh2, sc2, gt2 = mods[l]
        dx, dgate, dup, dyd, h_b, a_b, dg2, dsh2, dsc2, dgt2 = ffn_backward(
            "ffn_bwd", rec['x_mid'], dx, rec['gate'], rec['up'], g2[l], sh2, sc2, gt2 + send_token,
            lw[l]['wg'], lw[l]['wu'], lw[l]['wd'])
        out_l = [matmul_tn("tn_ffn_in", h_b, dgate, _MXU, col_blocks=N_DEV),
                 matmul_tn("tn_ffn_in", h_b, dup, _MXU, col_blocks=N_DEV),
                 matmul_tn("tn_ffn_out", a_b, dyd, _MXU).reshape(N_DEV, FFB, D)]
        g_n2[l] = dg2
        if l < N_A:
            (dx, dyy), (dz,), (g_b,), (dgt1, dbg) = seg_backward(
                "glu_bwd", seg_glu, [rec['x_in'], rec['y']], [gt1, _row(bglu_full[l])], [], [lw[l]['second']],
                [dx], (D,), (D,))
            out_l.append(rblk(matmul_tn("tn_sq", g_b, dz, _MXU)))
            g_bglu[l] = dbg
            wb, wc, a_tab = s5_mats[l]
            dh, dwb, dwc, da, dd = s5_scan_bwd("s5_scan_bwd", rec['h'], dyy, rec['s0'], wb, wc, a_tab, _row(d_full[l]))
            g_dskip[l] = dd
            dab_re, dab_im, dbb_re_t, dbb_im_t, dc_re, dc_im = _s5_unplace(dwb, dwc, da)
            dlr, dli, dldt, dbr_t, dbi_t = s5_prep_bwd("s5_prep_bwd", *s5_raw[l], expand,
                                                       (dab_re, dab_im, dbb_re_t, dbb_im_t))
            g_s5[l] = (dlr.reshape(G, N), dli.reshape(G, N), dldt.reshape(G),
                       dbr_t.reshape(P, G, N).transpose(1, 2, 0), dbi_t.reshape(P, G, N).transpose(1, 2, 0), dc_re, dc_im)
            (dx,), _, _, (dg1, dsh1, dsc1) = seg_backward(
                "pre_bwd", seg_pre, [rec['x_in']], [g1[l], sh1, sc1], [], [], [dh], (), (), dx_add=dx)
        else:
            j = l - N_A
            (dx, do), (dzo,), (o_b,), (dgt1,) = seg_backward(
                "o_bwd", seg_o, [rec['x_in'], rec['o']], [gt1], [], [lw[l]['second']], [dx], (D,), (D,))
            out_l.append(rblk(matmul_tn("tn_sq", o_b, dzo, _MXU)))
            dq, dk, dv = attn_bwd("attn_bwd", rec['q'], kv['k'], kv['v'], rec['o'], do, rec['lse'])
            dk_tot.append(dk)
            dv_tot.append(dv)
            (dx,), (dql, dqq), (hq_b, qn_b), (dg1, dsh1, dsc1, dqg, dq128) = seg_backward(
                "q_bwd", seg_q, [rec['x_in']], rec['q_smalls'], [cosf, sinf], [lw[l]['wdq'], lw[l]['wuq_pad']],
                [dq], (QL, H * HD), (D, QL), dx_add=dx)
            g_dq = rblk(matmul_tn("tn_dq", hq_b, dql, _MXU))
            g_uq = cblk(_unpad_heads(matmul_tn("tn_uq", qn_b, dqq, _MXU), DN + DR, HD))
            g_qn[j], g_q128[j] = dqg, dq128
        g_n1[l] = dg1
        dmod[l] = jnp.concatenate([dsh1, dsc1, dgt1, dsh2, dsc2, dgt2], axis=1)
        if l == N_A:
            dkk = sum_parts("sum_dk", jnp.stack(dk_tot))
            dvv = sum_parts("sum_dv", jnp.stack(dv_tot))
            (dx,), (dta, dtk, dtv), (hk_b, ckv_b), (dkg, dksh, dksc, dag, dgkn, dgkr) = seg_backward(
                "kv_bwd", seg_kv, [kv['x_in']], kv['smalls'], [cosf, sinf], kv['w'],
                [dkk, dvv], (KVL + HD, H * HD, H * DV), (D, KVL), dx_add=dx)
            g_wa = matmul_tn("tn_kva", hk_b, dta, _MXU)
            g_wa = jnp.concatenate([g_wa[:, :KVL], g_wa[:, KVL + DN:KVL + DN + DR]], axis=1)
            g_kn = matmul_tn("tn_kn", ckv_b, dtk, _MXU).reshape(KVL, H, HD)[:, :, :DN]
            g_v = matmul_tn("tn_v", ckv_b, dtv, _MXU).reshape(KVL, H, DV)
            g_wkvb = jnp.concatenate([g_kn, g_v], axis=2).reshape(KVL, H * (DN + DV))
            dkmod = jnp.concatenate([dksh, dksc], axis=1)
            out_l += [jnp.concatenate([cblk(g_wkvb), g_dq], axis=1), rblk(g_wa)]
        if l > N_A:
            out_l.append(g_dq)
        if l >= N_A:
            out_l.append(g_uq)
        if l > 0:
            sends[l] = exchange_start(f"a2a_start_{l}", out_l, False, dx)
            send_token = sends[l][4][0:1, 0:1]
    grad_x = dx

    small = {
        'norm1_g': jnp.concatenate(g_n1, axis=0), 'norm2_g': jnp.concatenate(g_n2, axis=0),
        's5_lam_re': jnp.stack([g_s5[l][0] for l in range(N_A)]), 's5_lam_im': jnp.stack([g_s5[l][1] for l in range(N_A)]),
        's5_log_dt': jnp.stack([g_s5[l][2] for l in range(N_A)]),
        's5_b_re': jnp.stack([g_s5[l][3] for l in range(N_A)]), 's5_b_im': jnp.stack([g_s5[l][4] for l in range(N_A)]),
        's5_c_re': jnp.stack([g_s5[l][5] for l in range(N_A)]), 's5_c_im': jnp.stack([g_s5[l][6] for l in range(N_A)]),
        'kv_norm_g': dkg, 'kv_a_norm_g': dag, 'k_nope_norm_g': dgkn[:, :DN], 'k_rope_norm_g': dgkr[:, DN:DN + DR],
        'mla_q_norm_g': jnp.concatenate(g_qn, axis=0),
        'mla_q_nope_norm_g': jnp.concatenate([g[:, :DN] for g in g_q128], axis=0),
        'mla_q_rope_norm_g': jnp.concatenate([g[:, DN:DN + DR] for g in g_q128], axis=0),
        's5_d': jnp.concatenate(g_dskip, axis=0), 's5_b_glu': jnp.concatenate(g_bglu, axis=0),
    }
    small_names = [n for n in REPLICATED if n not in ('ada_b', 'kv_ada_b')] + SHARDED_VEC
    flat_small = jnp.concatenate([small[n].reshape(-1) for n in small_names])
    n_small = int(flat_small.shape[0])
    pad_small = -(-n_small // 65536) * 65536
    flat_small = jnp.pad(flat_small, (0, pad_small - n_small)).reshape(pad_small // 128, 128)

    dm = jnp.concatenate(dmod + [dkmod], axis=1)[0]
    per_dev = []
    for d in range(N_DEV):
        cols = [dm[6 * D * l + per_l * d:6 * D * l + per_l * (d + 1)] for l in range(DEPTH)]
        cols.append(dm[6 * D * DEPTH + (2 * D // N_DEV) * d:6 * D * DEPTH + (2 * D // N_DEV) * (d + 1)])
        per_dev.append(jnp.concatenate(cols))
    dm_dev = jnp.stack(per_dev)
    gdm = all_gather("gather_dmod", dm_dev)
    small_st = exchange_start("small_start", [flat_small], True, gdm)
    sends[0] = exchange_start("a2a_start_0", out_l, False, small_st[4])
    dm_mine = lax.dynamic_index_in_dim(gdm, me, axis=1, keepdims=False) + sends[0][4][0, 0]
    g_wmod = small_matmul_tn("dmod_matmul", ca_all, dm_mine)
    g_ada_w = jnp.stack([g_wmod[:, per_l * l:per_l * (l + 1)] for l in range(DEPTH)])
    g_kv_ada_w = g_wmod[:, per_l * DEPTH:]
    dm_sum = sum_parts("sum_dmod", gdm.reshape(N_DEV, N_DEV, n_mod))
    g_ada_b = jnp.stack([jnp.concatenate([dm_sum[d, per_l * l:per_l * (l + 1)] for d in range(N_DEV)])
                         for l in range(DEPTH)])
    g_kv_ada_b = jnp.concatenate([dm_sum[d, per_l * DEPTH:] for d in range(N_DEV)])

    grads, out_delta, out_m, out_v = {}, {}, {}, {}

    def update(name, parts, base=0, stride=0):
        shp = W[name].shape
        shp3 = shp if len(shp) == 3 else (1,) + shp
        res = adamw("adamw_" + name, parts, W[name].reshape(shp3), M[name].reshape(shp3), V[name].reshape(shp3),
                    base, stride)
        grads[name], out_delta[name], out_m[name], out_v[name] = (a.reshape(shp) for a in res)

    update('ada_w', g_ada_w.reshape(1, DEPTH * D, per_l), 0, D)
    update('kv_ada_w', g_kv_ada_w[None])

    chains = {}

    def update_layer(name, parts, layer, base=0):
        shp = W[name].shape
        shp3 = shp if len(shp) == 3 else (1,) + shp
        chains[name] = adamw_layer(f"adamw_{name}_{layer}", parts, W[name].reshape(shp3), M[name].reshape(shp3),
                                   V[name].reshape(shp3), layer, chains.get(name), base)
        grads[name], out_delta[name], out_m[name], out_v[name] = (a.reshape(shp) for a in chains[name])

    ffn_parts = [[None] * DEPTH for _ in range(3)]

    def receive(l, after):
        lands = exchange_wait(f"a2a_wait_{l}", sends[l], after, False)
        recv = [lax.dynamic_update_slice(ld, lax.dynamic_index_in_dim(src, me, 0, keepdims=True), (me,) + (0,) * (src.ndim - 1))
                for ld, src in zip(lands, sends[l][2])]
        for i in range(3):
            ffn_parts[i][l] = recv[i]
        if l < N_A:
            update_layer('s5_w_glu', recv[3], l)
        else:
            update_layer('mla_w_o', recv[3], l - N_A)
            if l == N_A:
                update_layer('w_kv_b', recv[4], 0)
                update_layer('mla_w_dq', recv[4], 0, KVL)
                update_layer('w_kv_a', recv[5], 0)
            else:
                update_layer('mla_w_dq', recv[4], l - N_A)
            update_layer('mla_w_uq', recv[-1], l - N_A)

    for l in range(DEPTH - 1, 0, -1):
        receive(l, out_delta['kv_ada_w'])

    (small_land,) = exchange_wait("small_wait", small_st, chains['s5_w_glu'][1], True)
    small_all = lax.dynamic_update_slice(small_land, flat_small[None], (me, 0, 0))
    g_small_sum = sum_parts("sum_small", small_all).reshape(-1)
    off = 0
    for n in small_names:
        size = int(np.prod(small[n].shape))
        full = g_small_sum[off:off + size]
        off += size
        if n in SHARDED_VEC:
            full = lax.dynamic_slice_in_dim(full.reshape(N_A, D), me * (D // N_DEV), D // N_DEV, axis=1)
        grads[n] = full.reshape(W[n].shape)
    grads['ada_b'] = g_ada_b
    grads['kv_ada_b'] = g_kv_ada_b

    packed_names = REPLICATED + SHARDED_VEC

    def pack(dct):
        flat_ = jnp.concatenate([dct[n].reshape(-1) for n in packed_names])
        n_ = int(flat_.shape[0])
        p_ = -(-n_ // 65536) * 65536
        return jnp.pad(flat_, (0, p_ - n_)).reshape(p_ // 128, 128)

    _, d_p, m_p, v_p = adamw("adamw_small", pack(grads)[None], pack(W)[None], pack(M)[None], pack(V)[None])
    off = 0
    d_p, m_p, v_p = d_p.reshape(-1), m_p.reshape(-1), v_p.reshape(-1)
    for n in packed_names:
        size = int(np.prod(W[n].shape))
        out_delta[n] = d_p[off:off + size].reshape(W[n].shape)
        out_m[n] = m_p[off:off + size].reshape(W[n].shape)
        out_v[n] = v_p[off:off + size].reshape(W[n].shape)
        off += size

    receive(0, d_p)
    for i, name in enumerate(('ffn_w_gate', 'ffn_w_up', 'ffn_w_down')):
        res = adamw_multi("adamw_" + name, ffn_parts[i], W[name], M[name], V[name])
        grads[name], out_delta[name], out_m[name], out_v[name] = res

    return (loss, grad_x[None], *[grads[n] for n in WEIGHT_NAMES], *[out_delta[n] for n in WEIGHT_NAMES],
            *[out_m[n] for n in WEIGHT_NAMES], *[out_v[n] for n in WEIGHT_NAMES])
```

```python
import functools
import math

import numpy as np
import jax
import jax.numpy as jnp
from jax import lax
from jax.experimental import pallas as pl
from jax.experimental.pallas import tpu as pltpu

F32 = jnp.float32
_MXU = jnp.bfloat16
HI = lax.Precision.HIGHEST

D = 1024
DEPTH = 4
N_A = 2
FF = 2816
FFB = 384
FFP = 8 * FFB
N_DEV = 8
G = 64
P = 16
N = 64
GB = 8
NBLK = G // GB
HALF = GB * N
H = 16
HP = H // 2
DN, DR, DV = 64, 32, 64
HD = 128
QL = 256
KVL = 256
CHUNK = 64
ROPE_THETA = 10000.0
ATTN_SCALE = 1.0 / math.sqrt(DN + DR)
LOG2E = 1.4426950408889634
EXP2_SCALE = ATTN_SCALE * LOG2E
EPS = 1e-6
ADAM_LR, ADAM_B1, ADAM_B2, ADAM_EPS, ADAM_WD, ADAM_STEP = 0.001, 0.9, 0.999, 1e-08, 0.01, 10
VMEM_LIMIT = 56 * 1024 * 1024
MESH = pl.DeviceIdType.MESH

TILE_ROW = 256
TILE_ATT = 512
TILE_ATT_FWD = 512
TILE_ATT_KEYS = 512
TILE_SCAN = 512


def _params(n_grid):
    return pltpu.CompilerParams(dimension_semantics=("arbitrary",) * n_grid, vmem_limit_bytes=VMEM_LIMIT)


@jax.custom_vjp
def mm(a, w):
    return jnp.dot(a.astype(_MXU), w, preferred_element_type=F32)


def _mm_fwd(a, w):
    return mm(a, w), w


def _mm_bwd(w, g):
    da = lax.dot_general(g.astype(_MXU), w, (((1,), (1,)), ((), ())), preferred_element_type=F32)
    return da, jnp.zeros_like(w)


mm.defvjp(_mm_fwd, _mm_bwd)


def rms(x, g):
    return x * lax.rsqrt(jnp.mean(x * x, axis=-1, keepdims=True) + EPS) * g


def modulate(h, shift, scale):
    return h * (1.0 + scale) + shift


def _lane(n=HD):
    return lax.broadcasted_iota(jnp.int32, (1, n), 1)


def _rot_matrix():
    r = lax.broadcasted_iota(jnp.int32, (HD, HD), 0)
    c = lax.broadcasted_iota(jnp.int32, (HD, HD), 1)
    first = (c >= DN) & (c < DN + DR // 2) & (r == c + DR // 2)
    second = (c >= DN + DR // 2) & (c < DN + DR) & (r == c - DR // 2)
    return jnp.where(first, -1.0, jnp.where(second, 1.0, 0.0)).astype(F32)


def head_norm_rope(xh, g128, cosf, sinf, rot, with_nope):
    lane = _lane()
    m_n = lane < DN
    m_r = (lane >= DN) & (lane < DN + DR)
    sq = xh * xh
    inv_r = lax.rsqrt(jnp.sum(jnp.where(m_r, sq, 0.0), axis=-1, keepdims=True) / DR + EPS)
    if with_nope:
        inv_n = lax.rsqrt(jnp.sum(jnp.where(m_n, sq, 0.0), axis=-1, keepdims=True) / DN + EPS)
        inv = jnp.where(m_n, inv_n, jnp.where(m_r, inv_r, 0.0))
    else:
        inv = jnp.where(m_r, inv_r, 0.0)
    xg = xh * inv * g128
    return xg * cosf + jnp.dot(xg, rot, precision=HI, preferred_element_type=F32) * sinf


def seg_pre(x, g, sh, sc):
    return (modulate(rms(x, g), sh, sc),), ()


def seg_ffn(x, g, sh, sc, gt, t_g, t_u, t_d, wg, wu, wd):
    h = modulate(rms(x, g), sh, sc)
    gate = mm(h, wg) + t_g
    up = mm(h, wu) + t_u
    a = jax.nn.silu(gate) * up
    y = mm(a, wd) + t_d
    return (x + gt * y,), (h.astype(_MXU), a.astype(_MXU))


def seg_glu(x, y, gt, b, t_z, w):
    g = jax.nn.gelu(y)
    z = mm(g, w) + b + t_z
    return (x + gt * (g * jax.nn.sigmoid(z)),), (g.astype(_MXU),)


def seg_o(x, o, gt, t_o, w):
    return (x + gt * (mm(o, w) + t_o),), (o.astype(_MXU),)


def seg_q(x, g, sh, sc, qg, g128, t_l, t_q, cosf, sinf, wdq, wuq):
    h = modulate(rms(x, g), sh, sc)
    ql = mm(h, wdq) + t_l
    qn = rms(ql, qg)
    q = mm(qn, wuq) + t_q
    rot = _rot_matrix()
    heads = [head_norm_rope(q[:, HD * i:HD * (i + 1)], g128, cosf, sinf, rot, True) for i in range(H)]
    return (jnp.concatenate(heads, axis=1),), (h.astype(_MXU), qn.astype(_MXU))


def seg_kv(x, g, sh, sc, ag, gkn, gkr, t_a, t_k, t_v, cosf, sinf, wa, wkn, wv):
    hk = modulate(rms(x, g), sh, sc)
    kva = mm(hk, wa) + t_a
    ckv = rms(kva[:, :KVL], ag)
    kr = head_norm_rope(kva[:, KVL:KVL + HD], gkr, cosf, sinf, _rot_matrix(), False)
    kn = mm(ckv, wkn) + t_k
    v = mm(ckv, wv) + t_v
    heads = []
    for i in range(H):
        kh = kn[:, HD * i:HD * (i + 1)]
        inv = lax.rsqrt(jnp.sum(kh * kh, axis=-1, keepdims=True) / DN + EPS)
        heads.append(kh * inv * gkn + kr)
    return (jnp.concatenate(heads, axis=1), v), (hk.astype(_MXU), ckv.astype(_MXU))


def _row_call(name, body_fn, rows, fulls, out_rows, out_accs, tile):
    s = rows[0].shape[0]
    n_tiles = s // tile
    n_rows, n_fulls, n_or, n_oa = len(rows), len(fulls), len(out_rows), len(out_accs)

    def kern(*refs):
        i = pl.program_id(0)
        row_v = [r[...] for r in refs[:n_rows]]
        full_v = [r[...] for r in refs[n_rows:n_rows + n_fulls]]
        o_refs = refs[n_rows + n_fulls:]
        ro, ao = body_fn(row_v, full_v)
        for r, v in zip(o_refs[:n_or], ro):
            r[...] = v.astype(r.dtype)
        if n_oa:
            @pl.when(i == 0)
            def _():
                for r in o_refs[n_or:]:
                    r[...] = jnp.zeros(r.shape, r.dtype)
            for r, v in zip(o_refs[n_or:], ao):
                r[...] += v.astype(r.dtype)

    in_specs = [pl.BlockSpec((tile, a.shape[1]), lambda i: (i, 0)) for a in rows]
    for a in fulls:
        big = a.size * a.dtype.itemsize > (1 << 20)
        nd = a.ndim
        in_specs.append(pl.BlockSpec(a.shape, functools.partial(lambda i, nd_: (0,) * nd_, nd_=nd),
                                     **({"pipeline_mode": pl.Buffered(1)} if big else {})))
    out_shape = [jax.ShapeDtypeStruct((s, w), dt) for w, dt in out_rows]
    out_shape += [jax.ShapeDtypeStruct(shp, dt) for shp, dt in out_accs]
    out_specs = [pl.BlockSpec((tile, w), lambda i: (i, 0)) for w, _ in out_rows]
    out_specs += [pl.BlockSpec(shp, functools.partial(lambda i, nd_: (0,) * nd_, nd_=len(shp))) for shp, _ in out_accs]
    res = pl.pallas_call(kern, out_shape=out_shape, grid=(n_tiles,), in_specs=in_specs, out_specs=out_specs,
                         name=name, compiler_params=_params(1))(*rows, *fulls)
    return list(res)


def seg_forward(name, seg, rows, smalls, consts_rows, consts_full, out_widths, tile=TILE_ROW, tap_widths=()):
    n_r, n_s, n_cr = len(rows), len(smalls), len(consts_rows)

    def body(row_v, full_v):
        t = row_v[0].shape[0]
        taps = [jnp.zeros((t, w), F32) for w in tap_widths]
        outs, _ = seg(*row_v[:n_r], *full_v[:n_s], *taps, *row_v[n_r:], *full_v[n_s:])
        return outs, ()

    return _row_call(name, body, list(rows) + list(consts_rows), list(smalls) + list(consts_full),
                     out_widths, [], tile)


def seg_backward(name, seg, rows, smalls, consts_rows, consts_full, cots, tap_widths, aux_widths,
                 dx_add=None, tile=TILE_ROW):
    n_r, n_s, n_cr, n_c = len(rows), len(smalls), len(consts_rows), len(cots)
    has_add = dx_add is not None

    def body(row_v, full_v):
        t = row_v[0].shape[0]
        prim_rows = row_v[:n_r]
        c_rows = row_v[n_r:n_r + n_cr]
        cot_v = row_v[n_r + n_cr:n_r + n_cr + n_c]
        add_v = row_v[n_r + n_cr + n_c] if has_add else None
        small_v = full_v[:n_s]
        c_full = full_v[n_s:]
        taps = [jnp.zeros((t, w), F32) for w in tap_widths]

        def f(*args):
            return seg(*args, *c_rows, *c_full)

        _, vjp_fn, aux = jax.vjp(f, *prim_rows, *small_v, *taps, has_aux=True)
        grads = vjp_fn(tuple(c.astype(F32) for c in cot_v))
        d_rows = list(grads[:n_r])
        if has_add:
            d_rows[0] = d_rows[0] + add_v
        d_small = grads[n_r:n_r + n_s]
        d_taps = grads[n_r + n_s:]
        return d_rows + list(d_taps) + list(aux), [jnp.sum(g, axis=0, keepdims=True) if g.shape[0] != 1 else g
                                                   for g in d_small]

    all_rows = list(rows) + list(consts_rows) + list(cots) + ([dx_add] if has_add else [])
    out_rows = [(a.shape[1], F32) for a in rows] + [(w, _MXU) for w in tap_widths] + [(w, _MXU) for w in aux_widths]
    out_accs = [((1, a.shape[1]), F32) for a in smalls]
    res = _row_call(name, body, all_rows, list(smalls) + list(consts_full), out_rows, out_accs, tile)
    n_t, n_a = len(tap_widths), len(aux_widths)
    return res[:n_r], res[n_r:n_r + n_t], res[n_r + n_t:n_r + n_t + n_a], res[n_r + n_t + n_a:]


def _split(n):
    if n <= 1024:
        return n
    for t in (1408, 1024, 768, 512, 256, 128):
        if n % t == 0:
            return t
    raise ValueError(n)


def matmul_tn(name, a, b, out_dtype, col_blocks=None):
    s, k1 = a.shape
    _, k2 = b.shape
    tm, ts = _split(k1), 2048
    if col_blocks is None:
        tn, per_step, wblk = _split(k2), 1, None
    else:
        wblk = k2 // col_blocks
        per_step = max(1, min(col_blocks, 1536 // wblk))
        tn = per_step * wblk
    n_s = s // ts

    def kern(a_ref, b_ref, o_ref, acc_ref):
        k = pl.program_id(2)

        @pl.when(k == 0)
        def _():
            acc_ref[...] = jnp.zeros(acc_ref.shape, F32)

        acc_ref[...] += lax.dot_general(a_ref[...], b_ref[...], (((0,), (0,)), ((), ())),
                                        preferred_element_type=F32)

        @pl.when(k == n_s - 1)
        def _():
            if col_blocks is None:
                o_ref[...] = acc_ref[...].astype(o_ref.dtype)
            else:
                for cb in range(per_step):
                    o_ref[cb] = acc_ref[:, wblk * cb:wblk * (cb + 1)].astype(o_ref.dtype)

    if col_blocks is None:
        out_shape = jax.ShapeDtypeStruct((k1, k2), out_dtype)
        out_spec = pl.BlockSpec((tm, tn), lambda i, j, k: (i, j))
    else:
        out_shape = jax.ShapeDtypeStruct((col_blocks, k1, wblk), out_dtype)
        out_spec = pl.BlockSpec((per_step, tm, wblk), lambda i, j, k: (j, i, 0))
    return pl.pallas_call(
        kern, out_shape=out_shape, grid=(k1 // tm, k2 // tn, n_s),
        in_specs=[pl.BlockSpec((ts, tm), lambda i, j, k: (k, i)), pl.BlockSpec((ts, tn), lambda i, j, k: (k, j))],
        out_specs=out_spec,
        scratch_shapes=[pltpu.VMEM((tm, tn), F32)], name=name, compiler_params=_params(3))(a, b)


def ffn_forward(name, x, g, sh, sc, gt, wg, wu, wd, tile=TILE_ROW):
    s = x.shape[0]
    fp = wg.shape[1]
    blk = 2 * FFB
    n_blk = fp // blk

    def kern(x_ref, g_ref, sh_ref, sc_ref, gt_ref, wg_ref, wu_ref, wd_ref, o_ref, gate_ref, up_ref):
        xv = x_ref[...]
        hb = modulate(rms(xv, g_ref[...]), sh_ref[...], sc_ref[...]).astype(_MXU)
        y = jnp.zeros((tile, D), F32)
        for c in range(n_blk):
            cs = slice(blk * c, blk * (c + 1))
            gate = jnp.dot(hb, wg_ref[:, cs], preferred_element_type=F32)
            up = jnp.dot(hb, wu_ref[:, cs], preferred_element_type=F32)
            gate_ref[:, cs] = gate.astype(_MXU)
            up_ref[:, cs] = up.astype(_MXU)
            y = y + jnp.dot((jax.nn.silu(gate) * up).astype(_MXU), wd_ref[cs, :], preferred_element_type=F32)
        o_ref[...] = xv + gt_ref[...] * y

    row = lambda w: pl.BlockSpec((tile, w), lambda i: (i, 0))
    vec = pl.BlockSpec((1, D), lambda i: (0, 0))
    wspec = lambda a: pl.BlockSpec(a.shape, lambda i: (0, 0), pipeline_mode=pl.Buffered(1))
    return pl.pallas_call(
        kern, out_shape=[jax.ShapeDtypeStruct((s, D), F32), jax.ShapeDtypeStruct((s, fp), _MXU),
                         jax.ShapeDtypeStruct((s, fp), _MXU)],
        grid=(s // tile,), in_specs=[row(D), vec, vec, vec, vec, wspec(wg), wspec(wu), wspec(wd)],
        out_specs=[row(D), row(fp), row(fp)], name=name, compiler_params=_params(1))(x, g, sh, sc, gt, wg, wu, wd)


def ffn_backward(name, x, dxo, gate, up, g, sh, sc, gt, wg, wu, wd, tile=TILE_ROW):
    s = x.shape[0]
    fp = wg.shape[1]
    blk = 2 * FFB
    n_blk = fp // blk

    def kern(x_ref, dxo_ref, gate_ref, up_ref, g_ref, sh_ref, sc_ref, gt_ref, wg_ref, wu_ref, wd_ref,
             dx_ref, dg_ref, du_ref, dy_ref, h_ref, a_ref, dgn_ref, dsh_ref, dsc_ref, dgt_ref):
        i = pl.program_id(0)

        @pl.when(i == 0)
        def _():
            for r in (dgn_ref, dsh_ref, dsc_ref, dgt_ref):
                r[...] = jnp.zeros(r.shape, F32)

        dxo = dxo_ref[...]
        h, pre_vjp = jax.vjp(lambda *p: modulate(rms(p[0], p[1]), p[2], p[3]), x_ref[...], g_ref[...], sh_ref[...],
                             sc_ref[...])
        h_ref[...] = h.astype(_MXU)
        dyb = (gt_ref[...] * dxo).astype(_MXU)
        dy_ref[...] = dyb
        y = jnp.zeros((tile, D), F32)
        dh = jnp.zeros((tile, D), F32)
        tr = (((1,), (1,)), ((), ()))
        for c in range(n_blk):
            cs = slice(blk * c, blk * (c + 1))
            gate = gate_ref[:, cs].astype(F32)
            up = up_ref[:, cs].astype(F32)
            sig = jax.nn.sigmoid(gate)
            sl = gate * sig
            ab = (sl * up).astype(_MXU)
            a_ref[:, cs] = ab
            y = y + jnp.dot(ab, wd_ref[cs, :], preferred_element_type=F32)
            da = lax.dot_general(dyb, wd_ref[cs, :], tr, preferred_element_type=F32)
            dgb = (da * up * (sig * (1.0 + gate * (1.0 - sig)))).astype(_MXU)
            dub = (da * sl).astype(_MXU)
            dg_ref[:, cs] = dgb
            du_ref[:, cs] = dub
            dh = dh + lax.dot_general(dgb, wg_ref[:, cs], tr, preferred_element_type=F32) \
                + lax.dot_general(dub, wu_ref[:, cs], tr, preferred_element_type=F32)
        dgt_ref[...] += jnp.sum(dxo * y, axis=0, keepdims=True)
        dx_pre, dgn, dsh, dsc = pre_vjp(dh)
        dx_ref[...] = dxo + dx_pre
        dgn_ref[...] += dgn
        dsh_ref[...] += dsh
        dsc_ref[...] += dsc

    row = lambda w: pl.BlockSpec((tile, w), lambda i: (i, 0))
    vec = pl.BlockSpec((1, D), lambda i: (0, 0))
    wspec = lambda a: pl.BlockSpec(a.shape, lambda i: (0, 0), pipeline_mode=pl.Buffered(1))
    rows_out = [(D, F32), (fp, _MXU), (fp, _MXU), (D, _MXU), (D, _MXU), (fp, _MXU)]
    return pl.pallas_call(
        kern,
        out_shape=[jax.ShapeDtypeStruct((s, w), dt) for w, dt in rows_out] + [jax.ShapeDtypeStruct((1, D), F32)] * 4,
        grid=(s // tile,),
        in_specs=[row(D), row(D), row(fp), row(fp), vec, vec, vec, vec, wspec(wg), wspec(wu), wspec(wd)],
        out_specs=[row(w) for w, _ in rows_out] + [vec] * 4,
        name=name, compiler_params=_params(1))(x, dxo, gate, up, g, sh, sc, gt, wg, wu, wd)


def small_matmul(name, a, w, tn=256):
    m, k = a.shape
    n = w.shape[1]

    def kern(a_ref, w_ref, o_ref):
        o_ref[...] = jnp.dot(a_ref[...].astype(_MXU), w_ref[...].astype(_MXU), preferred_element_type=F32)

    return pl.pallas_call(kern, out_shape=jax.ShapeDtypeStruct((m, n), F32), grid=(n // tn,),
                          in_specs=[pl.BlockSpec((m, k), lambda j: (0, 0)), pl.BlockSpec((k, tn), lambda j: (0, j))],
                          out_specs=pl.BlockSpec((m, tn), lambda j: (0, j)), name=name,
                          compiler_params=_params(1))(a, w)


def small_matmul_tn(name, a, b, tn=256):
    m, k = a.shape
    n = b.shape[1]

    def kern(a_ref, b_ref, o_ref):
        o_ref[...] = lax.dot_general(a_ref[...].astype(_MXU), b_ref[...].astype(_MXU), (((0,), (0,)), ((), ())),
                                     preferred_element_type=F32)

    return pl.pallas_call(kern, out_shape=jax.ShapeDtypeStruct((k, n), F32), grid=(n // tn,),
                          in_specs=[pl.BlockSpec((m, k), lambda j: (0, 0)), pl.BlockSpec((m, tn), lambda j: (0, j))],
                          out_specs=pl.BlockSpec((k, tn), lambda j: (0, j)), name=name,
                          compiler_params=_params(1))(a, b)


def _s5_prep_math(lam_re, lam_im, log_dt, b_re_t, b_im_t, expand):
    dt = jnp.dot(jnp.exp(log_dt), expand, precision=HI, preferred_element_type=F32)
    mag = jnp.exp(lam_re * dt)
    ab_re = mag * jnp.cos(lam_im * dt)
    ab_im = mag * jnp.sin(lam_im * dt)
    den = lam_re * lam_re + lam_im * lam_im
    nr = ab_re - 1.0
    ni = ab_im
    f_re = (nr * lam_re + ni * lam_im) / den
    f_im = (ni * lam_re - nr * lam_im) / den
    bb_re = f_re * b_re_t - f_im * b_im_t
    bb_im = f_re * b_im_t + f_im * b_re_t
    return ab_re, ab_im, bb_re, bb_im


def _whole(kern, name, out_shape, *args):
    return pl.pallas_call(kern, out_shape=out_shape, name=name,
                          compiler_params=pltpu.CompilerParams(vmem_limit_bytes=VMEM_LIMIT))(*args)


def s5_prep_fwd(name, lam_re, lam_im, log_dt, b_re_t, b_im_t, expand):
    def kern(a, b, c, d, e, f, o0, o1, o2, o3):
        r = _s5_prep_math(a[...], b[...], c[...], d[...], e[...], f[...])
        for o, v in zip((o0, o1, o2, o3), r):
            o[...] = v

    gn = lam_re.shape[1]
    shp = [jax.ShapeDtypeStruct((1, gn), F32)] * 2 + [jax.ShapeDtypeStruct((P, gn), F32)] * 2
    return _whole(kern, name, shp, lam_re, lam_im, log_dt, b_re_t, b_im_t, expand)


def s5_prep_bwd(name, lam_re, lam_im, log_dt, b_re_t, b_im_t, expand, cots):
    def kern(a, b, c, d, e, f, c0, c1, c2, c3, o0, o1, o2, o3, o4):
        ex = f[...]
        _, vjp_fn = jax.vjp(lambda *p: _s5_prep_math(*p, ex), a[...], b[...], c[...], d[...], e[...])
        g = vjp_fn((c0[...], c1[...], c2[...], c3[...]))
        for o, v in zip((o0, o1, o2, o3, o4), g):
            o[...] = v

    shp = [jax.ShapeDtypeStruct(a.shape, F32) for a in (lam_re, lam_im, log_dt, b_re_t, b_im_t)]
    return _whole(kern, name, shp, lam_re, lam_im, log_dt, b_re_t, b_im_t, expand, *cots)


def _cpowers(ar, ai):
    pw = [(ar, ai)]
    for _ in range(7):
        pr, pi = pw[-1]
        pw.append((pr * ar - pi * ai, pr * ai + pi * ar))
    return pw


def _row_select(row, values):
    out = jnp.broadcast_to(values[7], (8, values[7].shape[1]))
    for r in range(6, -1, -1):
        out = jnp.where(row == r, values[r], out)
    return out


def _scan_tables(ar, ai, reverse):
    pw = _cpowers(ar, ai)
    row = lax.broadcasted_iota(jnp.int32, (8, ar.shape[1]), 0)
    steps = []
    for d in (1, 2, 4):
        keep = (row <= 7 - d) if reverse else (row >= d)
        steps.append((jnp.where(keep, pw[d - 1][0], 0.0), jnp.where(keep, pw[d - 1][1], 0.0)))
    order = list(range(7, -1, -1)) if reverse else list(range(8))
    carry = (_row_select(row, [pw[i][0] for i in order]), _row_select(row, [pw[i][1] for i in order]))
    return steps, carry


def _tile_scan_fwd(xr, xi, cr, ci, steps, carry_m):
    for d, (mr, mi) in zip((1, 2, 4), steps):
        sr = pltpu.roll(xr, d, 0)
        si = pltpu.roll(xi, d, 0)
        xr, xi = xr + mr * sr - mi * si, xi + mr * si + mi * sr
    pr, pi = carry_m
    return xr + pr * cr - pi * ci, xi + pr * ci + pi * cr


def _tile_scan_rev(xr, xi, cr, ci, steps, carry_m):
    for d, (mr, mi) in zip((1, 2, 4), steps):
        sr = pltpu.roll(xr, 8 - d, 0)
        si = pltpu.roll(xi, 8 - d, 0)
        xr, xi = xr + mr * sr + mi * si, xi + mr * si - mi * sr
    pr, pi = carry_m
    return xr + pr * cr + pi * ci, xi + pr * ci - pi * cr


def _fwd_scan_block(buf, row0, n_tiles8, ar, ai, c0r, c0i):
    steps, carry_m = _scan_tables(ar, ai, False)

    def body(j, carry):
        cr, ci = carry
        r0 = pl.multiple_of(row0 + j * 8, 8)
        xr = buf[pl.ds(r0, 8), 0:HALF]
        xi = buf[pl.ds(r0, 8), HALF:2 * HALF]
        xr, xi = _tile_scan_fwd(xr, xi, cr, ci, steps, carry_m)
        buf[pl.ds(r0, 8), 0:HALF] = xr
        buf[pl.ds(r0, 8), HALF:2 * HALF] = xi
        return xr[7:8], xi[7:8]

    return lax.fori_loop(0, n_tiles8, body, (c0r, c0i))


def s5_scan_fwd(name, h, wb, wc, a_tab, dskip, tile=TILE_SCAN):
    s = h.shape[0]
    n_t = s // tile

    def kern(h_ref, wb_ref, wc_ref, a_ref, d_ref, y_ref, s0_ref, carry_ref, buf):
        i = pl.program_id(0)

        @pl.when(i == 0)
        def _():
            carry_ref[...] = jnp.zeros(carry_ref.shape, F32)

        s0_ref[0] = carry_ref[...]
        for k in range(NBLK):
            cols = slice(GB * P * k, GB * P * (k + 1))
            u = h_ref[:, cols]
            buf[...] = jnp.dot(u.astype(_MXU), wb_ref[k], preferred_element_type=F32)
            ar = a_ref[k, :, 0:HALF]
            ai = a_ref[k, :, HALF:2 * HALF]
            cr, ci = _fwd_scan_block(buf, 0, tile // 8, ar, ai, carry_ref[k:k + 1, 0:HALF],
                                     carry_ref[k:k + 1, HALF:2 * HALF])
            carry_ref[k:k + 1, 0:HALF] = cr
            carry_ref[k:k + 1, HALF:2 * HALF] = ci
            y_ref[:, cols] = jnp.dot(buf[...].astype(_MXU), wc_ref[k], preferred_element_type=F32) + d_ref[:, cols] * u

    full = lambda a: pl.BlockSpec(a.shape, functools.partial(lambda i, nd_: (0,) * nd_, nd_=a.ndim))
    return pl.pallas_call(
        kern,
        out_shape=[jax.ShapeDtypeStruct((s, D), F32), jax.ShapeDtypeStruct((n_t, NBLK, 2 * HALF), F32)],
        grid=(n_t,),
        in_specs=[pl.BlockSpec((tile, D), lambda i: (i, 0)), full(wb), full(wc), full(a_tab), full(dskip)],
        out_specs=[pl.BlockSpec((tile, D), lambda i: (i, 0)), pl.BlockSpec((1, NBLK, 2 * HALF), lambda i: (i, 0, 0))],
        scratch_shapes=[pltpu.VMEM((NBLK, 2 * HALF), F32), pltpu.VMEM((tile, 2 * HALF), F32)],
        name=name, compiler_params=_params(1))(h, wb, wc, a_tab, dskip)


def s5_scan_bwd(name, h, dy, s0, wb, wc, a_tab, dskip, tile=TILE_SCAN):
    s = h.shape[0]
    n_t = s // tile
    n8 = tile // 8

    def kern(h_ref, dy_ref, s0_ref, wb_ref, wc_ref, a_ref, d_ref, dh_ref, dwb_ref, dwc_ref, da_ref, dd_ref,
             lam_ref, sbuf, gbuf):
        i = pl.program_id(0)

        @pl.when(i == 0)
        def _():
            lam_ref[...] = jnp.zeros(lam_ref.shape, F32)
            dwb_ref[...] = jnp.zeros(dwb_ref.shape, F32)
            dwc_ref[...] = jnp.zeros(dwc_ref.shape, F32)
            da_ref[...] = jnp.zeros(da_ref.shape, F32)
            dd_ref[...] = jnp.zeros(dd_ref.shape, F32)

        for k in range(NBLK):
            cols = slice(GB * P * k, GB * P * (k + 1))
            u = h_ref[:, cols]
            dyk = dy_ref[:, cols]
            ar = a_ref[k, :, 0:HALF]
            ai = a_ref[k, :, HALF:2 * HALF]
            sbuf[0:8, :] = jnp.broadcast_to(s0_ref[0, k:k + 1, :], (8, 2 * HALF))
            sbuf[8:tile + 8, :] = jnp.dot(u.astype(_MXU), wb_ref[k], preferred_element_type=F32)
            _fwd_scan_block(sbuf, 8, n8, ar, ai, s0_ref[0, k:k + 1, 0:HALF], s0_ref[0, k:k + 1, HALF:2 * HALF])
            dyb = dyk.astype(_MXU)
            gbuf[...] = lax.dot_general(dyb, wc_ref[k], (((1,), (1,)), ((), ())), preferred_element_type=F32)
            dwc_ref[k] += lax.dot_general(sbuf[8:tile + 8, :].astype(_MXU), dyb, (((0,), (0,)), ((), ())),
                                          preferred_element_type=F32)
            steps, carry_m = _scan_tables(ar, ai, True)
            row = lax.broadcasted_iota(jnp.int32, (8, HALF), 0)

            def body(jj, carry):
                cr, ci, dar, dai = carry
                j = n8 - 1 - jj
                r0 = pl.multiple_of(j * 8, 8)
                xr = gbuf[pl.ds(r0, 8), 0:HALF]
                xi = gbuf[pl.ds(r0, 8), HALF:2 * HALF]
                xr, xi = _tile_scan_rev(xr, xi, cr, ci, steps, carry_m)
                gbuf[pl.ds(r0, 8), 0:HALF] = xr
                gbuf[pl.ds(r0, 8), HALF:2 * HALF] = xi
                r1 = pl.multiple_of(j * 8 + 8, 8)
                spr = jnp.where(row == 0, sbuf[pl.ds(r0, 8), 0:HALF][7:8],
                                pltpu.roll(sbuf[pl.ds(r1, 8), 0:HALF], 1, 0))
                spi = jnp.where(row == 0, sbuf[pl.ds(r0, 8), HALF:2 * HALF][7:8],
                                pltpu.roll(sbuf[pl.ds(r1, 8), HALF:2 * HALF], 1, 0))
                dar = dar + xr * spr + xi * spi
                dai = dai + xi * spr - xr * spi
                return xr[0:1], xi[0:1], dar, dai

            z8 = jnp.zeros((8, HALF), F32)
            cr, ci, dar, dai = lax.fori_loop(
                0, n8, body, (lam_ref[k:k + 1, 0:HALF], lam_ref[k:k + 1, HALF:2 * HALF], z8, z8))
            lam_ref[k:k + 1, 0:HALF] = cr
            lam_ref[k:k + 1, HALF:2 * HALF] = ci
            da_ref[k:k + 1, 0:HALF] += jnp.sum(dar, axis=0, keepdims=True)
            da_ref[k:k + 1, HALF:2 * HALF] += jnp.sum(dai, axis=0, keepdims=True)
            lam = gbuf[...].astype(_MXU)
            dwb_ref[k] += lax.dot_general(u.astype(_MXU), lam, (((0,), (0,)), ((), ())), preferred_element_type=F32)
            du = lax.dot_general(lam, wb_ref[k], (((1,), (1,)), ((), ())), preferred_element_type=F32)
            dh_ref[:, cols] = du + d_ref[:, cols] * dyk
            dd_ref[:, cols] += jnp.sum(dyk * u, axis=0, keepdims=True)

    full = lambda a: pl.BlockSpec(a.shape, functools.partial(lambda i, nd_: (0,) * nd_, nd_=a.ndim))
    fullo = lambda shp: pl.BlockSpec(shp, functools.partial(lambda i, nd_: (0,) * nd_, nd_=len(shp)))
    rev = lambda i: (n_t - 1 - i, 0)
    return pl.pallas_call(
        kern,
        out_shape=[jax.ShapeDtypeStruct((s, D), F32), jax.ShapeDtypeStruct(wb.shape, F32),
                   jax.ShapeDtypeStruct(wc.shape, F32), jax.ShapeDtypeStruct((NBLK, 2 * HALF), F32),
                   jax.ShapeDtypeStruct((1, D), F32)],
        grid=(n_t,),
        in_specs=[pl.BlockSpec((tile, D), rev), pl.BlockSpec((tile, D), rev),
                  pl.BlockSpec((1, NBLK, 2 * HALF), lambda i: (n_t - 1 - i, 0, 0)),
                  full(wb), full(wc), full(a_tab), full(dskip)],
        out_specs=[pl.BlockSpec((tile, D), rev), fullo(wb.shape), fullo(wc.shape), fullo((NBLK, 2 * HALF)),
                   fullo((1, D))],
        scratch_shapes=[pltpu.VMEM((NBLK, 2 * HALF), F32), pltpu.VMEM((tile + 8, 2 * HALF), F32),
                        pltpu.VMEM((tile, 2 * HALF), F32)],
        name=name, compiler_params=_params(1))(h, dy, s0, wb, wc, a_tab, dskip)


def _chunk_mask(q0, k0, tq, tk):
    r = (q0 + lax.broadcasted_iota(jnp.int32, (tq, tk), 0)) // CHUNK
    c = (k0 + lax.broadcasted_iota(jnp.int32, (tq, tk), 1)) // CHUNK
    return r >= c


def _head_lanes(j):
    lane = _lane(2 * DV)
    return (lane >= DV * j) & (lane < DV * (j + 1))


def _raw_scores(q, kblk, masked, t):
    s = lax.dot_general(q, kblk, (((1,), (1,)), ((), ())), preferred_element_type=F32)
    return jnp.where(_chunk_mask(0, 0, t, t), s, -1e30) if masked else s


def attn_fwd(name, q, k, v, t=TILE_ATT_FWD, tk=TILE_ATT_KEYS):
    s = q.shape[0]
    n_q = s // t
    r = t // tk

    def kern(q_ref, k_ref, v_ref, o_ref, lse_ref):
        qi = pl.program_id(1)
        qs = [q_ref[:, HD * j:HD * (j + 1)] for j in range(2)]

        def absorb(k0, carry, mask):
            vblk = v_ref[pl.ds(k0, tk), :]
            scs = [lax.dot_general(qs[j], k_ref[pl.ds(k0, tk), HD * j:HD * (j + 1)], (((1,), (1,)), ((), ())),
                                   preferred_element_type=F32) for j in range(2)]
            if mask is not None:
                scs = [jnp.where(mask, sc, -1e30) for sc in scs]
            m_new = [jnp.maximum(carry[j][0], jnp.max(scs[j], axis=-1, keepdims=True)) for j in range(2)]
            ps = [jnp.exp2((scs[j] - m_new[j]) * EXP2_SCALE) for j in range(2)]
            alphas = [jnp.exp2((carry[j][0] - m_new[j]) * EXP2_SCALE) for j in range(2)]
            pvs = [jnp.dot(ps[j].astype(_MXU), vblk, preferred_element_type=F32) for j in range(2)]
            return tuple((m_new[j], alphas[j] * carry[j][1] + jnp.sum(ps[j], axis=-1, keepdims=True),
                          alphas[j] * carry[j][2] + pvs[j]) for j in range(2))

        init = tuple((jnp.full((t, 1), -1e30, F32), jnp.zeros((t, 1), F32), jnp.zeros((t, 2 * DV), F32))
                     for _ in range(2))
        carry = lax.fori_loop(0, qi * r, lambda kb, c: absorb(pl.multiple_of(kb * tk, tk), c, None), init)
        for i in range(r):
            carry = absorb(pl.multiple_of(qi * t + i * tk, tk), carry, _chunk_mask(0, i * tk, t, tk))
        outs = []
        for j in range(2):
            m, l, acc = carry[j]
            outs.append(acc / l)
            lse_ref[0, j] = m * ATTN_SCALE + jnp.log(l)
        o_ref[...] = jnp.where(_head_lanes(0), outs[0], outs[1])

    return pl.pallas_call(
        kern,
        out_shape=[jax.ShapeDtypeStruct((s, H * DV), F32), jax.ShapeDtypeStruct((HP, 2, s, 1), F32)],
        grid=(HP, n_q),
        in_specs=[pl.BlockSpec((t, 2 * HD), lambda hp, i: (i, hp)), pl.BlockSpec((s, 2 * HD), lambda hp, i: (0, hp)),
                  pl.BlockSpec((s, 2 * DV), lambda hp, i: (0, hp))],
        out_specs=[pl.BlockSpec((t, 2 * DV), lambda hp, i: (i, hp)),
                   pl.BlockSpec((1, 2, t, 1), lambda hp, i: (hp, 0, i, 0))],
        name=name, compiler_params=_params(2))(q, k, v)


def attn_bwd(name, q, k, v, o, do, lse, t=TILE_ATT):
    s = q.shape[0]
    n_q = s // t

    def kern(q_ref, k_ref, v_ref, o_ref, do_ref, lse_ref, dq_ref, dk_ref, dv_ref):
        qi = pl.program_id(1)

        @pl.when(qi == 0)
        def _():
            dk_ref[...] = jnp.zeros(dk_ref.shape, F32)
            dv_ref[...] = jnp.zeros(dv_ref.shape, F32)

        qs, doms, deltas, lse2 = [], [], [], []
        for j in range(2):
            qs.append(q_ref[:, HD * j:HD * (j + 1)])
            dom = jnp.where(_head_lanes(j), do_ref[...], 0.0)
            deltas.append(jnp.sum(dom * o_ref[...], axis=-1, keepdims=True))
            doms.append(dom.astype(_MXU))
            lse2.append(lse_ref[0, j] * LOG2E)

        def block(k0, dqs, masked):
            vblk = v_ref[pl.ds(k0, t), :]
            kblks = [k_ref[pl.ds(k0, t), HD * j:HD * (j + 1)] for j in range(2)]
            scs = [_raw_scores(qs[j], kblks[j], masked, t) for j in range(2)]
            dps = [lax.dot_general(doms[j], vblk, (((1,), (1,)), ((), ())), preferred_element_type=F32)
                   for j in range(2)]
            ps = [jnp.exp2(scs[j] * EXP2_SCALE - lse2[j]) for j in range(2)]
            dss = [(ps[j] * (dps[j] - deltas[j])).astype(_MXU) for j in range(2)]
            pbs = [ps[j].astype(_MXU) for j in range(2)]
            new = tuple(dqs[j] + jnp.dot(dss[j], kblks[j], preferred_element_type=F32) for j in range(2))
            for j in range(2):
                dk_ref[pl.ds(k0, t), HD * j:HD * (j + 1)] += lax.dot_general(
                    dss[j], qs[j], (((0,), (0,)), ((), ())), preferred_element_type=F32)
            dvs = [lax.dot_general(pbs[j], doms[j], (((0,), (0,)), ((), ())), preferred_element_type=F32)
                   for j in range(2)]
            dv_ref[pl.ds(k0, t), :] += dvs[0] + dvs[1]
            return new

        init = (jnp.zeros((t, HD), F32), jnp.zeros((t, HD), F32))
        dqs = lax.fori_loop(0, qi, lambda kb, c: block(pl.multiple_of(kb * t, t), c, False), init)
        dqs = block(pl.multiple_of(qi * t, t), dqs, True)
        for j in range(2):
            dq_ref[:, HD * j:HD * (j + 1)] = dqs[j] * ATTN_SCALE

        @pl.when(qi == n_q - 1)
        def _():
            dk_ref[...] = dk_ref[...] * ATTN_SCALE

    return pl.pallas_call(
        kern,
        out_shape=[jax.ShapeDtypeStruct((s, H * HD), F32), jax.ShapeDtypeStruct((s, H * HD), F32),
                   jax.ShapeDtypeStruct((s, H * DV), F32)],
        grid=(HP, n_q),
        in_specs=[pl.BlockSpec((t, 2 * HD), lambda hp, i: (i, hp)), pl.BlockSpec((s, 2 * HD), lambda hp, i: (0, hp)),
                  pl.BlockSpec((s, 2 * DV), lambda hp, i: (0, hp)), pl.BlockSpec((t, 2 * DV), lambda hp, i: (i, hp)),
                  pl.BlockSpec((t, 2 * DV), lambda hp, i: (i, hp)),
                  pl.BlockSpec((1, 2, t, 1), lambda hp, i: (hp, 0, i, 0))],
        out_specs=[pl.BlockSpec((t, 2 * HD), lambda hp, i: (i, hp)), pl.BlockSpec((s, 2 * HD), lambda hp, i: (0, hp)),
                   pl.BlockSpec((s, 2 * DV), lambda hp, i: (0, hp))],
        name=name, compiler_params=_params(2))(q, k, v, o, do, lse)


def rope_tables(name, pos_col, inv128):
    s = pos_col.shape[0]

    def kern(p_ref, inv_ref, c_ref, s_ref):
        ang = p_ref[...].astype(F32) * inv_ref[...]
        lane = _lane()
        m_r = (lane >= DN) & (lane < DN + DR)
        c_ref[...] = jnp.where(lane < DN, 1.0, jnp.where(m_r, jnp.cos(ang), 0.0))
        s_ref[...] = jnp.where(m_r, jnp.sin(ang), 0.0)

    return _whole(kern, name, [jax.ShapeDtypeStruct((s, HD), F32)] * 2, pos_col, inv128)


def loss_kernel(name, y, tgt, tile=TILE_ROW):
    def body(row_v, _):
        err = row_v[0] - row_v[1]
        part = 0.5 * jnp.sum(jnp.mean(err * err, axis=-1, keepdims=True), axis=0, keepdims=True)
        return [err * (1.0 / D)], [jnp.broadcast_to(part, (1, 128))]

    return _row_call(name, body, [y, tgt], [], [(D, F32)], [((1, 128), F32)], tile)


def _row_tile(r, c):
    cap = max(8, (1 << 18) // max(c, 1))
    for t in (2048, 1024, 512, 256, 128, 64, 32, 16, 8):
        if t <= cap and r % t == 0:
            return t
    return r


def sum_parts(name, parts):
    n, r, c = parts.shape
    t = _row_tile(r, c)

    def kern(p_ref, o_ref):
        acc = p_ref[0].astype(F32)
        for i in range(1, n):
            acc = acc + p_ref[i].astype(F32)
        o_ref[...] = acc

    return pl.pallas_call(kern, out_shape=jax.ShapeDtypeStruct((r, c), F32), grid=(r // t,),
                          in_specs=[pl.BlockSpec((n, t, c), lambda i: (0, i, 0))],
                          out_specs=pl.BlockSpec((t, c), lambda i: (i, 0)), name=name, compiler_params=_params(1))(parts)


def adamw(name, parts, w, m, v, base=0, stride=0):
    n, _, cp = parts.shape
    nl, r, c = w.shape
    t = _row_tile(math.gcd(math.gcd(r, base), stride), max(c, cp))
    c1 = 1.0 / (1.0 - ADAM_B1 ** ADAM_STEP)
    c2 = 1.0 / (1.0 - ADAM_B2 ** ADAM_STEP)

    def kern(p_ref, w_ref, m_ref, v_ref, g_ref, d_ref, nm_ref, nv_ref):
        g = p_ref[0].astype(F32)
        for i in range(1, n):
            g = g + p_ref[i].astype(F32)
        g = g[:, :c]
        nm = ADAM_B1 * m_ref[...] + (1.0 - ADAM_B1) * g
        nv = ADAM_B2 * v_ref[...] + (1.0 - ADAM_B2) * (g * g)
        g_ref[...] = g
        nm_ref[...] = nm
        nv_ref[...] = nv
        d_ref[...] = -ADAM_LR * ((nm * c1) / (jnp.sqrt(nv * c2) + ADAM_EPS) + ADAM_WD * w_ref[...])

    spec = pl.BlockSpec((None, t, c), lambda l, i: (l, i, 0))
    pspec = pl.BlockSpec((n, t, cp), lambda l, i: (0, (base + l * stride) // t + i, 0))
    return pl.pallas_call(kern, out_shape=[jax.ShapeDtypeStruct((nl, r, c), F32)] * 4, grid=(nl, r // t),
                          in_specs=[pspec, spec, spec, spec], out_specs=[spec] * 4, name=name,
                          compiler_params=_params(2))(parts, w, m, v)


def adamw_multi(name, parts_list, w, m, v):
    nl, r, c = w.shape
    n, _, cp = parts_list[0].shape
    t = _row_tile(r, max(c, cp))
    c1 = 1.0 / (1.0 - ADAM_B1 ** ADAM_STEP)
    c2 = 1.0 / (1.0 - ADAM_B2 ** ADAM_STEP)

    def kern(*refs):
        p_refs = refs[:nl]
        w_ref, m_ref, v_ref, g_ref, d_ref, nm_ref, nv_ref = refs[nl:]
        layer = pl.program_id(0)
        for ll in range(nl):
            @pl.when(layer == ll)
            def _(ll=ll):
                g = p_refs[ll][0].astype(F32)
                for i in range(1, n):
                    g = g + p_refs[ll][i].astype(F32)
                g = g[:, :c]
                nm = ADAM_B1 * m_ref[...] + (1.0 - ADAM_B1) * g
                nv = ADAM_B2 * v_ref[...] + (1.0 - ADAM_B2) * (g * g)
                g_ref[...] = g
                nm_ref[...] = nm
                nv_ref[...] = nv
                d_ref[...] = -ADAM_LR * ((nm * c1) / (jnp.sqrt(nv * c2) + ADAM_EPS) + ADAM_WD * w_ref[...])

    spec = pl.BlockSpec((None, t, c), lambda l, i: (l, i, 0))
    pspecs = [pl.BlockSpec((n, t, cp), functools.partial(lambda l, i, ll_: (0, jnp.where(l == ll_, i, 0), 0), ll_=ll))
              for ll in range(nl)]
    return pl.pallas_call(kern, out_shape=[jax.ShapeDtypeStruct((nl, r, c), F32)] * 4, grid=(nl, r // t),
                          in_specs=pspecs + [spec, spec, spec], out_specs=[spec] * 4, name=name,
                          compiler_params=_params(2))(*parts_list, w, m, v)


def adamw_layer(name, parts, w, m, v, layer, prev, base=0):
    n, _, cp = parts.shape
    nl, r, c = w.shape
    t = _row_tile(math.gcd(r, base), max(c, cp))
    c1 = 1.0 / (1.0 - ADAM_B1 ** ADAM_STEP)
    c2 = 1.0 / (1.0 - ADAM_B2 ** ADAM_STEP)
    chained = nl > 1

    def kern(p_ref, w_ref, m_ref, v_ref, *rest):
        g_ref, d_ref, nm_ref, nv_ref = rest[-4:]
        g = p_ref[0].astype(F32)
        for i in range(1, n):
            g = g + p_ref[i].astype(F32)
        g = g[:, :c]
        nm = ADAM_B1 * m_ref[...] + (1.0 - ADAM_B1) * g
        nv = ADAM_B2 * v_ref[...] + (1.0 - ADAM_B2) * (g * g)
        g_ref[...] = g
        nm_ref[...] = nm
        nv_ref[...] = nv
        d_ref[...] = -ADAM_LR * ((nm * c1) / (jnp.sqrt(nv * c2) + ADAM_EPS) + ADAM_WD * w_ref[...])

    spec = pl.BlockSpec((None, t, c), lambda i: (layer, i, 0))
    pspec = pl.BlockSpec((n, t, cp), lambda i: (0, base // t + i, 0))
    in_specs = [pspec, spec, spec, spec]
    args = [parts, w, m, v]
    aliases = {}
    if chained:
        if prev is None:
            prev = [lax.empty((nl, r, c), F32) for _ in range(4)]
        in_specs += [pl.BlockSpec(memory_space=pl.ANY)] * 4
        args += list(prev)
        aliases = {4 + i: i for i in range(4)}
    return pl.pallas_call(kern, out_shape=[jax.ShapeDtypeStruct((nl, r, c), F32)] * 4, grid=(r // t,),
                          in_specs=in_specs, out_specs=[spec] * 4, input_output_aliases=aliases, name=name,
                          compiler_params=_params(1))(*args)


def _me():
    return lax.axis_index("x"), lax.axis_index("y"), lax.axis_index("c")


def _flip(x, y, c, mask):
    return (jnp.where((mask >> 2) & 1, 1 - x, x), jnp.where((mask >> 1) & 1, 1 - y, y), jnp.where(mask & 1, 1 - c, c))


def _index(x, y, c):
    return 4 * x + 2 * y + c


def _exchange(name, arr, gather):
    out_shape = (N_DEV,) + arr.shape if gather else arr.shape

    def kern(in_ref, out_ref, send_sems, recv_sems, local_sem):
        x, y, c = _me()
        me = _index(x, y, c)
        mine = pltpu.make_async_copy(in_ref if gather else in_ref.at[me], out_ref.at[me], local_sem)
        mine.start()
        copies = []
        for mask in range(1, N_DEV):
            px, py, pc = _flip(x, y, c, mask)
            peer = _index(px, py, pc)
            cp = pltpu.make_async_remote_copy(
                src_ref=in_ref if gather else in_ref.at[peer], dst_ref=out_ref.at[me],
                send_sem=send_sems.at[mask - 1], recv_sem=recv_sems.at[mask - 1],
                device_id=(px, py, pc), device_id_type=MESH)
            cp.start()
            copies.append((cp, peer))
        for mask, (cp, peer) in enumerate(copies, start=1):
            pltpu.make_async_remote_copy(
                src_ref=in_ref if gather else in_ref.at[peer], dst_ref=out_ref.at[peer],
                send_sem=send_sems.at[mask - 1], recv_sem=recv_sems.at[mask - 1],
                device_id=_flip(x, y, c, mask), device_id_type=MESH).wait_recv()
        for cp, _ in copies:
            cp.wait_send()
        mine.wait()

    any_spec = pl.BlockSpec(memory_space=pl.ANY)
    return pl.pallas_call(
        kern, out_shape=jax.ShapeDtypeStruct(out_shape, arr.dtype), in_specs=[any_spec], out_specs=any_spec,
        scratch_shapes=[pltpu.SemaphoreType.DMA((N_DEV - 1,)), pltpu.SemaphoreType.DMA((N_DEV - 1,)),
                        pltpu.SemaphoreType.DMA],
        name=name, compiler_params=pltpu.CompilerParams(has_side_effects=True))(arr)


def all_gather(name, arr):
    return _exchange(name, arr, True)


def all_to_all(name, arr):
    return _exchange(name, arr, False)


_HBM = pl.BlockSpec(memory_space=pltpu.HBM)
_SEM = pl.BlockSpec(memory_space=pltpu.SEMAPHORE)
_EFFECT = pltpu.SideEffectType.DATAFLOW_SIDE_EFFECTING


def _split_copies(srcs, lands, send_sems, recv_sems, gather):
    x, y, c = _me()
    me = _index(x, y, c)
    out = []
    for a, (src, land) in enumerate(zip(srcs, lands)):
        for mask in range(1, N_DEV):
            px, py, pc = _flip(x, y, c, mask)
            peer = _index(px, py, pc)
            sem = (N_DEV - 1) * a + mask - 1
            mk = lambda dst_slot: pltpu.make_async_remote_copy(
                src_ref=src if gather else src.at[peer], dst_ref=land.at[dst_slot],
                send_sem=send_sems.at[sem], recv_sem=recv_sems.at[sem], device_id=(px, py, pc), device_id_type=MESH)
            out.append((mk(me), mk(peer)))
    return out


def exchange_start(name, arrs, gather, after):
    k = len(arrs)
    land_shapes = [((N_DEV,) + a.shape if gather else a.shape) for a in arrs]

    def body(*refs):
        srcs, lands = refs[:k], refs[k:2 * k]
        send_sems, recv_sems = refs[2 * k + 1], refs[2 * k + 2]
        token = refs[-1]
        for mine, _ in _split_copies(srcs, lands, send_sems, recv_sems, gather):
            mine.start()
        token[...] = jnp.zeros(token.shape, token.dtype)

    n_sem = (N_DEV - 1) * k
    res = pl.pallas_call(
        body, name=name,
        out_shape=(pltpu.SemaphoreType.DMA((n_sem,)), pltpu.SemaphoreType.DMA((n_sem,)),
                   *[pltpu.HBM(a.shape, a.dtype) for a in arrs],
                   *[pltpu.HBM(shp, a.dtype) for shp, a in zip(land_shapes, arrs)],
                   jax.ShapeDtypeStruct((8, 128), F32)),
        in_specs=[_HBM] * (2 * k) + [pl.BlockSpec(memory_space=pl.ANY)],
        out_specs=(_SEM, _SEM, *[_HBM] * (2 * k), pl.BlockSpec(memory_space=pltpu.VMEM)),
        input_output_aliases={i: 2 + i for i in range(2 * k)},
        compiler_params=pltpu.CompilerParams(has_side_effects=_EFFECT),
    )(*[pltpu.with_memory_space_constraint(a, pltpu.HBM) for a in arrs],
      *[pltpu.with_memory_space_constraint(lax.empty(shp, a.dtype), pltpu.HBM) for shp, a in zip(land_shapes, arrs)],
      after)
    return res[0], res[1], list(res[2:2 + k]), list(res[2 + k:2 + 2 * k]), res[-1]


def exchange_wait(name, started, after, gather):
    send_sems, recv_sems, thrus, lands, _ = started
    k = len(thrus)

    def body(*refs):
        srcs, lnds = refs[:k], refs[k:2 * k]
        s_sems, r_sems = refs[2 * k], refs[2 * k + 1]
        for mine, theirs in _split_copies(srcs, lnds, s_sems, r_sems, gather):
            mine.wait_send()
            theirs.wait_recv()

    res = pl.pallas_call(
        body, name=name,
        out_shape=tuple(pltpu.HBM(a.shape, a.dtype) for a in thrus + lands),
        in_specs=[_HBM] * (2 * k) + [_SEM, _SEM, pl.BlockSpec(memory_space=pl.ANY)], out_specs=tuple([_HBM] * (2 * k)),
        input_output_aliases={i: i for i in range(2 * k)},
        compiler_params=pltpu.CompilerParams(has_side_effects=_EFFECT),
    )(*thrus, *lands, send_sems, recv_sems, after)
    return list(res[k:])


def _pad_heads(w, real, padded):
    k = w.shape[0]
    w3 = w.reshape(k, H, real)
    return jnp.pad(w3, ((0, 0), (0, 0), (0, padded - real))).reshape(k, H * padded)


def _unpad_heads(w, real, padded):
    k = w.shape[0]
    return w.reshape(k, H, padded)[:, :, :real].reshape(k, H * real)


def _s5_place(ab_re, ab_im, bb_re_t, bb_im_t, c_re, c_im):
    eye = jnp.eye(GB, dtype=F32)

    def wb_part(bt):
        x4 = bt.reshape(P, NBLK, GB, N).transpose(1, 2, 0, 3)
        return jnp.einsum('kgpn,gh->kgphn', x4, eye).reshape(NBLK, GB * P, HALF)

    def wc_part(cc):
        x4 = cc.reshape(NBLK, GB, P, N)
        return jnp.einsum('kgpn,gh->kgnhp', x4, eye).reshape(NBLK, HALF, GB * P)

    wb = jnp.concatenate([wb_part(bb_re_t), wb_part(bb_im_t)], axis=-1)
    wc = jnp.concatenate([wc_part(c_re), -wc_part(c_im)], axis=1)
    a_tab = jnp.concatenate([ab_re.reshape(NBLK, 1, HALF), ab_im.reshape(NBLK, 1, HALF)], axis=-1)
    return wb.astype(_MXU), wc.astype(_MXU), a_tab


def _s5_unplace(dwb, dwc, da):
    eye = jnp.eye(GB, dtype=F32)

    def wb_part(dpart):
        x5 = dpart.reshape(NBLK, GB, P, GB, N)
        return jnp.einsum('kgphn,gh->kgpn', x5, eye).transpose(2, 0, 1, 3).reshape(P, G * N)

    def wc_part(dpart):
        x5 = dpart.reshape(NBLK, GB, N, GB, P)
        return jnp.einsum('kgnhp,gh->kgpn', x5, eye).reshape(G, P, N)

    dbb_re_t, dbb_im_t = wb_part(dwb[..., :HALF]), wb_part(dwb[..., HALF:])
    dc_re, dc_im = wc_part(dwc[:, :HALF]), -wc_part(dwc[:, HALF:])
    dab_re, dab_im = da[:, :HALF].reshape(1, G * N), da[:, HALF:].reshape(1, G * N)
    return dab_re, dab_im, dbb_re_t, dbb_im_t, dc_re, dc_im


def _row(v):
    return v.reshape(1, -1)


def kernel(x, c, positions, ada_w, ada_b, norm1_g, norm2_g, ffn_w_gate, ffn_w_up, ffn_w_down, s5_lam_re, s5_lam_im, s5_log_dt, s5_b_re, s5_b_im, s5_c_re, s5_c_im, s5_d, s5_w_glu, s5_b_glu, kv_ada_w, kv_ada_b, kv_norm_g, w_kv_a, kv_a_norm_g, w_kv_b, k_nope_norm_g, k_rope_norm_g, mla_w_dq, mla_q_norm_g, mla_w_uq, mla_q_nope_norm_g, mla_q_rope_norm_g, mla_w_o, loss_target, m_ada_w, m_ada_b, m_norm1_g, m_norm2_g, m_ffn_w_gate, m_ffn_w_up, m_ffn_w_down, m_s5_lam_re, m_s5_lam_im, m_s5_log_dt, m_s5_b_re, m_s5_b_im, m_s5_c_re, m_s5_c_im, m_s5_d, m_s5_w_glu, m_s5_b_glu, m_kv_ada_w, m_kv_ada_b, m_kv_norm_g, m_w_kv_a, m_kv_a_norm_g, m_w_kv_b, m_k_nope_norm_g, m_k_rope_norm_g, m_mla_w_dq, m_mla_q_norm_g, m_mla_w_uq, m_mla_q_nope_norm_g, m_mla_q_rope_norm_g, m_mla_w_o, v_ada_w, v_ada_b, v_norm1_g, v_norm2_g, v_ffn_w_gate, v_ffn_w_up, v_ffn_w_down, v_s5_lam_re, v_s5_lam_im, v_s5_log_dt, v_s5_b_re, v_s5_b_im, v_s5_c_re, v_s5_c_im, v_s5_d, v_s5_w_glu, v_s5_b_glu, v_kv_ada_w, v_kv_ada_b, v_kv_norm_g, v_w_kv_a, v_kv_a_norm_g, v_w_kv_b, v_k_nope_norm_g, v_k_rope_norm_g, v_mla_w_dq, v_mla_q_norm_g, v_mla_w_uq, v_mla_q_nope_norm_g, v_mla_q_rope_norm_g, v_mla_w_o):
    W = dict(ada_w=ada_w, ada_b=ada_b, norm1_g=norm1_g, norm2_g=norm2_g, ffn_w_gate=ffn_w_gate, ffn_w_up=ffn_w_up, ffn_w_down=ffn_w_down, s5_lam_re=s5_lam_re, s5_lam_im=s5_lam_im, s5_log_dt=s5_log_dt, s5_b_re=s5_b_re, s5_b_im=s5_b_im, s5_c_re=s5_c_re, s5_c_im=s5_c_im, s5_d=s5_d, s5_w_glu=s5_w_glu, s5_b_glu=s5_b_glu, kv_ada_w=kv_ada_w, kv_ada_b=kv_ada_b, kv_norm_g=kv_norm_g, w_kv_a=w_kv_a, kv_a_norm_g=kv_a_norm_g, w_kv_b=w_kv_b, k_nope_norm_g=k_nope_norm_g, k_rope_norm_g=k_rope_norm_g, mla_w_dq=mla_w_dq, mla_q_norm_g=mla_q_norm_g, mla_w_uq=mla_w_uq, mla_q_nope_norm_g=mla_q_nope_norm_g, mla_q_rope_norm_g=mla_q_rope_norm_g, mla_w_o=mla_w_o)
    M = dict(ada_w=m_ada_w, ada_b=m_ada_b, norm1_g=m_norm1_g, norm2_g=m_norm2_g, ffn_w_gate=m_ffn_w_gate, ffn_w_up=m_ffn_w_up, ffn_w_down=m_ffn_w_down, s5_lam_re=m_s5_lam_re, s5_lam_im=m_s5_lam_im, s5_log_dt=m_s5_log_dt, s5_b_re=m_s5_b_re, s5_b_im=m_s5_b_im, s5_c_re=m_s5_c_re, s5_c_im=m_s5_c_im, s5_d=m_s5_d, s5_w_glu=m_s5_w_glu, s5_b_glu=m_s5_b_glu, kv_ada_w=m_kv_ada_w, kv_ada_b=m_kv_ada_b, kv_norm_g=m_kv_norm_g, w_kv_a=m_w_kv_a, kv_a_norm_g=m_kv_a_norm_g, w_kv_b=m_w_kv_b, k_nope_norm_g=m_k_nope_norm_g, k_rope_norm_g=m_k_rope_norm_g, mla_w_dq=m_mla_w_dq, mla_q_norm_g=m_mla_q_norm_g, mla_w_uq=m_mla_w_uq, mla_q_nope_norm_g=m_mla_q_nope_norm_g, mla_q_rope_norm_g=m_mla_q_rope_norm_g, mla_w_o=m_mla_w_o)
    V = dict(ada_w=v_ada_w, ada_b=v_ada_b, norm1_g=v_norm1_g, norm2_g=v_norm2_g, ffn_w_gate=v_ffn_w_gate, ffn_w_up=v_ffn_w_up, ffn_w_down=v_ffn_w_down, s5_lam_re=v_s5_lam_re, s5_lam_im=v_s5_lam_im, s5_log_dt=v_s5_log_dt, s5_b_re=v_s5_b_re, s5_b_im=v_s5_b_im, s5_c_re=v_s5_c_re, s5_c_im=v_s5_c_im, s5_d=v_s5_d, s5_w_glu=v_s5_w_glu, s5_b_glu=v_s5_b_glu, kv_ada_w=v_kv_ada_w, kv_ada_b=v_kv_ada_b, kv_norm_g=v_kv_norm_g, w_kv_a=v_w_kv_a, kv_a_norm_g=v_kv_a_norm_g, w_kv_b=v_w_kv_b, k_nope_norm_g=v_k_nope_norm_g, k_rope_norm_g=v_k_rope_norm_g, mla_w_dq=v_mla_w_dq, mla_q_norm_g=v_mla_q_norm_g, mla_w_uq=v_mla_w_uq, mla_q_nope_norm_g=v_mla_q_nope_norm_g, mla_q_rope_norm_g=v_mla_q_rope_norm_g, mla_w_o=v_mla_w_o)
    return _step(x[0], c, positions, loss_target[0], W, M, V)


WEIGHT_NAMES = ['ada_w', 'ada_b', 'norm1_g', 'norm2_g', 'ffn_w_gate', 'ffn_w_up', 'ffn_w_down', 's5_lam_re', 's5_lam_im', 's5_log_dt', 's5_b_re', 's5_b_im', 's5_c_re', 's5_c_im', 's5_d', 's5_w_glu', 's5_b_glu', 'kv_ada_w', 'kv_ada_b', 'kv_norm_g', 'w_kv_a', 'kv_a_norm_g', 'w_kv_b', 'k_nope_norm_g', 'k_rope_norm_g', 'mla_w_dq', 'mla_q_norm_g', 'mla_w_uq', 'mla_q_nope_norm_g', 'mla_q_rope_norm_g', 'mla_w_o']
REPLICATED = ['ada_b', 'norm1_g', 'norm2_g', 's5_lam_re', 's5_lam_im', 's5_log_dt', 's5_b_re', 's5_b_im', 's5_c_re', 's5_c_im', 'kv_ada_b', 'kv_norm_g', 'kv_a_norm_g', 'k_nope_norm_g', 'k_rope_norm_g', 'mla_q_norm_g', 'mla_q_nope_norm_g', 'mla_q_rope_norm_g']
SHARDED_VEC = ['s5_d', 's5_b_glu']


def _step(x, c, positions, target, W, M, V):
    s = x.shape[0]
    me = _index(*_me())
    mxu = lambda a: a.astype(_MXU)

    pad_c = lambda a: jnp.pad(a, ((0, 0), (0, FFB - FF // N_DEV)))
    pad_r = lambda a: jnp.pad(a, ((0, FFB - FF // N_DEV), (0, 0)))
    cols = lambda g: g.transpose(1, 0, 2).reshape(g.shape[1], N_DEV * g.shape[2])
    rows = lambda g: g.reshape(N_DEV * g.shape[1], g.shape[2])

    def local_pack(l):
        second = W['s5_w_glu'][l] if l < N_A else W['mla_w_o'][l - N_A]
        arrs = [jnp.concatenate([mxu(pad_c(W['ffn_w_gate'][l])), mxu(pad_c(W['ffn_w_up'][l]))], axis=0),
                jnp.concatenate([mxu(pad_r(W['ffn_w_down'][l])), mxu(second)], axis=0)]
        if l == N_A:
            arrs += [jnp.concatenate([mxu(W['w_kv_b']), mxu(W['mla_w_dq'][0])], axis=0), mxu(W['w_kv_a'])]
        if l > N_A:
            arrs += [mxu(W['mla_w_dq'][l - N_A])]
        if l >= N_A:
            arrs += [mxu(W['mla_w_uq'][l - N_A])]
        return arrs


    def layer_weights(l, after):
        lands = exchange_wait(f"gather_wait_{l}", gathers[l], after, True)
        full = [lax.dynamic_update_slice(ld, src[None], (me,) + (0,) * src.ndim) for ld, src in zip(lands, gathers[l][2])]
        w = {'wg': cols(full[0][:, :D]), 'wu': cols(full[0][:, D:]), 'wd': rows(full[1][:, :FFB]),
             'second': rows(full[1][:, FFB:])}
        if l >= N_A:
            if l == N_A:
                wkvb3 = cols(full[2][:, :KVL]).reshape(KVL, H, DN + DV)
                wkva = rows(full[3])
                w['wa_pad'] = jnp.concatenate([wkva[:, :KVL], jnp.zeros((D, DN), _MXU), wkva[:, KVL:],
                                               jnp.zeros((D, HD - DN - DR), _MXU)], axis=1)
                w['wkn_pad'] = jnp.pad(wkvb3[:, :, :DN], ((0, 0), (0, 0), (0, HD - DN))).reshape(KVL, H * HD)
                w['wv'] = wkvb3[:, :, DN:].reshape(KVL, H * DV)
                w['wdq'] = rows(full[2][:, KVL:])
            else:
                w['wdq'] = rows(full[2])
            w['wuq_pad'] = _pad_heads(cols(full[-1]), DN + DR, HD)
        return w

    vec = jnp.concatenate([c.reshape(-1), W['s5_d'].reshape(-1), W['s5_b_glu'].reshape(-1)]).reshape(1, -1)
    vec = jnp.pad(vec, ((0, 7), (0, 0)))
    gv = all_gather("gather_vectors", vec)[:, 0, :]
    c_all = gv[:, :D]
    d_full = jnp.concatenate([gv[d, D:D + 2 * 128].reshape(N_A, 128) for d in range(N_DEV)], axis=1)
    bglu_full = jnp.concatenate([gv[d, D + 256:D + 512].reshape(N_A, 128) for d in range(N_DEV)], axis=1)

    ca_all = jax.nn.silu(c_all)
    w_mod = jnp.concatenate([W['ada_w'][l] for l in range(DEPTH)] + [W['kv_ada_w']], axis=1)
    n_mod = w_mod.shape[1]
    mod_cols = small_matmul("mod_matmul", ca_all, w_mod)
    gm = all_gather("gather_mod", mod_cols)
    gathers = [exchange_start(f"gather_start_{l}", local_pack(l), True, gm) for l in range(DEPTH)]
    tokens = sum(g[4][0, 0] for g in gathers)
    mine = lax.dynamic_index_in_dim(gm, me, axis=1, keepdims=False) + tokens
    per_l = D * 6 // N_DEV
    mods = []
    for l in range(DEPTH):
        full = jnp.concatenate([mine[d, per_l * l:per_l * (l + 1)] for d in range(N_DEV)]) + W['ada_b'][l]
        mods.append([_row(full[D * i:D * (i + 1)]) for i in range(6)])
    kfull = jnp.concatenate([mine[d, per_l * DEPTH:] for d in range(N_DEV)]) + W['kv_ada_b']
    k_shift, k_scale = _row(kfull[:D]), _row(kfull[D:])

    inv = 1.0 / (ROPE_THETA ** (np.arange(0, DR, 2, dtype=np.float32) / DR))
    inv128 = np.zeros((1, HD), np.float32)
    inv128[0, DN:DN + DR // 2] = inv
    inv128[0, DN + DR // 2:DN + DR] = inv
    cosf, sinf = rope_tables("rope_tables", positions.reshape(s, 1), jnp.asarray(inv128))
    zpad = lambda n: jnp.zeros((n,), F32)
    gkn128 = _row(jnp.concatenate([W['k_nope_norm_g'], zpad(HD - DN)]))
    gkr128 = _row(jnp.concatenate([zpad(DN), W['k_rope_norm_g'], zpad(HD - DN - DR)]))
    gq128 = [_row(jnp.concatenate([W['mla_q_nope_norm_g'][j], W['mla_q_rope_norm_g'][j], zpad(HD - DN - DR)]))
             for j in range(2)]

    expand = jnp.asarray(np.kron(np.eye(G, dtype=np.float32), np.ones((1, N), np.float32)))
    s5_raw, s5_mats = [], []
    for l in range(N_A):
        raw = (_row(W['s5_lam_re'][l]), _row(W['s5_lam_im'][l]), _row(W['s5_log_dt'][l]),
               W['s5_b_re'][l].transpose(2, 0, 1).reshape(P, G * N), W['s5_b_im'][l].transpose(2, 0, 1).reshape(P, G * N))
        ab_re, ab_im, bb_re_t, bb_im_t = s5_prep_fwd(f"s5_prep_fwd", *raw, expand)
        s5_raw.append(raw)
        s5_mats.append(_s5_place(ab_re, ab_im, bb_re_t, bb_im_t, W['s5_c_re'][l], W['s5_c_im'][l]))

    g1 = [_row(W['norm1_g'][l]) for l in range(DEPTH)]
    g2 = [_row(W['norm2_g'][l]) for l in range(DEPTH)]
    saved = []
    xs = x
    kv = None
    lw = [None] * DEPTH
    for l in range(DEPTH):
        sh1, sc1, gt1, sh2, sc2, gt2 = mods[l]
        rec = {'x_in': xs}
        if l >= N_A:
            lw[l] = layer_weights(l, xs)
        if l == N_A:
            kv_smalls = [_row(W['kv_norm_g']), k_shift, k_scale, _row(W['kv_a_norm_g']), gkn128, gkr128]
            kv_w = [lw[l]['wa_pad'], lw[l]['wkn_pad'], lw[l]['wv']]
            k_mat, v_mat = seg_forward("kv_fwd", seg_kv, [xs], kv_smalls, [cosf, sinf], kv_w,
                                       [(H * HD, _MXU), (H * DV, _MXU)], tap_widths=(KVL + HD, H * HD, H * DV))
            kv = {'x_in': xs, 'smalls': kv_smalls, 'k': k_mat, 'v': v_mat, 'w': kv_w}
        if l < N_A:
            (h,) = seg_forward("pre_fwd", seg_pre, [xs], [g1[l], sh1, sc1], [], [], [(D, F32)])
            wb, wc, a_tab = s5_mats[l]
            y, s0 = s5_scan_fwd("s5_scan_fwd", h, wb, wc, a_tab, _row(d_full[l]))
            lw[l] = layer_weights(l, y)
            (x_mid,) = seg_forward("glu_fwd", seg_glu, [xs, y], [gt1, _row(bglu_full[l])], [], [lw[l]['second']],
                                   [(D, F32)], tap_widths=(D,))
            rec.update(h=h, y=y, s0=s0)
        else:
            j = l - N_A
            q_smalls = [g1[l], sh1, sc1, _row(W['mla_q_norm_g'][j]), gq128[j]]
            (q_mat,) = seg_forward("q_fwd", seg_q, [xs], q_smalls, [cosf, sinf], [lw[l]['wdq'], lw[l]['wuq_pad']],
                                   [(H * HD, _MXU)], tap_widths=(QL, H * HD))
            o_mat, lse = attn_fwd("attn_fwd", q_mat, kv['k'], kv['v'])
            (x_mid,) = seg_forward("o_fwd", seg_o, [xs, o_mat], [gt1], [], [lw[l]['second']], [(D, F32)],
                                   tap_widths=(D,))
            rec.update(q=q_mat, o=o_mat, lse=lse, q_smalls=q_smalls)
        rec['x_mid'] = x_mid
        xs, rec['gate'], rec['up'] = ffn_forward("ffn_fwd", x_mid, g2[l], sh2, sc2, gt2,
                                                 lw[l]['wg'], lw[l]['wu'], lw[l]['wd'])
        saved.append(rec)

    dy, loss_part = loss_kernel("loss", xs, target)
    loss = lax.psum(loss_part[0, 0], ("x", "y", "c"))

    rblk = lambda a: a.reshape(N_DEV, a.shape[0] // N_DEV, a.shape[1])
    cblk = lambda a: a.reshape(a.shape[0], N_DEV, a.shape[1] // N_DEV).transpose(1, 0, 2)
    dmod = [None] * DEPTH
    dk_tot = []
    dv_tot = []
    dx = dy
    sends = [None] * DEPTH
    send_token = jnp.zeros((1, 1), F32)
    g_n1 = [None] * DEPTH
    g_n2 = [None] * DEPTH
    g_bglu = [None] * N_A
    g_dskip = [None] * N_A
    g_s5 = [None] * N_A
    g_qn, g_q128 = [None] * 2, [None] * 2
    for l in range(DEPTH - 1, -1, -1):
        rec = saved[l]
        sh1, sc1, gt1, sh2, sc2, gt2 = mods[l]
        dx, dgate, dup, dyd, h_b, a_b, dg2, dsh2, dsc2, dgt2 = ffn_backward(
            "ffn_bwd", rec['x_mid'], dx, rec['gate'], rec['up'], g2[l], sh2, sc2, gt2 + send_token,
            lw[l]['wg'], lw[l]['wu'], lw[l]['wd'])
        out_l = [matmul_tn("tn_ffn_in", h_b, dgate, _MXU, col_blocks=N_DEV),
                 matmul_tn("tn_ffn_in", h_b, dup, _MXU, col_blocks=N_DEV),
                 matmul_tn("tn_ffn_out", a_b, dyd, _MXU).reshape(N_DEV, FFB, D)]
        g_n2[l] = dg2
        if l == 0:
            sends_ffn0 = exchange_start("a2a_start_ffn0", out_l, False, dx)
            send_token = sends_ffn0[4][0:1, 0:1]
            out_l = []
        if l < N_A:
            (dx, dyy), (dz,), (g_b,), (dgt1, dbg) = seg_backward(
                "glu_bwd", seg_glu, [rec['x_in'], rec['y']], [gt1 + (send_token if l == 0 else 0.0), _row(bglu_full[l])], [],
                [lw[l]['second']],
                [dx], (D,), (D,))
            out_l.append(rblk(matmul_tn("tn_sq", g_b, dz, _MXU)))
            g_bglu[l] = dbg
            wb, wc, a_tab = s5_mats[l]
            dh, dwb, dwc, da, dd = s5_scan_bwd("s5_scan_bwd", rec['h'], dyy, rec['s0'], wb, wc, a_tab, _row(d_full[l]))
            g_dskip[l] = dd
            dab_re, dab_im, dbb_re_t, dbb_im_t, dc_re, dc_im = _s5_unplace(dwb, dwc, da)
            dlr, dli, dldt, dbr_t, dbi_t = s5_prep_bwd("s5_prep_bwd", *s5_raw[l], expand,
                                                       (dab_re, dab_im, dbb_re_t, dbb_im_t))
            g_s5[l] = (dlr.reshape(G, N), dli.reshape(G, N), dldt.reshape(G),
                       dbr_t.reshape(P, G, N).transpose(1, 2, 0), dbi_t.reshape(P, G, N).transpose(1, 2, 0), dc_re, dc_im)
            (dx,), _, _, (dg1, dsh1, dsc1) = seg_backward(
                "pre_bwd", seg_pre, [rec['x_in']], [g1[l], sh1, sc1], [], [], [dh], (), (), dx_add=dx)
        else:
            j = l - N_A
            (dx, do), (dzo,), (o_b,), (dgt1,) = seg_backward(
                "o_bwd", seg_o, [rec['x_in'], rec['o']], [gt1], [], [lw[l]['second']], [dx], (D,), (D,))
            out_l.append(rblk(matmul_tn("tn_sq", o_b, dzo, _MXU)))
            dq, dk, dv = attn_bwd("attn_bwd", rec['q'], kv['k'], kv['v'], rec['o'], do, rec['lse'])
            dk_tot.append(dk)
            dv_tot.append(dv)
            (dx,), (dql, dqq), (hq_b, qn_b), (dg1, dsh1, dsc1, dqg, dq128) = seg_backward(
                "q_bwd", seg_q, [rec['x_in']], rec['q_smalls'], [cosf, sinf], [lw[l]['wdq'], lw[l]['wuq_pad']],
                [dq], (QL, H * HD), (D, QL), dx_add=dx)
            g_dq = rblk(matmul_tn("tn_dq", hq_b, dql, _MXU))
            g_uq = cblk(_unpad_heads(matmul_tn("tn_uq", qn_b, dqq, _MXU), DN + DR, HD))
            g_qn[j], g_q128[j] = dqg, dq128
        g_n1[l] = dg1
        dmod[l] = jnp.concatenate([dsh1, dsc1, dgt1, dsh2, dsc2, dgt2], axis=1)
        if l == N_A:
            dkk = sum_parts("sum_dk", jnp.stack(dk_tot))
            dvv = sum_parts("sum_dv", jnp.stack(dv_tot))
            (dx,), (dta, dtk, dtv), (hk_b, ckv_b), (dkg, dksh, dksc, dag, dgkn, dgkr) = seg_backward(
                "kv_bwd", seg_kv, [kv['x_in']], kv['smalls'], [cosf, sinf], kv['w'],
                [dkk, dvv], (KVL + HD, H * HD, H * DV), (D, KVL), dx_add=dx)
            g_wa = matmul_tn("tn_kva", hk_b, dta, _MXU)
            g_wa = jnp.concatenate([g_wa[:, :KVL], g_wa[:, KVL + DN:KVL + DN + DR]], axis=1)
            g_kn = matmul_tn("tn_kn", ckv_b, dtk, _MXU).reshape(KVL, H, HD)[:, :, :DN]
            g_v = matmul_tn("tn_v", ckv_b, dtv, _MXU).reshape(KVL, H, DV)
            g_wkvb = jnp.concatenate([g_kn, g_v], axis=2).reshape(KVL, H * (DN + DV))
            dkmod = jnp.concatenate([dksh, dksc], axis=1)
            out_l += [jnp.concatenate([cblk(g_wkvb), g_dq], axis=1), rblk(g_wa)]
        if l > N_A:
            out_l.append(g_dq)
        if l >= N_A:
            out_l.append(g_uq)
        if l > 0:
            sends[l] = exchange_start(f"a2a_start_{l}", out_l, False, dx)
            send_token = sends[l][4][0:1, 0:1]
    grad_x = dx

    small = {
        'norm1_g': jnp.concatenate(g_n1, axis=0), 'norm2_g': jnp.concatenate(g_n2, axis=0),
        's5_lam_re': jnp.stack([g_s5[l][0] for l in range(N_A)]), 's5_lam_im': jnp.stack([g_s5[l][1] for l in range(N_A)]),
        's5_log_dt': jnp.stack([g_s5[l][2] for l in range(N_A)]),
        's5_b_re': jnp.stack([g_s5[l][3] for l in range(N_A)]), 's5_b_im': jnp.stack([g_s5[l][4] for l in range(N_A)]),
        's5_c_re': jnp.stack([g_s5[l][5] for l in range(N_A)]), 's5_c_im': jnp.stack([g_s5[l][6] for l in range(N_A)]),
        'kv_norm_g': dkg, 'kv_a_norm_g': dag, 'k_nope_norm_g': dgkn[:, :DN], 'k_rope_norm_g': dgkr[:, DN:DN + DR],
        'mla_q_norm_g': jnp.concatenate(g_qn, axis=0),
        'mla_q_nope_norm_g': jnp.concatenate([g[:, :DN] for g in g_q128], axis=0),
        'mla_q_rope_norm_g': jnp.concatenate([g[:, DN:DN + DR] for g in g_q128], axis=0),
        's5_d': jnp.concatenate(g_dskip, axis=0), 's5_b_glu': jnp.concatenate(g_bglu, axis=0),
    }
    small_names = [n for n in REPLICATED if n not in ('ada_b', 'kv_ada_b')] + SHARDED_VEC
    flat_small = jnp.concatenate([small[n].reshape(-1) for n in small_names])
    n_small = int(flat_small.shape[0])
    pad_small = -(-n_small // 65536) * 65536
    flat_small = jnp.pad(flat_small, (0, pad_small - n_small)).reshape(pad_small // 128, 128)

    dm = jnp.concatenate(dmod + [dkmod], axis=1)[0]
    per_dev = []
    for d in range(N_DEV):
        cols = [dm[6 * D * l + per_l * d:6 * D * l + per_l * (d + 1)] for l in range(DEPTH)]
        cols.append(dm[6 * D * DEPTH + (2 * D // N_DEV) * d:6 * D * DEPTH + (2 * D // N_DEV) * (d + 1)])
        per_dev.append(jnp.concatenate(cols))
    dm_dev = jnp.stack(per_dev)
    gdm = all_gather("gather_dmod", dm_dev)
    small_st = exchange_start("small_start", [flat_small], True, gdm)
    sends[0] = exchange_start("a2a_start_0", out_l, False, small_st[4])
    dm_mine = lax.dynamic_index_in_dim(gdm, me, axis=1, keepdims=False) + sends[0][4][0, 0]
    g_wmod = small_matmul_tn("dmod_matmul", ca_all, dm_mine)
    g_ada_w = jnp.stack([g_wmod[:, per_l * l:per_l * (l + 1)] for l in range(DEPTH)])
    g_kv_ada_w = g_wmod[:, per_l * DEPTH:]
    dm_sum = sum_parts("sum_dmod", gdm.reshape(N_DEV, N_DEV, n_mod))
    g_ada_b = jnp.stack([jnp.concatenate([dm_sum[d, per_l * l:per_l * (l + 1)] for d in range(N_DEV)])
                         for l in range(DEPTH)])
    g_kv_ada_b = jnp.concatenate([dm_sum[d, per_l * DEPTH:] for d in range(N_DEV)])

    grads, out_delta, out_m, out_v = {}, {}, {}, {}

    def update(name, parts, base=0, stride=0):
        shp = W[name].shape
        shp3 = shp if len(shp) == 3 else (1,) + shp
        res = adamw("adamw_" + name, parts, W[name].reshape(shp3), M[name].reshape(shp3), V[name].reshape(shp3),
                    base, stride)
        grads[name], out_delta[name], out_m[name], out_v[name] = (a.reshape(shp) for a in res)

    update('ada_w', g_ada_w.reshape(1, DEPTH * D, per_l), 0, D)
    update('kv_ada_w', g_kv_ada_w[None])

    chains = {}

    def update_layer(name, parts, layer, base=0):
        shp = W[name].shape
        shp3 = shp if len(shp) == 3 else (1,) + shp
        chains[name] = adamw_layer(f"adamw_{name}_{layer}", parts, W[name].reshape(shp3), M[name].reshape(shp3),
                                   V[name].reshape(shp3), layer, chains.get(name), base)
        grads[name], out_delta[name], out_m[name], out_v[name] = (a.reshape(shp) for a in chains[name])

    ffn_parts = [[None] * DEPTH for _ in range(3)]

    def landed(name, started, after):
        lands = exchange_wait(name, started, after, False)
        return [lax.dynamic_update_slice(ld, lax.dynamic_index_in_dim(src, me, 0, keepdims=True), (me,) + (0,) * (src.ndim - 1))
                for ld, src in zip(lands, started[2])]

    def receive(l, after):
        recv = landed(f"a2a_wait_{l}", sends[l], after)
        if l == 0:
            recv = landed("a2a_wait_ffn0", sends_ffn0, after) + recv
        for i in range(3):
            ffn_parts[i][l] = recv[i]
        if l < N_A:
            update_layer('s5_w_glu', recv[3], l)
        else:
            update_layer('mla_w_o', recv[3], l - N_A)
            if l == N_A:
                update_layer('w_kv_b', recv[4], 0)
                update_layer('mla_w_dq', recv[4], 0, KVL)
                update_layer('w_kv_a', recv[5], 0)
            else:
                update_layer('mla_w_dq', recv[4], l - N_A)
            update_layer('mla_w_uq', recv[-1], l - N_A)

    for l in range(DEPTH - 1, 0, -1):
        receive(l, out_delta['kv_ada_w'])

    (small_land,) = exchange_wait("small_wait", small_st, chains['s5_w_glu'][1], True)
    small_all = lax.dynamic_update_slice(small_land, flat_small[None], (me, 0, 0))
    g_small_sum = sum_parts("sum_small", small_all).reshape(-1)
    off = 0
    for n in small_names:
        size = int(np.prod(small[n].shape))
        full = g_small_sum[off:off + size]
        off += size
        if n in SHARDED_VEC:
            full = lax.dynamic_slice_in_dim(full.reshape(N_A, D), me * (D // N_DEV), D // N_DEV, axis=1)
        grads[n] = full.reshape(W[n].shape)
    grads['ada_b'] = g_ada_b
    grads['kv_ada_b'] = g_kv_ada_b

    big_small = ('s5_b_re', 's5_b_im', 's5_c_re', 's5_c_im')
    packed_names = [n for n in REPLICATED + SHARDED_VEC if n not in big_small]

    def pack(dct):
        flat_ = jnp.concatenate([dct[n].reshape(-1) for n in packed_names])
        n_ = int(flat_.shape[0])
        p_ = -(-n_ // 8192) * 8192
        return jnp.pad(flat_, (0, p_ - n_)).reshape(p_ // 128, 128)

    _, d_p, m_p, v_p = adamw("adamw_small", pack(grads)[None], pack(W)[None], pack(M)[None], pack(V)[None])
    off = 0
    d_p, m_p, v_p = d_p.reshape(-1), m_p.reshape(-1), v_p.reshape(-1)
    for n in packed_names:
        size = int(np.prod(W[n].shape))
        out_delta[n] = d_p[off:off + size].reshape(W[n].shape)
        out_m[n] = m_p[off:off + size].reshape(W[n].shape)
        out_v[n] = v_p[off:off + size].reshape(W[n].shape)
        off += size
    for n in big_small:
        shp = W[n].shape
        view = (1, int(np.prod(shp[:-1])), shp[-1])
        res = adamw("adamw_" + n, grads[n].reshape(view), W[n].reshape(view), M[n].reshape(view), V[n].reshape(view))
        _, out_delta[n], out_m[n], out_v[n] = (a.reshape(shp) for a in res)

    receive(0, d_p)
    for i, name in enumerate(('ffn_w_gate', 'ffn_w_up', 'ffn_w_down')):
        res = adamw_multi("adamw_" + name, ffn_parts[i], W[name], M[name], V[name])
        grads[name], out_delta[name], out_m[name], out_v[name] = res

    return (loss, grad_x[None], *[grads[n] for n in WEIGHT_NAMES], *[out_delta[n] for n in WEIGHT_NAMES],
            *[out_m[n] for n in WEIGHT_NAMES], *[out_v[n] for n in WEIGHT_NAMES])
```

```python
import functools
import math

import numpy as np
import jax
import jax.numpy as jnp
from jax import lax
from jax.experimental import pallas as pl
from jax.experimental.pallas import tpu as pltpu

F32 = jnp.float32
_MXU = jnp.bfloat16
HI = lax.Precision.HIGHEST

D = 1024
DEPTH = 4
N_A = 2
FF = 2816
FFB = 384
FFP = 8 * FFB
N_DEV = 8
G = 64
P = 16
N = 64
GB = 8
NBLK = G // GB
HALF = GB * N
H = 16
HP = H // 2
DN, DR, DV = 64, 32, 64
HD = 128
QL = 256
KVL = 256
CHUNK = 64
ROPE_THETA = 10000.0
ATTN_SCALE = 1.0 / math.sqrt(DN + DR)
LOG2E = 1.4426950408889634
EXP2_SCALE = ATTN_SCALE * LOG2E
EPS = 1e-6
ADAM_LR, ADAM_B1, ADAM_B2, ADAM_EPS, ADAM_WD, ADAM_STEP = 0.001, 0.9, 0.999, 1e-08, 0.01, 10
VMEM_LIMIT = 56 * 1024 * 1024
MESH = pl.DeviceIdType.MESH

TILE_ROW = 256
TILE_ATT = 512
TILE_ATT_FWD = 512
TILE_ATT_KEYS = 512
TILE_SCAN = 512


def _params(n_grid):
    return pltpu.CompilerParams(dimension_semantics=("arbitrary",) * n_grid, vmem_limit_bytes=VMEM_LIMIT)


@jax.custom_vjp
def mm(a, w):
    return jnp.dot(a.astype(_MXU), w, preferred_element_type=F32)


def _mm_fwd(a, w):
    return mm(a, w), w


def _mm_bwd(w, g):
    da = lax.dot_general(g.astype(_MXU), w, (((1,), (1,)), ((), ())), preferred_element_type=F32)
    return da, jnp.zeros_like(w)


mm.defvjp(_mm_fwd, _mm_bwd)


def rms(x, g):
    return x * lax.rsqrt(jnp.mean(x * x, axis=-1, keepdims=True) + EPS) * g


def modulate(h, shift, scale):
    return h * (1.0 + scale) + shift


def _lane(n=HD):
    return lax.broadcasted_iota(jnp.int32, (1, n), 1)


def _rot_matrix():
    r = lax.broadcasted_iota(jnp.int32, (HD, HD), 0)
    c = lax.broadcasted_iota(jnp.int32, (HD, HD), 1)
    first = (c >= DN) & (c < DN + DR // 2) & (r == c + DR // 2)
    second = (c >= DN + DR // 2) & (c < DN + DR) & (r == c - DR // 2)
    return jnp.where(first, -1.0, jnp.where(second, 1.0, 0.0)).astype(F32)


def head_norm_rope(xh, g128, cosf, sinf, rot, with_nope):
    lane = _lane()
    m_n = lane < DN
    m_r = (lane >= DN) & (lane < DN + DR)
    sq = xh * xh
    inv_r = lax.rsqrt(jnp.sum(jnp.where(m_r, sq, 0.0), axis=-1, keepdims=True) / DR + EPS)
    if with_nope:
        inv_n = lax.rsqrt(jnp.sum(jnp.where(m_n, sq, 0.0), axis=-1, keepdims=True) / DN + EPS)
        inv = jnp.where(m_n, inv_n, jnp.where(m_r, inv_r, 0.0))
    else:
        inv = jnp.where(m_r, inv_r, 0.0)
    xg = xh * inv * g128
    return xg * cosf + jnp.dot(xg, rot, precision=HI, preferred_element_type=F32) * sinf


def seg_pre(x, g, sh, sc):
    return (modulate(rms(x, g), sh, sc),), ()


def seg_ffn(x, g, sh, sc, gt, t_g, t_u, t_d, wg, wu, wd):
    h = modulate(rms(x, g), sh, sc)
    gate = mm(h, wg) + t_g
    up = mm(h, wu) + t_u
    a = jax.nn.silu(gate) * up
    y = mm(a, wd) + t_d
    return (x + gt * y,), (h.astype(_MXU), a.astype(_MXU))


def seg_glu(x, y, gt, b, t_z, w):
    g = jax.nn.gelu(y)
    z = mm(g, w) + b + t_z
    return (x + gt * (g * jax.nn.sigmoid(z)),), (g.astype(_MXU),)


def seg_o(x, o, gt, t_o, w):
    return (x + gt * (mm(o, w) + t_o),), (o.astype(_MXU),)


def seg_q(x, g, sh, sc, qg, g128, t_l, t_q, cosf, sinf, wdq, wuq):
    h = modulate(rms(x, g), sh, sc)
    ql = mm(h, wdq) + t_l
    qn = rms(ql, qg)
    q = mm(qn, wuq) + t_q
    rot = _rot_matrix()
    heads = [head_norm_rope(q[:, HD * i:HD * (i + 1)], g128, cosf, sinf, rot, True) for i in range(H)]
    return (jnp.concatenate(heads, axis=1),), (h.astype(_MXU), qn.astype(_MXU))


def seg_kv(x, g, sh, sc, ag, gkn, gkr, t_a, t_k, t_v, cosf, sinf, wa, wkn, wv):
    hk = modulate(rms(x, g), sh, sc)
    kva = mm(hk, wa) + t_a
    ckv = rms(kva[:, :KVL], ag)
    kr = head_norm_rope(kva[:, KVL:KVL + HD], gkr, cosf, sinf, _rot_matrix(), False)
    kn = mm(ckv, wkn) + t_k
    v = mm(ckv, wv) + t_v
    heads = []
    for i in range(H):
        kh = kn[:, HD * i:HD * (i + 1)]
        inv = lax.rsqrt(jnp.sum(kh * kh, axis=-1, keepdims=True) / DN + EPS)
        heads.append(kh * inv * gkn + kr)
    return (jnp.concatenate(heads, axis=1), v), (hk.astype(_MXU), ckv.astype(_MXU))


def _row_call(name, body_fn, rows, fulls, out_rows, out_accs, tile):
    s = rows[0].shape[0]
    n_tiles = s // tile
    n_rows, n_fulls, n_or, n_oa = len(rows), len(fulls), len(out_rows), len(out_accs)

    def kern(*refs):
        i = pl.program_id(0)
        row_v = [r[...] for r in refs[:n_rows]]
        full_v = [r[...] for r in refs[n_rows:n_rows + n_fulls]]
        o_refs = refs[n_rows + n_fulls:]
        ro, ao = body_fn(row_v, full_v)
        for r, v in zip(o_refs[:n_or], ro):
            r[...] = v.astype(r.dtype)
        if n_oa:
            @pl.when(i == 0)
            def _():
                for r in o_refs[n_or:]:
                    r[...] = jnp.zeros(r.shape, r.dtype)
            for r, v in zip(o_refs[n_or:], ao):
                r[...] += v.astype(r.dtype)

    in_specs = [pl.BlockSpec((tile, a.shape[1]), lambda i: (i, 0)) for a in rows]
    for a in fulls:
        big = a.size * a.dtype.itemsize > (1 << 20)
        nd = a.ndim
        in_specs.append(pl.BlockSpec(a.shape, functools.partial(lambda i, nd_: (0,) * nd_, nd_=nd),
                                     **({"pipeline_mode": pl.Buffered(1)} if big else {})))
    out_shape = [jax.ShapeDtypeStruct((s, w), dt) for w, dt in out_rows]
    out_shape += [jax.ShapeDtypeStruct(shp, dt) for shp, dt in out_accs]
    out_specs = [pl.BlockSpec((tile, w), lambda i: (i, 0)) for w, _ in out_rows]
    out_specs += [pl.BlockSpec(shp, functools.partial(lambda i, nd_: (0,) * nd_, nd_=len(shp))) for shp, _ in out_accs]
    res = pl.pallas_call(kern, out_shape=out_shape, grid=(n_tiles,), in_specs=in_specs, out_specs=out_specs,
                         name=name, compiler_params=_params(1))(*rows, *fulls)
    return list(res)


def seg_forward(name, seg, rows, smalls, consts_rows, consts_full, out_widths, tile=TILE_ROW, tap_widths=()):
    n_r, n_s, n_cr = len(rows), len(smalls), len(consts_rows)

    def body(row_v, full_v):
        t = row_v[0].shape[0]
        taps = [jnp.zeros((t, w), F32) for w in tap_widths]
        outs, _ = seg(*row_v[:n_r], *full_v[:n_s], *taps, *row_v[n_r:], *full_v[n_s:])
        return outs, ()

    return _row_call(name, body, list(rows) + list(consts_rows), list(smalls) + list(consts_full),
                     out_widths, [], tile)


def seg_backward(name, seg, rows, smalls, consts_rows, consts_full, cots, tap_widths, aux_widths,
                 dx_add=None, tile=TILE_ROW):
    cot_groups = [list(c) if isinstance(c, (list, tuple)) else [c] for c in cots]
    cot_flat = [a for grp in cot_groups for a in grp]
    n_r, n_s, n_cr, n_c = len(rows), len(smalls), len(consts_rows), len(cot_flat)
    has_add = dx_add is not None

    def body(row_v, full_v):
        t = row_v[0].shape[0]
        prim_rows = row_v[:n_r]
        c_rows = row_v[n_r:n_r + n_cr]
        cot_v = list(row_v[n_r + n_cr:n_r + n_cr + n_c])
        add_v = row_v[n_r + n_cr + n_c] if has_add else None
        small_v = full_v[:n_s]
        c_full = full_v[n_s:]
        taps = [jnp.zeros((t, w), F32) for w in tap_widths]
        cot_sum = []
        for grp in cot_groups:
            parts = [cot_v.pop(0).astype(F32) for _ in grp]
            cot_sum.append(functools.reduce(lambda x_, y_: x_ + y_, parts))

        def f(*args):
            return seg(*args, *c_rows, *c_full)

        _, vjp_fn, aux = jax.vjp(f, *prim_rows, *small_v, *taps, has_aux=True)
        grads = vjp_fn(tuple(cot_sum))
        d_rows = list(grads[:n_r])
        if has_add:
            d_rows[0] = d_rows[0] + add_v
        d_small = grads[n_r:n_r + n_s]
        d_taps = grads[n_r + n_s:]
        return d_rows + list(d_taps) + list(aux), [jnp.sum(g, axis=0, keepdims=True) if g.shape[0] != 1 else g
                                                   for g in d_small]

    all_rows = list(rows) + list(consts_rows) + cot_flat + ([dx_add] if has_add else [])
    out_rows = [(a.shape[1], F32) for a in rows] + [(w, _MXU) for w in tap_widths] + [(w, _MXU) for w in aux_widths]
    out_accs = [((1, a.shape[1]), F32) for a in smalls]
    res = _row_call(name, body, all_rows, list(smalls) + list(consts_full), out_rows, out_accs, tile)
    n_t, n_a = len(tap_widths), len(aux_widths)
    return res[:n_r], res[n_r:n_r + n_t], res[n_r + n_t:n_r + n_t + n_a], res[n_r + n_t + n_a:]


def _split(n):
    if n <= 1024:
        return n
    for t in (1408, 1024, 768, 512, 256, 128):
        if n % t == 0:
            return t
    raise ValueError(n)


def matmul_tn(name, a, b, out_dtype, col_blocks=None):
    s, k1 = a.shape
    _, k2 = b.shape
    tm, ts = _split(k1), 2048
    if col_blocks is None:
        tn, per_step, wblk = _split(k2), 1, None
    else:
        wblk = k2 // col_blocks
        per_step = max(1, min(col_blocks, 1536 // wblk))
        tn = per_step * wblk
    n_s = s // ts

    def kern(a_ref, b_ref, o_ref, acc_ref):
        k = pl.program_id(2)

        @pl.when(k == 0)
        def _():
            acc_ref[...] = jnp.zeros(acc_ref.shape, F32)

        acc_ref[...] += lax.dot_general(a_ref[...], b_ref[...], (((0,), (0,)), ((), ())),
                                        preferred_element_type=F32)

        @pl.when(k == n_s - 1)
        def _():
            if col_blocks is None:
                o_ref[...] = acc_ref[...].astype(o_ref.dtype)
            else:
                for cb in range(per_step):
                    o_ref[cb] = acc_ref[:, wblk * cb:wblk * (cb + 1)].astype(o_ref.dtype)

    if col_blocks is None:
        out_shape = jax.ShapeDtypeStruct((k1, k2), out_dtype)
        out_spec = pl.BlockSpec((tm, tn), lambda i, j, k: (i, j))
    else:
        out_shape = jax.ShapeDtypeStruct((col_blocks, k1, wblk), out_dtype)
        out_spec = pl.BlockSpec((per_step, tm, wblk), lambda i, j, k: (j, i, 0))
    return pl.pallas_call(
        kern, out_shape=out_shape, grid=(k1 // tm, k2 // tn, n_s),
        in_specs=[pl.BlockSpec((ts, tm), lambda i, j, k: (k, i)), pl.BlockSpec((ts, tn), lambda i, j, k: (k, j))],
        out_specs=out_spec,
        scratch_shapes=[pltpu.VMEM((tm, tn), F32)], name=name, compiler_params=_params(3))(a, b)


def ffn_forward(name, x, g, sh, sc, gt, wg, wu, wd, tile=TILE_ROW):
    s = x.shape[0]
    fp = wg.shape[1]
    blk = 2 * FFB
    n_blk = fp // blk

    def kern(x_ref, g_ref, sh_ref, sc_ref, gt_ref, wg_ref, wu_ref, wd_ref, o_ref, gate_ref, up_ref):
        xv = x_ref[...]
        hb = modulate(rms(xv, g_ref[...]), sh_ref[...], sc_ref[...]).astype(_MXU)
        y = jnp.zeros((tile, D), F32)
        for c in range(n_blk):
            cs = slice(blk * c, blk * (c + 1))
            gate = jnp.dot(hb, wg_ref[:, cs], preferred_element_type=F32)
            up = jnp.dot(hb, wu_ref[:, cs], preferred_element_type=F32)
            gate_ref[:, cs] = gate.astype(_MXU)
            up_ref[:, cs] = up.astype(_MXU)
            y = y + jnp.dot((jax.nn.silu(gate) * up).astype(_MXU), wd_ref[cs, :], preferred_element_type=F32)
        o_ref[...] = xv + gt_ref[...] * y

    row = lambda w: pl.BlockSpec((tile, w), lambda i: (i, 0))
    vec = pl.BlockSpec((1, D), lambda i: (0, 0))
    wspec = lambda a: pl.BlockSpec(a.shape, lambda i: (0, 0), pipeline_mode=pl.Buffered(1))
    return pl.pallas_call(
        kern, out_shape=[jax.ShapeDtypeStruct((s, D), F32), jax.ShapeDtypeStruct((s, fp), _MXU),
                         jax.ShapeDtypeStruct((s, fp), _MXU)],
        grid=(s // tile,), in_specs=[row(D), vec, vec, vec, vec, wspec(wg), wspec(wu), wspec(wd)],
        out_specs=[row(D), row(fp), row(fp)], name=name, compiler_params=_params(1))(x, g, sh, sc, gt, wg, wu, wd)


def ffn_backward(name, x, dxo, gate, up, g, sh, sc, gt, wg, wu, wd, tile=TILE_ROW):
    s = x.shape[0]
    fp = wg.shape[1]
    blk = 2 * FFB
    n_blk = fp // blk

    def kern(x_ref, dxo_ref, gate_ref, up_ref, g_ref, sh_ref, sc_ref, gt_ref, wg_ref, wu_ref, wd_ref,
             dx_ref, dg_ref, du_ref, dy_ref, h_ref, a_ref, dgn_ref, dsh_ref, dsc_ref, dgt_ref):
        i = pl.program_id(0)

        @pl.when(i == 0)
        def _():
            for r in (dgn_ref, dsh_ref, dsc_ref, dgt_ref):
                r[...] = jnp.zeros(r.shape, F32)

        dxo = dxo_ref[...]
        h, pre_vjp = jax.vjp(lambda *p: modulate(rms(p[0], p[1]), p[2], p[3]), x_ref[...], g_ref[...], sh_ref[...],
                             sc_ref[...])
        h_ref[...] = h.astype(_MXU)
        dyb = (gt_ref[...] * dxo).astype(_MXU)
        dy_ref[...] = dyb
        y = jnp.zeros((tile, D), F32)
        dh = jnp.zeros((tile, D), F32)
        tr = (((1,), (1,)), ((), ()))
        for c in range(n_blk):
            cs = slice(blk * c, blk * (c + 1))
            gate = gate_ref[:, cs].astype(F32)
            up = up_ref[:, cs].astype(F32)
            sig = jax.nn.sigmoid(gate)
            sl = gate * sig
            ab = (sl * up).astype(_MXU)
            a_ref[:, cs] = ab
            y = y + jnp.dot(ab, wd_ref[cs, :], preferred_element_type=F32)
            da = lax.dot_general(dyb, wd_ref[cs, :], tr, preferred_element_type=F32)
            dgb = (da * up * (sig * (1.0 + gate * (1.0 - sig)))).astype(_MXU)
            dub = (da * sl).astype(_MXU)
            dg_ref[:, cs] = dgb
            du_ref[:, cs] = dub
            dh = dh + lax.dot_general(dgb, wg_ref[:, cs], tr, preferred_element_type=F32) \
                + lax.dot_general(dub, wu_ref[:, cs], tr, preferred_element_type=F32)
        dgt_ref[...] += jnp.sum(dxo * y, axis=0, keepdims=True)
        dx_pre, dgn, dsh, dsc = pre_vjp(dh)
        dx_ref[...] = dxo + dx_pre
        dgn_ref[...] += dgn
        dsh_ref[...] += dsh
        dsc_ref[...] += dsc

    row = lambda w: pl.BlockSpec((tile, w), lambda i: (i, 0))
    vec = pl.BlockSpec((1, D), lambda i: (0, 0))
    wspec = lambda a: pl.BlockSpec(a.shape, lambda i: (0, 0), pipeline_mode=pl.Buffered(1))
    rows_out = [(D, F32), (fp, _MXU), (fp, _MXU), (D, _MXU), (D, _MXU), (fp, _MXU)]
    return pl.pallas_call(
        kern,
        out_shape=[jax.ShapeDtypeStruct((s, w), dt) for w, dt in rows_out] + [jax.ShapeDtypeStruct((1, D), F32)] * 4,
        grid=(s // tile,),
        in_specs=[row(D), row(D), row(fp), row(fp), vec, vec, vec, vec, wspec(wg), wspec(wu), wspec(wd)],
        out_specs=[row(w) for w, _ in rows_out] + [vec] * 4,
        name=name, compiler_params=_params(1))(x, dxo, gate, up, g, sh, sc, gt, wg, wu, wd)


def small_matmul(name, a, w, tn=256):
    m, k = a.shape
    n = w.shape[1]

    def kern(a_ref, w_ref, o_ref):
        o_ref[...] = jnp.dot(a_ref[...].astype(_MXU), w_ref[...].astype(_MXU), preferred_element_type=F32)

    return pl.pallas_call(kern, out_shape=jax.ShapeDtypeStruct((m, n), F32), grid=(n // tn,),
                          in_specs=[pl.BlockSpec((m, k), lambda j: (0, 0)), pl.BlockSpec((k, tn), lambda j: (0, j))],
                          out_specs=pl.BlockSpec((m, tn), lambda j: (0, j)), name=name,
                          compiler_params=_params(1))(a, w)


def small_matmul_tn(name, a, b, tn=256):
    m, k = a.shape
    n = b.shape[1]

    def kern(a_ref, b_ref, o_ref):
        o_ref[...] = lax.dot_general(a_ref[...].astype(_MXU), b_ref[...].astype(_MXU), (((0,), (0,)), ((), ())),
                                     preferred_element_type=F32)

    return pl.pallas_call(kern, out_shape=jax.ShapeDtypeStruct((k, n), F32), grid=(n // tn,),
                          in_specs=[pl.BlockSpec((m, k), lambda j: (0, 0)), pl.BlockSpec((m, tn), lambda j: (0, j))],
                          out_specs=pl.BlockSpec((k, tn), lambda j: (0, j)), name=name,
                          compiler_params=_params(1))(a, b)


def _s5_prep_math(lam_re, lam_im, log_dt, b_re_t, b_im_t, expand):
    dt = jnp.dot(jnp.exp(log_dt), expand, precision=HI, preferred_element_type=F32)
    mag = jnp.exp(lam_re * dt)
    ab_re = mag * jnp.cos(lam_im * dt)
    ab_im = mag * jnp.sin(lam_im * dt)
    den = lam_re * lam_re + lam_im * lam_im
    nr = ab_re - 1.0
    ni = ab_im
    f_re = (nr * lam_re + ni * lam_im) / den
    f_im = (ni * lam_re - nr * lam_im) / den
    bb_re = f_re * b_re_t - f_im * b_im_t
    bb_im = f_re * b_im_t + f_im * b_re_t
    return ab_re, ab_im, bb_re, bb_im


def _whole(kern, name, out_shape, *args):
    return pl.pallas_call(kern, out_shape=out_shape, name=name,
                          compiler_params=pltpu.CompilerParams(vmem_limit_bytes=VMEM_LIMIT))(*args)


def s5_prep_fwd(name, lam_re, lam_im, log_dt, b_re_t, b_im_t, expand):
    def kern(a, b, c, d, e, f, o0, o1, o2, o3):
        r = _s5_prep_math(a[...], b[...], c[...], d[...], e[...], f[...])
        for o, v in zip((o0, o1, o2, o3), r):
            o[...] = v

    gn = lam_re.shape[1]
    shp = [jax.ShapeDtypeStruct((1, gn), F32)] * 2 + [jax.ShapeDtypeStruct((P, gn), F32)] * 2
    return _whole(kern, name, shp, lam_re, lam_im, log_dt, b_re_t, b_im_t, expand)


def s5_prep_bwd(name, lam_re, lam_im, log_dt, b_re_t, b_im_t, expand, cots):
    def kern(a, b, c, d, e, f, c0, c1, c2, c3, o0, o1, o2, o3, o4):
        ex = f[...]
        _, vjp_fn = jax.vjp(lambda *p: _s5_prep_math(*p, ex), a[...], b[...], c[...], d[...], e[...])
        g = vjp_fn((c0[...], c1[...], c2[...], c3[...]))
        for o, v in zip((o0, o1, o2, o3, o4), g):
            o[...] = v

    shp = [jax.ShapeDtypeStruct(a.shape, F32) for a in (lam_re, lam_im, log_dt, b_re_t, b_im_t)]
    return _whole(kern, name, shp, lam_re, lam_im, log_dt, b_re_t, b_im_t, expand, *cots)


def _cpowers(ar, ai):
    pw = [(ar, ai)]
    for _ in range(7):
        pr, pi = pw[-1]
        pw.append((pr * ar - pi * ai, pr * ai + pi * ar))
    return pw


def _row_select(row, values):
    out = jnp.broadcast_to(values[7], (8, values[7].shape[1]))
    for r in range(6, -1, -1):
        out = jnp.where(row == r, values[r], out)
    return out


def _scan_tables(ar, ai, reverse):
    pw = _cpowers(ar, ai)
    row = lax.broadcasted_iota(jnp.int32, (8, ar.shape[1]), 0)
    steps = []
    for d in (1, 2, 4):
        keep = (row <= 7 - d) if reverse else (row >= d)
        steps.append((jnp.where(keep, pw[d - 1][0], 0.0), jnp.where(keep, pw[d - 1][1], 0.0)))
    order = list(range(7, -1, -1)) if reverse else list(range(8))
    carry = (_row_select(row, [pw[i][0] for i in order]), _row_select(row, [pw[i][1] for i in order]))
    return steps, carry


def _tile_scan_fwd(xr, xi, cr, ci, steps, carry_m):
    for d, (mr, mi) in zip((1, 2, 4), steps):
        sr = pltpu.roll(xr, d, 0)
        si = pltpu.roll(xi, d, 0)
        xr, xi = xr + mr * sr - mi * si, xi + mr * si + mi * sr
    pr, pi = carry_m
    return xr + pr * cr - pi * ci, xi + pr * ci + pi * cr


def _tile_scan_rev(xr, xi, cr, ci, steps, carry_m):
    for d, (mr, mi) in zip((1, 2, 4), steps):
        sr = pltpu.roll(xr, 8 - d, 0)
        si = pltpu.roll(xi, 8 - d, 0)
        xr, xi = xr + mr * sr + mi * si, xi + mr * si - mi * sr
    pr, pi = carry_m
    return xr + pr * cr + pi * ci, xi + pr * ci - pi * cr


def _fwd_scan_block(buf, row0, n_tiles8, ar, ai, c0r, c0i):
    steps, carry_m = _scan_tables(ar, ai, False)

    def body(j, carry):
        cr, ci = carry
        r0 = pl.multiple_of(row0 + j * 8, 8)
        xr = buf[pl.ds(r0, 8), 0:HALF]
        xi = buf[pl.ds(r0, 8), HALF:2 * HALF]
        xr, xi = _tile_scan_fwd(xr, xi, cr, ci, steps, carry_m)
        buf[pl.ds(r0, 8), 0:HALF] = xr
        buf[pl.ds(r0, 8), HALF:2 * HALF] = xi
        return xr[7:8], xi[7:8]

    return lax.fori_loop(0, n_tiles8, body, (c0r, c0i))


def s5_scan_fwd(name, h, wb, wc, a_tab, dskip, tile=TILE_SCAN):
    s = h.shape[0]
    n_t = s // tile

    def kern(h_ref, wb_ref, wc_ref, a_ref, d_ref, y_ref, s0_ref, carry_ref, buf):
        i = pl.program_id(0)

        @pl.when(i == 0)
        def _():
            carry_ref[...] = jnp.zeros(carry_ref.shape, F32)

        s0_ref[0] = carry_ref[...]
        for k in range(NBLK):
            cols = slice(GB * P * k, GB * P * (k + 1))
            u = h_ref[:, cols]
            buf[...] = jnp.dot(u.astype(_MXU), wb_ref[k], preferred_element_type=F32)
            ar = a_ref[k, :, 0:HALF]
            ai = a_ref[k, :, HALF:2 * HALF]
            cr, ci = _fwd_scan_block(buf, 0, tile // 8, ar, ai, carry_ref[k:k + 1, 0:HALF],
                                     carry_ref[k:k + 1, HALF:2 * HALF])
            carry_ref[k:k + 1, 0:HALF] = cr
            carry_ref[k:k + 1, HALF:2 * HALF] = ci
            y_ref[:, cols] = jnp.dot(buf[...].astype(_MXU), wc_ref[k], preferred_element_type=F32) + d_ref[:, cols] * u

    full = lambda a: pl.BlockSpec(a.shape, functools.partial(lambda i, nd_: (0,) * nd_, nd_=a.ndim))
    return pl.pallas_call(
        kern,
        out_shape=[jax.ShapeDtypeStruct((s, D), F32), jax.ShapeDtypeStruct((n_t, NBLK, 2 * HALF), F32)],
        grid=(n_t,),
        in_specs=[pl.BlockSpec((tile, D), lambda i: (i, 0)), full(wb), full(wc), full(a_tab), full(dskip)],
        out_specs=[pl.BlockSpec((tile, D), lambda i: (i, 0)), pl.BlockSpec((1, NBLK, 2 * HALF), lambda i: (i, 0, 0))],
        scratch_shapes=[pltpu.VMEM((NBLK, 2 * HALF), F32), pltpu.VMEM((tile, 2 * HALF), F32)],
        name=name, compiler_params=_params(1))(h, wb, wc, a_tab, dskip)


def s5_scan_bwd(name, h, dy, s0, wb, wc, a_tab, dskip, tile=TILE_SCAN):
    s = h.shape[0]
    n_t = s // tile
    n8 = tile // 8

    def kern(h_ref, dy_ref, s0_ref, wb_ref, wc_ref, a_ref, d_ref, dh_ref, dwb_ref, dwc_ref, da_ref, dd_ref,
             lam_ref, sbuf, gbuf):
        i = pl.program_id(0)

        @pl.when(i == 0)
        def _():
            lam_ref[...] = jnp.zeros(lam_ref.shape, F32)
            dwb_ref[...] = jnp.zeros(dwb_ref.shape, F32)
            dwc_ref[...] = jnp.zeros(dwc_ref.shape, F32)
            da_ref[...] = jnp.zeros(da_ref.shape, F32)
            dd_ref[...] = jnp.zeros(dd_ref.shape, F32)

        for k in range(NBLK):
            cols = slice(GB * P * k, GB * P * (k + 1))
            u = h_ref[:, cols]
            dyk = dy_ref[:, cols]
            ar = a_ref[k, :, 0:HALF]
            ai = a_ref[k, :, HALF:2 * HALF]
            sbuf[0:8, :] = jnp.broadcast_to(s0_ref[0, k:k + 1, :], (8, 2 * HALF))
            sbuf[8:tile + 8, :] = jnp.dot(u.astype(_MXU), wb_ref[k], preferred_element_type=F32)
            _fwd_scan_block(sbuf, 8, n8, ar, ai, s0_ref[0, k:k + 1, 0:HALF], s0_ref[0, k:k + 1, HALF:2 * HALF])
            dyb = dyk.astype(_MXU)
            gbuf[...] = lax.dot_general(dyb, wc_ref[k], (((1,), (1,)), ((), ())), preferred_element_type=F32)
            dwc_ref[k] += lax.dot_general(sbuf[8:tile + 8, :].astype(_MXU), dyb, (((0,), (0,)), ((), ())),
                                          preferred_element_type=F32)
            steps, carry_m = _scan_tables(ar, ai, True)
            row = lax.broadcasted_iota(jnp.int32, (8, HALF), 0)

            def body(jj, carry):
                cr, ci, dar, dai = carry
                j = n8 - 1 - jj
                r0 = pl.multiple_of(j * 8, 8)
                xr = gbuf[pl.ds(r0, 8), 0:HALF]
                xi = gbuf[pl.ds(r0, 8), HALF:2 * HALF]
                xr, xi = _tile_scan_rev(xr, xi, cr, ci, steps, carry_m)
                gbuf[pl.ds(r0, 8), 0:HALF] = xr
                gbuf[pl.ds(r0, 8), HALF:2 * HALF] = xi
                r1 = pl.multiple_of(j * 8 + 8, 8)
                spr = jnp.where(row == 0, sbuf[pl.ds(r0, 8), 0:HALF][7:8],
                                pltpu.roll(sbuf[pl.ds(r1, 8), 0:HALF], 1, 0))
                spi = jnp.where(row == 0, sbuf[pl.ds(r0, 8), HALF:2 * HALF][7:8],
                                pltpu.roll(sbuf[pl.ds(r1, 8), HALF:2 * HALF], 1, 0))
                dar = dar + xr * spr + xi * spi
                dai = dai + xi * spr - xr * spi
                return xr[0:1], xi[0:1], dar, dai

            z8 = jnp.zeros((8, HALF), F32)
            cr, ci, dar, dai = lax.fori_loop(
                0, n8, body, (lam_ref[k:k + 1, 0:HALF], lam_ref[k:k + 1, HALF:2 * HALF], z8, z8))
            lam_ref[k:k + 1, 0:HALF] = cr
            lam_ref[k:k + 1, HALF:2 * HALF] = ci
            da_ref[k:k + 1, 0:HALF] += jnp.sum(dar, axis=0, keepdims=True)
            da_ref[k:k + 1, HALF:2 * HALF] += jnp.sum(dai, axis=0, keepdims=True)
            lam = gbuf[...].astype(_MXU)
            dwb_ref[k] += lax.dot_general(u.astype(_MXU), lam, (((0,), (0,)), ((), ())), preferred_element_type=F32)
            du = lax.dot_general(lam, wb_ref[k], (((1,), (1,)), ((), ())), preferred_element_type=F32)
            dh_ref[:, cols] = du + d_ref[:, cols] * dyk
            dd_ref[:, cols] += jnp.sum(dyk * u, axis=0, keepdims=True)

    full = lambda a: pl.BlockSpec(a.shape, functools.partial(lambda i, nd_: (0,) * nd_, nd_=a.ndim))
    fullo = lambda shp: pl.BlockSpec(shp, functools.partial(lambda i, nd_: (0,) * nd_, nd_=len(shp)))
    rev = lambda i: (n_t - 1 - i, 0)
    return pl.pallas_call(
        kern,
        out_shape=[jax.ShapeDtypeStruct((s, D), F32), jax.ShapeDtypeStruct(wb.shape, F32),
                   jax.ShapeDtypeStruct(wc.shape, F32), jax.ShapeDtypeStruct((NBLK, 2 * HALF), F32),
                   jax.ShapeDtypeStruct((1, D), F32)],
        grid=(n_t,),
        in_specs=[pl.BlockSpec((tile, D), rev), pl.BlockSpec((tile, D), rev),
                  pl.BlockSpec((1, NBLK, 2 * HALF), lambda i: (n_t - 1 - i, 0, 0)),
                  full(wb), full(wc), full(a_tab), full(dskip)],
        out_specs=[pl.BlockSpec((tile, D), rev), fullo(wb.shape), fullo(wc.shape), fullo((NBLK, 2 * HALF)),
                   fullo((1, D))],
        scratch_shapes=[pltpu.VMEM((NBLK, 2 * HALF), F32), pltpu.VMEM((tile + 8, 2 * HALF), F32),
                        pltpu.VMEM((tile, 2 * HALF), F32)],
        name=name, compiler_params=_params(1))(h, dy, s0, wb, wc, a_tab, dskip)


def _chunk_mask(q0, k0, tq, tk):
    r = (q0 + lax.broadcasted_iota(jnp.int32, (tq, tk), 0)) // CHUNK
    c = (k0 + lax.broadcasted_iota(jnp.int32, (tq, tk), 1)) // CHUNK
    return r >= c


def _head_lanes(j):
    lane = _lane(2 * DV)
    return (lane >= DV * j) & (lane < DV * (j + 1))


def _raw_scores(q, kblk, masked, t):
    s = lax.dot_general(q, kblk, (((1,), (1,)), ((), ())), preferred_element_type=F32)
    return jnp.where(_chunk_mask(0, 0, t, t), s, -1e30) if masked else s


def attn_fwd(name, q, k, v, t=TILE_ATT_FWD, tk=TILE_ATT_KEYS):
    s = q.shape[0]
    n_q = s // t
    r = t // tk

    def kern(q_ref, k_ref, v_ref, o_ref, lse_ref):
        qi = pl.program_id(1)
        qs = [q_ref[:, HD * j:HD * (j + 1)] for j in range(2)]

        def absorb(k0, carry, mask):
            vblk = v_ref[pl.ds(k0, tk), :]
            scs = [lax.dot_general(qs[j], k_ref[pl.ds(k0, tk), HD * j:HD * (j + 1)], (((1,), (1,)), ((), ())),
                                   preferred_element_type=F32) for j in range(2)]
            if mask is not None:
                scs = [jnp.where(mask, sc, -1e30) for sc in scs]
            m_new = [jnp.maximum(carry[j][0], jnp.max(scs[j], axis=-1, keepdims=True)) for j in range(2)]
            ps = [jnp.exp2((scs[j] - m_new[j]) * EXP2_SCALE) for j in range(2)]
            alphas = [jnp.exp2((carry[j][0] - m_new[j]) * EXP2_SCALE) for j in range(2)]
            pvs = [jnp.dot(ps[j].astype(_MXU), vblk, preferred_element_type=F32) for j in range(2)]
            return tuple((m_new[j], alphas[j] * carry[j][1] + jnp.sum(ps[j], axis=-1, keepdims=True),
                          alphas[j] * carry[j][2] + pvs[j]) for j in range(2))

        init = tuple((jnp.full((t, 1), -1e30, F32), jnp.zeros((t, 1), F32), jnp.zeros((t, 2 * DV), F32))
                     for _ in range(2))
        carry = lax.fori_loop(0, qi * r, lambda kb, c: absorb(pl.multiple_of(kb * tk, tk), c, None), init)
        for i in range(r):
            carry = absorb(pl.multiple_of(qi * t + i * tk, tk), carry, _chunk_mask(0, i * tk, t, tk))
        outs = []
        for j in range(2):
            m, l, acc = carry[j]
            outs.append(acc / l)
            lse_ref[0, j] = m * ATTN_SCALE + jnp.log(l)
        o_ref[...] = jnp.where(_head_lanes(0), outs[0], outs[1])

    return pl.pallas_call(
        kern,
        out_shape=[jax.ShapeDtypeStruct((s, H * DV), F32), jax.ShapeDtypeStruct((HP, 2, s, 1), F32)],
        grid=(HP, n_q),
        in_specs=[pl.BlockSpec((t, 2 * HD), lambda hp, i: (i, hp)), pl.BlockSpec((s, 2 * HD), lambda hp, i: (0, hp)),
                  pl.BlockSpec((s, 2 * DV), lambda hp, i: (0, hp))],
        out_specs=[pl.BlockSpec((t, 2 * DV), lambda hp, i: (i, hp)),
                   pl.BlockSpec((1, 2, t, 1), lambda hp, i: (hp, 0, i, 0))],
        name=name, compiler_params=_params(2))(q, k, v)


def attn_bwd(name, q, k, v, o, do, lse, t=TILE_ATT):
    s = q.shape[0]
    n_q = s // t

    def kern(q_ref, k_ref, v_ref, o_ref, do_ref, lse_ref, dq_ref, dk_ref, dv_ref):
        qi = pl.program_id(1)

        @pl.when(qi == 0)
        def _():
            dk_ref[...] = jnp.zeros(dk_ref.shape, F32)
            dv_ref[...] = jnp.zeros(dv_ref.shape, F32)

        qs, doms, deltas, lse2 = [], [], [], []
        for j in range(2):
            qs.append(q_ref[:, HD * j:HD * (j + 1)])
            dom = jnp.where(_head_lanes(j), do_ref[...], 0.0)
            deltas.append(jnp.sum(dom * o_ref[...], axis=-1, keepdims=True))
            doms.append(dom.astype(_MXU))
            lse2.append(lse_ref[0, j] * LOG2E)

        def block(k0, dqs, masked):
            vblk = v_ref[pl.ds(k0, t), :]
            kblks = [k_ref[pl.ds(k0, t), HD * j:HD * (j + 1)] for j in range(2)]
            scs = [_raw_scores(qs[j], kblks[j], masked, t) for j in range(2)]
            dps = [lax.dot_general(doms[j], vblk, (((1,), (1,)), ((), ())), preferred_element_type=F32)
                   for j in range(2)]
            ps = [jnp.exp2(scs[j] * EXP2_SCALE - lse2[j]) for j in range(2)]
            dss = [(ps[j] * (dps[j] - deltas[j])).astype(_MXU) for j in range(2)]
            pbs = [ps[j].astype(_MXU) for j in range(2)]
            new = tuple(dqs[j] + jnp.dot(dss[j], kblks[j], preferred_element_type=F32) for j in range(2))
            for j in range(2):
                dk_ref[pl.ds(k0, t), HD * j:HD * (j + 1)] += lax.dot_general(
                    dss[j], qs[j], (((0,), (0,)), ((), ())), preferred_element_type=F32)
            dvs = [lax.dot_general(pbs[j], doms[j], (((0,), (0,)), ((), ())), preferred_element_type=F32)
                   for j in range(2)]
            dv_ref[pl.ds(k0, t), :] += dvs[0] + dvs[1]
            return new

        init = (jnp.zeros((t, HD), F32), jnp.zeros((t, HD), F32))
        dqs = lax.fori_loop(0, qi, lambda kb, c: block(pl.multiple_of(kb * t, t), c, False), init)
        dqs = block(pl.multiple_of(qi * t, t), dqs, True)
        for j in range(2):
            dq_ref[:, HD * j:HD * (j + 1)] = dqs[j] * ATTN_SCALE

        @pl.when(qi == n_q - 1)
        def _():
            dk_ref[...] = dk_ref[...] * ATTN_SCALE

    return pl.pallas_call(
        kern,
        out_shape=[jax.ShapeDtypeStruct((s, H * HD), F32), jax.ShapeDtypeStruct((s, H * HD), F32),
                   jax.ShapeDtypeStruct((s, H * DV), F32)],
        grid=(HP, n_q),
        in_specs=[pl.BlockSpec((t, 2 * HD), lambda hp, i: (i, hp)), pl.BlockSpec((s, 2 * HD), lambda hp, i: (0, hp)),
                  pl.BlockSpec((s, 2 * DV), lambda hp, i: (0, hp)), pl.BlockSpec((t, 2 * DV), lambda hp, i: (i, hp)),
                  pl.BlockSpec((t, 2 * DV), lambda hp, i: (i, hp)),
                  pl.BlockSpec((1, 2, t, 1), lambda hp, i: (hp, 0, i, 0))],
        out_specs=[pl.BlockSpec((t, 2 * HD), lambda hp, i: (i, hp)), pl.BlockSpec((s, 2 * HD), lambda hp, i: (0, hp)),
                   pl.BlockSpec((s, 2 * DV), lambda hp, i: (0, hp))],
        name=name, compiler_params=_params(2))(q, k, v, o, do, lse)


def rope_tables(name, pos_col, inv128):
    s = pos_col.shape[0]

    def kern(p_ref, inv_ref, c_ref, s_ref):
        ang = p_ref[...].astype(F32) * inv_ref[...]
        lane = _lane()
        m_r = (lane >= DN) & (lane < DN + DR)
        c_ref[...] = jnp.where(lane < DN, 1.0, jnp.where(m_r, jnp.cos(ang), 0.0))
        s_ref[...] = jnp.where(m_r, jnp.sin(ang), 0.0)

    return _whole(kern, name, [jax.ShapeDtypeStruct((s, HD), F32)] * 2, pos_col, inv128)


def loss_kernel(name, y, tgt, tile=TILE_ROW):
    def body(row_v, _):
        err = row_v[0] - row_v[1]
        part = 0.5 * jnp.sum(jnp.mean(err * err, axis=-1, keepdims=True), axis=0, keepdims=True)
        return [err * (1.0 / D)], [jnp.broadcast_to(part, (1, 128))]

    return _row_call(name, body, [y, tgt], [], [(D, F32)], [((1, 128), F32)], tile)


def _row_tile(r, c):
    cap = max(8, (1 << 18) // max(c, 1))
    for t in (2048, 1024, 512, 256, 128, 64, 32, 16, 8):
        if t <= cap and r % t == 0:
            return t
    return r


def sum_parts(name, parts):
    n, r, c = parts.shape
    t = _row_tile(r, c)

    def kern(p_ref, o_ref):
        acc = p_ref[0].astype(F32)
        for i in range(1, n):
            acc = acc + p_ref[i].astype(F32)
        o_ref[...] = acc

    return pl.pallas_call(kern, out_shape=jax.ShapeDtypeStruct((r, c), F32), grid=(r // t,),
                          in_specs=[pl.BlockSpec((n, t, c), lambda i: (0, i, 0))],
                          out_specs=pl.BlockSpec((t, c), lambda i: (i, 0)), name=name, compiler_params=_params(1))(parts)


def adamw(name, parts, w, m, v, base=0, stride=0):
    n, _, cp = parts.shape
    nl, r, c = w.shape
    t = _row_tile(math.gcd(math.gcd(r, base), stride), max(c, cp))
    c1 = 1.0 / (1.0 - ADAM_B1 ** ADAM_STEP)
    c2 = 1.0 / (1.0 - ADAM_B2 ** ADAM_STEP)

    def kern(p_ref, w_ref, m_ref, v_ref, g_ref, d_ref, nm_ref, nv_ref):
        g = p_ref[0].astype(F32)
        for i in range(1, n):
            g = g + p_ref[i].astype(F32)
        g = g[:, :c]
        nm = ADAM_B1 * m_ref[...] + (1.0 - ADAM_B1) * g
        nv = ADAM_B2 * v_ref[...] + (1.0 - ADAM_B2) * (g * g)
        g_ref[...] = g
        nm_ref[...] = nm
        nv_ref[...] = nv
        d_ref[...] = -ADAM_LR * ((nm * c1) / (jnp.sqrt(nv * c2) + ADAM_EPS) + ADAM_WD * w_ref[...])

    spec = pl.BlockSpec((None, t, c), lambda l, i: (l, i, 0))
    pspec = pl.BlockSpec((n, t, cp), lambda l, i: (0, (base + l * stride) // t + i, 0))
    return pl.pallas_call(kern, out_shape=[jax.ShapeDtypeStruct((nl, r, c), F32)] * 4, grid=(nl, r // t),
                          in_specs=[pspec, spec, spec, spec], out_specs=[spec] * 4, name=name,
                          compiler_params=_params(2))(parts, w, m, v)


def adamw_multi(name, parts_list, w, m, v):
    nl, r, c = w.shape
    n, _, cp = parts_list[0].shape
    t = _row_tile(r, max(c, cp))
    c1 = 1.0 / (1.0 - ADAM_B1 ** ADAM_STEP)
    c2 = 1.0 / (1.0 - ADAM_B2 ** ADAM_STEP)

    def kern(*refs):
        p_refs = refs[:nl]
        w_ref, m_ref, v_ref, g_ref, d_ref, nm_ref, nv_ref = refs[nl:]
        layer = pl.program_id(0)
        for ll in range(nl):
            @pl.when(layer == ll)
            def _(ll=ll):
                g = p_refs[ll][0].astype(F32)
                for i in range(1, n):
                    g = g + p_refs[ll][i].astype(F32)
                g = g[:, :c]
                nm = ADAM_B1 * m_ref[...] + (1.0 - ADAM_B1) * g
                nv = ADAM_B2 * v_ref[...] + (1.0 - ADAM_B2) * (g * g)
                g_ref[...] = g
                nm_ref[...] = nm
                nv_ref[...] = nv
                d_ref[...] = -ADAM_LR * ((nm * c1) / (jnp.sqrt(nv * c2) + ADAM_EPS) + ADAM_WD * w_ref[...])

    spec = pl.BlockSpec((None, t, c), lambda l, i: (l, i, 0))
    pspecs = [pl.BlockSpec((n, t, cp), functools.partial(lambda l, i, ll_: (0, jnp.where(l == ll_, i, 0), 0), ll_=ll))
              for ll in range(nl)]
    return pl.pallas_call(kern, out_shape=[jax.ShapeDtypeStruct((nl, r, c), F32)] * 4, grid=(nl, r // t),
                          in_specs=pspecs + [spec, spec, spec], out_specs=[spec] * 4, name=name,
                          compiler_params=_params(2))(*parts_list, w, m, v)


def adamw_layer(name, parts, w, m, v, layer, prev, base=0):
    n, _, cp = parts.shape
    nl, r, c = w.shape
    t = _row_tile(math.gcd(r, base), max(c, cp))
    c1 = 1.0 / (1.0 - ADAM_B1 ** ADAM_STEP)
    c2 = 1.0 / (1.0 - ADAM_B2 ** ADAM_STEP)
    chained = nl > 1

    def kern(p_ref, w_ref, m_ref, v_ref, *rest):
        g_ref, d_ref, nm_ref, nv_ref = rest[-4:]
        g = p_ref[0].astype(F32)
        for i in range(1, n):
            g = g + p_ref[i].astype(F32)
        g = g[:, :c]
        nm = ADAM_B1 * m_ref[...] + (1.0 - ADAM_B1) * g
        nv = ADAM_B2 * v_ref[...] + (1.0 - ADAM_B2) * (g * g)
        g_ref[...] = g
        nm_ref[...] = nm
        nv_ref[...] = nv
        d_ref[...] = -ADAM_LR * ((nm * c1) / (jnp.sqrt(nv * c2) + ADAM_EPS) + ADAM_WD * w_ref[...])

    spec = pl.BlockSpec((None, t, c), lambda i: (layer, i, 0))
    pspec = pl.BlockSpec((n, t, cp), lambda i: (0, base // t + i, 0))
    in_specs = [pspec, spec, spec, spec]
    args = [parts, w, m, v]
    aliases = {}
    if chained:
        if prev is None:
            prev = [lax.empty((nl, r, c), F32) for _ in range(4)]
        in_specs += [pl.BlockSpec(memory_space=pl.ANY)] * 4
        args += list(prev)
        aliases = {4 + i: i for i in range(4)}
    return pl.pallas_call(kern, out_shape=[jax.ShapeDtypeStruct((nl, r, c), F32)] * 4, grid=(r // t,),
                          in_specs=in_specs, out_specs=[spec] * 4, input_output_aliases=aliases, name=name,
                          compiler_params=_params(1))(*args)


def _me():
    return lax.axis_index("x"), lax.axis_index("y"), lax.axis_index("c")


def _flip(x, y, c, mask):
    return (jnp.where((mask >> 2) & 1, 1 - x, x), jnp.where((mask >> 1) & 1, 1 - y, y), jnp.where(mask & 1, 1 - c, c))


def _index(x, y, c):
    return 4 * x + 2 * y + c


def _exchange(name, arr, gather):
    out_shape = (N_DEV,) + arr.shape if gather else arr.shape

    def kern(in_ref, out_ref, send_sems, recv_sems, local_sem):
        x, y, c = _me()
        me = _index(x, y, c)
        mine = pltpu.make_async_copy(in_ref if gather else in_ref.at[me], out_ref.at[me], local_sem)
        mine.start()
        copies = []
        for mask in range(1, N_DEV):
            px, py, pc = _flip(x, y, c, mask)
            peer = _index(px, py, pc)
            cp = pltpu.make_async_remote_copy(
                src_ref=in_ref if gather else in_ref.at[peer], dst_ref=out_ref.at[me],
                send_sem=send_sems.at[mask - 1], recv_sem=recv_sems.at[mask - 1],
                device_id=(px, py, pc), device_id_type=MESH)
            cp.start()
            copies.append((cp, peer))
        for mask, (cp, peer) in enumerate(copies, start=1):
            pltpu.make_async_remote_copy(
                src_ref=in_ref if gather else in_ref.at[peer], dst_ref=out_ref.at[peer],
                send_sem=send_sems.at[mask - 1], recv_sem=recv_sems.at[mask - 1],
                device_id=_flip(x, y, c, mask), device_id_type=MESH).wait_recv()
        for cp, _ in copies:
            cp.wait_send()
        mine.wait()

    any_spec = pl.BlockSpec(memory_space=pl.ANY)
    return pl.pallas_call(
        kern, out_shape=jax.ShapeDtypeStruct(out_shape, arr.dtype), in_specs=[any_spec], out_specs=any_spec,
        scratch_shapes=[pltpu.SemaphoreType.DMA((N_DEV - 1,)), pltpu.SemaphoreType.DMA((N_DEV - 1,)),
                        pltpu.SemaphoreType.DMA],
        name=name, compiler_params=pltpu.CompilerParams(has_side_effects=True))(arr)


def all_gather(name, arr):
    return _exchange(name, arr, True)


def all_to_all(name, arr):
    return _exchange(name, arr, False)


_HBM = pl.BlockSpec(memory_space=pltpu.HBM)
_SEM = pl.BlockSpec(memory_space=pltpu.SEMAPHORE)
_EFFECT = pltpu.SideEffectType.DATAFLOW_SIDE_EFFECTING


def _split_copies(srcs, lands, send_sems, recv_sems, gather):
    x, y, c = _me()
    me = _index(x, y, c)
    out = []
    for a, (src, land) in enumerate(zip(srcs, lands)):
        for mask in range(1, N_DEV):
            px, py, pc = _flip(x, y, c, mask)
            peer = _index(px, py, pc)
            sem = (N_DEV - 1) * a + mask - 1
            mk = lambda dst_slot: pltpu.make_async_remote_copy(
                src_ref=src if gather else src.at[peer], dst_ref=land.at[dst_slot],
                send_sem=send_sems.at[sem], recv_sem=recv_sems.at[sem], device_id=(px, py, pc), device_id_type=MESH)
            out.append((mk(me), mk(peer)))
    return out


def exchange_start(name, arrs, gather, after):
    k = len(arrs)
    land_shapes = [((N_DEV,) + a.shape if gather else a.shape) for a in arrs]

    def body(*refs):
        srcs, lands = refs[:k], refs[k:2 * k]
        send_sems, recv_sems = refs[2 * k + 1], refs[2 * k + 2]
        token = refs[-1]
        for mine, _ in _split_copies(srcs, lands, send_sems, recv_sems, gather):
            mine.start()
        token[...] = jnp.zeros(token.shape, token.dtype)

    n_sem = (N_DEV - 1) * k
    res = pl.pallas_call(
        body, name=name,
        out_shape=(pltpu.SemaphoreType.DMA((n_sem,)), pltpu.SemaphoreType.DMA((n_sem,)),
                   *[pltpu.HBM(a.shape, a.dtype) for a in arrs],
                   *[pltpu.HBM(shp, a.dtype) for shp, a in zip(land_shapes, arrs)],
                   jax.ShapeDtypeStruct((8, 128), F32)),
        in_specs=[_HBM] * (2 * k) + [pl.BlockSpec(memory_space=pl.ANY)],
        out_specs=(_SEM, _SEM, *[_HBM] * (2 * k), pl.BlockSpec(memory_space=pltpu.VMEM)),
        input_output_aliases={i: 2 + i for i in range(2 * k)},
        compiler_params=pltpu.CompilerParams(has_side_effects=_EFFECT),
    )(*[pltpu.with_memory_space_constraint(a, pltpu.HBM) for a in arrs],
      *[pltpu.with_memory_space_constraint(lax.empty(shp, a.dtype), pltpu.HBM) for shp, a in zip(land_shapes, arrs)],
      after)
    return res[0], res[1], list(res[2:2 + k]), list(res[2 + k:2 + 2 * k]), res[-1]


def exchange_wait(name, started, after, gather):
    send_sems, recv_sems, thrus, lands, _ = started
    k = len(thrus)

    def body(*refs):
        srcs, lnds = refs[:k], refs[k:2 * k]
        s_sems, r_sems = refs[2 * k], refs[2 * k + 1]
        for mine, theirs in _split_copies(srcs, lnds, s_sems, r_sems, gather):
            mine.wait_send()
            theirs.wait_recv()

    res = pl.pallas_call(
        body, name=name,
        out_shape=tuple(pltpu.HBM(a.shape, a.dtype) for a in thrus + lands),
        in_specs=[_HBM] * (2 * k) + [_SEM, _SEM, pl.BlockSpec(memory_space=pl.ANY)], out_specs=tuple([_HBM] * (2 * k)),
        input_output_aliases={i: i for i in range(2 * k)},
        compiler_params=pltpu.CompilerParams(has_side_effects=_EFFECT),
    )(*thrus, *lands, send_sems, recv_sems, after)
    return list(res[k:])


def _pad_heads(w, real, padded):
    k = w.shape[0]
    w3 = w.reshape(k, H, real)
    return jnp.pad(w3, ((0, 0), (0, 0), (0, padded - real))).reshape(k, H * padded)


def _unpad_heads(w, real, padded):
    k = w.shape[0]
    return w.reshape(k, H, padded)[:, :, :real].reshape(k, H * real)


def _s5_place(ab_re, ab_im, bb_re_t, bb_im_t, c_re, c_im):
    eye = jnp.eye(GB, dtype=F32)

    def wb_part(bt):
        x4 = bt.reshape(P, NBLK, GB, N).transpose(1, 2, 0, 3)
        return jnp.einsum('kgpn,gh->kgphn', x4, eye).reshape(NBLK, GB * P, HALF)

    def wc_part(cc):
        x4 = cc.reshape(NBLK, GB, P, N)
        return jnp.einsum('kgpn,gh->kgnhp', x4, eye).reshape(NBLK, HALF, GB * P)

    wb = jnp.concatenate([wb_part(bb_re_t), wb_part(bb_im_t)], axis=-1)
    wc = jnp.concatenate([wc_part(c_re), -wc_part(c_im)], axis=1)
    a_tab = jnp.concatenate([ab_re.reshape(NBLK, 1, HALF), ab_im.reshape(NBLK, 1, HALF)], axis=-1)
    return wb.astype(_MXU), wc.astype(_MXU), a_tab


def _s5_unplace(dwb, dwc, da):
    eye = jnp.eye(GB, dtype=F32)

    def wb_part(dpart):
        x5 = dpart.reshape(NBLK, GB, P, GB, N)
        return jnp.einsum('kgphn,gh->kgpn', x5, eye).transpose(2, 0, 1, 3).reshape(P, G * N)

    def wc_part(dpart):
        x5 = dpart.reshape(NBLK, GB, N, GB, P)
        return jnp.einsum('kgnhp,gh->kgpn', x5, eye).reshape(G, P, N)

    dbb_re_t, dbb_im_t = wb_part(dwb[..., :HALF]), wb_part(dwb[..., HALF:])
    dc_re, dc_im = wc_part(dwc[:, :HALF]), -wc_part(dwc[:, HALF:])
    dab_re, dab_im = da[:, :HALF].reshape(1, G * N), da[:, HALF:].reshape(1, G * N)
    return dab_re, dab_im, dbb_re_t, dbb_im_t, dc_re, dc_im


def _row(v):
    return v.reshape(1, -1)


def _pack_rows(pieces):
    flat = jnp.concatenate(pieces)
    n = int(flat.shape[0])
    padded = -(-n // 65536) * 65536
    return jnp.pad(flat, (0, padded - n)).reshape(padded // 128, 128)


def kernel(x, c, positions, ada_w, ada_b, norm1_g, norm2_g, ffn_w_gate, ffn_w_up, ffn_w_down, s5_lam_re, s5_lam_im, s5_log_dt, s5_b_re, s5_b_im, s5_c_re, s5_c_im, s5_d, s5_w_glu, s5_b_glu, kv_ada_w, kv_ada_b, kv_norm_g, w_kv_a, kv_a_norm_g, w_kv_b, k_nope_norm_g, k_rope_norm_g, mla_w_dq, mla_q_norm_g, mla_w_uq, mla_q_nope_norm_g, mla_q_rope_norm_g, mla_w_o, loss_target, m_ada_w, m_ada_b, m_norm1_g, m_norm2_g, m_ffn_w_gate, m_ffn_w_up, m_ffn_w_down, m_s5_lam_re, m_s5_lam_im, m_s5_log_dt, m_s5_b_re, m_s5_b_im, m_s5_c_re, m_s5_c_im, m_s5_d, m_s5_w_glu, m_s5_b_glu, m_kv_ada_w, m_kv_ada_b, m_kv_norm_g, m_w_kv_a, m_kv_a_norm_g, m_w_kv_b, m_k_nope_norm_g, m_k_rope_norm_g, m_mla_w_dq, m_mla_q_norm_g, m_mla_w_uq, m_mla_q_nope_norm_g, m_mla_q_rope_norm_g, m_mla_w_o, v_ada_w, v_ada_b, v_norm1_g, v_norm2_g, v_ffn_w_gate, v_ffn_w_up, v_ffn_w_down, v_s5_lam_re, v_s5_lam_im, v_s5_log_dt, v_s5_b_re, v_s5_b_im, v_s5_c_re, v_s5_c_im, v_s5_d, v_s5_w_glu, v_s5_b_glu, v_kv_ada_w, v_kv_ada_b, v_kv_norm_g, v_w_kv_a, v_kv_a_norm_g, v_w_kv_b, v_k_nope_norm_g, v_k_rope_norm_g, v_mla_w_dq, v_mla_q_norm_g, v_mla_w_uq, v_mla_q_nope_norm_g, v_mla_q_rope_norm_g, v_mla_w_o):
    W = dict(ada_w=ada_w, ada_b=ada_b, norm1_g=norm1_g, norm2_g=norm2_g, ffn_w_gate=ffn_w_gate, ffn_w_up=ffn_w_up, ffn_w_down=ffn_w_down, s5_lam_re=s5_lam_re, s5_lam_im=s5_lam_im, s5_log_dt=s5_log_dt, s5_b_re=s5_b_re, s5_b_im=s5_b_im, s5_c_re=s5_c_re, s5_c_im=s5_c_im, s5_d=s5_d, s5_w_glu=s5_w_glu, s5_b_glu=s5_b_glu, kv_ada_w=kv_ada_w, kv_ada_b=kv_ada_b, kv_norm_g=kv_norm_g, w_kv_a=w_kv_a, kv_a_norm_g=kv_a_norm_g, w_kv_b=w_kv_b, k_nope_norm_g=k_nope_norm_g, k_rope_norm_g=k_rope_norm_g, mla_w_dq=mla_w_dq, mla_q_norm_g=mla_q_norm_g, mla_w_uq=mla_w_uq, mla_q_nope_norm_g=mla_q_nope_norm_g, mla_q_rope_norm_g=mla_q_rope_norm_g, mla_w_o=mla_w_o)
    M = dict(ada_w=m_ada_w, ada_b=m_ada_b, norm1_g=m_norm1_g, norm2_g=m_norm2_g, ffn_w_gate=m_ffn_w_gate, ffn_w_up=m_ffn_w_up, ffn_w_down=m_ffn_w_down, s5_lam_re=m_s5_lam_re, s5_lam_im=m_s5_lam_im, s5_log_dt=m_s5_log_dt, s5_b_re=m_s5_b_re, s5_b_im=m_s5_b_im, s5_c_re=m_s5_c_re, s5_c_im=m_s5_c_im, s5_d=m_s5_d, s5_w_glu=m_s5_w_glu, s5_b_glu=m_s5_b_glu, kv_ada_w=m_kv_ada_w, kv_ada_b=m_kv_ada_b, kv_norm_g=m_kv_norm_g, w_kv_a=m_w_kv_a, kv_a_norm_g=m_kv_a_norm_g, w_kv_b=m_w_kv_b, k_nope_norm_g=m_k_nope_norm_g, k_rope_norm_g=m_k_rope_norm_g, mla_w_dq=m_mla_w_dq, mla_q_norm_g=m_mla_q_norm_g, mla_w_uq=m_mla_w_uq, mla_q_nope_norm_g=m_mla_q_nope_norm_g, mla_q_rope_norm_g=m_mla_q_rope_norm_g, mla_w_o=m_mla_w_o)
    V = dict(ada_w=v_ada_w, ada_b=v_ada_b, norm1_g=v_norm1_g, norm2_g=v_norm2_g, ffn_w_gate=v_ffn_w_gate, ffn_w_up=v_ffn_w_up, ffn_w_down=v_ffn_w_down, s5_lam_re=v_s5_lam_re, s5_lam_im=v_s5_lam_im, s5_log_dt=v_s5_log_dt, s5_b_re=v_s5_b_re, s5_b_im=v_s5_b_im, s5_c_re=v_s5_c_re, s5_c_im=v_s5_c_im, s5_d=v_s5_d, s5_w_glu=v_s5_w_glu, s5_b_glu=v_s5_b_glu, kv_ada_w=v_kv_ada_w, kv_ada_b=v_kv_ada_b, kv_norm_g=v_kv_norm_g, w_kv_a=v_w_kv_a, kv_a_norm_g=v_kv_a_norm_g, w_kv_b=v_w_kv_b, k_nope_norm_g=v_k_nope_norm_g, k_rope_norm_g=v_k_rope_norm_g, mla_w_dq=v_mla_w_dq, mla_q_norm_g=v_mla_q_norm_g, mla_w_uq=v_mla_w_uq, mla_q_nope_norm_g=v_mla_q_nope_norm_g, mla_q_rope_norm_g=v_mla_q_rope_norm_g, mla_w_o=v_mla_w_o)
    return _step(x[0], c, positions, loss_target[0], W, M, V)


WEIGHT_NAMES = ['ada_w', 'ada_b', 'norm1_g', 'norm2_g', 'ffn_w_gate', 'ffn_w_up', 'ffn_w_down', 's5_lam_re', 's5_lam_im', 's5_log_dt', 's5_b_re', 's5_b_im', 's5_c_re', 's5_c_im', 's5_d', 's5_w_glu', 's5_b_glu', 'kv_ada_w', 'kv_ada_b', 'kv_norm_g', 'w_kv_a', 'kv_a_norm_g', 'w_kv_b', 'k_nope_norm_g', 'k_rope_norm_g', 'mla_w_dq', 'mla_q_norm_g', 'mla_w_uq', 'mla_q_nope_norm_g', 'mla_q_rope_norm_g', 'mla_w_o']
REPLICATED = ['ada_b', 'norm1_g', 'norm2_g', 's5_lam_re', 's5_lam_im', 's5_log_dt', 's5_b_re', 's5_b_im', 's5_c_re', 's5_c_im', 'kv_ada_b', 'kv_norm_g', 'kv_a_norm_g', 'k_nope_norm_g', 'k_rope_norm_g', 'mla_q_norm_g', 'mla_q_nope_norm_g', 'mla_q_rope_norm_g']
SHARDED_VEC = ['s5_d', 's5_b_glu']


def _step(x, c, positions, target, W, M, V):
    s = x.shape[0]
    me = _index(*_me())
    mxu = lambda a: a.astype(_MXU)

    pad_c = lambda a: jnp.pad(a, ((0, 0), (0, FFB - FF // N_DEV)))
    pad_r = lambda a: jnp.pad(a, ((0, FFB - FF // N_DEV), (0, 0)))
    cols = lambda g: g.transpose(1, 0, 2).reshape(g.shape[1], N_DEV * g.shape[2])
    rows = lambda g: g.reshape(N_DEV * g.shape[1], g.shape[2])

    def local_pack(l):
        second = W['s5_w_glu'][l] if l < N_A else W['mla_w_o'][l - N_A]
        arrs = [jnp.concatenate([mxu(pad_c(W['ffn_w_gate'][l])), mxu(pad_c(W['ffn_w_up'][l]))], axis=0),
                jnp.concatenate([mxu(pad_r(W['ffn_w_down'][l])), mxu(second)], axis=0)]
        if l == N_A:
            arrs += [jnp.concatenate([mxu(W['w_kv_b']), mxu(W['mla_w_dq'][0])], axis=0), mxu(W['w_kv_a'])]
        if l > N_A:
            arrs += [mxu(W['mla_w_dq'][l - N_A])]
        if l >= N_A:
            arrs += [mxu(W['mla_w_uq'][l - N_A])]
        return arrs


    def layer_weights(l, after):
        lands = exchange_wait(f"gather_wait_{l}", gathers[l], after, True)
        full = [lax.dynamic_update_slice(ld, src[None], (me,) + (0,) * src.ndim) for ld, src in zip(lands, gathers[l][2])]
        w = {'wg': cols(full[0][:, :D]), 'wu': cols(full[0][:, D:]), 'wd': rows(full[1][:, :FFB]),
             'second': rows(full[1][:, FFB:])}
        if l >= N_A:
            if l == N_A:
                wkvb3 = cols(full[2][:, :KVL]).reshape(KVL, H, DN + DV)
                wkva = rows(full[3])
                w['wa_pad'] = jnp.concatenate([wkva[:, :KVL], jnp.zeros((D, DN), _MXU), wkva[:, KVL:],
                                               jnp.zeros((D, HD - DN - DR), _MXU)], axis=1)
                w['wkn_pad'] = jnp.pad(wkvb3[:, :, :DN], ((0, 0), (0, 0), (0, HD - DN))).reshape(KVL, H * HD)
                w['wv'] = wkvb3[:, :, DN:].reshape(KVL, H * DV)
                w['wdq'] = rows(full[2][:, KVL:])
            else:
                w['wdq'] = rows(full[2])
            w['wuq_pad'] = _pad_heads(cols(full[-1]), DN + DR, HD)
        return w

    vec = jnp.concatenate([c.reshape(-1), W['s5_d'].reshape(-1), W['s5_b_glu'].reshape(-1)]).reshape(1, -1)
    vec = jnp.pad(vec, ((0, 7), (0, 0)))
    gv = all_gather("gather_vectors", vec)[:, 0, :]
    c_all = gv[:, :D]
    d_full = jnp.concatenate([gv[d, D:D + 2 * 128].reshape(N_A, 128) for d in range(N_DEV)], axis=1)
    bglu_full = jnp.concatenate([gv[d, D + 256:D + 512].reshape(N_A, 128) for d in range(N_DEV)], axis=1)

    ca_all = jax.nn.silu(c_all)
    w_mod = jnp.concatenate([W['ada_w'][l] for l in range(DEPTH)] + [W['kv_ada_w']], axis=1)
    n_mod = w_mod.shape[1]
    mod_cols = small_matmul("mod_matmul", ca_all, w_mod)
    gm = all_gather("gather_mod", mod_cols)
    gathers = [exchange_start(f"gather_start_{l}", local_pack(l), True, gm) for l in range(DEPTH)]
    tokens = sum(g[4][0, 0] for g in gathers)
    mine = lax.dynamic_index_in_dim(gm, me, axis=1, keepdims=False) + tokens
    per_l = D * 6 // N_DEV
    mods = []
    for l in range(DEPTH):
        full = jnp.concatenate([mine[d, per_l * l:per_l * (l + 1)] for d in range(N_DEV)]) + W['ada_b'][l]
        mods.append([_row(full[D * i:D * (i + 1)]) for i in range(6)])
    kfull = jnp.concatenate([mine[d, per_l * DEPTH:] for d in range(N_DEV)]) + W['kv_ada_b']
    k_shift, k_scale = _row(kfull[:D]), _row(kfull[D:])

    inv = 1.0 / (ROPE_THETA ** (np.arange(0, DR, 2, dtype=np.float32) / DR))
    inv128 = np.zeros((1, HD), np.float32)
    inv128[0, DN:DN + DR // 2] = inv
    inv128[0, DN + DR // 2:DN + DR] = inv
    cosf, sinf = rope_tables("rope_tables", positions.reshape(s, 1), jnp.asarray(inv128))
    zpad = lambda n: jnp.zeros((n,), F32)
    gkn128 = _row(jnp.concatenate([W['k_nope_norm_g'], zpad(HD - DN)]))
    gkr128 = _row(jnp.concatenate([zpad(DN), W['k_rope_norm_g'], zpad(HD - DN - DR)]))
    gq128 = [_row(jnp.concatenate([W['mla_q_nope_norm_g'][j], W['mla_q_rope_norm_g'][j], zpad(HD - DN - DR)]))
             for j in range(2)]

    expand = jnp.asarray(np.kron(np.eye(G, dtype=np.float32), np.ones((1, N), np.float32)))
    s5_raw, s5_mats = [], []
    for l in range(N_A):
        raw = (_row(W['s5_lam_re'][l]), _row(W['s5_lam_im'][l]), _row(W['s5_log_dt'][l]),
               W['s5_b_re'][l].transpose(2, 0, 1).reshape(P, G * N), W['s5_b_im'][l].transpose(2, 0, 1).reshape(P, G * N))
        ab_re, ab_im, bb_re_t, bb_im_t = s5_prep_fwd(f"s5_prep_fwd", *raw, expand)
        s5_raw.append(raw)
        s5_mats.append(_s5_place(ab_re, ab_im, bb_re_t, bb_im_t, W['s5_c_re'][l], W['s5_c_im'][l]))

    g1 = [_row(W['norm1_g'][l]) for l in range(DEPTH)]
    g2 = [_row(W['norm2_g'][l]) for l in range(DEPTH)]
    saved = []
    xs = x
    kv = None
    lw = [None] * DEPTH
    for l in range(DEPTH):
        sh1, sc1, gt1, sh2, sc2, gt2 = mods[l]
        rec = {'x_in': xs}
        if l >= N_A:
            lw[l] = layer_weights(l, xs)
        if l == N_A:
            kv_smalls = [_row(W['kv_norm_g']), k_shift, k_scale, _row(W['kv_a_norm_g']), gkn128, gkr128]
            kv_w = [lw[l]['wa_pad'], lw[l]['wkn_pad'], lw[l]['wv']]
            k_mat, v_mat = seg_forward("kv_fwd", seg_kv, [xs], kv_smalls, [cosf, sinf], kv_w,
                                       [(H * HD, _MXU), (H * DV, _MXU)], tap_widths=(KVL + HD, H * HD, H * DV))
            kv = {'x_in': xs, 'smalls': kv_smalls, 'k': k_mat, 'v': v_mat, 'w': kv_w}
        if l < N_A:
            (h,) = seg_forward("pre_fwd", seg_pre, [xs], [g1[l], sh1, sc1], [], [], [(D, F32)])
            wb, wc, a_tab = s5_mats[l]
            y, s0 = s5_scan_fwd("s5_scan_fwd", h, wb, wc, a_tab, _row(d_full[l]))
            lw[l] = layer_weights(l, y)
            (x_mid,) = seg_forward("glu_fwd", seg_glu, [xs, y], [gt1, _row(bglu_full[l])], [], [lw[l]['second']],
                                   [(D, F32)], tap_widths=(D,))
            rec.update(h=h, y=y, s0=s0)
        else:
            j = l - N_A
            q_smalls = [g1[l], sh1, sc1, _row(W['mla_q_norm_g'][j]), gq128[j]]
            (q_mat,) = seg_forward("q_fwd", seg_q, [xs], q_smalls, [cosf, sinf], [lw[l]['wdq'], lw[l]['wuq_pad']],
                                   [(H * HD, _MXU)], tap_widths=(QL, H * HD))
            o_mat, lse = attn_fwd("attn_fwd", q_mat, kv['k'], kv['v'])
            (x_mid,) = seg_forward("o_fwd", seg_o, [xs, o_mat], [gt1], [], [lw[l]['second']], [(D, F32)],
                                   tap_widths=(D,))
            rec.update(q=q_mat, o=o_mat, lse=lse, q_smalls=q_smalls)
        rec['x_mid'] = x_mid
        xs, rec['gate'], rec['up'] = ffn_forward("ffn_fwd", x_mid, g2[l], sh2, sc2, gt2,
                                                 lw[l]['wg'], lw[l]['wu'], lw[l]['wd'])
        saved.append(rec)

    dy, loss_part = loss_kernel("loss", xs, target)
    loss = lax.psum(loss_part[0, 0], ("x", "y", "c"))

    rblk = lambda a: a.reshape(N_DEV, a.shape[0] // N_DEV, a.shape[1])
    cblk = lambda a: a.reshape(a.shape[0], N_DEV, a.shape[1] // N_DEV).transpose(1, 0, 2)
    dmod = [None] * DEPTH
    dk_tot = []
    dv_tot = []
    dx = dy
    sends = [None] * DEPTH
    send_token = jnp.zeros((1, 1), F32)
    g_n1 = [None] * DEPTH
    g_n2 = [None] * DEPTH
    g_bglu = [None] * N_A
    g_dskip = [None] * N_A
    g_s5 = [None] * N_A
    g_qn, g_q128 = [None] * 2, [None] * 2
    for l in range(DEPTH - 1, -1, -1):
        rec = saved[l]
        sh1, sc1, gt1, sh2, sc2, gt2 = mods[l]
        dx, dgate, dup, dyd, h_b, a_b, dg2, dsh2, dsc2, dgt2 = ffn_backward(
            "ffn_bwd", rec['x_mid'], dx, rec['gate'], rec['up'], g2[l], sh2, sc2, gt2 + send_token,
            lw[l]['wg'], lw[l]['wu'], lw[l]['wd'])
        out_l = [matmul_tn("tn_ffn_in", h_b, dgate, _MXU, col_blocks=N_DEV),
                 matmul_tn("tn_ffn_in", h_b, dup, _MXU, col_blocks=N_DEV),
                 matmul_tn("tn_ffn_out", a_b, dyd, _MXU).reshape(N_DEV, FFB, D)]
        g_n2[l] = dg2
        if l == 0:
            sends_ffn0 = exchange_start("a2a_start_ffn0", out_l, False, dx)
            send_token = sends_ffn0[4][0:1, 0:1]
            out_l = []
        if l < N_A:
            (dx, dyy), (dz,), (g_b,), (dgt1, dbg) = seg_backward(
                "glu_bwd", seg_glu, [rec['x_in'], rec['y']], [gt1 + (send_token if l == 0 else 0.0), _row(bglu_full[l])], [],
                [lw[l]['second']],
                [dx], (D,), (D,))
            out_l.append(rblk(matmul_tn("tn_sq", g_b, dz, _MXU)))
            g_bglu[l] = dbg
            wb, wc, a_tab = s5_mats[l]
            dh, dwb, dwc, da, dd = s5_scan_bwd("s5_scan_bwd", rec['h'], dyy, rec['s0'], wb, wc, a_tab, _row(d_full[l]))
            g_dskip[l] = dd
            dab_re, dab_im, dbb_re_t, dbb_im_t, dc_re, dc_im = _s5_unplace(dwb, dwc, da)
            dlr, dli, dldt, dbr_t, dbi_t = s5_prep_bwd("s5_prep_bwd", *s5_raw[l], expand,
                                                       (dab_re, dab_im, dbb_re_t, dbb_im_t))
            g_s5[l] = (dlr.reshape(G, N), dli.reshape(G, N), dldt.reshape(G),
                       dbr_t.reshape(P, G, N).transpose(1, 2, 0), dbi_t.reshape(P, G, N).transpose(1, 2, 0), dc_re, dc_im)
            (dx,), _, _, (dg1, dsh1, dsc1) = seg_backward(
                "pre_bwd", seg_pre, [rec['x_in']], [g1[l], sh1, sc1], [], [], [dh], (), (), dx_add=dx)
        else:
            j = l - N_A
            (dx, do), (dzo,), (o_b,), (dgt1,) = seg_backward(
                "o_bwd", seg_o, [rec['x_in'], rec['o']], [gt1], [], [lw[l]['second']], [dx], (D,), (D,))
            out_l.append(rblk(matmul_tn("tn_sq", o_b, dzo, _MXU)))
            dq, dk, dv = attn_bwd("attn_bwd", rec['q'], kv['k'], kv['v'], rec['o'], do, rec['lse'])
            dk_tot.append(dk)
            dv_tot.append(dv)
            (dx,), (dql, dqq), (hq_b, qn_b), (dg1, dsh1, dsc1, dqg, dq128) = seg_backward(
                "q_bwd", seg_q, [rec['x_in']], rec['q_smalls'], [cosf, sinf], [lw[l]['wdq'], lw[l]['wuq_pad']],
                [dq], (QL, H * HD), (D, QL), dx_add=dx)
            g_dq = rblk(matmul_tn("tn_dq", hq_b, dql, _MXU))
            g_uq = cblk(_unpad_heads(matmul_tn("tn_uq", qn_b, dqq, _MXU), DN + DR, HD))
            g_qn[j], g_q128[j] = dqg, dq128
        g_n1[l] = dg1
        dmod[l] = jnp.concatenate([dsh1, dsc1, dgt1, dsh2, dsc2, dgt2], axis=1)
        if l == N_A:
            (dx,), (dta, dtk, dtv), (hk_b, ckv_b), (dkg, dksh, dksc, dag, dgkn, dgkr) = seg_backward(
                "kv_bwd", seg_kv, [kv['x_in']], kv['smalls'], [cosf, sinf], kv['w'],
                [dk_tot, dv_tot], (KVL + HD, H * HD, H * DV), (D, KVL), dx_add=dx)
            g_wa = matmul_tn("tn_kva", hk_b, dta, _MXU)
            g_wa = jnp.concatenate([g_wa[:, :KVL], g_wa[:, KVL + DN:KVL + DN + DR]], axis=1)
            g_kn = matmul_tn("tn_kn", ckv_b, dtk, _MXU).reshape(KVL, H, HD)[:, :, :DN]
            g_v = matmul_tn("tn_v", ckv_b, dtv, _MXU).reshape(KVL, H, DV)
            g_wkvb = jnp.concatenate([g_kn, g_v], axis=2).reshape(KVL, H * (DN + DV))
            dkmod = jnp.concatenate([dksh, dksc], axis=1)
            out_l += [jnp.concatenate([cblk(g_wkvb), g_dq], axis=1), rblk(g_wa)]
        if l > N_A:
            out_l.append(g_dq)
        if l >= N_A:
            out_l.append(g_uq)
        if l > 0:
            sends[l] = exchange_start(f"a2a_start_{l}", out_l, False, dx)
            send_token = sends[l][4][0:1, 0:1]
        if l == N_A - 1:
            early_flat = _pack_rows([a.reshape(-1) for a in g_s5[l]])
            early_st = exchange_start("small_start_s5", [early_flat], True, dx)
            send_token = send_token + early_st[4][0:1, 0:1]
    grad_x = dx

    s5_names = ['s5_lam_re', 's5_lam_im', 's5_log_dt', 's5_b_re', 's5_b_im', 's5_c_re', 's5_c_im']
    small = {
        'norm1_g': jnp.concatenate(g_n1, axis=0), 'norm2_g': jnp.concatenate(g_n2, axis=0),
        'kv_norm_g': dkg, 'kv_a_norm_g': dag, 'k_nope_norm_g': dgkn[:, :DN], 'k_rope_norm_g': dgkr[:, DN:DN + DR],
        'mla_q_norm_g': jnp.concatenate(g_qn, axis=0),
        'mla_q_nope_norm_g': jnp.concatenate([g[:, :DN] for g in g_q128], axis=0),
        'mla_q_rope_norm_g': jnp.concatenate([g[:, DN:DN + DR] for g in g_q128], axis=0),
        's5_d': jnp.concatenate(g_dskip, axis=0), 's5_b_glu': jnp.concatenate(g_bglu, axis=0),
    }
    for i, n in enumerate(s5_names):
        small[n] = jnp.stack([g_s5[l][i] for l in range(N_A - 1)])
    small_names = [n for n in REPLICATED if n not in ('ada_b', 'kv_ada_b')] + SHARDED_VEC
    flat_small = _pack_rows([small[n].reshape(-1) for n in small_names])

    dm = jnp.concatenate(dmod + [dkmod], axis=1)[0]
    per_dev = []
    for d in range(N_DEV):
        cols = [dm[6 * D * l + per_l * d:6 * D * l + per_l * (d + 1)] for l in range(DEPTH)]
        cols.append(dm[6 * D * DEPTH + (2 * D // N_DEV) * d:6 * D * DEPTH + (2 * D // N_DEV) * (d + 1)])
        per_dev.append(jnp.concatenate(cols))
    dm_dev = jnp.stack(per_dev)
    gdm = all_gather("gather_dmod", dm_dev)
    small_st = exchange_start("small_start", [flat_small], True, gdm)
    sends[0] = exchange_start("a2a_start_0", out_l, False, small_st[4])
    dm_mine = lax.dynamic_index_in_dim(gdm, me, axis=1, keepdims=False) + sends[0][4][0, 0]
    g_wmod = small_matmul_tn("dmod_matmul", ca_all, dm_mine)
    g_ada_w = jnp.stack([g_wmod[:, per_l * l:per_l * (l + 1)] for l in range(DEPTH)])
    g_kv_ada_w = g_wmod[:, per_l * DEPTH:]
    dm_sum = sum_parts("sum_dmod", gdm.reshape(N_DEV, N_DEV, n_mod))
    g_ada_b = jnp.stack([jnp.concatenate([dm_sum[d, per_l * l:per_l * (l + 1)] for d in range(N_DEV)])
                         for l in range(DEPTH)])
    g_kv_ada_b = jnp.concatenate([dm_sum[d, per_l * DEPTH:] for d in range(N_DEV)])

    grads, out_delta, out_m, out_v = {}, {}, {}, {}

    def update(name, parts, base=0, stride=0):
        shp = W[name].shape
        shp3 = shp if len(shp) == 3 else (1,) + shp
        res = adamw("adamw_" + name, parts, W[name].reshape(shp3), M[name].reshape(shp3), V[name].reshape(shp3),
                    base, stride)
        grads[name], out_delta[name], out_m[name], out_v[name] = (a.reshape(shp) for a in res)

    update('ada_w', g_ada_w.reshape(1, DEPTH * D, per_l), 0, D)
    update('kv_ada_w', g_kv_ada_w[None])

    chains = {}

    def update_layer(name, parts, layer, base=0):
        shp = W[name].shape
        shp3 = shp if len(shp) == 3 else (1,) + shp
        chains[name] = adamw_layer(f"adamw_{name}_{layer}", parts, W[name].reshape(shp3), M[name].reshape(shp3),
                                   V[name].reshape(shp3), layer, chains.get(name), base)
        grads[name], out_delta[name], out_m[name], out_v[name] = (a.reshape(shp) for a in chains[name])

    ffn_parts = [[None] * DEPTH for _ in range(3)]

    def landed(name, started, after):
        lands = exchange_wait(name, started, after, False)
        return [lax.dynamic_update_slice(ld, lax.dynamic_index_in_dim(src, me, 0, keepdims=True), (me,) + (0,) * (src.ndim - 1))
                for ld, src in zip(lands, started[2])]

    def receive(l, after):
        recv = landed(f"a2a_wait_{l}", sends[l], after)
        if l == 0:
            recv = landed("a2a_wait_ffn0", sends_ffn0, after) + recv
        for i in range(3):
            ffn_parts[i][l] = recv[i]
        if l < N_A:
            update_layer('s5_w_glu', recv[3], l)
        else:
            update_layer('mla_w_o', recv[3], l - N_A)
            if l == N_A:
                update_layer('w_kv_b', recv[4], 0)
                update_layer('mla_w_dq', recv[4], 0, KVL)
                update_layer('w_kv_a', recv[5], 0)
            else:
                update_layer('mla_w_dq', recv[4], l - N_A)
            update_layer('mla_w_uq', recv[-1], l - N_A)

    for l in range(DEPTH - 1, 0, -1):
        receive(l, out_delta['kv_ada_w'])

    def gathered_sum(name, started, own, after):
        (land,) = exchange_wait(name + "_wait", started, after, True)
        return sum_parts("sum_" + name, lax.dynamic_update_slice(land, own[None], (me, 0, 0))).reshape(-1)

    early_sum = gathered_sum("small_s5", early_st, early_flat, chains['s5_w_glu'][1])
    g_small_sum = gathered_sum("small", small_st, flat_small, early_sum)
    off = 0
    for n in small_names:
        size = int(np.prod(small[n].shape))
        full = g_small_sum[off:off + size]
        off += size
        if n in SHARDED_VEC:
            full = lax.dynamic_slice_in_dim(full.reshape(N_A, D), me * (D // N_DEV), D // N_DEV, axis=1)
        grads[n] = full.reshape(small[n].shape if n in s5_names else W[n].shape)
    off = 0
    for i, n in enumerate(s5_names):
        size = int(np.prod(g_s5[N_A - 1][i].shape))
        last = early_sum[off:off + size].reshape((1,) + g_s5[N_A - 1][i].shape)
        off += size
        grads[n] = jnp.concatenate([grads[n], last], axis=0)
    grads['ada_b'] = g_ada_b
    grads['kv_ada_b'] = g_kv_ada_b

    big_small = ('s5_b_re', 's5_b_im', 's5_c_re', 's5_c_im')
    packed_names = [n for n in REPLICATED + SHARDED_VEC if n not in big_small]

    def pack(dct):
        flat_ = jnp.concatenate([dct[n].reshape(-1) for n in packed_names])
        n_ = int(flat_.shape[0])
        p_ = -(-n_ // 8192) * 8192
        return jnp.pad(flat_, (0, p_ - n_)).reshape(p_ // 128, 128)

    _, d_p, m_p, v_p = adamw("adamw_small", pack(grads)[None], pack(W)[None], pack(M)[None], pack(V)[None])
    off = 0
    d_p, m_p, v_p = d_p.reshape(-1), m_p.reshape(-1), v_p.reshape(-1)
    for n in packed_names:
        size = int(np.prod(W[n].shape))
        out_delta[n] = d_p[off:off + size].reshape(W[n].shape)
        out_m[n] = m_p[off:off + size].reshape(W[n].shape)
        out_v[n] = v_p[off:off + size].reshape(W[n].shape)
        off += size
    for n in big_small:
        shp = W[n].shape
        view = (1, int(np.prod(shp[:-1])), shp[-1])
        res = adamw("adamw_" + n, grads[n].reshape(view), W[n].reshape(view), M[n].reshape(view), V[n].reshape(view))
        _, out_delta[n], out_m[n], out_v[n] = (a.reshape(shp) for a in res)

    receive(0, d_p)
    for i, name in enumerate(('ffn_w_gate', 'ffn_w_up', 'ffn_w_down')):
        res = adamw_multi("adamw_" + name, ffn_parts[i], W[name], M[name], V[name])
        grads[name], out_delta[name], out_m[name], out_v[name] = res

    return (loss, grad_x[None], *[grads[n] for n in WEIGHT_NAMES], *[out_delta[n] for n in WEIGHT_NAMES],
            *[out_m[n] for n in WEIGHT_NAMES], *[out_v[n] for n in WEIGHT_NAMES])
```

```python
import functools
import math

import numpy as np
import jax
import jax.numpy as jnp
from jax import lax
from jax.experimental import pallas as pl
from jax.experimental.pallas import tpu as pltpu

F32 = jnp.float32
_MXU = jnp.bfloat16
HI = lax.Precision.HIGHEST

D = 1024
DEPTH = 4
N_A = 2
FF = 2816
FFB = 384
FFP = 8 * FFB
N_DEV = 8
G = 64
P = 16
N = 64
GB = 8
NBLK = G // GB
HALF = GB * N
H = 16
HP = H // 2
DN, DR, DV = 64, 32, 64
HD = 128
QL = 256
KVL = 256
CHUNK = 64
ROPE_THETA = 10000.0
ATTN_SCALE = 1.0 / math.sqrt(DN + DR)
LOG2E = 1.4426950408889634
EXP2_SCALE = ATTN_SCALE * LOG2E
EPS = 1e-6
ADAM_LR, ADAM_B1, ADAM_B2, ADAM_EPS, ADAM_WD, ADAM_STEP = 0.001, 0.9, 0.999, 1e-08, 0.01, 10
VMEM_LIMIT = 56 * 1024 * 1024
MESH = pl.DeviceIdType.MESH

TILE_ROW = 256
TILE_ATT = 512
TILE_SCAN = 1024


def _params(n_grid):
    return pltpu.CompilerParams(dimension_semantics=("arbitrary",) * n_grid, vmem_limit_bytes=VMEM_LIMIT)


@jax.custom_vjp
def mm(a, w):
    return jnp.dot(a.astype(_MXU), w, preferred_element_type=F32)


def _mm_fwd(a, w):
    return mm(a, w), w


def _mm_bwd(w, g):
    da = lax.dot_general(g.astype(_MXU), w, (((1,), (1,)), ((), ())), preferred_element_type=F32)
    return da, jnp.zeros_like(w)


mm.defvjp(_mm_fwd, _mm_bwd)


def rms(x, g):
    return x * lax.rsqrt(jnp.mean(x * x, axis=-1, keepdims=True) + EPS) * g


def modulate(h, shift, scale):
    return h * (1.0 + scale) + shift


def _lane(n=HD):
    return lax.broadcasted_iota(jnp.int32, (1, n), 1)


def _rot_matrix():
    r = lax.broadcasted_iota(jnp.int32, (HD, HD), 0)
    c = lax.broadcasted_iota(jnp.int32, (HD, HD), 1)
    first = (c >= DN) & (c < DN + DR // 2) & (r == c + DR // 2)
    second = (c >= DN + DR // 2) & (c < DN + DR) & (r == c - DR // 2)
    return jnp.where(first, -1.0, jnp.where(second, 1.0, 0.0)).astype(F32)


def head_norm_rope(xh, g128, cosf, sinf, rot, with_nope):
    lane = _lane()
    m_n = lane < DN
    m_r = (lane >= DN) & (lane < DN + DR)
    sq = xh * xh
    inv_r = lax.rsqrt(jnp.sum(jnp.where(m_r, sq, 0.0), axis=-1, keepdims=True) / DR + EPS)
    if with_nope:
        inv_n = lax.rsqrt(jnp.sum(jnp.where(m_n, sq, 0.0), axis=-1, keepdims=True) / DN + EPS)
        inv = jnp.where(m_n, inv_n, jnp.where(m_r, inv_r, 0.0))
    else:
        inv = jnp.where(m_r, inv_r, 0.0)
    xg = xh * inv * g128
    return xg * cosf + jnp.dot(xg, rot, precision=HI, preferred_element_type=F32) * sinf


def seg_pre(x, g, sh, sc):
    return (modulate(rms(x, g), sh, sc),), ()


def seg_glu(x, y, gt, b, t_z, w):
    g = jax.nn.gelu(y)
    z = mm(g, w) + b + t_z
    return (x + gt * (g * jax.nn.sigmoid(z)),), (g.astype(_MXU),)


def seg_o(x, o, gt, t_o, w):
    return (x + gt * (mm(o, w) + t_o),), (o.astype(_MXU),)


def seg_q(x, g, sh, sc, qg, g128, t_l, t_q, cosf, sinf, wdq, wuq):
    h = modulate(rms(x, g), sh, sc)
    ql = mm(h, wdq) + t_l
    qn = rms(ql, qg)
    q = mm(qn, wuq) + t_q
    rot = _rot_matrix()
    heads = [head_norm_rope(q[:, HD * i:HD * (i + 1)], g128, cosf, sinf, rot, True) for i in range(H)]
    return (jnp.concatenate(heads, axis=1),), (h.astype(_MXU), qn.astype(_MXU))


def seg_kv(x, g, sh, sc, ag, gkn, gkr, t_a, t_k, t_v, cosf, sinf, wa, wkn, wv):
    hk = modulate(rms(x, g), sh, sc)
    kva = mm(hk, wa) + t_a
    ckv = rms(kva[:, :KVL], ag)
    kr = head_norm_rope(kva[:, KVL:KVL + HD], gkr, cosf, sinf, _rot_matrix(), False)
    kn = mm(ckv, wkn) + t_k
    v = mm(ckv, wv) + t_v
    heads = []
    for i in range(H):
        kh = kn[:, HD * i:HD * (i + 1)]
        inv = lax.rsqrt(jnp.sum(kh * kh, axis=-1, keepdims=True) / DN + EPS)
        heads.append(kh * inv * gkn + kr)
    return (jnp.concatenate(heads, axis=1), v), (hk.astype(_MXU), ckv.astype(_MXU))


def _row_call(name, body_fn, rows, fulls, out_rows, out_accs, tile):
    s = rows[0].shape[0]
    n_tiles = s // tile
    n_rows, n_fulls, n_or, n_oa = len(rows), len(fulls), len(out_rows), len(out_accs)

    def kern(*refs):
        i = pl.program_id(0)
        row_v = [r[...] for r in refs[:n_rows]]
        full_v = [r[...] for r in refs[n_rows:n_rows + n_fulls]]
        o_refs = refs[n_rows + n_fulls:]
        ro, ao = body_fn(row_v, full_v)
        for r, v in zip(o_refs[:n_or], ro):
            r[...] = v.astype(r.dtype)
        if n_oa:
            @pl.when(i == 0)
            def _():
                for r in o_refs[n_or:]:
                    r[...] = jnp.zeros(r.shape, r.dtype)
            for r, v in zip(o_refs[n_or:], ao):
                r[...] += v.astype(r.dtype)

    in_specs = [pl.BlockSpec((tile, a.shape[1]), lambda i: (i, 0)) for a in rows]
    for a in fulls:
        big = a.size * a.dtype.itemsize > (1 << 20)
        nd = a.ndim
        in_specs.append(pl.BlockSpec(a.shape, functools.partial(lambda i, nd_: (0,) * nd_, nd_=nd),
                                     **({"pipeline_mode": pl.Buffered(1)} if big else {})))
    out_shape = [jax.ShapeDtypeStruct((s, w), dt) for w, dt in out_rows]
    out_shape += [jax.ShapeDtypeStruct(shp, dt) for shp, dt in out_accs]
    out_specs = [pl.BlockSpec((tile, w), lambda i: (i, 0)) for w, _ in out_rows]
    out_specs += [pl.BlockSpec(shp, functools.partial(lambda i, nd_: (0,) * nd_, nd_=len(shp))) for shp, _ in out_accs]
    res = pl.pallas_call(kern, out_shape=out_shape, grid=(n_tiles,), in_specs=in_specs, out_specs=out_specs,
                         name=name, compiler_params=_params(1))(*rows, *fulls)
    return list(res)


def seg_forward(name, seg, rows, smalls, consts_rows, consts_full, out_widths, tile=TILE_ROW, tap_widths=()):
    n_r, n_s, n_cr = len(rows), len(smalls), len(consts_rows)

    def body(row_v, full_v):
        t = row_v[0].shape[0]
        taps = [jnp.zeros((t, w), F32) for w in tap_widths]
        outs, _ = seg(*row_v[:n_r], *full_v[:n_s], *taps, *row_v[n_r:], *full_v[n_s:])
        return outs, ()

    return _row_call(name, body, list(rows) + list(consts_rows), list(smalls) + list(consts_full),
                     out_widths, [], tile)


def seg_backward(name, seg, rows, smalls, consts_rows, consts_full, cots, tap_widths, aux_widths,
                 dx_add=None, tile=TILE_ROW):
    cot_groups = [list(c) if isinstance(c, (list, tuple)) else [c] for c in cots]
    cot_flat = [a for grp in cot_groups for a in grp]
    n_r, n_s, n_cr, n_c = len(rows), len(smalls), len(consts_rows), len(cot_flat)
    has_add = dx_add is not None

    def body(row_v, full_v):
        t = row_v[0].shape[0]
        prim_rows = row_v[:n_r]
        c_rows = row_v[n_r:n_r + n_cr]
        cot_v = list(row_v[n_r + n_cr:n_r + n_cr + n_c])
        add_v = row_v[n_r + n_cr + n_c] if has_add else None
        small_v = full_v[:n_s]
        c_full = full_v[n_s:]
        taps = [jnp.zeros((t, w), F32) for w in tap_widths]
        cot_sum = []
        for grp in cot_groups:
            parts = [cot_v.pop(0).astype(F32) for _ in grp]
            cot_sum.append(functools.reduce(lambda x_, y_: x_ + y_, parts))

        def f(*args):
            return seg(*args, *c_rows, *c_full)

        _, vjp_fn, aux = jax.vjp(f, *prim_rows, *small_v, *taps, has_aux=True)
        grads = vjp_fn(tuple(cot_sum))
        d_rows = list(grads[:n_r])
        if has_add:
            d_rows[0] = d_rows[0] + add_v
        d_small = grads[n_r:n_r + n_s]
        d_taps = grads[n_r + n_s:]
        return d_rows + list(d_taps) + list(aux), [jnp.sum(g, axis=0, keepdims=True) if g.shape[0] != 1 else g
                                                   for g in d_small]

    all_rows = list(rows) + list(consts_rows) + cot_flat + ([dx_add] if has_add else [])
    out_rows = [(a.shape[1], F32) for a in rows] + [(w, _MXU) for w in tap_widths] + [(w, _MXU) for w in aux_widths]
    out_accs = [((1, a.shape[1]), F32) for a in smalls]
    res = _row_call(name, body, all_rows, list(smalls) + list(consts_full), out_rows, out_accs, tile)
    n_t, n_a = len(tap_widths), len(aux_widths)
    return res[:n_r], res[n_r:n_r + n_t], res[n_r + n_t:n_r + n_t + n_a], res[n_r + n_t + n_a:]


def _split(n):
    if n <= 1024:
        return n
    for t in (1408, 1024, 768, 512, 256, 128):
        if n % t == 0:
            return t
    raise ValueError(n)


def matmul_tn(name, a, b, out_dtype, col_blocks=None):
    s, k1 = a.shape
    _, k2 = b.shape
    tm, ts = _split(k1), 2048
    if col_blocks is None:
        tn, per_step, wblk = _split(k2), 1, None
    else:
        wblk = k2 // col_blocks
        per_step = max(1, min(col_blocks, 1536 // wblk))
        tn = per_step * wblk
    n_s = s // ts

    def kern(a_ref, b_ref, o_ref, acc_ref):
        k = pl.program_id(2)

        @pl.when(k == 0)
        def _():
            acc_ref[...] = jnp.zeros(acc_ref.shape, F32)

        acc_ref[...] += lax.dot_general(a_ref[...], b_ref[...], (((0,), (0,)), ((), ())),
                                        preferred_element_type=F32)

        @pl.when(k == n_s - 1)
        def _():
            if col_blocks is None:
                o_ref[...] = acc_ref[...].astype(o_ref.dtype)
            else:
                for cb in range(per_step):
                    o_ref[cb] = acc_ref[:, wblk * cb:wblk * (cb + 1)].astype(o_ref.dtype)

    if col_blocks is None:
        out_shape = jax.ShapeDtypeStruct((k1, k2), out_dtype)
        out_spec = pl.BlockSpec((tm, tn), lambda i, j, k: (i, j))
    else:
        out_shape = jax.ShapeDtypeStruct((col_blocks, k1, wblk), out_dtype)
        out_spec = pl.BlockSpec((per_step, tm, wblk), lambda i, j, k: (j, i, 0))
    return pl.pallas_call(
        kern, out_shape=out_shape, grid=(k1 // tm, k2 // tn, n_s),
        in_specs=[pl.BlockSpec((ts, tm), lambda i, j, k: (k, i)), pl.BlockSpec((ts, tn), lambda i, j, k: (k, j))],
        out_specs=out_spec,
        scratch_shapes=[pltpu.VMEM((tm, tn), F32)], name=name, compiler_params=_params(3))(a, b)


def ffn_forward(name, x, g, sh, sc, gt, wg, wu, wd, tile=TILE_ROW):
    s = x.shape[0]
    fp = wg.shape[1]
    blk = 2 * FFB
    n_blk = fp // blk

    def kern(x_ref, g_ref, sh_ref, sc_ref, gt_ref, wg_ref, wu_ref, wd_ref, o_ref, gate_ref, up_ref):
        xv = x_ref[...]
        hb = modulate(rms(xv, g_ref[...]), sh_ref[...], sc_ref[...]).astype(_MXU)
        y = jnp.zeros((tile, D), F32)
        for c in range(n_blk):
            cs = slice(blk * c, blk * (c + 1))
            gate = jnp.dot(hb, wg_ref[:, cs], preferred_element_type=F32)
            up = jnp.dot(hb, wu_ref[:, cs], preferred_element_type=F32)
            gate_ref[:, cs] = gate.astype(_MXU)
            up_ref[:, cs] = up.astype(_MXU)
            y = y + jnp.dot((jax.nn.silu(gate) * up).astype(_MXU), wd_ref[cs, :], preferred_element_type=F32)
        o_ref[...] = xv + gt_ref[...] * y

    row = lambda w: pl.BlockSpec((tile, w), lambda i: (i, 0))
    vec = pl.BlockSpec((1, D), lambda i: (0, 0))
    wspec = lambda a: pl.BlockSpec(a.shape, lambda i: (0, 0), pipeline_mode=pl.Buffered(1))
    return pl.pallas_call(
        kern, out_shape=[jax.ShapeDtypeStruct((s, D), F32), jax.ShapeDtypeStruct((s, fp), _MXU),
                         jax.ShapeDtypeStruct((s, fp), _MXU)],
        grid=(s // tile,), in_specs=[row(D), vec, vec, vec, vec, wspec(wg), wspec(wu), wspec(wd)],
        out_specs=[row(D), row(fp), row(fp)], name=name, compiler_params=_params(1))(x, g, sh, sc, gt, wg, wu, wd)


def ffn_backward(name, x, dxo, gate, up, g, sh, sc, gt, wg, wu, wd, tile=TILE_ROW):
    s = x.shape[0]
    fp = wg.shape[1]
    blk = 2 * FFB
    n_blk = fp // blk

    def kern(x_ref, dxo_ref, gate_ref, up_ref, g_ref, sh_ref, sc_ref, gt_ref, wg_ref, wu_ref, wd_ref,
             dx_ref, dg_ref, du_ref, dy_ref, h_ref, a_ref, dgn_ref, dsh_ref, dsc_ref, dgt_ref):
        i = pl.program_id(0)

        @pl.when(i == 0)
        def _():
            for r in (dgn_ref, dsh_ref, dsc_ref, dgt_ref):
                r[...] = jnp.zeros(r.shape, F32)

        dxo = dxo_ref[...]
        h, pre_vjp = jax.vjp(lambda *p: modulate(rms(p[0], p[1]), p[2], p[3]), x_ref[...], g_ref[...], sh_ref[...],
                             sc_ref[...])
        h_ref[...] = h.astype(_MXU)
        dyb = (gt_ref[...] * dxo).astype(_MXU)
        dy_ref[...] = dyb
        y = jnp.zeros((tile, D), F32)
        dh = jnp.zeros((tile, D), F32)
        tr = (((1,), (1,)), ((), ()))
        for c in range(n_blk):
            cs = slice(blk * c, blk * (c + 1))
            gate = gate_ref[:, cs].astype(F32)
            up = up_ref[:, cs].astype(F32)
            sig = jax.nn.sigmoid(gate)
            sl = gate * sig
            ab = (sl * up).astype(_MXU)
            a_ref[:, cs] = ab
            y = y + jnp.dot(ab, wd_ref[cs, :], preferred_element_type=F32)
            da = lax.dot_general(dyb, wd_ref[cs, :], tr, preferred_element_type=F32)
            dgb = (da * up * (sig * (1.0 + gate * (1.0 - sig)))).astype(_MXU)
            dub = (da * sl).astype(_MXU)
            dg_ref[:, cs] = dgb
            du_ref[:, cs] = dub
            dh = dh + lax.dot_general(dgb, wg_ref[:, cs], tr, preferred_element_type=F32) \
                + lax.dot_general(dub, wu_ref[:, cs], tr, preferred_element_type=F32)
        dgt_ref[...] += jnp.sum(dxo * y, axis=0, keepdims=True)
        dx_pre, dgn, dsh, dsc = pre_vjp(dh)
        dx_ref[...] = dxo + dx_pre
        dgn_ref[...] += dgn
        dsh_ref[...] += dsh
        dsc_ref[...] += dsc

    row = lambda w: pl.BlockSpec((tile, w), lambda i: (i, 0))
    vec = pl.BlockSpec((1, D), lambda i: (0, 0))
    wspec = lambda a: pl.BlockSpec(a.shape, lambda i: (0, 0), pipeline_mode=pl.Buffered(1))
    rows_out = [(D, F32), (fp, _MXU), (fp, _MXU), (D, _MXU), (D, _MXU), (fp, _MXU)]
    return pl.pallas_call(
        kern,
        out_shape=[jax.ShapeDtypeStruct((s, w), dt) for w, dt in rows_out] + [jax.ShapeDtypeStruct((1, D), F32)] * 4,
        grid=(s // tile,),
        in_specs=[row(D), row(D), row(fp), row(fp), vec, vec, vec, vec, wspec(wg), wspec(wu), wspec(wd)],
        out_specs=[row(w) for w, _ in rows_out] + [vec] * 4,
        name=name, compiler_params=_params(1))(x, dxo, gate, up, g, sh, sc, gt, wg, wu, wd)


def small_matmul(name, a, w, tn=256):
    m, k = a.shape
    n = w.shape[1]

    def kern(a_ref, w_ref, o_ref):
        o_ref[...] = jnp.dot(a_ref[...].astype(_MXU), w_ref[...].astype(_MXU), preferred_element_type=F32)

    return pl.pallas_call(kern, out_shape=jax.ShapeDtypeStruct((m, n), F32), grid=(n // tn,),
                          in_specs=[pl.BlockSpec((m, k), lambda j: (0, 0)), pl.BlockSpec((k, tn), lambda j: (0, j))],
                          out_specs=pl.BlockSpec((m, tn), lambda j: (0, j)), name=name,
                          compiler_params=_params(1))(a, w)


def small_matmul_tn(name, a, b, tn=256):
    m, k = a.shape
    n = b.shape[1]

    def kern(a_ref, b_ref, o_ref):
        o_ref[...] = lax.dot_general(a_ref[...].astype(_MXU), b_ref[...].astype(_MXU), (((0,), (0,)), ((), ())),
                                     preferred_element_type=F32)

    return pl.pallas_call(kern, out_shape=jax.ShapeDtypeStruct((k, n), F32), grid=(n // tn,),
                          in_specs=[pl.BlockSpec((m, k), lambda j: (0, 0)), pl.BlockSpec((m, tn), lambda j: (0, j))],
                          out_specs=pl.BlockSpec((k, tn), lambda j: (0, j)), name=name,
                          compiler_params=_params(1))(a, b)


def _s5_prep_math(lam_re, lam_im, log_dt, b_re_t, b_im_t, expand):
    dt = jnp.dot(jnp.exp(log_dt), expand, precision=HI, preferred_element_type=F32)
    mag = jnp.exp(lam_re * dt)
    ab_re = mag * jnp.cos(lam_im * dt)
    ab_im = mag * jnp.sin(lam_im * dt)
    den = lam_re * lam_re + lam_im * lam_im
    nr = ab_re - 1.0
    ni = ab_im
    f_re = (nr * lam_re + ni * lam_im) / den
    f_im = (ni * lam_re - nr * lam_im) / den
    bb_re = f_re * b_re_t - f_im * b_im_t
    bb_im = f_re * b_im_t + f_im * b_re_t
    return ab_re, ab_im, bb_re, bb_im


def _whole(kern, name, out_shape, *args):
    return pl.pallas_call(kern, out_shape=out_shape, name=name,
                          compiler_params=pltpu.CompilerParams(vmem_limit_bytes=VMEM_LIMIT))(*args)


def s5_prep_fwd(name, lam_re, lam_im, log_dt, b_re_t, b_im_t, expand):
    def kern(a, b, c, d, e, f, o0, o1, o2, o3):
        r = _s5_prep_math(a[...], b[...], c[...], d[...], e[...], f[...])
        for o, v in zip((o0, o1, o2, o3), r):
            o[...] = v

    gn = lam_re.shape[1]
    shp = [jax.ShapeDtypeStruct((1, gn), F32)] * 2 + [jax.ShapeDtypeStruct((P, gn), F32)] * 2
    return _whole(kern, name, shp, lam_re, lam_im, log_dt, b_re_t, b_im_t, expand)


def s5_prep_bwd(name, lam_re, lam_im, log_dt, b_re_t, b_im_t, expand, cots):
    def kern(a, b, c, d, e, f, c0, c1, c2, c3, o0, o1, o2, o3, o4):
        ex = f[...]
        _, vjp_fn = jax.vjp(lambda *p: _s5_prep_math(*p, ex), a[...], b[...], c[...], d[...], e[...])
        g = vjp_fn((c0[...], c1[...], c2[...], c3[...]))
        for o, v in zip((o0, o1, o2, o3, o4), g):
            o[...] = v

    shp = [jax.ShapeDtypeStruct(a.shape, F32) for a in (lam_re, lam_im, log_dt, b_re_t, b_im_t)]
    return _whole(kern, name, shp, lam_re, lam_im, log_dt, b_re_t, b_im_t, expand, *cots)


def _cpowers(ar, ai):
    pw = [(ar, ai)]
    for _ in range(7):
        pr, pi = pw[-1]
        pw.append((pr * ar - pi * ai, pr * ai + pi * ar))
    return pw


def _row_select(row, values):
    out = jnp.broadcast_to(values[7], (8, values[7].shape[1]))
    for r in range(6, -1, -1):
        out = jnp.where(row == r, values[r], out)
    return out


def _scan_tables(ar, ai, reverse):
    pw = _cpowers(ar, ai)
    row = lax.broadcasted_iota(jnp.int32, (8, ar.shape[1]), 0)
    steps = []
    for d in (1, 2, 4):
        keep = (row <= 7 - d) if reverse else (row >= d)
        steps.append((jnp.where(keep, pw[d - 1][0], 0.0), jnp.where(keep, pw[d - 1][1], 0.0)))
    order = list(range(7, -1, -1)) if reverse else list(range(8))
    carry = (_row_select(row, [pw[i][0] for i in order]), _row_select(row, [pw[i][1] for i in order]))
    return steps, carry


def _tile_scan_fwd(xr, xi, cr, ci, steps, carry_m):
    for d, (mr, mi) in zip((1, 2, 4), steps):
        sr = pltpu.roll(xr, d, 0)
        si = pltpu.roll(xi, d, 0)
        xr, xi = xr + mr * sr - mi * si, xi + mr * si + mi * sr
    pr, pi = carry_m
    return xr + pr * cr - pi * ci, xi + pr * ci + pi * cr


def _tile_scan_rev(xr, xi, cr, ci, steps, carry_m):
    for d, (mr, mi) in zip((1, 2, 4), steps):
        sr = pltpu.roll(xr, 8 - d, 0)
        si = pltpu.roll(xi, 8 - d, 0)
        xr, xi = xr + mr * sr + mi * si, xi + mr * si - mi * sr
    pr, pi = carry_m
    return xr + pr * cr + pi * ci, xi + pr * ci - pi * cr


def _fwd_scan_block(buf, row0, n_tiles8, ar, ai, c0r, c0i):
    steps, carry_m = _scan_tables(ar, ai, False)

    def body(j, carry):
        cr, ci = carry
        r0 = pl.multiple_of(row0 + j * 8, 8)
        xr = buf[pl.ds(r0, 8), 0:HALF]
        xi = buf[pl.ds(r0, 8), HALF:2 * HALF]
        xr, xi = _tile_scan_fwd(xr, xi, cr, ci, steps, carry_m)
        buf[pl.ds(r0, 8), 0:HALF] = xr
        buf[pl.ds(r0, 8), HALF:2 * HALF] = xi
        return xr[7:8], xi[7:8]

    return lax.fori_loop(0, n_tiles8, body, (c0r, c0i))


def s5_scan_fwd(name, h, wb, wc, a_tab, dskip, tile=TILE_SCAN):
    s = h.shape[0]
    n_t = s // tile

    def kern(h_ref, wb_ref, wc_ref, a_ref, d_ref, y_ref, s0_ref, carry_ref, buf):
        i = pl.program_id(0)

        @pl.when(i == 0)
        def _():
            carry_ref[...] = jnp.zeros(carry_ref.shape, F32)

        s0_ref[0] = carry_ref[...]
        for k in range(NBLK):
            cols = slice(GB * P * k, GB * P * (k + 1))
            u = h_ref[:, cols]
            buf[...] = jnp.dot(u.astype(_MXU), wb_ref[k], preferred_element_type=F32)
            ar = a_ref[k, :, 0:HALF]
            ai = a_ref[k, :, HALF:2 * HALF]
            cr, ci = _fwd_scan_block(buf, 0, tile // 8, ar, ai, carry_ref[k:k + 1, 0:HALF],
                                     carry_ref[k:k + 1, HALF:2 * HALF])
            carry_ref[k:k + 1, 0:HALF] = cr
            carry_ref[k:k + 1, HALF:2 * HALF] = ci
            y_ref[:, cols] = jnp.dot(buf[...].astype(_MXU), wc_ref[k], preferred_element_type=F32) + d_ref[:, cols] * u

    full = lambda a: pl.BlockSpec(a.shape, functools.partial(lambda i, nd_: (0,) * nd_, nd_=a.ndim))
    return pl.pallas_call(
        kern,
        out_shape=[jax.ShapeDtypeStruct((s, D), F32), jax.ShapeDtypeStruct((n_t, NBLK, 2 * HALF), F32)],
        grid=(n_t,),
        in_specs=[pl.BlockSpec((tile, D), lambda i: (i, 0)), full(wb), full(wc), full(a_tab), full(dskip)],
        out_specs=[pl.BlockSpec((tile, D), lambda i: (i, 0)), pl.BlockSpec((1, NBLK, 2 * HALF), lambda i: (i, 0, 0))],
        scratch_shapes=[pltpu.VMEM((NBLK, 2 * HALF), F32), pltpu.VMEM((tile, 2 * HALF), F32)],
        name=name, compiler_params=_params(1))(h, wb, wc, a_tab, dskip)


def s5_scan_bwd(name, h, dy, s0, wb, wc, a_tab, dskip, tile=TILE_SCAN):
    s = h.shape[0]
    n_t = s // tile
    n8 = tile // 8

    def kern(h_ref, dy_ref, s0_ref, wb_ref, wc_ref, a_ref, d_ref, dh_ref, dwb_ref, dwc_ref, da_ref, dd_ref,
             lam_ref, sbuf, gbuf):
        i = pl.program_id(0)

        @pl.when(i == 0)
        def _():
            lam_ref[...] = jnp.zeros(lam_ref.shape, F32)
            dwb_ref[...] = jnp.zeros(dwb_ref.shape, F32)
            dwc_ref[...] = jnp.zeros(dwc_ref.shape, F32)
            da_ref[...] = jnp.zeros(da_ref.shape, F32)
            dd_ref[...] = jnp.zeros(dd_ref.shape, F32)

        for k in range(NBLK):
            cols = slice(GB * P * k, GB * P * (k + 1))
            u = h_ref[:, cols]
            dyk = dy_ref[:, cols]
            ar = a_ref[k, :, 0:HALF]
            ai = a_ref[k, :, HALF:2 * HALF]
            sbuf[0:8, :] = jnp.broadcast_to(s0_ref[0, k:k + 1, :], (8, 2 * HALF))
            sbuf[8:tile + 8, :] = jnp.dot(u.astype(_MXU), wb_ref[k], preferred_element_type=F32)
            _fwd_scan_block(sbuf, 8, n8, ar, ai, s0_ref[0, k:k + 1, 0:HALF], s0_ref[0, k:k + 1, HALF:2 * HALF])
            dyb = dyk.astype(_MXU)
            gbuf[...] = lax.dot_general(dyb, wc_ref[k], (((1,), (1,)), ((), ())), preferred_element_type=F32)
            dwc_ref[k] += lax.dot_general(sbuf[8:tile + 8, :].astype(_MXU), dyb, (((0,), (0,)), ((), ())),
                                          preferred_element_type=F32)
            steps, carry_m = _scan_tables(ar, ai, True)
            row = lax.broadcasted_iota(jnp.int32, (8, HALF), 0)

            def body(jj, carry):
                cr, ci, dar, dai = carry
                j = n8 - 1 - jj
                r0 = pl.multiple_of(j * 8, 8)
                xr = gbuf[pl.ds(r0, 8), 0:HALF]
                xi = gbuf[pl.ds(r0, 8), HALF:2 * HALF]
                xr, xi = _tile_scan_rev(xr, xi, cr, ci, steps, carry_m)
                gbuf[pl.ds(r0, 8), 0:HALF] = xr
                gbuf[pl.ds(r0, 8), HALF:2 * HALF] = xi
                r1 = pl.multiple_of(j * 8 + 8, 8)
                spr = jnp.where(row == 0, sbuf[pl.ds(r0, 8), 0:HALF][7:8],
                                pltpu.roll(sbuf[pl.ds(r1, 8), 0:HALF], 1, 0))
                spi = jnp.where(row == 0, sbuf[pl.ds(r0, 8), HALF:2 * HALF][7:8],
                                pltpu.roll(sbuf[pl.ds(r1, 8), HALF:2 * HALF], 1, 0))
                dar = dar + xr * spr + xi * spi
                dai = dai + xi * spr - xr * spi
                return xr[0:1], xi[0:1], dar, dai

            z8 = jnp.zeros((8, HALF), F32)
            cr, ci, dar, dai = lax.fori_loop(
                0, n8, body, (lam_ref[k:k + 1, 0:HALF], lam_ref[k:k + 1, HALF:2 * HALF], z8, z8))
            lam_ref[k:k + 1, 0:HALF] = cr
            lam_ref[k:k + 1, HALF:2 * HALF] = ci
            da_ref[k:k + 1, 0:HALF] += jnp.sum(dar, axis=0, keepdims=True)
            da_ref[k:k + 1, HALF:2 * HALF] += jnp.sum(dai, axis=0, keepdims=True)
            lam = gbuf[...].astype(_MXU)
            dwb_ref[k] += lax.dot_general(u.astype(_MXU), lam, (((0,), (0,)), ((), ())), preferred_element_type=F32)
            du = lax.dot_general(lam, wb_ref[k], (((1,), (1,)), ((), ())), preferred_element_type=F32)
            dh_ref[:, cols] = du + d_ref[:, cols] * dyk
            dd_ref[:, cols] += jnp.sum(dyk * u, axis=0, keepdims=True)

    full = lambda a: pl.BlockSpec(a.shape, functools.partial(lambda i, nd_: (0,) * nd_, nd_=a.ndim))
    fullo = lambda shp: pl.BlockSpec(shp, functools.partial(lambda i, nd_: (0,) * nd_, nd_=len(shp)))
    rev = lambda i: (n_t - 1 - i, 0)
    return pl.pallas_call(
        kern,
        out_shape=[jax.ShapeDtypeStruct((s, D), F32), jax.ShapeDtypeStruct(wb.shape, F32),
                   jax.ShapeDtypeStruct(wc.shape, F32), jax.ShapeDtypeStruct((NBLK, 2 * HALF), F32),
                   jax.ShapeDtypeStruct((1, D), F32)],
        grid=(n_t,),
        in_specs=[pl.BlockSpec((tile, D), rev), pl.BlockSpec((tile, D), rev),
                  pl.BlockSpec((1, NBLK, 2 * HALF), lambda i: (n_t - 1 - i, 0, 0)),
                  full(wb), full(wc), full(a_tab), full(dskip)],
        out_specs=[pl.BlockSpec((tile, D), rev), fullo(wb.shape), fullo(wc.shape), fullo((NBLK, 2 * HALF)),
                   fullo((1, D))],
        scratch_shapes=[pltpu.VMEM((NBLK, 2 * HALF), F32), pltpu.VMEM((tile + 8, 2 * HALF), F32),
                        pltpu.VMEM((tile, 2 * HALF), F32)],
        name=name, compiler_params=_params(1))(h, dy, s0, wb, wc, a_tab, dskip)


def _chunk_mask(q0, k0, tq, tk):
    r = (q0 + lax.broadcasted_iota(jnp.int32, (tq, tk), 0)) // CHUNK
    c = (k0 + lax.broadcasted_iota(jnp.int32, (tq, tk), 1)) // CHUNK
    return r >= c


def _head_lanes(j):
    lane = _lane(2 * DV)
    return (lane >= DV * j) & (lane < DV * (j + 1))


def _raw_scores(q, kblk, masked, t):
    s = lax.dot_general(q, kblk, (((1,), (1,)), ((), ())), preferred_element_type=F32)
    return jnp.where(_chunk_mask(0, 0, t, t), s, -1e30) if masked else s


def attn_fwd(name, q, k, v, t=TILE_ATT, tk=TILE_ATT):
    s = q.shape[0]
    n_q = s // t
    r = t // tk

    def kern(q_ref, k_ref, v_ref, o_ref, lse_ref):
        qi = pl.program_id(1)
        qs = [q_ref[:, HD * j:HD * (j + 1)] for j in range(2)]

        def absorb(k0, carry, mask):
            vblk = v_ref[pl.ds(k0, tk), :]
            scs = [lax.dot_general(qs[j], k_ref[pl.ds(k0, tk), HD * j:HD * (j + 1)], (((1,), (1,)), ((), ())),
                                   preferred_element_type=F32) for j in range(2)]
            if mask is not None:
                scs = [jnp.where(mask, sc, -1e30) for sc in scs]
            m_new = [jnp.maximum(carry[j][0], jnp.max(scs[j], axis=-1, keepdims=True)) for j in range(2)]
            ps = [jnp.exp2((scs[j] - m_new[j]) * EXP2_SCALE) for j in range(2)]
            alphas = [jnp.exp2((carry[j][0] - m_new[j]) * EXP2_SCALE) for j in range(2)]
            pvs = [jnp.dot(ps[j].astype(_MXU), vblk, preferred_element_type=F32) for j in range(2)]
            return tuple((m_new[j], alphas[j] * carry[j][1] + jnp.sum(ps[j], axis=-1, keepdims=True),
                          alphas[j] * carry[j][2] + pvs[j]) for j in range(2))

        init = tuple((jnp.full((t, 1), -1e30, F32), jnp.zeros((t, 1), F32), jnp.zeros((t, 2 * DV), F32))
                     for _ in range(2))
        carry = lax.fori_loop(0, qi * r, lambda kb, c: absorb(pl.multiple_of(kb * tk, tk), c, None), init)
        for i in range(r):
            carry = absorb(pl.multiple_of(qi * t + i * tk, tk), carry, _chunk_mask(0, i * tk, t, tk))
        outs = []
        for j in range(2):
            m, l, acc = carry[j]
            outs.append(acc / l)
            lse_ref[0, j] = m * ATTN_SCALE + jnp.log(l)
        o_ref[...] = jnp.where(_head_lanes(0), outs[0], outs[1])

    return pl.pallas_call(
        kern,
        out_shape=[jax.ShapeDtypeStruct((s, H * DV), F32), jax.ShapeDtypeStruct((HP, 2, s, 1), F32)],
        grid=(HP, n_q),
        in_specs=[pl.BlockSpec((t, 2 * HD), lambda hp, i: (i, hp)), pl.BlockSpec((s, 2 * HD), lambda hp, i: (0, hp)),
                  pl.BlockSpec((s, 2 * DV), lambda hp, i: (0, hp))],
        out_specs=[pl.BlockSpec((t, 2 * DV), lambda hp, i: (i, hp)),
                   pl.BlockSpec((1, 2, t, 1), lambda hp, i: (hp, 0, i, 0))],
        name=name, compiler_params=_params(2))(q, k, v)


def attn_bwd(name, q, k, v, o, do, lse, t=TILE_ATT):
    s = q.shape[0]
    n_q = s // t

    def kern(q_ref, k_ref, v_ref, o_ref, do_ref, lse_ref, dq_ref, dk_ref, dv_ref):
        qi = pl.program_id(1)

        @pl.when(qi == 0)
        def _():
            dk_ref[...] = jnp.zeros(dk_ref.shape, F32)
            dv_ref[...] = jnp.zeros(dv_ref.shape, F32)

        qs, doms, deltas, lse2 = [], [], [], []
        for j in range(2):
            qs.append(q_ref[:, HD * j:HD * (j + 1)])
            dom = jnp.where(_head_lanes(j), do_ref[...], 0.0)
            deltas.append(jnp.sum(dom * o_ref[...], axis=-1, keepdims=True))
            doms.append(dom.astype(_MXU))
            lse2.append(lse_ref[0, j] * LOG2E)

        def block(k0, dqs, masked):
            vblk = v_ref[pl.ds(k0, t), :]
            kblks = [k_ref[pl.ds(k0, t), HD * j:HD * (j + 1)] for j in range(2)]
            scs = [_raw_scores(qs[j], kblks[j], masked, t) for j in range(2)]
            dps = [lax.dot_general(doms[j], vblk, (((1,), (1,)), ((), ())), preferred_element_type=F32)
                   for j in range(2)]
            ps = [jnp.exp2(scs[j] * EXP2_SCALE - lse2[j]) for j in range(2)]
            dss = [(ps[j] * (dps[j] - deltas[j])).astype(_MXU) for j in range(2)]
            pbs = [ps[j].astype(_MXU) for j in range(2)]
            new = tuple(dqs[j] + jnp.dot(dss[j], kblks[j], preferred_element_type=F32) for j in range(2))
            for j in range(2):
                dk_ref[pl.ds(k0, t), HD * j:HD * (j + 1)] += lax.dot_general(
                    dss[j], qs[j], (((0,), (0,)), ((), ())), preferred_element_type=F32)
            dvs = [lax.dot_general(pbs[j], doms[j], (((0,), (0,)), ((), ())), preferred_element_type=F32)
                   for j in range(2)]
            dv_ref[pl.ds(k0, t), :] += dvs[0] + dvs[1]
            return new

        init = (jnp.zeros((t, HD), F32), jnp.zeros((t, HD), F32))
        dqs = lax.fori_loop(0, qi, lambda kb, c: block(pl.multiple_of(kb * t, t), c, False), init)
        dqs = block(pl.multiple_of(qi * t, t), dqs, True)
        for j in range(2):
            dq_ref[:, HD * j:HD * (j + 1)] = dqs[j] * ATTN_SCALE

        @pl.when(qi == n_q - 1)
        def _():
            dk_ref[...] = dk_ref[...] * ATTN_SCALE

    return pl.pallas_call(
        kern,
        out_shape=[jax.ShapeDtypeStruct((s, H * HD), F32), jax.ShapeDtypeStruct((s, H * HD), F32),
                   jax.ShapeDtypeStruct((s, H * DV), F32)],
        grid=(HP, n_q),
        in_specs=[pl.BlockSpec((t, 2 * HD), lambda hp, i: (i, hp)), pl.BlockSpec((s, 2 * HD), lambda hp, i: (0, hp)),
                  pl.BlockSpec((s, 2 * DV), lambda hp, i: (0, hp)), pl.BlockSpec((t, 2 * DV), lambda hp, i: (i, hp)),
                  pl.BlockSpec((t, 2 * DV), lambda hp, i: (i, hp)),
                  pl.BlockSpec((1, 2, t, 1), lambda hp, i: (hp, 0, i, 0))],
        out_specs=[pl.BlockSpec((t, 2 * HD), lambda hp, i: (i, hp)), pl.BlockSpec((s, 2 * HD), lambda hp, i: (0, hp)),
                   pl.BlockSpec((s, 2 * DV), lambda hp, i: (0, hp))],
        name=name, compiler_params=_params(2))(q, k, v, o, do, lse)


def rope_tables(name, pos_col, inv128):
    s = pos_col.shape[0]

    def kern(p_ref, inv_ref, c_ref, s_ref):
        ang = p_ref[...].astype(F32) * inv_ref[...]
        lane = _lane()
        m_r = (lane >= DN) & (lane < DN + DR)
        c_ref[...] = jnp.where(lane < DN, 1.0, jnp.where(m_r, jnp.cos(ang), 0.0))
        s_ref[...] = jnp.where(m_r, jnp.sin(ang), 0.0)

    return _whole(kern, name, [jax.ShapeDtypeStruct((s, HD), F32)] * 2, pos_col, inv128)


def loss_kernel(name, y, tgt, tile=TILE_ROW):
    def body(row_v, _):
        err = row_v[0] - row_v[1]
        part = 0.5 * jnp.sum(jnp.mean(err * err, axis=-1, keepdims=True), axis=0, keepdims=True)
        return [err * (1.0 / D)], [jnp.broadcast_to(part, (1, 128))]

    return _row_call(name, body, [y, tgt], [], [(D, F32)], [((1, 128), F32)], tile)


def _row_tile(r, c):
    cap = max(8, (1 << 18) // max(c, 1))
    for t in (2048, 1024, 512, 256, 128, 64, 32, 16, 8):
        if t <= cap and r % t == 0:
            return t
    return r


def sum_parts(name, parts):
    n, r, c = parts.shape
    t = _row_tile(r, c)

    def kern(p_ref, o_ref):
        acc = p_ref[0].astype(F32)
        for i in range(1, n):
            acc = acc + p_ref[i].astype(F32)
        o_ref[...] = acc

    return pl.pallas_call(kern, out_shape=jax.ShapeDtypeStruct((r, c), F32), grid=(r // t,),
                          in_specs=[pl.BlockSpec((n, t, c), lambda i: (0, i, 0))],
                          out_specs=pl.BlockSpec((t, c), lambda i: (i, 0)), name=name, compiler_params=_params(1))(parts)


def adamw(name, parts, w, m, v, base=0, stride=0):
    n, _, cp = parts.shape
    nl, r, c = w.shape
    t = _row_tile(math.gcd(math.gcd(r, base), stride), max(c, cp))
    c1 = 1.0 / (1.0 - ADAM_B1 ** ADAM_STEP)
    c2 = 1.0 / (1.0 - ADAM_B2 ** ADAM_STEP)

    def kern(p_ref, w_ref, m_ref, v_ref, g_ref, d_ref, nm_ref, nv_ref):
        g = p_ref[0].astype(F32)
        for i in range(1, n):
            g = g + p_ref[i].astype(F32)
        g = g[:, :c]
        nm = ADAM_B1 * m_ref[...] + (1.0 - ADAM_B1) * g
        nv = ADAM_B2 * v_ref[...] + (1.0 - ADAM_B2) * (g * g)
        g_ref[...] = g
        nm_ref[...] = nm
        nv_ref[...] = nv
        d_ref[...] = -ADAM_LR * ((nm * c1) / (jnp.sqrt(nv * c2) + ADAM_EPS) + ADAM_WD * w_ref[...])

    spec = pl.BlockSpec((None, t, c), lambda l, i: (l, i, 0))
    pspec = pl.BlockSpec((n, t, cp), lambda l, i: (0, (base + l * stride) // t + i, 0))
    return pl.pallas_call(kern, out_shape=[jax.ShapeDtypeStruct((nl, r, c), F32)] * 4, grid=(nl, r // t),
                          in_specs=[pspec, spec, spec, spec], out_specs=[spec] * 4, name=name,
                          compiler_params=_params(2))(parts, w, m, v)


def adamw_multi(name, parts_list, w, m, v):
    nl, r, c = w.shape
    n, _, cp = parts_list[0].shape
    t = _row_tile(r, max(c, cp))
    c1 = 1.0 / (1.0 - ADAM_B1 ** ADAM_STEP)
    c2 = 1.0 / (1.0 - ADAM_B2 ** ADAM_STEP)

    def kern(*refs):
        p_refs = refs[:nl]
        w_ref, m_ref, v_ref, g_ref, d_ref, nm_ref, nv_ref = refs[nl:]
        layer = pl.program_id(0)
        for ll in range(nl):
            @pl.when(layer == ll)
            def _(ll=ll):
                g = p_refs[ll][0].astype(F32)
                for i in range(1, n):
                    g = g + p_refs[ll][i].astype(F32)
                g = g[:, :c]
                nm = ADAM_B1 * m_ref[...] + (1.0 - ADAM_B1) * g
                nv = ADAM_B2 * v_ref[...] + (1.0 - ADAM_B2) * (g * g)
                g_ref[...] = g
                nm_ref[...] = nm
                nv_ref[...] = nv
                d_ref[...] = -ADAM_LR * ((nm * c1) / (jnp.sqrt(nv * c2) + ADAM_EPS) + ADAM_WD * w_ref[...])

    spec = pl.BlockSpec((None, t, c), lambda l, i: (l, i, 0))
    pspecs = [pl.BlockSpec((n, t, cp), functools.partial(lambda l, i, ll_: (0, jnp.where(l == ll_, i, 0), 0), ll_=ll))
              for ll in range(nl)]
    return pl.pallas_call(kern, out_shape=[jax.ShapeDtypeStruct((nl, r, c), F32)] * 4, grid=(nl, r // t),
                          in_specs=pspecs + [spec, spec, spec], out_specs=[spec] * 4, name=name,
                          compiler_params=_params(2))(*parts_list, w, m, v)


def adamw_layer(name, parts, w, m, v, layer, prev, base=0):
    n, _, cp = parts.shape
    nl, r, c = w.shape
    t = _row_tile(math.gcd(r, base), max(c, cp))
    c1 = 1.0 / (1.0 - ADAM_B1 ** ADAM_STEP)
    c2 = 1.0 / (1.0 - ADAM_B2 ** ADAM_STEP)
    chained = nl > 1

    def kern(p_ref, w_ref, m_ref, v_ref, *rest):
        g_ref, d_ref, nm_ref, nv_ref = rest[-4:]
        g = p_ref[0].astype(F32)
        for i in range(1, n):
            g = g + p_ref[i].astype(F32)
        g = g[:, :c]
        nm = ADAM_B1 * m_ref[...] + (1.0 - ADAM_B1) * g
        nv = ADAM_B2 * v_ref[...] + (1.0 - ADAM_B2) * (g * g)
        g_ref[...] = g
        nm_ref[...] = nm
        nv_ref[...] = nv
        d_ref[...] = -ADAM_LR * ((nm * c1) / (jnp.sqrt(nv * c2) + ADAM_EPS) + ADAM_WD * w_ref[...])

    spec = pl.BlockSpec((None, t, c), lambda i: (layer, i, 0))
    pspec = pl.BlockSpec((n, t, cp), lambda i: (0, base // t + i, 0))
    in_specs = [pspec, spec, spec, spec]
    args = [parts, w, m, v]
    aliases = {}
    if chained:
        if prev is None:
            prev = [lax.empty((nl, r, c), F32) for _ in range(4)]
        in_specs += [pl.BlockSpec(memory_space=pl.ANY)] * 4
        args += list(prev)
        aliases = {4 + i: i for i in range(4)}
    return pl.pallas_call(kern, out_shape=[jax.ShapeDtypeStruct((nl, r, c), F32)] * 4, grid=(r // t,),
                          in_specs=in_specs, out_specs=[spec] * 4, input_output_aliases=aliases, name=name,
                          compiler_params=_params(1))(*args)


def _me():
    return lax.axis_index("x"), lax.axis_index("y"), lax.axis_index("c")


def _flip(x, y, c, mask):
    return (jnp.where((mask >> 2) & 1, 1 - x, x), jnp.where((mask >> 1) & 1, 1 - y, y), jnp.where(mask & 1, 1 - c, c))


def _index(x, y, c):
    return 4 * x + 2 * y + c


def _exchange(name, arr, gather):
    out_shape = (N_DEV,) + arr.shape if gather else arr.shape

    def kern(in_ref, out_ref, send_sems, recv_sems, local_sem):
        x, y, c = _me()
        me = _index(x, y, c)
        mine = pltpu.make_async_copy(in_ref if gather else in_ref.at[me], out_ref.at[me], local_sem)
        mine.start()
        copies = []
        for mask in range(1, N_DEV):
            px, py, pc = _flip(x, y, c, mask)
            peer = _index(px, py, pc)
            cp = pltpu.make_async_remote_copy(
                src_ref=in_ref if gather else in_ref.at[peer], dst_ref=out_ref.at[me],
                send_sem=send_sems.at[mask - 1], recv_sem=recv_sems.at[mask - 1],
                device_id=(px, py, pc), device_id_type=MESH)
            cp.start()
            copies.append((cp, peer))
        for mask, (cp, peer) in enumerate(copies, start=1):
            pltpu.make_async_remote_copy(
                src_ref=in_ref if gather else in_ref.at[peer], dst_ref=out_ref.at[peer],
                send_sem=send_sems.at[mask - 1], recv_sem=recv_sems.at[mask - 1],
                device_id=_flip(x, y, c, mask), device_id_type=MESH).wait_recv()
        for cp, _ in copies:
            cp.wait_send()
        mine.wait()

    any_spec = pl.BlockSpec(memory_space=pl.ANY)
    return pl.pallas_call(
        kern, out_shape=jax.ShapeDtypeStruct(out_shape, arr.dtype), in_specs=[any_spec], out_specs=any_spec,
        scratch_shapes=[pltpu.SemaphoreType.DMA((N_DEV - 1,)), pltpu.SemaphoreType.DMA((N_DEV - 1,)),
                        pltpu.SemaphoreType.DMA],
        name=name, compiler_params=pltpu.CompilerParams(has_side_effects=True))(arr)


def all_gather(name, arr):
    return _exchange(name, arr, True)


_HBM = pl.BlockSpec(memory_space=pltpu.HBM)
_SEM = pl.BlockSpec(memory_space=pltpu.SEMAPHORE)
_EFFECT = pltpu.SideEffectType.DATAFLOW_SIDE_EFFECTING


def _split_copies(srcs, lands, send_sems, recv_sems, gather):
    x, y, c = _me()
    me = _index(x, y, c)
    out = []
    for a, (src, land) in enumerate(zip(srcs, lands)):
        for mask in range(1, N_DEV):
            px, py, pc = _flip(x, y, c, mask)
            peer = _index(px, py, pc)
            sem = (N_DEV - 1) * a + mask - 1
            mk = lambda dst_slot: pltpu.make_async_remote_copy(
                src_ref=src if gather else src.at[peer], dst_ref=land.at[dst_slot],
                send_sem=send_sems.at[sem], recv_sem=recv_sems.at[sem], device_id=(px, py, pc), device_id_type=MESH)
            out.append((mk(me), mk(peer)))
    return out


def exchange_start(name, arrs, gather, after):
    k = len(arrs)
    land_shapes = [((N_DEV,) + a.shape if gather else a.shape) for a in arrs]

    def body(*refs):
        srcs, lands = refs[:k], refs[k:2 * k]
        send_sems, recv_sems = refs[2 * k + 1], refs[2 * k + 2]
        token = refs[-1]
        for mine, _ in _split_copies(srcs, lands, send_sems, recv_sems, gather):
            mine.start()
        token[...] = jnp.zeros(token.shape, token.dtype)

    n_sem = (N_DEV - 1) * k
    res = pl.pallas_call(
        body, name=name,
        out_shape=(pltpu.SemaphoreType.DMA((n_sem,)), pltpu.SemaphoreType.DMA((n_sem,)),
                   *[pltpu.HBM(a.shape, a.dtype) for a in arrs],
                   *[pltpu.HBM(shp, a.dtype) for shp, a in zip(land_shapes, arrs)],
                   jax.ShapeDtypeStruct((8, 128), F32)),
        in_specs=[_HBM] * (2 * k) + [pl.BlockSpec(memory_space=pl.ANY)],
        out_specs=(_SEM, _SEM, *[_HBM] * (2 * k), pl.BlockSpec(memory_space=pltpu.VMEM)),
        input_output_aliases={i: 2 + i for i in range(2 * k)},
        compiler_params=pltpu.CompilerParams(has_side_effects=_EFFECT),
    )(*[pltpu.with_memory_space_constraint(a, pltpu.HBM) for a in arrs],
      *[pltpu.with_memory_space_constraint(lax.empty(shp, a.dtype), pltpu.HBM) for shp, a in zip(land_shapes, arrs)],
      after)
    return res[0], res[1], list(res[2:2 + k]), list(res[2 + k:2 + 2 * k]), res[-1]


def exchange_wait(name, started, after, gather):
    send_sems, recv_sems, thrus, lands, _ = started
    k = len(thrus)

    def body(*refs):
        srcs, lnds = refs[:k], refs[k:2 * k]
        s_sems, r_sems = refs[2 * k], refs[2 * k + 1]
        for mine, theirs in _split_copies(srcs, lnds, s_sems, r_sems, gather):
            mine.wait_send()
            theirs.wait_recv()

    res = pl.pallas_call(
        body, name=name,
        out_shape=tuple(pltpu.HBM(a.shape, a.dtype) for a in thrus + lands),
        in_specs=[_HBM] * (2 * k) + [_SEM, _SEM, pl.BlockSpec(memory_space=pl.ANY)], out_specs=tuple([_HBM] * (2 * k)),
        input_output_aliases={i: i for i in range(2 * k)},
        compiler_params=pltpu.CompilerParams(has_side_effects=_EFFECT),
    )(*thrus, *lands, send_sems, recv_sems, after)
    return list(res[k:])


def _pad_heads(w, real, padded):
    k = w.shape[0]
    w3 = w.reshape(k, H, real)
    return jnp.pad(w3, ((0, 0), (0, 0), (0, padded - real))).reshape(k, H * padded)


def _unpad_heads(w, real, padded):
    k = w.shape[0]
    return w.reshape(k, H, padded)[:, :, :real].reshape(k, H * real)


def _s5_place(ab_re, ab_im, bb_re_t, bb_im_t, c_re, c_im):
    eye = jnp.eye(GB, dtype=F32)

    def wb_part(bt):
        x4 = bt.reshape(P, NBLK, GB, N).transpose(1, 2, 0, 3)
        return jnp.einsum('kgpn,gh->kgphn', x4, eye).reshape(NBLK, GB * P, HALF)

    def wc_part(cc):
        x4 = cc.reshape(NBLK, GB, P, N)
        return jnp.einsum('kgpn,gh->kgnhp', x4, eye).reshape(NBLK, HALF, GB * P)

    wb = jnp.concatenate([wb_part(bb_re_t), wb_part(bb_im_t)], axis=-1)
    wc = jnp.concatenate([wc_part(c_re), -wc_part(c_im)], axis=1)
    a_tab = jnp.concatenate([ab_re.reshape(NBLK, 1, HALF), ab_im.reshape(NBLK, 1, HALF)], axis=-1)
    return wb.astype(_MXU), wc.astype(_MXU), a_tab


def _s5_unplace(dwb, dwc, da):
    eye = jnp.eye(GB, dtype=F32)

    def wb_part(dpart):
        x5 = dpart.reshape(NBLK, GB, P, GB, N)
        return jnp.einsum('kgphn,gh->kgpn', x5, eye).transpose(2, 0, 1, 3).reshape(P, G * N)

    def wc_part(dpart):
        x5 = dpart.reshape(NBLK, GB, N, GB, P)
        return jnp.einsum('kgnhp,gh->kgpn', x5, eye).reshape(G, P, N)

    dbb_re_t, dbb_im_t = wb_part(dwb[..., :HALF]), wb_part(dwb[..., HALF:])
    dc_re, dc_im = wc_part(dwc[:, :HALF]), -wc_part(dwc[:, HALF:])
    dab_re, dab_im = da[:, :HALF].reshape(1, G * N), da[:, HALF:].reshape(1, G * N)
    return dab_re, dab_im, dbb_re_t, dbb_im_t, dc_re, dc_im


def _row(v):
    return v.reshape(1, -1)


def _pack_rows(pieces):
    flat = jnp.concatenate(pieces)
    n = int(flat.shape[0])
    padded = -(-n // 65536) * 65536
    return jnp.pad(flat, (0, padded - n)).reshape(padded // 128, 128)


def kernel(x, c, positions, ada_w, ada_b, norm1_g, norm2_g, ffn_w_gate, ffn_w_up, ffn_w_down, s5_lam_re, s5_lam_im, s5_log_dt, s5_b_re, s5_b_im, s5_c_re, s5_c_im, s5_d, s5_w_glu, s5_b_glu, kv_ada_w, kv_ada_b, kv_norm_g, w_kv_a, kv_a_norm_g, w_kv_b, k_nope_norm_g, k_rope_norm_g, mla_w_dq, mla_q_norm_g, mla_w_uq, mla_q_nope_norm_g, mla_q_rope_norm_g, mla_w_o, loss_target, m_ada_w, m_ada_b, m_norm1_g, m_norm2_g, m_ffn_w_gate, m_ffn_w_up, m_ffn_w_down, m_s5_lam_re, m_s5_lam_im, m_s5_log_dt, m_s5_b_re, m_s5_b_im, m_s5_c_re, m_s5_c_im, m_s5_d, m_s5_w_glu, m_s5_b_glu, m_kv_ada_w, m_kv_ada_b, m_kv_norm_g, m_w_kv_a, m_kv_a_norm_g, m_w_kv_b, m_k_nope_norm_g, m_k_rope_norm_g, m_mla_w_dq, m_mla_q_norm_g, m_mla_w_uq, m_mla_q_nope_norm_g, m_mla_q_rope_norm_g, m_mla_w_o, v_ada_w, v_ada_b, v_norm1_g, v_norm2_g, v_ffn_w_gate, v_ffn_w_up, v_ffn_w_down, v_s5_lam_re, v_s5_lam_im, v_s5_log_dt, v_s5_b_re, v_s5_b_im, v_s5_c_re, v_s5_c_im, v_s5_d, v_s5_w_glu, v_s5_b_glu, v_kv_ada_w, v_kv_ada_b, v_kv_norm_g, v_w_kv_a, v_kv_a_norm_g, v_w_kv_b, v_k_nope_norm_g, v_k_rope_norm_g, v_mla_w_dq, v_mla_q_norm_g, v_mla_w_uq, v_mla_q_nope_norm_g, v_mla_q_rope_norm_g, v_mla_w_o):
    W = dict(ada_w=ada_w, ada_b=ada_b, norm1_g=norm1_g, norm2_g=norm2_g, ffn_w_gate=ffn_w_gate, ffn_w_up=ffn_w_up, ffn_w_down=ffn_w_down, s5_lam_re=s5_lam_re, s5_lam_im=s5_lam_im, s5_log_dt=s5_log_dt, s5_b_re=s5_b_re, s5_b_im=s5_b_im, s5_c_re=s5_c_re, s5_c_im=s5_c_im, s5_d=s5_d, s5_w_glu=s5_w_glu, s5_b_glu=s5_b_glu, kv_ada_w=kv_ada_w, kv_ada_b=kv_ada_b, kv_norm_g=kv_norm_g, w_kv_a=w_kv_a, kv_a_norm_g=kv_a_norm_g, w_kv_b=w_kv_b, k_nope_norm_g=k_nope_norm_g, k_rope_norm_g=k_rope_norm_g, mla_w_dq=mla_w_dq, mla_q_norm_g=mla_q_norm_g, mla_w_uq=mla_w_uq, mla_q_nope_norm_g=mla_q_nope_norm_g, mla_q_rope_norm_g=mla_q_rope_norm_g, mla_w_o=mla_w_o)
    M = dict(ada_w=m_ada_w, ada_b=m_ada_b, norm1_g=m_norm1_g, norm2_g=m_norm2_g, ffn_w_gate=m_ffn_w_gate, ffn_w_up=m_ffn_w_up, ffn_w_down=m_ffn_w_down, s5_lam_re=m_s5_lam_re, s5_lam_im=m_s5_lam_im, s5_log_dt=m_s5_log_dt, s5_b_re=m_s5_b_re, s5_b_im=m_s5_b_im, s5_c_re=m_s5_c_re, s5_c_im=m_s5_c_im, s5_d=m_s5_d, s5_w_glu=m_s5_w_glu, s5_b_glu=m_s5_b_glu, kv_ada_w=m_kv_ada_w, kv_ada_b=m_kv_ada_b, kv_norm_g=m_kv_norm_g, w_kv_a=m_w_kv_a, kv_a_norm_g=m_kv_a_norm_g, w_kv_b=m_w_kv_b, k_nope_norm_g=m_k_nope_norm_g, k_rope_norm_g=m_k_rope_norm_g, mla_w_dq=m_mla_w_dq, mla_q_norm_g=m_mla_q_norm_g, mla_w_uq=m_mla_w_uq, mla_q_nope_norm_g=m_mla_q_nope_norm_g, mla_q_rope_norm_g=m_mla_q_rope_norm_g, mla_w_o=m_mla_w_o)
    V = dict(ada_w=v_ada_w, ada_b=v_ada_b, norm1_g=v_norm1_g, norm2_g=v_norm2_g, ffn_w_gate=v_ffn_w_gate, ffn_w_up=v_ffn_w_up, ffn_w_down=v_ffn_w_down, s5_lam_re=v_s5_lam_re, s5_lam_im=v_s5_lam_im, s5_log_dt=v_s5_log_dt, s5_b_re=v_s5_b_re, s5_b_im=v_s5_b_im, s5_c_re=v_s5_c_re, s5_c_im=v_s5_c_im, s5_d=v_s5_d, s5_w_glu=v_s5_w_glu, s5_b_glu=v_s5_b_glu, kv_ada_w=v_kv_ada_w, kv_ada_b=v_kv_ada_b, kv_norm_g=v_kv_norm_g, w_kv_a=v_w_kv_a, kv_a_norm_g=v_kv_a_norm_g, w_kv_b=v_w_kv_b, k_nope_norm_g=v_k_nope_norm_g, k_rope_norm_g=v_k_rope_norm_g, mla_w_dq=v_mla_w_dq, mla_q_norm_g=v_mla_q_norm_g, mla_w_uq=v_mla_w_uq, mla_q_nope_norm_g=v_mla_q_nope_norm_g, mla_q_rope_norm_g=v_mla_q_rope_norm_g, mla_w_o=v_mla_w_o)
    return _step(x[0], c, positions, loss_target[0], W, M, V)


WEIGHT_NAMES = ['ada_w', 'ada_b', 'norm1_g', 'norm2_g', 'ffn_w_gate', 'ffn_w_up', 'ffn_w_down', 's5_lam_re', 's5_lam_im', 's5_log_dt', 's5_b_re', 's5_b_im', 's5_c_re', 's5_c_im', 's5_d', 's5_w_glu', 's5_b_glu', 'kv_ada_w', 'kv_ada_b', 'kv_norm_g', 'w_kv_a', 'kv_a_norm_g', 'w_kv_b', 'k_nope_norm_g', 'k_rope_norm_g', 'mla_w_dq', 'mla_q_norm_g', 'mla_w_uq', 'mla_q_nope_norm_g', 'mla_q_rope_norm_g', 'mla_w_o']
REPLICATED = ['ada_b', 'norm1_g', 'norm2_g', 's5_lam_re', 's5_lam_im', 's5_log_dt', 's5_b_re', 's5_b_im', 's5_c_re', 's5_c_im', 'kv_ada_b', 'kv_norm_g', 'kv_a_norm_g', 'k_nope_norm_g', 'k_rope_norm_g', 'mla_q_norm_g', 'mla_q_nope_norm_g', 'mla_q_rope_norm_g']
SHARDED_VEC = ['s5_d', 's5_b_glu']


def _step(x, c, positions, target, W, M, V):
    s = x.shape[0]
    me = _index(*_me())
    mxu = lambda a: a.astype(_MXU)

    pad_c = lambda a: jnp.pad(a, ((0, 0), (0, FFB - FF // N_DEV)))
    pad_r = lambda a: jnp.pad(a, ((0, FFB - FF // N_DEV), (0, 0)))
    cols = lambda g: g.transpose(1, 0, 2).reshape(g.shape[1], N_DEV * g.shape[2])
    rows = lambda g: g.reshape(N_DEV * g.shape[1], g.shape[2])

    def local_pack(l):
        second = W['s5_w_glu'][l] if l < N_A else W['mla_w_o'][l - N_A]
        arrs = [jnp.concatenate([mxu(pad_c(W['ffn_w_gate'][l])), mxu(pad_c(W['ffn_w_up'][l]))], axis=0),
                jnp.concatenate([mxu(pad_r(W['ffn_w_down'][l])), mxu(second)], axis=0)]
        if l == N_A:
            arrs += [jnp.concatenate([mxu(W['w_kv_b']), mxu(W['mla_w_dq'][0])], axis=0), mxu(W['w_kv_a'])]
        if l > N_A:
            arrs += [mxu(W['mla_w_dq'][l - N_A])]
        if l >= N_A:
            arrs += [mxu(W['mla_w_uq'][l - N_A])]
        return arrs


    def layer_weights(l, after):
        lands = exchange_wait(f"gather_wait_{l}", gathers[l], after, True)
        full = [lax.dynamic_update_slice(ld, src[None], (me,) + (0,) * src.ndim) for ld, src in zip(lands, gathers[l][2])]
        w = {'wg': cols(full[0][:, :D]), 'wu': cols(full[0][:, D:]), 'wd': rows(full[1][:, :FFB]),
             'second': rows(full[1][:, FFB:])}
        if l >= N_A:
            if l == N_A:
                wkvb3 = cols(full[2][:, :KVL]).reshape(KVL, H, DN + DV)
                wkva = rows(full[3])
                w['wa_pad'] = jnp.concatenate([wkva[:, :KVL], jnp.zeros((D, DN), _MXU), wkva[:, KVL:],
                                               jnp.zeros((D, HD - DN - DR), _MXU)], axis=1)
                w['wkn_pad'] = jnp.pad(wkvb3[:, :, :DN], ((0, 0), (0, 0), (0, HD - DN))).reshape(KVL, H * HD)
                w['wv'] = wkvb3[:, :, DN:].reshape(KVL, H * DV)
                w['wdq'] = rows(full[2][:, KVL:])
            else:
                w['wdq'] = rows(full[2])
            w['wuq_pad'] = _pad_heads(cols(full[-1]), DN + DR, HD)
        return w

    vec = jnp.concatenate([c.reshape(-1), W['s5_d'].reshape(-1), W['s5_b_glu'].reshape(-1)]).reshape(1, -1)
    vec = jnp.pad(vec, ((0, 7), (0, 0)))
    gv = all_gather("gather_vectors", vec)[:, 0, :]
    c_all = gv[:, :D]
    d_full = jnp.concatenate([gv[d, D:D + 2 * 128].reshape(N_A, 128) for d in range(N_DEV)], axis=1)
    bglu_full = jnp.concatenate([gv[d, D + 256:D + 512].reshape(N_A, 128) for d in range(N_DEV)], axis=1)

    ca_all = jax.nn.silu(c_all)
    w_mod = jnp.concatenate([W['ada_w'][l] for l in range(DEPTH)] + [W['kv_ada_w']], axis=1)
    n_mod = w_mod.shape[1]
    mod_cols = small_matmul("mod_matmul", ca_all, w_mod)
    gm = all_gather("gather_mod", mod_cols)
    gathers = [exchange_start(f"gather_start_{l}", local_pack(l), True, gm) for l in range(DEPTH)]
    tokens = sum(g[4][0, 0] for g in gathers)
    mine = lax.dynamic_index_in_dim(gm, me, axis=1, keepdims=False) + tokens
    per_l = D * 6 // N_DEV
    mods = []
    for l in range(DEPTH):
        full = jnp.concatenate([mine[d, per_l * l:per_l * (l + 1)] for d in range(N_DEV)]) + W['ada_b'][l]
        mods.append([_row(full[D * i:D * (i + 1)]) for i in range(6)])
    kfull = jnp.concatenate([mine[d, per_l * DEPTH:] for d in range(N_DEV)]) + W['kv_ada_b']
    k_shift, k_scale = _row(kfull[:D]), _row(kfull[D:])

    inv = 1.0 / (ROPE_THETA ** (np.arange(0, DR, 2, dtype=np.float32) / DR))
    inv128 = np.zeros((1, HD), np.float32)
    inv128[0, DN:DN + DR // 2] = inv
    inv128[0, DN + DR // 2:DN + DR] = inv
    cosf, sinf = rope_tables("rope_tables", positions.reshape(s, 1), jnp.asarray(inv128))
    zpad = lambda n: jnp.zeros((n,), F32)
    gkn128 = _row(jnp.concatenate([W['k_nope_norm_g'], zpad(HD - DN)]))
    gkr128 = _row(jnp.concatenate([zpad(DN), W['k_rope_norm_g'], zpad(HD - DN - DR)]))
    gq128 = [_row(jnp.concatenate([W['mla_q_nope_norm_g'][j], W['mla_q_rope_norm_g'][j], zpad(HD - DN - DR)]))
             for j in range(2)]

    expand = jnp.asarray(np.kron(np.eye(G, dtype=np.float32), np.ones((1, N), np.float32)))
    s5_raw, s5_mats = [], []
    for l in range(N_A):
        raw = (_row(W['s5_lam_re'][l]), _row(W['s5_lam_im'][l]), _row(W['s5_log_dt'][l]),
               W['s5_b_re'][l].transpose(2, 0, 1).reshape(P, G * N), W['s5_b_im'][l].transpose(2, 0, 1).reshape(P, G * N))
        ab_re, ab_im, bb_re_t, bb_im_t = s5_prep_fwd(f"s5_prep_fwd", *raw, expand)
        s5_raw.append(raw)
        s5_mats.append(_s5_place(ab_re, ab_im, bb_re_t, bb_im_t, W['s5_c_re'][l], W['s5_c_im'][l]))

    g1 = [_row(W['norm1_g'][l]) for l in range(DEPTH)]
    g2 = [_row(W['norm2_g'][l]) for l in range(DEPTH)]
    saved = []
    xs = x
    kv = None
    lw = [None] * DEPTH
    for l in range(DEPTH):
        sh1, sc1, gt1, sh2, sc2, gt2 = mods[l]
        rec = {'x_in': xs}
        if l >= N_A:
            lw[l] = layer_weights(l, xs)
        if l == N_A:
            kv_smalls = [_row(W['kv_norm_g']), k_shift, k_scale, _row(W['kv_a_norm_g']), gkn128, gkr128]
            kv_w = [lw[l]['wa_pad'], lw[l]['wkn_pad'], lw[l]['wv']]
            k_mat, v_mat = seg_forward("kv_fwd", seg_kv, [xs], kv_smalls, [cosf, sinf], kv_w,
                                       [(H * HD, _MXU), (H * DV, _MXU)], tap_widths=(KVL + HD, H * HD, H * DV))
            kv = {'x_in': xs, 'smalls': kv_smalls, 'k': k_mat, 'v': v_mat, 'w': kv_w}
        if l < N_A:
            (h,) = seg_forward("pre_fwd", seg_pre, [xs], [g1[l], sh1, sc1], [], [], [(D, F32)])
            wb, wc, a_tab = s5_mats[l]
            y, s0 = s5_scan_fwd("s5_scan_fwd", h, wb, wc, a_tab, _row(d_full[l]))
            lw[l] = layer_weights(l, y)
            (x_mid,) = seg_forward("glu_fwd", seg_glu, [xs, y], [gt1, _row(bglu_full[l])], [], [lw[l]['second']],
                                   [(D, F32)], tap_widths=(D,))
            rec.update(h=h, y=y, s0=s0)
        else:
            j = l - N_A
            q_smalls = [g1[l], sh1, sc1, _row(W['mla_q_norm_g'][j]), gq128[j]]
            (q_mat,) = seg_forward("q_fwd", seg_q, [xs], q_smalls, [cosf, sinf], [lw[l]['wdq'], lw[l]['wuq_pad']],
                                   [(H * HD, _MXU)], tap_widths=(QL, H * HD))
            o_mat, lse = attn_fwd("attn_fwd", q_mat, kv['k'], kv['v'])
            (x_mid,) = seg_forward("o_fwd", seg_o, [xs, o_mat], [gt1], [], [lw[l]['second']], [(D, F32)],
                                   tap_widths=(D,))
            rec.update(q=q_mat, o=o_mat, lse=lse, q_smalls=q_smalls)
        rec['x_mid'] = x_mid
        xs, rec['gate'], rec['up'] = ffn_forward("ffn_fwd", x_mid, g2[l], sh2, sc2, gt2,
                                                 lw[l]['wg'], lw[l]['wu'], lw[l]['wd'])
        saved.append(rec)

    dy, loss_part = loss_kernel("loss", xs, target)
    loss = lax.psum(loss_part[0, 0], ("x", "y", "c"))

    rblk = lambda a: a.reshape(N_DEV, a.shape[0] // N_DEV, a.shape[1])
    cblk = lambda a: a.reshape(a.shape[0], N_DEV, a.shape[1] // N_DEV).transpose(1, 0, 2)
    dmod = [None] * DEPTH
    dk_tot = []
    dv_tot = []
    dx = dy
    sends = [None] * DEPTH
    send_token = jnp.zeros((1, 1), F32)
    g_n1 = [None] * DEPTH
    g_n2 = [None] * DEPTH
    g_bglu = [None] * N_A
    g_dskip = [None] * N_A
    g_s5 = [None] * N_A
    g_qn, g_q128 = [None] * 2, [None] * 2
    for l in range(DEPTH - 1, -1, -1):
        rec = saved[l]
        sh1, sc1, gt1, sh2, sc2, gt2 = mods[l]
        dx, dgate, dup, dyd, h_b, a_b, dg2, dsh2, dsc2, dgt2 = ffn_backward(
            "ffn_bwd", rec['x_mid'], dx, rec['gate'], rec['up'], g2[l], sh2, sc2, gt2 + send_token,
            lw[l]['wg'], lw[l]['wu'], lw[l]['wd'])
        out_l = [matmul_tn("tn_ffn_in", h_b, dgate, _MXU, col_blocks=N_DEV),
                 matmul_tn("tn_ffn_in", h_b, dup, _MXU, col_blocks=N_DEV),
                 matmul_tn("tn_ffn_out", a_b, dyd, _MXU).reshape(N_DEV, FFB, D)]
        g_n2[l] = dg2
        if l == 0:
            sends_ffn0 = exchange_start("a2a_start_ffn0", out_l, False, dx)
            send_token = sends_ffn0[4][0:1, 0:1]
            out_l = []
        if l < N_A:
            (dx, dyy), (dz,), (g_b,), (dgt1, dbg) = seg_backward(
                "glu_bwd", seg_glu, [rec['x_in'], rec['y']], [gt1 + (send_token if l == 0 else 0.0), _row(bglu_full[l])], [],
                [lw[l]['second']],
                [dx], (D,), (D,))
            out_l.append(rblk(matmul_tn("tn_sq", g_b, dz, _MXU)))
            g_bglu[l] = dbg
            wb, wc, a_tab = s5_mats[l]
            dh, dwb, dwc, da, dd = s5_scan_bwd("s5_scan_bwd", rec['h'], dyy, rec['s0'], wb, wc, a_tab, _row(d_full[l]))
            g_dskip[l] = dd
            dab_re, dab_im, dbb_re_t, dbb_im_t, dc_re, dc_im = _s5_unplace(dwb, dwc, da)
            dlr, dli, dldt, dbr_t, dbi_t = s5_prep_bwd("s5_prep_bwd", *s5_raw[l], expand,
                                                       (dab_re, dab_im, dbb_re_t, dbb_im_t))
            g_s5[l] = (dlr.reshape(G, N), dli.reshape(G, N), dldt.reshape(G),
                       dbr_t.reshape(P, G, N).transpose(1, 2, 0), dbi_t.reshape(P, G, N).transpose(1, 2, 0), dc_re, dc_im)
            (dx,), _, _, (dg1, dsh1, dsc1) = seg_backward(
                "pre_bwd", seg_pre, [rec['x_in']], [g1[l], sh1, sc1], [], [], [dh], (), (), dx_add=dx)
        else:
            j = l - N_A
            (dx, do), (dzo,), (o_b,), (dgt1,) = seg_backward(
                "o_bwd", seg_o, [rec['x_in'], rec['o']], [gt1], [], [lw[l]['second']], [dx], (D,), (D,))
            out_l.append(rblk(matmul_tn("tn_sq", o_b, dzo, _MXU)))
            dq, dk, dv = attn_bwd("attn_bwd", rec['q'], kv['k'], kv['v'], rec['o'], do, rec['lse'])
            dk_tot.append(dk)
            dv_tot.append(dv)
            (dx,), (dql, dqq), (hq_b, qn_b), (dg1, dsh1, dsc1, dqg, dq128) = seg_backward(
                "q_bwd", seg_q, [rec['x_in']], rec['q_smalls'], [cosf, sinf], [lw[l]['wdq'], lw[l]['wuq_pad']],
                [dq], (QL, H * HD), (D, QL), dx_add=dx)
            g_dq = rblk(matmul_tn("tn_dq", hq_b, dql, _MXU))
            g_uq = cblk(_unpad_heads(matmul_tn("tn_uq", qn_b, dqq, _MXU), DN + DR, HD))
            g_qn[j], g_q128[j] = dqg, dq128
        g_n1[l] = dg1
        dmod[l] = jnp.concatenate([dsh1, dsc1, dgt1, dsh2, dsc2, dgt2], axis=1)
        if l == N_A:
            (dx,), (dta, dtk, dtv), (hk_b, ckv_b), (dkg, dksh, dksc, dag, dgkn, dgkr) = seg_backward(
                "kv_bwd", seg_kv, [kv['x_in']], kv['smalls'], [cosf, sinf], kv['w'],
                [dk_tot, dv_tot], (KVL + HD, H * HD, H * DV), (D, KVL), dx_add=dx)
            g_wa = matmul_tn("tn_kva", hk_b, dta, _MXU)
            g_wa = jnp.concatenate([g_wa[:, :KVL], g_wa[:, KVL + DN:KVL + DN + DR]], axis=1)
            g_kn = matmul_tn("tn_kn", ckv_b, dtk, _MXU).reshape(KVL, H, HD)[:, :, :DN]
            g_v = matmul_tn("tn_v", ckv_b, dtv, _MXU).reshape(KVL, H, DV)
            g_wkvb = jnp.concatenate([g_kn, g_v], axis=2).reshape(KVL, H * (DN + DV))
            dkmod = jnp.concatenate([dksh, dksc], axis=1)
            out_l += [jnp.concatenate([cblk(g_wkvb), g_dq], axis=1), rblk(g_wa)]
        if l > N_A:
            out_l.append(g_dq)
        if l >= N_A:
            out_l.append(g_uq)
        if l > 0:
            sends[l] = exchange_start(f"a2a_start_{l}", out_l, False, dx)
            send_token = sends[l][4][0:1, 0:1]
        if l == N_A - 1:
            early_flat = _pack_rows([a.reshape(-1) for a in g_s5[l]])
            early_st = exchange_start("small_start_s5", [early_flat], True, dx)
            send_token = send_token + early_st[4][0:1, 0:1]
    grad_x = dx

    s5_names = ['s5_lam_re', 's5_lam_im', 's5_log_dt', 's5_b_re', 's5_b_im', 's5_c_re', 's5_c_im']
    small = {
        'norm1_g': jnp.concatenate(g_n1, axis=0), 'norm2_g': jnp.concatenate(g_n2, axis=0),
        'kv_norm_g': dkg, 'kv_a_norm_g': dag, 'k_nope_norm_g': dgkn[:, :DN], 'k_rope_norm_g': dgkr[:, DN:DN + DR],
        'mla_q_norm_g': jnp.concatenate(g_qn, axis=0),
        'mla_q_nope_norm_g': jnp.concatenate([g[:, :DN] for g in g_q128], axis=0),
        'mla_q_rope_norm_g': jnp.concatenate([g[:, DN:DN + DR] for g in g_q128], axis=0),
        's5_d': jnp.concatenate(g_dskip, axis=0), 's5_b_glu': jnp.concatenate(g_bglu, axis=0),
    }
    for i, n in enumerate(s5_names):
        small[n] = jnp.stack([g_s5[l][i] for l in range(N_A - 1)])
    small_names = [n for n in REPLICATED if n not in ('ada_b', 'kv_ada_b')] + SHARDED_VEC
    flat_small = _pack_rows([small[n].reshape(-1) for n in small_names])

    dm = jnp.concatenate(dmod + [dkmod], axis=1)[0]
    per_dev = []
    for d in range(N_DEV):
        cols = [dm[6 * D * l + per_l * d:6 * D * l + per_l * (d + 1)] for l in range(DEPTH)]
        cols.append(dm[6 * D * DEPTH + (2 * D // N_DEV) * d:6 * D * DEPTH + (2 * D // N_DEV) * (d + 1)])
        per_dev.append(jnp.concatenate(cols))
    dm_dev = jnp.stack(per_dev)
    gdm = all_gather("gather_dmod", dm_dev)
    small_st = exchange_start("small_start", [flat_small], True, gdm)
    sends[0] = exchange_start("a2a_start_0", out_l, False, small_st[4])
    dm_mine = lax.dynamic_index_in_dim(gdm, me, axis=1, keepdims=False) + sends[0][4][0, 0]
    g_wmod = small_matmul_tn("dmod_matmul", ca_all, dm_mine)
    g_ada_w = jnp.stack([g_wmod[:, per_l * l:per_l * (l + 1)] for l in range(DEPTH)])
    g_kv_ada_w = g_wmod[:, per_l * DEPTH:]
    dm_sum = sum_parts("sum_dmod", gdm.reshape(N_DEV, N_DEV, n_mod))
    g_ada_b = jnp.stack([jnp.concatenate([dm_sum[d, per_l * l:per_l * (l + 1)] for d in range(N_DEV)])
                         for l in range(DEPTH)])
    g_kv_ada_b = jnp.concatenate([dm_sum[d, per_l * DEPTH:] for d in range(N_DEV)])

    grads, out_delta, out_m, out_v = {}, {}, {}, {}

    def update(name, parts, base=0, stride=0):
        shp = W[name].shape
        shp3 = shp if len(shp) == 3 else (1,) + shp
        res = adamw("adamw_" + name, parts, W[name].reshape(shp3), M[name].reshape(shp3), V[name].reshape(shp3),
                    base, stride)
        grads[name], out_delta[name], out_m[name], out_v[name] = (a.reshape(shp) for a in res)

    update('ada_w', g_ada_w.reshape(1, DEPTH * D, per_l), 0, D)
    update('kv_ada_w', g_kv_ada_w[None])

    chains = {}

    def update_layer(name, parts, layer, base=0):
        shp = W[name].shape
        shp3 = shp if len(shp) == 3 else (1,) + shp
        chains[name] = adamw_layer(f"adamw_{name}_{layer}", parts, W[name].reshape(shp3), M[name].reshape(shp3),
                                   V[name].reshape(shp3), layer, chains.get(name), base)
        grads[name], out_delta[name], out_m[name], out_v[name] = (a.reshape(shp) for a in chains[name])

    ffn_parts = [[None] * DEPTH for _ in range(3)]

    def landed(name, started, after):
        lands = exchange_wait(name, started, after, False)
        return [lax.dynamic_update_slice(ld, lax.dynamic_index_in_dim(src, me, 0, keepdims=True), (me,) + (0,) * (src.ndim - 1))
                for ld, src in zip(lands, started[2])]

    def receive(l, after):
        recv = landed(f"a2a_wait_{l}", sends[l], after)
        if l == 0:
            recv = landed("a2a_wait_ffn0", sends_ffn0, after) + recv
        for i in range(3):
            ffn_parts[i][l] = recv[i]
        if l < N_A:
            update_layer('s5_w_glu', recv[3], l)
        else:
            update_layer('mla_w_o', recv[3], l - N_A)
            if l == N_A:
                update_layer('w_kv_b', recv[4], 0)
                update_layer('mla_w_dq', recv[4], 0, KVL)
                update_layer('w_kv_a', recv[5], 0)
            else:
                update_layer('mla_w_dq', recv[4], l - N_A)
            update_layer('mla_w_uq', recv[-1], l - N_A)

    for l in range(DEPTH - 1, 0, -1):
        receive(l, out_delta['kv_ada_w'])

    def gathered_sum(name, started, own, after):
        (land,) = exchange_wait(name + "_wait", started, after, True)
        return sum_parts("sum_" + name, lax.dynamic_update_slice(land, own[None], (me, 0, 0))).reshape(-1)

    early_sum = gathered_sum("small_s5", early_st, early_flat, chains['s5_w_glu'][1])
    g_small_sum = gathered_sum("small", small_st, flat_small, early_sum)
    off = 0
    for n in small_names:
        size = int(np.prod(small[n].shape))
        full = g_small_sum[off:off + size]
        off += size
        if n in SHARDED_VEC:
            full = lax.dynamic_slice_in_dim(full.reshape(N_A, D), me * (D // N_DEV), D // N_DEV, axis=1)
        grads[n] = full.reshape(small[n].shape if n in s5_names else W[n].shape)
    off = 0
    for i, n in enumerate(s5_names):
        size = int(np.prod(g_s5[N_A - 1][i].shape))
        last = early_sum[off:off + size].reshape((1,) + g_s5[N_A - 1][i].shape)
        off += size
        grads[n] = jnp.concatenate([grads[n], last], axis=0)
    grads['ada_b'] = g_ada_b
    grads['kv_ada_b'] = g_kv_ada_b

    big_small = ('s5_b_re', 's5_b_im', 's5_c_re', 's5_c_im')
    packed_names = [n for n in REPLICATED + SHARDED_VEC if n not in big_small]

    def pack(dct):
        flat_ = jnp.concatenate([dct[n].reshape(-1) for n in packed_names])
        n_ = int(flat_.shape[0])
        p_ = -(-n_ // 8192) * 8192
        return jnp.pad(flat_, (0, p_ - n_)).reshape(p_ // 128, 128)

    _, d_p, m_p, v_p = adamw("adamw_small", pack(grads)[None], pack(W)[None], pack(M)[None], pack(V)[None])
    off = 0
    d_p, m_p, v_p = d_p.reshape(-1), m_p.reshape(-1), v_p.reshape(-1)
    for n in packed_names:
        size = int(np.prod(W[n].shape))
        out_delta[n] = d_p[off:off + size].reshape(W[n].shape)
        out_m[n] = m_p[off:off + size].reshape(W[n].shape)
        out_v[n] = v_p[off:off + size].reshape(W[n].shape)
        off += size
    for n in big_small:
        shp = W[n].shape
        view = (1, int(np.prod(shp[:-1])), shp[-1])
        res = adamw("adamw_" + n, grads[n].reshape(view), W[n].reshape(view), M[n].reshape(view), V[n].reshape(view))
        _, out_delta[n], out_m[n], out_v[n] = (a.reshape(shp) for a in res)

    receive(0, d_p)
    for i, name in enumerate(('ffn_w_gate', 'ffn_w_up', 'ffn_w_down')):
        res = adamw_multi("adamw_" + name, ffn_parts[i], W[name], M[name], V[name])
        grads[name], out_delta[name], out_m[name], out_v[name] = res

    return (loss, grad_x[None], *[grads[n] for n in WEIGHT_NAMES], *[out_delta[n] for n in WEIGHT_NAMES],
            *[out_m[n] for n in WEIGHT_NAMES], *[out_v[n] for n in WEIGHT_NAMES])
```

```python
import functools
import math

import numpy as np
import jax
import jax.numpy as jnp
from jax import lax
from jax.experimental import pallas as pl
from jax.experimental.pallas import tpu as pltpu

F32 = jnp.float32
_MXU = jnp.bfloat16
HI = lax.Precision.HIGHEST

D = 1024
DEPTH = 4
N_A = 2
FF = 2816
FFB = 384
FFP = 8 * FFB
N_DEV = 8
G = 64
P = 16
N = 64
GB = 8
NBLK = G // GB
HALF = GB * N
H = 16
HP = H // 2
DN, DR, DV = 64, 32, 64
HD = 128
QL = 256
KVL = 256
CHUNK = 64
ROPE_THETA = 10000.0
ATTN_SCALE = 1.0 / math.sqrt(DN + DR)
LOG2E = 1.4426950408889634
EXP2_SCALE = ATTN_SCALE * LOG2E
EPS = 1e-6
ADAM_LR, ADAM_B1, ADAM_B2, ADAM_EPS, ADAM_WD, ADAM_STEP = 0.001, 0.9, 0.999, 1e-08, 0.01, 10
VMEM_LIMIT = 56 * 1024 * 1024
MESH = pl.DeviceIdType.MESH

TILE_ROW = 256
TILE_ATT = 512
TILE_SCAN = 1024


def _params(n_grid, fuse_inputs=None):
    return pltpu.CompilerParams(dimension_semantics=("arbitrary",) * n_grid, vmem_limit_bytes=VMEM_LIMIT,
                                allow_input_fusion=fuse_inputs)


@jax.custom_vjp
def mm(a, w):
    return jnp.dot(a.astype(_MXU), w, preferred_element_type=F32)


def _mm_fwd(a, w):
    return mm(a, w), w


def _mm_bwd(w, g):
    da = lax.dot_general(g.astype(_MXU), w, (((1,), (1,)), ((), ())), preferred_element_type=F32)
    return da, jnp.zeros_like(w)


mm.defvjp(_mm_fwd, _mm_bwd)


def rms(x, g):
    return x * lax.rsqrt(jnp.mean(x * x, axis=-1, keepdims=True) + EPS) * g


def modulate(h, shift, scale):
    return h * (1.0 + scale) + shift


def _lane(n=HD):
    return lax.broadcasted_iota(jnp.int32, (1, n), 1)


def _rot_matrix():
    r = lax.broadcasted_iota(jnp.int32, (HD, HD), 0)
    c = lax.broadcasted_iota(jnp.int32, (HD, HD), 1)
    first = (c >= DN) & (c < DN + DR // 2) & (r == c + DR // 2)
    second = (c >= DN + DR // 2) & (c < DN + DR) & (r == c - DR // 2)
    return jnp.where(first, -1.0, jnp.where(second, 1.0, 0.0)).astype(F32)


def head_norm_rope(xh, g128, cosf, sinf, rot, with_nope):
    lane = _lane()
    m_n = lane < DN
    m_r = (lane >= DN) & (lane < DN + DR)
    sq = xh * xh
    inv_r = lax.rsqrt(jnp.sum(jnp.where(m_r, sq, 0.0), axis=-1, keepdims=True) / DR + EPS)
    if with_nope:
        inv_n = lax.rsqrt(jnp.sum(jnp.where(m_n, sq, 0.0), axis=-1, keepdims=True) / DN + EPS)
        inv = jnp.where(m_n, inv_n, jnp.where(m_r, inv_r, 0.0))
    else:
        inv = jnp.where(m_r, inv_r, 0.0)
    xg = xh * inv * g128
    return xg * cosf + jnp.dot(xg, rot, precision=HI, preferred_element_type=F32) * sinf


def seg_pre(x, g, sh, sc):
    return (modulate(rms(x, g), sh, sc),), ()


def seg_glu(x, y, gt, b, t_z, w):
    g = jax.nn.gelu(y)
    z = mm(g, w) + b + t_z
    return (x + gt * (g * jax.nn.sigmoid(z)),), (g.astype(_MXU),)


def seg_o(x, o, gt, t_o, w):
    return (x + gt * (mm(o, w) + t_o),), (o.astype(_MXU),)


def seg_q(x, g, sh, sc, qg, g128, t_l, t_q, cosf, sinf, wdq, wuq):
    h = modulate(rms(x, g), sh, sc)
    ql = mm(h, wdq) + t_l
    qn = rms(ql, qg)
    q = mm(qn, wuq) + t_q
    rot = _rot_matrix()
    heads = [head_norm_rope(q[:, HD * i:HD * (i + 1)], g128, cosf, sinf, rot, True) for i in range(H)]
    return (jnp.concatenate(heads, axis=1),), (h.astype(_MXU), qn.astype(_MXU))


def seg_kv(x, g, sh, sc, ag, gkn, gkr, t_a, t_k, t_v, cosf, sinf, wa, wkn, wv):
    hk = modulate(rms(x, g), sh, sc)
    kva = mm(hk, wa) + t_a
    ckv = rms(kva[:, :KVL], ag)
    kr = head_norm_rope(kva[:, KVL:KVL + HD], gkr, cosf, sinf, _rot_matrix(), False)
    kn = mm(ckv, wkn) + t_k
    v = mm(ckv, wv) + t_v
    heads = []
    for i in range(H):
        kh = kn[:, HD * i:HD * (i + 1)]
        inv = lax.rsqrt(jnp.sum(kh * kh, axis=-1, keepdims=True) / DN + EPS)
        heads.append(kh * inv * gkn + kr)
    return (jnp.concatenate(heads, axis=1), v), (hk.astype(_MXU), ckv.astype(_MXU))


def _row_call(name, body_fn, rows, fulls, out_rows, out_accs, tile):
    s = rows[0].shape[0]
    n_tiles = s // tile
    n_rows, n_fulls, n_or, n_oa = len(rows), len(fulls), len(out_rows), len(out_accs)

    def kern(*refs):
        i = pl.program_id(0)
        row_v = [r[...] for r in refs[:n_rows]]
        full_v = [r[...] for r in refs[n_rows:n_rows + n_fulls]]
        o_refs = refs[n_rows + n_fulls:]
        ro, ao = body_fn(row_v, full_v)
        for r, v in zip(o_refs[:n_or], ro):
            r[...] = v.astype(r.dtype)
        if n_oa:
            @pl.when(i == 0)
            def _():
                for r in o_refs[n_or:]:
                    r[...] = jnp.zeros(r.shape, r.dtype)
            for r, v in zip(o_refs[n_or:], ao):
                r[...] += v.astype(r.dtype)

    in_specs = [pl.BlockSpec((tile, a.shape[1]), lambda i: (i, 0)) for a in rows]
    for a in fulls:
        big = a.size * a.dtype.itemsize > (1 << 20)
        nd = a.ndim
        in_specs.append(pl.BlockSpec(a.shape, functools.partial(lambda i, nd_: (0,) * nd_, nd_=nd),
                                     **({"pipeline_mode": pl.Buffered(1)} if big else {})))
    out_shape = [jax.ShapeDtypeStruct((s, w), dt) for w, dt in out_rows]
    out_shape += [jax.ShapeDtypeStruct(shp, dt) for shp, dt in out_accs]
    out_specs = [pl.BlockSpec((tile, w), lambda i: (i, 0)) for w, _ in out_rows]
    out_specs += [pl.BlockSpec(shp, functools.partial(lambda i, nd_: (0,) * nd_, nd_=len(shp))) for shp, _ in out_accs]
    fuse = [False] * n_rows + [a.size * a.dtype.itemsize > (1 << 20) for a in fulls]
    res = pl.pallas_call(kern, out_shape=out_shape, grid=(n_tiles,), in_specs=in_specs, out_specs=out_specs,
                         name=name, compiler_params=_params(1, fuse))(*rows, *fulls)
    return list(res)


def seg_forward(name, seg, rows, smalls, consts_rows, consts_full, out_widths, tile=TILE_ROW, tap_widths=()):
    n_r, n_s, n_cr = len(rows), len(smalls), len(consts_rows)

    def body(row_v, full_v):
        t = row_v[0].shape[0]
        taps = [jnp.zeros((t, w), F32) for w in tap_widths]
        outs, _ = seg(*row_v[:n_r], *full_v[:n_s], *taps, *row_v[n_r:], *full_v[n_s:])
        return outs, ()

    return _row_call(name, body, list(rows) + list(consts_rows), list(smalls) + list(consts_full),
                     out_widths, [], tile)


def seg_backward(name, seg, rows, smalls, consts_rows, consts_full, cots, tap_widths, aux_widths,
                 dx_add=None, tile=TILE_ROW):
    cot_groups = [list(c) if isinstance(c, (list, tuple)) else [c] for c in cots]
    cot_flat = [a for grp in cot_groups for a in grp]
    n_r, n_s, n_cr, n_c = len(rows), len(smalls), len(consts_rows), len(cot_flat)
    has_add = dx_add is not None

    def body(row_v, full_v):
        t = row_v[0].shape[0]
        prim_rows = row_v[:n_r]
        c_rows = row_v[n_r:n_r + n_cr]
        cot_v = list(row_v[n_r + n_cr:n_r + n_cr + n_c])
        add_v = row_v[n_r + n_cr + n_c] if has_add else None
        small_v = full_v[:n_s]
        c_full = full_v[n_s:]
        taps = [jnp.zeros((t, w), F32) for w in tap_widths]
        cot_sum = []
        for grp in cot_groups:
            parts = [cot_v.pop(0).astype(F32) for _ in grp]
            cot_sum.append(functools.reduce(lambda x_, y_: x_ + y_, parts))

        def f(*args):
            return seg(*args, *c_rows, *c_full)

        _, vjp_fn, aux = jax.vjp(f, *prim_rows, *small_v, *taps, has_aux=True)
        grads = vjp_fn(tuple(cot_sum))
        d_rows = list(grads[:n_r])
        if has_add:
            d_rows[0] = d_rows[0] + add_v
        d_small = grads[n_r:n_r + n_s]
        d_taps = grads[n_r + n_s:]
        return d_rows + list(d_taps) + list(aux), [jnp.sum(g, axis=0, keepdims=True) if g.shape[0] != 1 else g
                                                   for g in d_small]

    all_rows = list(rows) + list(consts_rows) + cot_flat + ([dx_add] if has_add else [])
    out_rows = [(a.shape[1], F32) for a in rows] + [(w, _MXU) for w in tap_widths] + [(w, _MXU) for w in aux_widths]
    out_accs = [((1, a.shape[1]), F32) for a in smalls]
    res = _row_call(name, body, all_rows, list(smalls) + list(consts_full), out_rows, out_accs, tile)
    n_t, n_a = len(tap_widths), len(aux_widths)
    return res[:n_r], res[n_r:n_r + n_t], res[n_r + n_t:n_r + n_t + n_a], res[n_r + n_t + n_a:]


def _split(n):
    if n <= 1024:
        return n
    for t in (1408, 1024, 768, 512, 256, 128):
        if n % t == 0:
            return t
    raise ValueError(n)


def matmul_tn(name, a, b, out_dtype, col_blocks=None):
    s, k1 = a.shape
    _, k2 = b.shape
    tm, ts = _split(k1), 2048
    if col_blocks is None:
        tn, per_step, wblk = _split(k2), 1, None
    else:
        wblk = k2 // col_blocks
        per_step = max(1, min(col_blocks, 1536 // wblk))
        tn = per_step * wblk
    n_s = s // ts

    def kern(a_ref, b_ref, o_ref, acc_ref):
        k = pl.program_id(2)

        @pl.when(k == 0)
        def _():
            acc_ref[...] = jnp.zeros(acc_ref.shape, F32)

        acc_ref[...] += lax.dot_general(a_ref[...], b_ref[...], (((0,), (0,)), ((), ())),
                                        preferred_element_type=F32)

        @pl.when(k == n_s - 1)
        def _():
            if col_blocks is None:
                o_ref[...] = acc_ref[...].astype(o_ref.dtype)
            else:
                for cb in range(per_step):
                    o_ref[cb] = acc_ref[:, wblk * cb:wblk * (cb + 1)].astype(o_ref.dtype)

    if col_blocks is None:
        out_shape = jax.ShapeDtypeStruct((k1, k2), out_dtype)
        out_spec = pl.BlockSpec((tm, tn), lambda i, j, k: (i, j))
    else:
        out_shape = jax.ShapeDtypeStruct((col_blocks, k1, wblk), out_dtype)
        out_spec = pl.BlockSpec((per_step, tm, wblk), lambda i, j, k: (j, i, 0))
    return pl.pallas_call(
        kern, out_shape=out_shape, grid=(k1 // tm, k2 // tn, n_s),
        in_specs=[pl.BlockSpec((ts, tm), lambda i, j, k: (k, i)), pl.BlockSpec((ts, tn), lambda i, j, k: (k, j))],
        out_specs=out_spec,
        scratch_shapes=[pltpu.VMEM((tm, tn), F32)], name=name, compiler_params=_params(3))(a, b)


def ffn_forward(name, x, g, sh, sc, gt, wg, wu, wd, tile=TILE_ROW):
    s = x.shape[0]
    fp = wg.shape[1]
    blk = 2 * FFB
    n_blk = fp // blk

    def kern(x_ref, g_ref, sh_ref, sc_ref, gt_ref, wg_ref, wu_ref, wd_ref, o_ref, gate_ref, up_ref):
        xv = x_ref[...]
        hb = modulate(rms(xv, g_ref[...]), sh_ref[...], sc_ref[...]).astype(_MXU)
        y = jnp.zeros((tile, D), F32)
        for c in range(n_blk):
            cs = slice(blk * c, blk * (c + 1))
            gate = jnp.dot(hb, wg_ref[:, cs], preferred_element_type=F32)
            up = jnp.dot(hb, wu_ref[:, cs], preferred_element_type=F32)
            gate_ref[:, cs] = gate.astype(_MXU)
            up_ref[:, cs] = up.astype(_MXU)
            y = y + jnp.dot((jax.nn.silu(gate) * up).astype(_MXU), wd_ref[cs, :], preferred_element_type=F32)
        o_ref[...] = xv + gt_ref[...] * y

    row = lambda w: pl.BlockSpec((tile, w), lambda i: (i, 0))
    vec = pl.BlockSpec((1, D), lambda i: (0, 0))
    wspec = lambda a: pl.BlockSpec(a.shape, lambda i: (0, 0), pipeline_mode=pl.Buffered(1))
    return pl.pallas_call(
        kern, out_shape=[jax.ShapeDtypeStruct((s, D), F32), jax.ShapeDtypeStruct((s, fp), _MXU),
                         jax.ShapeDtypeStruct((s, fp), _MXU)],
        grid=(s // tile,), in_specs=[row(D), vec, vec, vec, vec, wspec(wg), wspec(wu), wspec(wd)],
        out_specs=[row(D), row(fp), row(fp)], name=name,
        compiler_params=_params(1, [False] * 5 + [True] * 3))(x, g, sh, sc, gt, wg, wu, wd)


def ffn_backward(name, x, dxo, gate, up, g, sh, sc, gt, wg, wu, wd, tile=TILE_ROW):
    s = x.shape[0]
    fp = wg.shape[1]
    blk = 2 * FFB
    n_blk = fp // blk

    def kern(x_ref, dxo_ref, gate_ref, up_ref, g_ref, sh_ref, sc_ref, gt_ref, wg_ref, wu_ref, wd_ref,
             dx_ref, dg_ref, du_ref, dy_ref, h_ref, a_ref, dgn_ref, dsh_ref, dsc_ref, dgt_ref):
        i = pl.program_id(0)

        @pl.when(i == 0)
        def _():
            for r in (dgn_ref, dsh_ref, dsc_ref, dgt_ref):
                r[...] = jnp.zeros(r.shape, F32)

        dxo = dxo_ref[...]
        h, pre_vjp = jax.vjp(lambda *p: modulate(rms(p[0], p[1]), p[2], p[3]), x_ref[...], g_ref[...], sh_ref[...],
                             sc_ref[...])
        h_ref[...] = h.astype(_MXU)
        dyb = (gt_ref[...] * dxo).astype(_MXU)
        dy_ref[...] = dyb
        y = jnp.zeros((tile, D), F32)
        dh = jnp.zeros((tile, D), F32)
        tr = (((1,), (1,)), ((), ()))
        for c in range(n_blk):
            cs = slice(blk * c, blk * (c + 1))
            gate = gate_ref[:, cs].astype(F32)
            up = up_ref[:, cs].astype(F32)
            sig = jax.nn.sigmoid(gate)
            sl = gate * sig
            ab = (sl * up).astype(_MXU)
            a_ref[:, cs] = ab
            y = y + jnp.dot(ab, wd_ref[cs, :], preferred_element_type=F32)
            da = lax.dot_general(dyb, wd_ref[cs, :], tr, preferred_element_type=F32)
            dgb = (da * up * (sig * (1.0 + gate * (1.0 - sig)))).astype(_MXU)
            dub = (da * sl).astype(_MXU)
            dg_ref[:, cs] = dgb
            du_ref[:, cs] = dub
            dh = dh + lax.dot_general(dgb, wg_ref[:, cs], tr, preferred_element_type=F32) \
                + lax.dot_general(dub, wu_ref[:, cs], tr, preferred_element_type=F32)
        dgt_ref[...] += jnp.sum(dxo * y, axis=0, keepdims=True)
        dx_pre, dgn, dsh, dsc = pre_vjp(dh)
        dx_ref[...] = dxo + dx_pre
        dgn_ref[...] += dgn
        dsh_ref[...] += dsh
        dsc_ref[...] += dsc

    row = lambda w: pl.BlockSpec((tile, w), lambda i: (i, 0))
    vec = pl.BlockSpec((1, D), lambda i: (0, 0))
    wspec = lambda a: pl.BlockSpec(a.shape, lambda i: (0, 0), pipeline_mode=pl.Buffered(1))
    rows_out = [(D, F32), (fp, _MXU), (fp, _MXU), (D, _MXU), (D, _MXU), (fp, _MXU)]
    return pl.pallas_call(
        kern,
        out_shape=[jax.ShapeDtypeStruct((s, w), dt) for w, dt in rows_out] + [jax.ShapeDtypeStruct((1, D), F32)] * 4,
        grid=(s // tile,),
        in_specs=[row(D), row(D), row(fp), row(fp), vec, vec, vec, vec, wspec(wg), wspec(wu), wspec(wd)],
        out_specs=[row(w) for w, _ in rows_out] + [vec] * 4,
        name=name, compiler_params=_params(1, [False] * 8 + [True] * 3))(x, dxo, gate, up, g, sh, sc, gt, wg, wu, wd)


def small_matmul(name, a, w, tn=256):
    m, k = a.shape
    n = w.shape[1]

    def kern(a_ref, w_ref, o_ref):
        o_ref[...] = jnp.dot(a_ref[...].astype(_MXU), w_ref[...].astype(_MXU), preferred_element_type=F32)

    return pl.pallas_call(kern, out_shape=jax.ShapeDtypeStruct((m, n), F32), grid=(n // tn,),
                          in_specs=[pl.BlockSpec((m, k), lambda j: (0, 0)), pl.BlockSpec((k, tn), lambda j: (0, j))],
                          out_specs=pl.BlockSpec((m, tn), lambda j: (0, j)), name=name,
                          compiler_params=_params(1))(a, w)


def small_matmul_tn(name, a, b, tn=256):
    m, k = a.shape
    n = b.shape[1]

    def kern(a_ref, b_ref, o_ref):
        o_ref[...] = lax.dot_general(a_ref[...].astype(_MXU), b_ref[...].astype(_MXU), (((0,), (0,)), ((), ())),
                                     preferred_element_type=F32)

    return pl.pallas_call(kern, out_shape=jax.ShapeDtypeStruct((k, n), F32), grid=(n // tn,),
                          in_specs=[pl.BlockSpec((m, k), lambda j: (0, 0)), pl.BlockSpec((m, tn), lambda j: (0, j))],
                          out_specs=pl.BlockSpec((k, tn), lambda j: (0, j)), name=name,
                          compiler_params=_params(1))(a, b)


def _s5_prep_math(lam_re, lam_im, log_dt, b_re_t, b_im_t, expand):
    dt = jnp.dot(jnp.exp(log_dt), expand, precision=HI, preferred_element_type=F32)
    mag = jnp.exp(lam_re * dt)
    ab_re = mag * jnp.cos(lam_im * dt)
    ab_im = mag * jnp.sin(lam_im * dt)
    den = lam_re * lam_re + lam_im * lam_im
    nr = ab_re - 1.0
    ni = ab_im
    f_re = (nr * lam_re + ni * lam_im) / den
    f_im = (ni * lam_re - nr * lam_im) / den
    bb_re = f_re * b_re_t - f_im * b_im_t
    bb_im = f_re * b_im_t + f_im * b_re_t
    return ab_re, ab_im, bb_re, bb_im


def _whole(kern, name, out_shape, *args):
    return pl.pallas_call(kern, out_shape=out_shape, name=name,
                          compiler_params=pltpu.CompilerParams(vmem_limit_bytes=VMEM_LIMIT))(*args)


def s5_prep_fwd(name, lam_re, lam_im, log_dt, b_re_t, b_im_t, expand):
    def kern(a, b, c, d, e, f, o0, o1, o2, o3):
        r = _s5_prep_math(a[...], b[...], c[...], d[...], e[...], f[...])
        for o, v in zip((o0, o1, o2, o3), r):
            o[...] = v

    gn = lam_re.shape[1]
    shp = [jax.ShapeDtypeStruct((1, gn), F32)] * 2 + [jax.ShapeDtypeStruct((P, gn), F32)] * 2
    return _whole(kern, name, shp, lam_re, lam_im, log_dt, b_re_t, b_im_t, expand)


def s5_prep_bwd(name, lam_re, lam_im, log_dt, b_re_t, b_im_t, expand, cots):
    def kern(a, b, c, d, e, f, c0, c1, c2, c3, o0, o1, o2, o3, o4):
        ex = f[...]
        _, vjp_fn = jax.vjp(lambda *p: _s5_prep_math(*p, ex), a[...], b[...], c[...], d[...], e[...])
        g = vjp_fn((c0[...], c1[...], c2[...], c3[...]))
        for o, v in zip((o0, o1, o2, o3, o4), g):
            o[...] = v

    shp = [jax.ShapeDtypeStruct(a.shape, F32) for a in (lam_re, lam_im, log_dt, b_re_t, b_im_t)]
    return _whole(kern, name, shp, lam_re, lam_im, log_dt, b_re_t, b_im_t, expand, *cots)


def _cpowers(ar, ai):
    pw = [(ar, ai)]
    for _ in range(7):
        pr, pi = pw[-1]
        pw.append((pr * ar - pi * ai, pr * ai + pi * ar))
    return pw


def _row_select(row, values):
    out = jnp.broadcast_to(values[7], (8, values[7].shape[1]))
    for r in range(6, -1, -1):
        out = jnp.where(row == r, values[r], out)
    return out


def _scan_tables(ar, ai, reverse):
    pw = _cpowers(ar, ai)
    row = lax.broadcasted_iota(jnp.int32, (8, ar.shape[1]), 0)
    steps = []
    for d in (1, 2, 4):
        keep = (row <= 7 - d) if reverse else (row >= d)
        steps.append((jnp.where(keep, pw[d - 1][0], 0.0), jnp.where(keep, pw[d - 1][1], 0.0)))
    order = list(range(7, -1, -1)) if reverse else list(range(8))
    carry = (_row_select(row, [pw[i][0] for i in order]), _row_select(row, [pw[i][1] for i in order]))
    return steps, carry


def _tile_scan_fwd(xr, xi, cr, ci, steps, carry_m):
    for d, (mr, mi) in zip((1, 2, 4), steps):
        sr = pltpu.roll(xr, d, 0)
        si = pltpu.roll(xi, d, 0)
        xr, xi = xr + mr * sr - mi * si, xi + mr * si + mi * sr
    pr, pi = carry_m
    return xr + pr * cr - pi * ci, xi + pr * ci + pi * cr


def _tile_scan_rev(xr, xi, cr, ci, steps, carry_m):
    for d, (mr, mi) in zip((1, 2, 4), steps):
        sr = pltpu.roll(xr, 8 - d, 0)
        si = pltpu.roll(xi, 8 - d, 0)
        xr, xi = xr + mr * sr + mi * si, xi + mr * si - mi * sr
    pr, pi = carry_m
    return xr + pr * cr + pi * ci, xi + pr * ci - pi * cr


def _fwd_scan_block(buf, row0, n_tiles8, ar, ai, c0r, c0i):
    steps, carry_m = _scan_tables(ar, ai, False)

    def body(j, carry):
        cr, ci = carry
        r0 = pl.multiple_of(row0 + j * 8, 8)
        xr = buf[pl.ds(r0, 8), 0:HALF]
        xi = buf[pl.ds(r0, 8), HALF:2 * HALF]
        xr, xi = _tile_scan_fwd(xr, xi, cr, ci, steps, carry_m)
        buf[pl.ds(r0, 8), 0:HALF] = xr
        buf[pl.ds(r0, 8), HALF:2 * HALF] = xi
        return xr[7:8], xi[7:8]

    return lax.fori_loop(0, n_tiles8, body, (c0r, c0i))


def s5_scan_fwd(name, h, wb, wc, a_tab, dskip, tile=TILE_SCAN):
    s = h.shape[0]
    n_t = s // tile

    def kern(h_ref, wb_ref, wc_ref, a_ref, d_ref, y_ref, s0_ref, carry_ref, buf):
        i = pl.program_id(0)

        @pl.when(i == 0)
        def _():
            carry_ref[...] = jnp.zeros(carry_ref.shape, F32)

        s0_ref[0] = carry_ref[...]
        for k in range(NBLK):
            cols = slice(GB * P * k, GB * P * (k + 1))
            u = h_ref[:, cols]
            buf[...] = jnp.dot(u.astype(_MXU), wb_ref[k], preferred_element_type=F32)
            ar = a_ref[k, :, 0:HALF]
            ai = a_ref[k, :, HALF:2 * HALF]
            cr, ci = _fwd_scan_block(buf, 0, tile // 8, ar, ai, carry_ref[k:k + 1, 0:HALF],
                                     carry_ref[k:k + 1, HALF:2 * HALF])
            carry_ref[k:k + 1, 0:HALF] = cr
            carry_ref[k:k + 1, HALF:2 * HALF] = ci
            y_ref[:, cols] = jnp.dot(buf[...].astype(_MXU), wc_ref[k], preferred_element_type=F32) + d_ref[:, cols] * u

    full = lambda a: pl.BlockSpec(a.shape, functools.partial(lambda i, nd_: (0,) * nd_, nd_=a.ndim))
    return pl.pallas_call(
        kern,
        out_shape=[jax.ShapeDtypeStruct((s, D), F32), jax.ShapeDtypeStruct((n_t, NBLK, 2 * HALF), F32)],
        grid=(n_t,),
        in_specs=[pl.BlockSpec((tile, D), lambda i: (i, 0)), full(wb), full(wc), full(a_tab), full(dskip)],
        out_specs=[pl.BlockSpec((tile, D), lambda i: (i, 0)), pl.BlockSpec((1, NBLK, 2 * HALF), lambda i: (i, 0, 0))],
        scratch_shapes=[pltpu.VMEM((NBLK, 2 * HALF), F32), pltpu.VMEM((tile, 2 * HALF), F32)],
        name=name, compiler_params=_params(1))(h, wb, wc, a_tab, dskip)


def s5_scan_bwd(name, h, dy, s0, wb, wc, a_tab, dskip, tile=TILE_SCAN):
    s = h.shape[0]
    n_t = s // tile
    n8 = tile // 8

    def kern(h_ref, dy_ref, s0_ref, wb_ref, wc_ref, a_ref, d_ref, dh_ref, dwb_ref, dwc_ref, da_ref, dd_ref,
             lam_ref, sbuf, gbuf):
        i = pl.program_id(0)

        @pl.when(i == 0)
        def _():
            lam_ref[...] = jnp.zeros(lam_ref.shape, F32)
            dwb_ref[...] = jnp.zeros(dwb_ref.shape, F32)
            dwc_ref[...] = jnp.zeros(dwc_ref.shape, F32)
            da_ref[...] = jnp.zeros(da_ref.shape, F32)
            dd_ref[...] = jnp.zeros(dd_ref.shape, F32)

        for k in range(NBLK):
            cols = slice(GB * P * k, GB * P * (k + 1))
            u = h_ref[:, cols]
            dyk = dy_ref[:, cols]
            ar = a_ref[k, :, 0:HALF]
            ai = a_ref[k, :, HALF:2 * HALF]
            sbuf[0:8, :] = jnp.broadcast_to(s0_ref[0, k:k + 1, :], (8, 2 * HALF))
            sbuf[8:tile + 8, :] = jnp.dot(u.astype(_MXU), wb_ref[k], preferred_element_type=F32)
            _fwd_scan_block(sbuf, 8, n8, ar, ai, s0_ref[0, k:k + 1, 0:HALF], s0_ref[0, k:k + 1, HALF:2 * HALF])
            dyb = dyk.astype(_MXU)
            gbuf[...] = lax.dot_general(dyb, wc_ref[k], (((1,), (1,)), ((), ())), preferred_element_type=F32)
            dwc_ref[k] += lax.dot_general(sbuf[8:tile + 8, :].astype(_MXU), dyb, (((0,), (0,)), ((), ())),
                                          preferred_element_type=F32)
            steps, carry_m = _scan_tables(ar, ai, True)
            row = lax.broadcasted_iota(jnp.int32, (8, HALF), 0)

            def body(jj, carry):
                cr, ci, dar, dai = carry
                j = n8 - 1 - jj
                r0 = pl.multiple_of(j * 8, 8)
                xr = gbuf[pl.ds(r0, 8), 0:HALF]
                xi = gbuf[pl.ds(r0, 8), HALF:2 * HALF]
                xr, xi = _tile_scan_rev(xr, xi, cr, ci, steps, carry_m)
                gbuf[pl.ds(r0, 8), 0:HALF] = xr
                gbuf[pl.ds(r0, 8), HALF:2 * HALF] = xi
                r1 = pl.multiple_of(j * 8 + 8, 8)
                spr = jnp.where(row == 0, sbuf[pl.ds(r0, 8), 0:HALF][7:8],
                                pltpu.roll(sbuf[pl.ds(r1, 8), 0:HALF], 1, 0))
                spi = jnp.where(row == 0, sbuf[pl.ds(r0, 8), HALF:2 * HALF][7:8],
                                pltpu.roll(sbuf[pl.ds(r1, 8), HALF:2 * HALF], 1, 0))
                dar = dar + xr * spr + xi * spi
                dai = dai + xi * spr - xr * spi
                return xr[0:1], xi[0:1], dar, dai

            z8 = jnp.zeros((8, HALF), F32)
            cr, ci, dar, dai = lax.fori_loop(
                0, n8, body, (lam_ref[k:k + 1, 0:HALF], lam_ref[k:k + 1, HALF:2 * HALF], z8, z8))
            lam_ref[k:k + 1, 0:HALF] = cr
            lam_ref[k:k + 1, HALF:2 * HALF] = ci
            da_ref[k:k + 1, 0:HALF] += jnp.sum(dar, axis=0, keepdims=True)
            da_ref[k:k + 1, HALF:2 * HALF] += jnp.sum(dai, axis=0, keepdims=True)
            lam = gbuf[...].astype(_MXU)
            dwb_ref[k] += lax.dot_general(u.astype(_MXU), lam, (((0,), (0,)), ((), ())), preferred_element_type=F32)
            du = lax.dot_general(lam, wb_ref[k], (((1,), (1,)), ((), ())), preferred_element_type=F32)
            dh_ref[:, cols] = du + d_ref[:, cols] * dyk
            dd_ref[:, cols] += jnp.sum(dyk * u, axis=0, keepdims=True)

    full = lambda a: pl.BlockSpec(a.shape, functools.partial(lambda i, nd_: (0,) * nd_, nd_=a.ndim))
    fullo = lambda shp: pl.BlockSpec(shp, functools.partial(lambda i, nd_: (0,) * nd_, nd_=len(shp)))
    rev = lambda i: (n_t - 1 - i, 0)
    return pl.pallas_call(
        kern,
        out_shape=[jax.ShapeDtypeStruct((s, D), F32), jax.ShapeDtypeStruct(wb.shape, F32),
                   jax.ShapeDtypeStruct(wc.shape, F32), jax.ShapeDtypeStruct((NBLK, 2 * HALF), F32),
                   jax.ShapeDtypeStruct((1, D), F32)],
        grid=(n_t,),
        in_specs=[pl.BlockSpec((tile, D), rev), pl.BlockSpec((tile, D), rev),
                  pl.BlockSpec((1, NBLK, 2 * HALF), lambda i: (n_t - 1 - i, 0, 0)),
                  full(wb), full(wc), full(a_tab), full(dskip)],
        out_specs=[pl.BlockSpec((tile, D), rev), fullo(wb.shape), fullo(wc.shape), fullo((NBLK, 2 * HALF)),
                   fullo((1, D))],
        scratch_shapes=[pltpu.VMEM((NBLK, 2 * HALF), F32), pltpu.VMEM((tile + 8, 2 * HALF), F32),
                        pltpu.VMEM((tile, 2 * HALF), F32)],
        name=name, compiler_params=_params(1))(h, dy, s0, wb, wc, a_tab, dskip)


def _chunk_mask(q0, k0, tq, tk):
    r = (q0 + lax.broadcasted_iota(jnp.int32, (tq, tk), 0)) // CHUNK
    c = (k0 + lax.broadcasted_iota(jnp.int32, (tq, tk), 1)) // CHUNK
    return r >= c


def _head_lanes(j):
    lane = _lane(2 * DV)
    return (lane >= DV * j) & (lane < DV * (j + 1))


def _raw_scores(q, kblk, masked, t):
    s = lax.dot_general(q, kblk, (((1,), (1,)), ((), ())), preferred_element_type=F32)
    return jnp.where(_chunk_mask(0, 0, t, t), s, -1e30) if masked else s


def attn_fwd(name, q, k, v, t=TILE_ATT, tk=TILE_ATT):
    s = q.shape[0]
    n_q = s // t
    r = t // tk

    def kern(q_ref, k_ref, v_ref, o_ref, lse_ref):
        qi = pl.program_id(1)
        qs = [q_ref[:, HD * j:HD * (j + 1)] for j in range(2)]

        def absorb(k0, carry, mask):
            vblk = v_ref[pl.ds(k0, tk), :]
            scs = [lax.dot_general(qs[j], k_ref[pl.ds(k0, tk), HD * j:HD * (j + 1)], (((1,), (1,)), ((), ())),
                                   preferred_element_type=F32) for j in range(2)]
            if mask is not None:
                scs = [jnp.where(mask, sc, -1e30) for sc in scs]
            m_new = [jnp.maximum(carry[j][0], jnp.max(scs[j], axis=-1, keepdims=True)) for j in range(2)]
            ps = [jnp.exp2((scs[j] - m_new[j]) * EXP2_SCALE) for j in range(2)]
            alphas = [jnp.exp2((carry[j][0] - m_new[j]) * EXP2_SCALE) for j in range(2)]
            pvs = [jnp.dot(ps[j].astype(_MXU), vblk, preferred_element_type=F32) for j in range(2)]
            return tuple((m_new[j], alphas[j] * carry[j][1] + jnp.sum(ps[j], axis=-1, keepdims=True),
                          alphas[j] * carry[j][2] + pvs[j]) for j in range(2))

        init = tuple((jnp.full((t, 1), -1e30, F32), jnp.zeros((t, 1), F32), jnp.zeros((t, 2 * DV), F32))
                     for _ in range(2))
        carry = lax.fori_loop(0, qi * r, lambda kb, c: absorb(pl.multiple_of(kb * tk, tk), c, None), init)
        for i in range(r):
            carry = absorb(pl.multiple_of(qi * t + i * tk, tk), carry, _chunk_mask(0, i * tk, t, tk))
        outs = []
        for j in range(2):
            m, l, acc = carry[j]
            outs.append(acc / l)
            lse_ref[0, j] = m * ATTN_SCALE + jnp.log(l)
        o_ref[...] = jnp.where(_head_lanes(0), outs[0], outs[1])

    return pl.pallas_call(
        kern,
        out_shape=[jax.ShapeDtypeStruct((s, H * DV), F32), jax.ShapeDtypeStruct((HP, 2, s, 1), F32)],
        grid=(HP, n_q),
        in_specs=[pl.BlockSpec((t, 2 * HD), lambda hp, i: (i, hp)), pl.BlockSpec((s, 2 * HD), lambda hp, i: (0, hp)),
                  pl.BlockSpec((s, 2 * DV), lambda hp, i: (0, hp))],
        out_specs=[pl.BlockSpec((t, 2 * DV), lambda hp, i: (i, hp)),
                   pl.BlockSpec((1, 2, t, 1), lambda hp, i: (hp, 0, i, 0))],
        name=name, compiler_params=_params(2))(q, k, v)


def attn_bwd(name, q, k, v, o, do, lse, t=TILE_ATT):
    s = q.shape[0]
    n_q = s // t

    def kern(q_ref, k_ref, v_ref, o_ref, do_ref, lse_ref, dq_ref, dk_ref, dv_ref):
        qi = pl.program_id(1)

        @pl.when(qi == 0)
        def _():
            dk_ref[...] = jnp.zeros(dk_ref.shape, F32)
            dv_ref[...] = jnp.zeros(dv_ref.shape, F32)

        qs, doms, deltas, lse2 = [], [], [], []
        for j in range(2):
            qs.append(q_ref[:, HD * j:HD * (j + 1)])
            dom = jnp.where(_head_lanes(j), do_ref[...], 0.0)
            deltas.append(jnp.sum(dom * o_ref[...], axis=-1, keepdims=True))
            doms.append(dom.astype(_MXU))
            lse2.append(lse_ref[0, j] * LOG2E)

        def block(k0, dqs, masked):
            vblk = v_ref[pl.ds(k0, t), :]
            kblks = [k_ref[pl.ds(k0, t), HD * j:HD * (j + 1)] for j in range(2)]
            scs = [_raw_scores(qs[j], kblks[j], masked, t) for j in range(2)]
            dps = [lax.dot_general(doms[j], vblk, (((1,), (1,)), ((), ())), preferred_element_type=F32)
                   for j in range(2)]
            ps = [jnp.exp2(scs[j] * EXP2_SCALE - lse2[j]) for j in range(2)]
            dss = [(ps[j] * (dps[j] - deltas[j])).astype(_MXU) for j in range(2)]
            pbs = [ps[j].astype(_MXU) for j in range(2)]
            new = tuple(dqs[j] + jnp.dot(dss[j], kblks[j], preferred_element_type=F32) for j in range(2))
            for j in range(2):
                dk_ref[pl.ds(k0, t), HD * j:HD * (j + 1)] += lax.dot_general(
                    dss[j], qs[j], (((0,), (0,)), ((), ())), preferred_element_type=F32)
            dvs = [lax.dot_general(pbs[j], doms[j], (((0,), (0,)), ((), ())), preferred_element_type=F32)
                   for j in range(2)]
            dv_ref[pl.ds(k0, t), :] += dvs[0] + dvs[1]
            return new

        init = (jnp.zeros((t, HD), F32), jnp.zeros((t, HD), F32))
        dqs = lax.fori_loop(0, qi, lambda kb, c: block(pl.multiple_of(kb * t, t), c, False), init)
        dqs = block(pl.multiple_of(qi * t, t), dqs, True)
        for j in range(2):
            dq_ref[:, HD * j:HD * (j + 1)] = dqs[j] * ATTN_SCALE

        @pl.when(qi == n_q - 1)
        def _():
            dk_ref[...] = dk_ref[...] * ATTN_SCALE

    return pl.pallas_call(
        kern,
        out_shape=[jax.ShapeDtypeStruct((s, H * HD), F32), jax.ShapeDtypeStruct((s, H * HD), F32),
                   jax.ShapeDtypeStruct((s, H * DV), F32)],
        grid=(HP, n_q),
        in_specs=[pl.BlockSpec((t, 2 * HD), lambda hp, i: (i, hp)), pl.BlockSpec((s, 2 * HD), lambda hp, i: (0, hp)),
                  pl.BlockSpec((s, 2 * DV), lambda hp, i: (0, hp)), pl.BlockSpec((t, 2 * DV), lambda hp, i: (i, hp)),
                  pl.BlockSpec((t, 2 * DV), lambda hp, i: (i, hp)),
                  pl.BlockSpec((1, 2, t, 1), lambda hp, i: (hp, 0, i, 0))],
        out_specs=[pl.BlockSpec((t, 2 * HD), lambda hp, i: (i, hp)), pl.BlockSpec((s, 2 * HD), lambda hp, i: (0, hp)),
                   pl.BlockSpec((s, 2 * DV), lambda hp, i: (0, hp))],
        name=name, compiler_params=_params(2))(q, k, v, o, do, lse)


def rope_tables(name, pos_col, inv128):
    s = pos_col.shape[0]

    def kern(p_ref, inv_ref, c_ref, s_ref):
        ang = p_ref[...].astype(F32) * inv_ref[...]
        lane = _lane()
        m_r = (lane >= DN) & (lane < DN + DR)
        c_ref[...] = jnp.where(lane < DN, 1.0, jnp.where(m_r, jnp.cos(ang), 0.0))
        s_ref[...] = jnp.where(m_r, jnp.sin(ang), 0.0)

    return _whole(kern, name, [jax.ShapeDtypeStruct((s, HD), F32)] * 2, pos_col, inv128)


def loss_kernel(name, y, tgt, tile=TILE_ROW):
    def body(row_v, _):
        err = row_v[0] - row_v[1]
        part = 0.5 * jnp.sum(jnp.mean(err * err, axis=-1, keepdims=True), axis=0, keepdims=True)
        return [err * (1.0 / D)], [jnp.broadcast_to(part, (1, 128))]

    return _row_call(name, body, [y, tgt], [], [(D, F32)], [((1, 128), F32)], tile)


def _row_tile(r, c):
    cap = max(8, (1 << 18) // max(c, 1))
    for t in (2048, 1024, 512, 256, 128, 64, 32, 16, 8):
        if t <= cap and r % t == 0:
            return t
    return r


def sum_parts(name, parts):
    n, r, c = parts.shape
    t = _row_tile(r, c)

    def kern(p_ref, o_ref):
        acc = p_ref[0].astype(F32)
        for i in range(1, n):
            acc = acc + p_ref[i].astype(F32)
        o_ref[...] = acc

    return pl.pallas_call(kern, out_shape=jax.ShapeDtypeStruct((r, c), F32), grid=(r // t,),
                          in_specs=[pl.BlockSpec((n, t, c), lambda i: (0, i, 0))],
                          out_specs=pl.BlockSpec((t, c), lambda i: (i, 0)), name=name, compiler_params=_params(1))(parts)


def adamw(name, parts, w, m, v, base=0, stride=0):
    n, _, cp = parts.shape
    nl, r, c = w.shape
    t = _row_tile(math.gcd(math.gcd(r, base), stride), max(c, cp))
    c1 = 1.0 / (1.0 - ADAM_B1 ** ADAM_STEP)
    c2 = 1.0 / (1.0 - ADAM_B2 ** ADAM_STEP)

    def kern(p_ref, w_ref, m_ref, v_ref, g_ref, d_ref, nm_ref, nv_ref):
        g = p_ref[0].astype(F32)
        for i in range(1, n):
            g = g + p_ref[i].astype(F32)
        g = g[:, :c]
        nm = ADAM_B1 * m_ref[...] + (1.0 - ADAM_B1) * g
        nv = ADAM_B2 * v_ref[...] + (1.0 - ADAM_B2) * (g * g)
        g_ref[...] = g
        nm_ref[...] = nm
        nv_ref[...] = nv
        d_ref[...] = -ADAM_LR * ((nm * c1) / (jnp.sqrt(nv * c2) + ADAM_EPS) + ADAM_WD * w_ref[...])

    spec = pl.BlockSpec((None, t, c), lambda l, i: (l, i, 0))
    pspec = pl.BlockSpec((n, t, cp), lambda l, i: (0, (base + l * stride) // t + i, 0))
    return pl.pallas_call(kern, out_shape=[jax.ShapeDtypeStruct((nl, r, c), F32)] * 4, grid=(nl, r // t),
                          in_specs=[pspec, spec, spec, spec], out_specs=[spec] * 4, name=name,
                          compiler_params=_params(2))(parts, w, m, v)


def adamw_multi(name, parts_list, w, m, v):
    nl, r, c = w.shape
    n, _, cp = parts_list[0].shape
    t = _row_tile(r, max(c, cp))
    c1 = 1.0 / (1.0 - ADAM_B1 ** ADAM_STEP)
    c2 = 1.0 / (1.0 - ADAM_B2 ** ADAM_STEP)

    def kern(*refs):
        p_refs = refs[:nl]
        w_ref, m_ref, v_ref, g_ref, d_ref, nm_ref, nv_ref = refs[nl:]
        layer = pl.program_id(0)
        for ll in range(nl):
            @pl.when(layer == ll)
            def _(ll=ll):
                g = p_refs[ll][0].astype(F32)
                for i in range(1, n):
                    g = g + p_refs[ll][i].astype(F32)
                g = g[:, :c]
                nm = ADAM_B1 * m_ref[...] + (1.0 - ADAM_B1) * g
                nv = ADAM_B2 * v_ref[...] + (1.0 - ADAM_B2) * (g * g)
                g_ref[...] = g
                nm_ref[...] = nm
                nv_ref[...] = nv
                d_ref[...] = -ADAM_LR * ((nm * c1) / (jnp.sqrt(nv * c2) + ADAM_EPS) + ADAM_WD * w_ref[...])

    spec = pl.BlockSpec((None, t, c), lambda l, i: (l, i, 0))
    pspecs = [pl.BlockSpec((n, t, cp), functools.partial(lambda l, i, ll_: (0, jnp.where(l == ll_, i, 0), 0), ll_=ll))
              for ll in range(nl)]
    return pl.pallas_call(kern, out_shape=[jax.ShapeDtypeStruct((nl, r, c), F32)] * 4, grid=(nl, r // t),
                          in_specs=pspecs + [spec, spec, spec], out_specs=[spec] * 4, name=name,
                          compiler_params=_params(2))(*parts_list, w, m, v)


def adamw_layer(name, parts, w, m, v, layer, prev, base=0):
    n, _, cp = parts.shape
    nl, r, c = w.shape
    t = _row_tile(math.gcd(r, base), max(c, cp))
    c1 = 1.0 / (1.0 - ADAM_B1 ** ADAM_STEP)
    c2 = 1.0 / (1.0 - ADAM_B2 ** ADAM_STEP)
    chained = nl > 1

    def kern(p_ref, w_ref, m_ref, v_ref, *rest):
        g_ref, d_ref, nm_ref, nv_ref = rest[-4:]
        g = p_ref[0].astype(F32)
        for i in range(1, n):
            g = g + p_ref[i].astype(F32)
        g = g[:, :c]
        nm = ADAM_B1 * m_ref[...] + (1.0 - ADAM_B1) * g
        nv = ADAM_B2 * v_ref[...] + (1.0 - ADAM_B2) * (g * g)
        g_ref[...] = g
        nm_ref[...] = nm
        nv_ref[...] = nv
        d_ref[...] = -ADAM_LR * ((nm * c1) / (jnp.sqrt(nv * c2) + ADAM_EPS) + ADAM_WD * w_ref[...])

    spec = pl.BlockSpec((None, t, c), lambda i: (layer, i, 0))
    pspec = pl.BlockSpec((n, t, cp), lambda i: (0, base // t + i, 0))
    in_specs = [pspec, spec, spec, spec]
    args = [parts, w, m, v]
    aliases = {}
    if chained:
        if prev is None:
            prev = [lax.empty((nl, r, c), F32) for _ in range(4)]
        in_specs += [pl.BlockSpec(memory_space=pl.ANY)] * 4
        args += list(prev)
        aliases = {4 + i: i for i in range(4)}
    return pl.pallas_call(kern, out_shape=[jax.ShapeDtypeStruct((nl, r, c), F32)] * 4, grid=(r // t,),
                          in_specs=in_specs, out_specs=[spec] * 4, input_output_aliases=aliases, name=name,
                          compiler_params=_params(1))(*args)


def _me():
    return lax.axis_index("x"), lax.axis_index("y"), lax.axis_index("c")


def _flip(x, y, c, mask):
    return (jnp.where((mask >> 2) & 1, 1 - x, x), jnp.where((mask >> 1) & 1, 1 - y, y), jnp.where(mask & 1, 1 - c, c))


def _index(x, y, c):
    return 4 * x + 2 * y + c


def _exchange(name, arr, gather):
    out_shape = (N_DEV,) + arr.shape if gather else arr.shape

    def kern(in_ref, out_ref, send_sems, recv_sems, local_sem):
        x, y, c = _me()
        me = _index(x, y, c)
        mine = pltpu.make_async_copy(in_ref if gather else in_ref.at[me], out_ref.at[me], local_sem)
        mine.start()
        copies = []
        for mask in range(1, N_DEV):
            px, py, pc = _flip(x, y, c, mask)
            peer = _index(px, py, pc)
            cp = pltpu.make_async_remote_copy(
                src_ref=in_ref if gather else in_ref.at[peer], dst_ref=out_ref.at[me],
                send_sem=send_sems.at[mask - 1], recv_sem=recv_sems.at[mask - 1],
                device_id=(px, py, pc), device_id_type=MESH)
            cp.start()
            copies.append((cp, peer))
        for mask, (cp, peer) in enumerate(copies, start=1):
            pltpu.make_async_remote_copy(
                src_ref=in_ref if gather else in_ref.at[peer], dst_ref=out_ref.at[peer],
                send_sem=send_sems.at[mask - 1], recv_sem=recv_sems.at[mask - 1],
                device_id=_flip(x, y, c, mask), device_id_type=MESH).wait_recv()
        for cp, _ in copies:
            cp.wait_send()
        mine.wait()

    any_spec = pl.BlockSpec(memory_space=pl.ANY)
    return pl.pallas_call(
        kern, out_shape=jax.ShapeDtypeStruct(out_shape, arr.dtype), in_specs=[any_spec], out_specs=any_spec,
        scratch_shapes=[pltpu.SemaphoreType.DMA((N_DEV - 1,)), pltpu.SemaphoreType.DMA((N_DEV - 1,)),
                        pltpu.SemaphoreType.DMA],
        name=name, compiler_params=pltpu.CompilerParams(has_side_effects=True))(arr)


def all_gather(name, arr):
    return _exchange(name, arr, True)


_HBM = pl.BlockSpec(memory_space=pltpu.HBM)
_SEM = pl.BlockSpec(memory_space=pltpu.SEMAPHORE)
_EFFECT = pltpu.SideEffectType.DATAFLOW_SIDE_EFFECTING


def _split_copies(srcs, lands, send_sems, recv_sems, gather):
    x, y, c = _me()
    me = _index(x, y, c)
    out = []
    for a, (src, land) in enumerate(zip(srcs, lands)):
        for mask in range(1, N_DEV):
            px, py, pc = _flip(x, y, c, mask)
            peer = _index(px, py, pc)
            sem = (N_DEV - 1) * a + mask - 1
            mk = lambda dst_slot: pltpu.make_async_remote_copy(
                src_ref=src if gather else src.at[peer], dst_ref=land.at[dst_slot],
                send_sem=send_sems.at[sem], recv_sem=recv_sems.at[sem], device_id=(px, py, pc), device_id_type=MESH)
            out.append((mk(me), mk(peer)))
    return out


def exchange_start(name, arrs, gather, after):
    k = len(arrs)
    land_shapes = [((N_DEV,) + a.shape if gather else a.shape) for a in arrs]

    def body(*refs):
        srcs, lands = refs[:k], refs[k:2 * k]
        send_sems, recv_sems = refs[2 * k + 1], refs[2 * k + 2]
        token = refs[-1]
        for mine, _ in _split_copies(srcs, lands, send_sems, recv_sems, gather):
            mine.start()
        token[...] = jnp.zeros(token.shape, token.dtype)

    n_sem = (N_DEV - 1) * k
    res = pl.pallas_call(
        body, name=name,
        out_shape=(pltpu.SemaphoreType.DMA((n_sem,)), pltpu.SemaphoreType.DMA((n_sem,)),
                   *[pltpu.HBM(a.shape, a.dtype) for a in arrs],
                   *[pltpu.HBM(shp, a.dtype) for shp, a in zip(land_shapes, arrs)],
                   jax.ShapeDtypeStruct((8, 128), F32)),
        in_specs=[_HBM] * (2 * k) + [pl.BlockSpec(memory_space=pl.ANY)],
        out_specs=(_SEM, _SEM, *[_HBM] * (2 * k), pl.BlockSpec(memory_space=pltpu.VMEM)),
        input_output_aliases={i: 2 + i for i in range(2 * k)},
        compiler_params=pltpu.CompilerParams(has_side_effects=_EFFECT),
    )(*[pltpu.with_memory_space_constraint(a, pltpu.HBM) for a in arrs],
      *[pltpu.with_memory_space_constraint(lax.empty(shp, a.dtype), pltpu.HBM) for shp, a in zip(land_shapes, arrs)],
      after)
    return res[0], res[1], list(res[2:2 + k]), list(res[2 + k:2 + 2 * k]), res[-1]


def exchange_wait(name, started, after, gather):
    send_sems, recv_sems, thrus, lands, _ = started
    k = len(thrus)

    def body(*refs):
        srcs, lnds = refs[:k], refs[k:2 * k]
        s_sems, r_sems = refs[2 * k], refs[2 * k + 1]
        for mine, theirs in _split_copies(srcs, lnds, s_sems, r_sems, gather):
            mine.wait_send()
            theirs.wait_recv()

    res = pl.pallas_call(
        body, name=name,
        out_shape=tuple(pltpu.HBM(a.shape, a.dtype) for a in thrus + lands),
        in_specs=[_HBM] * (2 * k) + [_SEM, _SEM, pl.BlockSpec(memory_space=pl.ANY)], out_specs=tuple([_HBM] * (2 * k)),
        input_output_aliases={i: i for i in range(2 * k)},
        compiler_params=pltpu.CompilerParams(has_side_effects=_EFFECT),
    )(*thrus, *lands, send_sems, recv_sems, after)
    return list(res[k:])


def _pad_heads(w, real, padded):
    k = w.shape[0]
    w3 = w.reshape(k, H, real)
    return jnp.pad(w3, ((0, 0), (0, 0), (0, padded - real))).reshape(k, H * padded)


def _unpad_heads(w, real, padded):
    k = w.shape[0]
    return w.reshape(k, H, padded)[:, :, :real].reshape(k, H * real)


def _s5_place(ab_re, ab_im, bb_re_t, bb_im_t, c_re, c_im):
    eye = jnp.eye(GB, dtype=F32)

    def wb_part(bt):
        x4 = bt.reshape(P, NBLK, GB, N).transpose(1, 2, 0, 3)
        return jnp.einsum('kgpn,gh->kgphn', x4, eye).reshape(NBLK, GB * P, HALF)

    def wc_part(cc):
        x4 = cc.reshape(NBLK, GB, P, N)
        return jnp.einsum('kgpn,gh->kgnhp', x4, eye).reshape(NBLK, HALF, GB * P)

    wb = jnp.concatenate([wb_part(bb_re_t), wb_part(bb_im_t)], axis=-1)
    wc = jnp.concatenate([wc_part(c_re), -wc_part(c_im)], axis=1)
    a_tab = jnp.concatenate([ab_re.reshape(NBLK, 1, HALF), ab_im.reshape(NBLK, 1, HALF)], axis=-1)
    return wb.astype(_MXU), wc.astype(_MXU), a_tab


def _s5_unplace(dwb, dwc, da):
    eye = jnp.eye(GB, dtype=F32)

    def wb_part(dpart):
        x5 = dpart.reshape(NBLK, GB, P, GB, N)
        return jnp.einsum('kgphn,gh->kgpn', x5, eye).transpose(2, 0, 1, 3).reshape(P, G * N)

    def wc_part(dpart):
        x5 = dpart.reshape(NBLK, GB, N, GB, P)
        return jnp.einsum('kgnhp,gh->kgpn', x5, eye).reshape(G, P, N)

    dbb_re_t, dbb_im_t = wb_part(dwb[..., :HALF]), wb_part(dwb[..., HALF:])
    dc_re, dc_im = wc_part(dwc[:, :HALF]), -wc_part(dwc[:, HALF:])
    dab_re, dab_im = da[:, :HALF].reshape(1, G * N), da[:, HALF:].reshape(1, G * N)
    return dab_re, dab_im, dbb_re_t, dbb_im_t, dc_re, dc_im


def _row(v):
    return v.reshape(1, -1)


def _pack_rows(pieces):
    flat = jnp.concatenate(pieces)
    n = int(flat.shape[0])
    padded = -(-n // 65536) * 65536
    return jnp.pad(flat, (0, padded - n)).reshape(padded // 128, 128)


def kernel(x, c, positions, ada_w, ada_b, norm1_g, norm2_g, ffn_w_gate, ffn_w_up, ffn_w_down, s5_lam_re, s5_lam_im, s5_log_dt, s5_b_re, s5_b_im, s5_c_re, s5_c_im, s5_d, s5_w_glu, s5_b_glu, kv_ada_w, kv_ada_b, kv_norm_g, w_kv_a, kv_a_norm_g, w_kv_b, k_nope_norm_g, k_rope_norm_g, mla_w_dq, mla_q_norm_g, mla_w_uq, mla_q_nope_norm_g, mla_q_rope_norm_g, mla_w_o, loss_target, m_ada_w, m_ada_b, m_norm1_g, m_norm2_g, m_ffn_w_gate, m_ffn_w_up, m_ffn_w_down, m_s5_lam_re, m_s5_lam_im, m_s5_log_dt, m_s5_b_re, m_s5_b_im, m_s5_c_re, m_s5_c_im, m_s5_d, m_s5_w_glu, m_s5_b_glu, m_kv_ada_w, m_kv_ada_b, m_kv_norm_g, m_w_kv_a, m_kv_a_norm_g, m_w_kv_b, m_k_nope_norm_g, m_k_rope_norm_g, m_mla_w_dq, m_mla_q_norm_g, m_mla_w_uq, m_mla_q_nope_norm_g, m_mla_q_rope_norm_g, m_mla_w_o, v_ada_w, v_ada_b, v_norm1_g, v_norm2_g, v_ffn_w_gate, v_ffn_w_up, v_ffn_w_down, v_s5_lam_re, v_s5_lam_im, v_s5_log_dt, v_s5_b_re, v_s5_b_im, v_s5_c_re, v_s5_c_im, v_s5_d, v_s5_w_glu, v_s5_b_glu, v_kv_ada_w, v_kv_ada_b, v_kv_norm_g, v_w_kv_a, v_kv_a_norm_g, v_w_kv_b, v_k_nope_norm_g, v_k_rope_norm_g, v_mla_w_dq, v_mla_q_norm_g, v_mla_w_uq, v_mla_q_nope_norm_g, v_mla_q_rope_norm_g, v_mla_w_o):
    W = dict(ada_w=ada_w, ada_b=ada_b, norm1_g=norm1_g, norm2_g=norm2_g, ffn_w_gate=ffn_w_gate, ffn_w_up=ffn_w_up, ffn_w_down=ffn_w_down, s5_lam_re=s5_lam_re, s5_lam_im=s5_lam_im, s5_log_dt=s5_log_dt, s5_b_re=s5_b_re, s5_b_im=s5_b_im, s5_c_re=s5_c_re, s5_c_im=s5_c_im, s5_d=s5_d, s5_w_glu=s5_w_glu, s5_b_glu=s5_b_glu, kv_ada_w=kv_ada_w, kv_ada_b=kv_ada_b, kv_norm_g=kv_norm_g, w_kv_a=w_kv_a, kv_a_norm_g=kv_a_norm_g, w_kv_b=w_kv_b, k_nope_norm_g=k_nope_norm_g, k_rope_norm_g=k_rope_norm_g, mla_w_dq=mla_w_dq, mla_q_norm_g=mla_q_norm_g, mla_w_uq=mla_w_uq, mla_q_nope_norm_g=mla_q_nope_norm_g, mla_q_rope_norm_g=mla_q_rope_norm_g, mla_w_o=mla_w_o)
    M = dict(ada_w=m_ada_w, ada_b=m_ada_b, norm1_g=m_norm1_g, norm2_g=m_norm2_g, ffn_w_gate=m_ffn_w_gate, ffn_w_up=m_ffn_w_up, ffn_w_down=m_ffn_w_down, s5_lam_re=m_s5_lam_re, s5_lam_im=m_s5_lam_im, s5_log_dt=m_s5_log_dt, s5_b_re=m_s5_b_re, s5_b_im=m_s5_b_im, s5_c_re=m_s5_c_re, s5_c_im=m_s5_c_im, s5_d=m_s5_d, s5_w_glu=m_s5_w_glu, s5_b_glu=m_s5_b_glu, kv_ada_w=m_kv_ada_w, kv_ada_b=m_kv_ada_b, kv_norm_g=m_kv_norm_g, w_kv_a=m_w_kv_a, kv_a_norm_g=m_kv_a_norm_g, w_kv_b=m_w_kv_b, k_nope_norm_g=m_k_nope_norm_g, k_rope_norm_g=m_k_rope_norm_g, mla_w_dq=m_mla_w_dq, mla_q_norm_g=m_mla_q_norm_g, mla_w_uq=m_mla_w_uq, mla_q_nope_norm_g=m_mla_q_nope_norm_g, mla_q_rope_norm_g=m_mla_q_rope_norm_g, mla_w_o=m_mla_w_o)
    V = dict(ada_w=v_ada_w, ada_b=v_ada_b, norm1_g=v_norm1_g, norm2_g=v_norm2_g, ffn_w_gate=v_ffn_w_gate, ffn_w_up=v_ffn_w_up, ffn_w_down=v_ffn_w_down, s5_lam_re=v_s5_lam_re, s5_lam_im=v_s5_lam_im, s5_log_dt=v_s5_log_dt, s5_b_re=v_s5_b_re, s5_b_im=v_s5_b_im, s5_c_re=v_s5_c_re, s5_c_im=v_s5_c_im, s5_d=v_s5_d, s5_w_glu=v_s5_w_glu, s5_b_glu=v_s5_b_glu, kv_ada_w=v_kv_ada_w, kv_ada_b=v_kv_ada_b, kv_norm_g=v_kv_norm_g, w_kv_a=v_w_kv_a, kv_a_norm_g=v_kv_a_norm_g, w_kv_b=v_w_kv_b, k_nope_norm_g=v_k_nope_norm_g, k_rope_norm_g=v_k_rope_norm_g, mla_w_dq=v_mla_w_dq, mla_q_norm_g=v_mla_q_norm_g, mla_w_uq=v_mla_w_uq, mla_q_nope_norm_g=v_mla_q_nope_norm_g, mla_q_rope_norm_g=v_mla_q_rope_norm_g, mla_w_o=v_mla_w_o)
    return _step(x[0], c, positions, loss_target[0], W, M, V)


WEIGHT_NAMES = ['ada_w', 'ada_b', 'norm1_g', 'norm2_g', 'ffn_w_gate', 'ffn_w_up', 'ffn_w_down', 's5_lam_re', 's5_lam_im', 's5_log_dt', 's5_b_re', 's5_b_im', 's5_c_re', 's5_c_im', 's5_d', 's5_w_glu', 's5_b_glu', 'kv_ada_w', 'kv_ada_b', 'kv_norm_g', 'w_kv_a', 'kv_a_norm_g', 'w_kv_b', 'k_nope_norm_g', 'k_rope_norm_g', 'mla_w_dq', 'mla_q_norm_g', 'mla_w_uq', 'mla_q_nope_norm_g', 'mla_q_rope_norm_g', 'mla_w_o']
REPLICATED = ['ada_b', 'norm1_g', 'norm2_g', 's5_lam_re', 's5_lam_im', 's5_log_dt', 's5_b_re', 's5_b_im', 's5_c_re', 's5_c_im', 'kv_ada_b', 'kv_norm_g', 'kv_a_norm_g', 'k_nope_norm_g', 'k_rope_norm_g', 'mla_q_norm_g', 'mla_q_nope_norm_g', 'mla_q_rope_norm_g']
SHARDED_VEC = ['s5_d', 's5_b_glu']


def _step(x, c, positions, target, W, M, V):
    s = x.shape[0]
    me = _index(*_me())
    mxu = lambda a: a.astype(_MXU)

    pad_c = lambda a: jnp.pad(a, ((0, 0), (0, FFB - FF // N_DEV)))
    pad_r = lambda a: jnp.pad(a, ((0, FFB - FF // N_DEV), (0, 0)))
    cols = lambda g: g.transpose(1, 0, 2).reshape(g.shape[1], N_DEV * g.shape[2])
    rows = lambda g: g.reshape(N_DEV * g.shape[1], g.shape[2])

    def local_pack(l):
        second = W['s5_w_glu'][l] if l < N_A else W['mla_w_o'][l - N_A]
        arrs = [jnp.concatenate([mxu(pad_c(W['ffn_w_gate'][l])), mxu(pad_c(W['ffn_w_up'][l]))], axis=0),
                jnp.concatenate([mxu(pad_r(W['ffn_w_down'][l])), mxu(second)], axis=0)]
        if l == N_A:
            arrs += [jnp.concatenate([mxu(W['w_kv_b']), mxu(W['mla_w_dq'][0])], axis=0), mxu(W['w_kv_a'])]
        if l > N_A:
            arrs += [mxu(W['mla_w_dq'][l - N_A])]
        if l >= N_A:
            arrs += [mxu(W['mla_w_uq'][l - N_A])]
        return arrs


    def layer_weights(l, after):
        lands = exchange_wait(f"gather_wait_{l}", gathers[l], after, True)
        full = [lax.dynamic_update_slice(ld, src[None], (me,) + (0,) * src.ndim) for ld, src in zip(lands, gathers[l][2])]
        w = {'wg': cols(full[0][:, :D]), 'wu': cols(full[0][:, D:]), 'wd': rows(full[1][:, :FFB]),
             'second': rows(full[1][:, FFB:])}
        if l >= N_A:
            if l == N_A:
                wkvb3 = cols(full[2][:, :KVL]).reshape(KVL, H, DN + DV)
                wkva = rows(full[3])
                w['wa_pad'] = jnp.concatenate([wkva[:, :KVL], jnp.zeros((D, DN), _MXU), wkva[:, KVL:],
                                               jnp.zeros((D, HD - DN - DR), _MXU)], axis=1)
                w['wkn_pad'] = jnp.pad(wkvb3[:, :, :DN], ((0, 0), (0, 0), (0, HD - DN))).reshape(KVL, H * HD)
                w['wv'] = wkvb3[:, :, DN:].reshape(KVL, H * DV)
                w['wdq'] = rows(full[2][:, KVL:])
            else:
                w['wdq'] = rows(full[2])
            w['wuq_pad'] = _pad_heads(cols(full[-1]), DN + DR, HD)
        return w

    vec = jnp.concatenate([c.reshape(-1), W['s5_d'].reshape(-1), W['s5_b_glu'].reshape(-1)]).reshape(1, -1)
    vec = jnp.pad(vec, ((0, 7), (0, 0)))
    gv = all_gather("gather_vectors", vec)[:, 0, :]
    c_all = gv[:, :D]
    d_full = jnp.concatenate([gv[d, D:D + 2 * 128].reshape(N_A, 128) for d in range(N_DEV)], axis=1)
    bglu_full = jnp.concatenate([gv[d, D + 256:D + 512].reshape(N_A, 128) for d in range(N_DEV)], axis=1)

    ca_all = jax.nn.silu(c_all)
    w_mod = jnp.concatenate([W['ada_w'][l] for l in range(DEPTH)] + [W['kv_ada_w']], axis=1)
    n_mod = w_mod.shape[1]
    mod_cols = small_matmul("mod_matmul", ca_all, w_mod)
    gm = all_gather("gather_mod", mod_cols)
    gathers = [exchange_start(f"gather_start_{l}", local_pack(l), True, gm) for l in range(DEPTH)]
    tokens = sum(g[4][0, 0] for g in gathers)
    mine = lax.dynamic_index_in_dim(gm, me, axis=1, keepdims=False) + tokens
    per_l = D * 6 // N_DEV
    mods = []
    for l in range(DEPTH):
        full = jnp.concatenate([mine[d, per_l * l:per_l * (l + 1)] for d in range(N_DEV)]) + W['ada_b'][l]
        mods.append([_row(full[D * i:D * (i + 1)]) for i in range(6)])
    kfull = jnp.concatenate([mine[d, per_l * DEPTH:] for d in range(N_DEV)]) + W['kv_ada_b']
    k_shift, k_scale = _row(kfull[:D]), _row(kfull[D:])

    inv = 1.0 / (ROPE_THETA ** (np.arange(0, DR, 2, dtype=np.float32) / DR))
    inv128 = np.zeros((1, HD), np.float32)
    inv128[0, DN:DN + DR // 2] = inv
    inv128[0, DN + DR // 2:DN + DR] = inv
    cosf, sinf = rope_tables("rope_tables", positions.reshape(s, 1), jnp.asarray(inv128))
    zpad = lambda n: jnp.zeros((n,), F32)
    gkn128 = _row(jnp.concatenate([W['k_nope_norm_g'], zpad(HD - DN)]))
    gkr128 = _row(jnp.concatenate([zpad(DN), W['k_rope_norm_g'], zpad(HD - DN - DR)]))
    gq128 = [_row(jnp.concatenate([W['mla_q_nope_norm_g'][j], W['mla_q_rope_norm_g'][j], zpad(HD - DN - DR)]))
             for j in range(2)]

    expand = jnp.asarray(np.kron(np.eye(G, dtype=np.float32), np.ones((1, N), np.float32)))
    s5_raw, s5_mats = [], []
    for l in range(N_A):
        raw = (_row(W['s5_lam_re'][l]), _row(W['s5_lam_im'][l]), _row(W['s5_log_dt'][l]),
               W['s5_b_re'][l].transpose(2, 0, 1).reshape(P, G * N), W['s5_b_im'][l].transpose(2, 0, 1).reshape(P, G * N))
        ab_re, ab_im, bb_re_t, bb_im_t = s5_prep_fwd(f"s5_prep_fwd", *raw, expand)
        s5_raw.append(raw)
        s5_mats.append(_s5_place(ab_re, ab_im, bb_re_t, bb_im_t, W['s5_c_re'][l], W['s5_c_im'][l]))

    g1 = [_row(W['norm1_g'][l]) for l in range(DEPTH)]
    g2 = [_row(W['norm2_g'][l]) for l in range(DEPTH)]
    saved = []
    xs = x
    kv = None
    lw = [None] * DEPTH
    for l in range(DEPTH):
        sh1, sc1, gt1, sh2, sc2, gt2 = mods[l]
        rec = {'x_in': xs}
        if l >= N_A:
            lw[l] = layer_weights(l, xs)
        if l == N_A:
            kv_smalls = [_row(W['kv_norm_g']), k_shift, k_scale, _row(W['kv_a_norm_g']), gkn128, gkr128]
            kv_w = [lw[l]['wa_pad'], lw[l]['wkn_pad'], lw[l]['wv']]
            k_mat, v_mat = seg_forward("kv_fwd", seg_kv, [xs], kv_smalls, [cosf, sinf], kv_w,
                                       [(H * HD, _MXU), (H * DV, _MXU)], tap_widths=(KVL + HD, H * HD, H * DV))
            kv = {'x_in': xs, 'smalls': kv_smalls, 'k': k_mat, 'v': v_mat, 'w': kv_w}
        if l < N_A:
            (h,) = seg_forward("pre_fwd", seg_pre, [xs], [g1[l], sh1, sc1], [], [], [(D, F32)])
            wb, wc, a_tab = s5_mats[l]
            y, s0 = s5_scan_fwd("s5_scan_fwd", h, wb, wc, a_tab, _row(d_full[l]))
            lw[l] = layer_weights(l, y)
            (x_mid,) = seg_forward("glu_fwd", seg_glu, [xs, y], [gt1, _row(bglu_full[l])], [], [lw[l]['second']],
                                   [(D, F32)], tap_widths=(D,))
            rec.update(h=h, y=y, s0=s0)
        else:
            j = l - N_A
            q_smalls = [g1[l], sh1, sc1, _row(W['mla_q_norm_g'][j]), gq128[j]]
            (q_mat,) = seg_forward("q_fwd", seg_q, [xs], q_smalls, [cosf, sinf], [lw[l]['wdq'], lw[l]['wuq_pad']],
                                   [(H * HD, _MXU)], tap_widths=(QL, H * HD))
            o_mat, lse = attn_fwd("attn_fwd", q_mat, kv['k'], kv['v'])
            (x_mid,) = seg_forward("o_fwd", seg_o, [xs, o_mat], [gt1], [], [lw[l]['second']], [(D, F32)],
                                   tap_widths=(D,))
            rec.update(q=q_mat, o=o_mat, lse=lse, q_smalls=q_smalls)
        rec['x_mid'] = x_mid
        xs, rec['gate'], rec['up'] = ffn_forward("ffn_fwd", x_mid, g2[l], sh2, sc2, gt2,
                                                 lw[l]['wg'], lw[l]['wu'], lw[l]['wd'])
        saved.append(rec)

    dy, loss_part = loss_kernel("loss", xs, target)
    loss = lax.psum(loss_part[0, 0], ("x", "y", "c"))

    rblk = lambda a: a.reshape(N_DEV, a.shape[0] // N_DEV, a.shape[1])
    cblk = lambda a: a.reshape(a.shape[0], N_DEV, a.shape[1] // N_DEV).transpose(1, 0, 2)
    dmod = [None] * DEPTH
    dk_tot = []
    dv_tot = []
    dx = dy
    sends = [None] * DEPTH
    send_token = jnp.zeros((1, 1), F32)
    g_n1 = [None] * DEPTH
    g_n2 = [None] * DEPTH
    g_bglu = [None] * N_A
    g_dskip = [None] * N_A
    g_s5 = [None] * N_A
    g_qn, g_q128 = [None] * 2, [None] * 2
    for l in range(DEPTH - 1, -1, -1):
        rec = saved[l]
        sh1, sc1, gt1, sh2, sc2, gt2 = mods[l]
        dx, dgate, dup, dyd, h_b, a_b, dg2, dsh2, dsc2, dgt2 = ffn_backward(
            "ffn_bwd", rec['x_mid'], dx, rec['gate'], rec['up'], g2[l], sh2, sc2, gt2 + send_token,
            lw[l]['wg'], lw[l]['wu'], lw[l]['wd'])
        out_l = [matmul_tn("tn_ffn_in", h_b, dgate, _MXU, col_blocks=N_DEV),
                 matmul_tn("tn_ffn_in", h_b, dup, _MXU, col_blocks=N_DEV),
                 matmul_tn("tn_ffn_out", a_b, dyd, _MXU).reshape(N_DEV, FFB, D)]
        g_n2[l] = dg2
        if l == 0:
            sends_ffn0 = exchange_start("a2a_start_ffn0", out_l, False, dx)
            send_token = sends_ffn0[4][0:1, 0:1]
            out_l = []
        if l < N_A:
            (dx, dyy), (dz,), (g_b,), (dgt1, dbg) = seg_backward(
                "glu_bwd", seg_glu, [rec['x_in'], rec['y']], [gt1 + (send_token if l == 0 else 0.0), _row(bglu_full[l])], [],
                [lw[l]['second']],
                [dx], (D,), (D,))
            out_l.append(rblk(matmul_tn("tn_sq", g_b, dz, _MXU)))
            g_bglu[l] = dbg
            wb, wc, a_tab = s5_mats[l]
            dh, dwb, dwc, da, dd = s5_scan_bwd("s5_scan_bwd", rec['h'], dyy, rec['s0'], wb, wc, a_tab, _row(d_full[l]))
            g_dskip[l] = dd
            dab_re, dab_im, dbb_re_t, dbb_im_t, dc_re, dc_im = _s5_unplace(dwb, dwc, da)
            dlr, dli, dldt, dbr_t, dbi_t = s5_prep_bwd("s5_prep_bwd", *s5_raw[l], expand,
                                                       (dab_re, dab_im, dbb_re_t, dbb_im_t))
            g_s5[l] = (dlr.reshape(G, N), dli.reshape(G, N), dldt.reshape(G),
                       dbr_t.reshape(P, G, N).transpose(1, 2, 0), dbi_t.reshape(P, G, N).transpose(1, 2, 0), dc_re, dc_im)
            (dx,), _, _, (dg1, dsh1, dsc1) = seg_backward(
                "pre_bwd", seg_pre, [rec['x_in']], [g1[l], sh1, sc1], [], [], [dh], (), (), dx_add=dx)
        else:
            j = l - N_A
            (dx, do), (dzo,), (o_b,), (dgt1,) = seg_backward(
                "o_bwd", seg_o, [rec['x_in'], rec['o']], [gt1], [], [lw[l]['second']], [dx], (D,), (D,))
            out_l.append(rblk(matmul_tn("tn_sq", o_b, dzo, _MXU)))
            dq, dk, dv = attn_bwd("attn_bwd", rec['q'], kv['k'], kv['v'], rec['o'], do, rec['lse'])
            dk_tot.append(dk)
            dv_tot.append(dv)
            (dx,), (dql, dqq), (hq_b, qn_b), (dg1, dsh1, dsc1, dqg, dq128) = seg_backward(
                "q_bwd", seg_q, [rec['x_in']], rec['q_smalls'], [cosf, sinf], [lw[l]['wdq'], lw[l]['wuq_pad']],
                [dq], (QL, H * HD), (D, QL), dx_add=dx)
            g_dq = rblk(matmul_tn("tn_dq", hq_b, dql, _MXU))
            g_uq = cblk(_unpad_heads(matmul_tn("tn_uq", qn_b, dqq, _MXU), DN + DR, HD))
            g_qn[j], g_q128[j] = dqg, dq128
        g_n1[l] = dg1
        dmod[l] = jnp.concatenate([dsh1, dsc1, dgt1, dsh2, dsc2, dgt2], axis=1)
        if l == N_A:
            (dx,), (dta, dtk, dtv), (hk_b, ckv_b), (dkg, dksh, dksc, dag, dgkn, dgkr) = seg_backward(
                "kv_bwd", seg_kv, [kv['x_in']], kv['smalls'], [cosf, sinf], kv['w'],
                [dk_tot, dv_tot], (KVL + HD, H * HD, H * DV), (D, KVL), dx_add=dx)
            g_wa = matmul_tn("tn_kva", hk_b, dta, _MXU)
            g_wa = jnp.concatenate([g_wa[:, :KVL], g_wa[:, KVL + DN:KVL + DN + DR]], axis=1)
            g_kn = matmul_tn("tn_kn", ckv_b, dtk, _MXU).reshape(KVL, H, HD)[:, :, :DN]
            g_v = matmul_tn("tn_v", ckv_b, dtv, _MXU).reshape(KVL, H, DV)
            g_wkvb = jnp.concatenate([g_kn, g_v], axis=2).reshape(KVL, H * (DN + DV))
            dkmod = jnp.concatenate([dksh, dksc], axis=1)
            out_l += [jnp.concatenate([cblk(g_wkvb), g_dq], axis=1), rblk(g_wa)]
        if l > N_A:
            out_l.append(g_dq)
        if l >= N_A:
            out_l.append(g_uq)
        if l > 0:
            sends[l] = exchange_start(f"a2a_start_{l}", out_l, False, dx)
            send_token = sends[l][4][0:1, 0:1]
        if l == N_A - 1:
            early_flat = _pack_rows([a.reshape(-1) for a in g_s5[l]])
            early_st = exchange_start("small_start_s5", [early_flat], True, dx)
            send_token = send_token + early_st[4][0:1, 0:1]
    grad_x = dx

    s5_names = ['s5_lam_re', 's5_lam_im', 's5_log_dt', 's5_b_re', 's5_b_im', 's5_c_re', 's5_c_im']
    small = {
        'norm1_g': jnp.concatenate(g_n1, axis=0), 'norm2_g': jnp.concatenate(g_n2, axis=0),
        'kv_norm_g': dkg, 'kv_a_norm_g': dag, 'k_nope_norm_g': dgkn[:, :DN], 'k_rope_norm_g': dgkr[:, DN:DN + DR],
        'mla_q_norm_g': jnp.concatenate(g_qn, axis=0),
        'mla_q_nope_norm_g': jnp.concatenate([g[:, :DN] for g in g_q128], axis=0),
        'mla_q_rope_norm_g': jnp.concatenate([g[:, DN:DN + DR] for g in g_q128], axis=0),
        's5_d': jnp.concatenate(g_dskip, axis=0), 's5_b_glu': jnp.concatenate(g_bglu, axis=0),
    }
    for i, n in enumerate(s5_names):
        small[n] = jnp.stack([g_s5[l][i] for l in range(N_A - 1)])
    small_names = [n for n in REPLICATED if n not in ('ada_b', 'kv_ada_b')] + SHARDED_VEC
    flat_small = _pack_rows([small[n].reshape(-1) for n in small_names])

    dm = jnp.concatenate(dmod + [dkmod], axis=1)[0]
    per_dev = []
    for d in range(N_DEV):
        cols = [dm[6 * D * l + per_l * d:6 * D * l + per_l * (d + 1)] for l in range(DEPTH)]
        cols.append(dm[6 * D * DEPTH + (2 * D // N_DEV) * d:6 * D * DEPTH + (2 * D // N_DEV) * (d + 1)])
        per_dev.append(jnp.concatenate(cols))
    dm_dev = jnp.stack(per_dev)
    gdm = all_gather("gather_dmod", dm_dev)
    small_st = exchange_start("small_start", [flat_small], True, gdm)
    sends[0] = exchange_start("a2a_start_0", out_l, False, small_st[4])
    dm_mine = lax.dynamic_index_in_dim(gdm, me, axis=1, keepdims=False) + sends[0][4][0, 0]
    g_wmod = small_matmul_tn("dmod_matmul", ca_all, dm_mine)
    g_ada_w = jnp.stack([g_wmod[:, per_l * l:per_l * (l + 1)] for l in range(DEPTH)])
    g_kv_ada_w = g_wmod[:, per_l * DEPTH:]
    dm_sum = sum_parts("sum_dmod", gdm.reshape(N_DEV, N_DEV, n_mod))
    g_ada_b = jnp.stack([jnp.concatenate([dm_sum[d, per_l * l:per_l * (l + 1)] for d in range(N_DEV)])
                         for l in range(DEPTH)])
    g_kv_ada_b = jnp.concatenate([dm_sum[d, per_l * DEPTH:] for d in range(N_DEV)])

    grads, out_delta, out_m, out_v = {}, {}, {}, {}

    def update(name, parts, base=0, stride=0):
        shp = W[name].shape
        shp3 = shp if len(shp) == 3 else (1,) + shp
        res = adamw("adamw_" + name, parts, W[name].reshape(shp3), M[name].reshape(shp3), V[name].reshape(shp3),
                    base, stride)
        grads[name], out_delta[name], out_m[name], out_v[name] = (a.reshape(shp) for a in res)

    update('ada_w', g_ada_w.reshape(1, DEPTH * D, per_l), 0, D)
    update('kv_ada_w', g_kv_ada_w[None])

    chains = {}

    def update_layer(name, parts, layer, base=0):
        shp = W[name].shape
        shp3 = shp if len(shp) == 3 else (1,) + shp
        chains[name] = adamw_layer(f"adamw_{name}_{layer}", parts, W[name].reshape(shp3), M[name].reshape(shp3),
                                   V[name].reshape(shp3), layer, chains.get(name), base)
        grads[name], out_delta[name], out_m[name], out_v[name] = (a.reshape(shp) for a in chains[name])

    ffn_parts = [[None] * DEPTH for _ in range(3)]

    def landed(name, started, after):
        lands = exchange_wait(name, started, after, False)
        return [lax.dynamic_update_slice(ld, lax.dynamic_index_in_dim(src, me, 0, keepdims=True), (me,) + (0,) * (src.ndim - 1))
                for ld, src in zip(lands, started[2])]

    def receive(l, after):
        recv = landed(f"a2a_wait_{l}", sends[l], after)
        if l == 0:
            recv = landed("a2a_wait_ffn0", sends_ffn0, after) + recv
        for i in range(3):
            ffn_parts[i][l] = recv[i]
        if l < N_A:
            update_layer('s5_w_glu', recv[3], l)
        else:
            update_layer('mla_w_o', recv[3], l - N_A)
            if l == N_A:
                update_layer('w_kv_b', recv[4], 0)
                update_layer('mla_w_dq', recv[4], 0, KVL)
                update_layer('w_kv_a', recv[5], 0)
            else:
                update_layer('mla_w_dq', recv[4], l - N_A)
            update_layer('mla_w_uq', recv[-1], l - N_A)

    for l in range(DEPTH - 1, 0, -1):
        receive(l, out_delta['kv_ada_w'])

    def gathered_sum(name, started, own, after):
        (land,) = exchange_wait(name + "_wait", started, after, True)
        return sum_parts("sum_" + name, lax.dynamic_update_slice(land, own[None], (me, 0, 0))).reshape(-1)

    early_sum = gathered_sum("small_s5", early_st, early_flat, chains['s5_w_glu'][1])
    g_small_sum = gathered_sum("small", small_st, flat_small, early_sum)
    off = 0
    for n in small_names:
        size = int(np.prod(small[n].shape))
        full = g_small_sum[off:off + size]
        off += size
        if n in SHARDED_VEC:
            full = lax.dynamic_slice_in_dim(full.reshape(N_A, D), me * (D // N_DEV), D // N_DEV, axis=1)
        grads[n] = full.reshape(small[n].shape if n in s5_names else W[n].shape)
    off = 0
    for i, n in enumerate(s5_names):
        size = int(np.prod(g_s5[N_A - 1][i].shape))
        last = early_sum[off:off + size].reshape((1,) + g_s5[N_A - 1][i].shape)
        off += size
        grads[n] = jnp.concatenate([grads[n], last], axis=0)
    grads['ada_b'] = g_ada_b
    grads['kv_ada_b'] = g_kv_ada_b

    big_small = ('s5_b_re', 's5_b_im', 's5_c_re', 's5_c_im')
    packed_names = [n for n in REPLICATED + SHARDED_VEC if n not in big_small]

    def pack(dct):
        flat_ = jnp.concatenate([dct[n].reshape(-1) for n in packed_names])
        n_ = int(flat_.shape[0])
        p_ = -(-n_ // 8192) * 8192
        return jnp.pad(flat_, (0, p_ - n_)).reshape(p_ // 128, 128)

    _, d_p, m_p, v_p = adamw("adamw_small", pack(grads)[None], pack(W)[None], pack(M)[None], pack(V)[None])
    off = 0
    d_p, m_p, v_p = d_p.reshape(-1), m_p.reshape(-1), v_p.reshape(-1)
    for n in packed_names:
        size = int(np.prod(W[n].shape))
        out_delta[n] = d_p[off:off + size].reshape(W[n].shape)
        out_m[n] = m_p[off:off + size].reshape(W[n].shape)
        out_v[n] = v_p[off:off + size].reshape(W[n].shape)
        off += size
    for n in big_small:
        shp = W[n].shape
        view = (1, int(np.prod(shp[:-1])), shp[-1])
        res = adamw("adamw_" + n, grads[n].reshape(view), W[n].reshape(view), M[n].reshape(view), V[n].reshape(view))
        _, out_delta[n], out_m[n], out_v[n] = (a.reshape(shp) for a in res)

    receive(0, d_p)
    for i, name in enumerate(('ffn_w_gate', 'ffn_w_up', 'ffn_w_down')):
        res = adamw_multi("adamw_" + name, ffn_parts[i], W[name], M[name], V[name])
        grads[name], out_delta[name], out_m[name], out_v[name] = res

    return (loss, grad_x[None], *[grads[n] for n in WEIGHT_NAMES], *[out_delta[n] for n in WEIGHT_NAMES],
            *[out_m[n] for n in WEIGHT_NAMES], *[out_v[n] for n in WEIGHT_NAMES])
```

```python
import functools
import math

import numpy as np
import jax
import jax.numpy as jnp
from jax import lax
from jax.experimental import pallas as pl
from jax.experimental.pallas import tpu as pltpu

F32 = jnp.float32
_MXU = jnp.bfloat16
HI = lax.Precision.HIGHEST

D = 1024
DEPTH = 4
N_A = 2
FF = 2816
FFB = 384
FFP = 8 * FFB
N_DEV = 8
G = 64
P = 16
N = 64
GB = 8
NBLK = G // GB
HALF = GB * N
H = 16
HP = H // 2
DN, DR, DV = 64, 32, 64
HD = 128
QL = 256
KVL = 256
CHUNK = 64
ROPE_THETA = 10000.0
ATTN_SCALE = 1.0 / math.sqrt(DN + DR)
LOG2E = 1.4426950408889634
EXP2_SCALE = ATTN_SCALE * LOG2E
EPS = 1e-6
ADAM_LR, ADAM_B1, ADAM_B2, ADAM_EPS, ADAM_WD, ADAM_STEP = 0.001, 0.9, 0.999, 1e-08, 0.01, 10
VMEM_LIMIT = 56 * 1024 * 1024
MESH = pl.DeviceIdType.MESH

TILE_ROW = 256
TILE_ATT = 512
TILE_SCAN = 512


def _params(n_grid, fuse_inputs=None):
    return pltpu.CompilerParams(dimension_semantics=("arbitrary",) * n_grid, vmem_limit_bytes=VMEM_LIMIT,
                                allow_input_fusion=fuse_inputs)


@jax.custom_vjp
def mm(a, w):
    return jnp.dot(a.astype(_MXU), w, preferred_element_type=F32)


def _mm_fwd(a, w):
    return mm(a, w), w


def _mm_bwd(w, g):
    da = lax.dot_general(g.astype(_MXU), w, (((1,), (1,)), ((), ())), preferred_element_type=F32)
    return da, jnp.zeros_like(w)


mm.defvjp(_mm_fwd, _mm_bwd)


def rms(x, g):
    return x * lax.rsqrt(jnp.mean(x * x, axis=-1, keepdims=True) + EPS) * g


def modulate(h, shift, scale):
    return h * (1.0 + scale) + shift


def _lane(n=HD):
    return lax.broadcasted_iota(jnp.int32, (1, n), 1)


def _rot_matrix():
    r = lax.broadcasted_iota(jnp.int32, (HD, HD), 0)
    c = lax.broadcasted_iota(jnp.int32, (HD, HD), 1)
    first = (c >= DN) & (c < DN + DR // 2) & (r == c + DR // 2)
    second = (c >= DN + DR // 2) & (c < DN + DR) & (r == c - DR // 2)
    return jnp.where(first, -1.0, jnp.where(second, 1.0, 0.0)).astype(F32)


def head_norm_rope(xh, g128, cosf, sinf, rot, with_nope):
    lane = _lane()
    m_n = lane < DN
    m_r = (lane >= DN) & (lane < DN + DR)
    sq = xh * xh
    inv_r = lax.rsqrt(jnp.sum(jnp.where(m_r, sq, 0.0), axis=-1, keepdims=True) / DR + EPS)
    if with_nope:
        inv_n = lax.rsqrt(jnp.sum(jnp.where(m_n, sq, 0.0), axis=-1, keepdims=True) / DN + EPS)
        inv = jnp.where(m_n, inv_n, jnp.where(m_r, inv_r, 0.0))
    else:
        inv = jnp.where(m_r, inv_r, 0.0)
    xg = xh * inv * g128
    return xg * cosf + jnp.dot(xg, rot, precision=HI, preferred_element_type=F32) * sinf


def seg_glu(x, y, gt, b, t_z, w):
    g = jax.nn.gelu(y)
    z = mm(g, w) + b + t_z
    return (x + gt * (g * jax.nn.sigmoid(z)),), (g.astype(_MXU),)


def seg_o(x, o, gt, t_o, w):
    return (x + gt * (mm(o, w) + t_o),), (o.astype(_MXU),)


def seg_q(x, g, sh, sc, qg, g128, t_l, t_q, cosf, sinf, wdq, wuq):
    h = modulate(rms(x, g), sh, sc)
    ql = mm(h, wdq) + t_l
    qn = rms(ql, qg)
    q = mm(qn, wuq) + t_q
    rot = _rot_matrix()
    heads = [head_norm_rope(q[:, HD * i:HD * (i + 1)], g128, cosf, sinf, rot, True) for i in range(H)]
    return (jnp.concatenate(heads, axis=1),), (h.astype(_MXU), qn.astype(_MXU))


def seg_kv(x, g, sh, sc, ag, gkn, gkr, t_a, t_k, t_v, cosf, sinf, wa, wkn, wv):
    hk = modulate(rms(x, g), sh, sc)
    kva = mm(hk, wa) + t_a
    ckv = rms(kva[:, :KVL], ag)
    kr = head_norm_rope(kva[:, KVL:KVL + HD], gkr, cosf, sinf, _rot_matrix(), False)
    kn = mm(ckv, wkn) + t_k
    v = mm(ckv, wv) + t_v
    heads = []
    for i in range(H):
        kh = kn[:, HD * i:HD * (i + 1)]
        inv = lax.rsqrt(jnp.sum(kh * kh, axis=-1, keepdims=True) / DN + EPS)
        heads.append(kh * inv * gkn + kr)
    return (jnp.concatenate(heads, axis=1), v), (hk.astype(_MXU), ckv.astype(_MXU))


def _row_call(name, body_fn, rows, fulls, out_rows, out_accs, tile):
    s = rows[0].shape[0]
    n_tiles = s // tile
    n_rows, n_fulls, n_or, n_oa = len(rows), len(fulls), len(out_rows), len(out_accs)

    def kern(*refs):
        i = pl.program_id(0)
        row_v = [r[...] for r in refs[:n_rows]]
        full_v = [r[...] for r in refs[n_rows:n_rows + n_fulls]]
        o_refs = refs[n_rows + n_fulls:]
        ro, ao = body_fn(row_v, full_v)
        for r, v in zip(o_refs[:n_or], ro):
            r[...] = v.astype(r.dtype)
        if n_oa:
            @pl.when(i == 0)
            def _():
                for r in o_refs[n_or:]:
                    r[...] = jnp.zeros(r.shape, r.dtype)
            for r, v in zip(o_refs[n_or:], ao):
                r[...] += v.astype(r.dtype)

    in_specs = [pl.BlockSpec((tile, a.shape[1]), lambda i: (i, 0)) for a in rows]
    for a in fulls:
        big = a.size * a.dtype.itemsize > (1 << 20)
        nd = a.ndim
        in_specs.append(pl.BlockSpec(a.shape, functools.partial(lambda i, nd_: (0,) * nd_, nd_=nd),
                                     **({"pipeline_mode": pl.Buffered(1)} if big else {})))
    out_shape = [jax.ShapeDtypeStruct((s, w), dt) for w, dt in out_rows]
    out_shape += [jax.ShapeDtypeStruct(shp, dt) for shp, dt in out_accs]
    out_specs = [pl.BlockSpec((tile, w), lambda i: (i, 0)) for w, _ in out_rows]
    out_specs += [pl.BlockSpec(shp, functools.partial(lambda i, nd_: (0,) * nd_, nd_=len(shp))) for shp, _ in out_accs]
    fuse = [False] * n_rows + [a.size * a.dtype.itemsize > (1 << 20) for a in fulls]
    res = pl.pallas_call(kern, out_shape=out_shape, grid=(n_tiles,), in_specs=in_specs, out_specs=out_specs,
                         name=name, compiler_params=_params(1, fuse))(*rows, *fulls)
    return list(res)


def seg_forward(name, seg, rows, smalls, consts_rows, consts_full, out_widths, tile=TILE_ROW, tap_widths=()):
    n_r, n_s, n_cr = len(rows), len(smalls), len(consts_rows)

    def body(row_v, full_v):
        t = row_v[0].shape[0]
        taps = [jnp.zeros((t, w), F32) for w in tap_widths]
        outs, _ = seg(*row_v[:n_r], *full_v[:n_s], *taps, *row_v[n_r:], *full_v[n_s:])
        return outs, ()

    return _row_call(name, body, list(rows) + list(consts_rows), list(smalls) + list(consts_full),
                     out_widths, [], tile)


def seg_backward(name, seg, rows, smalls, consts_rows, consts_full, cots, tap_widths, aux_widths,
                 dx_add=None, tile=TILE_ROW):
    cot_groups = [list(c) if isinstance(c, (list, tuple)) else [c] for c in cots]
    cot_flat = [a for grp in cot_groups for a in grp]
    n_r, n_s, n_cr, n_c = len(rows), len(smalls), len(consts_rows), len(cot_flat)
    has_add = dx_add is not None

    def body(row_v, full_v):
        t = row_v[0].shape[0]
        prim_rows = row_v[:n_r]
        c_rows = row_v[n_r:n_r + n_cr]
        cot_v = list(row_v[n_r + n_cr:n_r + n_cr + n_c])
        add_v = row_v[n_r + n_cr + n_c] if has_add else None
        small_v = full_v[:n_s]
        c_full = full_v[n_s:]
        taps = [jnp.zeros((t, w), F32) for w in tap_widths]
        cot_sum = []
        for grp in cot_groups:
            parts = [cot_v.pop(0).astype(F32) for _ in grp]
            cot_sum.append(functools.reduce(lambda x_, y_: x_ + y_, parts))

        def f(*args):
            return seg(*args, *c_rows, *c_full)

        _, vjp_fn, aux = jax.vjp(f, *prim_rows, *small_v, *taps, has_aux=True)
        grads = vjp_fn(tuple(cot_sum))
        d_rows = list(grads[:n_r])
        if has_add:
            d_rows[0] = d_rows[0] + add_v
        d_small = grads[n_r:n_r + n_s]
        d_taps = grads[n_r + n_s:]
        return d_rows + list(d_taps) + list(aux), [jnp.sum(g, axis=0, keepdims=True) if g.shape[0] != 1 else g
                                                   for g in d_small]

    all_rows = list(rows) + list(consts_rows) + cot_flat + ([dx_add] if has_add else [])
    out_rows = [(a.shape[1], F32) for a in rows] + [(w, _MXU) for w in tap_widths] + [(w, _MXU) for w in aux_widths]
    out_accs = [((1, a.shape[1]), F32) for a in smalls]
    res = _row_call(name, body, all_rows, list(smalls) + list(consts_full), out_rows, out_accs, tile)
    n_t, n_a = len(tap_widths), len(aux_widths)
    return res[:n_r], res[n_r:n_r + n_t], res[n_r + n_t:n_r + n_t + n_a], res[n_r + n_t + n_a:]


def _split(n):
    if n <= 1024:
        return n
    for t in (1408, 1024, 768, 512, 256, 128):
        if n % t == 0:
            return t
    raise ValueError(n)


def matmul_tn(name, a, b, out_dtype, col_blocks=None):
    s, k1 = a.shape
    _, k2 = b.shape
    tm, ts = _split(k1), 2048
    if col_blocks is None:
        tn, per_step, wblk = _split(k2), 1, None
    else:
        wblk = k2 // col_blocks
        per_step = max(1, min(col_blocks, 1536 // wblk))
        tn = per_step * wblk
    n_s = s // ts

    def kern(a_ref, b_ref, o_ref, acc_ref):
        k = pl.program_id(2)

        @pl.when(k == 0)
        def _():
            acc_ref[...] = jnp.zeros(acc_ref.shape, F32)

        acc_ref[...] += lax.dot_general(a_ref[...], b_ref[...], (((0,), (0,)), ((), ())),
                                        preferred_element_type=F32)

        @pl.when(k == n_s - 1)
        def _():
            if col_blocks is None:
                o_ref[...] = acc_ref[...].astype(o_ref.dtype)
            else:
                for cb in range(per_step):
                    o_ref[cb] = acc_ref[:, wblk * cb:wblk * (cb + 1)].astype(o_ref.dtype)

    if col_blocks is None:
        out_shape = jax.ShapeDtypeStruct((k1, k2), out_dtype)
        out_spec = pl.BlockSpec((tm, tn), lambda i, j, k: (i, j))
    else:
        out_shape = jax.ShapeDtypeStruct((col_blocks, k1, wblk), out_dtype)
        out_spec = pl.BlockSpec((per_step, tm, wblk), lambda i, j, k: (j, i, 0))
    return pl.pallas_call(
        kern, out_shape=out_shape, grid=(k1 // tm, k2 // tn, n_s),
        in_specs=[pl.BlockSpec((ts, tm), lambda i, j, k: (k, i)), pl.BlockSpec((ts, tn), lambda i, j, k: (k, j))],
        out_specs=out_spec,
        scratch_shapes=[pltpu.VMEM((tm, tn), F32)], name=name, compiler_params=_params(3))(a, b)


def ffn_forward(name, x, g, sh, sc, gt, wg, wu, wd, tile=TILE_ROW):
    s = x.shape[0]
    fp = wg.shape[1]
    blk = 2 * FFB
    n_blk = fp // blk

    def kern(x_ref, g_ref, sh_ref, sc_ref, gt_ref, wg_ref, wu_ref, wd_ref, o_ref, gate_ref, up_ref):
        xv = x_ref[...]
        hb = modulate(rms(xv, g_ref[...]), sh_ref[...], sc_ref[...]).astype(_MXU)
        y = jnp.zeros((tile, D), F32)
        for c in range(n_blk):
            cs = slice(blk * c, blk * (c + 1))
            gate = jnp.dot(hb, wg_ref[:, cs], preferred_element_type=F32)
            up = jnp.dot(hb, wu_ref[:, cs], preferred_element_type=F32)
            gate_ref[:, cs] = gate.astype(_MXU)
            up_ref[:, cs] = up.astype(_MXU)
            y = y + jnp.dot((jax.nn.silu(gate) * up).astype(_MXU), wd_ref[cs, :], preferred_element_type=F32)
        o_ref[...] = xv + gt_ref[...] * y

    row = lambda w: pl.BlockSpec((tile, w), lambda i: (i, 0))
    vec = pl.BlockSpec((1, D), lambda i: (0, 0))
    wspec = lambda a: pl.BlockSpec(a.shape, lambda i: (0, 0), pipeline_mode=pl.Buffered(1))
    return pl.pallas_call(
        kern, out_shape=[jax.ShapeDtypeStruct((s, D), F32), jax.ShapeDtypeStruct((s, fp), _MXU),
                         jax.ShapeDtypeStruct((s, fp), _MXU)],
        grid=(s // tile,), in_specs=[row(D), vec, vec, vec, vec, wspec(wg), wspec(wu), wspec(wd)],
        out_specs=[row(D), row(fp), row(fp)], name=name,
        compiler_params=_params(1, [False] * 5 + [True] * 3))(x, g, sh, sc, gt, wg, wu, wd)


def ffn_backward(name, x, dxo, gate, up, g, sh, sc, gt, wg, wu, wd, tile=TILE_ROW):
    s = x.shape[0]
    fp = wg.shape[1]
    blk = 2 * FFB
    n_blk = fp // blk

    def kern(x_ref, dxo_ref, gate_ref, up_ref, g_ref, sh_ref, sc_ref, gt_ref, wg_ref, wu_ref, wd_ref,
             dx_ref, dg_ref, du_ref, dy_ref, h_ref, a_ref, dgn_ref, dsh_ref, dsc_ref, dgt_ref):
        i = pl.program_id(0)

        @pl.when(i == 0)
        def _():
            for r in (dgn_ref, dsh_ref, dsc_ref, dgt_ref):
                r[...] = jnp.zeros(r.shape, F32)

        dxo = dxo_ref[...]
        h, pre_vjp = jax.vjp(lambda *p: modulate(rms(p[0], p[1]), p[2], p[3]), x_ref[...], g_ref[...], sh_ref[...],
                             sc_ref[...])
        h_ref[...] = h.astype(_MXU)
        dyb = (gt_ref[...] * dxo).astype(_MXU)
        dy_ref[...] = dyb
        y = jnp.zeros((tile, D), F32)
        dh = jnp.zeros((tile, D), F32)
        tr = (((1,), (1,)), ((), ()))
        for c in range(n_blk):
            cs = slice(blk * c, blk * (c + 1))
            gate = gate_ref[:, cs].astype(F32)
            up = up_ref[:, cs].astype(F32)
            sig = jax.nn.sigmoid(gate)
            sl = gate * sig
            ab = (sl * up).astype(_MXU)
            a_ref[:, cs] = ab
            y = y + jnp.dot(ab, wd_ref[cs, :], preferred_element_type=F32)
            da = lax.dot_general(dyb, wd_ref[cs, :], tr, preferred_element_type=F32)
            dgb = (da * up * (sig * (1.0 + gate * (1.0 - sig)))).astype(_MXU)
            dub = (da * sl).astype(_MXU)
            dg_ref[:, cs] = dgb
            du_ref[:, cs] = dub
            dh = dh + lax.dot_general(dgb, wg_ref[:, cs], tr, preferred_element_type=F32) \
                + lax.dot_general(dub, wu_ref[:, cs], tr, preferred_element_type=F32)
        dgt_ref[...] += jnp.sum(dxo * y, axis=0, keepdims=True)
        dx_pre, dgn, dsh, dsc = pre_vjp(dh)
        dx_ref[...] = dxo + dx_pre
        dgn_ref[...] += dgn
        dsh_ref[...] += dsh
        dsc_ref[...] += dsc

    row = lambda w: pl.BlockSpec((tile, w), lambda i: (i, 0))
    vec = pl.BlockSpec((1, D), lambda i: (0, 0))
    wspec = lambda a: pl.BlockSpec(a.shape, lambda i: (0, 0), pipeline_mode=pl.Buffered(1))
    rows_out = [(D, F32), (fp, _MXU), (fp, _MXU), (D, _MXU), (D, _MXU), (fp, _MXU)]
    return pl.pallas_call(
        kern,
        out_shape=[jax.ShapeDtypeStruct((s, w), dt) for w, dt in rows_out] + [jax.ShapeDtypeStruct((1, D), F32)] * 4,
        grid=(s // tile,),
        in_specs=[row(D), row(D), row(fp), row(fp), vec, vec, vec, vec, wspec(wg), wspec(wu), wspec(wd)],
        out_specs=[row(w) for w, _ in rows_out] + [vec] * 4,
        name=name, compiler_params=_params(1, [False] * 8 + [True] * 3))(x, dxo, gate, up, g, sh, sc, gt, wg, wu, wd)


def small_matmul(name, a, w, tn=256):
    m, k = a.shape
    n = w.shape[1]

    def kern(a_ref, w_ref, o_ref):
        o_ref[...] = jnp.dot(a_ref[...].astype(_MXU), w_ref[...].astype(_MXU), preferred_element_type=F32)

    return pl.pallas_call(kern, out_shape=jax.ShapeDtypeStruct((m, n), F32), grid=(n // tn,),
                          in_specs=[pl.BlockSpec((m, k), lambda j: (0, 0)), pl.BlockSpec((k, tn), lambda j: (0, j))],
                          out_specs=pl.BlockSpec((m, tn), lambda j: (0, j)), name=name,
                          compiler_params=_params(1))(a, w)


def small_matmul_tn(name, a, b, tn=256):
    m, k = a.shape
    n = b.shape[1]

    def kern(a_ref, b_ref, o_ref):
        o_ref[...] = lax.dot_general(a_ref[...].astype(_MXU), b_ref[...].astype(_MXU), (((0,), (0,)), ((), ())),
                                     preferred_element_type=F32)

    return pl.pallas_call(kern, out_shape=jax.ShapeDtypeStruct((k, n), F32), grid=(n // tn,),
                          in_specs=[pl.BlockSpec((m, k), lambda j: (0, 0)), pl.BlockSpec((m, tn), lambda j: (0, j))],
                          out_specs=pl.BlockSpec((k, tn), lambda j: (0, j)), name=name,
                          compiler_params=_params(1))(a, b)


def _s5_prep_math(lam_re, lam_im, log_dt, b_re_t, b_im_t, expand):
    dt = jnp.dot(jnp.exp(log_dt), expand, precision=HI, preferred_element_type=F32)
    mag = jnp.exp(lam_re * dt)
    ab_re = mag * jnp.cos(lam_im * dt)
    ab_im = mag * jnp.sin(lam_im * dt)
    den = lam_re * lam_re + lam_im * lam_im
    nr = ab_re - 1.0
    ni = ab_im
    f_re = (nr * lam_re + ni * lam_im) / den
    f_im = (ni * lam_re - nr * lam_im) / den
    bb_re = f_re * b_re_t - f_im * b_im_t
    bb_im = f_re * b_im_t + f_im * b_re_t
    return ab_re, ab_im, bb_re, bb_im


def _whole(kern, name, out_shape, *args):
    return pl.pallas_call(kern, out_shape=out_shape, name=name,
                          compiler_params=pltpu.CompilerParams(vmem_limit_bytes=VMEM_LIMIT))(*args)


def s5_prep_fwd(name, lam_re, lam_im, log_dt, b_re_t, b_im_t, expand):
    def kern(a, b, c, d, e, f, o0, o1, o2, o3):
        r = _s5_prep_math(a[...], b[...], c[...], d[...], e[...], f[...])
        for o, v in zip((o0, o1, o2, o3), r):
            o[...] = v

    gn = lam_re.shape[1]
    shp = [jax.ShapeDtypeStruct((1, gn), F32)] * 2 + [jax.ShapeDtypeStruct((P, gn), F32)] * 2
    return _whole(kern, name, shp, lam_re, lam_im, log_dt, b_re_t, b_im_t, expand)


def s5_prep_bwd(name, lam_re, lam_im, log_dt, b_re_t, b_im_t, expand, cots):
    def kern(a, b, c, d, e, f, c0, c1, c2, c3, o0, o1, o2, o3, o4):
        ex = f[...]
        _, vjp_fn = jax.vjp(lambda *p: _s5_prep_math(*p, ex), a[...], b[...], c[...], d[...], e[...])
        g = vjp_fn((c0[...], c1[...], c2[...], c3[...]))
        for o, v in zip((o0, o1, o2, o3, o4), g):
            o[...] = v

    shp = [jax.ShapeDtypeStruct(a.shape, F32) for a in (lam_re, lam_im, log_dt, b_re_t, b_im_t)]
    return _whole(kern, name, shp, lam_re, lam_im, log_dt, b_re_t, b_im_t, expand, *cots)


def _cpowers(ar, ai):
    pw = [(ar, ai)]
    for _ in range(7):
        pr, pi = pw[-1]
        pw.append((pr * ar - pi * ai, pr * ai + pi * ar))
    return pw


def _row_select(row, values):
    out = jnp.broadcast_to(values[7], (8, values[7].shape[1]))
    for r in range(6, -1, -1):
        out = jnp.where(row == r, values[r], out)
    return out


def _scan_tables(ar, ai, reverse):
    pw = _cpowers(ar, ai)
    row = lax.broadcasted_iota(jnp.int32, (8, ar.shape[1]), 0)
    steps = []
    for d in (1, 2, 4):
        keep = (row <= 7 - d) if reverse else (row >= d)
        steps.append((jnp.where(keep, pw[d - 1][0], 0.0), jnp.where(keep, pw[d - 1][1], 0.0)))
    order = list(range(7, -1, -1)) if reverse else list(range(8))
    carry = (_row_select(row, [pw[i][0] for i in order]), _row_select(row, [pw[i][1] for i in order]))
    return steps, carry


def _tile_scan_fwd(xr, xi, cr, ci, steps, carry_m):
    for d, (mr, mi) in zip((1, 2, 4), steps):
        sr = pltpu.roll(xr, d, 0)
        si = pltpu.roll(xi, d, 0)
        xr, xi = xr + mr * sr - mi * si, xi + mr * si + mi * sr
    pr, pi = carry_m
    return xr + pr * cr - pi * ci, xi + pr * ci + pi * cr


def _tile_scan_rev(xr, xi, cr, ci, steps, carry_m):
    for d, (mr, mi) in zip((1, 2, 4), steps):
        sr = pltpu.roll(xr, 8 - d, 0)
        si = pltpu.roll(xi, 8 - d, 0)
        xr, xi = xr + mr * sr + mi * si, xi + mr * si - mi * sr
    pr, pi = carry_m
    return xr + pr * cr + pi * ci, xi + pr * ci - pi * cr


def _fwd_scan_block(buf, row0, n_tiles8, ar, ai, c0r, c0i):
    steps, carry_m = _scan_tables(ar, ai, False)

    def body(j, carry):
        cr, ci = carry
        r0 = pl.multiple_of(row0 + j * 8, 8)
        xr = buf[pl.ds(r0, 8), 0:HALF]
        xi = buf[pl.ds(r0, 8), HALF:2 * HALF]
        xr, xi = _tile_scan_fwd(xr, xi, cr, ci, steps, carry_m)
        buf[pl.ds(r0, 8), 0:HALF] = xr
        buf[pl.ds(r0, 8), HALF:2 * HALF] = xi
        return xr[7:8], xi[7:8]

    return lax.fori_loop(0, n_tiles8, body, (c0r, c0i))


def s5_scan_fwd(name, x, g, sh, sc, wb, wc, a_tab, dskip, tile=TILE_SCAN):
    s = x.shape[0]
    n_t = s // tile

    def kern(x_ref, g_ref, sh_ref, sc_ref, wb_ref, wc_ref, a_ref, d_ref, y_ref, s0_ref, carry_ref, buf, hbuf):
        i = pl.program_id(0)

        @pl.when(i == 0)
        def _():
            carry_ref[...] = jnp.zeros(carry_ref.shape, F32)

        s0_ref[0] = carry_ref[...]
        hbuf[...] = modulate(rms(x_ref[...], g_ref[...]), sh_ref[...], sc_ref[...])
        for k in range(NBLK):
            cols = slice(GB * P * k, GB * P * (k + 1))
            u = hbuf[:, cols]
            buf[...] = jnp.dot(u.astype(_MXU), wb_ref[k], preferred_element_type=F32)
            ar = a_ref[k, :, 0:HALF]
            ai = a_ref[k, :, HALF:2 * HALF]
            cr, ci = _fwd_scan_block(buf, 0, tile // 8, ar, ai, carry_ref[k:k + 1, 0:HALF],
                                     carry_ref[k:k + 1, HALF:2 * HALF])
            carry_ref[k:k + 1, 0:HALF] = cr
            carry_ref[k:k + 1, HALF:2 * HALF] = ci
            y_ref[:, cols] = jnp.dot(buf[...].astype(_MXU), wc_ref[k], preferred_element_type=F32) + d_ref[:, cols] * u

    full = lambda a: pl.BlockSpec(a.shape, functools.partial(lambda i, nd_: (0,) * nd_, nd_=a.ndim))
    return pl.pallas_call(
        kern,
        out_shape=[jax.ShapeDtypeStruct((s, D), F32), jax.ShapeDtypeStruct((n_t, NBLK, 2 * HALF), F32)],
        grid=(n_t,),
        in_specs=[pl.BlockSpec((tile, D), lambda i: (i, 0)), full(g), full(sh), full(sc), full(wb), full(wc), full(a_tab),
                  full(dskip)],
        out_specs=[pl.BlockSpec((tile, D), lambda i: (i, 0)), pl.BlockSpec((1, NBLK, 2 * HALF), lambda i: (i, 0, 0))],
        scratch_shapes=[pltpu.VMEM((NBLK, 2 * HALF), F32), pltpu.VMEM((tile, 2 * HALF), F32), pltpu.VMEM((tile, D), F32)],
        name=name, compiler_params=_params(1))(x, g, sh, sc, wb, wc, a_tab, dskip)


def s5_scan_bwd(name, x, g, sh, sc, dy, s0, wb, wc, a_tab, dskip, dx_add, tile=TILE_SCAN):
    s = x.shape[0]
    n_t = s // tile
    n8 = tile // 8

    def pre(x_, g_, sh_, sc_):
        return modulate(rms(x_, g_), sh_, sc_)

    def kern(x_ref, g_ref, sh_ref, sc_ref, dy_ref, s0_ref, wb_ref, wc_ref, a_ref, d_ref, add_ref,
             dx_ref, dwb_ref, dwc_ref, da_ref, dd_ref, dgn_ref, dsh_ref, dsc_ref, lam_ref, sbuf, gbuf, hbuf, dh_ref):
        i = pl.program_id(0)

        @pl.when(i == 0)
        def _():
            lam_ref[...] = jnp.zeros(lam_ref.shape, F32)
            for r in (dwb_ref, dwc_ref, da_ref, dd_ref, dgn_ref, dsh_ref, dsc_ref):
                r[...] = jnp.zeros(r.shape, F32)

        hbuf[...] = pre(x_ref[...], g_ref[...], sh_ref[...], sc_ref[...])
        for k in range(NBLK):
            cols = slice(GB * P * k, GB * P * (k + 1))
            u = hbuf[:, cols]
            dyk = dy_ref[:, cols]
            ar = a_ref[k, :, 0:HALF]
            ai = a_ref[k, :, HALF:2 * HALF]
            sbuf[0:8, :] = jnp.broadcast_to(s0_ref[0, k:k + 1, :], (8, 2 * HALF))
            sbuf[8:tile + 8, :] = jnp.dot(u.astype(_MXU), wb_ref[k], preferred_element_type=F32)
            _fwd_scan_block(sbuf, 8, n8, ar, ai, s0_ref[0, k:k + 1, 0:HALF], s0_ref[0, k:k + 1, HALF:2 * HALF])
            dyb = dyk.astype(_MXU)
            gbuf[...] = lax.dot_general(dyb, wc_ref[k], (((1,), (1,)), ((), ())), preferred_element_type=F32)
            dwc_ref[k] += lax.dot_general(sbuf[8:tile + 8, :].astype(_MXU), dyb, (((0,), (0,)), ((), ())),
                                          preferred_element_type=F32)
            steps, carry_m = _scan_tables(ar, ai, True)
            row = lax.broadcasted_iota(jnp.int32, (8, HALF), 0)

            def body(jj, carry):
                cr, ci, dar, dai = carry
                j = n8 - 1 - jj
                r0 = pl.multiple_of(j * 8, 8)
                xr = gbuf[pl.ds(r0, 8), 0:HALF]
                xi = gbuf[pl.ds(r0, 8), HALF:2 * HALF]
                xr, xi = _tile_scan_rev(xr, xi, cr, ci, steps, carry_m)
                gbuf[pl.ds(r0, 8), 0:HALF] = xr
                gbuf[pl.ds(r0, 8), HALF:2 * HALF] = xi
                r1 = pl.multiple_of(j * 8 + 8, 8)
                spr = jnp.where(row == 0, sbuf[pl.ds(r0, 8), 0:HALF][7:8],
                                pltpu.roll(sbuf[pl.ds(r1, 8), 0:HALF], 1, 0))
                spi = jnp.where(row == 0, sbuf[pl.ds(r0, 8), HALF:2 * HALF][7:8],
                                pltpu.roll(sbuf[pl.ds(r1, 8), HALF:2 * HALF], 1, 0))
                dar = dar + xr * spr + xi * spi
                dai = dai + xi * spr - xr * spi
                return xr[0:1], xi[0:1], dar, dai

            z8 = jnp.zeros((8, HALF), F32)
            cr, ci, dar, dai = lax.fori_loop(
                0, n8, body, (lam_ref[k:k + 1, 0:HALF], lam_ref[k:k + 1, HALF:2 * HALF], z8, z8))
            lam_ref[k:k + 1, 0:HALF] = cr
            lam_ref[k:k + 1, HALF:2 * HALF] = ci
            da_ref[k:k + 1, 0:HALF] += jnp.sum(dar, axis=0, keepdims=True)
            da_ref[k:k + 1, HALF:2 * HALF] += jnp.sum(dai, axis=0, keepdims=True)
            lam = gbuf[...].astype(_MXU)
            dwb_ref[k] += lax.dot_general(u.astype(_MXU), lam, (((0,), (0,)), ((), ())), preferred_element_type=F32)
            du = lax.dot_general(lam, wb_ref[k], (((1,), (1,)), ((), ())), preferred_element_type=F32)
            dh_ref[:, cols] = du + d_ref[:, cols] * dyk
            dd_ref[:, cols] += jnp.sum(dyk * u, axis=0, keepdims=True)

        _, pre_vjp = jax.vjp(pre, x_ref[...], g_ref[...], sh_ref[...], sc_ref[...])
        dxp, dgn, dsh, dsc = pre_vjp(dh_ref[...])
        dx_ref[...] = dxp + add_ref[...]
        dgn_ref[...] += dgn
        dsh_ref[...] += dsh
        dsc_ref[...] += dsc

    full = lambda a: pl.BlockSpec(a.shape, functools.partial(lambda i, nd_: (0,) * nd_, nd_=a.ndim))
    fullo = lambda shp: pl.BlockSpec(shp, functools.partial(lambda i, nd_: (0,) * nd_, nd_=len(shp)))
    rev = lambda i: (n_t - 1 - i, 0)
    rows = pl.BlockSpec((tile, D), rev)
    return pl.pallas_call(
        kern,
        out_shape=[jax.ShapeDtypeStruct((s, D), F32), jax.ShapeDtypeStruct(wb.shape, F32),
                   jax.ShapeDtypeStruct(wc.shape, F32), jax.ShapeDtypeStruct((NBLK, 2 * HALF), F32),
                   jax.ShapeDtypeStruct((1, D), F32)] + [jax.ShapeDtypeStruct((1, D), F32)] * 3,
        grid=(n_t,),
        in_specs=[rows, full(g), full(sh), full(sc), rows,
                  pl.BlockSpec((1, NBLK, 2 * HALF), lambda i: (n_t - 1 - i, 0, 0)),
                  full(wb), full(wc), full(a_tab), full(dskip), rows],
        out_specs=[rows, fullo(wb.shape), fullo(wc.shape), fullo((NBLK, 2 * HALF)), fullo((1, D))] + [fullo((1, D))] * 3,
        scratch_shapes=[pltpu.VMEM((NBLK, 2 * HALF), F32), pltpu.VMEM((tile + 8, 2 * HALF), F32),
                        pltpu.VMEM((tile, 2 * HALF), F32), pltpu.VMEM((tile, D), F32), pltpu.VMEM((tile, D), F32)],
        name=name, compiler_params=_params(1))(x, g, sh, sc, dy, s0, wb, wc, a_tab, dskip, dx_add)


def _chunk_mask(q0, k0, tq, tk):
    r = (q0 + lax.broadcasted_iota(jnp.int32, (tq, tk), 0)) // CHUNK
    c = (k0 + lax.broadcasted_iota(jnp.int32, (tq, tk), 1)) // CHUNK
    return r >= c


def _head_lanes(j):
    lane = _lane(2 * DV)
    return (lane >= DV * j) & (lane < DV * (j + 1))


def _raw_scores(q, kblk, masked, t):
    s = lax.dot_general(q, kblk, (((1,), (1,)), ((), ())), preferred_element_type=F32)
    return jnp.where(_chunk_mask(0, 0, t, t), s, -1e30) if masked else s


def attn_fwd(name, q, k, v, t=TILE_ATT, tk=TILE_ATT):
    s = q.shape[0]
    n_q = s // t
    r = t // tk

    def kern(q_ref, k_ref, v_ref, o_ref, lse_ref):
        qi = pl.program_id(1)
        qs = [q_ref[:, HD * j:HD * (j + 1)] for j in range(2)]

        def absorb(k0, carry, mask):
            vblk = v_ref[pl.ds(k0, tk), :]
            scs = [lax.dot_general(qs[j], k_ref[pl.ds(k0, tk), HD * j:HD * (j + 1)], (((1,), (1,)), ((), ())),
                                   preferred_element_type=F32) for j in range(2)]
            if mask is not None:
                scs = [jnp.where(mask, sc, -1e30) for sc in scs]
            m_new = [jnp.maximum(carry[j][0], jnp.max(scs[j], axis=-1, keepdims=True)) for j in range(2)]
            ps = [jnp.exp2((scs[j] - m_new[j]) * EXP2_SCALE) for j in range(2)]
            alphas = [jnp.exp2((carry[j][0] - m_new[j]) * EXP2_SCALE) for j in range(2)]
            pvs = [jnp.dot(ps[j].astype(_MXU), vblk, preferred_element_type=F32) for j in range(2)]
            return tuple((m_new[j], alphas[j] * carry[j][1] + jnp.sum(ps[j], axis=-1, keepdims=True),
                          alphas[j] * carry[j][2] + pvs[j]) for j in range(2))

        init = tuple((jnp.full((t, 1), -1e30, F32), jnp.zeros((t, 1), F32), jnp.zeros((t, 2 * DV), F32))
                     for _ in range(2))
        carry = lax.fori_loop(0, qi * r, lambda kb, c: absorb(pl.multiple_of(kb * tk, tk), c, None), init)
        for i in range(r):
            carry = absorb(pl.multiple_of(qi * t + i * tk, tk), carry, _chunk_mask(0, i * tk, t, tk))
        outs = []
        for j in range(2):
            m, l, acc = carry[j]
            outs.append(acc / l)
            lse_ref[0, j] = m * ATTN_SCALE + jnp.log(l)
        o_ref[...] = jnp.where(_head_lanes(0), outs[0], outs[1])

    return pl.pallas_call(
        kern,
        out_shape=[jax.ShapeDtypeStruct((s, H * DV), F32), jax.ShapeDtypeStruct((HP, 2, s, 1), F32)],
        grid=(HP, n_q),
        in_specs=[pl.BlockSpec((t, 2 * HD), lambda hp, i: (i, hp)), pl.BlockSpec((s, 2 * HD), lambda hp, i: (0, hp)),
                  pl.BlockSpec((s, 2 * DV), lambda hp, i: (0, hp))],
        out_specs=[pl.BlockSpec((t, 2 * DV), lambda hp, i: (i, hp)),
                   pl.BlockSpec((1, 2, t, 1), lambda hp, i: (hp, 0, i, 0))],
        name=name, compiler_params=_params(2))(q, k, v)


def attn_bwd(name, q, k, v, o, do, lse, t=TILE_ATT):
    s = q.shape[0]
    n_q = s // t

    def kern(q_ref, k_ref, v_ref, o_ref, do_ref, lse_ref, dq_ref, dk_ref, dv_ref):
        qi = pl.program_id(1)

        @pl.when(qi == 0)
        def _():
            dk_ref[...] = jnp.zeros(dk_ref.shape, F32)
            dv_ref[...] = jnp.zeros(dv_ref.shape, F32)

        qs, doms, deltas, lse2 = [], [], [], []
        for j in range(2):
            qs.append(q_ref[:, HD * j:HD * (j + 1)])
            dom = jnp.where(_head_lanes(j), do_ref[...], 0.0)
            deltas.append(jnp.sum(dom * o_ref[...], axis=-1, keepdims=True))
            doms.append(dom.astype(_MXU))
            lse2.append(lse_ref[0, j] * LOG2E)

        def block(k0, dqs, masked):
            vblk = v_ref[pl.ds(k0, t), :]
            kblks = [k_ref[pl.ds(k0, t), HD * j:HD * (j + 1)] for j in range(2)]
            scs = [_raw_scores(qs[j], kblks[j], masked, t) for j in range(2)]
            dps = [lax.dot_general(doms[j], vblk, (((1,), (1,)), ((), ())), preferred_element_type=F32)
                   for j in range(2)]
            ps = [jnp.exp2(scs[j] * EXP2_SCALE - lse2[j]) for j in range(2)]
            dss = [(ps[j] * (dps[j] - deltas[j])).astype(_MXU) for j in range(2)]
            pbs = [ps[j].astype(_MXU) for j in range(2)]
            new = tuple(dqs[j] + jnp.dot(dss[j], kblks[j], preferred_element_type=F32) for j in range(2))
            for j in range(2):
                dk_ref[pl.ds(k0, t), HD * j:HD * (j + 1)] += lax.dot_general(
                    dss[j], qs[j], (((0,), (0,)), ((), ())), preferred_element_type=F32)
            dvs = [lax.dot_general(pbs[j], doms[j], (((0,), (0,)), ((), ())), preferred_element_type=F32)
                   for j in range(2)]
            dv_ref[pl.ds(k0, t), :] += dvs[0] + dvs[1]
            return new

        init = (jnp.zeros((t, HD), F32), jnp.zeros((t, HD), F32))
        dqs = lax.fori_loop(0, qi, lambda kb, c: block(pl.multiple_of(kb * t, t), c, False), init)
        dqs = block(pl.multiple_of(qi * t, t), dqs, True)
        for j in range(2):
            dq_ref[:, HD * j:HD * (j + 1)] = dqs[j] * ATTN_SCALE

        @pl.when(qi == n_q - 1)
        def _():
            dk_ref[...] = dk_ref[...] * ATTN_SCALE

    return pl.pallas_call(
        kern,
        out_shape=[jax.ShapeDtypeStruct((s, H * HD), F32), jax.ShapeDtypeStruct((s, H * HD), F32),
                   jax.ShapeDtypeStruct((s, H * DV), F32)],
        grid=(HP, n_q),
        in_specs=[pl.BlockSpec((t, 2 * HD), lambda hp, i: (i, hp)), pl.BlockSpec((s, 2 * HD), lambda hp, i: (0, hp)),
                  pl.BlockSpec((s, 2 * DV), lambda hp, i: (0, hp)), pl.BlockSpec((t, 2 * DV), lambda hp, i: (i, hp)),
                  pl.BlockSpec((t, 2 * DV), lambda hp, i: (i, hp)),
                  pl.BlockSpec((1, 2, t, 1), lambda hp, i: (hp, 0, i, 0))],
        out_specs=[pl.BlockSpec((t, 2 * HD), lambda hp, i: (i, hp)), pl.BlockSpec((s, 2 * HD), lambda hp, i: (0, hp)),
                   pl.BlockSpec((s, 2 * DV), lambda hp, i: (0, hp))],
        name=name, compiler_params=_params(2))(q, k, v, o, do, lse)


def rope_tables(name, pos_col, inv128):
    s = pos_col.shape[0]

    def kern(p_ref, inv_ref, c_ref, s_ref):
        ang = p_ref[...].astype(F32) * inv_ref[...]
        lane = _lane()
        m_r = (lane >= DN) & (lane < DN + DR)
        c_ref[...] = jnp.where(lane < DN, 1.0, jnp.where(m_r, jnp.cos(ang), 0.0))
        s_ref[...] = jnp.where(m_r, jnp.sin(ang), 0.0)

    return _whole(kern, name, [jax.ShapeDtypeStruct((s, HD), F32)] * 2, pos_col, inv128)


def loss_kernel(name, y, tgt, tile=TILE_ROW):
    def body(row_v, _):
        err = row_v[0] - row_v[1]
        part = 0.5 * jnp.sum(jnp.mean(err * err, axis=-1, keepdims=True), axis=0, keepdims=True)
        return [err * (1.0 / D)], [jnp.broadcast_to(part, (1, 128))]

    return _row_call(name, body, [y, tgt], [], [(D, F32)], [((1, 128), F32)], tile)


def _row_tile(r, c):
    cap = max(8, (1 << 18) // max(c, 1))
    for t in (2048, 1024, 512, 256, 128, 64, 32, 16, 8):
        if t <= cap and r % t == 0:
            return t
    return r


def sum_parts(name, parts):
    n, r, c = parts.shape
    t = _row_tile(r, c)

    def kern(p_ref, o_ref):
        acc = p_ref[0].astype(F32)
        for i in range(1, n):
            acc = acc + p_ref[i].astype(F32)
        o_ref[...] = acc

    return pl.pallas_call(kern, out_shape=jax.ShapeDtypeStruct((r, c), F32), grid=(r // t,),
                          in_specs=[pl.BlockSpec((n, t, c), lambda i: (0, i, 0))],
                          out_specs=pl.BlockSpec((t, c), lambda i: (i, 0)), name=name, compiler_params=_params(1))(parts)


def adamw(name, parts, w, m, v, base=0, stride=0):
    n, _, cp = parts.shape
    nl, r, c = w.shape
    t = _row_tile(math.gcd(math.gcd(r, base), stride), max(c, cp))
    c1 = 1.0 / (1.0 - ADAM_B1 ** ADAM_STEP)
    c2 = 1.0 / (1.0 - ADAM_B2 ** ADAM_STEP)

    def kern(p_ref, w_ref, m_ref, v_ref, g_ref, d_ref, nm_ref, nv_ref):
        g = p_ref[0].astype(F32)
        for i in range(1, n):
            g = g + p_ref[i].astype(F32)
        g = g[:, :c]
        nm = ADAM_B1 * m_ref[...] + (1.0 - ADAM_B1) * g
        nv = ADAM_B2 * v_ref[...] + (1.0 - ADAM_B2) * (g * g)
        g_ref[...] = g
        nm_ref[...] = nm
        nv_ref[...] = nv
        d_ref[...] = -ADAM_LR * ((nm * c1) / (jnp.sqrt(nv * c2) + ADAM_EPS) + ADAM_WD * w_ref[...])

    spec = pl.BlockSpec((None, t, c), lambda l, i: (l, i, 0))
    pspec = pl.BlockSpec((n, t, cp), lambda l, i: (0, (base + l * stride) // t + i, 0))
    return pl.pallas_call(kern, out_shape=[jax.ShapeDtypeStruct((nl, r, c), F32)] * 4, grid=(nl, r // t),
                          in_specs=[pspec, spec, spec, spec], out_specs=[spec] * 4, name=name,
                          compiler_params=_params(2))(parts, w, m, v)


def adamw_multi(name, parts_list, w, m, v):
    nl, r, c = w.shape
    n, _, cp = parts_list[0].shape
    t = _row_tile(r, max(c, cp))
    c1 = 1.0 / (1.0 - ADAM_B1 ** ADAM_STEP)
    c2 = 1.0 / (1.0 - ADAM_B2 ** ADAM_STEP)

    def kern(*refs):
        p_refs = refs[:nl]
        w_ref, m_ref, v_ref, g_ref, d_ref, nm_ref, nv_ref = refs[nl:]
        layer = pl.program_id(0)
        for ll in range(nl):
            @pl.when(layer == ll)
            def _(ll=ll):
                g = p_refs[ll][0].astype(F32)
                for i in range(1, n):
                    g = g + p_refs[ll][i].astype(F32)
                g = g[:, :c]
                nm = ADAM_B1 * m_ref[...] + (1.0 - ADAM_B1) * g
                nv = ADAM_B2 * v_ref[...] + (1.0 - ADAM_B2) * (g * g)
                g_ref[...] = g
                nm_ref[...] = nm
                nv_ref[...] = nv
                d_ref[...] = -ADAM_LR * ((nm * c1) / (jnp.sqrt(nv * c2) + ADAM_EPS) + ADAM_WD * w_ref[...])

    spec = pl.BlockSpec((None, t, c), lambda l, i: (l, i, 0))
    pspecs = [pl.BlockSpec((n, t, cp), functools.partial(lambda l, i, ll_: (0, jnp.where(l == ll_, i, 0), 0), ll_=ll))
              for ll in range(nl)]
    return pl.pallas_call(kern, out_shape=[jax.ShapeDtypeStruct((nl, r, c), F32)] * 4, grid=(nl, r // t),
                          in_specs=pspecs + [spec, spec, spec], out_specs=[spec] * 4, name=name,
                          compiler_params=_params(2))(*parts_list, w, m, v)


def adamw_layer(name, parts, w, m, v, layer, prev, base=0):
    n, _, cp = parts.shape
    nl, r, c = w.shape
    t = _row_tile(math.gcd(r, base), max(c, cp))
    c1 = 1.0 / (1.0 - ADAM_B1 ** ADAM_STEP)
    c2 = 1.0 / (1.0 - ADAM_B2 ** ADAM_STEP)
    chained = nl > 1

    def kern(p_ref, w_ref, m_ref, v_ref, *rest):
        g_ref, d_ref, nm_ref, nv_ref = rest[-4:]
        g = p_ref[0].astype(F32)
        for i in range(1, n):
            g = g + p_ref[i].astype(F32)
        g = g[:, :c]
        nm = ADAM_B1 * m_ref[...] + (1.0 - ADAM_B1) * g
        nv = ADAM_B2 * v_ref[...] + (1.0 - ADAM_B2) * (g * g)
        g_ref[...] = g
        nm_ref[...] = nm
        nv_ref[...] = nv
        d_ref[...] = -ADAM_LR * ((nm * c1) / (jnp.sqrt(nv * c2) + ADAM_EPS) + ADAM_WD * w_ref[...])

    spec = pl.BlockSpec((None, t, c), lambda i: (layer, i, 0))
    pspec = pl.BlockSpec((n, t, cp), lambda i: (0, base // t + i, 0))
    in_specs = [pspec, spec, spec, spec]
    args = [parts, w, m, v]
    aliases = {}
    if chained:
        if prev is None:
            prev = [lax.empty((nl, r, c), F32) for _ in range(4)]
        in_specs += [pl.BlockSpec(memory_space=pl.ANY)] * 4
        args += list(prev)
        aliases = {4 + i: i for i in range(4)}
    return pl.pallas_call(kern, out_shape=[jax.ShapeDtypeStruct((nl, r, c), F32)] * 4, grid=(r // t,),
                          in_specs=in_specs, out_specs=[spec] * 4, input_output_aliases=aliases, name=name,
                          compiler_params=_params(1))(*args)


def _me():
    return lax.axis_index("x"), lax.axis_index("y"), lax.axis_index("c")


def _flip(x, y, c, mask):
    return (jnp.where((mask >> 2) & 1, 1 - x, x), jnp.where((mask >> 1) & 1, 1 - y, y), jnp.where(mask & 1, 1 - c, c))


def _index(x, y, c):
    return 4 * x + 2 * y + c


def _exchange(name, arr, gather):
    out_shape = (N_DEV,) + arr.shape if gather else arr.shape

    def kern(in_ref, out_ref, send_sems, recv_sems, local_sem):
        x, y, c = _me()
        me = _index(x, y, c)
        mine = pltpu.make_async_copy(in_ref if gather else in_ref.at[me], out_ref.at[me], local_sem)
        mine.start()
        copies = []
        for mask in range(1, N_DEV):
            px, py, pc = _flip(x, y, c, mask)
            peer = _index(px, py, pc)
            cp = pltpu.make_async_remote_copy(
                src_ref=in_ref if gather else in_ref.at[peer], dst_ref=out_ref.at[me],
                send_sem=send_sems.at[mask - 1], recv_sem=recv_sems.at[mask - 1],
                device_id=(px, py, pc), device_id_type=MESH)
            cp.start()
            copies.append((cp, peer))
        for mask, (cp, peer) in enumerate(copies, start=1):
            pltpu.make_async_remote_copy(
                src_ref=in_ref if gather else in_ref.at[peer], dst_ref=out_ref.at[peer],
                send_sem=send_sems.at[mask - 1], recv_sem=recv_sems.at[mask - 1],
                device_id=_flip(x, y, c, mask), device_id_type=MESH).wait_recv()
        for cp, _ in copies:
            cp.wait_send()
        mine.wait()

    any_spec = pl.BlockSpec(memory_space=pl.ANY)
    return pl.pallas_call(
        kern, out_shape=jax.ShapeDtypeStruct(out_shape, arr.dtype), in_specs=[any_spec], out_specs=any_spec,
        scratch_shapes=[pltpu.SemaphoreType.DMA((N_DEV - 1,)), pltpu.SemaphoreType.DMA((N_DEV - 1,)),
                        pltpu.SemaphoreType.DMA],
        name=name, compiler_params=pltpu.CompilerParams(has_side_effects=True))(arr)


def all_gather(name, arr):
    return _exchange(name, arr, True)


_HBM = pl.BlockSpec(memory_space=pltpu.HBM)
_SEM = pl.BlockSpec(memory_space=pltpu.SEMAPHORE)
_EFFECT = pltpu.SideEffectType.DATAFLOW_SIDE_EFFECTING


def _split_copies(srcs, lands, send_sems, recv_sems, gather):
    x, y, c = _me()
    me = _index(x, y, c)
    out = []
    for a, (src, land) in enumerate(zip(srcs, lands)):
        for mask in range(1, N_DEV):
            px, py, pc = _flip(x, y, c, mask)
            peer = _index(px, py, pc)
            sem = (N_DEV - 1) * a + mask - 1
            mk = lambda dst_slot: pltpu.make_async_remote_copy(
                src_ref=src if gather else src.at[peer], dst_ref=land.at[dst_slot],
                send_sem=send_sems.at[sem], recv_sem=recv_sems.at[sem], device_id=(px, py, pc), device_id_type=MESH)
            out.append((mk(me), mk(peer)))
    return out


def exchange_start(name, arrs, gather, after):
    k = len(arrs)
    land_shapes = [((N_DEV,) + a.shape if gather else a.shape) for a in arrs]

    def body(*refs):
        srcs, lands = refs[:k], refs[k:2 * k]
        send_sems, recv_sems = refs[2 * k + 1], refs[2 * k + 2]
        token = refs[-1]
        for mine, _ in _split_copies(srcs, lands, send_sems, recv_sems, gather):
            mine.start()
        token[...] = jnp.zeros(token.shape, token.dtype)

    n_sem = (N_DEV - 1) * k
    res = pl.pallas_call(
        body, name=name,
        out_shape=(pltpu.SemaphoreType.DMA((n_sem,)), pltpu.SemaphoreType.DMA((n_sem,)),
                   *[pltpu.HBM(a.shape, a.dtype) for a in arrs],
                   *[pltpu.HBM(shp, a.dtype) for shp, a in zip(land_shapes, arrs)],
                   jax.ShapeDtypeStruct((8, 128), F32)),
        in_specs=[_HBM] * (2 * k) + [pl.BlockSpec(memory_space=pl.ANY)],
        out_specs=(_SEM, _SEM, *[_HBM] * (2 * k), pl.BlockSpec(memory_space=pltpu.VMEM)),
        input_output_aliases={i: 2 + i for i in range(2 * k)},
        compiler_params=pltpu.CompilerParams(has_side_effects=_EFFECT),
    )(*[pltpu.with_memory_space_constraint(a, pltpu.HBM) for a in arrs],
      *[pltpu.with_memory_space_constraint(lax.empty(shp, a.dtype), pltpu.HBM) for shp, a in zip(land_shapes, arrs)],
      after)
    return res[0], res[1], list(res[2:2 + k]), list(res[2 + k:2 + 2 * k]), res[-1]


def exchange_wait(name, started, after, gather):
    send_sems, recv_sems, thrus, lands, _ = started
    k = len(thrus)

    def body(*refs):
        srcs, lnds = refs[:k], refs[k:2 * k]
        s_sems, r_sems = refs[2 * k], refs[2 * k + 1]
        for mine, theirs in _split_copies(srcs, lnds, s_sems, r_sems, gather):
            mine.wait_send()
            theirs.wait_recv()

    res = pl.pallas_call(
        body, name=name,
        out_shape=tuple(pltpu.HBM(a.shape, a.dtype) for a in thrus + lands),
        in_specs=[_HBM] * (2 * k) + [_SEM, _SEM, pl.BlockSpec(memory_space=pl.ANY)], out_specs=tuple([_HBM] * (2 * k)),
        input_output_aliases={i: i for i in range(2 * k)},
        compiler_params=pltpu.CompilerParams(has_side_effects=_EFFECT),
    )(*thrus, *lands, send_sems, recv_sems, after)
    return list(res[k:])


def _pad_heads(w, real, padded):
    k = w.shape[0]
    w3 = w.reshape(k, H, real)
    return jnp.pad(w3, ((0, 0), (0, 0), (0, padded - real))).reshape(k, H * padded)


def _unpad_heads(w, real, padded):
    k = w.shape[0]
    return w.reshape(k, H, padded)[:, :, :real].reshape(k, H * real)


def _s5_place(ab_re, ab_im, bb_re_t, bb_im_t, c_re, c_im):
    eye = jnp.eye(GB, dtype=F32)

    def wb_part(bt):
        x4 = bt.reshape(P, NBLK, GB, N).transpose(1, 2, 0, 3)
        return jnp.einsum('kgpn,gh->kgphn', x4, eye).reshape(NBLK, GB * P, HALF)

    def wc_part(cc):
        x4 = cc.reshape(NBLK, GB, P, N)
        return jnp.einsum('kgpn,gh->kgnhp', x4, eye).reshape(NBLK, HALF, GB * P)

    wb = jnp.concatenate([wb_part(bb_re_t), wb_part(bb_im_t)], axis=-1)
    wc = jnp.concatenate([wc_part(c_re), -wc_part(c_im)], axis=1)
    a_tab = jnp.concatenate([ab_re.reshape(NBLK, 1, HALF), ab_im.reshape(NBLK, 1, HALF)], axis=-1)
    return wb.astype(_MXU), wc.astype(_MXU), a_tab


def _s5_unplace(dwb, dwc, da):
    eye = jnp.eye(GB, dtype=F32)

    def wb_part(dpart):
        x5 = dpart.reshape(NBLK, GB, P, GB, N)
        return jnp.einsum('kgphn,gh->kgpn', x5, eye).transpose(2, 0, 1, 3).reshape(P, G * N)

    def wc_part(dpart):
        x5 = dpart.reshape(NBLK, GB, N, GB, P)
        return jnp.einsum('kgnhp,gh->kgpn', x5, eye).reshape(G, P, N)

    dbb_re_t, dbb_im_t = wb_part(dwb[..., :HALF]), wb_part(dwb[..., HALF:])
    dc_re, dc_im = wc_part(dwc[:, :HALF]), -wc_part(dwc[:, HALF:])
    dab_re, dab_im = da[:, :HALF].reshape(1, G * N), da[:, HALF:].reshape(1, G * N)
    return dab_re, dab_im, dbb_re_t, dbb_im_t, dc_re, dc_im


def _row(v):
    return v.reshape(1, -1)


def _pack_rows(pieces):
    flat = jnp.concatenate(pieces)
    n = int(flat.shape[0])
    padded = -(-n // 65536) * 65536
    return jnp.pad(flat, (0, padded - n)).reshape(padded // 128, 128)


def kernel(x, c, positions, ada_w, ada_b, norm1_g, norm2_g, ffn_w_gate, ffn_w_up, ffn_w_down, s5_lam_re, s5_lam_im, s5_log_dt, s5_b_re, s5_b_im, s5_c_re, s5_c_im, s5_d, s5_w_glu, s5_b_glu, kv_ada_w, kv_ada_b, kv_norm_g, w_kv_a, kv_a_norm_g, w_kv_b, k_nope_norm_g, k_rope_norm_g, mla_w_dq, mla_q_norm_g, mla_w_uq, mla_q_nope_norm_g, mla_q_rope_norm_g, mla_w_o, loss_target, m_ada_w, m_ada_b, m_norm1_g, m_norm2_g, m_ffn_w_gate, m_ffn_w_up, m_ffn_w_down, m_s5_lam_re, m_s5_lam_im, m_s5_log_dt, m_s5_b_re, m_s5_b_im, m_s5_c_re, m_s5_c_im, m_s5_d, m_s5_w_glu, m_s5_b_glu, m_kv_ada_w, m_kv_ada_b, m_kv_norm_g, m_w_kv_a, m_kv_a_norm_g, m_w_kv_b, m_k_nope_norm_g, m_k_rope_norm_g, m_mla_w_dq, m_mla_q_norm_g, m_mla_w_uq, m_mla_q_nope_norm_g, m_mla_q_rope_norm_g, m_mla_w_o, v_ada_w, v_ada_b, v_norm1_g, v_norm2_g, v_ffn_w_gate, v_ffn_w_up, v_ffn_w_down, v_s5_lam_re, v_s5_lam_im, v_s5_log_dt, v_s5_b_re, v_s5_b_im, v_s5_c_re, v_s5_c_im, v_s5_d, v_s5_w_glu, v_s5_b_glu, v_kv_ada_w, v_kv_ada_b, v_kv_norm_g, v_w_kv_a, v_kv_a_norm_g, v_w_kv_b, v_k_nope_norm_g, v_k_rope_norm_g, v_mla_w_dq, v_mla_q_norm_g, v_mla_w_uq, v_mla_q_nope_norm_g, v_mla_q_rope_norm_g, v_mla_w_o):
    W = dict(ada_w=ada_w, ada_b=ada_b, norm1_g=norm1_g, norm2_g=norm2_g, ffn_w_gate=ffn_w_gate, ffn_w_up=ffn_w_up, ffn_w_down=ffn_w_down, s5_lam_re=s5_lam_re, s5_lam_im=s5_lam_im, s5_log_dt=s5_log_dt, s5_b_re=s5_b_re, s5_b_im=s5_b_im, s5_c_re=s5_c_re, s5_c_im=s5_c_im, s5_d=s5_d, s5_w_glu=s5_w_glu, s5_b_glu=s5_b_glu, kv_ada_w=kv_ada_w, kv_ada_b=kv_ada_b, kv_norm_g=kv_norm_g, w_kv_a=w_kv_a, kv_a_norm_g=kv_a_norm_g, w_kv_b=w_kv_b, k_nope_norm_g=k_nope_norm_g, k_rope_norm_g=k_rope_norm_g, mla_w_dq=mla_w_dq, mla_q_norm_g=mla_q_norm_g, mla_w_uq=mla_w_uq, mla_q_nope_norm_g=mla_q_nope_norm_g, mla_q_rope_norm_g=mla_q_rope_norm_g, mla_w_o=mla_w_o)
    M = dict(ada_w=m_ada_w, ada_b=m_ada_b, norm1_g=m_norm1_g, norm2_g=m_norm2_g, ffn_w_gate=m_ffn_w_gate, ffn_w_up=m_ffn_w_up, ffn_w_down=m_ffn_w_down, s5_lam_re=m_s5_lam_re, s5_lam_im=m_s5_lam_im, s5_log_dt=m_s5_log_dt, s5_b_re=m_s5_b_re, s5_b_im=m_s5_b_im, s5_c_re=m_s5_c_re, s5_c_im=m_s5_c_im, s5_d=m_s5_d, s5_w_glu=m_s5_w_glu, s5_b_glu=m_s5_b_glu, kv_ada_w=m_kv_ada_w, kv_ada_b=m_kv_ada_b, kv_norm_g=m_kv_norm_g, w_kv_a=m_w_kv_a, kv_a_norm_g=m_kv_a_norm_g, w_kv_b=m_w_kv_b, k_nope_norm_g=m_k_nope_norm_g, k_rope_norm_g=m_k_rope_norm_g, mla_w_dq=m_mla_w_dq, mla_q_norm_g=m_mla_q_norm_g, mla_w_uq=m_mla_w_uq, mla_q_nope_norm_g=m_mla_q_nope_norm_g, mla_q_rope_norm_g=m_mla_q_rope_norm_g, mla_w_o=m_mla_w_o)
    V = dict(ada_w=v_ada_w, ada_b=v_ada_b, norm1_g=v_norm1_g, norm2_g=v_norm2_g, ffn_w_gate=v_ffn_w_gate, ffn_w_up=v_ffn_w_up, ffn_w_down=v_ffn_w_down, s5_lam_re=v_s5_lam_re, s5_lam_im=v_s5_lam_im, s5_log_dt=v_s5_log_dt, s5_b_re=v_s5_b_re, s5_b_im=v_s5_b_im, s5_c_re=v_s5_c_re, s5_c_im=v_s5_c_im, s5_d=v_s5_d, s5_w_glu=v_s5_w_glu, s5_b_glu=v_s5_b_glu, kv_ada_w=v_kv_ada_w, kv_ada_b=v_kv_ada_b, kv_norm_g=v_kv_norm_g, w_kv_a=v_w_kv_a, kv_a_norm_g=v_kv_a_norm_g, w_kv_b=v_w_kv_b, k_nope_norm_g=v_k_nope_norm_g, k_rope_norm_g=v_k_rope_norm_g, mla_w_dq=v_mla_w_dq, mla_q_norm_g=v_mla_q_norm_g, mla_w_uq=v_mla_w_uq, mla_q_nope_norm_g=v_mla_q_nope_norm_g, mla_q_rope_norm_g=v_mla_q_rope_norm_g, mla_w_o=v_mla_w_o)
    return _step(x[0], c, positions, loss_target[0], W, M, V)


WEIGHT_NAMES = ['ada_w', 'ada_b', 'norm1_g', 'norm2_g', 'ffn_w_gate', 'ffn_w_up', 'ffn_w_down', 's5_lam_re', 's5_lam_im', 's5_log_dt', 's5_b_re', 's5_b_im', 's5_c_re', 's5_c_im', 's5_d', 's5_w_glu', 's5_b_glu', 'kv_ada_w', 'kv_ada_b', 'kv_norm_g', 'w_kv_a', 'kv_a_norm_g', 'w_kv_b', 'k_nope_norm_g', 'k_rope_norm_g', 'mla_w_dq', 'mla_q_norm_g', 'mla_w_uq', 'mla_q_nope_norm_g', 'mla_q_rope_norm_g', 'mla_w_o']
REPLICATED = ['ada_b', 'norm1_g', 'norm2_g', 's5_lam_re', 's5_lam_im', 's5_log_dt', 's5_b_re', 's5_b_im', 's5_c_re', 's5_c_im', 'kv_ada_b', 'kv_norm_g', 'kv_a_norm_g', 'k_nope_norm_g', 'k_rope_norm_g', 'mla_q_norm_g', 'mla_q_nope_norm_g', 'mla_q_rope_norm_g']
SHARDED_VEC = ['s5_d', 's5_b_glu']


def _step(x, c, positions, target, W, M, V):
    s = x.shape[0]
    me = _index(*_me())
    mxu = lambda a: a.astype(_MXU)

    pad_c = lambda a: jnp.pad(a, ((0, 0), (0, FFB - FF // N_DEV)))
    pad_r = lambda a: jnp.pad(a, ((0, FFB - FF // N_DEV), (0, 0)))
    cols = lambda g: g.transpose(1, 0, 2).reshape(g.shape[1], N_DEV * g.shape[2])
    rows = lambda g: g.reshape(N_DEV * g.shape[1], g.shape[2])

    def local_pack(l):
        second = W['s5_w_glu'][l] if l < N_A else W['mla_w_o'][l - N_A]
        arrs = [jnp.concatenate([mxu(pad_c(W['ffn_w_gate'][l])), mxu(pad_c(W['ffn_w_up'][l]))], axis=0),
                jnp.concatenate([mxu(pad_r(W['ffn_w_down'][l])), mxu(second)], axis=0)]
        if l == N_A:
            arrs += [jnp.concatenate([mxu(W['w_kv_b']), mxu(W['mla_w_dq'][0])], axis=0), mxu(W['w_kv_a'])]
        if l > N_A:
            arrs += [mxu(W['mla_w_dq'][l - N_A])]
        if l >= N_A:
            arrs += [mxu(W['mla_w_uq'][l - N_A])]
        return arrs


    def layer_weights(l, after):
        lands = exchange_wait(f"gather_wait_{l}", gathers[l], after, True)
        full = [lax.dynamic_update_slice(ld, src[None], (me,) + (0,) * src.ndim) for ld, src in zip(lands, gathers[l][2])]
        w = {'wg': cols(full[0][:, :D]), 'wu': cols(full[0][:, D:]), 'wd': rows(full[1][:, :FFB]),
             'second': rows(full[1][:, FFB:])}
        if l >= N_A:
            if l == N_A:
                wkvb3 = cols(full[2][:, :KVL]).reshape(KVL, H, DN + DV)
                wkva = rows(full[3])
                w['wa_pad'] = jnp.concatenate([wkva[:, :KVL], jnp.zeros((D, DN), _MXU), wkva[:, KVL:],
                                               jnp.zeros((D, HD - DN - DR), _MXU)], axis=1)
                w['wkn_pad'] = jnp.pad(wkvb3[:, :, :DN], ((0, 0), (0, 0), (0, HD - DN))).reshape(KVL, H * HD)
                w['wv'] = wkvb3[:, :, DN:].reshape(KVL, H * DV)
                w['wdq'] = rows(full[2][:, KVL:])
            else:
                w['wdq'] = rows(full[2])
            w['wuq_pad'] = _pad_heads(cols(full[-1]), DN + DR, HD)
        return w

    vec = jnp.concatenate([c.reshape(-1), W['s5_d'].reshape(-1), W['s5_b_glu'].reshape(-1)]).reshape(1, -1)
    vec = jnp.pad(vec, ((0, 7), (0, 0)))
    gv = all_gather("gather_vectors", vec)[:, 0, :]
    c_all = gv[:, :D]
    d_full = jnp.concatenate([gv[d, D:D + 2 * 128].reshape(N_A, 128) for d in range(N_DEV)], axis=1)
    bglu_full = jnp.concatenate([gv[d, D + 256:D + 512].reshape(N_A, 128) for d in range(N_DEV)], axis=1)

    ca_all = jax.nn.silu(c_all)
    w_mod = jnp.concatenate([W['ada_w'][l] for l in range(DEPTH)] + [W['kv_ada_w']], axis=1)
    n_mod = w_mod.shape[1]
    mod_cols = small_matmul("mod_matmul", ca_all, w_mod)
    gm = all_gather("gather_mod", mod_cols)
    gathers = [exchange_start(f"gather_start_{l}", local_pack(l), True, gm) for l in range(DEPTH)]
    tokens = sum(g[4][0, 0] for g in gathers)
    mine = lax.dynamic_index_in_dim(gm, me, axis=1, keepdims=False) + tokens
    per_l = D * 6 // N_DEV
    mods = []
    for l in range(DEPTH):
        full = jnp.concatenate([mine[d, per_l * l:per_l * (l + 1)] for d in range(N_DEV)]) + W['ada_b'][l]
        mods.append([_row(full[D * i:D * (i + 1)]) for i in range(6)])
    kfull = jnp.concatenate([mine[d, per_l * DEPTH:] for d in range(N_DEV)]) + W['kv_ada_b']
    k_shift, k_scale = _row(kfull[:D]), _row(kfull[D:])

    inv = 1.0 / (ROPE_THETA ** (np.arange(0, DR, 2, dtype=np.float32) / DR))
    inv128 = np.zeros((1, HD), np.float32)
    inv128[0, DN:DN + DR // 2] = inv
    inv128[0, DN + DR // 2:DN + DR] = inv
    cosf, sinf = rope_tables("rope_tables", positions.reshape(s, 1), jnp.asarray(inv128))
    zpad = lambda n: jnp.zeros((n,), F32)
    gkn128 = _row(jnp.concatenate([W['k_nope_norm_g'], zpad(HD - DN)]))
    gkr128 = _row(jnp.concatenate([zpad(DN), W['k_rope_norm_g'], zpad(HD - DN - DR)]))
    gq128 = [_row(jnp.concatenate([W['mla_q_nope_norm_g'][j], W['mla_q_rope_norm_g'][j], zpad(HD - DN - DR)]))
             for j in range(2)]

    expand = jnp.asarray(np.kron(np.eye(G, dtype=np.float32), np.ones((1, N), np.float32)))
    s5_raw, s5_mats = [], []
    for l in range(N_A):
        raw = (_row(W['s5_lam_re'][l]), _row(W['s5_lam_im'][l]), _row(W['s5_log_dt'][l]),
               W['s5_b_re'][l].transpose(2, 0, 1).reshape(P, G * N), W['s5_b_im'][l].transpose(2, 0, 1).reshape(P, G * N))
        ab_re, ab_im, bb_re_t, bb_im_t = s5_prep_fwd(f"s5_prep_fwd", *raw, expand)
        s5_raw.append(raw)
        s5_mats.append(_s5_place(ab_re, ab_im, bb_re_t, bb_im_t, W['s5_c_re'][l], W['s5_c_im'][l]))

    g1 = [_row(W['norm1_g'][l]) for l in range(DEPTH)]
    g2 = [_row(W['norm2_g'][l]) for l in range(DEPTH)]
    saved = []
    xs = x
    kv = None
    lw = [None] * DEPTH
    for l in range(DEPTH):
        sh1, sc1, gt1, sh2, sc2, gt2 = mods[l]
        rec = {'x_in': xs}
        if l >= N_A:
            lw[l] = layer_weights(l, xs)
        if l == N_A:
            kv_smalls = [_row(W['kv_norm_g']), k_shift, k_scale, _row(W['kv_a_norm_g']), gkn128, gkr128]
            kv_w = [lw[l]['wa_pad'], lw[l]['wkn_pad'], lw[l]['wv']]
            k_mat, v_mat = seg_forward("kv_fwd", seg_kv, [xs], kv_smalls, [cosf, sinf], kv_w,
                                       [(H * HD, _MXU), (H * DV, _MXU)], tap_widths=(KVL + HD, H * HD, H * DV))
            kv = {'x_in': xs, 'smalls': kv_smalls, 'k': k_mat, 'v': v_mat, 'w': kv_w}
        if l < N_A:
            wb, wc, a_tab = s5_mats[l]
            y, s0 = s5_scan_fwd("s5_scan_fwd", xs, g1[l], sh1, sc1, wb, wc, a_tab, _row(d_full[l]))
            lw[l] = layer_weights(l, y)
            (x_mid,) = seg_forward("glu_fwd", seg_glu, [xs, y], [gt1, _row(bglu_full[l])], [], [lw[l]['second']],
                                   [(D, F32)], tap_widths=(D,))
            rec.update(y=y, s0=s0)
        else:
            j = l - N_A
            q_smalls = [g1[l], sh1, sc1, _row(W['mla_q_norm_g'][j]), gq128[j]]
            (q_mat,) = seg_forward("q_fwd", seg_q, [xs], q_smalls, [cosf, sinf], [lw[l]['wdq'], lw[l]['wuq_pad']],
                                   [(H * HD, _MXU)], tap_widths=(QL, H * HD))
            o_mat, lse = attn_fwd("attn_fwd", q_mat, kv['k'], kv['v'])
            (x_mid,) = seg_forward("o_fwd", seg_o, [xs, o_mat], [gt1], [], [lw[l]['second']], [(D, F32)],
                                   tap_widths=(D,))
            rec.update(q=q_mat, o=o_mat, lse=lse, q_smalls=q_smalls)
        rec['x_mid'] = x_mid
        xs, rec['gate'], rec['up'] = ffn_forward("ffn_fwd", x_mid, g2[l], sh2, sc2, gt2,
                                                 lw[l]['wg'], lw[l]['wu'], lw[l]['wd'])
        saved.append(rec)

    dy, loss_part = loss_kernel("loss", xs, target)
    loss = lax.psum(loss_part[0, 0], ("x", "y", "c"))

    rblk = lambda a: a.reshape(N_DEV, a.shape[0] // N_DEV, a.shape[1])
    cblk = lambda a: a.reshape(a.shape[0], N_DEV, a.shape[1] // N_DEV).transpose(1, 0, 2)
    dmod = [None] * DEPTH
    dk_tot = []
    dv_tot = []
    dx = dy
    sends = [None] * DEPTH
    send_token = jnp.zeros((1, 1), F32)
    g_n1 = [None] * DEPTH
    g_n2 = [None] * DEPTH
    g_bglu = [None] * N_A
    g_dskip = [None] * N_A
    g_s5 = [None] * N_A
    g_qn, g_q128 = [None] * 2, [None] * 2
    for l in range(DEPTH - 1, -1, -1):
        rec = saved[l]
        sh1, sc1, gt1, sh2, sc2, gt2 = mods[l]
        dx, dgate, dup, dyd, h_b, a_b, dg2, dsh2, dsc2, dgt2 = ffn_backward(
            "ffn_bwd", rec['x_mid'], dx, rec['gate'], rec['up'], g2[l], sh2, sc2, gt2 + send_token,
            lw[l]['wg'], lw[l]['wu'], lw[l]['wd'])
        out_l = [matmul_tn("tn_ffn_in", h_b, dgate, _MXU, col_blocks=N_DEV),
                 matmul_tn("tn_ffn_in", h_b, dup, _MXU, col_blocks=N_DEV),
                 matmul_tn("tn_ffn_out", a_b, dyd, _MXU).reshape(N_DEV, FFB, D)]
        g_n2[l] = dg2
        if l == 0:
            sends_ffn0 = exchange_start("a2a_start_ffn0", out_l, False, dx)
            send_token = sends_ffn0[4][0:1, 0:1]
            out_l = []
        if l < N_A:
            (dx, dyy), (dz,), (g_b,), (dgt1, dbg) = seg_backward(
                "glu_bwd", seg_glu, [rec['x_in'], rec['y']], [gt1 + (send_token if l == 0 else 0.0), _row(bglu_full[l])], [],
                [lw[l]['second']],
                [dx], (D,), (D,))
            out_l.append(rblk(matmul_tn("tn_sq", g_b, dz, _MXU)))
            g_bglu[l] = dbg
            wb, wc, a_tab = s5_mats[l]
            dx, dwb, dwc, da, dd, dg1, dsh1, dsc1 = s5_scan_bwd(
                "s5_scan_bwd", rec['x_in'], g1[l], sh1, sc1, dyy, rec['s0'], wb, wc, a_tab, _row(d_full[l]), dx)
            g_dskip[l] = dd
            dab_re, dab_im, dbb_re_t, dbb_im_t, dc_re, dc_im = _s5_unplace(dwb, dwc, da)
            dlr, dli, dldt, dbr_t, dbi_t = s5_prep_bwd("s5_prep_bwd", *s5_raw[l], expand,
                                                       (dab_re, dab_im, dbb_re_t, dbb_im_t))
            g_s5[l] = (dlr.reshape(G, N), dli.reshape(G, N), dldt.reshape(G),
                       dbr_t.reshape(P, G, N).transpose(1, 2, 0), dbi_t.reshape(P, G, N).transpose(1, 2, 0), dc_re, dc_im)
        else:
            j = l - N_A
            (dx, do), (dzo,), (o_b,), (dgt1,) = seg_backward(
                "o_bwd", seg_o, [rec['x_in'], rec['o']], [gt1], [], [lw[l]['second']], [dx], (D,), (D,))
            out_l.append(rblk(matmul_tn("tn_sq", o_b, dzo, _MXU)))
            dq, dk, dv = attn_bwd("attn_bwd", rec['q'], kv['k'], kv['v'], rec['o'], do, rec['lse'])
            dk_tot.append(dk)
            dv_tot.append(dv)
            (dx,), (dql, dqq), (hq_b, qn_b), (dg1, dsh1, dsc1, dqg, dq128) = seg_backward(
                "q_bwd", seg_q, [rec['x_in']], rec['q_smalls'], [cosf, sinf], [lw[l]['wdq'], lw[l]['wuq_pad']],
                [dq], (QL, H * HD), (D, QL), dx_add=dx)
            g_dq = rblk(matmul_tn("tn_dq", hq_b, dql, _MXU))
            g_uq = cblk(_unpad_heads(matmul_tn("tn_uq", qn_b, dqq, _MXU), DN + DR, HD))
            g_qn[j], g_q128[j] = dqg, dq128
        g_n1[l] = dg1
        dmod[l] = jnp.concatenate([dsh1, dsc1, dgt1, dsh2, dsc2, dgt2], axis=1)
        if l == N_A:
            (dx,), (dta, dtk, dtv), (hk_b, ckv_b), (dkg, dksh, dksc, dag, dgkn, dgkr) = seg_backward(
                "kv_bwd", seg_kv, [kv['x_in']], kv['smalls'], [cosf, sinf], kv['w'],
                [dk_tot, dv_tot], (KVL + HD, H * HD, H * DV), (D, KVL), dx_add=dx)
            g_wa = matmul_tn("tn_kva", hk_b, dta, _MXU)
            g_wa = jnp.concatenate([g_wa[:, :KVL], g_wa[:, KVL + DN:KVL + DN + DR]], axis=1)
            g_kn = matmul_tn("tn_kn", ckv_b, dtk, _MXU).reshape(KVL, H, HD)[:, :, :DN]
            g_v = matmul_tn("tn_v", ckv_b, dtv, _MXU).reshape(KVL, H, DV)
            g_wkvb = jnp.concatenate([g_kn, g_v], axis=2).reshape(KVL, H * (DN + DV))
            dkmod = jnp.concatenate([dksh, dksc], axis=1)
            out_l += [jnp.concatenate([cblk(g_wkvb), g_dq], axis=1), rblk(g_wa)]
        if l > N_A:
            out_l.append(g_dq)
        if l >= N_A:
            out_l.append(g_uq)
        if l > 0:
            sends[l] = exchange_start(f"a2a_start_{l}", out_l, False, dx)
            send_token = sends[l][4][0:1, 0:1]
        if l == N_A - 1:
            early_flat = _pack_rows([a.reshape(-1) for a in g_s5[l]])
            early_st = exchange_start("small_start_s5", [early_flat], True, dx)
            send_token = send_token + early_st[4][0:1, 0:1]
    grad_x = dx

    s5_names = ['s5_lam_re', 's5_lam_im', 's5_log_dt', 's5_b_re', 's5_b_im', 's5_c_re', 's5_c_im']
    small = {
        'norm1_g': jnp.concatenate(g_n1, axis=0), 'norm2_g': jnp.concatenate(g_n2, axis=0),
        'kv_norm_g': dkg, 'kv_a_norm_g': dag, 'k_nope_norm_g': dgkn[:, :DN], 'k_rope_norm_g': dgkr[:, DN:DN + DR],
        'mla_q_norm_g': jnp.concatenate(g_qn, axis=0),
        'mla_q_nope_norm_g': jnp.concatenate([g[:, :DN] for g in g_q128], axis=0),
        'mla_q_rope_norm_g': jnp.concatenate([g[:, DN:DN + DR] for g in g_q128], axis=0),
        's5_d': jnp.concatenate(g_dskip, axis=0), 's5_b_glu': jnp.concatenate(g_bglu, axis=0),
    }
    for i, n in enumerate(s5_names):
        small[n] = jnp.stack([g_s5[l][i] for l in range(N_A - 1)])
    small_names = [n for n in REPLICATED if n not in ('ada_b', 'kv_ada_b')] + SHARDED_VEC
    flat_small = _pack_rows([small[n].reshape(-1) for n in small_names])

    dm = jnp.concatenate(dmod + [dkmod], axis=1)[0]
    per_dev = []
    for d in range(N_DEV):
        cols = [dm[6 * D * l + per_l * d:6 * D * l + per_l * (d + 1)] for l in range(DEPTH)]
        cols.append(dm[6 * D * DEPTH + (2 * D // N_DEV) * d:6 * D * DEPTH + (2 * D // N_DEV) * (d + 1)])
        per_dev.append(jnp.concatenate(cols))
    dm_dev = jnp.stack(per_dev)
    gdm = all_gather("gather_dmod", dm_dev)
    small_st = exchange_start("small_start", [flat_small], True, gdm)
    sends[0] = exchange_start("a2a_start_0", out_l, False, small_st[4])
    dm_mine = lax.dynamic_index_in_dim(gdm, me, axis=1, keepdims=False) + sends[0][4][0, 0]
    g_wmod = small_matmul_tn("dmod_matmul", ca_all, dm_mine)
    g_ada_w = jnp.stack([g_wmod[:, per_l * l:per_l * (l + 1)] for l in range(DEPTH)])
    g_kv_ada_w = g_wmod[:, per_l * DEPTH:]
    dm_sum = sum_parts("sum_dmod", gdm.reshape(N_DEV, N_DEV, n_mod))
    g_ada_b = jnp.stack([jnp.concatenate([dm_sum[d, per_l * l:per_l * (l + 1)] for d in range(N_DEV)])
                         for l in range(DEPTH)])
    g_kv_ada_b = jnp.concatenate([dm_sum[d, per_l * DEPTH:] for d in range(N_DEV)])

    grads, out_delta, out_m, out_v = {}, {}, {}, {}

    def update(name, parts, base=0, stride=0):
        shp = W[name].shape
        shp3 = shp if len(shp) == 3 else (1,) + shp
        res = adamw("adamw_" + name, parts, W[name].reshape(shp3), M[name].reshape(shp3), V[name].reshape(shp3),
                    base, stride)
        grads[name], out_delta[name], out_m[name], out_v[name] = (a.reshape(shp) for a in res)

    update('ada_w', g_ada_w.reshape(1, DEPTH * D, per_l), 0, D)
    update('kv_ada_w', g_kv_ada_w[None])

    chains = {}

    def update_layer(name, parts, layer, base=0):
        shp = W[name].shape
        shp3 = shp if len(shp) == 3 else (1,) + shp
        chains[name] = adamw_layer(f"adamw_{name}_{layer}", parts, W[name].reshape(shp3), M[name].reshape(shp3),
                                   V[name].reshape(shp3), layer, chains.get(name), base)
        grads[name], out_delta[name], out_m[name], out_v[name] = (a.reshape(shp) for a in chains[name])

    ffn_parts = [[None] * DEPTH for _ in range(3)]

    def landed(name, started, after):
        lands = exchange_wait(name, started, after, False)
        return [lax.dynamic_update_slice(ld, lax.dynamic_index_in_dim(src, me, 0, keepdims=True), (me,) + (0,) * (src.ndim - 1))
                for ld, src in zip(lands, started[2])]

    def receive(l, after):
        recv = landed(f"a2a_wait_{l}", sends[l], after)
        if l == 0:
            recv = landed("a2a_wait_ffn0", sends_ffn0, after) + recv
        for i in range(3):
            ffn_parts[i][l] = recv[i]
        if l < N_A:
            update_layer('s5_w_glu', recv[3], l)
        else:
            update_layer('mla_w_o', recv[3], l - N_A)
            if l == N_A:
                update_layer('w_kv_b', recv[4], 0)
                update_layer('mla_w_dq', recv[4], 0, KVL)
                update_layer('w_kv_a', recv[5], 0)
            else:
                update_layer('mla_w_dq', recv[4], l - N_A)
            update_layer('mla_w_uq', recv[-1], l - N_A)

    for l in range(DEPTH - 1, 0, -1):
        receive(l, out_delta['kv_ada_w'])

    def gathered_sum(name, started, own, after):
        (land,) = exchange_wait(name + "_wait", started, after, True)
        return sum_parts("sum_" + name, lax.dynamic_update_slice(land, own[None], (me, 0, 0))).reshape(-1)

    early_sum = gathered_sum("small_s5", early_st, early_flat, chains['s5_w_glu'][1])
    g_small_sum = gathered_sum("small", small_st, flat_small, early_sum)
    off = 0
    for n in small_names:
        size = int(np.prod(small[n].shape))
        full = g_small_sum[off:off + size]
        off += size
        if n in SHARDED_VEC:
            full = lax.dynamic_slice_in_dim(full.reshape(N_A, D), me * (D // N_DEV), D // N_DEV, axis=1)
        grads[n] = full.reshape(small[n].shape if n in s5_names else W[n].shape)
    off = 0
    for i, n in enumerate(s5_names):
        size = int(np.prod(g_s5[N_A - 1][i].shape))
        last = early_sum[off:off + size].reshape((1,) + g_s5[N_A - 1][i].shape)
        off += size
        grads[n] = jnp.concatenate([grads[n], last], axis=0)
    grads['ada_b'] = g_ada_b
    grads['kv_ada_b'] = g_kv_ada_b

    big_small = ('s5_b_re', 's5_b_im', 's5_c_re', 's5_c_im')
    packed_names = [n for n in REPLICATED + SHARDED_VEC if n not in big_small]

    def pack(dct):
        flat_ = jnp.concatenate([dct[n].reshape(-1) for n in packed_names])
        n_ = int(flat_.shape[0])
        p_ = -(-n_ // 8192) * 8192
        return jnp.pad(flat_, (0, p_ - n_)).reshape(p_ // 128, 128)

    _, d_p, m_p, v_p = adamw("adamw_small", pack(grads)[None], pack(W)[None], pack(M)[None], pack(V)[None])
    off = 0
    d_p, m_p, v_p = d_p.reshape(-1), m_p.reshape(-1), v_p.reshape(-1)
    for n in packed_names:
        size = int(np.prod(W[n].shape))
        out_delta[n] = d_p[off:off + size].reshape(W[n].shape)
        out_m[n] = m_p[off:off + size].reshape(W[n].shape)
        out_v[n] = v_p[off:off + size].reshape(W[n].shape)
        off += size
    for n in big_small:
        shp = W[n].shape
        view = (1, int(np.prod(shp[:-1])), shp[-1])
        res = adamw("adamw_" + n, grads[n].reshape(view), W[n].reshape(view), M[n].reshape(view), V[n].reshape(view))
        _, out_delta[n], out_m[n], out_v[n] = (a.reshape(shp) for a in res)

    receive(0, d_p)
    for i, name in enumerate(('ffn_w_gate', 'ffn_w_up', 'ffn_w_down')):
        res = adamw_multi("adamw_" + name, ffn_parts[i], W[name], M[name], V[name])
        grads[name], out_delta[name], out_m[name], out_v[name] = res

    return (loss, grad_x[None], *[grads[n] for n in WEIGHT_NAMES], *[out_delta[n] for n in WEIGHT_NAMES],
            *[out_m[n] for n in WEIGHT_NAMES], *[out_v[n] for n in WEIGHT_NAMES])
```

```python
import functools
import math

import numpy as np
import jax
import jax.numpy as jnp
from jax import lax
from jax.experimental import pallas as pl
from jax.experimental.pallas import tpu as pltpu

F32 = jnp.float32
_MXU = jnp.bfloat16
HI = lax.Precision.HIGHEST

D = 1024
DEPTH = 4
N_A = 2
FF = 2816
FFB = 384
FFP = 8 * FFB
N_DEV = 8
G = 64
P = 16
N = 64
GB = 8
NBLK = G // GB
HALF = GB * N
H = 16
HP = H // 2
DN, DR, DV = 64, 32, 64
HD = 128
QL = 256
KVL = 256
CHUNK = 64
ROPE_THETA = 10000.0
ATTN_SCALE = 1.0 / math.sqrt(DN + DR)
LOG2E = 1.4426950408889634
EXP2_SCALE = ATTN_SCALE * LOG2E
EPS = 1e-6
ADAM_LR, ADAM_B1, ADAM_B2, ADAM_EPS, ADAM_WD, ADAM_STEP = 0.001, 0.9, 0.999, 1e-08, 0.01, 10
VMEM_LIMIT = 56 * 1024 * 1024
MESH = pl.DeviceIdType.MESH

TILE_ROW = 256
TILE_ATT = 512
TILE_SCAN = 512


def _params(n_grid):
    return pltpu.CompilerParams(dimension_semantics=("arbitrary",) * n_grid, vmem_limit_bytes=VMEM_LIMIT)


@jax.custom_vjp
def mm(a, w):
    return jnp.dot(a.astype(_MXU), w, preferred_element_type=F32)


def _mm_fwd(a, w):
    return mm(a, w), w


def _mm_bwd(w, g):
    da = lax.dot_general(g.astype(_MXU), w, (((1,), (1,)), ((), ())), preferred_element_type=F32)
    return da, jnp.zeros_like(w)


mm.defvjp(_mm_fwd, _mm_bwd)


def rms(x, g):
    return x * lax.rsqrt(jnp.mean(x * x, axis=-1, keepdims=True) + EPS) * g


def modulate(h, shift, scale):
    return h * (1.0 + scale) + shift


def _lane(n=HD):
    return lax.broadcasted_iota(jnp.int32, (1, n), 1)


def _rot_matrix():
    r = lax.broadcasted_iota(jnp.int32, (HD, HD), 0)
    c = lax.broadcasted_iota(jnp.int32, (HD, HD), 1)
    first = (c >= DN) & (c < DN + DR // 2) & (r == c + DR // 2)
    second = (c >= DN + DR // 2) & (c < DN + DR) & (r == c - DR // 2)
    return jnp.where(first, -1.0, jnp.where(second, 1.0, 0.0)).astype(F32)


def head_norm_rope(xh, g128, cosf, sinf, rot, with_nope):
    lane = _lane()
    m_n = lane < DN
    m_r = (lane >= DN) & (lane < DN + DR)
    sq = xh * xh
    inv_r = lax.rsqrt(jnp.sum(jnp.where(m_r, sq, 0.0), axis=-1, keepdims=True) / DR + EPS)
    if with_nope:
        inv_n = lax.rsqrt(jnp.sum(jnp.where(m_n, sq, 0.0), axis=-1, keepdims=True) / DN + EPS)
        inv = jnp.where(m_n, inv_n, jnp.where(m_r, inv_r, 0.0))
    else:
        inv = jnp.where(m_r, inv_r, 0.0)
    xg = xh * inv * g128
    return xg * cosf + jnp.dot(xg, rot, precision=HI, preferred_element_type=F32) * sinf


def seg_glu(x, y, gt, b, t_z, w):
    g = jax.nn.gelu(y)
    z = mm(g, w) + b + t_z
    return (x + gt * (g * jax.nn.sigmoid(z)),), (g.astype(_MXU),)


def seg_o(x, o, gt, t_o, w):
    return (x + gt * (mm(o, w) + t_o),), (o.astype(_MXU),)


def seg_q(x, g, sh, sc, qg, g128, t_l, t_q, cosf, sinf, wdq, wuq):
    h = modulate(rms(x, g), sh, sc)
    ql = mm(h, wdq) + t_l
    qn = rms(ql, qg)
    q = mm(qn, wuq) + t_q
    rot = _rot_matrix()
    heads = [head_norm_rope(q[:, HD * i:HD * (i + 1)], g128, cosf, sinf, rot, True) for i in range(H)]
    return (jnp.concatenate(heads, axis=1),), (h.astype(_MXU), qn.astype(_MXU))


def seg_kv(x, g, sh, sc, ag, gkn, gkr, t_a, t_k, t_v, cosf, sinf, wa, wkn, wv):
    hk = modulate(rms(x, g), sh, sc)
    kva = mm(hk, wa) + t_a
    ckv = rms(kva[:, :KVL], ag)
    kr = head_norm_rope(kva[:, KVL:KVL + HD], gkr, cosf, sinf, _rot_matrix(), False)
    kn = mm(ckv, wkn) + t_k
    v = mm(ckv, wv) + t_v
    heads = []
    for i in range(H):
        kh = kn[:, HD * i:HD * (i + 1)]
        inv = lax.rsqrt(jnp.sum(kh * kh, axis=-1, keepdims=True) / DN + EPS)
        heads.append(kh * inv * gkn + kr)
    return (jnp.concatenate(heads, axis=1), v), (hk.astype(_MXU), ckv.astype(_MXU))


def _row_call(name, body_fn, rows, fulls, out_rows, out_accs, tile):
    s = rows[0].shape[0]
    n_tiles = s // tile
    n_rows, n_fulls, n_or, n_oa = len(rows), len(fulls), len(out_rows), len(out_accs)

    def kern(*refs):
        i = pl.program_id(0)
        row_v = [r[...] for r in refs[:n_rows]]
        full_v = [r[...] for r in refs[n_rows:n_rows + n_fulls]]
        o_refs = refs[n_rows + n_fulls:]
        ro, ao = body_fn(row_v, full_v)
        for r, v in zip(o_refs[:n_or], ro):
            r[...] = v.astype(r.dtype)
        if n_oa:
            @pl.when(i == 0)
            def _():
                for r in o_refs[n_or:]:
                    r[...] = jnp.zeros(r.shape, r.dtype)
            for r, v in zip(o_refs[n_or:], ao):
                r[...] += v.astype(r.dtype)

    in_specs = [pl.BlockSpec((tile, a.shape[1]), lambda i: (i, 0)) for a in rows]
    for a in fulls:
        big = a.size * a.dtype.itemsize > (1 << 20)
        nd = a.ndim
        in_specs.append(pl.BlockSpec(a.shape, functools.partial(lambda i, nd_: (0,) * nd_, nd_=nd),
                                     **({"pipeline_mode": pl.Buffered(1)} if big else {})))
    out_shape = [jax.ShapeDtypeStruct((s, w), dt) for w, dt in out_rows]
    out_shape += [jax.ShapeDtypeStruct(shp, dt) for shp, dt in out_accs]
    out_specs = [pl.BlockSpec((tile, w), lambda i: (i, 0)) for w, _ in out_rows]
    out_specs += [pl.BlockSpec(shp, functools.partial(lambda i, nd_: (0,) * nd_, nd_=len(shp))) for shp, _ in out_accs]
    res = pl.pallas_call(kern, out_shape=out_shape, grid=(n_tiles,), in_specs=in_specs, out_specs=out_specs,
                         name=name, compiler_params=_params(1))(*rows, *fulls)
    return list(res)


def seg_forward(name, seg, rows, smalls, consts_rows, consts_full, out_widths, tile=TILE_ROW, tap_widths=()):
    n_r, n_s, n_cr = len(rows), len(smalls), len(consts_rows)

    def body(row_v, full_v):
        t = row_v[0].shape[0]
        taps = [jnp.zeros((t, w), F32) for w in tap_widths]
        outs, _ = seg(*row_v[:n_r], *full_v[:n_s], *taps, *row_v[n_r:], *full_v[n_s:])
        return outs, ()

    return _row_call(name, body, list(rows) + list(consts_rows), list(smalls) + list(consts_full),
                     out_widths, [], tile)


def seg_backward(name, seg, rows, smalls, consts_rows, consts_full, cots, tap_widths, aux_widths,
                 dx_add=None, tile=TILE_ROW):
    cot_groups = [list(c) if isinstance(c, (list, tuple)) else [c] for c in cots]
    cot_flat = [a for grp in cot_groups for a in grp]
    n_r, n_s, n_cr, n_c = len(rows), len(smalls), len(consts_rows), len(cot_flat)
    has_add = dx_add is not None

    def body(row_v, full_v):
        t = row_v[0].shape[0]
        prim_rows = row_v[:n_r]
        c_rows = row_v[n_r:n_r + n_cr]
        cot_v = list(row_v[n_r + n_cr:n_r + n_cr + n_c])
        add_v = row_v[n_r + n_cr + n_c] if has_add else None
        small_v = full_v[:n_s]
        c_full = full_v[n_s:]
        taps = [jnp.zeros((t, w), F32) for w in tap_widths]
        cot_sum = []
        for grp in cot_groups:
            parts = [cot_v.pop(0).astype(F32) for _ in grp]
            cot_sum.append(functools.reduce(lambda x_, y_: x_ + y_, parts))

        def f(*args):
            return seg(*args, *c_rows, *c_full)

        _, vjp_fn, aux = jax.vjp(f, *prim_rows, *small_v, *taps, has_aux=True)
        grads = vjp_fn(tuple(cot_sum))
        d_rows = list(grads[:n_r])
        if has_add:
            d_rows[0] = d_rows[0] + add_v
        d_small = grads[n_r:n_r + n_s]
        d_taps = grads[n_r + n_s:]
        return d_rows + list(d_taps) + list(aux), [jnp.sum(g, axis=0, keepdims=True) if g.shape[0] != 1 else g
                                                   for g in d_small]

    all_rows = list(rows) + list(consts_rows) + cot_flat + ([dx_add] if has_add else [])
    out_rows = [(a.shape[1], F32) for a in rows] + [(w, _MXU) for w in tap_widths] + [(w, _MXU) for w in aux_widths]
    out_accs = [((1, a.shape[1]), F32) for a in smalls]
    res = _row_call(name, body, all_rows, list(smalls) + list(consts_full), out_rows, out_accs, tile)
    n_t, n_a = len(tap_widths), len(aux_widths)
    return res[:n_r], res[n_r:n_r + n_t], res[n_r + n_t:n_r + n_t + n_a], res[n_r + n_t + n_a:]


def _split(n):
    if n <= 1024:
        return n
    for t in (1408, 1024, 768, 512, 256, 128):
        if n % t == 0:
            return t
    raise ValueError(n)


def matmul_tn(name, a, b, out_dtype, col_blocks=None):
    s, k1 = a.shape
    _, k2 = b.shape
    tm, ts = _split(k1), 2048
    if col_blocks is None:
        tn, per_step, wblk = _split(k2), 1, None
    else:
        wblk = k2 // col_blocks
        per_step = max(1, min(col_blocks, 1536 // wblk))
        tn = per_step * wblk
    n_s = s // ts

    def kern(a_ref, b_ref, o_ref, acc_ref):
        k = pl.program_id(2)

        @pl.when(k == 0)
        def _():
            acc_ref[...] = jnp.zeros(acc_ref.shape, F32)

        acc_ref[...] += lax.dot_general(a_ref[...], b_ref[...], (((0,), (0,)), ((), ())),
                                        preferred_element_type=F32)

        @pl.when(k == n_s - 1)
        def _():
            if col_blocks is None:
                o_ref[...] = acc_ref[...].astype(o_ref.dtype)
            else:
                for cb in range(per_step):
                    o_ref[cb] = acc_ref[:, wblk * cb:wblk * (cb + 1)].astype(o_ref.dtype)

    if col_blocks is None:
        out_shape = jax.ShapeDtypeStruct((k1, k2), out_dtype)
        out_spec = pl.BlockSpec((tm, tn), lambda i, j, k: (i, j))
    else:
        out_shape = jax.ShapeDtypeStruct((col_blocks, k1, wblk), out_dtype)
        out_spec = pl.BlockSpec((per_step, tm, wblk), lambda i, j, k: (j, i, 0))
    return pl.pallas_call(
        kern, out_shape=out_shape, grid=(k1 // tm, k2 // tn, n_s),
        in_specs=[pl.BlockSpec((ts, tm), lambda i, j, k: (k, i)), pl.BlockSpec((ts, tn), lambda i, j, k: (k, j))],
        out_specs=out_spec,
        scratch_shapes=[pltpu.VMEM((tm, tn), F32)], name=name, compiler_params=_params(3))(a, b)


def ffn_forward(name, x, g, sh, sc, gt, wg, wu, wd, tile=2 * TILE_ROW):
    s = x.shape[0]
    fp = wg.shape[1]
    blk = 2 * FFB
    n_blk = fp // blk

    def kern(x_ref, g_ref, sh_ref, sc_ref, gt_ref, wg_ref, wu_ref, wd_ref, o_ref, gate_ref, up_ref):
        xv = x_ref[...]
        hb = modulate(rms(xv, g_ref[...]), sh_ref[...], sc_ref[...]).astype(_MXU)
        y = jnp.zeros((tile, D), F32)
        for c in range(n_blk):
            cs = slice(blk * c, blk * (c + 1))
            gate = jnp.dot(hb, wg_ref[:, cs], preferred_element_type=F32)
            up = jnp.dot(hb, wu_ref[:, cs], preferred_element_type=F32)
            gate_ref[:, cs] = gate.astype(_MXU)
            up_ref[:, cs] = up.astype(_MXU)
            y = y + jnp.dot((jax.nn.silu(gate) * up).astype(_MXU), wd_ref[cs, :], preferred_element_type=F32)
        o_ref[...] = xv + gt_ref[...] * y

    row = lambda w: pl.BlockSpec((tile, w), lambda i: (i, 0))
    vec = pl.BlockSpec((1, D), lambda i: (0, 0))
    wspec = lambda a: pl.BlockSpec(a.shape, lambda i: (0, 0), pipeline_mode=pl.Buffered(1))
    return pl.pallas_call(
        kern, out_shape=[jax.ShapeDtypeStruct((s, D), F32), jax.ShapeDtypeStruct((s, fp), _MXU),
                         jax.ShapeDtypeStruct((s, fp), _MXU)],
        grid=(s // tile,), in_specs=[row(D), vec, vec, vec, vec, wspec(wg), wspec(wu), wspec(wd)],
        out_specs=[row(D), row(fp), row(fp)], name=name, compiler_params=_params(1))(x, g, sh, sc, gt, wg, wu, wd)


def ffn_backward(name, x, dxo, gate, up, g, sh, sc, gt, wg, wu, wd, tile=TILE_ROW):
    s = x.shape[0]
    fp = wg.shape[1]
    blk = 2 * FFB
    n_blk = fp // blk

    def kern(x_ref, dxo_ref, gate_ref, up_ref, g_ref, sh_ref, sc_ref, gt_ref, wg_ref, wu_ref, wd_ref,
             dx_ref, dg_ref, du_ref, dy_ref, h_ref, a_ref, dgn_ref, dsh_ref, dsc_ref, dgt_ref):
        i = pl.program_id(0)

        @pl.when(i == 0)
        def _():
            for r in (dgn_ref, dsh_ref, dsc_ref, dgt_ref):
                r[...] = jnp.zeros(r.shape, F32)

        dxo = dxo_ref[...]
        h, pre_vjp = jax.vjp(lambda *p: modulate(rms(p[0], p[1]), p[2], p[3]), x_ref[...], g_ref[...], sh_ref[...],
                             sc_ref[...])
        h_ref[...] = h.astype(_MXU)
        dyb = (gt_ref[...] * dxo).astype(_MXU)
        dy_ref[...] = dyb
        y = jnp.zeros((tile, D), F32)
        dh = jnp.zeros((tile, D), F32)
        tr = (((1,), (1,)), ((), ()))
        for c in range(n_blk):
            cs = slice(blk * c, blk * (c + 1))
            gate = gate_ref[:, cs].astype(F32)
            up = up_ref[:, cs].astype(F32)
            sig = jax.nn.sigmoid(gate)
            sl = gate * sig
            ab = (sl * up).astype(_MXU)
            a_ref[:, cs] = ab
            y = y + jnp.dot(ab, wd_ref[cs, :], preferred_element_type=F32)
            da = lax.dot_general(dyb, wd_ref[cs, :], tr, preferred_element_type=F32)
            dgb = (da * up * (sig * (1.0 + gate * (1.0 - sig)))).astype(_MXU)
            dub = (da * sl).astype(_MXU)
            dg_ref[:, cs] = dgb
            du_ref[:, cs] = dub
            dh = dh + lax.dot_general(dgb, wg_ref[:, cs], tr, preferred_element_type=F32) \
                + lax.dot_general(dub, wu_ref[:, cs], tr, preferred_element_type=F32)
        dgt_ref[...] += jnp.sum(dxo * y, axis=0, keepdims=True)
        dx_pre, dgn, dsh, dsc = pre_vjp(dh)
        dx_ref[...] = dxo + dx_pre
        dgn_ref[...] += dgn
        dsh_ref[...] += dsh
        dsc_ref[...] += dsc

    row = lambda w: pl.BlockSpec((tile, w), lambda i: (i, 0))
    vec = pl.BlockSpec((1, D), lambda i: (0, 0))
    wspec = lambda a: pl.BlockSpec(a.shape, lambda i: (0, 0), pipeline_mode=pl.Buffered(1))
    rows_out = [(D, F32), (fp, _MXU), (fp, _MXU), (D, _MXU), (D, _MXU), (fp, _MXU)]
    return pl.pallas_call(
        kern,
        out_shape=[jax.ShapeDtypeStruct((s, w), dt) for w, dt in rows_out] + [jax.ShapeDtypeStruct((1, D), F32)] * 4,
        grid=(s // tile,),
        in_specs=[row(D), row(D), row(fp), row(fp), vec, vec, vec, vec, wspec(wg), wspec(wu), wspec(wd)],
        out_specs=[row(w) for w, _ in rows_out] + [vec] * 4,
        name=name, compiler_params=_params(1))(x, dxo, gate, up, g, sh, sc, gt, wg, wu, wd)


def small_matmul(name, a, w, tn=256):
    m, k = a.shape
    n = w.shape[1]

    def kern(a_ref, w_ref, o_ref):
        o_ref[...] = jnp.dot(a_ref[...].astype(_MXU), w_ref[...].astype(_MXU), preferred_element_type=F32)

    return pl.pallas_call(kern, out_shape=jax.ShapeDtypeStruct((m, n), F32), grid=(n // tn,),
                          in_specs=[pl.BlockSpec((m, k), lambda j: (0, 0)), pl.BlockSpec((k, tn), lambda j: (0, j))],
                          out_specs=pl.BlockSpec((m, tn), lambda j: (0, j)), name=name,
                          compiler_params=_params(1))(a, w)


def small_matmul_tn(name, a, b, tn=256):
    m, k = a.shape
    n = b.shape[1]

    def kern(a_ref, b_ref, o_ref):
        o_ref[...] = lax.dot_general(a_ref[...].astype(_MXU), b_ref[...].astype(_MXU), (((0,), (0,)), ((), ())),
                                     preferred_element_type=F32)

    return pl.pallas_call(kern, out_shape=jax.ShapeDtypeStruct((k, n), F32), grid=(n // tn,),
                          in_specs=[pl.BlockSpec((m, k), lambda j: (0, 0)), pl.BlockSpec((m, tn), lambda j: (0, j))],
                          out_specs=pl.BlockSpec((k, tn), lambda j: (0, j)), name=name,
                          compiler_params=_params(1))(a, b)


def _s5_prep_math(lam_re, lam_im, log_dt, b_re_t, b_im_t, expand):
    dt = jnp.dot(jnp.exp(log_dt), expand, precision=HI, preferred_element_type=F32)
    mag = jnp.exp(lam_re * dt)
    ab_re = mag * jnp.cos(lam_im * dt)
    ab_im = mag * jnp.sin(lam_im * dt)
    den = lam_re * lam_re + lam_im * lam_im
    nr = ab_re - 1.0
    ni = ab_im
    f_re = (nr * lam_re + ni * lam_im) / den
    f_im = (ni * lam_re - nr * lam_im) / den
    bb_re = f_re * b_re_t - f_im * b_im_t
    bb_im = f_re * b_im_t + f_im * b_re_t
    return ab_re, ab_im, bb_re, bb_im


def _whole(kern, name, out_shape, *args):
    return pl.pallas_call(kern, out_shape=out_shape, name=name,
                          compiler_params=pltpu.CompilerParams(vmem_limit_bytes=VMEM_LIMIT))(*args)


def s5_prep_fwd(name, lam_re, lam_im, log_dt, b_re_t, b_im_t, expand):
    def kern(a, b, c, d, e, f, o0, o1, o2, o3):
        r = _s5_prep_math(a[...], b[...], c[...], d[...], e[...], f[...])
        for o, v in zip((o0, o1, o2, o3), r):
            o[...] = v

    gn = lam_re.shape[1]
    shp = [jax.ShapeDtypeStruct((1, gn), F32)] * 2 + [jax.ShapeDtypeStruct((P, gn), F32)] * 2
    return _whole(kern, name, shp, lam_re, lam_im, log_dt, b_re_t, b_im_t, expand)


def s5_prep_bwd(name, lam_re, lam_im, log_dt, b_re_t, b_im_t, expand, cots):
    def kern(a, b, c, d, e, f, c0, c1, c2, c3, o0, o1, o2, o3, o4):
        ex = f[...]
        _, vjp_fn = jax.vjp(lambda *p: _s5_prep_math(*p, ex), a[...], b[...], c[...], d[...], e[...])
        g = vjp_fn((c0[...], c1[...], c2[...], c3[...]))
        for o, v in zip((o0, o1, o2, o3, o4), g):
            o[...] = v

    shp = [jax.ShapeDtypeStruct(a.shape, F32) for a in (lam_re, lam_im, log_dt, b_re_t, b_im_t)]
    return _whole(kern, name, shp, lam_re, lam_im, log_dt, b_re_t, b_im_t, expand, *cots)


def _cpowers(ar, ai):
    pw = [(ar, ai)]
    for _ in range(7):
        pr, pi = pw[-1]
        pw.append((pr * ar - pi * ai, pr * ai + pi * ar))
    return pw


def _row_select(row, values):
    out = jnp.broadcast_to(values[7], (8, values[7].shape[1]))
    for r in range(6, -1, -1):
        out = jnp.where(row == r, values[r], out)
    return out


def _scan_tables(ar, ai, reverse):
    pw = _cpowers(ar, ai)
    row = lax.broadcasted_iota(jnp.int32, (8, ar.shape[1]), 0)
    steps = []
    for d in (1, 2, 4):
        keep = (row <= 7 - d) if reverse else (row >= d)
        steps.append((jnp.where(keep, pw[d - 1][0], 0.0), jnp.where(keep, pw[d - 1][1], 0.0)))
    order = list(range(7, -1, -1)) if reverse else list(range(8))
    carry = (_row_select(row, [pw[i][0] for i in order]), _row_select(row, [pw[i][1] for i in order]))
    return steps, carry


def _tile_scan_fwd(xr, xi, cr, ci, steps, carry_m):
    for d, (mr, mi) in zip((1, 2, 4), steps):
        sr = pltpu.roll(xr, d, 0)
        si = pltpu.roll(xi, d, 0)
        xr, xi = xr + mr * sr - mi * si, xi + mr * si + mi * sr
    pr, pi = carry_m
    return xr + pr * cr - pi * ci, xi + pr * ci + pi * cr


def _tile_scan_rev(xr, xi, cr, ci, steps, carry_m):
    for d, (mr, mi) in zip((1, 2, 4), steps):
        sr = pltpu.roll(xr, 8 - d, 0)
        si = pltpu.roll(xi, 8 - d, 0)
        xr, xi = xr + mr * sr + mi * si, xi + mr * si - mi * sr
    pr, pi = carry_m
    return xr + pr * cr + pi * ci, xi + pr * ci - pi * cr


def _fwd_scan_block(buf, row0, n_tiles8, ar, ai, c0r, c0i):
    steps, carry_m = _scan_tables(ar, ai, False)

    def body(j, carry):
        cr, ci = carry
        r0 = pl.multiple_of(row0 + j * 8, 8)
        xr = buf[pl.ds(r0, 8), 0:HALF]
        xi = buf[pl.ds(r0, 8), HALF:2 * HALF]
        xr, xi = _tile_scan_fwd(xr, xi, cr, ci, steps, carry_m)
        buf[pl.ds(r0, 8), 0:HALF] = xr
        buf[pl.ds(r0, 8), HALF:2 * HALF] = xi
        return xr[7:8], xi[7:8]

    return lax.fori_loop(0, n_tiles8, body, (c0r, c0i))


def s5_scan_fwd(name, x, g, sh, sc, wb, wc, a_tab, dskip, tile=TILE_SCAN):
    s = x.shape[0]
    n_t = s // tile

    def kern(x_ref, g_ref, sh_ref, sc_ref, wb_ref, wc_ref, a_ref, d_ref, y_ref, s0_ref, carry_ref, buf, hbuf):
        i = pl.program_id(0)

        @pl.when(i == 0)
        def _():
            carry_ref[...] = jnp.zeros(carry_ref.shape, F32)

        s0_ref[0] = carry_ref[...]
        hbuf[...] = modulate(rms(x_ref[...], g_ref[...]), sh_ref[...], sc_ref[...])
        for k in range(NBLK):
            cols = slice(GB * P * k, GB * P * (k + 1))
            u = hbuf[:, cols]
            buf[...] = jnp.dot(u.astype(_MXU), wb_ref[k], preferred_element_type=F32)
            ar = a_ref[k, :, 0:HALF]
            ai = a_ref[k, :, HALF:2 * HALF]
            cr, ci = _fwd_scan_block(buf, 0, tile // 8, ar, ai, carry_ref[k:k + 1, 0:HALF],
                                     carry_ref[k:k + 1, HALF:2 * HALF])
            carry_ref[k:k + 1, 0:HALF] = cr
            carry_ref[k:k + 1, HALF:2 * HALF] = ci
            y_ref[:, cols] = jnp.dot(buf[...].astype(_MXU), wc_ref[k], preferred_element_type=F32) + d_ref[:, cols] * u

    full = lambda a: pl.BlockSpec(a.shape, functools.partial(lambda i, nd_: (0,) * nd_, nd_=a.ndim))
    return pl.pallas_call(
        kern,
        out_shape=[jax.ShapeDtypeStruct((s, D), F32), jax.ShapeDtypeStruct((n_t, NBLK, 2 * HALF), F32)],
        grid=(n_t,),
        in_specs=[pl.BlockSpec((tile, D), lambda i: (i, 0)), full(g), full(sh), full(sc), full(wb), full(wc), full(a_tab),
                  full(dskip)],
        out_specs=[pl.BlockSpec((tile, D), lambda i: (i, 0)), pl.BlockSpec((1, NBLK, 2 * HALF), lambda i: (i, 0, 0))],
        scratch_shapes=[pltpu.VMEM((NBLK, 2 * HALF), F32), pltpu.VMEM((tile, 2 * HALF), F32), pltpu.VMEM((tile, D), F32)],
        name=name, compiler_params=_params(1))(x, g, sh, sc, wb, wc, a_tab, dskip)


def s5_scan_bwd(name, x, g, sh, sc, dy, s0, wb, wc, a_tab, dskip, dx_add, tile=TILE_SCAN):
    s = x.shape[0]
    n_t = s // tile
    n8 = tile // 8

    def pre(x_, g_, sh_, sc_):
        return modulate(rms(x_, g_), sh_, sc_)

    def kern(x_ref, g_ref, sh_ref, sc_ref, dy_ref, s0_ref, wb_ref, wc_ref, a_ref, d_ref, add_ref,
             dx_ref, dwb_ref, dwc_ref, da_ref, dd_ref, dgn_ref, dsh_ref, dsc_ref, lam_ref, sbuf, gbuf, hbuf, dh_ref):
        i = pl.program_id(0)

        @pl.when(i == 0)
        def _():
            lam_ref[...] = jnp.zeros(lam_ref.shape, F32)
            for r in (dwb_ref, dwc_ref, da_ref, dd_ref, dgn_ref, dsh_ref, dsc_ref):
                r[...] = jnp.zeros(r.shape, F32)

        hbuf[...] = pre(x_ref[...], g_ref[...], sh_ref[...], sc_ref[...])
        for k in range(NBLK):
            cols = slice(GB * P * k, GB * P * (k + 1))
            u = hbuf[:, cols]
            dyk = dy_ref[:, cols]
            ar = a_ref[k, :, 0:HALF]
            ai = a_ref[k, :, HALF:2 * HALF]
            sbuf[0:8, :] = jnp.broadcast_to(s0_ref[0, k:k + 1, :], (8, 2 * HALF))
            sbuf[8:tile + 8, :] = jnp.dot(u.astype(_MXU), wb_ref[k], preferred_element_type=F32)
            _fwd_scan_block(sbuf, 8, n8, ar, ai, s0_ref[0, k:k + 1, 0:HALF], s0_ref[0, k:k + 1, HALF:2 * HALF])
            dyb = dyk.astype(_MXU)
            gbuf[...] = lax.dot_general(dyb, wc_ref[k], (((1,), (1,)), ((), ())), preferred_element_type=F32)
            dwc_ref[k] += lax.dot_general(sbuf[8:tile + 8, :].astype(_MXU), dyb, (((0,), (0,)), ((), ())),
                                          preferred_element_type=F32)
            steps, carry_m = _scan_tables(ar, ai, True)
            row = lax.broadcasted_iota(jnp.int32, (8, HALF), 0)

            def body(jj, carry):
                cr, ci, dar, dai = carry
                j = n8 - 1 - jj
                r0 = pl.multiple_of(j * 8, 8)
                xr = gbuf[pl.ds(r0, 8), 0:HALF]
                xi = gbuf[pl.ds(r0, 8), HALF:2 * HALF]
                xr, xi = _tile_scan_rev(xr, xi, cr, ci, steps, carry_m)
                gbuf[pl.ds(r0, 8), 0:HALF] = xr
                gbuf[pl.ds(r0, 8), HALF:2 * HALF] = xi
                r1 = pl.multiple_of(j * 8 + 8, 8)
                spr = jnp.where(row == 0, sbuf[pl.ds(r0, 8), 0:HALF][7:8],
                                pltpu.roll(sbuf[pl.ds(r1, 8), 0:HALF], 1, 0))
                spi = jnp.where(row == 0, sbuf[pl.ds(r0, 8), HALF:2 * HALF][7:8],
                                pltpu.roll(sbuf[pl.ds(r1, 8), HALF:2 * HALF], 1, 0))
                dar = dar + xr * spr + xi * spi
                dai = dai + xi * spr - xr * spi
                return xr[0:1], xi[0:1], dar, dai

            z8 = jnp.zeros((8, HALF), F32)
            cr, ci, dar, dai = lax.fori_loop(
                0, n8, body, (lam_ref[k:k + 1, 0:HALF], lam_ref[k:k + 1, HALF:2 * HALF], z8, z8))
            lam_ref[k:k + 1, 0:HALF] = cr
            lam_ref[k:k + 1, HALF:2 * HALF] = ci
            da_ref[k:k + 1, 0:HALF] += jnp.sum(dar, axis=0, keepdims=True)
            da_ref[k:k + 1, HALF:2 * HALF] += jnp.sum(dai, axis=0, keepdims=True)
            lam = gbuf[...].astype(_MXU)
            dwb_ref[k] += lax.dot_general(u.astype(_MXU), lam, (((0,), (0,)), ((), ())), preferred_element_type=F32)
            du = lax.dot_general(lam, wb_ref[k], (((1,), (1,)), ((), ())), preferred_element_type=F32)
            dh_ref[:, cols] = du + d_ref[:, cols] * dyk
            dd_ref[:, cols] += jnp.sum(dyk * u, axis=0, keepdims=True)

        _, pre_vjp = jax.vjp(pre, x_ref[...], g_ref[...], sh_ref[...], sc_ref[...])
        dxp, dgn, dsh, dsc = pre_vjp(dh_ref[...])
        dx_ref[...] = dxp + add_ref[...]
        dgn_ref[...] += dgn
        dsh_ref[...] += dsh
        dsc_ref[...] += dsc

    full = lambda a: pl.BlockSpec(a.shape, functools.partial(lambda i, nd_: (0,) * nd_, nd_=a.ndim))
    fullo = lambda shp: pl.BlockSpec(shp, functools.partial(lambda i, nd_: (0,) * nd_, nd_=len(shp)))
    rev = lambda i: (n_t - 1 - i, 0)
    rows = pl.BlockSpec((tile, D), rev)
    return pl.pallas_call(
        kern,
        out_shape=[jax.ShapeDtypeStruct((s, D), F32), jax.ShapeDtypeStruct(wb.shape, F32),
                   jax.ShapeDtypeStruct(wc.shape, F32), jax.ShapeDtypeStruct((NBLK, 2 * HALF), F32),
                   jax.ShapeDtypeStruct((1, D), F32)] + [jax.ShapeDtypeStruct((1, D), F32)] * 3,
        grid=(n_t,),
        in_specs=[rows, full(g), full(sh), full(sc), rows,
                  pl.BlockSpec((1, NBLK, 2 * HALF), lambda i: (n_t - 1 - i, 0, 0)),
                  full(wb), full(wc), full(a_tab), full(dskip), rows],
        out_specs=[rows, fullo(wb.shape), fullo(wc.shape), fullo((NBLK, 2 * HALF)), fullo((1, D))] + [fullo((1, D))] * 3,
        scratch_shapes=[pltpu.VMEM((NBLK, 2 * HALF), F32), pltpu.VMEM((tile + 8, 2 * HALF), F32),
                        pltpu.VMEM((tile, 2 * HALF), F32), pltpu.VMEM((tile, D), F32), pltpu.VMEM((tile, D), F32)],
        name=name, compiler_params=_params(1))(x, g, sh, sc, dy, s0, wb, wc, a_tab, dskip, dx_add)


def _chunk_mask(q0, k0, tq, tk):
    r = (q0 + lax.broadcasted_iota(jnp.int32, (tq, tk), 0)) // CHUNK
    c = (k0 + lax.broadcasted_iota(jnp.int32, (tq, tk), 1)) // CHUNK
    return r >= c


def _head_lanes(j):
    lane = _lane(2 * DV)
    return (lane >= DV * j) & (lane < DV * (j + 1))


def _raw_scores(q, kblk, masked, t):
    s = lax.dot_general(q, kblk, (((1,), (1,)), ((), ())), preferred_element_type=F32)
    return jnp.where(_chunk_mask(0, 0, t, t), s, -1e30) if masked else s


def attn_fwd(name, q, k, v, t=TILE_ATT, tk=TILE_ATT):
    s = q.shape[0]
    n_q = s // t
    r = t // tk

    def kern(q_ref, k_ref, v_ref, o_ref, lse_ref):
        qi = pl.program_id(1)
        qs = [q_ref[:, HD * j:HD * (j + 1)] for j in range(2)]

        def absorb(k0, carry, mask):
            vblk = v_ref[pl.ds(k0, tk), :]
            scs = [lax.dot_general(qs[j], k_ref[pl.ds(k0, tk), HD * j:HD * (j + 1)], (((1,), (1,)), ((), ())),
                                   preferred_element_type=F32) for j in range(2)]
            if mask is not None:
                scs = [jnp.where(mask, sc, -1e30) for sc in scs]
            m_new = [jnp.maximum(carry[j][0], jnp.max(scs[j], axis=-1, keepdims=True)) for j in range(2)]
            ps = [jnp.exp2((scs[j] - m_new[j]) * EXP2_SCALE) for j in range(2)]
            alphas = [jnp.exp2((carry[j][0] - m_new[j]) * EXP2_SCALE) for j in range(2)]
            pvs = [jnp.dot(ps[j].astype(_MXU), vblk, preferred_element_type=F32) for j in range(2)]
            return tuple((m_new[j], alphas[j] * carry[j][1] + jnp.sum(ps[j], axis=-1, keepdims=True),
                          alphas[j] * carry[j][2] + pvs[j]) for j in range(2))

        init = tuple((jnp.full((t, 1), -1e30, F32), jnp.zeros((t, 1), F32), jnp.zeros((t, 2 * DV), F32))
                     for _ in range(2))
        carry = lax.fori_loop(0, qi * r, lambda kb, c: absorb(pl.multiple_of(kb * tk, tk), c, None), init)
        for i in range(r):
            carry = absorb(pl.multiple_of(qi * t + i * tk, tk), carry, _chunk_mask(0, i * tk, t, tk))
        outs = []
        for j in range(2):
            m, l, acc = carry[j]
            outs.append(acc / l)
            lse_ref[0, j] = m * ATTN_SCALE + jnp.log(l)
        o_ref[...] = jnp.where(_head_lanes(0), outs[0], outs[1])

    return pl.pallas_call(
        kern,
        out_shape=[jax.ShapeDtypeStruct((s, H * DV), F32), jax.ShapeDtypeStruct((HP, 2, s, 1), F32)],
        grid=(HP, n_q),
        in_specs=[pl.BlockSpec((t, 2 * HD), lambda hp, i: (i, hp)), pl.BlockSpec((s, 2 * HD), lambda hp, i: (0, hp)),
                  pl.BlockSpec((s, 2 * DV), lambda hp, i: (0, hp))],
        out_specs=[pl.BlockSpec((t, 2 * DV), lambda hp, i: (i, hp)),
                   pl.BlockSpec((1, 2, t, 1), lambda hp, i: (hp, 0, i, 0))],
        name=name, compiler_params=_params(2))(q, k, v)


def attn_bwd(name, q, k, v, o, do, lse, t=TILE_ATT):
    s = q.shape[0]
    n_q = s // t

    def kern(q_ref, k_ref, v_ref, o_ref, do_ref, lse_ref, dq_ref, dk_ref, dv_ref):
        qi = pl.program_id(1)

        @pl.when(qi == 0)
        def _():
            dk_ref[...] = jnp.zeros(dk_ref.shape, F32)
            dv_ref[...] = jnp.zeros(dv_ref.shape, F32)

        qs, doms, deltas, lse2 = [], [], [], []
        for j in range(2):
            qs.append(q_ref[:, HD * j:HD * (j + 1)])
            dom = jnp.where(_head_lanes(j), do_ref[...], 0.0)
            deltas.append(jnp.sum(dom * o_ref[...], axis=-1, keepdims=True))
            doms.append(dom.astype(_MXU))
            lse2.append(lse_ref[0, j] * LOG2E)

        def block(k0, dqs, masked):
            vblk = v_ref[pl.ds(k0, t), :]
            kblks = [k_ref[pl.ds(k0, t), HD * j:HD * (j + 1)] for j in range(2)]
            scs = [_raw_scores(qs[j], kblks[j], masked, t) for j in range(2)]
            dps = [lax.dot_general(doms[j], vblk, (((1,), (1,)), ((), ())), preferred_element_type=F32)
                   for j in range(2)]
            ps = [jnp.exp2(scs[j] * EXP2_SCALE - lse2[j]) for j in range(2)]
            dss = [(ps[j] * (dps[j] - deltas[j])).astype(_MXU) for j in range(2)]
            pbs = [ps[j].astype(_MXU) for j in range(2)]
            new = tuple(dqs[j] + jnp.dot(dss[j], kblks[j], preferred_element_type=F32) for j in range(2))
            for j in range(2):
                dk_ref[pl.ds(k0, t), HD * j:HD * (j + 1)] += lax.dot_general(
                    dss[j], qs[j], (((0,), (0,)), ((), ())), preferred_element_type=F32)
            dvs = [lax.dot_general(pbs[j], doms[j], (((0,), (0,)), ((), ())), preferred_element_type=F32)
                   for j in range(2)]
            dv_ref[pl.ds(k0, t), :] += dvs[0] + dvs[1]
            return new

        init = (jnp.zeros((t, HD), F32), jnp.zeros((t, HD), F32))
        dqs = lax.fori_loop(0, qi, lambda kb, c: block(pl.multiple_of(kb * t, t), c, False), init)
        dqs = block(pl.multiple_of(qi * t, t), dqs, True)
        for j in range(2):
            dq_ref[:, HD * j:HD * (j + 1)] = dqs[j] * ATTN_SCALE

        @pl.when(qi == n_q - 1)
        def _():
            dk_ref[...] = dk_ref[...] * ATTN_SCALE

    return pl.pallas_call(
        kern,
        out_shape=[jax.ShapeDtypeStruct((s, H * HD), F32), jax.ShapeDtypeStruct((s, H * HD), F32),
                   jax.ShapeDtypeStruct((s, H * DV), F32)],
        grid=(HP, n_q),
        in_specs=[pl.BlockSpec((t, 2 * HD), lambda hp, i: (i, hp)), pl.BlockSpec((s, 2 * HD), lambda hp, i: (0, hp)),
                  pl.BlockSpec((s, 2 * DV), lambda hp, i: (0, hp)), pl.BlockSpec((t, 2 * DV), lambda hp, i: (i, hp)),
                  pl.BlockSpec((t, 2 * DV), lambda hp, i: (i, hp)),
                  pl.BlockSpec((1, 2, t, 1), lambda hp, i: (hp, 0, i, 0))],
        out_specs=[pl.BlockSpec((t, 2 * HD), lambda hp, i: (i, hp)), pl.BlockSpec((s, 2 * HD), lambda hp, i: (0, hp)),
                   pl.BlockSpec((s, 2 * DV), lambda hp, i: (0, hp))],
        name=name, compiler_params=_params(2))(q, k, v, o, do, lse)


def rope_tables(name, pos_col, inv128):
    s = pos_col.shape[0]

    def kern(p_ref, inv_ref, c_ref, s_ref):
        ang = p_ref[...].astype(F32) * inv_ref[...]
        lane = _lane()
        m_r = (lane >= DN) & (lane < DN + DR)
        c_ref[...] = jnp.where(lane < DN, 1.0, jnp.where(m_r, jnp.cos(ang), 0.0))
        s_ref[...] = jnp.where(m_r, jnp.sin(ang), 0.0)

    return _whole(kern, name, [jax.ShapeDtypeStruct((s, HD), F32)] * 2, pos_col, inv128)


def loss_kernel(name, y, tgt, tile=TILE_ROW):
    def body(row_v, _):
        err = row_v[0] - row_v[1]
        part = 0.5 * jnp.sum(jnp.mean(err * err, axis=-1, keepdims=True), axis=0, keepdims=True)
        return [err * (1.0 / D)], [jnp.broadcast_to(part, (1, 128))]

    return _row_call(name, body, [y, tgt], [], [(D, F32)], [((1, 128), F32)], tile)


def _row_tile(r, c):
    cap = max(8, (1 << 18) // max(c, 1))
    for t in (2048, 1024, 512, 256, 128, 64, 32, 16, 8):
        if t <= cap and r % t == 0:
            return t
    return r


def sum_parts(name, parts):
    n, r, c = parts.shape
    t = _row_tile(r, c)

    def kern(p_ref, o_ref):
        acc = p_ref[0].astype(F32)
        for i in range(1, n):
            acc = acc + p_ref[i].astype(F32)
        o_ref[...] = acc

    return pl.pallas_call(kern, out_shape=jax.ShapeDtypeStruct((r, c), F32), grid=(r // t,),
                          in_specs=[pl.BlockSpec((n, t, c), lambda i: (0, i, 0))],
                          out_specs=pl.BlockSpec((t, c), lambda i: (i, 0)), name=name, compiler_params=_params(1))(parts)


def adamw(name, parts, w, m, v, base=0, stride=0):
    n, _, cp = parts.shape
    nl, r, c = w.shape
    t = _row_tile(math.gcd(math.gcd(r, base), stride), max(c, cp))
    c1 = 1.0 / (1.0 - ADAM_B1 ** ADAM_STEP)
    c2 = 1.0 / (1.0 - ADAM_B2 ** ADAM_STEP)

    def kern(p_ref, w_ref, m_ref, v_ref, g_ref, d_ref, nm_ref, nv_ref):
        g = p_ref[0].astype(F32)
        for i in range(1, n):
            g = g + p_ref[i].astype(F32)
        g = g[:, :c]
        nm = ADAM_B1 * m_ref[...] + (1.0 - ADAM_B1) * g
        nv = ADAM_B2 * v_ref[...] + (1.0 - ADAM_B2) * (g * g)
        g_ref[...] = g
        nm_ref[...] = nm
        nv_ref[...] = nv
        d_ref[...] = -ADAM_LR * ((nm * c1) / (jnp.sqrt(nv * c2) + ADAM_EPS) + ADAM_WD * w_ref[...])

    spec = pl.BlockSpec((None, t, c), lambda l, i: (l, i, 0))
    pspec = pl.BlockSpec((n, t, cp), lambda l, i: (0, (base + l * stride) // t + i, 0))
    return pl.pallas_call(kern, out_shape=[jax.ShapeDtypeStruct((nl, r, c), F32)] * 4, grid=(nl, r // t),
                          in_specs=[pspec, spec, spec, spec], out_specs=[spec] * 4, name=name,
                          compiler_params=_params(2))(parts, w, m, v)


def adamw_multi(name, parts_list, w, m, v):
    nl, r, c = w.shape
    n, _, cp = parts_list[0].shape
    t = _row_tile(r, max(c, cp))
    c1 = 1.0 / (1.0 - ADAM_B1 ** ADAM_STEP)
    c2 = 1.0 / (1.0 - ADAM_B2 ** ADAM_STEP)

    def kern(*refs):
        p_refs = refs[:nl]
        w_ref, m_ref, v_ref, g_ref, d_ref, nm_ref, nv_ref = refs[nl:]
        layer = pl.program_id(0)
        for ll in range(nl):
            @pl.when(layer == ll)
            def _(ll=ll):
                g = p_refs[ll][0].astype(F32)
                for i in range(1, n):
                    g = g + p_refs[ll][i].astype(F32)
                g = g[:, :c]
                nm = ADAM_B1 * m_ref[...] + (1.0 - ADAM_B1) * g
                nv = ADAM_B2 * v_ref[...] + (1.0 - ADAM_B2) * (g * g)
                g_ref[...] = g
                nm_ref[...] = nm
                nv_ref[...] = nv
                d_ref[...] = -ADAM_LR * ((nm * c1) / (jnp.sqrt(nv * c2) + ADAM_EPS) + ADAM_WD * w_ref[...])

    spec = pl.BlockSpec((None, t, c), lambda l, i: (l, i, 0))
    pspecs = [pl.BlockSpec((n, t, cp), functools.partial(lambda l, i, ll_: (0, jnp.where(l == ll_, i, 0), 0), ll_=ll))
              for ll in range(nl)]
    return pl.pallas_call(kern, out_shape=[jax.ShapeDtypeStruct((nl, r, c), F32)] * 4, grid=(nl, r // t),
                          in_specs=pspecs + [spec, spec, spec], out_specs=[spec] * 4, name=name,
                          compiler_params=_params(2))(*parts_list, w, m, v)


def adamw_layer(name, parts, w, m, v, layer, prev, base=0):
    n, _, cp = parts.shape
    nl, r, c = w.shape
    t = _row_tile(math.gcd(r, base), max(c, cp))
    c1 = 1.0 / (1.0 - ADAM_B1 ** ADAM_STEP)
    c2 = 1.0 / (1.0 - ADAM_B2 ** ADAM_STEP)
    chained = nl > 1

    def kern(p_ref, w_ref, m_ref, v_ref, *rest):
        g_ref, d_ref, nm_ref, nv_ref = rest[-4:]
        g = p_ref[0].astype(F32)
        for i in range(1, n):
            g = g + p_ref[i].astype(F32)
        g = g[:, :c]
        nm = ADAM_B1 * m_ref[...] + (1.0 - ADAM_B1) * g
        nv = ADAM_B2 * v_ref[...] + (1.0 - ADAM_B2) * (g * g)
        g_ref[...] = g
        nm_ref[...] = nm
        nv_ref[...] = nv
        d_ref[...] = -ADAM_LR * ((nm * c1) / (jnp.sqrt(nv * c2) + ADAM_EPS) + ADAM_WD * w_ref[...])

    spec = pl.BlockSpec((None, t, c), lambda i: (layer, i, 0))
    pspec = pl.BlockSpec((n, t, cp), lambda i: (0, base // t + i, 0))
    in_specs = [pspec, spec, spec, spec]
    args = [parts, w, m, v]
    aliases = {}
    if chained:
        if prev is None:
            prev = [lax.empty((nl, r, c), F32) for _ in range(4)]
        in_specs += [pl.BlockSpec(memory_space=pl.ANY)] * 4
        args += list(prev)
        aliases = {4 + i: i for i in range(4)}
    return pl.pallas_call(kern, out_shape=[jax.ShapeDtypeStruct((nl, r, c), F32)] * 4, grid=(r // t,),
                          in_specs=in_specs, out_specs=[spec] * 4, input_output_aliases=aliases, name=name,
                          compiler_params=_params(1))(*args)


def _me():
    return lax.axis_index("x"), lax.axis_index("y"), lax.axis_index("c")


def _flip(x, y, c, mask):
    return (jnp.where((mask >> 2) & 1, 1 - x, x), jnp.where((mask >> 1) & 1, 1 - y, y), jnp.where(mask & 1, 1 - c, c))


def _index(x, y, c):
    return 4 * x + 2 * y + c


def _exchange(name, arr, gather):
    out_shape = (N_DEV,) + arr.shape if gather else arr.shape

    def kern(in_ref, out_ref, send_sems, recv_sems, local_sem):
        x, y, c = _me()
        me = _index(x, y, c)
        mine = pltpu.make_async_copy(in_ref if gather else in_ref.at[me], out_ref.at[me], local_sem)
        mine.start()
        copies = []
        for mask in range(1, N_DEV):
            px, py, pc = _flip(x, y, c, mask)
            peer = _index(px, py, pc)
            cp = pltpu.make_async_remote_copy(
                src_ref=in_ref if gather else in_ref.at[peer], dst_ref=out_ref.at[me],
                send_sem=send_sems.at[mask - 1], recv_sem=recv_sems.at[mask - 1],
                device_id=(px, py, pc), device_id_type=MESH)
            cp.start()
            copies.append((cp, peer))
        for mask, (cp, peer) in enumerate(copies, start=1):
            pltpu.make_async_remote_copy(
                src_ref=in_ref if gather else in_ref.at[peer], dst_ref=out_ref.at[peer],
                send_sem=send_sems.at[mask - 1], recv_sem=recv_sems.at[mask - 1],
                device_id=_flip(x, y, c, mask), device_id_type=MESH).wait_recv()
        for cp, _ in copies:
            cp.wait_send()
        mine.wait()

    any_spec = pl.BlockSpec(memory_space=pl.ANY)
    return pl.pallas_call(
        kern, out_shape=jax.ShapeDtypeStruct(out_shape, arr.dtype), in_specs=[any_spec], out_specs=any_spec,
        scratch_shapes=[pltpu.SemaphoreType.DMA((N_DEV - 1,)), pltpu.SemaphoreType.DMA((N_DEV - 1,)),
                        pltpu.SemaphoreType.DMA],
        name=name, compiler_params=pltpu.CompilerParams(has_side_effects=True))(arr)


def all_gather(name, arr):
    return _exchange(name, arr, True)


_HBM = pl.BlockSpec(memory_space=pltpu.HBM)
_SEM = pl.BlockSpec(memory_space=pltpu.SEMAPHORE)
_EFFECT = pltpu.SideEffectType.DATAFLOW_SIDE_EFFECTING


def _split_copies(srcs, lands, send_sems, recv_sems, gather):
    x, y, c = _me()
    me = _index(x, y, c)
    out = []
    for a, (src, land) in enumerate(zip(srcs, lands)):
        for mask in range(1, N_DEV):
            px, py, pc = _flip(x, y, c, mask)
            peer = _index(px, py, pc)
            sem = (N_DEV - 1) * a + mask - 1
            mk = lambda dst_slot: pltpu.make_async_remote_copy(
                src_ref=src if gather else src.at[peer], dst_ref=land.at[dst_slot],
                send_sem=send_sems.at[sem], recv_sem=recv_sems.at[sem], device_id=(px, py, pc), device_id_type=MESH)
            out.append((mk(me), mk(peer)))
    return out


def exchange_start(name, arrs, gather, after):
    k = len(arrs)
    land_shapes = [((N_DEV,) + a.shape if gather else a.shape) for a in arrs]

    def body(*refs):
        srcs, lands = refs[:k], refs[k:2 * k]
        send_sems, recv_sems = refs[2 * k + 1], refs[2 * k + 2]
        token = refs[-1]
        for mine, _ in _split_copies(srcs, lands, send_sems, recv_sems, gather):
            mine.start()
        token[...] = jnp.zeros(token.shape, token.dtype)

    n_sem = (N_DEV - 1) * k
    res = pl.pallas_call(
        body, name=name,
        out_shape=(pltpu.SemaphoreType.DMA((n_sem,)), pltpu.SemaphoreType.DMA((n_sem,)),
                   *[pltpu.HBM(a.shape, a.dtype) for a in arrs],
                   *[pltpu.HBM(shp, a.dtype) for shp, a in zip(land_shapes, arrs)],
                   jax.ShapeDtypeStruct((8, 128), F32)),
        in_specs=[_HBM] * (2 * k) + [pl.BlockSpec(memory_space=pl.ANY)],
        out_specs=(_SEM, _SEM, *[_HBM] * (2 * k), pl.BlockSpec(memory_space=pltpu.VMEM)),
        input_output_aliases={i: 2 + i for i in range(2 * k)},
        compiler_params=pltpu.CompilerParams(has_side_effects=_EFFECT),
    )(*[pltpu.with_memory_space_constraint(a, pltpu.HBM) for a in arrs],
      *[pltpu.with_memory_space_constraint(lax.empty(shp, a.dtype), pltpu.HBM) for shp, a in zip(land_shapes, arrs)],
      after)
    return res[0], res[1], list(res[2:2 + k]), list(res[2 + k:2 + 2 * k]), res[-1]


def exchange_wait(name, started, after, gather):
    send_sems, recv_sems, thrus, lands, _ = started
    k = len(thrus)

    def body(*refs):
        srcs, lnds = refs[:k], refs[k:2 * k]
        s_sems, r_sems = refs[2 * k], refs[2 * k + 1]
        for mine, theirs in _split_copies(srcs, lnds, s_sems, r_sems, gather):
            mine.wait_send()
            theirs.wait_recv()

    res = pl.pallas_call(
        body, name=name,
        out_shape=tuple(pltpu.HBM(a.shape, a.dtype) for a in thrus + lands),
        in_specs=[_HBM] * (2 * k) + [_SEM, _SEM, pl.BlockSpec(memory_space=pl.ANY)], out_specs=tuple([_HBM] * (2 * k)),
        input_output_aliases={i: i for i in range(2 * k)},
        compiler_params=pltpu.CompilerParams(has_side_effects=_EFFECT),
    )(*thrus, *lands, send_sems, recv_sems, after)
    return list(res[k:])


def _pad_heads(w, real, padded):
    k = w.shape[0]
    w3 = w.reshape(k, H, real)
    return jnp.pad(w3, ((0, 0), (0, 0), (0, padded - real))).reshape(k, H * padded)


def _unpad_heads(w, real, padded):
    k = w.shape[0]
    return w.reshape(k, H, padded)[:, :, :real].reshape(k, H * real)


def _s5_place(ab_re, ab_im, bb_re_t, bb_im_t, c_re, c_im):
    eye = jnp.eye(GB, dtype=F32)

    def wb_part(bt):
        x4 = bt.reshape(P, NBLK, GB, N).transpose(1, 2, 0, 3)
        return jnp.einsum('kgpn,gh->kgphn', x4, eye).reshape(NBLK, GB * P, HALF)

    def wc_part(cc):
        x4 = cc.reshape(NBLK, GB, P, N)
        return jnp.einsum('kgpn,gh->kgnhp', x4, eye).reshape(NBLK, HALF, GB * P)

    wb = jnp.concatenate([wb_part(bb_re_t), wb_part(bb_im_t)], axis=-1)
    wc = jnp.concatenate([wc_part(c_re), -wc_part(c_im)], axis=1)
    a_tab = jnp.concatenate([ab_re.reshape(NBLK, 1, HALF), ab_im.reshape(NBLK, 1, HALF)], axis=-1)
    return wb.astype(_MXU), wc.astype(_MXU), a_tab


def _s5_unplace(dwb, dwc, da):
    eye = jnp.eye(GB, dtype=F32)

    def wb_part(dpart):
        x5 = dpart.reshape(NBLK, GB, P, GB, N)
        return jnp.einsum('kgphn,gh->kgpn', x5, eye).transpose(2, 0, 1, 3).reshape(P, G * N)

    def wc_part(dpart):
        x5 = dpart.reshape(NBLK, GB, N, GB, P)
        return jnp.einsum('kgnhp,gh->kgpn', x5, eye).reshape(G, P, N)

    dbb_re_t, dbb_im_t = wb_part(dwb[..., :HALF]), wb_part(dwb[..., HALF:])
    dc_re, dc_im = wc_part(dwc[:, :HALF]), -wc_part(dwc[:, HALF:])
    dab_re, dab_im = da[:, :HALF].reshape(1, G * N), da[:, HALF:].reshape(1, G * N)
    return dab_re, dab_im, dbb_re_t, dbb_im_t, dc_re, dc_im


def _row(v):
    return v.reshape(1, -1)


def _pack_rows(pieces):
    flat = jnp.concatenate(pieces)
    n = int(flat.shape[0])
    padded = -(-n // 65536) * 65536
    return jnp.pad(flat, (0, padded - n)).reshape(padded // 128, 128)


def kernel(x, c, positions, ada_w, ada_b, norm1_g, norm2_g, ffn_w_gate, ffn_w_up, ffn_w_down, s5_lam_re, s5_lam_im, s5_log_dt, s5_b_re, s5_b_im, s5_c_re, s5_c_im, s5_d, s5_w_glu, s5_b_glu, kv_ada_w, kv_ada_b, kv_norm_g, w_kv_a, kv_a_norm_g, w_kv_b, k_nope_norm_g, k_rope_norm_g, mla_w_dq, mla_q_norm_g, mla_w_uq, mla_q_nope_norm_g, mla_q_rope_norm_g, mla_w_o, loss_target, m_ada_w, m_ada_b, m_norm1_g, m_norm2_g, m_ffn_w_gate, m_ffn_w_up, m_ffn_w_down, m_s5_lam_re, m_s5_lam_im, m_s5_log_dt, m_s5_b_re, m_s5_b_im, m_s5_c_re, m_s5_c_im, m_s5_d, m_s5_w_glu, m_s5_b_glu, m_kv_ada_w, m_kv_ada_b, m_kv_norm_g, m_w_kv_a, m_kv_a_norm_g, m_w_kv_b, m_k_nope_norm_g, m_k_rope_norm_g, m_mla_w_dq, m_mla_q_norm_g, m_mla_w_uq, m_mla_q_nope_norm_g, m_mla_q_rope_norm_g, m_mla_w_o, v_ada_w, v_ada_b, v_norm1_g, v_norm2_g, v_ffn_w_gate, v_ffn_w_up, v_ffn_w_down, v_s5_lam_re, v_s5_lam_im, v_s5_log_dt, v_s5_b_re, v_s5_b_im, v_s5_c_re, v_s5_c_im, v_s5_d, v_s5_w_glu, v_s5_b_glu, v_kv_ada_w, v_kv_ada_b, v_kv_norm_g, v_w_kv_a, v_kv_a_norm_g, v_w_kv_b, v_k_nope_norm_g, v_k_rope_norm_g, v_mla_w_dq, v_mla_q_norm_g, v_mla_w_uq, v_mla_q_nope_norm_g, v_mla_q_rope_norm_g, v_mla_w_o):
    W = dict(ada_w=ada_w, ada_b=ada_b, norm1_g=norm1_g, norm2_g=norm2_g, ffn_w_gate=ffn_w_gate, ffn_w_up=ffn_w_up, ffn_w_down=ffn_w_down, s5_lam_re=s5_lam_re, s5_lam_im=s5_lam_im, s5_log_dt=s5_log_dt, s5_b_re=s5_b_re, s5_b_im=s5_b_im, s5_c_re=s5_c_re, s5_c_im=s5_c_im, s5_d=s5_d, s5_w_glu=s5_w_glu, s5_b_glu=s5_b_glu, kv_ada_w=kv_ada_w, kv_ada_b=kv_ada_b, kv_norm_g=kv_norm_g, w_kv_a=w_kv_a, kv_a_norm_g=kv_a_norm_g, w_kv_b=w_kv_b, k_nope_norm_g=k_nope_norm_g, k_rope_norm_g=k_rope_norm_g, mla_w_dq=mla_w_dq, mla_q_norm_g=mla_q_norm_g, mla_w_uq=mla_w_uq, mla_q_nope_norm_g=mla_q_nope_norm_g, mla_q_rope_norm_g=mla_q_rope_norm_g, mla_w_o=mla_w_o)
    M = dict(ada_w=m_ada_w, ada_b=m_ada_b, norm1_g=m_norm1_g, norm2_g=m_norm2_g, ffn_w_gate=m_ffn_w_gate, ffn_w_up=m_ffn_w_up, ffn_w_down=m_ffn_w_down, s5_lam_re=m_s5_lam_re, s5_lam_im=m_s5_lam_im, s5_log_dt=m_s5_log_dt, s5_b_re=m_s5_b_re, s5_b_im=m_s5_b_im, s5_c_re=m_s5_c_re, s5_c_im=m_s5_c_im, s5_d=m_s5_d, s5_w_glu=m_s5_w_glu, s5_b_glu=m_s5_b_glu, kv_ada_w=m_kv_ada_w, kv_ada_b=m_kv_ada_b, kv_norm_g=m_kv_norm_g, w_kv_a=m_w_kv_a, kv_a_norm_g=m_kv_a_norm_g, w_kv_b=m_w_kv_b, k_nope_norm_g=m_k_nope_norm_g, k_rope_norm_g=m_k_rope_norm_g, mla_w_dq=m_mla_w_dq, mla_q_norm_g=m_mla_q_norm_g, mla_w_uq=m_mla_w_uq, mla_q_nope_norm_g=m_mla_q_nope_norm_g, mla_q_rope_norm_g=m_mla_q_rope_norm_g, mla_w_o=m_mla_w_o)
    V = dict(ada_w=v_ada_w, ada_b=v_ada_b, norm1_g=v_norm1_g, norm2_g=v_norm2_g, ffn_w_gate=v_ffn_w_gate, ffn_w_up=v_ffn_w_up, ffn_w_down=v_ffn_w_down, s5_lam_re=v_s5_lam_re, s5_lam_im=v_s5_lam_im, s5_log_dt=v_s5_log_dt, s5_b_re=v_s5_b_re, s5_b_im=v_s5_b_im, s5_c_re=v_s5_c_re, s5_c_im=v_s5_c_im, s5_d=v_s5_d, s5_w_glu=v_s5_w_glu, s5_b_glu=v_s5_b_glu, kv_ada_w=v_kv_ada_w, kv_ada_b=v_kv_ada_b, kv_norm_g=v_kv_norm_g, w_kv_a=v_w_kv_a, kv_a_norm_g=v_kv_a_norm_g, w_kv_b=v_w_kv_b, k_nope_norm_g=v_k_nope_norm_g, k_rope_norm_g=v_k_rope_norm_g, mla_w_dq=v_mla_w_dq, mla_q_norm_g=v_mla_q_norm_g, mla_w_uq=v_mla_w_uq, mla_q_nope_norm_g=v_mla_q_nope_norm_g, mla_q_rope_norm_g=v_mla_q_rope_norm_g, mla_w_o=v_mla_w_o)
    return _step(x[0], c, positions, loss_target[0], W, M, V)


WEIGHT_NAMES = ['ada_w', 'ada_b', 'norm1_g', 'norm2_g', 'ffn_w_gate', 'ffn_w_up', 'ffn_w_down', 's5_lam_re', 's5_lam_im', 's5_log_dt', 's5_b_re', 's5_b_im', 's5_c_re', 's5_c_im', 's5_d', 's5_w_glu', 's5_b_glu', 'kv_ada_w', 'kv_ada_b', 'kv_norm_g', 'w_kv_a', 'kv_a_norm_g', 'w_kv_b', 'k_nope_norm_g', 'k_rope_norm_g', 'mla_w_dq', 'mla_q_norm_g', 'mla_w_uq', 'mla_q_nope_norm_g', 'mla_q_rope_norm_g', 'mla_w_o']
REPLICATED = ['ada_b', 'norm1_g', 'norm2_g', 's5_lam_re', 's5_lam_im', 's5_log_dt', 's5_b_re', 's5_b_im', 's5_c_re', 's5_c_im', 'kv_ada_b', 'kv_norm_g', 'kv_a_norm_g', 'k_nope_norm_g', 'k_rope_norm_g', 'mla_q_norm_g', 'mla_q_nope_norm_g', 'mla_q_rope_norm_g']
SHARDED_VEC = ['s5_d', 's5_b_glu']


def _step(x, c, positions, target, W, M, V):
    s = x.shape[0]
    me = _index(*_me())
    mxu = lambda a: a.astype(_MXU)

    pad_c = lambda a: jnp.pad(a, ((0, 0), (0, FFB - FF // N_DEV)))
    pad_r = lambda a: jnp.pad(a, ((0, FFB - FF // N_DEV), (0, 0)))
    cols = lambda g: g.transpose(1, 0, 2).reshape(g.shape[1], N_DEV * g.shape[2])
    rows = lambda g: g.reshape(N_DEV * g.shape[1], g.shape[2])

    def local_pack(l):
        second = W['s5_w_glu'][l] if l < N_A else W['mla_w_o'][l - N_A]
        arrs = [jnp.concatenate([mxu(pad_c(W['ffn_w_gate'][l])), mxu(pad_c(W['ffn_w_up'][l]))], axis=0),
                jnp.concatenate([mxu(pad_r(W['ffn_w_down'][l])), mxu(second)], axis=0)]
        if l == N_A:
            arrs += [jnp.concatenate([mxu(W['w_kv_b']), mxu(W['mla_w_dq'][0])], axis=0), mxu(W['w_kv_a'])]
        if l > N_A:
            arrs += [mxu(W['mla_w_dq'][l - N_A])]
        if l >= N_A:
            arrs += [mxu(W['mla_w_uq'][l - N_A])]
        return arrs


    def layer_weights(l, after):
        lands = exchange_wait(f"gather_wait_{l}", gathers[l], after, True)
        full = [lax.dynamic_update_slice(ld, src[None], (me,) + (0,) * src.ndim) for ld, src in zip(lands, gathers[l][2])]
        w = {'wg': cols(full[0][:, :D]), 'wu': cols(full[0][:, D:]), 'wd': rows(full[1][:, :FFB]),
             'second': rows(full[1][:, FFB:])}
        if l >= N_A:
            if l == N_A:
                wkvb3 = cols(full[2][:, :KVL]).reshape(KVL, H, DN + DV)
                wkva = rows(full[3])
                w['wa_pad'] = jnp.concatenate([wkva[:, :KVL], jnp.zeros((D, DN), _MXU), wkva[:, KVL:],
                                               jnp.zeros((D, HD - DN - DR), _MXU)], axis=1)
                w['wkn_pad'] = jnp.pad(wkvb3[:, :, :DN], ((0, 0), (0, 0), (0, HD - DN))).reshape(KVL, H * HD)
                w['wv'] = wkvb3[:, :, DN:].reshape(KVL, H * DV)
                w['wdq'] = rows(full[2][:, KVL:])
            else:
                w['wdq'] = rows(full[2])
            w['wuq_pad'] = _pad_heads(cols(full[-1]), DN + DR, HD)
        return w

    vec = jnp.concatenate([c.reshape(-1), W['s5_d'].reshape(-1), W['s5_b_glu'].reshape(-1)]).reshape(1, -1)
    vec = jnp.pad(vec, ((0, 7), (0, 0)))
    gv = all_gather("gather_vectors", vec)[:, 0, :]
    c_all = gv[:, :D]
    d_full = jnp.concatenate([gv[d, D:D + 2 * 128].reshape(N_A, 128) for d in range(N_DEV)], axis=1)
    bglu_full = jnp.concatenate([gv[d, D + 256:D + 512].reshape(N_A, 128) for d in range(N_DEV)], axis=1)

    ca_all = jax.nn.silu(c_all)
    w_mod = jnp.concatenate([W['ada_w'][l] for l in range(DEPTH)] + [W['kv_ada_w']], axis=1)
    n_mod = w_mod.shape[1]
    mod_cols = small_matmul("mod_matmul", ca_all, w_mod)
    gm = all_gather("gather_mod", mod_cols)
    gathers = [exchange_start(f"gather_start_{l}", local_pack(l), True, gm) for l in range(DEPTH)]
    tokens = sum(g[4][0, 0] for g in gathers)
    mine = lax.dynamic_index_in_dim(gm, me, axis=1, keepdims=False) + tokens
    per_l = D * 6 // N_DEV
    mods = []
    for l in range(DEPTH):
        full = jnp.concatenate([mine[d, per_l * l:per_l * (l + 1)] for d in range(N_DEV)]) + W['ada_b'][l]
        mods.append([_row(full[D * i:D * (i + 1)]) for i in range(6)])
    kfull = jnp.concatenate([mine[d, per_l * DEPTH:] for d in range(N_DEV)]) + W['kv_ada_b']
    k_shift, k_scale = _row(kfull[:D]), _row(kfull[D:])

    inv = 1.0 / (ROPE_THETA ** (np.arange(0, DR, 2, dtype=np.float32) / DR))
    inv128 = np.zeros((1, HD), np.float32)
    inv128[0, DN:DN + DR // 2] = inv
    inv128[0, DN + DR // 2:DN + DR] = inv
    cosf, sinf = rope_tables("rope_tables", positions.reshape(s, 1), jnp.asarray(inv128))
    zpad = lambda n: jnp.zeros((n,), F32)
    gkn128 = _row(jnp.concatenate([W['k_nope_norm_g'], zpad(HD - DN)]))
    gkr128 = _row(jnp.concatenate([zpad(DN), W['k_rope_norm_g'], zpad(HD - DN - DR)]))
    gq128 = [_row(jnp.concatenate([W['mla_q_nope_norm_g'][j], W['mla_q_rope_norm_g'][j], zpad(HD - DN - DR)]))
             for j in range(2)]

    expand = jnp.asarray(np.kron(np.eye(G, dtype=np.float32), np.ones((1, N), np.float32)))
    s5_raw, s5_mats = [], []
    for l in range(N_A):
        raw = (_row(W['s5_lam_re'][l]), _row(W['s5_lam_im'][l]), _row(W['s5_log_dt'][l]),
               W['s5_b_re'][l].transpose(2, 0, 1).reshape(P, G * N), W['s5_b_im'][l].transpose(2, 0, 1).reshape(P, G * N))
        ab_re, ab_im, bb_re_t, bb_im_t = s5_prep_fwd(f"s5_prep_fwd", *raw, expand)
        s5_raw.append(raw)
        s5_mats.append(_s5_place(ab_re, ab_im, bb_re_t, bb_im_t, W['s5_c_re'][l], W['s5_c_im'][l]))

    g1 = [_row(W['norm1_g'][l]) for l in range(DEPTH)]
    g2 = [_row(W['norm2_g'][l]) for l in range(DEPTH)]
    saved = []
    xs = x
    kv = None
    lw = [None] * DEPTH
    for l in range(DEPTH):
        sh1, sc1, gt1, sh2, sc2, gt2 = mods[l]
        rec = {'x_in': xs}
        if l >= N_A:
            lw[l] = layer_weights(l, xs)
        if l == N_A:
            kv_smalls = [_row(W['kv_norm_g']), k_shift, k_scale, _row(W['kv_a_norm_g']), gkn128, gkr128]
            kv_w = [lw[l]['wa_pad'], lw[l]['wkn_pad'], lw[l]['wv']]
            k_mat, v_mat = seg_forward("kv_fwd", seg_kv, [xs], kv_smalls, [cosf, sinf], kv_w,
                                       [(H * HD, _MXU), (H * DV, _MXU)], tap_widths=(KVL + HD, H * HD, H * DV))
            kv = {'x_in': xs, 'smalls': kv_smalls, 'k': k_mat, 'v': v_mat, 'w': kv_w}
        if l < N_A:
            wb, wc, a_tab = s5_mats[l]
            y, s0 = s5_scan_fwd("s5_scan_fwd", xs, g1[l], sh1, sc1, wb, wc, a_tab, _row(d_full[l]))
            lw[l] = layer_weights(l, y)
            (x_mid,) = seg_forward("glu_fwd", seg_glu, [xs, y], [gt1, _row(bglu_full[l])], [], [lw[l]['second']],
                                   [(D, F32)], tap_widths=(D,))
            rec.update(y=y, s0=s0)
        else:
            j = l - N_A
            q_smalls = [g1[l], sh1, sc1, _row(W['mla_q_norm_g'][j]), gq128[j]]
            (q_mat,) = seg_forward("q_fwd", seg_q, [xs], q_smalls, [cosf, sinf], [lw[l]['wdq'], lw[l]['wuq_pad']],
                                   [(H * HD, _MXU)], tap_widths=(QL, H * HD))
            o_mat, lse = attn_fwd("attn_fwd", q_mat, kv['k'], kv['v'])
            (x_mid,) = seg_forward("o_fwd", seg_o, [xs, o_mat], [gt1], [], [lw[l]['second']], [(D, F32)],
                                   tap_widths=(D,))
            rec.update(q=q_mat, o=o_mat, lse=lse, q_smalls=q_smalls)
        rec['x_mid'] = x_mid
        xs, rec['gate'], rec['up'] = ffn_forward("ffn_fwd", x_mid, g2[l], sh2, sc2, gt2,
                                                 lw[l]['wg'], lw[l]['wu'], lw[l]['wd'])
        saved.append(rec)

    dy, loss_part = loss_kernel("loss", xs, target)
    loss = lax.psum(loss_part[0, 0], ("x", "y", "c"))

    rblk = lambda a: a.reshape(N_DEV, a.shape[0] // N_DEV, a.shape[1])
    cblk = lambda a: a.reshape(a.shape[0], N_DEV, a.shape[1] // N_DEV).transpose(1, 0, 2)
    dmod = [None] * DEPTH
    dk_tot = []
    dv_tot = []
    dx = dy
    sends = [None] * DEPTH
    send_token = jnp.zeros((1, 1), F32)
    g_n1 = [None] * DEPTH
    g_n2 = [None] * DEPTH
    g_bglu = [None] * N_A
    g_dskip = [None] * N_A
    g_s5 = [None] * N_A
    g_qn, g_q128 = [None] * 2, [None] * 2
    for l in range(DEPTH - 1, -1, -1):
        rec = saved[l]
        sh1, sc1, gt1, sh2, sc2, gt2 = mods[l]
        dx, dgate, dup, dyd, h_b, a_b, dg2, dsh2, dsc2, dgt2 = ffn_backward(
            "ffn_bwd", rec['x_mid'], dx, rec['gate'], rec['up'], g2[l], sh2, sc2, gt2 + send_token,
            lw[l]['wg'], lw[l]['wu'], lw[l]['wd'])
        out_l = [matmul_tn("tn_ffn_in", h_b, dgate, _MXU, col_blocks=N_DEV),
                 matmul_tn("tn_ffn_in", h_b, dup, _MXU, col_blocks=N_DEV),
                 matmul_tn("tn_ffn_out", a_b, dyd, _MXU).reshape(N_DEV, FFB, D)]
        g_n2[l] = dg2
        if l == 0:
            sends_ffn0 = exchange_start("a2a_start_ffn0", out_l, False, dx)
            send_token = sends_ffn0[4][0:1, 0:1]
            out_l = []
        if l < N_A:
            (dx, dyy), (dz,), (g_b,), (dgt1, dbg) = seg_backward(
                "glu_bwd", seg_glu, [rec['x_in'], rec['y']], [gt1 + (send_token if l == 0 else 0.0), _row(bglu_full[l])], [],
                [lw[l]['second']],
                [dx], (D,), (D,))
            out_l.append(rblk(matmul_tn("tn_sq", g_b, dz, _MXU)))
            g_bglu[l] = dbg
            wb, wc, a_tab = s5_mats[l]
            dx, dwb, dwc, da, dd, dg1, dsh1, dsc1 = s5_scan_bwd(
                "s5_scan_bwd", rec['x_in'], g1[l], sh1, sc1, dyy, rec['s0'], wb, wc, a_tab, _row(d_full[l]), dx)
            g_dskip[l] = dd
            dab_re, dab_im, dbb_re_t, dbb_im_t, dc_re, dc_im = _s5_unplace(dwb, dwc, da)
            dlr, dli, dldt, dbr_t, dbi_t = s5_prep_bwd("s5_prep_bwd", *s5_raw[l], expand,
                                                       (dab_re, dab_im, dbb_re_t, dbb_im_t))
            g_s5[l] = (dlr.reshape(G, N), dli.reshape(G, N), dldt.reshape(G),
                       dbr_t.reshape(P, G, N).transpose(1, 2, 0), dbi_t.reshape(P, G, N).transpose(1, 2, 0), dc_re, dc_im)
        else:
            j = l - N_A
            (dx, do), (dzo,), (o_b,), (dgt1,) = seg_backward(
                "o_bwd", seg_o, [rec['x_in'], rec['o']], [gt1], [], [lw[l]['second']], [dx], (D,), (D,))
            out_l.append(rblk(matmul_tn("tn_sq", o_b, dzo, _MXU)))
            dq, dk, dv = attn_bwd("attn_bwd", rec['q'], kv['k'], kv['v'], rec['o'], do, rec['lse'])
            dk_tot.append(dk)
            dv_tot.append(dv)
            (dx,), (dql, dqq), (hq_b, qn_b), (dg1, dsh1, dsc1, dqg, dq128) = seg_backward(
                "q_bwd", seg_q, [rec['x_in']], rec['q_smalls'], [cosf, sinf], [lw[l]['wdq'], lw[l]['wuq_pad']],
                [dq], (QL, H * HD), (D, QL), dx_add=dx)
            g_dq = rblk(matmul_tn("tn_dq", hq_b, dql, _MXU))
            g_uq = cblk(_unpad_heads(matmul_tn("tn_uq", qn_b, dqq, _MXU), DN + DR, HD))
            g_qn[j], g_q128[j] = dqg, dq128
        g_n1[l] = dg1
        dmod[l] = jnp.concatenate([dsh1, dsc1, dgt1, dsh2, dsc2, dgt2], axis=1)
        if l == N_A:
            (dx,), (dta, dtk, dtv), (hk_b, ckv_b), (dkg, dksh, dksc, dag, dgkn, dgkr) = seg_backward(
                "kv_bwd", seg_kv, [kv['x_in']], kv['smalls'], [cosf, sinf], kv['w'],
                [dk_tot, dv_tot], (KVL + HD, H * HD, H * DV), (D, KVL), dx_add=dx)
            g_wa = matmul_tn("tn_kva", hk_b, dta, _MXU)
            g_wa = jnp.concatenate([g_wa[:, :KVL], g_wa[:, KVL + DN:KVL + DN + DR]], axis=1)
            g_kn = matmul_tn("tn_kn", ckv_b, dtk, _MXU).reshape(KVL, H, HD)[:, :, :DN]
            g_v = matmul_tn("tn_v", ckv_b, dtv, _MXU).reshape(KVL, H, DV)
            g_wkvb = jnp.concatenate([g_kn, g_v], axis=2).reshape(KVL, H * (DN + DV))
            dkmod = jnp.concatenate([dksh, dksc], axis=1)
            out_l += [jnp.concatenate([cblk(g_wkvb), g_dq], axis=1), rblk(g_wa)]
        if l > N_A:
            out_l.append(g_dq)
        if l >= N_A:
            out_l.append(g_uq)
        if l > 0:
            sends[l] = exchange_start(f"a2a_start_{l}", out_l, False, dx)
            send_token = sends[l][4][0:1, 0:1]
        if l == N_A - 1:
            early_flat = _pack_rows([a.reshape(-1) for a in g_s5[l]])
            early_st = exchange_start("small_start_s5", [early_flat], True, dx)
            send_token = send_token + early_st[4][0:1, 0:1]
    grad_x = dx

    s5_names = ['s5_lam_re', 's5_lam_im', 's5_log_dt', 's5_b_re', 's5_b_im', 's5_c_re', 's5_c_im']
    small = {
        'norm1_g': jnp.concatenate(g_n1, axis=0), 'norm2_g': jnp.concatenate(g_n2, axis=0),
        'kv_norm_g': dkg, 'kv_a_norm_g': dag, 'k_nope_norm_g': dgkn[:, :DN], 'k_rope_norm_g': dgkr[:, DN:DN + DR],
        'mla_q_norm_g': jnp.concatenate(g_qn, axis=0),
        'mla_q_nope_norm_g': jnp.concatenate([g[:, :DN] for g in g_q128], axis=0),
        'mla_q_rope_norm_g': jnp.concatenate([g[:, DN:DN + DR] for g in g_q128], axis=0),
        's5_d': jnp.concatenate(g_dskip, axis=0), 's5_b_glu': jnp.concatenate(g_bglu, axis=0),
    }
    for i, n in enumerate(s5_names):
        small[n] = jnp.stack([g_s5[l][i] for l in range(N_A - 1)])
    small_names = [n for n in REPLICATED if n not in ('ada_b', 'kv_ada_b')] + SHARDED_VEC
    flat_small = _pack_rows([small[n].reshape(-1) for n in small_names])

    dm = jnp.concatenate(dmod + [dkmod], axis=1)[0]
    per_dev = []
    for d in range(N_DEV):
        cols = [dm[6 * D * l + per_l * d:6 * D * l + per_l * (d + 1)] for l in range(DEPTH)]
        cols.append(dm[6 * D * DEPTH + (2 * D // N_DEV) * d:6 * D * DEPTH + (2 * D // N_DEV) * (d + 1)])
        per_dev.append(jnp.concatenate(cols))
    dm_dev = jnp.stack(per_dev)
    gdm = all_gather("gather_dmod", dm_dev)
    small_st = exchange_start("small_start", [flat_small], True, gdm)
    sends[0] = exchange_start("a2a_start_0", out_l, False, small_st[4])
    dm_mine = lax.dynamic_index_in_dim(gdm, me, axis=1, keepdims=False) + sends[0][4][0, 0]
    g_wmod = small_matmul_tn("dmod_matmul", ca_all, dm_mine)
    g_ada_w = jnp.stack([g_wmod[:, per_l * l:per_l * (l + 1)] for l in range(DEPTH)])
    g_kv_ada_w = g_wmod[:, per_l * DEPTH:]
    dm_sum = sum_parts("sum_dmod", gdm.reshape(N_DEV, N_DEV, n_mod))
    g_ada_b = jnp.stack([jnp.concatenate([dm_sum[d, per_l * l:per_l * (l + 1)] for d in range(N_DEV)])
                         for l in range(DEPTH)])
    g_kv_ada_b = jnp.concatenate([dm_sum[d, per_l * DEPTH:] for d in range(N_DEV)])

    grads, out_delta, out_m, out_v = {}, {}, {}, {}

    def update(name, parts, base=0, stride=0):
        shp = W[name].shape
        shp3 = shp if len(shp) == 3 else (1,) + shp
        res = adamw("adamw_" + name, parts, W[name].reshape(shp3), M[name].reshape(shp3), V[name].reshape(shp3),
                    base, stride)
        grads[name], out_delta[name], out_m[name], out_v[name] = (a.reshape(shp) for a in res)

    update('ada_w', g_ada_w.reshape(1, DEPTH * D, per_l), 0, D)
    update('kv_ada_w', g_kv_ada_w[None])

    chains = {}

    def update_layer(name, parts, layer, base=0):
        shp = W[name].shape
        shp3 = shp if len(shp) == 3 else (1,) + shp
        chains[name] = adamw_layer(f"adamw_{name}_{layer}", parts, W[name].reshape(shp3), M[name].reshape(shp3),
                                   V[name].reshape(shp3), layer, chains.get(name), base)
        grads[name], out_delta[name], out_m[name], out_v[name] = (a.reshape(shp) for a in chains[name])

    ffn_parts = [[None] * DEPTH for _ in range(3)]

    def landed(name, started, after):
        lands = exchange_wait(name, started, after, False)
        return [lax.dynamic_update_slice(ld, lax.dynamic_index_in_dim(src, me, 0, keepdims=True), (me,) + (0,) * (src.ndim - 1))
                for ld, src in zip(lands, started[2])]

    def receive(l, after):
        recv = landed(f"a2a_wait_{l}", sends[l], after)
        if l == 0:
            recv = landed("a2a_wait_ffn0", sends_ffn0, after) + recv
        for i in range(3):
            ffn_parts[i][l] = recv[i]
        if l < N_A:
            update_layer('s5_w_glu', recv[3], l)
        else:
            update_layer('mla_w_o', recv[3], l - N_A)
            if l == N_A:
                update_layer('w_kv_b', recv[4], 0)
                update_layer('mla_w_dq', recv[4], 0, KVL)
                update_layer('w_kv_a', recv[5], 0)
            else:
                update_layer('mla_w_dq', recv[4], l - N_A)
            update_layer('mla_w_uq', recv[-1], l - N_A)

    for l in range(DEPTH - 1, 0, -1):
        receive(l, out_delta['kv_ada_w'])

    def gathered_sum(name, started, own, after):
        (land,) = exchange_wait(name + "_wait", started, after, True)
        return sum_parts("sum_" + name, lax.dynamic_update_slice(land, own[None], (me, 0, 0))).reshape(-1)

    early_sum = gathered_sum("small_s5", early_st, early_flat, chains['s5_w_glu'][1])
    g_small_sum = gathered_sum("small", small_st, flat_small, early_sum)
    off = 0
    for n in small_names:
        size = int(np.prod(small[n].shape))
        full = g_small_sum[off:off + size]
        off += size
        if n in SHARDED_VEC:
            full = lax.dynamic_slice_in_dim(full.reshape(N_A, D), me * (D // N_DEV), D // N_DEV, axis=1)
        grads[n] = full.reshape(small[n].shape if n in s5_names else W[n].shape)
    off = 0
    for i, n in enumerate(s5_names):
        size = int(np.prod(g_s5[N_A - 1][i].shape))
        last = early_sum[off:off + size].reshape((1,) + g_s5[N_A - 1][i].shape)
        off += size
        grads[n] = jnp.concatenate([grads[n], last], axis=0)
    grads['ada_b'] = g_ada_b
    grads['kv_ada_b'] = g_kv_ada_b

    big_small = ('s5_b_re', 's5_b_im', 's5_c_re', 's5_c_im')
    packed_names = [n for n in REPLICATED + SHARDED_VEC if n not in big_small]

    def pack(dct):
        flat_ = jnp.concatenate([dct[n].reshape(-1) for n in packed_names])
        n_ = int(flat_.shape[0])
        p_ = -(-n_ // 8192) * 8192
        return jnp.pad(flat_, (0, p_ - n_)).reshape(p_ // 128, 128)

    _, d_p, m_p, v_p = adamw("adamw_small", pack(grads)[None], pack(W)[None], pack(M)[None], pack(V)[None])
    off = 0
    d_p, m_p, v_p = d_p.reshape(-1), m_p.reshape(-1), v_p.reshape(-1)
    for n in packed_names:
        size = int(np.prod(W[n].shape))
        out_delta[n] = d_p[off:off + size].reshape(W[n].shape)
        out_m[n] = m_p[off:off + size].reshape(W[n].shape)
        out_v[n] = v_p[off:off + size].reshape(W[n].shape)
        off += size
    for n in big_small:
        shp = W[n].shape
        view = (1, int(np.prod(shp[:-1])), shp[-1])
        res = adamw("adamw_" + n, grads[n].reshape(view), W[n].reshape(view), M[n].reshape(view), V[n].reshape(view))
        _, out_delta[n], out_m[n], out_v[n] = (a.reshape(shp) for a in res)

    receive(0, d_p)
    for i, name in enumerate(('ffn_w_gate', 'ffn_w_up', 'ffn_w_down')):
        res = adamw_multi("adamw_" + name, ffn_parts[i], W[name], M[name], V[name])
        grads[name], out_delta[name], out_m[name], out_v[name] = res

    return (loss, grad_x[None], *[grads[n] for n in WEIGHT_NAMES], *[out_delta[n] for n in WEIGHT_NAMES],
            *[out_m[n] for n in WEIGHT_NAMES], *[out_v[n] for n in WEIGHT_NAMES])
```
